```python
import math
import jax, jax.numpy as jnp
from jax import lax
import numpy as np

D_MODEL = 1024
BATCH = 8
SEQ = 8192
DEPTH = 2

N_A_LAYERS = DEPTH // 2
N_B_LAYERS = DEPTH - N_A_LAYERS
HG_HEADS = 8
HG_DIM = D_MODEL // HG_HEADS
HG_CHUNK = 32
SW_Q_HEADS = 16
SW_KV_HEADS = 4
SW_HEAD_DIM = D_MODEL // SW_Q_HEADS
SW_GROUP = SW_Q_HEADS // SW_KV_HEADS
SW_WINDOW = 128
REL_BUCKETS = 32
REL_MAX_DIST = 128
FFN_DIM = 2816
CONV_WIDTH = 3
ALPHA = (2.0 * DEPTH) ** 0.25
BETA = (8.0 * DEPTH) ** -0.25
LN_EPS = 1e-5
RMS_EPS = 1e-6

kernel_name = "yoco_hgrn2_swa_sink_convffn"


def layer_norm(x, g, b):
    xf = x.astype(jnp.float32)
    mu = xf.mean(-1, keepdims=True)
    var = jnp.square(xf - mu).mean(-1, keepdims=True)
    y = (xf - mu) * lax.rsqrt(var + LN_EPS) * g.astype(jnp.float32) + b.astype(jnp.float32)
    return y.astype(x.dtype)


def hgrn2_chunkwise(q, k, v, log_f):
    B, S, H, Dk = q.shape
    Dv = v.shape[-1]
    n = S // HG_CHUNK

    def chunks(a):
        return a.reshape(B, n, HG_CHUNK, H, a.shape[-1]).transpose(1, 0, 3, 2, 4)

    qc, kc, vc = chunks(q), chunks(k), chunks(v)
    bc = jnp.cumsum(chunks(log_f), axis=3)
    causal = jnp.tril(jnp.ones((HG_CHUNK, HG_CHUNK), dtype=bool))[:, :, None]

    def step(state, inp):
        q_, k_, v_, b_ = inp
        o_inter = jnp.einsum('bhtk,bhkv->bhtv', q_ * jnp.exp(b_), state)
        diff = b_[:, :, :, None, :] - b_[:, :, None, :, :]
        decay = jnp.where(causal, jnp.exp(jnp.minimum(diff, 0.0)), 0.0)
        scores = jnp.einsum('bhtsk,bhsk->bhts', q_[:, :, :, None, :] * decay, k_)
        o = o_inter + jnp.einsum('bhts,bhsv->bhtv', scores, v_)
        b_last = b_[:, :, -1, :]
        k_dec = k_ * jnp.exp(b_last[:, :, None, :] - b_)
        state = jnp.exp(b_last)[..., None] * state + jnp.einsum('bhsk,bhsv->bhkv', k_dec, v_)
        return state, o

    state0 = jnp.zeros((B, H, Dk, Dv), jnp.float32)
    _, o = lax.scan(step, state0, (qc, kc, vc, bc))
    return o.transpose(1, 0, 3, 2, 4).reshape(B, S, H, Dv)


def hgrn2_mixer(h, w_in, lower_bound, g_norm_w, w_out):
    B, S, D = h.shape
    q, f, i, g = jnp.split(h @ w_in, 4, axis=-1)

    def heads(a):
        return a.reshape(B, S, HG_HEADS, HG_DIM).astype(jnp.float32)

    lb = lower_bound.astype(jnp.float32).reshape(HG_HEADS, HG_DIM)
    fg = lb + (1.0 - lb) * jax.nn.sigmoid(heads(f))
    o = hgrn2_chunkwise(jax.nn.silu(heads(q)), 1.0 - fg, heads(i), jnp.log(fg))
    o = o * lax.rsqrt(jnp.mean(jnp.square(o), -1, keepdims=True) + RMS_EPS)
    o = o * g_norm_w.astype(jnp.float32) * jax.nn.silu(heads(g))
    return o.reshape(B, S, D).astype(h.dtype) @ w_out


def t5_causal_bucket(dist):
    exact = REL_BUCKETS // 2
    d = jnp.maximum(dist, 1).astype(jnp.float32)
    log_b = exact + (jnp.log(d / exact) / math.log(REL_MAX_DIST / exact)
                     * (REL_BUCKETS - exact)).astype(jnp.int32)
    return jnp.where(dist < exact, dist, jnp.minimum(log_b, REL_BUCKETS - 1))


def banded_bias_and_mask(rel_table, n_blocks):
    t = jnp.arange(SW_WINDOW)[:, None] + SW_WINDOW
    s = jnp.arange(2 * SW_WINDOW)[None, :]
    dist = t - s
    bias = rel_table[t5_causal_bucket(jnp.maximum(dist, 0))].transpose(2, 0, 1)
    band = (dist >= 0) & (dist < SW_WINDOW)
    has_prev = (jnp.arange(n_blocks) > 0)[:, None, None]
    mask = band[None] & (has_prev | (s >= SW_WINDOW)[None])
    return bias, mask


def swa_sink_mixer(h, k, v, w_q, sinks, bias, mask, w_out):
    B, S, D = h.shape
    nb = S // SW_WINDOW
    q = (h @ w_q).reshape(B, nb, SW_WINDOW, SW_KV_HEADS, SW_GROUP, SW_HEAD_DIM)

    def with_prev(a):
        ab = a.reshape(B, nb, SW_WINDOW, SW_KV_HEADS, SW_HEAD_DIM)
        prev = jnp.pad(ab, ((0, 0), (1, 0), (0, 0), (0, 0), (0, 0)))[:, :-1]
        return jnp.concatenate([prev, ab], axis=2)

    kk, vv = with_prev(k), with_prev(v)
    scale = SW_HEAD_DIM ** -0.5
    logits = jnp.einsum('bntgrd,bnsgd->bngrts', q, kk).astype(jnp.float32) * scale
    logits = logits + bias.astype(jnp.float32).reshape(SW_KV_HEADS, SW_GROUP, SW_WINDOW, 2 * SW_WINDOW)
    logits = jnp.where(mask[None, :, None, None], logits, -jnp.inf)
    sink = sinks.astype(jnp.float32).reshape(1, 1, SW_KV_HEADS, SW_GROUP, 1)
    m = jnp.maximum(logits.max(-1), sink)
    p = jnp.exp(logits - m[..., None])
    denom = p.sum(-1) + jnp.exp(sink - m)
    o = jnp.einsum('bngrts,bnsgd->bntgrd', p, vv.astype(jnp.float32))
    o = o / jnp.moveaxis(denom, -1, 2)[..., None]
    return o.reshape(B, S, D).astype(h.dtype) @ w_out


def conv_ffn(h, w_in, conv_w, conv_b, w_out):
    u = h @ w_in
    C = u.shape[-1]
    u = lax.conv_general_dilated(u, conv_w[:, None, :], window_strides=(1,),
                                 padding=[(CONV_WIDTH - 1, 0)],
                                 dimension_numbers=('NWC', 'WIO', 'NWC'),
                                 feature_group_count=C) + conv_b
    a, b = jnp.split(u, 2, axis=-1)
    return (jax.nn.silu(a) * b) @ w_out


def _fwd_setup_inputs(seed: int = 0) -> dict:
    key = jax.random.key(seed)
    ks = jax.random.split(key, 24)
    D, F = D_MODEL, FFN_DIM
    kv_dim = SW_KV_HEADS * SW_HEAD_DIM
    nrm = jax.random.normal

    x = nrm(ks[0], (BATCH, SEQ, D), jnp.float32)

    hgrn_w_in = nrm(ks[1], (N_A_LAYERS, D, 4 * D), jnp.float32) * D ** -0.5
    hgrn_w_in = hgrn_w_in.at[..., 2 * D:3 * D].multiply(BETA)
    hgrn_lb_logits = nrm(ks[2], (N_A_LAYERS + 1, D), jnp.float32) * 0.5
    hgrn_gnorm_w = 1.0 + 0.02 * nrm(ks[3], (N_A_LAYERS, HG_DIM), jnp.float32)
    hgrn_w_out = nrm(ks[4], (N_A_LAYERS, D, D), jnp.float32) * D ** -0.5 * BETA

    swa_w_q = nrm(ks[5], (N_B_LAYERS, D, D), jnp.float32) * D ** -0.5
    swa_sinks = nrm(ks[6], (N_B_LAYERS, SW_Q_HEADS), jnp.float32) * 0.5
    swa_w_out = nrm(ks[7], (N_B_LAYERS, D, D), jnp.float32) * D ** -0.5 * BETA
    shared_w_kv = nrm(ks[8], (D, 2 * kv_dim), jnp.float32) * D ** -0.5
    shared_w_kv = shared_w_kv.at[:, kv_dim:].multiply(BETA)
    rel_bias = nrm(ks[9], (REL_BUCKETS, SW_Q_HEADS), jnp.float32) * 0.5

    ffn_w_in = nrm(ks[10], (DEPTH, D, 2 * F), jnp.float32) * D ** -0.5 * BETA
    ffn_conv_w = nrm(ks[11], (DEPTH, CONV_WIDTH, 2 * F), jnp.float32) * CONV_WIDTH ** -0.5
    ffn_conv_b = nrm(ks[12], (DEPTH, 2 * F), jnp.float32) * 0.02
    ffn_w_out = nrm(ks[13], (DEPTH, F, D), jnp.float32) * F ** -0.5 * BETA

    ln_mix_g = 1.0 + 0.02 * nrm(ks[14], (DEPTH, D), jnp.float32)
    ln_mix_b = 0.02 * nrm(ks[15], (DEPTH, D), jnp.float32)
    ln_ffn_g = 1.0 + 0.02 * nrm(ks[16], (DEPTH, D), jnp.float32)
    ln_ffn_b = 0.02 * nrm(ks[17], (DEPTH, D), jnp.float32)

    return {"x": x, "hgrn_w_in": hgrn_w_in, "hgrn_lb_logits": hgrn_lb_logits,
            "hgrn_gnorm_w": hgrn_gnorm_w, "hgrn_w_out": hgrn_w_out,
            "swa_w_q": swa_w_q, "swa_sinks": swa_sinks, "swa_w_out": swa_w_out,
            "shared_w_kv": shared_w_kv, "rel_bias": rel_bias,
            "ffn_w_in": ffn_w_in, "ffn_conv_w": ffn_conv_w, "ffn_conv_b": ffn_conv_b,
            "ffn_w_out": ffn_w_out, "ln_mix_g": ln_mix_g, "ln_mix_b": ln_mix_b,
            "ln_ffn_g": ln_ffn_g, "ln_ffn_b": ln_ffn_b}


def _fwd_reference(x, hgrn_w_in, hgrn_lb_logits, hgrn_gnorm_w, hgrn_w_out,
              swa_w_q, swa_sinks, swa_w_out, shared_w_kv, rel_bias,
              ffn_w_in, ffn_conv_w, ffn_conv_b, ffn_w_out,
              ln_mix_g, ln_mix_b, ln_ffn_g, ln_ffn_b):
    B, S, D = x.shape
    n_blocks = S // SW_WINDOW
    lower_bounds = jnp.cumsum(jax.nn.softmax(hgrn_lb_logits.astype(jnp.float32), axis=0), axis=0)
    bias, mask = banded_bias_and_mask(rel_bias, n_blocks)

    h = x
    k_shared = v_shared = None
    for layer in range(DEPTH):
        if layer < N_A_LAYERS:
            mix = hgrn2_mixer(h, hgrn_w_in[layer], lower_bounds[layer],
                              hgrn_gnorm_w[layer], hgrn_w_out[layer])
        else:
            j = layer - N_A_LAYERS
            mix = swa_sink_mixer(h, k_shared, v_shared, swa_w_q[j], swa_sinks[j],
                                 bias, mask, swa_w_out[j])
        h = layer_norm(ALPHA * h + mix, ln_mix_g[layer], ln_mix_b[layer])
        ff = conv_ffn(h, ffn_w_in[layer], ffn_conv_w[layer], ffn_conv_b[layer], ffn_w_out[layer])
        h = layer_norm(ALPHA * h + ff, ln_ffn_g[layer], ln_ffn_b[layer])
        if layer == N_A_LAYERS - 1:
            k_flat, v_flat = jnp.split(h @ shared_w_kv, 2, axis=-1)
            k_shared = k_flat.reshape(B, S, SW_KV_HEADS, SW_HEAD_DIM)
            v_shared = v_flat.reshape(B, S, SW_KV_HEADS, SW_HEAD_DIM)
    return h


import jax as _jax
import jax.numpy as _jnp

TWIN_FORMAT = 'train_step'
FWD_PARAMS = ['x', 'hgrn_w_in', 'hgrn_lb_logits', 'hgrn_gnorm_w', 'hgrn_w_out', 'swa_w_q', 'swa_sinks', 'swa_w_out', 'shared_w_kv', 'rel_bias', 'ffn_w_in', 'ffn_conv_w', 'ffn_conv_b', 'ffn_w_out', 'ln_mix_g', 'ln_mix_b', 'ln_ffn_g', 'ln_ffn_b']
TWIN_WEIGHTS = ['hgrn_w_in', 'hgrn_lb_logits', 'hgrn_gnorm_w', 'hgrn_w_out', 'swa_w_q', 'swa_sinks', 'swa_w_out', 'shared_w_kv', 'rel_bias', 'ffn_w_in', 'ffn_conv_w', 'ffn_conv_b', 'ffn_w_out', 'ln_mix_g', 'ln_mix_b', 'ln_ffn_g', 'ln_ffn_b']
TWIN_DIFF_INPUT = 'x'
TWIN_INPUTS = ['x', 'hgrn_w_in', 'hgrn_lb_logits', 'hgrn_gnorm_w', 'hgrn_w_out', 'swa_w_q', 'swa_sinks', 'swa_w_out', 'shared_w_kv', 'rel_bias', 'ffn_w_in', 'ffn_conv_w', 'ffn_conv_b', 'ffn_w_out', 'ln_mix_g', 'ln_mix_b', 'ln_ffn_g', 'ln_ffn_b', 'loss_target', 'm_hgrn_w_in', 'm_hgrn_lb_logits', 'm_hgrn_gnorm_w', 'm_hgrn_w_out', 'm_swa_w_q', 'm_swa_sinks', 'm_swa_w_out', 'm_shared_w_kv', 'm_rel_bias', 'm_ffn_w_in', 'm_ffn_conv_w', 'm_ffn_conv_b', 'm_ffn_w_out', 'm_ln_mix_g', 'm_ln_mix_b', 'm_ln_ffn_g', 'm_ln_ffn_b', 'v_hgrn_w_in', 'v_hgrn_lb_logits', 'v_hgrn_gnorm_w', 'v_hgrn_w_out', 'v_swa_w_q', 'v_swa_sinks', 'v_swa_w_out', 'v_shared_w_kv', 'v_rel_bias', 'v_ffn_w_in', 'v_ffn_conv_w', 'v_ffn_conv_b', 'v_ffn_w_out', 'v_ln_mix_g', 'v_ln_mix_b', 'v_ln_ffn_g', 'v_ln_ffn_b']
TWIN_OUTPUTS = ['loss', 'grad_x', 'grad_hgrn_w_in', 'grad_hgrn_lb_logits', 'grad_hgrn_gnorm_w', 'grad_hgrn_w_out', 'grad_swa_w_q', 'grad_swa_sinks', 'grad_swa_w_out', 'grad_shared_w_kv', 'grad_rel_bias', 'grad_ffn_w_in', 'grad_ffn_conv_w', 'grad_ffn_conv_b', 'grad_ffn_w_out', 'grad_ln_mix_g', 'grad_ln_mix_b', 'grad_ln_ffn_g', 'grad_ln_ffn_b', 'delta_hgrn_w_in', 'delta_hgrn_lb_logits', 'delta_hgrn_gnorm_w', 'delta_hgrn_w_out', 'delta_swa_w_q', 'delta_swa_sinks', 'delta_swa_w_out', 'delta_shared_w_kv', 'delta_rel_bias', 'delta_ffn_w_in', 'delta_ffn_conv_w', 'delta_ffn_conv_b', 'delta_ffn_w_out', 'delta_ln_mix_g', 'delta_ln_mix_b', 'delta_ln_ffn_g', 'delta_ln_ffn_b', 'new_m_hgrn_w_in', 'new_m_hgrn_lb_logits', 'new_m_hgrn_gnorm_w', 'new_m_hgrn_w_out', 'new_m_swa_w_q', 'new_m_swa_sinks', 'new_m_swa_w_out', 'new_m_shared_w_kv', 'new_m_rel_bias', 'new_m_ffn_w_in', 'new_m_ffn_conv_w', 'new_m_ffn_conv_b', 'new_m_ffn_w_out', 'new_m_ln_mix_g', 'new_m_ln_mix_b', 'new_m_ln_ffn_g', 'new_m_ln_ffn_b', 'new_v_hgrn_w_in', 'new_v_hgrn_lb_logits', 'new_v_hgrn_gnorm_w', 'new_v_hgrn_w_out', 'new_v_swa_w_q', 'new_v_swa_sinks', 'new_v_swa_w_out', 'new_v_shared_w_kv', 'new_v_rel_bias', 'new_v_ffn_w_in', 'new_v_ffn_conv_w', 'new_v_ffn_conv_b', 'new_v_ffn_w_out', 'new_v_ln_mix_g', 'new_v_ln_mix_b', 'new_v_ln_ffn_g', 'new_v_ln_ffn_b']
TWIN_LEAF_KINDS = {'loss': 'loss', 'grad_x': 'grad_x', 'grad_hgrn_w_in': 'grad_w', 'grad_hgrn_lb_logits': 'grad_w', 'grad_hgrn_gnorm_w': 'grad_w', 'grad_hgrn_w_out': 'grad_w', 'grad_swa_w_q': 'grad_w', 'grad_swa_sinks': 'grad_w', 'grad_swa_w_out': 'grad_w', 'grad_shared_w_kv': 'grad_w', 'grad_rel_bias': 'grad_w', 'grad_ffn_w_in': 'grad_w', 'grad_ffn_conv_w': 'grad_w', 'grad_ffn_conv_b': 'grad_w', 'grad_ffn_w_out': 'grad_w', 'grad_ln_mix_g': 'grad_w', 'grad_ln_mix_b': 'grad_w', 'grad_ln_ffn_g': 'grad_w', 'grad_ln_ffn_b': 'grad_w', 'delta_hgrn_w_in': 'delta_w', 'delta_hgrn_lb_logits': 'delta_w', 'delta_hgrn_gnorm_w': 'delta_w', 'delta_hgrn_w_out': 'delta_w', 'delta_swa_w_q': 'delta_w', 'delta_swa_sinks': 'delta_w', 'delta_swa_w_out': 'delta_w', 'delta_shared_w_kv': 'delta_w', 'delta_rel_bias': 'delta_w', 'delta_ffn_w_in': 'delta_w', 'delta_ffn_conv_w': 'delta_w', 'delta_ffn_conv_b': 'delta_w', 'delta_ffn_w_out': 'delta_w', 'delta_ln_mix_g': 'delta_w', 'delta_ln_mix_b': 'delta_w', 'delta_ln_ffn_g': 'delta_w', 'delta_ln_ffn_b': 'delta_w', 'new_m_hgrn_w_in': 'new_m', 'new_m_hgrn_lb_logits': 'new_m', 'new_m_hgrn_gnorm_w': 'new_m', 'new_m_hgrn_w_out': 'new_m', 'new_m_swa_w_q': 'new_m', 'new_m_swa_sinks': 'new_m', 'new_m_swa_w_out': 'new_m', 'new_m_shared_w_kv': 'new_m', 'new_m_rel_bias': 'new_m', 'new_m_ffn_w_in': 'new_m', 'new_m_ffn_conv_w': 'new_m', 'new_m_ffn_conv_b': 'new_m', 'new_m_ffn_w_out': 'new_m', 'new_m_ln_mix_g': 'new_m', 'new_m_ln_mix_b': 'new_m', 'new_m_ln_ffn_g': 'new_m', 'new_m_ln_ffn_b': 'new_m', 'new_v_hgrn_w_in': 'new_v', 'new_v_hgrn_lb_logits': 'new_v', 'new_v_hgrn_gnorm_w': 'new_v', 'new_v_hgrn_w_out': 'new_v', 'new_v_swa_w_q': 'new_v', 'new_v_swa_sinks': 'new_v', 'new_v_swa_w_out': 'new_v', 'new_v_shared_w_kv': 'new_v', 'new_v_rel_bias': 'new_v', 'new_v_ffn_w_in': 'new_v', 'new_v_ffn_conv_w': 'new_v', 'new_v_ffn_conv_b': 'new_v', 'new_v_ffn_w_out': 'new_v', 'new_v_ln_mix_g': 'new_v', 'new_v_ln_mix_b': 'new_v', 'new_v_ln_ffn_g': 'new_v', 'new_v_ln_ffn_b': 'new_v'}


def _forward(args):
    return _fwd_reference(*[args[k] for k in FWD_PARAMS])


def _output_shape():
    def fwd():
        inp = _fwd_setup_inputs(0)
        return _fwd_reference(*[inp[k] for k in FWD_PARAMS])
    out = _jax.eval_shape(fwd)
    return out.shape, out.dtype

N_MICROBATCH = 1
ADAM_LR = 0.001
ADAM_B1 = 0.9
ADAM_B2 = 0.999
ADAM_EPS = 1e-08
ADAM_WD = 0.01
ADAM_STEP = 10
PER_EXAMPLE_BATCH_AXIS = {'x': 0, 'loss_target': 0}
SHARED_INPUTS = []
_WEIGHT_DTYPES = {'hgrn_w_in': _jnp.float32, 'hgrn_lb_logits': _jnp.float32, 'hgrn_gnorm_w': _jnp.float32, 'hgrn_w_out': _jnp.float32, 'swa_w_q': _jnp.float32, 'swa_sinks': _jnp.float32, 'swa_w_out': _jnp.float32, 'shared_w_kv': _jnp.float32, 'rel_bias': _jnp.float32, 'ffn_w_in': _jnp.float32, 'ffn_conv_w': _jnp.float32, 'ffn_conv_b': _jnp.float32, 'ffn_w_out': _jnp.float32, 'ln_mix_g': _jnp.float32, 'ln_mix_b': _jnp.float32, 'ln_ffn_g': _jnp.float32, 'ln_ffn_b': _jnp.float32}
MOMENT_SCALE = {'hgrn_w_in': 5.862923e-02, 'hgrn_lb_logits': 4.460218e-03, 'hgrn_gnorm_w': 1.569251e-01, 'hgrn_w_out': 1.036381e-01, 'swa_w_q': 8.331079e-03, 'swa_sinks': 6.768253e-03, 'swa_w_out': 1.981512e-02, 'shared_w_kv': 3.057857e-02, 'rel_bias': 1.082330e-02, 'ffn_w_in': 1.515028e-02, 'ffn_conv_w': 7.606680e-03, 'ffn_conv_b': 1.700027e-02, 'ffn_w_out': 2.495606e-02, 'ln_mix_g': 2.246286e+00, 'ln_mix_b': 1.039676e+00, 'ln_ffn_g': 4.535608e+01, 'ln_ffn_b': 1.744940e+00}


def _to_microbatches(a, axis):
    t = _jnp.moveaxis(a, axis, 0)
    t = t.reshape((N_MICROBATCH, t.shape[0] // N_MICROBATCH) + t.shape[1:])
    return _jnp.moveaxis(t, 1, axis + 1)


def setup_inputs(seed: int = 0) -> dict:
    inp = _fwd_setup_inputs(seed)
    key = _jax.random.fold_in(_jax.random.key(seed), 7919)
    shape, _ = _output_shape()
    out = dict(inp)
    out["loss_target"] = _jax.random.normal(_jax.random.fold_in(key, 0), shape, _jnp.float32)
    for i, name in enumerate(TWIN_WEIGHTS):
        w = inp[name].astype(_jnp.float32)
        if MOMENT_SCALE is None:
            s = _jnp.sqrt(_jnp.mean(_jnp.square(w)) + 1e-30)
        else:
            s = MOMENT_SCALE[name]
        km, kv = _jax.random.split(_jax.random.fold_in(key, i + 1))
        out[name] = w
        out["m_" + name] = s * _jax.random.normal(km, w.shape, _jnp.float32)
        out["v_" + name] = (s * s) * _jax.random.uniform(kv, w.shape, _jnp.float32, 0.5, 1.5)
    if N_MICROBATCH > 1:
        for name, axis in PER_EXAMPLE_BATCH_AXIS.items():
            out[name] = _to_microbatches(out[name], axis)
    return {'x': out['x'], 'hgrn_w_in': out['hgrn_w_in'], 'hgrn_lb_logits': out['hgrn_lb_logits'], 'hgrn_gnorm_w': out['hgrn_gnorm_w'], 'hgrn_w_out': out['hgrn_w_out'], 'swa_w_q': out['swa_w_q'], 'swa_sinks': out['swa_sinks'], 'swa_w_out': out['swa_w_out'], 'shared_w_kv': out['shared_w_kv'], 'rel_bias': out['rel_bias'], 'ffn_w_in': out['ffn_w_in'], 'ffn_conv_w': out['ffn_conv_w'], 'ffn_conv_b': out['ffn_conv_b'], 'ffn_w_out': out['ffn_w_out'], 'ln_mix_g': out['ln_mix_g'], 'ln_mix_b': out['ln_mix_b'], 'ln_ffn_g': out['ln_ffn_g'], 'ln_ffn_b': out['ln_ffn_b'], 'loss_target': out['loss_target'], 'm_hgrn_w_in': out['m_hgrn_w_in'], 'm_hgrn_lb_logits': out['m_hgrn_lb_logits'], 'm_hgrn_gnorm_w': out['m_hgrn_gnorm_w'], 'm_hgrn_w_out': out['m_hgrn_w_out'], 'm_swa_w_q': out['m_swa_w_q'], 'm_swa_sinks': out['m_swa_sinks'], 'm_swa_w_out': out['m_swa_w_out'], 'm_shared_w_kv': out['m_shared_w_kv'], 'm_rel_bias': out['m_rel_bias'], 'm_ffn_w_in': out['m_ffn_w_in'], 'm_ffn_conv_w': out['m_ffn_conv_w'], 'm_ffn_conv_b': out['m_ffn_conv_b'], 'm_ffn_w_out': out['m_ffn_w_out'], 'm_ln_mix_g': out['m_ln_mix_g'], 'm_ln_mix_b': out['m_ln_mix_b'], 'm_ln_ffn_g': out['m_ln_ffn_g'], 'm_ln_ffn_b': out['m_ln_ffn_b'], 'v_hgrn_w_in': out['v_hgrn_w_in'], 'v_hgrn_lb_logits': out['v_hgrn_lb_logits'], 'v_hgrn_gnorm_w': out['v_hgrn_gnorm_w'], 'v_hgrn_w_out': out['v_hgrn_w_out'], 'v_swa_w_q': out['v_swa_w_q'], 'v_swa_sinks': out['v_swa_sinks'], 'v_swa_w_out': out['v_swa_w_out'], 'v_shared_w_kv': out['v_shared_w_kv'], 'v_rel_bias': out['v_rel_bias'], 'v_ffn_w_in': out['v_ffn_w_in'], 'v_ffn_conv_w': out['v_ffn_conv_w'], 'v_ffn_conv_b': out['v_ffn_conv_b'], 'v_ffn_w_out': out['v_ffn_w_out'], 'v_ln_mix_g': out['v_ln_mix_g'], 'v_ln_mix_b': out['v_ln_mix_b'], 'v_ln_ffn_g': out['v_ln_ffn_g'], 'v_ln_ffn_b': out['v_ln_ffn_b']}


def _loss(weights, diff, rest, loss_target):
    with _jax.named_scope("forward"):
        args = {**rest, TWIN_DIFF_INPUT: diff, **{k: w.astype(_WEIGHT_DTYPES[k]) for k, w in weights.items()}}
        y = _forward(args)
    with _jax.named_scope("loss_head"):
        err = _jnp.square(y.astype(_jnp.float32) - loss_target)
        return 0.5 * _jnp.sum(_jnp.mean(err, axis=-1)) if err.ndim else 0.5 * err


def _adamw(w, g, m, v):
    m = ADAM_B1 * m + (1.0 - ADAM_B1) * g
    v = ADAM_B2 * v + (1.0 - ADAM_B2) * _jnp.square(g)
    m_hat = m / (1.0 - ADAM_B1 ** ADAM_STEP)
    v_hat = v / (1.0 - ADAM_B2 ** ADAM_STEP)
    delta = -ADAM_LR * (m_hat / (_jnp.sqrt(v_hat) + ADAM_EPS) + ADAM_WD * w)
    return delta, m, v


def reference(x, hgrn_w_in, hgrn_lb_logits, hgrn_gnorm_w, hgrn_w_out, swa_w_q, swa_sinks, swa_w_out, shared_w_kv, rel_bias, ffn_w_in, ffn_conv_w, ffn_conv_b, ffn_w_out, ln_mix_g, ln_mix_b, ln_ffn_g, ln_ffn_b, loss_target, m_hgrn_w_in, m_hgrn_lb_logits, m_hgrn_gnorm_w, m_hgrn_w_out, m_swa_w_q, m_swa_sinks, m_swa_w_out, m_shared_w_kv, m_rel_bias, m_ffn_w_in, m_ffn_conv_w, m_ffn_conv_b, m_ffn_w_out, m_ln_mix_g, m_ln_mix_b, m_ln_ffn_g, m_ln_ffn_b, v_hgrn_w_in, v_hgrn_lb_logits, v_hgrn_gnorm_w, v_hgrn_w_out, v_swa_w_q, v_swa_sinks, v_swa_w_out, v_shared_w_kv, v_rel_bias, v_ffn_w_in, v_ffn_conv_w, v_ffn_conv_b, v_ffn_w_out, v_ln_mix_g, v_ln_mix_b, v_ln_ffn_g, v_ln_ffn_b):
    given = dict(x=x, hgrn_w_in=hgrn_w_in, hgrn_lb_logits=hgrn_lb_logits, hgrn_gnorm_w=hgrn_gnorm_w, hgrn_w_out=hgrn_w_out, swa_w_q=swa_w_q, swa_sinks=swa_sinks, swa_w_out=swa_w_out, shared_w_kv=shared_w_kv, rel_bias=rel_bias, ffn_w_in=ffn_w_in, ffn_conv_w=ffn_conv_w, ffn_conv_b=ffn_conv_b, ffn_w_out=ffn_w_out, ln_mix_g=ln_mix_g, ln_mix_b=ln_mix_b, ln_ffn_g=ln_ffn_g, ln_ffn_b=ln_ffn_b, loss_target=loss_target, m_hgrn_w_in=m_hgrn_w_in, m_hgrn_lb_logits=m_hgrn_lb_logits, m_hgrn_gnorm_w=m_hgrn_gnorm_w, m_hgrn_w_out=m_hgrn_w_out, m_swa_w_q=m_swa_w_q, m_swa_sinks=m_swa_sinks, m_swa_w_out=m_swa_w_out, m_shared_w_kv=m_shared_w_kv, m_rel_bias=m_rel_bias, m_ffn_w_in=m_ffn_w_in, m_ffn_conv_w=m_ffn_conv_w, m_ffn_conv_b=m_ffn_conv_b, m_ffn_w_out=m_ffn_w_out, m_ln_mix_g=m_ln_mix_g, m_ln_mix_b=m_ln_mix_b, m_ln_ffn_g=m_ln_ffn_g, m_ln_ffn_b=m_ln_ffn_b, v_hgrn_w_in=v_hgrn_w_in, v_hgrn_lb_logits=v_hgrn_lb_logits, v_hgrn_gnorm_w=v_hgrn_gnorm_w, v_hgrn_w_out=v_hgrn_w_out, v_swa_w_q=v_swa_w_q, v_swa_sinks=v_swa_sinks, v_swa_w_out=v_swa_w_out, v_shared_w_kv=v_shared_w_kv, v_rel_bias=v_rel_bias, v_ffn_w_in=v_ffn_w_in, v_ffn_conv_w=v_ffn_conv_w, v_ffn_conv_b=v_ffn_conv_b, v_ffn_w_out=v_ffn_w_out, v_ln_mix_g=v_ln_mix_g, v_ln_mix_b=v_ln_mix_b, v_ln_ffn_g=v_ln_ffn_g, v_ln_ffn_b=v_ln_ffn_b)
    weights = {n: given[n] for n in TWIN_WEIGHTS}
    shared = {n: given[n] for n in SHARED_INPUTS}
    per_example = {n: given[n] for n in ['x']}
    grad_fn = _jax.value_and_grad(_loss, argnums=(0, 1))

    def one_microbatch(ex, loss_target):
        ex = dict(ex)
        diff = ex.pop(TWIN_DIFF_INPUT)
        return grad_fn(weights, diff, {**shared, **ex}, loss_target)

    if N_MICROBATCH == 1:
        loss, (grad_w, grad_x) = one_microbatch(per_example, given["loss_target"])
    else:
        def body(carry, xs):
            loss_sum, grad_sum = carry
            l_k, (gw_k, gx_k) = one_microbatch(xs[0], xs[1])
            with _jax.named_scope("update"):
                return (loss_sum + l_k, _jax.tree.map(_jnp.add, grad_sum, gw_k)), gx_k

        init = (_jnp.zeros((), _jnp.float32), _jax.tree.map(_jnp.zeros_like, weights))
        (loss, grad_w), grad_x = _jax.lax.scan(body, init, (per_example, given["loss_target"]))
    with _jax.named_scope("update"):
        delta_w, new_m, new_v = {}, {}, {}
        for n in TWIN_WEIGHTS:
            delta_w[n], new_m[n], new_v[n] = _adamw(weights[n], grad_w[n], given["m_" + n], given["v_" + n])
    return (loss, grad_x, *[grad_w[n] for n in TWIN_WEIGHTS], *[delta_w[n] for n in TWIN_WEIGHTS],
            *[new_m[n] for n in TWIN_WEIGHTS], *[new_v[n] for n in TWIN_WEIGHTS])
```

```python
import functools
import math

import numpy as np
import jax
import jax.numpy as jnp
from jax import lax
from jax.experimental import pallas as pl
from jax.experimental.pallas import tpu as pltpu

F32 = jnp.float32
MXU = jnp.bfloat16

N_DEV = 8
D_MODEL = 1024
DEPTH = 2
HG_HEADS = 8
HG_DIM = 128
HG_CHUNK = 64
SW_Q_HEADS = 16
SW_KV_HEADS = 4
SW_GROUP = 4
SW_HEAD_DIM = 64
SW_WINDOW = 128
REL_BUCKETS = 32
REL_MAX_DIST = 128
FFN_DIM = 2816
ALPHA = (2.0 * DEPTH) ** 0.25
LN_EPS = 1e-5
RMS_EPS = 1e-6
ADAM_LR = 0.001
ADAM_B1 = 0.9
ADAM_B2 = 0.999
ADAM_EPS = 1e-08
ADAM_WD = 0.01
ADAM_STEP = 10
EXP_CLAMP = 80.0
NEG_BIG = -1e30

SUBLANES = 8
LANES = 128
VMEM_LIMIT = 48 * 2 ** 20
TOKEN_TILE = 512


def _params(**kw):
    return pltpu.CompilerParams(vmem_limit_bytes=VMEM_LIMIT, **kw)


def _sigmoid(x):
    return 1.0 / (1.0 + jnp.exp(-x))


def _dot(a, b):
    return jnp.dot(a.astype(MXU), b.astype(MXU), preferred_element_type=F32)


def _dot_nt(a, b):
    return lax.dot_general(a.astype(MXU), b.astype(MXU), (((1,), (1,)), ((), ())), preferred_element_type=F32)


def _dot_tn(a, b):
    return lax.dot_general(a.astype(MXU), b.astype(MXU), (((0,), (0,)), ((), ())), preferred_element_type=F32)


def _trunc_bf16(x):
    bits = lax.bitcast_convert_type(x, jnp.int32)
    return lax.bitcast_convert_type(bits & jnp.int32(-65536), F32)


def _split3(x):
    hi = _trunc_bf16(x)
    r = x - hi
    mid = _trunc_bf16(r)
    lo = r - mid
    return hi.astype(jnp.bfloat16), mid.astype(jnp.bfloat16), lo.astype(jnp.bfloat16)


def _dot_hp(a, b, contract):
    def halves(x):
        hi = _trunc_bf16(x)
        return hi.astype(jnp.bfloat16), (x - hi).astype(jnp.bfloat16)

    ah, al = halves(a)
    bh, bl = halves(b)
    d = lambda p, q: lax.dot_general(p, q, (contract, ((), ())), preferred_element_type=F32)
    return d(ah, bh) + d(ah, bl) + d(al, bh)


def _exact_dot(m01, x):
    hi, mid, lo = _split3(x)
    d = lambda p: jnp.dot(m01, p, preferred_element_type=F32)
    return d(hi) + d(mid) + d(lo)


def _exact_dot_r(x, m01):
    hi, mid, lo = _split3(x)
    d = lambda p: jnp.dot(p, m01, preferred_element_type=F32)
    return d(hi) + d(mid) + d(lo)


def _mm_nn(a, w, *, name, res=None, res_scale=1.0, ln=None, out_dtype=F32, tm=None):
    nbk, T, kw = a.shape
    _, nbn, _, nw = w.shape
    tm = min(tm or TOKEN_TILE, T)
    has_res = res is not None
    assert ln is None or nbn == 1

    def body(*refs):
        refs = list(refs)
        a_ref, w_ref = refs[:2]
        pos = 2
        res_ref = None
        if has_res:
            res_ref = refs[pos]
            pos += 1
        if ln is not None:
            g_ref, b_ref = refs[pos:pos + 2]
            pos += 2
        o_ref = refs[pos]
        pos += 1
        if ln is not None:
            xh_ref, rs_ref = refs[pos:pos + 2]
            pos += 2
        acc_ref = refs[pos] if nbk > 1 else None
        k = pl.program_id(2)
        part = _dot(a_ref[...], w_ref[...])

        def finish(acc):
            y = acc
            if has_res:
                y = y + res_scale * res_ref[...].astype(F32)
            if ln is None:
                o_ref[...] = y.astype(o_ref.dtype)
            else:
                mu = jnp.mean(y, axis=-1, keepdims=True)
                yc = y - mu
                var = jnp.mean(yc * yc, axis=-1, keepdims=True)
                rstd = lax.rsqrt(var + LN_EPS)
                xh = yc * rstd
                xh_ref[...] = xh
                rs_ref[...] = rstd
                o_ref[...] = (xh * g_ref[...] + b_ref[...]).astype(o_ref.dtype)

        if nbk == 1:
            finish(part)
        else:
            @pl.when(k == 0)
            def _():
                acc_ref[...] = part

            @pl.when(k > 0)
            def _():
                acc_ref[...] += part

            @pl.when(k == nbk - 1)
            def _():
                finish(acc_ref[...])

    in_specs = [pl.BlockSpec((None, tm, kw), lambda i, n, k: (k, i, 0)),
                pl.BlockSpec((None, None, kw, nw), lambda i, n, k: (k, n, 0, 0))]
    args = [a, w]
    if has_res:
        in_specs.append(pl.BlockSpec((None, tm, nw), lambda i, n, k: (n, i, 0)))
        args.append(res)
    if ln is not None:
        in_specs += [pl.BlockSpec((1, nw), lambda i, n, k: (0, 0))] * 2
        args += list(ln)
    out_spec = pl.BlockSpec((None, tm, nw), lambda i, n, k: (n, i, 0))
    out_shape = jax.ShapeDtypeStruct((nbn, T, nw), out_dtype)
    if ln is not None:
        out_specs = [out_spec, out_spec, pl.BlockSpec((tm, 1), lambda i, n, k: (i, 0))]
        out_shape = [out_shape, jax.ShapeDtypeStruct((nbn, T, nw), F32), jax.ShapeDtypeStruct((T, 1), F32)]
    else:
        out_specs = out_spec
    scratch = [pltpu.VMEM((tm, nw), F32)] if nbk > 1 else []
    return pl.pallas_call(body, name=name, grid=(T // tm, nbn, nbk), in_specs=in_specs, out_specs=out_specs,
                          out_shape=out_shape, scratch_shapes=scratch, compiler_params=_params())(*args)


def _mm_nt(dy, w, *, name, res=None, res_scale=1.0, out_dtype=F32, tm=None):
    nbn, T, nw = dy.shape
    nbk, _, kw, _ = w.shape
    tm = min(tm or TOKEN_TILE, T)
    has_res = res is not None

    def body(*refs):
        refs = list(refs)
        dy_ref, w_ref = refs[:2]
        pos = 2
        res_ref = None
        if has_res:
            res_ref = refs[pos]
            pos += 1
        o_ref = refs[pos]
        pos += 1
        acc_ref = refs[pos] if nbn > 1 else None
        n = pl.program_id(2)
        part = _dot_nt(dy_ref[...], w_ref[...])

        def finish(acc):
            y = acc
            if has_res:
                y = y + res_scale * res_ref[...].astype(F32)
            o_ref[...] = y.astype(o_ref.dtype)

        if nbn == 1:
            finish(part)
        else:
            @pl.when(n == 0)
            def _():
                acc_ref[...] = part

            @pl.when(n > 0)
            def _():
                acc_ref[...] += part

            @pl.when(n == nbn - 1)
            def _():
                finish(acc_ref[...])

    in_specs = [pl.BlockSpec((None, tm, nw), lambda i, k, n: (n, i, 0)),
                pl.BlockSpec((None, None, kw, nw), lambda i, k, n: (k, n, 0, 0))]
    args = [dy, w]
    if has_res:
        in_specs.append(pl.BlockSpec((None, tm, kw), lambda i, k, n: (k, i, 0)))
        args.append(res)
    scratch = [pltpu.VMEM((tm, kw), F32)] if nbn > 1 else []
    return pl.pallas_call(body, name=name, grid=(T // tm, nbk, nbn), in_specs=in_specs,
                          out_specs=pl.BlockSpec((None, tm, kw), lambda i, k, n: (k, i, 0)),
                          out_shape=jax.ShapeDtypeStruct((nbk, T, kw), out_dtype), scratch_shapes=scratch,
                          compiler_params=_params())(*args)


def _mm_tn(a, dy, *, name, tm=None):
    nbk, T, kw = a.shape
    nbn, _, nw = dy.shape
    tm = min(tm or TOKEN_TILE, T)

    def body(a_ref, dy_ref, o_ref):
        i = pl.program_id(2)
        part = _dot_tn(a_ref[...], dy_ref[...])

        @pl.when(i == 0)
        def _():
            o_ref[...] = part

        @pl.when(i > 0)
        def _():
            o_ref[...] += part

    return pl.pallas_call(body, name=name, grid=(nbk, nbn, T // tm),
                          in_specs=[pl.BlockSpec((None, tm, kw), lambda k, n, i: (k, i, 0)),
                                    pl.BlockSpec((None, tm, nw), lambda k, n, i: (n, i, 0))],
                          out_specs=pl.BlockSpec((None, None, kw, nw), lambda k, n, i: (k, n, 0, 0)),
                          out_shape=jax.ShapeDtypeStruct((nbk, nbn, kw, nw), F32),
                          compiler_params=_params())(a, dy)


def _ln_bwd(dh, xhat, rstd, g, *, name, tm=None):
    _, T, D = dh.shape
    tm = min(tm or TOKEN_TILE, T)

    def body(dh_ref, xh_ref, rs_ref, g_ref, dy_ref, dg_ref, db_ref):
        i = pl.program_id(0)
        dhv = dh_ref[...]
        xh = xh_ref[...]
        dxh = dhv * g_ref[...]
        m1 = jnp.mean(dxh, axis=-1, keepdims=True)
        m2 = jnp.mean(dxh * xh, axis=-1, keepdims=True)
        dy_ref[...] = rs_ref[...] * (dxh - m1 - xh * m2)
        dg = jnp.sum(dhv * xh, axis=0, keepdims=True)
        db = jnp.sum(dhv, axis=0, keepdims=True)

        @pl.when(i == 0)
        def _():
            dg_ref[...] = dg
            db_ref[...] = db

        @pl.when(i > 0)
        def _():
            dg_ref[...] += dg
            db_ref[...] += db

    tok = pl.BlockSpec((None, tm, D), lambda i: (0, i, 0))
    vec = pl.BlockSpec((1, D), lambda i: (0, 0))
    return pl.pallas_call(body, name=name, grid=(T // tm,),
                          in_specs=[tok, tok, pl.BlockSpec((tm, 1), lambda i: (i, 0)), vec],
                          out_specs=[tok, vec, vec],
                          out_shape=[jax.ShapeDtypeStruct((1, T, D), F32), jax.ShapeDtypeStruct((1, D), F32),
                                     jax.ShapeDtypeStruct((1, D), F32)],
                          compiler_params=_params())(dh, xhat, rstd, g)


def _loss_head(h, tgt, *, name, tm=None):
    _, T, D = h.shape
    tm = min(tm or TOKEN_TILE, T)

    def body(h_ref, t_ref, dh_ref, loss_ref):
        i = pl.program_id(0)
        err = h_ref[...] - t_ref[...]
        dh_ref[...] = err / D
        part = 0.5 * jnp.sum(jnp.mean(err * err, axis=-1, keepdims=True), axis=0, keepdims=True)

        @pl.when(i == 0)
        def _():
            loss_ref[...] = jnp.zeros_like(loss_ref) + part

        @pl.when(i > 0)
        def _():
            loss_ref[...] += part

    tok = pl.BlockSpec((None, tm, D), lambda i: (0, i, 0))
    return pl.pallas_call(body, name=name, grid=(T // tm,), in_specs=[tok, tok],
                          out_specs=[tok, pl.BlockSpec((SUBLANES, LANES), lambda i: (0, 0))],
                          out_shape=[jax.ShapeDtypeStruct((1, T, D), F32),
                                     jax.ShapeDtypeStruct((SUBLANES, LANES), F32)],
                          compiler_params=_params())(h, tgt)


def _shift_rows(ext, k, n):
    if k == 0:
        return ext[SUBLANES:SUBLANES + n]
    return pltpu.roll(ext, k, axis=0)[SUBLANES:SUBLANES + n]


def _conv_rows(ext, cw_ref, n):
    return (cw_ref[0:1, :] * _shift_rows(ext, 2, n) + cw_ref[1:2, :] * _shift_rows(ext, 1, n)
            + cw_ref[2:3, :] * ext[SUBLANES:SUBLANES + n] + cw_ref[3:4, :])


def _convgate_fwd(u, cw, *, name, tm=None):
    nb, T, fb = u.shape
    half = nb // 2
    tm = min(tm or TOKEN_TILE, T)
    r8 = tm // SUBLANES

    def body(ua_ref, uap_ref, ub_ref, ubp_ref, cwa_ref, cwb_ref, o_ref):
        first = pl.program_id(0) == 0

        def conv(cur_ref, prev_ref, cw_ref):
            prev = jnp.where(first, 0.0, prev_ref[...])
            ext = jnp.concatenate([prev, cur_ref[...]], axis=0)
            return _conv_rows(ext, cw_ref, tm)

        a = conv(ua_ref, uap_ref, cwa_ref)
        b = conv(ub_ref, ubp_ref, cwb_ref)
        o_ref[...] = (a * _sigmoid(a) * b).astype(o_ref.dtype)

    cur = lambda off: pl.BlockSpec((None, tm, fb), lambda i, p: (p + off, i, 0))
    prev = lambda off: pl.BlockSpec((None, SUBLANES, fb), lambda i, p: (p + off, jnp.maximum(i * r8 - 1, 0), 0))
    cws = lambda off: pl.BlockSpec((None, SUBLANES, fb), lambda i, p: (p + off, 0, 0))
    return pl.pallas_call(body, name=name, grid=(T // tm, half),
                          in_specs=[cur(0), prev(0), cur(half), prev(half), cws(0), cws(half)],
                          out_specs=pl.BlockSpec((None, tm, fb), lambda i, p: (p, i, 0)),
                          out_shape=jax.ShapeDtypeStruct((half, T, fb), MXU),
                          compiler_params=_params())(u, u, u, u, cw, cw)


def _convgate_bwd(u, dact, cw, *, name, tm=None):
    nb, T, fb = u.shape
    half = nb // 2
    tm = min(tm or TOKEN_TILE, T)
    r8 = tm // SUBLANES
    n_tiles = T // tm
    last8 = T // SUBLANES - 1

    def body(ua_ref, uap_ref, uan_ref, ub_ref, ubp_ref, ubn_ref, da_ref, dan_ref, cwa_ref, cwb_ref, cwo_ref,
             du_ref, dcw_ref):
        n = pl.program_id(0)
        i = pl.program_id(1)
        first = i == 0
        last = i == n_tiles - 1
        m = tm + SUBLANES

        def ext_of(cur_ref, prev_ref, nxt_ref):
            prev = jnp.where(first, 0.0, prev_ref[...])
            return jnp.concatenate([prev, cur_ref[...], nxt_ref[...]], axis=0)

        ext_a = ext_of(ua_ref, uap_ref, uan_ref)
        ext_b = ext_of(ub_ref, ubp_ref, ubn_ref)
        a = _conv_rows(ext_a, cwa_ref, m)
        b = _conv_rows(ext_b, cwb_ref, m)
        sa = _sigmoid(a)
        dact_m = jnp.concatenate([da_ref[...], jnp.where(last, 0.0, dan_ref[...])], axis=0).astype(F32)
        is_a = n < half
        dc = jnp.where(is_a, dact_m * b * (sa * (1.0 + a * (1.0 - sa))), dact_m * (a * sa))
        dc1 = pltpu.roll(dc, m - 1, axis=0)[:tm]
        dc2 = pltpu.roll(dc, m - 2, axis=0)[:tm]
        dc0 = dc[:tm]
        du_ref[...] = cwo_ref[2:3, :] * dc0 + cwo_ref[1:2, :] * dc1 + cwo_ref[0:1, :] * dc2
        ext_o = jnp.where(is_a, ext_a, ext_b)
        g0 = jnp.sum(dc0 * _shift_rows(ext_o, 2, tm), axis=0, keepdims=True)
        g1 = jnp.sum(dc0 * _shift_rows(ext_o, 1, tm), axis=0, keepdims=True)
        g2 = jnp.sum(dc0 * ext_o[SUBLANES:SUBLANES + tm], axis=0, keepdims=True)
        g3 = jnp.sum(dc0, axis=0, keepdims=True)
        rows = lax.broadcasted_iota(jnp.int32, (SUBLANES, fb), 0)
        part = jnp.where(rows == 0, g0, jnp.where(rows == 1, g1, jnp.where(rows == 2, g2,
                                                                       jnp.where(rows == 3, g3, 0.0))))

        @pl.when(first)
        def _():
            dcw_ref[...] = part

        @pl.when(i > 0)
        def _():
            dcw_ref[...] += part

    pa = lambda n: n % half
    pb = lambda n: n % half + half
    cur = lambda f: pl.BlockSpec((None, tm, fb), lambda n, i: (f(n), i, 0))
    prev = lambda f: pl.BlockSpec((None, SUBLANES, fb), lambda n, i: (f(n), jnp.maximum(i * r8 - 1, 0), 0))
    nxt = lambda f: pl.BlockSpec((None, SUBLANES, fb), lambda n, i: (f(n), jnp.minimum((i + 1) * r8, last8), 0))
    cws = lambda f: pl.BlockSpec((None, SUBLANES, fb), lambda n, i: (f(n), 0, 0))
    own = lambda n: n
    return pl.pallas_call(body, name=name, grid=(nb, n_tiles),
                          in_specs=[cur(pa), prev(pa), nxt(pa), cur(pb), prev(pb), nxt(pb), cur(pa), nxt(pa),
                                    cws(pa), cws(pb), cws(own)],
                          out_specs=[pl.BlockSpec((None, tm, fb), lambda n, i: (n, i, 0)),
                                     pl.BlockSpec((None, SUBLANES, fb), lambda n, i: (n, 0, 0))],
                          out_shape=[jax.ShapeDtypeStruct((nb, T, fb), F32),
                                     jax.ShapeDtypeStruct((nb, SUBLANES, fb), F32)],
                          compiler_params=_params())(u, u, u, u, u, u, dact, dact, cw, cw, cw)


def _tri(n, lower):
    r = lax.broadcasted_iota(jnp.int32, (n, n), 0)
    c = lax.broadcasted_iota(jnp.int32, (n, n), 1)
    return (r >= c) if lower else (r <= c)


def _hgrn_gates(zq, zf, lb):
    sq = _sigmoid(zq)
    sf = _sigmoid(zf)
    fg = lb + (1.0 - lb) * sf
    return zq * sq, sq, sf, fg, jnp.log(fg)


def _lb_of(lbl_ref, cols):
    return _sigmoid(lbl_ref[0:1, cols] - lbl_ref[1:2, cols])


def _ones_where(mask):
    return jnp.where(mask, 1.0, 0.0).astype(jnp.bfloat16)


def _hgrn_fwd(z, lbl, gw, *, name):
    _, T, zw = z.shape
    C = min(HG_CHUNK, T)
    nch = T // C
    hpb = zw // HG_DIM

    def body(z_ref, lbl_ref, gw_ref, og_ref, st_ref, s_scr, bc_scr):
        c = pl.program_id(0)

        @pl.when(c == 0)
        def _():
            s_scr[...] = jnp.zeros_like(s_scr)

        low = _tri(C, True)
        low01 = _ones_where(low)
        gwv = gw_ref[...]
        for blk in range(2):
            zq = z_ref[blk]
            zf = z_ref[2 + blk]
            qq, _, _, fg, lf = _hgrn_gates(zq, zf, _lb_of(lbl_ref, slice(blk * zw, (blk + 1) * zw)))
            kk = 1.0 - fg
            bc_scr[...] = _exact_dot(low01, lf)
            for hh in range(hpb):
                h = blk * hpb + hh
                cols = slice(hh * HG_DIM, (hh + 1) * HG_DIM)
                b = bc_scr[:, cols]
                q_h, k_h = qq[:, cols], kk[:, cols]
                v_h = z_ref[4 + blk, :, cols]
                g_h = z_ref[6 + blk, :, cols]
                bm = bc_scr[C // 2 - 1:C // 2, cols]
                bl = bc_scr[C - 1:C, cols]
                qt = q_h * jnp.exp(jnp.minimum(b - bm, EXP_CLAMP))
                kt = k_h * jnp.exp(jnp.minimum(bm - b, EXP_CLAMP))
                A = jnp.where(low, _dot_nt(qt, kt), 0.0)
                s0 = s_scr[h]
                st_ref[h] = s0
                o = _dot_nt(q_h * jnp.exp(b), s0) + _dot(A, v_h)
                s_scr[h] = s0 * jnp.exp(bl) + _dot_tn(v_h, k_h * jnp.exp(bl - b))
                r = lax.rsqrt(jnp.mean(o * o, axis=-1, keepdims=True) + RMS_EPS)
                og = o * r * gwv * (g_h * _sigmoid(g_h))
                og_ref[:, h * HG_DIM:(h + 1) * HG_DIM] = og.astype(og_ref.dtype)

    return pl.pallas_call(body, name=name, grid=(nch,),
                          in_specs=[pl.BlockSpec((8, C, zw), lambda c: (0, c, 0)),
                                    pl.BlockSpec((2, D_MODEL), lambda c: (0, 0)),
                                    pl.BlockSpec((1, HG_DIM), lambda c: (0, 0))],
                          out_specs=[pl.BlockSpec((None, C, D_MODEL), lambda c: (0, c, 0)),
                                     pl.BlockSpec((None, HG_HEADS, HG_DIM, HG_DIM), lambda c: (c, 0, 0, 0))],
                          out_shape=[jax.ShapeDtypeStruct((1, T, D_MODEL), MXU),
                                     jax.ShapeDtypeStruct((nch, HG_HEADS, HG_DIM, HG_DIM), F32)],
                          scratch_shapes=[pltpu.VMEM((HG_HEADS, HG_DIM, HG_DIM), F32), pltpu.VMEM((C, zw), F32)],
                          compiler_params=_params())(z, lbl, gw)


def _hgrn_bwd(z, dog, states, lbl, gw, *, name):
    _, T, zw = z.shape
    C = min(HG_CHUNK, T)
    nch = T // C
    hpb = zw // HG_DIM

    def body(z_ref, dog_ref, st0_ref, st1_ref, lbl_ref, gw_ref, dz_ref, dlb_ref, dgw_ref, d_scr, bc_scr):
        step = pl.program_id(0)

        @pl.when(step == 0)
        def _():
            d_scr[...] = jnp.zeros_like(d_scr)
            dlb_ref[...] = jnp.zeros_like(dlb_ref)
            dgw_ref[...] = jnp.zeros_like(dgw_ref)

        low = _tri(C, True)
        low01 = _ones_where(low)
        up01 = _ones_where(_tri(C, False))
        gwv = gw_ref[...]
        dgw_acc = jnp.zeros((1, HG_DIM), F32)
        for blk in range(2):
            zq = z_ref[blk]
            zf = z_ref[2 + blk]
            lbb = _lb_of(lbl_ref, slice(blk * zw, (blk + 1) * zw))
            qq, sq, sf, fg, lf = _hgrn_gates(zq, zf, lbb)
            kk = 1.0 - fg
            bc_scr[...] = _exact_dot(low01, lf)
            dlb_blk = []
            for hh in range(hpb):
                h = blk * hpb + hh
                cols = slice(hh * HG_DIM, (hh + 1) * HG_DIM)
                b = bc_scr[:, cols]
                q_h, k_h = qq[:, cols], kk[:, cols]
                v_h = z_ref[4 + blk, :, cols]
                g_h = z_ref[6 + blk, :, cols]
                bm = bc_scr[C // 2 - 1:C // 2, cols]
                bl = bc_scr[C - 1:C, cols]
                eq = jnp.exp(jnp.minimum(b - bm, EXP_CLAMP))
                ek = jnp.exp(jnp.minimum(bm - b, EXP_CLAMP))
                eb = jnp.exp(b)
                el = jnp.exp(bl - b)
                ebl = jnp.exp(bl)
                qt, kt, q0, kd = q_h * eq, k_h * ek, q_h * eb, k_h * el
                A = jnp.where(low, _dot_nt(qt, kt), 0.0)
                s0 = st0_ref[h]
                s1 = st1_ref[h]
                dt = d_scr[h]
                o = _dot_nt(q0, s0) + _dot(A, v_h)
                r = lax.rsqrt(jnp.mean(o * o, axis=-1, keepdims=True) + RMS_EPS)
                on = o * r
                sg = _sigmoid(g_h)
                dogh = dog_ref[:, h * HG_DIM:(h + 1) * HG_DIM].astype(F32)
                t1 = dogh * on
                dgw_acc = dgw_acc + jnp.sum(t1 * (g_h * sg), axis=0, keepdims=True)
                dg = t1 * gwv * (sg * (1.0 + g_h * (1.0 - sg)))
                don = dogh * gwv * (g_h * sg)
                do = r * (don - on * jnp.mean(don * on, axis=-1, keepdims=True))
                P = jnp.where(low, _dot_nt(do, v_h), 0.0)
                dqq = eb * _dot(do, s0) + eq * _dot_hp(P, kt, ((1,), (0,)))
                dkk = el * _dot(v_h, dt) + ek * _dot_hp(P, qt, ((0,), (0,)))
                dv = _dot_nt(kd, dt) + _dot_tn(A, do)
                d_scr[h] = dt * ebl + _dot_tn(do, q0)
                edge = jnp.sum(dt * s1, axis=0, keepdims=True)
                dlf = _exact_dot(up01, q_h * dqq - k_h * dkk) + edge
                fg_h = fg[:, cols]
                sf_h = sf[:, cols]
                sq_h = sq[:, cols]
                zq_h = zq[:, cols]
                dfg = dlf / fg_h - dkk
                lb_h = lbb[:, cols]
                dlb_blk.append(jnp.sum(dfg * (1.0 - sf_h), axis=0, keepdims=True))
                dz_ref[blk, :, cols] = dqq * (sq_h * (1.0 + zq_h * (1.0 - sq_h)))
                dz_ref[2 + blk, :, cols] = dfg * (1.0 - lb_h) * sf_h * (1.0 - sf_h)
                dz_ref[4 + blk, :, cols] = dv
                dz_ref[6 + blk, :, cols] = dg
            dlb_ref[:, blk * zw:(blk + 1) * zw] += jnp.concatenate(dlb_blk, axis=1)
        dgw_ref[...] += dgw_acc

    rev = lambda s: nch - 1 - s
    return pl.pallas_call(body, name=name, grid=(nch,),
                          in_specs=[pl.BlockSpec((8, C, zw), lambda s: (0, rev(s), 0)),
                                    pl.BlockSpec((None, C, D_MODEL), lambda s: (0, rev(s), 0)),
                                    pl.BlockSpec((None, HG_HEADS, HG_DIM, HG_DIM), lambda s: (rev(s), 0, 0, 0)),
                                    pl.BlockSpec((None, HG_HEADS, HG_DIM, HG_DIM),
                                                 lambda s: (jnp.minimum(rev(s) + 1, nch - 1), 0, 0, 0)),
                                    pl.BlockSpec((2, D_MODEL), lambda s: (0, 0)),
                                    pl.BlockSpec((1, HG_DIM), lambda s: (0, 0))],
                          out_specs=[pl.BlockSpec((8, C, zw), lambda s: (0, rev(s), 0)),
                                     pl.BlockSpec((1, D_MODEL), lambda s: (0, 0)),
                                     pl.BlockSpec((1, HG_DIM), lambda s: (0, 0))],
                          out_shape=[jax.ShapeDtypeStruct((8, T, zw), F32), jax.ShapeDtypeStruct((1, D_MODEL), F32),
                                     jax.ShapeDtypeStruct((1, HG_DIM), F32)],
                          scratch_shapes=[pltpu.VMEM((HG_HEADS, HG_DIM, HG_DIM), F32), pltpu.VMEM((C, zw), F32)],
                          compiler_params=_params())(z, dog, states, states, lbl, gw)


def _bucket_onehot():
    W = SW_WINDOW
    t = np.arange(W)[:, None] + W
    s = np.arange(2 * W)[None, :]
    dist = t - s
    exact = REL_BUCKETS // 2
    d = np.maximum(np.maximum(dist, 0), 1).astype(np.float32)
    log_b = exact + (np.log(d / np.float32(exact)) / np.float32(math.log(REL_MAX_DIST / exact))
                     * np.float32(REL_BUCKETS - exact)).astype(np.int32)
    bucket = np.where(np.maximum(dist, 0) < exact, np.maximum(dist, 0), np.minimum(log_b, REL_BUCKETS - 1))
    valid = (dist >= 0) & (dist < W)
    onehot = (bucket[..., None] == np.arange(REL_BUCKETS)) & valid[..., None]
    return onehot.reshape(W * 2 * W, REL_BUCKETS).astype(np.float32)


def _bias_expand(rel_t, onehot_t, *, name):
    hq, nbk = rel_t.shape
    n = onehot_t.shape[1]

    def body(r_ref, oh_ref, o_ref):
        o_ref[...] = _exact_dot_r(r_ref[...], oh_ref[...])

    return pl.pallas_call(body, name=name, out_shape=jax.ShapeDtypeStruct((hq, n), F32),
                          compiler_params=_params())(rel_t, onehot_t)


def _bias_reduce(dbias, onehot, *, name):
    hq = dbias.shape[0]
    nbk = onehot.shape[1]

    def body(d_ref, oh_ref, o_ref):
        o_ref[...] = _exact_dot_r(d_ref[...], oh_ref[...])

    return pl.pallas_call(body, name=name, out_shape=jax.ShapeDtypeStruct((hq, nbk), F32),
                          compiler_params=_params())(dbias, onehot)


def _swa_mask(j):
    W = SW_WINDOW
    t = lax.broadcasted_iota(jnp.int32, (W, 2 * W), 0) + W
    s = lax.broadcasted_iota(jnp.int32, (W, 2 * W), 1)
    dist = t - s
    band = (dist >= 0) & (dist < W)
    m = band & ((j > 0) | (s >= W))
    return jnp.concatenate([m] * SW_GROUP, axis=0)


def _half_mask(rows, half):
    lane = lax.broadcasted_iota(jnp.int32, (rows, LANES), 1)
    return (lane >= SW_HEAD_DIM) if half else (lane < SW_HEAD_DIM)


def _swa_head(ref, col0, head, to_half):
    slab, half = head // 2, head % 2
    x = ref[:, col0 + slab * LANES:col0 + (slab + 1) * LANES]
    x = jnp.where(_half_mask(x.shape[0], half), x, 0.0)
    return x if half == to_half else pltpu.roll(x, SW_HEAD_DIM, axis=1)


def _swa_stack(ref, g):
    return jnp.concatenate([_swa_head(ref, 0, g * SW_GROUP + r, g % 2) for r in range(SW_GROUP)], axis=0)


def _swa_unstack(ref, x, g):
    W = SW_WINDOW
    for pair in range(SW_GROUP // 2):
        parts = []
        for r in (2 * pair, 2 * pair + 1):
            piece = x[r * W:(r + 1) * W]
            parts.append(piece if r % 2 == g % 2 else pltpu.roll(piece, SW_HEAD_DIM, axis=1))
        slab = (g * SW_GROUP) // 2 + pair
        ref[:, slab * LANES:(slab + 1) * LANES] = parts[0] + parts[1]


def _swa_kv(kp_ref, kc_ref, col0, g):
    return jnp.concatenate([_swa_head(kp_ref, col0, g, g % 2), _swa_head(kc_ref, col0, g, g % 2)], axis=0)


def _lane_pick(tile, h):
    lane = lax.broadcasted_iota(jnp.int32, tile.shape, 1)
    return jnp.sum(jnp.where(lane == h, tile, 0.0), axis=-1, keepdims=True)


def _lane_put(tile, h, col):
    lane = lax.broadcasted_iota(jnp.int32, tile.shape, 1)
    return jnp.where(lane == h, col, tile)


def _swa_rows(vals):
    return jnp.concatenate([jnp.broadcast_to(v, (SW_WINDOW, 1)) for v in vals], axis=0)


def _swa_fwd(q, kv, bias, sinks, *, name):
    _, T, D = q.shape
    W = SW_WINDOW
    nb = T // W
    dh = SW_HEAD_DIM
    kvw = SW_KV_HEADS * dh
    scale = dh ** -0.5

    def body(q_ref, kc_ref, kp_ref, bias_ref, sink_ref, o_ref, lse_ref):
        j = pl.program_id(0)
        mask = _swa_mask(j)
        sk = sink_ref[...]
        lse_tile = jnp.zeros((W, SW_Q_HEADS), F32)
        for g in range(SW_KV_HEADS):
            kk = _swa_kv(kp_ref, kc_ref, 0, g)
            vv = _swa_kv(kp_ref, kc_ref, kvw, g)
            qs = _swa_stack(q_ref, g)
            bias_g = bias_ref[g * SW_GROUP:(g + 1) * SW_GROUP].reshape(SW_GROUP * W, 2 * W)
            logits = jnp.where(mask, _dot_nt(qs, kk) * scale + bias_g, NEG_BIG)
            sink = _swa_rows([_lane_pick(sk, g * SW_GROUP + r) for r in range(SW_GROUP)])
            m = jnp.maximum(jnp.max(logits, axis=-1, keepdims=True), sink)
            p = jnp.exp(logits - m)
            den = jnp.sum(p, axis=-1, keepdims=True) + jnp.exp(sink - m)
            _swa_unstack(o_ref, _dot(p, vv) / den, g)
            lse = m + jnp.log(den)
            for r in range(SW_GROUP):
                lse_tile = _lane_put(lse_tile, g * SW_GROUP + r, lse[r * W:(r + 1) * W])
        lse_ref[...] = lse_tile

    return pl.pallas_call(body, name=name, grid=(nb,),
                          in_specs=[pl.BlockSpec((None, W, D), lambda j: (0, j, 0)),
                                    pl.BlockSpec((None, W, 2 * kvw), lambda j: (0, j, 0)),
                                    pl.BlockSpec((None, W, 2 * kvw), lambda j: (0, jnp.maximum(j - 1, 0), 0)),
                                    pl.BlockSpec((SW_Q_HEADS, W, 2 * W), lambda j: (0, 0, 0)),
                                    pl.BlockSpec((1, SW_Q_HEADS), lambda j: (0, 0))],
                          out_specs=[pl.BlockSpec((None, W, D), lambda j: (0, j, 0)),
                                     pl.BlockSpec((W, SW_Q_HEADS), lambda j: (j, 0))],
                          out_shape=[jax.ShapeDtypeStruct((1, T, D), F32), jax.ShapeDtypeStruct((T, SW_Q_HEADS), F32)],
                          compiler_params=_params())(q, kv, kv, bias, sinks)


def _swa_bwd(q, kv, o, lse, do, bias, sinks, *, name):
    _, T, D = q.shape
    W = SW_WINDOW
    nb = T // W
    dh = SW_HEAD_DIM
    kvw = SW_KV_HEADS * dh
    scale = dh ** -0.5
    cl = lambda j: jnp.minimum(j, nb - 1)

    def body(q_ref, kc_ref, kp_ref, o_ref, lse_ref, do_ref, bias_ref, sink_ref,
             dq_ref, dkv_ref, dbias_ref, dsink_ref, carry):
        j = pl.program_id(0)

        @pl.when(j == 0)
        def _():
            carry[...] = jnp.zeros_like(carry)
            dbias_ref[...] = jnp.zeros_like(dbias_ref)
            dsink_ref[...] = jnp.zeros_like(dsink_ref)

        @pl.when(j < nb)
        def _():
            mask = _swa_mask(j)
            sk = sink_ref[...]
            lse_tile = lse_ref[...]
            dsink = jnp.zeros((1, SW_Q_HEADS), F32)
            dks, dvs = [], []
            for g in range(SW_KV_HEADS):
                kk = _swa_kv(kp_ref, kc_ref, 0, g)
                vv = _swa_kv(kp_ref, kc_ref, kvw, g)
                qs = _swa_stack(q_ref, g)
                os_ = _swa_stack(o_ref, g)
                dos = _swa_stack(do_ref, g)
                bias_g = bias_ref[g * SW_GROUP:(g + 1) * SW_GROUP].reshape(SW_GROUP * W, 2 * W)
                heads = [g * SW_GROUP + r for r in range(SW_GROUP)]
                lse = jnp.concatenate([_lane_pick(lse_tile, h) for h in heads], axis=0)
                sink = _swa_rows([_lane_pick(sk, h) for h in heads])
                logits = jnp.where(mask, _dot_nt(qs, kk) * scale + bias_g, NEG_BIG)
                p = jnp.exp(logits - lse)
                psink = jnp.exp(sink - lse)
                delta = jnp.sum(dos * os_, axis=-1, keepdims=True)
                dl = p * (_dot_nt(dos, vv) - delta)
                _swa_unstack(dq_ref, _dot(dl, kk) * scale, g)
                dks.append(_dot_tn(dl, qs) * scale)
                dvs.append(_dot_tn(p, dos))
                dbias_ref[g * SW_GROUP:(g + 1) * SW_GROUP] += dl.reshape(SW_GROUP, W, 2 * W)
                sd = psink * delta
                for r, h in enumerate(heads):
                    dsink = _lane_put(dsink, h, -jnp.sum(sd[r * W:(r + 1) * W], axis=0, keepdims=True))
            dsink_ref[...] += dsink
            for slab in range(SW_KV_HEADS // 2):
                for col0, parts in ((0, dks), (kvw, dvs)):
                    both = parts[2 * slab] + parts[2 * slab + 1]
                    cols = slice(col0 + slab * LANES, col0 + (slab + 1) * LANES)
                    dkv_ref[:, cols] = carry[:, cols] + both[:W]
                    carry[:, cols] = both[W:]

        @pl.when(j == nb)
        def _():
            dkv_ref[...] = carry[...]

    tok = lambda w: pl.BlockSpec((None, W, w), lambda j: (0, cl(j), 0))
    return pl.pallas_call(body, name=name, grid=(nb + 1,),
                          in_specs=[tok(D), tok(2 * kvw),
                                    pl.BlockSpec((None, W, 2 * kvw), lambda j: (0, jnp.maximum(cl(j) - 1, 0), 0)),
                                    tok(D), pl.BlockSpec((W, SW_Q_HEADS), lambda j: (cl(j), 0)), tok(D),
                                    pl.BlockSpec((SW_Q_HEADS, W, 2 * W), lambda j: (0, 0, 0)),
                                    pl.BlockSpec((1, SW_Q_HEADS), lambda j: (0, 0))],
                          out_specs=[tok(D),
                                     pl.BlockSpec((None, W, 2 * kvw), lambda j: (0, jnp.maximum(j - 1, 0), 0)),
                                     pl.BlockSpec((SW_Q_HEADS, W, 2 * W), lambda j: (0, 0, 0)),
                                     pl.BlockSpec((1, SW_Q_HEADS), lambda j: (0, 0))],
                          out_shape=[jax.ShapeDtypeStruct((1, T, D), F32), jax.ShapeDtypeStruct((1, T, 2 * kvw), F32),
                                     jax.ShapeDtypeStruct((SW_Q_HEADS, W, 2 * W), F32),
                                     jax.ShapeDtypeStruct((1, SW_Q_HEADS), F32)],
                          scratch_shapes=[pltpu.VMEM((W, 2 * kvw), F32)],
                          compiler_params=_params())(q, kv, kv, o, lse, do, bias, sinks)


def _exchange(arrs, modes, *, name):
    n = len(arrs)
    out_shape = [jax.ShapeDtypeStruct((N_DEV,) + a.shape[-2:], a.dtype) for a in arrs]

    def body(*refs):
        ins, outs = refs[:n], refs[n:2 * n]
        send_sems, recv_sems, local_sems = refs[2 * n:]
        x, y, c = lax.axis_index("x"), lax.axis_index("y"), lax.axis_index("c")
        me = 4 * x + 2 * y + c
        remote, local = [], []
        for a in range(n):
            gather = modes[a] == "gather"
            lc = pltpu.make_async_copy(ins[a] if gather else ins[a].at[me], outs[a].at[me], local_sems.at[a])
            lc.start()
            local.append(lc)
            for k in range(1, N_DEV):
                px = (x + (k >> 2)) % 2
                py = (y + ((k >> 1) & 1)) % 2
                pc = (c + (k & 1)) % 2
                pid = 4 * px + 2 * py + pc
                cp = pltpu.make_async_remote_copy(
                    src_ref=ins[a] if gather else ins[a].at[pid], dst_ref=outs[a].at[me],
                    send_sem=send_sems.at[a * N_DEV + k], recv_sem=recv_sems.at[a * N_DEV + k],
                    device_id=(px, py, pc), device_id_type=pl.DeviceIdType.MESH)
                cp.start()
                rc = pltpu.make_async_remote_copy(
                    src_ref=ins[a] if gather else ins[a].at[pid], dst_ref=outs[a].at[pid],
                    send_sem=send_sems.at[a * N_DEV + k], recv_sem=recv_sems.at[a * N_DEV + k],
                    device_id=(px, py, pc), device_id_type=pl.DeviceIdType.MESH)
                remote.append((cp, rc))
        for cp, rc in remote:
            rc.wait_recv()
        for cp, rc in remote:
            cp.wait_send()
        for lc in local:
            lc.wait()

    any_spec = pl.BlockSpec(memory_space=pl.ANY)
    return pl.pallas_call(body, name=name, in_specs=[any_spec] * n, out_specs=[any_spec] * n, out_shape=out_shape,
                          scratch_shapes=[pltpu.SemaphoreType.DMA((n * N_DEV,)), pltpu.SemaphoreType.DMA((n * N_DEV,)),
                                          pltpu.SemaphoreType.DMA((n,))],
                          compiler_params=pltpu.CompilerParams(has_side_effects=True))(*arrs)


def _adam_math(w, g, m, v):
    m = ADAM_B1 * m + (1.0 - ADAM_B1) * g
    v = ADAM_B2 * v + (1.0 - ADAM_B2) * (g * g)
    m_hat = m / (1.0 - ADAM_B1 ** ADAM_STEP)
    v_hat = v / (1.0 - ADAM_B2 ** ADAM_STEP)
    delta = -ADAM_LR * (m_hat / (jnp.sqrt(v_hat) + ADAM_EPS) + ADAM_WD * w)
    return delta, m, v


def _adamw(parts, w, m, v, *, name, layer=None):
    S, R, C = parts.shape
    tr = R
    for cand in (256, 128, 64, 32, 16, 8):
        if R % cand == 0 and S * cand * C * 4 <= 4 * 2 ** 20:
            tr = cand
            break

    def body(p_ref, w_ref, m_ref, v_ref, g_ref, d_ref, nm_ref, nv_ref):
        g = p_ref[0]
        for s in range(1, S):
            g = g + p_ref[s]
        delta, nm, nv = _adam_math(w_ref[...], g, m_ref[...], v_ref[...])
        g_ref[...] = g
        d_ref[...] = delta
        nm_ref[...] = nm
        nv_ref[...] = nv

    if layer is None:
        wspec = pl.BlockSpec((tr, C), lambda i: (i, 0))
    else:
        wspec = pl.BlockSpec((None, tr, C), lambda i: (layer, i, 0))
    ospec = pl.BlockSpec((tr, C), lambda i: (i, 0))
    osh = jax.ShapeDtypeStruct((R, C), F32)
    return pl.pallas_call(body, name=name, grid=(R // tr,),
                          in_specs=[pl.BlockSpec((S, tr, C), lambda i: (0, i, 0)), wspec, wspec, wspec],
                          out_specs=[ospec] * 4, out_shape=[osh] * 4, compiler_params=_params())(parts, w, m, v)


def _sum_parts(parts, *, name):
    S, R, C = parts.shape

    def body(p_ref, o_ref):
        g = p_ref[0]
        for s in range(1, S):
            g = g + p_ref[s]
        o_ref[...] = g

    return pl.pallas_call(body, name=name, out_shape=jax.ShapeDtypeStruct((R, C), F32),
                          compiler_params=_params())(parts)


def _pack_rows(arrays):
    pieces, layout, row = [], [], 0
    for a in arrays:
        flat = a.reshape(-1).astype(F32)
        rows = -(-flat.shape[0] // (SUBLANES * LANES)) * SUBLANES
        flat = jnp.pad(flat, (0, rows * LANES - flat.shape[0]))
        pieces.append(flat.reshape(rows, LANES))
        layout.append((row, rows, a.shape))
        row += rows
    return jnp.concatenate(pieces, axis=0), layout


def _unpack_rows(packed, layout):
    out = []
    for row, rows, shape in layout:
        size = int(np.prod(shape))
        out.append(packed[row:row + rows].reshape(-1)[:size].reshape(shape))
    return out


def _ffn_fwd(h, w_in, w_out, cw, ln_g, ln_b, tag):
    u = _mm_nn(h, w_in, name=f"ffn_in_{tag}")
    act = _convgate_fwd(u, cw, name=f"ffn_gate_{tag}")
    hn, xh, rs = _mm_nn(act, w_out, res=h, res_scale=ALPHA, ln=(ln_g, ln_b), name=f"ffn_out_{tag}")
    return hn, xh, rs, u, act


def _ffn_bwd(dy, h, u, act, w_in, w_out, cw, tag):
    dw_out = _mm_tn(act, dy, name=f"ffn_dwout_{tag}")
    dact = _mm_nt(dy, w_out, name=f"ffn_dact_{tag}")
    du, dcw = _convgate_bwd(u, dact, cw, name=f"ffn_dgate_{tag}")
    dw_in = _mm_tn(h, du, name=f"ffn_dwin_{tag}")
    dh = _mm_nt(du, w_in, res=dy, res_scale=ALPHA, name=f"ffn_dh_{tag}")
    return dh, dw_in, dw_out, dcw


def kernel(x, hgrn_w_in, hgrn_lb_logits, hgrn_gnorm_w, hgrn_w_out, swa_w_q, swa_sinks, swa_w_out, shared_w_kv, rel_bias, ffn_w_in, ffn_conv_w, ffn_conv_b, ffn_w_out, ln_mix_g, ln_mix_b, ln_ffn_g, ln_ffn_b, loss_target, m_hgrn_w_in, m_hgrn_lb_logits, m_hgrn_gnorm_w, m_hgrn_w_out, m_swa_w_q, m_swa_sinks, m_swa_w_out, m_shared_w_kv, m_rel_bias, m_ffn_w_in, m_ffn_conv_w, m_ffn_conv_b, m_ffn_w_out, m_ln_mix_g, m_ln_mix_b, m_ln_ffn_g, m_ln_ffn_b, v_hgrn_w_in, v_hgrn_lb_logits, v_hgrn_gnorm_w, v_hgrn_w_out, v_swa_w_q, v_swa_sinks, v_swa_w_out, v_shared_w_kv, v_rel_bias, v_ffn_w_in, v_ffn_conv_w, v_ffn_conv_b, v_ffn_w_out, v_ln_mix_g, v_ln_mix_b, v_ln_ffn_g, v_ln_ffn_b):
    T = x.shape[1]
    D = D_MODEL
    W = SW_WINDOW
    fb = ffn_w_in.shape[2]
    me = 4 * lax.axis_index("x") + 2 * lax.axis_index("y") + lax.axis_index("c")

    small_fwd, small_fwd_layout = _pack_rows([hgrn_lb_logits, ffn_conv_w])
    big = [hgrn_w_in[0], hgrn_w_out[0], swa_w_q[0], swa_w_out[0], shared_w_kv,
           ffn_w_in[0], ffn_w_in[1], ffn_w_out[0], ffn_w_out[1]]
    gathered = _exchange([b.astype(MXU) for b in big] + [small_fwd], ["gather"] * (len(big) + 1),
                         name="gather_weights")
    w_hin = gathered[0][None]
    w_hout = gathered[1].reshape(1, 1, D, D)
    w_q = gathered[2].reshape(1, 1, D, D)
    w_o = gathered[3].reshape(1, 1, D, D)
    w_kv = gathered[4].reshape(1, 1, D, 2 * SW_KV_HEADS * SW_HEAD_DIM)
    w_fin = [gathered[5][None], gathered[6][None]]
    w_fout = [gathered[7].reshape(4, 1, 2 * ffn_w_out.shape[1], D), gathered[8].reshape(4, 1, 2 * ffn_w_out.shape[1], D)]
    small_all = gathered[9]
    (lb_row, lb_rows, _), (cw_row, cw_rows, _) = small_fwd_layout
    lbl = small_all[:, lb_row:lb_row + 2, :].transpose(1, 0, 2).reshape(2, D)
    conv_w_all = small_all[:, cw_row:cw_row + cw_rows, :].reshape(N_DEV, -1)[:, :DEPTH * 3 * fb]
    conv_w_all = conv_w_all.reshape(N_DEV, DEPTH, 3, fb).transpose(1, 0, 2, 3)
    conv_b_all = ffn_conv_b.reshape(DEPTH, N_DEV, 1, fb)
    no_pad = ((0, 0), (0, 0))
    cw = (jnp.pad(conv_w_all, no_pad + ((0, SUBLANES - 3), (0, 0)))
          + jnp.pad(conv_b_all, no_pad + ((3, SUBLANES - 4), (0, 0))))

    row = lambda a, l: a[l:l + 1]

    z = _mm_nn(x, w_hin, name="hgrn_in")
    og, states = _hgrn_fwd(z, lbl, hgrn_gnorm_w, name="hgrn_rec")
    h1, xh1, rs1 = _mm_nn(og, w_hout, res=x, res_scale=ALPHA, ln=(row(ln_mix_g, 0), row(ln_mix_b, 0)),
                          name="hgrn_out")
    h2, xh2, rs2, u0, act0 = _ffn_fwd(h1, w_fin[0], w_fout[0], cw[0], row(ln_ffn_g, 0), row(ln_ffn_b, 0), "l0")
    kv = _mm_nn(h2, w_kv, name="swa_kv")
    q = _mm_nn(h2, w_q, name="swa_q")
    onehot = _bucket_onehot()
    bias = _bias_expand(rel_bias.T, jnp.asarray(onehot.T, jnp.bfloat16), name="swa_bias").reshape(SW_Q_HEADS, W, 2 * W)
    ao, lse = _swa_fwd(q, kv, bias, swa_sinks, name="swa_attn")
    h3, xh3, rs3 = _mm_nn(ao, w_o, res=h2, res_scale=ALPHA, ln=(row(ln_mix_g, 1), row(ln_mix_b, 1)), name="swa_out")
    h4, xh4, rs4, u1, act1 = _ffn_fwd(h3, w_fin[1], w_fout[1], cw[1], row(ln_ffn_g, 1), row(ln_ffn_b, 1), "l1")
    dh4, loss_tile = _loss_head(h4, loss_target, name="loss_head")

    dy4, dg_f1, db_f1 = _ln_bwd(dh4, xh4, rs4, row(ln_ffn_g, 1), name="ln_ffn1_bwd")
    dh3, dw_fin1, dw_fout1, dcw1 = _ffn_bwd(dy4, h3, u1, act1, w_fin[1], w_fout[1], cw[1], "l1")
    dy3, dg_m1, db_m1 = _ln_bwd(dh3, xh3, rs3, row(ln_mix_g, 1), name="ln_mix1_bwd")
    dw_o = _mm_tn(ao, dy3, name="swa_dwo")
    dao = _mm_nt(dy3, w_o, name="swa_dao")
    dq, dkv, dbias, dsinks = _swa_bwd(q, kv, ao, lse, dao, bias, swa_sinks, name="swa_attn_bwd")
    drel_t = _bias_reduce(dbias.reshape(SW_Q_HEADS, W * 2 * W), jnp.asarray(onehot, jnp.bfloat16), name="swa_dbias")
    dw_q = _mm_tn(h2, dq, name="swa_dwq")
    dw_kv = _mm_tn(h2, dkv, name="swa_dwkv")
    dh2 = _mm_nt(dq, w_q, res=dy3, res_scale=ALPHA, name="swa_dh_q")
    dh2 = _mm_nt(dkv, w_kv, res=dh2, res_scale=1.0, name="swa_dh_kv")
    dy2, dg_f0, db_f0 = _ln_bwd(dh2, xh2, rs2, row(ln_ffn_g, 0), name="ln_ffn0_bwd")
    dh1, dw_fin0, dw_fout0, dcw0 = _ffn_bwd(dy2, h1, u0, act0, w_fin[0], w_fout[0], cw[0], "l0")
    dy1, dg_m0, db_m0 = _ln_bwd(dh1, xh1, rs1, row(ln_mix_g, 0), name="ln_mix0_bwd")
    dw_hout = _mm_tn(og, dy1, name="hgrn_dwout")
    dog = _mm_nt(dy1, w_hout, name="hgrn_dog")
    dz, dlb, dgw = _hgrn_bwd(z, dog, states, lbl, hgrn_gnorm_w, name="hgrn_rec_bwd")
    dw_hin = _mm_tn(x, dz, name="hgrn_dwin")
    dx = _mm_nt(dz, w_hin, res=dy1, res_scale=ALPHA, name="hgrn_dx")

    p0 = _sigmoid(lbl[0:1] - lbl[1:2])
    dl0 = dlb * p0 * (1.0 - p0)
    d_lbl = dl0 * jnp.array([[1.0], [-1.0]], F32)
    dcw = jnp.stack([dcw0, dcw1], axis=0)
    d_conv_w = dcw[:, :, 0:3, :]
    d_conv_b = dcw[:, :, 3, :].reshape(DEPTH, N_DEV * fb)
    first_row = lax.broadcasted_iota(jnp.int32, (DEPTH, D), 0) == 0
    two_rows = lambda a, b: jnp.where(first_row, a, b)
    d_ln_mix_g = two_rows(dg_m0, dg_m1)
    d_ln_mix_b = two_rows(db_m0, db_m1)
    d_ln_ffn_g = two_rows(dg_f0, dg_f1)
    d_ln_ffn_b = two_rows(db_f0, db_f1)
    small_grads, small_layout = _pack_rows([d_lbl, d_conv_w, dgw, dsinks, drel_t.T, d_conv_b, d_ln_mix_g, d_ln_mix_b,
                                            d_ln_ffn_g, d_ln_ffn_b, loss_tile[0:1, 0:1]])

    big_grads = [dw_hin.reshape(N_DEV, D, -1), dw_hout.reshape(N_DEV, D // N_DEV, D), dw_q.reshape(N_DEV, D // N_DEV, D),
                 dw_o.reshape(N_DEV, D // N_DEV, D), dw_kv.reshape(N_DEV, D // N_DEV, -1),
                 dw_fin0.reshape(N_DEV, D, fb), dw_fin1.reshape(N_DEV, D, fb),
                 dw_fout0.reshape(N_DEV, -1, D), dw_fout1.reshape(N_DEV, -1, D)]
    received = _exchange(big_grads + [small_grads], ["scatter"] * len(big_grads) + ["gather"], name="exchange_grads")

    outs = {}

    def put(name_, res):
        outs["grad_" + name_], outs["delta_" + name_], outs["new_m_" + name_], outs["new_v_" + name_] = res

    def big_update(name_, parts, w, m, v):
        shp = w.shape
        if w.ndim == 3 and shp[0] == 1:
            r = _adamw(parts, w[0], m[0], v[0], name="adamw_" + name_)
            put(name_, [a.reshape(shp) for a in r])
        else:
            r = _adamw(parts, w, m, v, name="adamw_" + name_)
            put(name_, r)

    big_update("hgrn_w_in", received[0], hgrn_w_in, m_hgrn_w_in, v_hgrn_w_in)
    big_update("hgrn_w_out", received[1], hgrn_w_out, m_hgrn_w_out, v_hgrn_w_out)
    big_update("swa_w_q", received[2], swa_w_q, m_swa_w_q, v_swa_w_q)
    big_update("swa_w_out", received[3], swa_w_out, m_swa_w_out, v_swa_w_out)
    big_update("shared_w_kv", received[4], shared_w_kv, m_shared_w_kv, v_shared_w_kv)
    for name_, idx, w, m, v in (("ffn_w_in", 5, ffn_w_in, m_ffn_w_in, v_ffn_w_in),
                                ("ffn_w_out", 7, ffn_w_out, m_ffn_w_out, v_ffn_w_out)):
        per_layer = [_adamw(received[idx + l], w, m, v, layer=l, name=f"adamw_{name_}_{l}") for l in range(DEPTH)]
        put(name_, [jnp.stack([per_layer[0][i], per_layer[1][i]], axis=0) for i in range(4)])

    small_sum = _sum_parts(received[9], name="sum_small_grads")
    (g_lbl, g_conv_w, g_gw, g_sinks, g_rel, g_conv_b, g_mix_g, g_mix_b, g_ffn_g, g_ffn_b,
     loss) = _unpack_rows(small_sum, small_layout)
    g_lbl_mine = lax.dynamic_slice_in_dim(g_lbl, me * (D // N_DEV), D // N_DEV, axis=1)
    g_conv_w_mine = lax.dynamic_index_in_dim(g_conv_w, me, axis=1, keepdims=False)
    small_names = ["hgrn_lb_logits", "ffn_conv_w", "hgrn_gnorm_w", "swa_sinks", "rel_bias", "ffn_conv_b",
                   "ln_mix_g", "ln_mix_b", "ln_ffn_g", "ln_ffn_b"]
    small_g = [g_lbl_mine, g_conv_w_mine, g_gw, g_sinks, g_rel, g_conv_b, g_mix_g, g_mix_b, g_ffn_g, g_ffn_b]
    small_w = [hgrn_lb_logits, ffn_conv_w, hgrn_gnorm_w, swa_sinks, rel_bias, ffn_conv_b, ln_mix_g, ln_mix_b,
               ln_ffn_g, ln_ffn_b]
    small_m = [m_hgrn_lb_logits, m_ffn_conv_w, m_hgrn_gnorm_w, m_swa_sinks, m_rel_bias, m_ffn_conv_b, m_ln_mix_g,
               m_ln_mix_b, m_ln_ffn_g, m_ln_ffn_b]
    small_v = [v_hgrn_lb_logits, v_ffn_conv_w, v_hgrn_gnorm_w, v_swa_sinks, v_rel_bias, v_ffn_conv_b, v_ln_mix_g,
               v_ln_mix_b, v_ln_ffn_g, v_ln_ffn_b]
    pg, lay = _pack_rows(small_g)
    pw, _ = _pack_rows(small_w)
    pm, _ = _pack_rows(small_m)
    pv, _ = _pack_rows(small_v)
    res = _adamw(pg[None], pw, pm, pv, name="adamw_small")
    unpacked = [_unpack_rows(r, lay) for r in res]
    for i, name_ in enumerate(small_names):
        put(name_, [unpacked[j][i] for j in range(4)])

    order = ["hgrn_w_in", "hgrn_lb_logits", "hgrn_gnorm_w", "hgrn_w_out", "swa_w_q", "swa_sinks", "swa_w_out",
             "shared_w_kv", "rel_bias", "ffn_w_in", "ffn_conv_w", "ffn_conv_b", "ffn_w_out", "ln_mix_g", "ln_mix_b",
             "ln_ffn_g", "ln_ffn_b"]
    result = [loss.reshape(()), dx]
    for kind in ("grad_", "delta_", "new_m_", "new_v_"):
        result += [outs[kind + n] for n in order]
    return tuple(result)
```

```python
import functools
import math

import numpy as np
import jax
import jax.numpy as jnp
from jax import lax
from jax.experimental import pallas as pl
from jax.experimental.pallas import tpu as pltpu

F32 = jnp.float32
MXU = jnp.bfloat16

N_DEV = 8
D_MODEL = 1024
DEPTH = 2
HG_HEADS = 8
HG_DIM = 128
HG_CHUNK = 64
SW_Q_HEADS = 16
SW_KV_HEADS = 4
SW_GROUP = 4
SW_HEAD_DIM = 64
SW_WINDOW = 128
REL_BUCKETS = 32
REL_MAX_DIST = 128
FFN_DIM = 2816
ALPHA = (2.0 * DEPTH) ** 0.25
LN_EPS = 1e-5
RMS_EPS = 1e-6
ADAM_LR = 0.001
ADAM_B1 = 0.9
ADAM_B2 = 0.999
ADAM_EPS = 1e-08
ADAM_WD = 0.01
ADAM_STEP = 10
EXP_CLAMP = 80.0
NEG_BIG = -1e30

SUBLANES = 8
LANES = 128
VMEM_LIMIT = 48 * 2 ** 20
TOKEN_TILE = 512
WIDE_TOKEN_TILE = 1024
GRAD_DTYPE = jnp.bfloat16
HALO = 16


def _params(**kw):
    return pltpu.CompilerParams(vmem_limit_bytes=VMEM_LIMIT, **kw)


def _sigmoid(x):
    return 1.0 / (1.0 + jnp.exp(-x))


def _dot(a, b):
    return jnp.dot(a.astype(MXU), b.astype(MXU), preferred_element_type=F32)


def _dot_nt(a, b):
    return lax.dot_general(a.astype(MXU), b.astype(MXU), (((1,), (1,)), ((), ())), preferred_element_type=F32)


def _dot_tn(a, b):
    return lax.dot_general(a.astype(MXU), b.astype(MXU), (((0,), (0,)), ((), ())), preferred_element_type=F32)


def _trunc_bf16(x):
    bits = lax.bitcast_convert_type(x, jnp.int32)
    return lax.bitcast_convert_type(bits & jnp.int32(-65536), F32)


def _split3(x):
    hi = _trunc_bf16(x)
    r = x - hi
    mid = _trunc_bf16(r)
    lo = r - mid
    return hi.astype(jnp.bfloat16), mid.astype(jnp.bfloat16), lo.astype(jnp.bfloat16)


def _dot_hp(a, b, contract):
    def halves(x):
        hi = _trunc_bf16(x)
        return hi.astype(jnp.bfloat16), (x - hi).astype(jnp.bfloat16)

    ah, al = halves(a)
    bh, bl = halves(b)
    d = lambda p, q: lax.dot_general(p, q, (contract, ((), ())), preferred_element_type=F32)
    return d(ah, bh) + d(ah, bl) + d(al, bh)


def _exact_dot(m01, x):
    hi, mid, lo = _split3(x)
    d = lambda p: jnp.dot(m01, p, preferred_element_type=F32)
    return d(hi) + d(mid) + d(lo)


def _exact_dot_r(x, m01):
    hi, mid, lo = _split3(x)
    d = lambda p: jnp.dot(p, m01, preferred_element_type=F32)
    return d(hi) + d(mid) + d(lo)


def _mm_nn(a, w, *, name, res=None, res_scale=1.0, ln=None, out_dtype=F32, tm=None):
    nbk, T, kw = a.shape
    _, nbn, _, nw = w.shape
    tm = min(tm or TOKEN_TILE, T)
    has_res = res is not None
    assert ln is None or nbn == 1

    def body(*refs):
        refs = list(refs)
        a_ref, w_ref = refs[:2]
        pos = 2
        res_ref = None
        if has_res:
            res_ref = refs[pos]
            pos += 1
        if ln is not None:
            g_ref, b_ref = refs[pos:pos + 2]
            pos += 2
        o_ref = refs[pos]
        pos += 1
        if ln is not None:
            ob_ref, xh_ref, rs_ref = refs[pos:pos + 3]
        for n in range(nbn):
            y = _dot(a_ref[0], w_ref[0, n])
            for k in range(1, nbk):
                y = y + _dot(a_ref[k], w_ref[k, n])
            if has_res:
                y = y + res_scale * res_ref[n].astype(F32)
            if ln is None:
                o_ref[n] = y.astype(o_ref.dtype)
            else:
                mu = jnp.mean(y, axis=-1, keepdims=True)
                yc = y - mu
                var = jnp.mean(yc * yc, axis=-1, keepdims=True)
                rstd = lax.rsqrt(var + LN_EPS)
                xh = yc * rstd
                xh_ref[n] = xh
                rs_ref[...] = rstd
                h = xh * g_ref[...] + b_ref[...]
                o_ref[n] = h
                ob_ref[n] = h.astype(ob_ref.dtype)

    in_specs = [pl.BlockSpec((nbk, tm, kw), lambda i: (0, i, 0)),
                pl.BlockSpec((nbk, nbn, kw, nw), lambda i: (0, 0, 0, 0))]
    args = [a, w]
    if has_res:
        in_specs.append(pl.BlockSpec((nbn, tm, nw), lambda i: (0, i, 0)))
        args.append(res)
    if ln is not None:
        in_specs += [pl.BlockSpec((1, nw), lambda i: (0, 0))] * 2
        args += list(ln)
    out_spec = pl.BlockSpec((nbn, tm, nw), lambda i: (0, i, 0))
    out_shape = jax.ShapeDtypeStruct((nbn, T, nw), out_dtype)
    if ln is not None:
        out_specs = [out_spec, out_spec, out_spec, pl.BlockSpec((tm, 1), lambda i: (i, 0))]
        out_shape = [out_shape, jax.ShapeDtypeStruct((nbn, T, nw), MXU), jax.ShapeDtypeStruct((nbn, T, nw), F32),
                     jax.ShapeDtypeStruct((T, 1), F32)]
    else:
        out_specs = out_spec
    return pl.pallas_call(body, name=name, grid=(T // tm,), in_specs=in_specs, out_specs=out_specs,
                          out_shape=out_shape, compiler_params=_params())(*args)


def _same(n):
    return n


def _mm_nt(dy, w, *, name, res=None, res_scale=1.0, out_dtype=F32, tm=None, n_map=_same):
    nbn, T, nw = dy.shape
    nbk, _, kw, _ = w.shape
    tm = min(tm or WIDE_TOKEN_TILE, T)
    has_res = res is not None

    def body(*refs):
        refs = list(refs)
        dy_ref, w_ref = refs[:2]
        pos = 2
        res_ref = None
        if has_res:
            res_ref = refs[pos]
            pos += 1
        o_ref = refs[pos]
        pos += 1
        acc_ref = refs[pos] if nbn > 1 else None
        n = pl.program_id(2)
        part = _dot_nt(dy_ref[...], w_ref[...])

        def finish(acc):
            y = acc
            if has_res:
                y = y + res_scale * res_ref[...].astype(F32)
            o_ref[...] = y.astype(o_ref.dtype)

        if nbn == 1:
            finish(part)
        else:
            @pl.when(n == 0)
            def _():
                acc_ref[...] = part

            @pl.when(n > 0)
            def _():
                acc_ref[...] += part

            @pl.when(n == nbn - 1)
            def _():
                finish(acc_ref[...])

    in_specs = [pl.BlockSpec((None, tm, nw), lambda i, k, n: (n, i, 0)),
                pl.BlockSpec((None, None, kw, nw), lambda i, k, n: (k, n_map(n), 0, 0))]
    args = [dy, w]
    if has_res:
        in_specs.append(pl.BlockSpec((None, tm, kw), lambda i, k, n: (k, i, 0)))
        args.append(res)
    scratch = [pltpu.VMEM((tm, kw), F32)] if nbn > 1 else []
    return pl.pallas_call(body, name=name, grid=(T // tm, nbk, nbn), in_specs=in_specs,
                          out_specs=pl.BlockSpec((None, tm, kw), lambda i, k, n: (k, i, 0)),
                          out_shape=jax.ShapeDtypeStruct((nbk, T, kw), out_dtype), scratch_shapes=scratch,
                          compiler_params=_params())(*args)


def _mm_tn(a, dy, *, name, tm=None, n_map=_same):
    nbk, T, kw = a.shape
    nbn, _, nw = dy.shape
    tm = min(tm or WIDE_TOKEN_TILE, T)
    nt = T // tm

    def body(a_ref, dy_ref, o_ref, acc_ref):
        i = pl.program_id(2)
        part = _dot_tn(a_ref[...], dy_ref[...])

        @pl.when(i == 0)
        def _():
            acc_ref[...] = part

        @pl.when(i > 0)
        def _():
            acc_ref[...] += part

        @pl.when(i == nt - 1)
        def _():
            o_ref[...] = acc_ref[...].astype(o_ref.dtype)

    return pl.pallas_call(body, name=name, grid=(nbk, nbn, nt),
                          in_specs=[pl.BlockSpec((None, tm, kw), lambda k, n, i: (k, i, 0)),
                                    pl.BlockSpec((None, tm, nw), lambda k, n, i: (n, i, 0))],
                          out_specs=pl.BlockSpec((None, None, kw, nw), lambda k, n, i: (k, n_map(n), 0, 0)),
                          out_shape=jax.ShapeDtypeStruct((nbk, nbn, kw, nw), GRAD_DTYPE),
                          scratch_shapes=[pltpu.VMEM((kw, nw), F32)],
                          compiler_params=_params())(a, dy)


def _ln_bwd(dh, xhat, rstd, g, *, name, tm=None):
    _, T, D = dh.shape
    tm = min(tm or TOKEN_TILE, T)

    def body(dh_ref, xh_ref, rs_ref, g_ref, dy_ref, dg_ref, db_ref):
        i = pl.program_id(0)
        dhv = dh_ref[...]
        xh = xh_ref[...]
        dxh = dhv * g_ref[...]
        m1 = jnp.mean(dxh, axis=-1, keepdims=True)
        m2 = jnp.mean(dxh * xh, axis=-1, keepdims=True)
        dy_ref[...] = rs_ref[...] * (dxh - m1 - xh * m2)
        dg = jnp.sum(dhv * xh, axis=0, keepdims=True)
        db = jnp.sum(dhv, axis=0, keepdims=True)

        @pl.when(i == 0)
        def _():
            dg_ref[...] = dg
            db_ref[...] = db

        @pl.when(i > 0)
        def _():
            dg_ref[...] += dg
            db_ref[...] += db

    tok = pl.BlockSpec((None, tm, D), lambda i: (0, i, 0))
    vec = pl.BlockSpec((1, D), lambda i: (0, 0))
    return pl.pallas_call(body, name=name, grid=(T // tm,),
                          in_specs=[tok, tok, pl.BlockSpec((tm, 1), lambda i: (i, 0)), vec],
                          out_specs=[tok, vec, vec],
                          out_shape=[jax.ShapeDtypeStruct((1, T, D), F32), jax.ShapeDtypeStruct((1, D), F32),
                                     jax.ShapeDtypeStruct((1, D), F32)],
                          compiler_params=_params())(dh, xhat, rstd, g)


def _loss_head(h, tgt, *, name, tm=None):
    _, T, D = h.shape
    tm = min(tm or TOKEN_TILE, T)

    def body(h_ref, t_ref, dh_ref, loss_ref):
        i = pl.program_id(0)
        err = h_ref[...] - t_ref[...]
        dh_ref[...] = err / D
        part = 0.5 * jnp.sum(jnp.mean(err * err, axis=-1, keepdims=True), axis=0, keepdims=True)

        @pl.when(i == 0)
        def _():
            loss_ref[...] = jnp.zeros_like(loss_ref) + part

        @pl.when(i > 0)
        def _():
            loss_ref[...] += part

    tok = pl.BlockSpec((None, tm, D), lambda i: (0, i, 0))
    return pl.pallas_call(body, name=name, grid=(T // tm,), in_specs=[tok, tok],
                          out_specs=[tok, pl.BlockSpec((SUBLANES, LANES), lambda i: (0, 0))],
                          out_shape=[jax.ShapeDtypeStruct((1, T, D), F32),
                                     jax.ShapeDtypeStruct((SUBLANES, LANES), F32)],
                          compiler_params=_params())(h, tgt)


def _shift_rows(ext, k, n, halo):
    if k == 0:
        return ext[halo:halo + n]
    return pltpu.roll(ext, k, axis=0)[halo:halo + n]


def _conv_rows(ext, cw_ref, n, halo):
    return (cw_ref[0:1, :] * _shift_rows(ext, 2, n, halo) + cw_ref[1:2, :] * _shift_rows(ext, 1, n, halo)
            + cw_ref[2:3, :] * ext[halo:halo + n] + cw_ref[3:4, :])


def _pair_map(n):
    return n // 2 + 4 * (n % 2)


def _ffn_up(hb, w_in, cw, *, name, tm=None):
    _, T, D = hb.shape
    _, nb, _, fb = w_in.shape
    half = nb // 2
    tm = min(tm or TOKEN_TILE, T)

    def body(h_ref, wa_ref, wb_ref, cwa_ref, cwb_ref, u_ref, act_ref, carry):
        @pl.when(pl.program_id(1) == 0)
        def _():
            carry[...] = jnp.zeros_like(carry)

        h = h_ref[...]
        conv = []
        for s, (w_ref, cw_ref) in enumerate(((wa_ref, cwa_ref), (wb_ref, cwb_ref))):
            u = _dot(h, w_ref[...]).astype(u_ref.dtype)
            u_ref[s] = u
            uf = u.astype(F32)
            ext = jnp.concatenate([carry[s], uf], axis=0)
            conv.append(_conv_rows(ext, cw_ref, tm, SUBLANES))
            carry[s] = uf[tm - SUBLANES:tm]
        a, b = conv
        act_ref[...] = (a * _sigmoid(a) * b).astype(act_ref.dtype)

    wspec = lambda off: pl.BlockSpec((None, None, D, fb), lambda p, i: (0, p + off, 0, 0))
    cws = lambda off: pl.BlockSpec((None, SUBLANES, fb), lambda p, i: (p + off, 0, 0))
    return pl.pallas_call(body, name=name, grid=(half, T // tm),
                          in_specs=[pl.BlockSpec((None, tm, D), lambda p, i: (0, i, 0)), wspec(0), wspec(half),
                                    cws(0), cws(half)],
                          out_specs=[pl.BlockSpec((None, 2, tm, fb), lambda p, i: (p, 0, i, 0)),
                                     pl.BlockSpec((None, tm, fb), lambda p, i: (p, i, 0))],
                          out_shape=[jax.ShapeDtypeStruct((half, 2, T, fb), MXU),
                                     jax.ShapeDtypeStruct((half, T, fb), MXU)],
                          scratch_shapes=[pltpu.VMEM((2, SUBLANES, fb), F32)],
                          compiler_params=_params())(hb, w_in, w_in, cw, cw)


def _ffn_gate_bwd(dy, u, w_out, cw, *, name, tm=None):
    _, T, D = dy.shape
    half, _, _, fb = u.shape
    tm = min(tm or TOKEN_TILE, T)
    nt = T // tm
    rh = tm // HALO

    def body(dy_ref, u_ref, up_ref, w_ref, cwa_ref, cwb_ref, du_ref, dwo_ref, dcw_ref, carry, acc):
        i = pl.program_id(1)
        tile = nt - 1 - i

        @pl.when(i == 0)
        def _():
            carry[...] = jnp.zeros_like(carry)
            acc[...] = jnp.zeros_like(acc)
            dcw_ref[...] = jnp.zeros_like(dcw_ref)

        dyv = dy_ref[...]
        dact = _dot_nt(dyv, w_ref[...])
        exts, conv = [], []
        for s, cw_ref in enumerate((cwa_ref, cwb_ref)):
            prev = jnp.where(tile == 0, 0.0, up_ref[s].astype(F32))
            ext = jnp.concatenate([prev, u_ref[s].astype(F32)], axis=0)
            exts.append(ext)
            conv.append(_conv_rows(ext, cw_ref, tm, HALO))
        a, b = conv
        sa = _sigmoid(a)
        silu = a * sa
        acc[...] += _dot_tn(silu * b, dyv)
        dcs = (dact * b * (sa * (1.0 + a * (1.0 - sa))), dact * silu)
        m = tm + SUBLANES
        rows = lax.broadcasted_iota(jnp.int32, (SUBLANES, fb), 0)
        for s, cw_ref in enumerate((cwa_ref, cwb_ref)):
            dc = dcs[s]
            nxt = jnp.concatenate([dc, carry[s]], axis=0)
            du = (cw_ref[2:3, :] * dc + cw_ref[1:2, :] * pltpu.roll(nxt, m - 1, axis=0)[:tm]
                  + cw_ref[0:1, :] * pltpu.roll(nxt, m - 2, axis=0)[:tm])
            du_ref[s] = du.astype(du_ref.dtype)
            carry[s] = dc[0:SUBLANES]
            g0 = jnp.sum(dc * _shift_rows(exts[s], 2, tm, HALO), axis=0, keepdims=True)
            g1 = jnp.sum(dc * _shift_rows(exts[s], 1, tm, HALO), axis=0, keepdims=True)
            g2 = jnp.sum(dc * exts[s][HALO:HALO + tm], axis=0, keepdims=True)
            g3 = jnp.sum(dc, axis=0, keepdims=True)
            dcw_ref[s] += jnp.where(rows == 0, g0, jnp.where(rows == 1, g1, jnp.where(rows == 2, g2,
                                                                                jnp.where(rows == 3, g3, 0.0))))

        @pl.when(i == nt - 1)
        def _():
            dwo_ref[...] = acc[...].astype(dwo_ref.dtype)

    rev = lambda i: nt - 1 - i
    cws = lambda off: pl.BlockSpec((None, SUBLANES, fb), lambda p, i: (p + off, 0, 0))
    return pl.pallas_call(body, name=name, grid=(half, nt),
                          in_specs=[pl.BlockSpec((None, tm, D), lambda p, i: (0, rev(i), 0)),
                                    pl.BlockSpec((None, 2, tm, fb), lambda p, i: (p, 0, rev(i), 0)),
                                    pl.BlockSpec((None, 2, HALO, fb),
                                                 lambda p, i: (p, 0, jnp.maximum(rev(i) * rh - 1, 0), 0)),
                                    pl.BlockSpec((None, None, fb, D), lambda p, i: (p, 0, 0, 0)),
                                    cws(0), cws(half)],
                          out_specs=[pl.BlockSpec((None, 2, tm, fb), lambda p, i: (p, 0, rev(i), 0)),
                                     pl.BlockSpec((None, None, fb, D), lambda p, i: (p, 0, 0, 0)),
                                     pl.BlockSpec((None, 2, SUBLANES, fb), lambda p, i: (p, 0, 0, 0))],
                          out_shape=[jax.ShapeDtypeStruct((half, 2, T, fb), MXU),
                                     jax.ShapeDtypeStruct((half, 1, fb, D), GRAD_DTYPE),
                                     jax.ShapeDtypeStruct((half, 2, SUBLANES, fb), F32)],
                          scratch_shapes=[pltpu.VMEM((2, SUBLANES, fb), F32), pltpu.VMEM((fb, D), F32)],
                          compiler_params=_params())(dy, u, u, w_out, cw, cw)


def _tri(n, lower):
    r = lax.broadcasted_iota(jnp.int32, (n, n), 0)
    c = lax.broadcasted_iota(jnp.int32, (n, n), 1)
    return (r >= c) if lower else (r <= c)


def _hgrn_gates(zq, zf, lb):
    sq = _sigmoid(zq)
    sf = _sigmoid(zf)
    fg = lb + (1.0 - lb) * sf
    return zq * sq, sq, sf, fg, jnp.log(fg)


def _lb_of(lbl_ref, cols):
    return _sigmoid(lbl_ref[0:1, cols] - lbl_ref[1:2, cols])


def _ones_where(mask):
    return jnp.where(mask, 1.0, 0.0).astype(jnp.bfloat16)


def _hgrn_fwd(z, lbl, gw, *, name):
    _, T, zw = z.shape
    C = min(HG_CHUNK, T)
    nch = T // C
    hpb = zw // HG_DIM

    def body(z_ref, lbl_ref, gw_ref, og_ref, st_ref, s_scr, bc_scr):
        c = pl.program_id(0)

        @pl.when(c == 0)
        def _():
            s_scr[...] = jnp.zeros_like(s_scr)

        low = _tri(C, True)
        low01 = _ones_where(low)
        gwv = gw_ref[...]
        for blk in range(2):
            zq = z_ref[blk]
            zf = z_ref[2 + blk]
            qq, _, _, fg, lf = _hgrn_gates(zq, zf, _lb_of(lbl_ref, slice(blk * zw, (blk + 1) * zw)))
            kk = 1.0 - fg
            bc_scr[...] = _exact_dot(low01, lf)
            for hh in range(hpb):
                h = blk * hpb + hh
                cols = slice(hh * HG_DIM, (hh + 1) * HG_DIM)
                b = bc_scr[:, cols]
                q_h, k_h = qq[:, cols], kk[:, cols]
                v_h = z_ref[4 + blk, :, cols]
                g_h = z_ref[6 + blk, :, cols]
                bm = bc_scr[C // 2 - 1:C // 2, cols]
                bl = bc_scr[C - 1:C, cols]
                qt = q_h * jnp.exp(jnp.minimum(b - bm, EXP_CLAMP))
                kt = k_h * jnp.exp(jnp.minimum(bm - b, EXP_CLAMP))
                A = jnp.where(low, _dot_nt(qt, kt), 0.0)
                s0 = s_scr[h]
                st_ref[h] = s0
                o = _dot_nt(q_h * jnp.exp(b), s0) + _dot(A, v_h)
                s_scr[h] = s0 * jnp.exp(bl) + _dot_tn(v_h, k_h * jnp.exp(bl - b))
                r = lax.rsqrt(jnp.mean(o * o, axis=-1, keepdims=True) + RMS_EPS)
                og = o * r * gwv * (g_h * _sigmoid(g_h))
                og_ref[:, h * HG_DIM:(h + 1) * HG_DIM] = og.astype(og_ref.dtype)

    return pl.pallas_call(body, name=name, grid=(nch,),
                          in_specs=[pl.BlockSpec((8, C, zw), lambda c: (0, c, 0)),
                                    pl.BlockSpec((2, D_MODEL), lambda c: (0, 0)),
                                    pl.BlockSpec((1, HG_DIM), lambda c: (0, 0))],
                          out_specs=[pl.BlockSpec((None, C, D_MODEL), lambda c: (0, c, 0)),
                                     pl.BlockSpec((None, HG_HEADS, HG_DIM, HG_DIM), lambda c: (c, 0, 0, 0))],
                          out_shape=[jax.ShapeDtypeStruct((1, T, D_MODEL), MXU),
                                     jax.ShapeDtypeStruct((nch, HG_HEADS, HG_DIM, HG_DIM), F32)],
                          scratch_shapes=[pltpu.VMEM((HG_HEADS, HG_DIM, HG_DIM), F32), pltpu.VMEM((C, zw), F32)],
                          compiler_params=_params())(z, lbl, gw)


def _hgrn_bwd(z, dog, states, lbl, gw, *, name):
    _, T, zw = z.shape
    C = min(HG_CHUNK, T)
    nch = T // C
    hpb = zw // HG_DIM

    def body(z_ref, dog_ref, st0_ref, st1_ref, lbl_ref, gw_ref, dz_ref, dlb_ref, dgw_ref, d_scr, bc_scr):
        step = pl.program_id(0)

        @pl.when(step == 0)
        def _():
            d_scr[...] = jnp.zeros_like(d_scr)
            dlb_ref[...] = jnp.zeros_like(dlb_ref)
            dgw_ref[...] = jnp.zeros_like(dgw_ref)

        low = _tri(C, True)
        low01 = _ones_where(low)
        up01 = _ones_where(_tri(C, False))
        gwv = gw_ref[...]
        dgw_acc = jnp.zeros((1, HG_DIM), F32)
        for blk in range(2):
            zq = z_ref[blk]
            zf = z_ref[2 + blk]
            lbb = _lb_of(lbl_ref, slice(blk * zw, (blk + 1) * zw))
            qq, sq, sf, fg, lf = _hgrn_gates(zq, zf, lbb)
            kk = 1.0 - fg
            bc_scr[...] = _exact_dot(low01, lf)
            dlb_blk = []
            for hh in range(hpb):
                h = blk * hpb + hh
                cols = slice(hh * HG_DIM, (hh + 1) * HG_DIM)
                b = bc_scr[:, cols]
                q_h, k_h = qq[:, cols], kk[:, cols]
                v_h = z_ref[4 + blk, :, cols]
                g_h = z_ref[6 + blk, :, cols]
                bm = bc_scr[C // 2 - 1:C // 2, cols]
                bl = bc_scr[C - 1:C, cols]
                eq = jnp.exp(jnp.minimum(b - bm, EXP_CLAMP))
                ek = jnp.exp(jnp.minimum(bm - b, EXP_CLAMP))
                eb = jnp.exp(b)
                el = jnp.exp(bl - b)
                ebl = jnp.exp(bl)
                qt, kt, q0, kd = q_h * eq, k_h * ek, q_h * eb, k_h * el
                A = jnp.where(low, _dot_nt(qt, kt), 0.0)
                s0 = st0_ref[h]
                s1 = st1_ref[h]
                dt = d_scr[h]
                o = _dot_nt(q0, s0) + _dot(A, v_h)
                r = lax.rsqrt(jnp.mean(o * o, axis=-1, keepdims=True) + RMS_EPS)
                on = o * r
                sg = _sigmoid(g_h)
                dogh = dog_ref[:, h * HG_DIM:(h + 1) * HG_DIM].astype(F32)
                t1 = dogh * on
                dgw_acc = dgw_acc + jnp.sum(t1 * (g_h * sg), axis=0, keepdims=True)
                dg = t1 * gwv * (sg * (1.0 + g_h * (1.0 - sg)))
                don = dogh * gwv * (g_h * sg)
                do = r * (don - on * jnp.mean(don * on, axis=-1, keepdims=True))
                P = jnp.where(low, _dot_nt(do, v_h), 0.0)
                dqq = eb * _dot(do, s0) + eq * _dot_hp(P, kt, ((1,), (0,)))
                dkk = el * _dot(v_h, dt) + ek * _dot_hp(P, qt, ((0,), (0,)))
                dv = _dot_nt(kd, dt) + _dot_tn(A, do)
                d_scr[h] = dt * ebl + _dot_tn(do, q0)
                edge = jnp.sum(dt * s1, axis=0, keepdims=True)
                dlf = _exact_dot(up01, q_h * dqq - k_h * dkk) + edge
                fg_h = fg[:, cols]
                sf_h = sf[:, cols]
                sq_h = sq[:, cols]
                zq_h = zq[:, cols]
                dfg = dlf / fg_h - dkk
                lb_h = lbb[:, cols]
                dlb_blk.append(jnp.sum(dfg * (1.0 - sf_h), axis=0, keepdims=True))
                dz_ref[blk, :, cols] = dqq * (sq_h * (1.0 + zq_h * (1.0 - sq_h)))
                dz_ref[2 + blk, :, cols] = dfg * (1.0 - lb_h) * sf_h * (1.0 - sf_h)
                dz_ref[4 + blk, :, cols] = dv
                dz_ref[6 + blk, :, cols] = dg
            dlb_ref[:, blk * zw:(blk + 1) * zw] += jnp.concatenate(dlb_blk, axis=1)
        dgw_ref[...] += dgw_acc

    rev = lambda s: nch - 1 - s
    return pl.pallas_call(body, name=name, grid=(nch,),
                          in_specs=[pl.BlockSpec((8, C, zw), lambda s: (0, rev(s), 0)),
                                    pl.BlockSpec((None, C, D_MODEL), lambda s: (0, rev(s), 0)),
                                    pl.BlockSpec((None, HG_HEADS, HG_DIM, HG_DIM), lambda s: (rev(s), 0, 0, 0)),
                                    pl.BlockSpec((None, HG_HEADS, HG_DIM, HG_DIM),
                                                 lambda s: (jnp.minimum(rev(s) + 1, nch - 1), 0, 0, 0)),
                                    pl.BlockSpec((2, D_MODEL), lambda s: (0, 0)),
                                    pl.BlockSpec((1, HG_DIM), lambda s: (0, 0))],
                          out_specs=[pl.BlockSpec((8, C, zw), lambda s: (0, rev(s), 0)),
                                     pl.BlockSpec((1, D_MODEL), lambda s: (0, 0)),
                                     pl.BlockSpec((1, HG_DIM), lambda s: (0, 0))],
                          out_shape=[jax.ShapeDtypeStruct((8, T, zw), F32), jax.ShapeDtypeStruct((1, D_MODEL), F32),
                                     jax.ShapeDtypeStruct((1, HG_DIM), F32)],
                          scratch_shapes=[pltpu.VMEM((HG_HEADS, HG_DIM, HG_DIM), F32), pltpu.VMEM((C, zw), F32)],
                          compiler_params=_params())(z, dog, states, states, lbl, gw)


def _bucket_onehot():
    W = SW_WINDOW
    t = np.arange(W)[:, None] + W
    s = np.arange(2 * W)[None, :]
    dist = t - s
    exact = REL_BUCKETS // 2
    d = np.maximum(np.maximum(dist, 0), 1).astype(np.float32)
    log_b = exact + (np.log(d / np.float32(exact)) / np.float32(math.log(REL_MAX_DIST / exact))
                     * np.float32(REL_BUCKETS - exact)).astype(np.int32)
    bucket = np.where(np.maximum(dist, 0) < exact, np.maximum(dist, 0), np.minimum(log_b, REL_BUCKETS - 1))
    valid = (dist >= 0) & (dist < W)
    onehot = (bucket[..., None] == np.arange(REL_BUCKETS)) & valid[..., None]
    return onehot.reshape(W * 2 * W, REL_BUCKETS).astype(np.float32)


def _bias_expand(rel_t, onehot_t, *, name):
    hq, nbk = rel_t.shape
    n = onehot_t.shape[1]

    def body(r_ref, oh_ref, o_ref):
        o_ref[...] = _exact_dot_r(r_ref[...], oh_ref[...])

    return pl.pallas_call(body, name=name, out_shape=jax.ShapeDtypeStruct((hq, n), F32),
                          compiler_params=_params())(rel_t, onehot_t)


def _bias_reduce(dbias, onehot, *, name):
    hq = dbias.shape[0]
    nbk = onehot.shape[1]

    def body(d_ref, oh_ref, o_ref):
        o_ref[...] = _exact_dot_r(d_ref[...], oh_ref[...])

    return pl.pallas_call(body, name=name, out_shape=jax.ShapeDtypeStruct((hq, nbk), F32),
                          compiler_params=_params())(dbias, onehot)


def _swa_mask(j):
    W = SW_WINDOW
    t = lax.broadcasted_iota(jnp.int32, (W, 2 * W), 0) + W
    s = lax.broadcasted_iota(jnp.int32, (W, 2 * W), 1)
    dist = t - s
    band = (dist >= 0) & (dist < W)
    m = band & ((j > 0) | (s >= W))
    return jnp.concatenate([m] * SW_GROUP, axis=0)


def _half_mask(rows, half):
    lane = lax.broadcasted_iota(jnp.int32, (rows, LANES), 1)
    return (lane >= SW_HEAD_DIM) if half else (lane < SW_HEAD_DIM)


def _swa_head(ref, col0, head, to_half):
    slab, half = head // 2, head % 2
    x = ref[:, col0 + slab * LANES:col0 + (slab + 1) * LANES]
    x = jnp.where(_half_mask(x.shape[0], half), x, 0.0)
    return x if half == to_half else pltpu.roll(x, SW_HEAD_DIM, axis=1)


def _swa_stack(ref, g):
    return jnp.concatenate([_swa_head(ref, 0, g * SW_GROUP + r, g % 2) for r in range(SW_GROUP)], axis=0)


def _swa_unstack(ref, x, g):
    W = SW_WINDOW
    for pair in range(SW_GROUP // 2):
        parts = []
        for r in (2 * pair, 2 * pair + 1):
            piece = x[r * W:(r + 1) * W]
            parts.append(piece if r % 2 == g % 2 else pltpu.roll(piece, SW_HEAD_DIM, axis=1))
        slab = (g * SW_GROUP) // 2 + pair
        ref[:, slab * LANES:(slab + 1) * LANES] = parts[0] + parts[1]


def _swa_kv(kp_ref, kc_ref, col0, g):
    return jnp.concatenate([_swa_head(kp_ref, col0, g, g % 2), _swa_head(kc_ref, col0, g, g % 2)], axis=0)


def _lane_pick(tile, h):
    lane = lax.broadcasted_iota(jnp.int32, tile.shape, 1)
    return jnp.sum(jnp.where(lane == h, tile, 0.0), axis=-1, keepdims=True)


def _lane_put(tile, h, col):
    lane = lax.broadcasted_iota(jnp.int32, tile.shape, 1)
    return jnp.where(lane == h, col, tile)


def _swa_rows(vals):
    return jnp.concatenate([jnp.broadcast_to(v, (SW_WINDOW, 1)) for v in vals], axis=0)


def _swa_fwd(q, kv, bias, sinks, *, name):
    _, T, D = q.shape
    W = SW_WINDOW
    nb = T // W
    dh = SW_HEAD_DIM
    kvw = SW_KV_HEADS * dh
    scale = dh ** -0.5

    def body(q_ref, kc_ref, kp_ref, bias_ref, sink_ref, o_ref, lse_ref):
        j = pl.program_id(0)
        mask = _swa_mask(j)
        sk = sink_ref[...]
        lse_tile = jnp.zeros((W, SW_Q_HEADS), F32)
        for g in range(SW_KV_HEADS):
            kk = _swa_kv(kp_ref, kc_ref, 0, g)
            vv = _swa_kv(kp_ref, kc_ref, kvw, g)
            qs = _swa_stack(q_ref, g)
            bias_g = bias_ref[g * SW_GROUP:(g + 1) * SW_GROUP].reshape(SW_GROUP * W, 2 * W)
            logits = jnp.where(mask, _dot_nt(qs, kk) * scale + bias_g, NEG_BIG)
            sink = _swa_rows([_lane_pick(sk, g * SW_GROUP + r) for r in range(SW_GROUP)])
            m = jnp.maximum(jnp.max(logits, axis=-1, keepdims=True), sink)
            p = jnp.exp(logits - m)
            den = jnp.sum(p, axis=-1, keepdims=True) + jnp.exp(sink - m)
            _swa_unstack(o_ref, _dot(p, vv) / den, g)
            lse = m + jnp.log(den)
            for r in range(SW_GROUP):
                lse_tile = _lane_put(lse_tile, g * SW_GROUP + r, lse[r * W:(r + 1) * W])
        lse_ref[...] = lse_tile

    return pl.pallas_call(body, name=name, grid=(nb,),
                          in_specs=[pl.BlockSpec((None, W, D), lambda j: (0, j, 0)),
                                    pl.BlockSpec((None, W, 2 * kvw), lambda j: (0, j, 0)),
                                    pl.BlockSpec((None, W, 2 * kvw), lambda j: (0, jnp.maximum(j - 1, 0), 0)),
                                    pl.BlockSpec((SW_Q_HEADS, W, 2 * W), lambda j: (0, 0, 0)),
                                    pl.BlockSpec((1, SW_Q_HEADS), lambda j: (0, 0))],
                          out_specs=[pl.BlockSpec((None, W, D), lambda j: (0, j, 0)),
                                     pl.BlockSpec((W, SW_Q_HEADS), lambda j: (j, 0))],
                          out_shape=[jax.ShapeDtypeStruct((1, T, D), F32), jax.ShapeDtypeStruct((T, SW_Q_HEADS), F32)],
                          compiler_params=_params())(q, kv, kv, bias, sinks)


def _swa_bwd(q, kv, o, lse, do, bias, sinks, *, name):
    _, T, D = q.shape
    W = SW_WINDOW
    nb = T // W
    dh = SW_HEAD_DIM
    kvw = SW_KV_HEADS * dh
    scale = dh ** -0.5
    cl = lambda j: jnp.minimum(j, nb - 1)

    def body(q_ref, kc_ref, kp_ref, o_ref, lse_ref, do_ref, bias_ref, sink_ref,
             dq_ref, dkv_ref, dbias_ref, dsink_ref, carry):
        j = pl.program_id(0)

        @pl.when(j == 0)
        def _():
            carry[...] = jnp.zeros_like(carry)
            dbias_ref[...] = jnp.zeros_like(dbias_ref)
            dsink_ref[...] = jnp.zeros_like(dsink_ref)

        @pl.when(j < nb)
        def _():
            mask = _swa_mask(j)
            sk = sink_ref[...]
            lse_tile = lse_ref[...]
            dsink = jnp.zeros((1, SW_Q_HEADS), F32)
            dks, dvs = [], []
            for g in range(SW_KV_HEADS):
                kk = _swa_kv(kp_ref, kc_ref, 0, g)
                vv = _swa_kv(kp_ref, kc_ref, kvw, g)
                qs = _swa_stack(q_ref, g)
                os_ = _swa_stack(o_ref, g)
                dos = _swa_stack(do_ref, g)
                bias_g = bias_ref[g * SW_GROUP:(g + 1) * SW_GROUP].reshape(SW_GROUP * W, 2 * W)
                heads = [g * SW_GROUP + r for r in range(SW_GROUP)]
                lse = jnp.concatenate([_lane_pick(lse_tile, h) for h in heads], axis=0)
                sink = _swa_rows([_lane_pick(sk, h) for h in heads])
                logits = jnp.where(mask, _dot_nt(qs, kk) * scale + bias_g, NEG_BIG)
                p = jnp.exp(logits - lse)
                psink = jnp.exp(sink - lse)
                delta = jnp.sum(dos * os_, axis=-1, keepdims=True)
                dl = p * (_dot_nt(dos, vv) - delta)
                _swa_unstack(dq_ref, _dot(dl, kk) * scale, g)
                dks.append(_dot_tn(dl, qs) * scale)
                dvs.append(_dot_tn(p, dos))
                dbias_ref[g * SW_GROUP:(g + 1) * SW_GROUP] += dl.reshape(SW_GROUP, W, 2 * W)
                sd = psink * delta
                for r, h in enumerate(heads):
                    dsink = _lane_put(dsink, h, -jnp.sum(sd[r * W:(r + 1) * W], axis=0, keepdims=True))
            dsink_ref[...] += dsink
            for slab in range(SW_KV_HEADS // 2):
                for col0, parts in ((0, dks), (kvw, dvs)):
                    both = parts[2 * slab] + parts[2 * slab + 1]
                    cols = slice(col0 + slab * LANES, col0 + (slab + 1) * LANES)
                    dkv_ref[:, cols] = carry[:, cols] + both[:W]
                    carry[:, cols] = both[W:]

        @pl.when(j == nb)
        def _():
            dkv_ref[...] = carry[...]

    tok = lambda w: pl.BlockSpec((None, W, w), lambda j: (0, cl(j), 0))
    return pl.pallas_call(body, name=name, grid=(nb + 1,),
                          in_specs=[tok(D), tok(2 * kvw),
                                    pl.BlockSpec((None, W, 2 * kvw), lambda j: (0, jnp.maximum(cl(j) - 1, 0), 0)),
                                    tok(D), pl.BlockSpec((W, SW_Q_HEADS), lambda j: (cl(j), 0)), tok(D),
                                    pl.BlockSpec((SW_Q_HEADS, W, 2 * W), lambda j: (0, 0, 0)),
                                    pl.BlockSpec((1, SW_Q_HEADS), lambda j: (0, 0))],
                          out_specs=[tok(D),
                                     pl.BlockSpec((None, W, 2 * kvw), lambda j: (0, jnp.maximum(j - 1, 0), 0)),
                                     pl.BlockSpec((SW_Q_HEADS, W, 2 * W), lambda j: (0, 0, 0)),
                                     pl.BlockSpec((1, SW_Q_HEADS), lambda j: (0, 0))],
                          out_shape=[jax.ShapeDtypeStruct((1, T, D), F32), jax.ShapeDtypeStruct((1, T, 2 * kvw), F32),
                                     jax.ShapeDtypeStruct((SW_Q_HEADS, W, 2 * W), F32),
                                     jax.ShapeDtypeStruct((1, SW_Q_HEADS), F32)],
                          scratch_shapes=[pltpu.VMEM((W, 2 * kvw), F32)],
                          compiler_params=_params())(q, kv, kv, o, lse, do, bias, sinks)


def _exchange(arrs, modes, *, name):
    n = len(arrs)
    out_shape = [jax.ShapeDtypeStruct((N_DEV,) + a.shape[-2:], a.dtype) for a in arrs]

    def body(*refs):
        ins, outs = refs[:n], refs[n:2 * n]
        send_sems, recv_sems, local_sems = refs[2 * n:]
        x, y, c = lax.axis_index("x"), lax.axis_index("y"), lax.axis_index("c")
        me = 4 * x + 2 * y + c
        remote, local = [], []
        for a in range(n):
            gather = modes[a] == "gather"
            lc = pltpu.make_async_copy(ins[a] if gather else ins[a].at[me], outs[a].at[me], local_sems.at[a])
            lc.start()
            local.append(lc)
            for k in range(1, N_DEV):
                px = (x + (k >> 2)) % 2
                py = (y + ((k >> 1) & 1)) % 2
                pc = (c + (k & 1)) % 2
                pid = 4 * px + 2 * py + pc
                cp = pltpu.make_async_remote_copy(
                    src_ref=ins[a] if gather else ins[a].at[pid], dst_ref=outs[a].at[me],
                    send_sem=send_sems.at[a * N_DEV + k], recv_sem=recv_sems.at[a * N_DEV + k],
                    device_id=(px, py, pc), device_id_type=pl.DeviceIdType.MESH)
                cp.start()
                rc = pltpu.make_async_remote_copy(
                    src_ref=ins[a] if gather else ins[a].at[pid], dst_ref=outs[a].at[pid],
                    send_sem=send_sems.at[a * N_DEV + k], recv_sem=recv_sems.at[a * N_DEV + k],
                    device_id=(px, py, pc), device_id_type=pl.DeviceIdType.MESH)
                remote.append((cp, rc))
        for cp, rc in remote:
            rc.wait_recv()
        for cp, rc in remote:
            cp.wait_send()
        for lc in local:
            lc.wait()

    any_spec = pl.BlockSpec(memory_space=pl.ANY)
    return pl.pallas_call(body, name=name, in_specs=[any_spec] * n, out_specs=[any_spec] * n, out_shape=out_shape,
                          scratch_shapes=[pltpu.SemaphoreType.DMA((n * N_DEV,)), pltpu.SemaphoreType.DMA((n * N_DEV,)),
                                          pltpu.SemaphoreType.DMA((n,))],
                          compiler_params=pltpu.CompilerParams(has_side_effects=True))(*arrs)


def _adam_math(w, g, m, v):
    m = ADAM_B1 * m + (1.0 - ADAM_B1) * g
    v = ADAM_B2 * v + (1.0 - ADAM_B2) * (g * g)
    m_hat = m / (1.0 - ADAM_B1 ** ADAM_STEP)
    v_hat = v / (1.0 - ADAM_B2 ** ADAM_STEP)
    delta = -ADAM_LR * (m_hat / (jnp.sqrt(v_hat) + ADAM_EPS) + ADAM_WD * w)
    return delta, m, v


def _adamw(parts, w, m, v, *, name, layer=None):
    S, R, C = parts.shape
    tr = R
    for cand in (256, 128, 64, 32, 16, 8):
        if R % cand == 0 and S * cand * C * 4 <= 4 * 2 ** 20:
            tr = cand
            break

    def body(p_ref, w_ref, m_ref, v_ref, g_ref, d_ref, nm_ref, nv_ref):
        g = p_ref[0].astype(F32)
        for s in range(1, S):
            g = g + p_ref[s].astype(F32)
        delta, nm, nv = _adam_math(w_ref[...], g, m_ref[...], v_ref[...])
        g_ref[...] = g
        d_ref[...] = delta
        nm_ref[...] = nm
        nv_ref[...] = nv

    if layer is None:
        wspec = pl.BlockSpec((tr, C), lambda i: (i, 0))
    else:
        wspec = pl.BlockSpec((None, tr, C), lambda i: (layer, i, 0))
    ospec = pl.BlockSpec((tr, C), lambda i: (i, 0))
    osh = jax.ShapeDtypeStruct((R, C), F32)
    return pl.pallas_call(body, name=name, grid=(R // tr,),
                          in_specs=[pl.BlockSpec((S, tr, C), lambda i: (0, i, 0)), wspec, wspec, wspec],
                          out_specs=[ospec] * 4, out_shape=[osh] * 4, compiler_params=_params())(parts, w, m, v)


def _sum_parts(parts, *, name):
    S, R, C = parts.shape

    def body(p_ref, o_ref):
        g = p_ref[0]
        for s in range(1, S):
            g = g + p_ref[s]
        o_ref[...] = g

    return pl.pallas_call(body, name=name, out_shape=jax.ShapeDtypeStruct((R, C), F32),
                          compiler_params=_params())(parts)


def _pack_rows(arrays):
    pieces, layout, row = [], [], 0
    for a in arrays:
        flat = a.reshape(-1).astype(F32)
        rows = -(-flat.shape[0] // (SUBLANES * LANES)) * SUBLANES
        flat = jnp.pad(flat, (0, rows * LANES - flat.shape[0]))
        pieces.append(flat.reshape(rows, LANES))
        layout.append((row, rows, a.shape))
        row += rows
    return jnp.concatenate(pieces, axis=0), layout


def _unpack_rows(packed, layout):
    out = []
    for row, rows, shape in layout:
        size = int(np.prod(shape))
        out.append(packed[row:row + rows].reshape(-1)[:size].reshape(shape))
    return out


def _ffn_fwd(h, w_in, w_out, cw, ln_g, ln_b, tag):
    h, hb = h
    u, act = _ffn_up(hb, w_in, cw, name=f"ffn_up_{tag}")
    hn, hnb, xh, rs = _mm_nn(act, w_out, res=h, res_scale=ALPHA, ln=(ln_g, ln_b), name=f"ffn_down_{tag}")
    return (hn, hnb), xh, rs, u


def _ffn_bwd(dy, hb, u, w_in, w_out, cw, tag):
    du, dw_out, dcw = _ffn_gate_bwd(dy, u, w_out, cw, name=f"ffn_gate_bwd_{tag}")
    du = du.reshape((-1,) + du.shape[2:])
    dw_in = _mm_tn(hb, du, n_map=_pair_map, name=f"ffn_dwin_{tag}")
    dh = _mm_nt(du, w_in, n_map=_pair_map, res=dy, res_scale=ALPHA, name=f"ffn_dh_{tag}")
    dcw = dcw.transpose(1, 0, 2, 3).reshape((-1,) + dcw.shape[2:])
    return dh, dw_in, dw_out, dcw


def kernel(x, hgrn_w_in, hgrn_lb_logits, hgrn_gnorm_w, hgrn_w_out, swa_w_q, swa_sinks, swa_w_out, shared_w_kv, rel_bias, ffn_w_in, ffn_conv_w, ffn_conv_b, ffn_w_out, ln_mix_g, ln_mix_b, ln_ffn_g, ln_ffn_b, loss_target, m_hgrn_w_in, m_hgrn_lb_logits, m_hgrn_gnorm_w, m_hgrn_w_out, m_swa_w_q, m_swa_sinks, m_swa_w_out, m_shared_w_kv, m_rel_bias, m_ffn_w_in, m_ffn_conv_w, m_ffn_conv_b, m_ffn_w_out, m_ln_mix_g, m_ln_mix_b, m_ln_ffn_g, m_ln_ffn_b, v_hgrn_w_in, v_hgrn_lb_logits, v_hgrn_gnorm_w, v_hgrn_w_out, v_swa_w_q, v_swa_sinks, v_swa_w_out, v_shared_w_kv, v_rel_bias, v_ffn_w_in, v_ffn_conv_w, v_ffn_conv_b, v_ffn_w_out, v_ln_mix_g, v_ln_mix_b, v_ln_ffn_g, v_ln_ffn_b):
    T = x.shape[1]
    D = D_MODEL
    W = SW_WINDOW
    fb = ffn_w_in.shape[2]
    me = 4 * lax.axis_index("x") + 2 * lax.axis_index("y") + lax.axis_index("c")

    small_fwd, small_fwd_layout = _pack_rows([hgrn_lb_logits, ffn_conv_w])
    big = [hgrn_w_in[0], hgrn_w_out[0], swa_w_q[0], swa_w_out[0], shared_w_kv,
           ffn_w_in[0], ffn_w_in[1], ffn_w_out[0], ffn_w_out[1]]
    gathered = _exchange([b.astype(MXU) for b in big] + [small_fwd], ["gather"] * (len(big) + 1),
                         name="gather_weights")
    w_hin = gathered[0][None]
    w_hout = gathered[1].reshape(1, 1, D, D)
    w_q = gathered[2].reshape(1, 1, D, D)
    w_o = gathered[3].reshape(1, 1, D, D)
    w_kv = gathered[4].reshape(1, 1, D, 2 * SW_KV_HEADS * SW_HEAD_DIM)
    w_fin = [gathered[5][None], gathered[6][None]]
    w_fout = [gathered[7].reshape(4, 1, 2 * ffn_w_out.shape[1], D), gathered[8].reshape(4, 1, 2 * ffn_w_out.shape[1], D)]
    small_all = gathered[9]
    (lb_row, lb_rows, _), (cw_row, cw_rows, _) = small_fwd_layout
    lbl = small_all[:, lb_row:lb_row + 2, :].transpose(1, 0, 2).reshape(2, D)
    conv_w_all = small_all[:, cw_row:cw_row + cw_rows, :].reshape(N_DEV, -1)[:, :DEPTH * 3 * fb]
    conv_w_all = conv_w_all.reshape(N_DEV, DEPTH, 3, fb).transpose(1, 0, 2, 3)
    conv_b_all = ffn_conv_b.reshape(DEPTH, N_DEV, 1, fb)
    no_pad = ((0, 0), (0, 0))
    cw = (jnp.pad(conv_w_all, no_pad + ((0, SUBLANES - 3), (0, 0)))
          + jnp.pad(conv_b_all, no_pad + ((3, SUBLANES - 4), (0, 0))))

    row = lambda a, l: a[l:l + 1]

    z = _mm_nn(x, w_hin, name="hgrn_in")
    og, states = _hgrn_fwd(z, lbl, hgrn_gnorm_w, name="hgrn_rec")
    h1, h1b, xh1, rs1 = _mm_nn(og, w_hout, res=x, res_scale=ALPHA, ln=(row(ln_mix_g, 0), row(ln_mix_b, 0)),
                               name="hgrn_out")
    (h2, h2b), xh2, rs2, u0 = _ffn_fwd((h1, h1b), w_fin[0], w_fout[0], cw[0], row(ln_ffn_g, 0), row(ln_ffn_b, 0), "l0")
    kv = _mm_nn(h2b, w_kv, name="swa_kv")
    q = _mm_nn(h2b, w_q, name="swa_q")
    onehot = _bucket_onehot()
    bias = _bias_expand(rel_bias.T, jnp.asarray(onehot.T, jnp.bfloat16), name="swa_bias").reshape(SW_Q_HEADS, W, 2 * W)
    ao, lse = _swa_fwd(q, kv, bias, swa_sinks, name="swa_attn")
    h3, h3b, xh3, rs3 = _mm_nn(ao, w_o, res=h2, res_scale=ALPHA, ln=(row(ln_mix_g, 1), row(ln_mix_b, 1)),
                               name="swa_out")
    (h4, _), xh4, rs4, u1 = _ffn_fwd((h3, h3b), w_fin[1], w_fout[1], cw[1], row(ln_ffn_g, 1), row(ln_ffn_b, 1), "l1")
    dh4, loss_tile = _loss_head(h4, loss_target, name="loss_head")

    dy4, dg_f1, db_f1 = _ln_bwd(dh4, xh4, rs4, row(ln_ffn_g, 1), name="ln_ffn1_bwd")
    dh3, dw_fin1, dw_fout1, dcw1 = _ffn_bwd(dy4, h3b, u1, w_fin[1], w_fout[1], cw[1], "l1")
    dy3, dg_m1, db_m1 = _ln_bwd(dh3, xh3, rs3, row(ln_mix_g, 1), name="ln_mix1_bwd")
    dw_o = _mm_tn(ao, dy3, name="swa_dwo")
    dao = _mm_nt(dy3, w_o, name="swa_dao")
    dq, dkv, dbias, dsinks = _swa_bwd(q, kv, ao, lse, dao, bias, swa_sinks, name="swa_attn_bwd")
    drel_t = _bias_reduce(dbias.reshape(SW_Q_HEADS, W * 2 * W), jnp.asarray(onehot, jnp.bfloat16), name="swa_dbias")
    dw_q = _mm_tn(h2b, dq, name="swa_dwq")
    dw_kv = _mm_tn(h2b, dkv, name="swa_dwkv")
    dh2 = _mm_nt(dq, w_q, res=dy3, res_scale=ALPHA, name="swa_dh_q")
    dh2 = _mm_nt(dkv, w_kv, res=dh2, res_scale=1.0, name="swa_dh_kv")
    dy2, dg_f0, db_f0 = _ln_bwd(dh2, xh2, rs2, row(ln_ffn_g, 0), name="ln_ffn0_bwd")
    dh1, dw_fin0, dw_fout0, dcw0 = _ffn_bwd(dy2, h1b, u0, w_fin[0], w_fout[0], cw[0], "l0")
    dy1, dg_m0, db_m0 = _ln_bwd(dh1, xh1, rs1, row(ln_mix_g, 0), name="ln_mix0_bwd")
    dw_hout = _mm_tn(og, dy1, name="hgrn_dwout")
    dog = _mm_nt(dy1, w_hout, name="hgrn_dog")
    dz, dlb, dgw = _hgrn_bwd(z, dog, states, lbl, hgrn_gnorm_w, name="hgrn_rec_bwd")
    dw_hin = _mm_tn(x, dz, name="hgrn_dwin")
    dx = _mm_nt(dz, w_hin, res=dy1, res_scale=ALPHA, name="hgrn_dx")

    p0 = _sigmoid(lbl[0:1] - lbl[1:2])
    dl0 = dlb * p0 * (1.0 - p0)
    d_lbl = dl0 * jnp.array([[1.0], [-1.0]], F32)
    dcw = jnp.stack([dcw0, dcw1], axis=0)
    d_conv_w = dcw[:, :, 0:3, :]
    d_conv_b = dcw[:, :, 3, :].reshape(DEPTH, N_DEV * fb)
    first_row = lax.broadcasted_iota(jnp.int32, (DEPTH, D), 0) == 0
    two_rows = lambda a, b: jnp.where(first_row, a, b)
    d_ln_mix_g = two_rows(dg_m0, dg_m1)
    d_ln_mix_b = two_rows(db_m0, db_m1)
    d_ln_ffn_g = two_rows(dg_f0, dg_f1)
    d_ln_ffn_b = two_rows(db_f0, db_f1)
    small_grads, small_layout = _pack_rows([d_lbl, d_conv_w, dgw, dsinks, drel_t.T, d_conv_b, d_ln_mix_g, d_ln_mix_b,
                                            d_ln_ffn_g, d_ln_ffn_b, loss_tile[0:1, 0:1]])

    big_grads = [dw_hin.reshape(N_DEV, D, -1), dw_hout.reshape(N_DEV, D // N_DEV, D), dw_q.reshape(N_DEV, D // N_DEV, D),
                 dw_o.reshape(N_DEV, D // N_DEV, D), dw_kv.reshape(N_DEV, D // N_DEV, -1),
                 dw_fin0.reshape(N_DEV, D, fb), dw_fin1.reshape(N_DEV, D, fb),
                 dw_fout0.reshape(N_DEV, -1, D), dw_fout1.reshape(N_DEV, -1, D)]
    received = _exchange(big_grads + [small_grads], ["scatter"] * len(big_grads) + ["gather"], name="exchange_grads")

    outs = {}

    def put(name_, res):
        outs["grad_" + name_], outs["delta_" + name_], outs["new_m_" + name_], outs["new_v_" + name_] = res

    def big_update(name_, parts, w, m, v):
        shp = w.shape
        if w.ndim == 3 and shp[0] == 1:
            r = _adamw(parts, w[0], m[0], v[0], name="adamw_" + name_)
            put(name_, [a.reshape(shp) for a in r])
        else:
            r = _adamw(parts, w, m, v, name="adamw_" + name_)
            put(name_, r)

    big_update("hgrn_w_in", received[0], hgrn_w_in, m_hgrn_w_in, v_hgrn_w_in)
    big_update("hgrn_w_out", received[1], hgrn_w_out, m_hgrn_w_out, v_hgrn_w_out)
    big_update("swa_w_q", received[2], swa_w_q, m_swa_w_q, v_swa_w_q)
    big_update("swa_w_out", received[3], swa_w_out, m_swa_w_out, v_swa_w_out)
    big_update("shared_w_kv", received[4], shared_w_kv, m_shared_w_kv, v_shared_w_kv)
    for name_, idx, w, m, v in (("ffn_w_in", 5, ffn_w_in, m_ffn_w_in, v_ffn_w_in),
                                ("ffn_w_out", 7, ffn_w_out, m_ffn_w_out, v_ffn_w_out)):
        per_layer = [_adamw(received[idx + l], w, m, v, layer=l, name=f"adamw_{name_}_{l}") for l in range(DEPTH)]
        put(name_, [jnp.stack([per_layer[0][i], per_layer[1][i]], axis=0) for i in range(4)])

    small_sum = _sum_parts(received[9], name="sum_small_grads")
    (g_lbl, g_conv_w, g_gw, g_sinks, g_rel, g_conv_b, g_mix_g, g_mix_b, g_ffn_g, g_ffn_b,
     loss) = _unpack_rows(small_sum, small_layout)
    g_lbl_mine = lax.dynamic_slice_in_dim(g_lbl, me * (D // N_DEV), D // N_DEV, axis=1)
    g_conv_w_mine = lax.dynamic_index_in_dim(g_conv_w, me, axis=1, keepdims=False)
    small_names = ["hgrn_lb_logits", "ffn_conv_w", "hgrn_gnorm_w", "swa_sinks", "rel_bias", "ffn_conv_b",
                   "ln_mix_g", "ln_mix_b", "ln_ffn_g", "ln_ffn_b"]
    small_g = [g_lbl_mine, g_conv_w_mine, g_gw, g_sinks, g_rel, g_conv_b, g_mix_g, g_mix_b, g_ffn_g, g_ffn_b]
    small_w = [hgrn_lb_logits, ffn_conv_w, hgrn_gnorm_w, swa_sinks, rel_bias, ffn_conv_b, ln_mix_g, ln_mix_b,
               ln_ffn_g, ln_ffn_b]
    small_m = [m_hgrn_lb_logits, m_ffn_conv_w, m_hgrn_gnorm_w, m_swa_sinks, m_rel_bias, m_ffn_conv_b, m_ln_mix_g,
               m_ln_mix_b, m_ln_ffn_g, m_ln_ffn_b]
    small_v = [v_hgrn_lb_logits, v_ffn_conv_w, v_hgrn_gnorm_w, v_swa_sinks, v_rel_bias, v_ffn_conv_b, v_ln_mix_g,
               v_ln_mix_b, v_ln_ffn_g, v_ln_ffn_b]
    pg, lay = _pack_rows(small_g)
    pw, _ = _pack_rows(small_w)
    pm, _ = _pack_rows(small_m)
    pv, _ = _pack_rows(small_v)
    res = _adamw(pg[None], pw, pm, pv, name="adamw_small")
    unpacked = [_unpack_rows(r, lay) for r in res]
    for i, name_ in enumerate(small_names):
        put(name_, [unpacked[j][i] for j in range(4)])

    order = ["hgrn_w_in", "hgrn_lb_logits", "hgrn_gnorm_w", "hgrn_w_out", "swa_w_q", "swa_sinks", "swa_w_out",
             "shared_w_kv", "rel_bias", "ffn_w_in", "ffn_conv_w", "ffn_conv_b", "ffn_w_out", "ln_mix_g", "ln_mix_b",
             "ln_ffn_g", "ln_ffn_b"]
    result = [loss.reshape(()), dx]
    for kind in ("grad_", "delta_", "new_m_", "new_v_"):
        result += [outs[kind + n] for n in order]
    return tuple(result)
```

```python
import functools
import math

import numpy as np
import jax
import jax.numpy as jnp
from jax import lax
from jax.experimental import pallas as pl
from jax.experimental.pallas import tpu as pltpu

F32 = jnp.float32
MXU = jnp.bfloat16

N_DEV = 8
D_MODEL = 1024
DEPTH = 2
HG_HEADS = 8
HG_DIM = 128
HG_CHUNK = 64
SW_Q_HEADS = 16
SW_KV_HEADS = 4
SW_GROUP = 4
SW_HEAD_DIM = 64
SW_WINDOW = 128
REL_BUCKETS = 32
REL_MAX_DIST = 128
FFN_DIM = 2816
ALPHA = (2.0 * DEPTH) ** 0.25
LN_EPS = 1e-5
RMS_EPS = 1e-6
ADAM_LR = 0.001
ADAM_B1 = 0.9
ADAM_B2 = 0.999
ADAM_EPS = 1e-08
ADAM_WD = 0.01
ADAM_STEP = 10
EXP_CLAMP = 80.0
NEG_BIG = -1e30

SUBLANES = 8
LANES = 128
VMEM_LIMIT = 48 * 2 ** 20
TOKEN_TILE = 512
WIDE_TOKEN_TILE = 1024
GRAD_DTYPE = jnp.bfloat16
HALO = 16


def _params(**kw):
    return pltpu.CompilerParams(vmem_limit_bytes=VMEM_LIMIT, **kw)


def _sigmoid(x):
    return 1.0 / (1.0 + jnp.exp(-x))


def _dot(a, b):
    return jnp.dot(a.astype(MXU), b.astype(MXU), preferred_element_type=F32)


def _dot_nt(a, b):
    return lax.dot_general(a.astype(MXU), b.astype(MXU), (((1,), (1,)), ((), ())), preferred_element_type=F32)


def _dot_tn(a, b):
    return lax.dot_general(a.astype(MXU), b.astype(MXU), (((0,), (0,)), ((), ())), preferred_element_type=F32)


def _trunc_bf16(x):
    bits = lax.bitcast_convert_type(x, jnp.int32)
    return lax.bitcast_convert_type(bits & jnp.int32(-65536), F32)


def _split3(x):
    hi = _trunc_bf16(x)
    r = x - hi
    mid = _trunc_bf16(r)
    lo = r - mid
    return hi.astype(jnp.bfloat16), mid.astype(jnp.bfloat16), lo.astype(jnp.bfloat16)


def _dot_hp(a, b, contract):
    def halves(x):
        hi = _trunc_bf16(x)
        return hi.astype(jnp.bfloat16), (x - hi).astype(jnp.bfloat16)

    ah, al = halves(a)
    bh, bl = halves(b)
    d = lambda p, q: lax.dot_general(p, q, (contract, ((), ())), preferred_element_type=F32)
    return d(ah, bh) + d(ah, bl) + d(al, bh)


def _exact_dot(m01, x):
    hi, mid, lo = _split3(x)
    d = lambda p: jnp.dot(m01, p, preferred_element_type=F32)
    return d(hi) + d(mid) + d(lo)


def _exact_dot_r(x, m01):
    hi, mid, lo = _split3(x)
    d = lambda p: jnp.dot(p, m01, preferred_element_type=F32)
    return d(hi) + d(mid) + d(lo)


def _mm_nn(a, w, *, name, res=None, res_scale=1.0, ln=None, out_dtype=F32, tm=None):
    nbk, T, kw = a.shape
    _, nbn, _, nw = w.shape
    tm = min(tm or TOKEN_TILE, T)
    has_res = res is not None
    assert ln is None or nbn == 1

    def body(*refs):
        refs = list(refs)
        a_ref, w_ref = refs[:2]
        pos = 2
        res_ref = None
        if has_res:
            res_ref = refs[pos]
            pos += 1
        if ln is not None:
            g_ref, b_ref = refs[pos:pos + 2]
            pos += 2
        o_ref = refs[pos]
        pos += 1
        if ln is not None:
            ob_ref, xh_ref, rs_ref = refs[pos:pos + 3]
        for n in range(nbn):
            y = _dot(a_ref[0], w_ref[0, n])
            for k in range(1, nbk):
                y = y + _dot(a_ref[k], w_ref[k, n])
            if has_res:
                y = y + res_scale * res_ref[n].astype(F32)
            if ln is None:
                o_ref[n] = y.astype(o_ref.dtype)
            else:
                mu = jnp.mean(y, axis=-1, keepdims=True)
                yc = y - mu
                var = jnp.mean(yc * yc, axis=-1, keepdims=True)
                rstd = lax.rsqrt(var + LN_EPS)
                xh = yc * rstd
                xh_ref[n] = xh
                rs_ref[...] = rstd
                h = xh * g_ref[...] + b_ref[...]
                o_ref[n] = h
                ob_ref[n] = h.astype(ob_ref.dtype)

    in_specs = [pl.BlockSpec((nbk, tm, kw), lambda i: (0, i, 0)),
                pl.BlockSpec((nbk, nbn, kw, nw), lambda i: (0, 0, 0, 0))]
    args = [a, w]
    if has_res:
        in_specs.append(pl.BlockSpec((nbn, tm, nw), lambda i: (0, i, 0)))
        args.append(res)
    if ln is not None:
        in_specs += [pl.BlockSpec((1, nw), lambda i: (0, 0))] * 2
        args += list(ln)
    out_spec = pl.BlockSpec((nbn, tm, nw), lambda i: (0, i, 0))
    out_shape = jax.ShapeDtypeStruct((nbn, T, nw), out_dtype)
    if ln is not None:
        out_specs = [out_spec, out_spec, out_spec, pl.BlockSpec((tm, 1), lambda i: (i, 0))]
        out_shape = [out_shape, jax.ShapeDtypeStruct((nbn, T, nw), MXU), jax.ShapeDtypeStruct((nbn, T, nw), F32),
                     jax.ShapeDtypeStruct((T, 1), F32)]
    else:
        out_specs = out_spec
    return pl.pallas_call(body, name=name, grid=(T // tm,), in_specs=in_specs, out_specs=out_specs,
                          out_shape=out_shape, compiler_params=_params())(*args)


def _same(n):
    return n


def _mm_nt(dy, w, *, name, res=None, res_scale=1.0, out_dtype=F32, tm=None, n_map=_same, behind=()):
    nbn, T, nw = dy.shape
    nbk, _, kw, _ = w.shape
    tm = min(tm or WIDE_TOKEN_TILE, T)
    has_res = res is not None

    def body(*refs):
        refs = list(refs)
        dy_ref, w_ref = refs[:2]
        pos = 2
        res_ref = None
        if has_res:
            res_ref = refs[pos]
            pos += 1
        pos += len(behind)
        o_ref = refs[pos]
        pos += 1
        acc_ref = refs[pos] if nbn > 1 else None
        n = pl.program_id(2)
        part = _dot_nt(dy_ref[...], w_ref[...])

        def finish(acc):
            y = acc
            if has_res:
                y = y + res_scale * res_ref[...].astype(F32)
            o_ref[...] = y.astype(o_ref.dtype)

        if nbn == 1:
            finish(part)
        else:
            @pl.when(n == 0)
            def _():
                acc_ref[...] = part

            @pl.when(n > 0)
            def _():
                acc_ref[...] += part

            @pl.when(n == nbn - 1)
            def _():
                finish(acc_ref[...])

    in_specs = [pl.BlockSpec((None, tm, nw), lambda i, k, n: (n, i, 0)),
                pl.BlockSpec((None, None, kw, nw), lambda i, k, n: (k, n_map(n), 0, 0))]
    args = [dy, w]
    if has_res:
        in_specs.append(pl.BlockSpec((None, tm, kw), lambda i, k, n: (k, i, 0)))
        args.append(res)
    in_specs += [pl.BlockSpec(memory_space=pl.ANY)] * len(behind)
    args += list(behind)
    scratch = [pltpu.VMEM((tm, kw), F32)] if nbn > 1 else []
    return pl.pallas_call(body, name=name, grid=(T // tm, nbk, nbn), in_specs=in_specs,
                          out_specs=pl.BlockSpec((None, tm, kw), lambda i, k, n: (k, i, 0)),
                          out_shape=jax.ShapeDtypeStruct((nbk, T, kw), out_dtype), scratch_shapes=scratch,
                          compiler_params=_params())(*args)


def _mm_tn(a, dy, *, name, tm=None, n_map=_same):
    nbk, T, kw = a.shape
    nbn, _, nw = dy.shape
    tm = min(tm or WIDE_TOKEN_TILE, T)
    nt = T // tm

    def body(a_ref, dy_ref, o_ref, acc_ref):
        i = pl.program_id(2)
        part = _dot_tn(a_ref[...], dy_ref[...])

        @pl.when(i == 0)
        def _():
            acc_ref[...] = part

        @pl.when(i > 0)
        def _():
            acc_ref[...] += part

        @pl.when(i == nt - 1)
        def _():
            o_ref[...] = acc_ref[...].astype(o_ref.dtype)

    return pl.pallas_call(body, name=name, grid=(nbk, nbn, nt),
                          in_specs=[pl.BlockSpec((None, tm, kw), lambda k, n, i: (k, i, 0)),
                                    pl.BlockSpec((None, tm, nw), lambda k, n, i: (n, i, 0))],
                          out_specs=pl.BlockSpec((None, None, kw, nw), lambda k, n, i: (k, n_map(n), 0, 0)),
                          out_shape=jax.ShapeDtypeStruct((nbk, nbn, kw, nw), GRAD_DTYPE),
                          scratch_shapes=[pltpu.VMEM((kw, nw), F32)],
                          compiler_params=_params())(a, dy)


def _ln_bwd(dh, xhat, rstd, g, *, name, tm=None, behind=()):
    _, T, D = dh.shape
    tm = min(tm or TOKEN_TILE, T)

    def body(dh_ref, xh_ref, rs_ref, g_ref, *rest):
        dy_ref, dg_ref, db_ref = rest[-3:]
        i = pl.program_id(0)
        dhv = dh_ref[...]
        xh = xh_ref[...]
        dxh = dhv * g_ref[...]
        m1 = jnp.mean(dxh, axis=-1, keepdims=True)
        m2 = jnp.mean(dxh * xh, axis=-1, keepdims=True)
        dy_ref[...] = rs_ref[...] * (dxh - m1 - xh * m2)
        dg = jnp.sum(dhv * xh, axis=0, keepdims=True)
        db = jnp.sum(dhv, axis=0, keepdims=True)

        @pl.when(i == 0)
        def _():
            dg_ref[...] = dg
            db_ref[...] = db

        @pl.when(i > 0)
        def _():
            dg_ref[...] += dg
            db_ref[...] += db

    tok = pl.BlockSpec((None, tm, D), lambda i: (0, i, 0))
    vec = pl.BlockSpec((1, D), lambda i: (0, 0))
    return pl.pallas_call(body, name=name, grid=(T // tm,),
                          in_specs=[tok, tok, pl.BlockSpec((tm, 1), lambda i: (i, 0)), vec]
                          + [pl.BlockSpec(memory_space=pl.ANY)] * len(behind),
                          out_specs=[tok, vec, vec],
                          out_shape=[jax.ShapeDtypeStruct((1, T, D), F32), jax.ShapeDtypeStruct((1, D), F32),
                                     jax.ShapeDtypeStruct((1, D), F32)],
                          compiler_params=_params())(dh, xhat, rstd, g, *behind)


def _loss_head(h, tgt, *, name, tm=None):
    _, T, D = h.shape
    tm = min(tm or TOKEN_TILE, T)

    def body(h_ref, t_ref, dh_ref, loss_ref):
        i = pl.program_id(0)
        err = h_ref[...] - t_ref[...]
        dh_ref[...] = err / D
        part = 0.5 * jnp.sum(jnp.mean(err * err, axis=-1, keepdims=True), axis=0, keepdims=True)

        @pl.when(i == 0)
        def _():
            loss_ref[...] = jnp.zeros_like(loss_ref) + part

        @pl.when(i > 0)
        def _():
            loss_ref[...] += part

    tok = pl.BlockSpec((None, tm, D), lambda i: (0, i, 0))
    return pl.pallas_call(body, name=name, grid=(T // tm,), in_specs=[tok, tok],
                          out_specs=[tok, pl.BlockSpec((SUBLANES, LANES), lambda i: (0, 0))],
                          out_shape=[jax.ShapeDtypeStruct((1, T, D), F32),
                                     jax.ShapeDtypeStruct((SUBLANES, LANES), F32)],
                          compiler_params=_params())(h, tgt)


def _shift_rows(ext, k, n, halo):
    if k == 0:
        return ext[halo:halo + n]
    return pltpu.roll(ext, k, axis=0)[halo:halo + n]


def _conv_rows(ext, cw_ref, n, halo):
    return (cw_ref[0:1, :] * _shift_rows(ext, 2, n, halo) + cw_ref[1:2, :] * _shift_rows(ext, 1, n, halo)
            + cw_ref[2:3, :] * ext[halo:halo + n] + cw_ref[3:4, :])


def _pair_map(n):
    return n // 2 + 4 * (n % 2)


def _ffn_up(hb, w_in, cw, *, name, tm=None):
    _, T, D = hb.shape
    _, nb, _, fb = w_in.shape
    half = nb // 2
    tm = min(tm or TOKEN_TILE, T)

    def body(h_ref, wa_ref, wb_ref, cwa_ref, cwb_ref, u_ref, act_ref, carry):
        @pl.when(pl.program_id(1) == 0)
        def _():
            carry[...] = jnp.zeros_like(carry)

        h = h_ref[...]
        conv = []
        for s, (w_ref, cw_ref) in enumerate(((wa_ref, cwa_ref), (wb_ref, cwb_ref))):
            u = _dot(h, w_ref[...]).astype(u_ref.dtype)
            u_ref[s] = u
            uf = u.astype(F32)
            ext = jnp.concatenate([carry[s], uf], axis=0)
            conv.append(_conv_rows(ext, cw_ref, tm, SUBLANES))
            carry[s] = uf[tm - SUBLANES:tm]
        a, b = conv
        act_ref[...] = (a * _sigmoid(a) * b).astype(act_ref.dtype)

    wspec = lambda off: pl.BlockSpec((None, None, D, fb), lambda p, i: (0, p + off, 0, 0))
    cws = lambda off: pl.BlockSpec((None, SUBLANES, fb), lambda p, i: (p + off, 0, 0))
    return pl.pallas_call(body, name=name, grid=(half, T // tm),
                          in_specs=[pl.BlockSpec((None, tm, D), lambda p, i: (0, i, 0)), wspec(0), wspec(half),
                                    cws(0), cws(half)],
                          out_specs=[pl.BlockSpec((None, 2, tm, fb), lambda p, i: (p, 0, i, 0)),
                                     pl.BlockSpec((None, tm, fb), lambda p, i: (p, i, 0))],
                          out_shape=[jax.ShapeDtypeStruct((half, 2, T, fb), MXU),
                                     jax.ShapeDtypeStruct((half, T, fb), MXU)],
                          scratch_shapes=[pltpu.VMEM((2, SUBLANES, fb), F32)],
                          compiler_params=_params())(hb, w_in, w_in, cw, cw)


def _ffn_gate_bwd(dy, u, w_out, cw, *, name, tm=None):
    _, T, D = dy.shape
    half, _, _, fb = u.shape
    tm = min(tm or TOKEN_TILE, T)
    nt = T // tm
    rh = tm // HALO

    def body(dy_ref, u_ref, up_ref, w_ref, cwa_ref, cwb_ref, du_ref, dwo_ref, dcw_ref, carry, acc):
        i = pl.program_id(1)
        tile = nt - 1 - i

        @pl.when(i == 0)
        def _():
            carry[...] = jnp.zeros_like(carry)
            acc[...] = jnp.zeros_like(acc)
            dcw_ref[...] = jnp.zeros_like(dcw_ref)

        dyv = dy_ref[...]
        dact = _dot_nt(dyv, w_ref[...])
        exts, conv = [], []
        for s, cw_ref in enumerate((cwa_ref, cwb_ref)):
            prev = jnp.where(tile == 0, 0.0, up_ref[s].astype(F32))
            ext = jnp.concatenate([prev, u_ref[s].astype(F32)], axis=0)
            exts.append(ext)
            conv.append(_conv_rows(ext, cw_ref, tm, HALO))
        a, b = conv
        sa = _sigmoid(a)
        silu = a * sa
        acc[...] += _dot_tn(silu * b, dyv)
        dcs = (dact * b * (sa * (1.0 + a * (1.0 - sa))), dact * silu)
        m = tm + SUBLANES
        rows = lax.broadcasted_iota(jnp.int32, (SUBLANES, fb), 0)
        for s, cw_ref in enumerate((cwa_ref, cwb_ref)):
            dc = dcs[s]
            nxt = jnp.concatenate([dc, carry[s]], axis=0)
            du = (cw_ref[2:3, :] * dc + cw_ref[1:2, :] * pltpu.roll(nxt, m - 1, axis=0)[:tm]
                  + cw_ref[0:1, :] * pltpu.roll(nxt, m - 2, axis=0)[:tm])
            du_ref[s] = du.astype(du_ref.dtype)
            carry[s] = dc[0:SUBLANES]
            g0 = jnp.sum(dc * _shift_rows(exts[s], 2, tm, HALO), axis=0, keepdims=True)
            g1 = jnp.sum(dc * _shift_rows(exts[s], 1, tm, HALO), axis=0, keepdims=True)
            g2 = jnp.sum(dc * exts[s][HALO:HALO + tm], axis=0, keepdims=True)
            g3 = jnp.sum(dc, axis=0, keepdims=True)
            dcw_ref[s] += jnp.where(rows == 0, g0, jnp.where(rows == 1, g1, jnp.where(rows == 2, g2,
                                                                                jnp.where(rows == 3, g3, 0.0))))

        @pl.when(i == nt - 1)
        def _():
            dwo_ref[...] = acc[...].astype(dwo_ref.dtype)

    rev = lambda i: nt - 1 - i
    cws = lambda off: pl.BlockSpec((None, SUBLANES, fb), lambda p, i: (p + off, 0, 0))
    return pl.pallas_call(body, name=name, grid=(half, nt),
                          in_specs=[pl.BlockSpec((None, tm, D), lambda p, i: (0, rev(i), 0)),
                                    pl.BlockSpec((None, 2, tm, fb), lambda p, i: (p, 0, rev(i), 0)),
                                    pl.BlockSpec((None, 2, HALO, fb),
                                                 lambda p, i: (p, 0, jnp.maximum(rev(i) * rh - 1, 0), 0)),
                                    pl.BlockSpec((None, None, fb, D), lambda p, i: (p, 0, 0, 0)),
                                    cws(0), cws(half)],
                          out_specs=[pl.BlockSpec((None, 2, tm, fb), lambda p, i: (p, 0, rev(i), 0)),
                                     pl.BlockSpec((None, None, fb, D), lambda p, i: (p, 0, 0, 0)),
                                     pl.BlockSpec((None, 2, SUBLANES, fb), lambda p, i: (p, 0, 0, 0))],
                          out_shape=[jax.ShapeDtypeStruct((half, 2, T, fb), MXU),
                                     jax.ShapeDtypeStruct((half, 1, fb, D), GRAD_DTYPE),
                                     jax.ShapeDtypeStruct((half, 2, SUBLANES, fb), F32)],
                          scratch_shapes=[pltpu.VMEM((2, SUBLANES, fb), F32), pltpu.VMEM((fb, D), F32)],
                          compiler_params=_params())(dy, u, u, w_out, cw, cw)


def _tri(n, lower):
    r = lax.broadcasted_iota(jnp.int32, (n, n), 0)
    c = lax.broadcasted_iota(jnp.int32, (n, n), 1)
    return (r >= c) if lower else (r <= c)


def _hgrn_gates(zq, zf, lb):
    sq = _sigmoid(zq)
    sf = _sigmoid(zf)
    fg = lb + (1.0 - lb) * sf
    return zq * sq, sq, sf, fg, jnp.log(fg)


def _lb_of(lbl_ref, cols):
    return _sigmoid(lbl_ref[0:1, cols] - lbl_ref[1:2, cols])


def _ones_where(mask):
    return jnp.where(mask, 1.0, 0.0).astype(jnp.bfloat16)


def _hgrn_fwd(z, lbl, gw, *, name):
    _, T, zw = z.shape
    C = min(HG_CHUNK, T)
    nch = T // C
    hpb = zw // HG_DIM

    def body(z_ref, lbl_ref, gw_ref, og_ref, st_ref, s_scr, bc_scr):
        c = pl.program_id(0)

        @pl.when(c == 0)
        def _():
            s_scr[...] = jnp.zeros_like(s_scr)

        low = _tri(C, True)
        low01 = _ones_where(low)
        gwv = gw_ref[...]
        for blk in range(2):
            zq = z_ref[blk]
            zf = z_ref[2 + blk]
            qq, _, _, fg, lf = _hgrn_gates(zq, zf, _lb_of(lbl_ref, slice(blk * zw, (blk + 1) * zw)))
            kk = 1.0 - fg
            bc_scr[...] = _exact_dot(low01, lf)
            for hh in range(hpb):
                h = blk * hpb + hh
                cols = slice(hh * HG_DIM, (hh + 1) * HG_DIM)
                b = bc_scr[:, cols]
                q_h, k_h = qq[:, cols], kk[:, cols]
                v_h = z_ref[4 + blk, :, cols]
                g_h = z_ref[6 + blk, :, cols]
                bm = bc_scr[C // 2 - 1:C // 2, cols]
                bl = bc_scr[C - 1:C, cols]
                qt = q_h * jnp.exp(jnp.minimum(b - bm, EXP_CLAMP))
                kt = k_h * jnp.exp(jnp.minimum(bm - b, EXP_CLAMP))
                A = jnp.where(low, _dot_nt(qt, kt), 0.0)
                s0 = s_scr[h]
                st_ref[h] = s0
                o = _dot_nt(q_h * jnp.exp(b), s0) + _dot(A, v_h)
                s_scr[h] = s0 * jnp.exp(bl) + _dot_tn(v_h, k_h * jnp.exp(bl - b))
                r = lax.rsqrt(jnp.mean(o * o, axis=-1, keepdims=True) + RMS_EPS)
                og = o * r * gwv * (g_h * _sigmoid(g_h))
                og_ref[:, h * HG_DIM:(h + 1) * HG_DIM] = og.astype(og_ref.dtype)

    return pl.pallas_call(body, name=name, grid=(nch,),
                          in_specs=[pl.BlockSpec((8, C, zw), lambda c: (0, c, 0)),
                                    pl.BlockSpec((2, D_MODEL), lambda c: (0, 0)),
                                    pl.BlockSpec((1, HG_DIM), lambda c: (0, 0))],
                          out_specs=[pl.BlockSpec((None, C, D_MODEL), lambda c: (0, c, 0)),
                                     pl.BlockSpec((None, HG_HEADS, HG_DIM, HG_DIM), lambda c: (c, 0, 0, 0))],
                          out_shape=[jax.ShapeDtypeStruct((1, T, D_MODEL), MXU),
                                     jax.ShapeDtypeStruct((nch, HG_HEADS, HG_DIM, HG_DIM), F32)],
                          scratch_shapes=[pltpu.VMEM((HG_HEADS, HG_DIM, HG_DIM), F32), pltpu.VMEM((C, zw), F32)],
                          compiler_params=_params())(z, lbl, gw)


def _hgrn_bwd(z, dog, states, lbl, gw, *, name):
    _, T, zw = z.shape
    C = min(HG_CHUNK, T)
    nch = T // C
    hpb = zw // HG_DIM

    def body(z_ref, dog_ref, st0_ref, st1_ref, lbl_ref, gw_ref, dz_ref, dlb_ref, dgw_ref, d_scr, bc_scr):
        step = pl.program_id(0)

        @pl.when(step == 0)
        def _():
            d_scr[...] = jnp.zeros_like(d_scr)
            dlb_ref[...] = jnp.zeros_like(dlb_ref)
            dgw_ref[...] = jnp.zeros_like(dgw_ref)

        low = _tri(C, True)
        low01 = _ones_where(low)
        up01 = _ones_where(_tri(C, False))
        gwv = gw_ref[...]
        dgw_acc = jnp.zeros((1, HG_DIM), F32)
        for blk in range(2):
            zq = z_ref[blk]
            zf = z_ref[2 + blk]
            lbb = _lb_of(lbl_ref, slice(blk * zw, (blk + 1) * zw))
            qq, sq, sf, fg, lf = _hgrn_gates(zq, zf, lbb)
            kk = 1.0 - fg
            bc_scr[...] = _exact_dot(low01, lf)
            dlb_blk = []
            for hh in range(hpb):
                h = blk * hpb + hh
                cols = slice(hh * HG_DIM, (hh + 1) * HG_DIM)
                b = bc_scr[:, cols]
                q_h, k_h = qq[:, cols], kk[:, cols]
                v_h = z_ref[4 + blk, :, cols]
                g_h = z_ref[6 + blk, :, cols]
                bm = bc_scr[C // 2 - 1:C // 2, cols]
                bl = bc_scr[C - 1:C, cols]
                eq = jnp.exp(jnp.minimum(b - bm, EXP_CLAMP))
                ek = jnp.exp(jnp.minimum(bm - b, EXP_CLAMP))
                eb = jnp.exp(b)
                el = jnp.exp(bl - b)
                ebl = jnp.exp(bl)
                qt, kt, q0, kd = q_h * eq, k_h * ek, q_h * eb, k_h * el
                A = jnp.where(low, _dot_nt(qt, kt), 0.0)
                s0 = st0_ref[h]
                s1 = st1_ref[h]
                dt = d_scr[h]
                o = _dot_nt(q0, s0) + _dot(A, v_h)
                r = lax.rsqrt(jnp.mean(o * o, axis=-1, keepdims=True) + RMS_EPS)
                on = o * r
                sg = _sigmoid(g_h)
                dogh = dog_ref[:, h * HG_DIM:(h + 1) * HG_DIM].astype(F32)
                t1 = dogh * on
                dgw_acc = dgw_acc + jnp.sum(t1 * (g_h * sg), axis=0, keepdims=True)
                dg = t1 * gwv * (sg * (1.0 + g_h * (1.0 - sg)))
                don = dogh * gwv * (g_h * sg)
                do = r * (don - on * jnp.mean(don * on, axis=-1, keepdims=True))
                P = jnp.where(low, _dot_nt(do, v_h), 0.0)
                dqq = eb * _dot(do, s0) + eq * _dot_hp(P, kt, ((1,), (0,)))
                dkk = el * _dot(v_h, dt) + ek * _dot_hp(P, qt, ((0,), (0,)))
                dv = _dot_nt(kd, dt) + _dot_tn(A, do)
                d_scr[h] = dt * ebl + _dot_tn(do, q0)
                edge = jnp.sum(dt * s1, axis=0, keepdims=True)
                dlf = _exact_dot(up01, q_h * dqq - k_h * dkk) + edge
                fg_h = fg[:, cols]
                sf_h = sf[:, cols]
                sq_h = sq[:, cols]
                zq_h = zq[:, cols]
                dfg = dlf / fg_h - dkk
                lb_h = lbb[:, cols]
                dlb_blk.append(jnp.sum(dfg * (1.0 - sf_h), axis=0, keepdims=True))
                dz_ref[blk, :, cols] = dqq * (sq_h * (1.0 + zq_h * (1.0 - sq_h)))
                dz_ref[2 + blk, :, cols] = dfg * (1.0 - lb_h) * sf_h * (1.0 - sf_h)
                dz_ref[4 + blk, :, cols] = dv
                dz_ref[6 + blk, :, cols] = dg
            dlb_ref[:, blk * zw:(blk + 1) * zw] += jnp.concatenate(dlb_blk, axis=1)
        dgw_ref[...] += dgw_acc

    rev = lambda s: nch - 1 - s
    return pl.pallas_call(body, name=name, grid=(nch,),
                          in_specs=[pl.BlockSpec((8, C, zw), lambda s: (0, rev(s), 0)),
                                    pl.BlockSpec((None, C, D_MODEL), lambda s: (0, rev(s), 0)),
                                    pl.BlockSpec((None, HG_HEADS, HG_DIM, HG_DIM), lambda s: (rev(s), 0, 0, 0)),
                                    pl.BlockSpec((None, HG_HEADS, HG_DIM, HG_DIM),
                                                 lambda s: (jnp.minimum(rev(s) + 1, nch - 1), 0, 0, 0)),
                                    pl.BlockSpec((2, D_MODEL), lambda s: (0, 0)),
                                    pl.BlockSpec((1, HG_DIM), lambda s: (0, 0))],
                          out_specs=[pl.BlockSpec((8, C, zw), lambda s: (0, rev(s), 0)),
                                     pl.BlockSpec((1, D_MODEL), lambda s: (0, 0)),
                                     pl.BlockSpec((1, HG_DIM), lambda s: (0, 0))],
                          out_shape=[jax.ShapeDtypeStruct((8, T, zw), F32), jax.ShapeDtypeStruct((1, D_MODEL), F32),
                                     jax.ShapeDtypeStruct((1, HG_DIM), F32)],
                          scratch_shapes=[pltpu.VMEM((HG_HEADS, HG_DIM, HG_DIM), F32), pltpu.VMEM((C, zw), F32)],
                          compiler_params=_params())(z, dog, states, states, lbl, gw)


def _bucket_onehot():
    W = SW_WINDOW
    t = np.arange(W)[:, None] + W
    s = np.arange(2 * W)[None, :]
    dist = t - s
    exact = REL_BUCKETS // 2
    d = np.maximum(np.maximum(dist, 0), 1).astype(np.float32)
    log_b = exact + (np.log(d / np.float32(exact)) / np.float32(math.log(REL_MAX_DIST / exact))
                     * np.float32(REL_BUCKETS - exact)).astype(np.int32)
    bucket = np.where(np.maximum(dist, 0) < exact, np.maximum(dist, 0), np.minimum(log_b, REL_BUCKETS - 1))
    valid = (dist >= 0) & (dist < W)
    onehot = (bucket[..., None] == np.arange(REL_BUCKETS)) & valid[..., None]
    return onehot.reshape(W * 2 * W, REL_BUCKETS).astype(np.float32)


def _bias_expand(rel_t, onehot_t, *, name):
    hq, nbk = rel_t.shape
    n = onehot_t.shape[1]

    def body(r_ref, oh_ref, o_ref):
        o_ref[...] = _exact_dot_r(r_ref[...], oh_ref[...])

    return pl.pallas_call(body, name=name, out_shape=jax.ShapeDtypeStruct((hq, n), F32),
                          compiler_params=_params())(rel_t, onehot_t)


def _bias_reduce(dbias, onehot, *, name):
    hq = dbias.shape[0]
    nbk = onehot.shape[1]

    def body(d_ref, oh_ref, o_ref):
        o_ref[...] = _exact_dot_r(d_ref[...], oh_ref[...])

    return pl.pallas_call(body, name=name, out_shape=jax.ShapeDtypeStruct((hq, nbk), F32),
                          compiler_params=_params())(dbias, onehot)


def _swa_mask(j):
    W = SW_WINDOW
    t = lax.broadcasted_iota(jnp.int32, (W, 2 * W), 0) + W
    s = lax.broadcasted_iota(jnp.int32, (W, 2 * W), 1)
    dist = t - s
    band = (dist >= 0) & (dist < W)
    m = band & ((j > 0) | (s >= W))
    return jnp.concatenate([m] * SW_GROUP, axis=0)


def _half_mask(rows, half):
    lane = lax.broadcasted_iota(jnp.int32, (rows, LANES), 1)
    return (lane >= SW_HEAD_DIM) if half else (lane < SW_HEAD_DIM)


def _swa_head(ref, col0, head, to_half):
    slab, half = head // 2, head % 2
    x = ref[:, col0 + slab * LANES:col0 + (slab + 1) * LANES]
    x = jnp.where(_half_mask(x.shape[0], half), x, 0.0)
    return x if half == to_half else pltpu.roll(x, SW_HEAD_DIM, axis=1)


def _swa_stack(ref, g):
    return jnp.concatenate([_swa_head(ref, 0, g * SW_GROUP + r, g % 2) for r in range(SW_GROUP)], axis=0)


def _swa_unstack(ref, x, g):
    W = SW_WINDOW
    for pair in range(SW_GROUP // 2):
        parts = []
        for r in (2 * pair, 2 * pair + 1):
            piece = x[r * W:(r + 1) * W]
            parts.append(piece if r % 2 == g % 2 else pltpu.roll(piece, SW_HEAD_DIM, axis=1))
        slab = (g * SW_GROUP) // 2 + pair
        ref[:, slab * LANES:(slab + 1) * LANES] = parts[0] + parts[1]


def _swa_kv(kp_ref, kc_ref, col0, g):
    return jnp.concatenate([_swa_head(kp_ref, col0, g, g % 2), _swa_head(kc_ref, col0, g, g % 2)], axis=0)


def _lane_pick(tile, h):
    lane = lax.broadcasted_iota(jnp.int32, tile.shape, 1)
    return jnp.sum(jnp.where(lane == h, tile, 0.0), axis=-1, keepdims=True)


def _lane_put(tile, h, col):
    lane = lax.broadcasted_iota(jnp.int32, tile.shape, 1)
    return jnp.where(lane == h, col, tile)


def _swa_rows(vals):
    return jnp.concatenate([jnp.broadcast_to(v, (SW_WINDOW, 1)) for v in vals], axis=0)


def _swa_fwd(q, kv, bias, sinks, *, name):
    _, T, D = q.shape
    W = SW_WINDOW
    nb = T // W
    dh = SW_HEAD_DIM
    kvw = SW_KV_HEADS * dh
    scale = dh ** -0.5

    def body(q_ref, kc_ref, kp_ref, bias_ref, sink_ref, o_ref, lse_ref):
        j = pl.program_id(0)
        mask = _swa_mask(j)
        sk = sink_ref[...]
        lse_tile = jnp.zeros((W, SW_Q_HEADS), F32)
        for g in range(SW_KV_HEADS):
            kk = _swa_kv(kp_ref, kc_ref, 0, g)
            vv = _swa_kv(kp_ref, kc_ref, kvw, g)
            qs = _swa_stack(q_ref, g)
            bias_g = bias_ref[g * SW_GROUP:(g + 1) * SW_GROUP].reshape(SW_GROUP * W, 2 * W)
            logits = jnp.where(mask, _dot_nt(qs, kk) * scale + bias_g, NEG_BIG)
            sink = _swa_rows([_lane_pick(sk, g * SW_GROUP + r) for r in range(SW_GROUP)])
            m = jnp.maximum(jnp.max(logits, axis=-1, keepdims=True), sink)
            p = jnp.exp(logits - m)
            den = jnp.sum(p, axis=-1, keepdims=True) + jnp.exp(sink - m)
            _swa_unstack(o_ref, _dot(p, vv) / den, g)
            lse = m + jnp.log(den)
            for r in range(SW_GROUP):
                lse_tile = _lane_put(lse_tile, g * SW_GROUP + r, lse[r * W:(r + 1) * W])
        lse_ref[...] = lse_tile

    return pl.pallas_call(body, name=name, grid=(nb,),
                          in_specs=[pl.BlockSpec((None, W, D), lambda j: (0, j, 0)),
                                    pl.BlockSpec((None, W, 2 * kvw), lambda j: (0, j, 0)),
                                    pl.BlockSpec((None, W, 2 * kvw), lambda j: (0, jnp.maximum(j - 1, 0), 0)),
                                    pl.BlockSpec((SW_Q_HEADS, W, 2 * W), lambda j: (0, 0, 0)),
                                    pl.BlockSpec((1, SW_Q_HEADS), lambda j: (0, 0))],
                          out_specs=[pl.BlockSpec((None, W, D), lambda j: (0, j, 0)),
                                     pl.BlockSpec((W, SW_Q_HEADS), lambda j: (j, 0))],
                          out_shape=[jax.ShapeDtypeStruct((1, T, D), F32), jax.ShapeDtypeStruct((T, SW_Q_HEADS), F32)],
                          compiler_params=_params())(q, kv, kv, bias, sinks)


def _swa_bwd(q, kv, o, lse, do, bias, sinks, *, name):
    _, T, D = q.shape
    W = SW_WINDOW
    nb = T // W
    dh = SW_HEAD_DIM
    kvw = SW_KV_HEADS * dh
    scale = dh ** -0.5
    cl = lambda j: jnp.minimum(j, nb - 1)

    def body(q_ref, kc_ref, kp_ref, o_ref, lse_ref, do_ref, bias_ref, sink_ref,
             dq_ref, dkv_ref, dbias_ref, dsink_ref, carry):
        j = pl.program_id(0)

        @pl.when(j == 0)
        def _():
            carry[...] = jnp.zeros_like(carry)
            dbias_ref[...] = jnp.zeros_like(dbias_ref)
            dsink_ref[...] = jnp.zeros_like(dsink_ref)

        @pl.when(j < nb)
        def _():
            mask = _swa_mask(j)
            sk = sink_ref[...]
            lse_tile = lse_ref[...]
            dsink = jnp.zeros((1, SW_Q_HEADS), F32)
            dks, dvs = [], []
            for g in range(SW_KV_HEADS):
                kk = _swa_kv(kp_ref, kc_ref, 0, g)
                vv = _swa_kv(kp_ref, kc_ref, kvw, g)
                qs = _swa_stack(q_ref, g)
                os_ = _swa_stack(o_ref, g)
                dos = _swa_stack(do_ref, g)
                bias_g = bias_ref[g * SW_GROUP:(g + 1) * SW_GROUP].reshape(SW_GROUP * W, 2 * W)
                heads = [g * SW_GROUP + r for r in range(SW_GROUP)]
                lse = jnp.concatenate([_lane_pick(lse_tile, h) for h in heads], axis=0)
                sink = _swa_rows([_lane_pick(sk, h) for h in heads])
                logits = jnp.where(mask, _dot_nt(qs, kk) * scale + bias_g, NEG_BIG)
                p = jnp.exp(logits - lse)
                psink = jnp.exp(sink - lse)
                delta = jnp.sum(dos * os_, axis=-1, keepdims=True)
                dl = p * (_dot_nt(dos, vv) - delta)
                _swa_unstack(dq_ref, _dot(dl, kk) * scale, g)
                dks.append(_dot_tn(dl, qs) * scale)
                dvs.append(_dot_tn(p, dos))
                dbias_ref[g * SW_GROUP:(g + 1) * SW_GROUP] += dl.reshape(SW_GROUP, W, 2 * W)
                sd = psink * delta
                for r, h in enumerate(heads):
                    dsink = _lane_put(dsink, h, -jnp.sum(sd[r * W:(r + 1) * W], axis=0, keepdims=True))
            dsink_ref[...] += dsink
            for slab in range(SW_KV_HEADS // 2):
                for col0, parts in ((0, dks), (kvw, dvs)):
                    both = parts[2 * slab] + parts[2 * slab + 1]
                    cols = slice(col0 + slab * LANES, col0 + (slab + 1) * LANES)
                    dkv_ref[:, cols] = carry[:, cols] + both[:W]
                    carry[:, cols] = both[W:]

        @pl.when(j == nb)
        def _():
            dkv_ref[...] = carry[...]

    tok = lambda w: pl.BlockSpec((None, W, w), lambda j: (0, cl(j), 0))
    return pl.pallas_call(body, name=name, grid=(nb + 1,),
                          in_specs=[tok(D), tok(2 * kvw),
                                    pl.BlockSpec((None, W, 2 * kvw), lambda j: (0, jnp.maximum(cl(j) - 1, 0), 0)),
                                    tok(D), pl.BlockSpec((W, SW_Q_HEADS), lambda j: (cl(j), 0)), tok(D),
                                    pl.BlockSpec((SW_Q_HEADS, W, 2 * W), lambda j: (0, 0, 0)),
                                    pl.BlockSpec((1, SW_Q_HEADS), lambda j: (0, 0))],
                          out_specs=[tok(D),
                                     pl.BlockSpec((None, W, 2 * kvw), lambda j: (0, jnp.maximum(j - 1, 0), 0)),
                                     pl.BlockSpec((SW_Q_HEADS, W, 2 * W), lambda j: (0, 0, 0)),
                                     pl.BlockSpec((1, SW_Q_HEADS), lambda j: (0, 0))],
                          out_shape=[jax.ShapeDtypeStruct((1, T, D), F32), jax.ShapeDtypeStruct((1, T, 2 * kvw), F32),
                                     jax.ShapeDtypeStruct((SW_Q_HEADS, W, 2 * W), F32),
                                     jax.ShapeDtypeStruct((1, SW_Q_HEADS), F32)],
                          scratch_shapes=[pltpu.VMEM((W, 2 * kvw), F32)],
                          compiler_params=_params())(q, kv, kv, o, lse, do, bias, sinks)


_HBM = pl.BlockSpec(memory_space=pltpu.HBM)
_SEM = pl.BlockSpec(memory_space=pltpu.SEMAPHORE)
_EFFECT = pltpu.SideEffectType.DATAFLOW_SIDE_EFFECTING
N_PEERS = N_DEV - 1


def _peer(k):
    x, y, c = lax.axis_index("x"), lax.axis_index("y"), lax.axis_index("c")
    px = (x + (k >> 2)) % 2
    py = (y + ((k >> 1) & 1)) % 2
    pc = (c + (k & 1)) % 2
    return (px, py, pc), 4 * px + 2 * py + pc


def _my_number():
    return 4 * lax.axis_index("x") + 2 * lax.axis_index("y") + lax.axis_index("c")


def _landing(src, mode):
    me = _my_number()
    own = src if mode == "gather" else lax.dynamic_index_in_dim(src, me, 0, keepdims=False)
    return lax.dynamic_update_index_in_dim(lax.empty((N_DEV,) + own.shape, own.dtype), own, me, 0)


def _copy(src_ref, land_ref, mode, send, recv, j, k, dst_slot):
    peer, pid = _peer(k)
    return pltpu.make_async_remote_copy(
        src_ref=src_ref if mode == "gather" else src_ref.at[pid], dst_ref=land_ref.at[dst_slot(pid)],
        send_sem=send.at[j * N_PEERS + k - 1], recv_sem=recv.at[j * N_PEERS + k - 1],
        device_id=peer, device_id_type=pl.DeviceIdType.MESH)


def _send_start(groups, *, name):
    flat = [t for g in groups for t in g]
    n, ng = len(flat), len(groups)
    srcs = [pltpu.with_memory_space_constraint(s, pltpu.HBM) for s, _ in flat]
    lands = [pltpu.with_memory_space_constraint(_landing(s, m), pltpu.HBM) for s, m in flat]

    def body(*refs):
        src_refs, land_refs = refs[:n], refs[n:2 * n]
        sems = refs[2 * n:2 * n + 2 * ng]
        token = refs[-1]
        me = _my_number()
        a = 0
        for gi, g in enumerate(groups):
            for j, (_, mode) in enumerate(g):
                for k in range(1, N_DEV):
                    _copy(src_refs[a], land_refs[a], mode, sems[2 * gi], sems[2 * gi + 1], j, k, lambda pid: me).start()
                a += 1
        token[...] = jnp.zeros_like(token)

    sem_shapes = []
    for g in groups:
        sem_shapes += [pltpu.SemaphoreType.DMA((len(g) * N_PEERS,))] * 2
    out = pl.pallas_call(
        body, name=name,
        out_shape=tuple(sem_shapes) + tuple(pltpu.HBM(a.shape, a.dtype) for a in srcs + lands)
        + (jax.ShapeDtypeStruct((SUBLANES, LANES), F32),),
        in_specs=[_HBM] * (2 * n), out_specs=[_SEM] * (2 * ng) + [_HBM] * (2 * n) + [pl.BlockSpec(memory_space=pltpu.VMEM)],
        input_output_aliases={i: 2 * ng + i for i in range(2 * n)},
        compiler_params=pltpu.CompilerParams(has_side_effects=_EFFECT))(*srcs, *lands)
    sems, thru, token = out[:2 * ng], out[2 * ng:2 * ng + 2 * n], out[-1]
    handles, a = [], 0
    for gi, g in enumerate(groups):
        m = len(g)
        handles.append((sems[2 * gi], sems[2 * gi + 1], list(thru[a:a + m]), list(thru[n + a:n + a + m]),
                        [mode for _, mode in g]))
        a += m
    return handles, token


def _send_wait(handle, after, *, name):
    send, recv, srcs, lands, modes = handle
    m = len(srcs)

    def body(*refs):
        src_refs, land_refs = refs[:m], refs[m:2 * m]
        send_ref, recv_ref = refs[2 * m], refs[2 * m + 1]
        for j in range(m):
            for k in range(1, N_DEV):
                cp = _copy(src_refs[j], land_refs[j], modes[j], send_ref, recv_ref, j, k, lambda pid: pid)
                cp.wait_send()
                cp.wait_recv()

    out = pl.pallas_call(
        body, name=name, out_shape=tuple(pltpu.HBM(a.shape, a.dtype) for a in srcs + lands),
        in_specs=[_HBM] * (2 * m) + [_SEM, _SEM] + [pl.BlockSpec(memory_space=pl.ANY)] * len(after),
        out_specs=[_HBM] * (2 * m), input_output_aliases={i: i for i in range(2 * m)},
        compiler_params=pltpu.CompilerParams(has_side_effects=_EFFECT))(*srcs, *lands, send, recv, *after)
    return list(out[m:])


def _adam_math(w, g, m, v):
    m = ADAM_B1 * m + (1.0 - ADAM_B1) * g
    v = ADAM_B2 * v + (1.0 - ADAM_B2) * (g * g)
    m_hat = m / (1.0 - ADAM_B1 ** ADAM_STEP)
    v_hat = v / (1.0 - ADAM_B2 ** ADAM_STEP)
    delta = -ADAM_LR * (m_hat / (jnp.sqrt(v_hat) + ADAM_EPS) + ADAM_WD * w)
    return delta, m, v


def _adamw(parts, w, m, v, *, name, layer=None):
    S, R, C = parts.shape
    tr = R
    for cand in (256, 128, 64, 32, 16, 8):
        if R % cand == 0 and S * cand * C * 4 <= 4 * 2 ** 20:
            tr = cand
            break

    def body(p_ref, w_ref, m_ref, v_ref, g_ref, d_ref, nm_ref, nv_ref):
        g = p_ref[0].astype(F32)
        for s in range(1, S):
            g = g + p_ref[s].astype(F32)
        delta, nm, nv = _adam_math(w_ref[...], g, m_ref[...], v_ref[...])
        g_ref[...] = g
        d_ref[...] = delta
        nm_ref[...] = nm
        nv_ref[...] = nv

    if layer is None:
        wspec = pl.BlockSpec((tr, C), lambda i: (i, 0))
    else:
        wspec = pl.BlockSpec((None, tr, C), lambda i: (layer, i, 0))
    ospec = pl.BlockSpec((tr, C), lambda i: (i, 0))
    osh = jax.ShapeDtypeStruct((R, C), F32)
    return pl.pallas_call(body, name=name, grid=(R // tr,),
                          in_specs=[pl.BlockSpec((S, tr, C), lambda i: (0, i, 0)), wspec, wspec, wspec],
                          out_specs=[ospec] * 4, out_shape=[osh] * 4, compiler_params=_params())(parts, w, m, v)


def _sum_parts(parts, *, name):
    S, R, C = parts.shape

    def body(p_ref, o_ref):
        g = p_ref[0]
        for s in range(1, S):
            g = g + p_ref[s]
        o_ref[...] = g

    return pl.pallas_call(body, name=name, out_shape=jax.ShapeDtypeStruct((R, C), F32),
                          compiler_params=_params())(parts)


def _pack_rows(arrays):
    pieces, layout, row = [], [], 0
    for a in arrays:
        flat = a.reshape(-1).astype(F32)
        rows = -(-flat.shape[0] // (SUBLANES * LANES)) * SUBLANES
        flat = jnp.pad(flat, (0, rows * LANES - flat.shape[0]))
        pieces.append(flat.reshape(rows, LANES))
        layout.append((row, rows, a.shape))
        row += rows
    return jnp.concatenate(pieces, axis=0), layout


def _unpack_rows(packed, layout):
    out = []
    for row, rows, shape in layout:
        size = int(np.prod(shape))
        out.append(packed[row:row + rows].reshape(-1)[:size].reshape(shape))
    return out


def _ffn_fwd(h, w_in, w_out, cw, ln_g, ln_b, tag):
    h, hb = h
    u, act = _ffn_up(hb, w_in, cw, name=f"ffn_up_{tag}")
    hn, hnb, xh, rs = _mm_nn(act, w_out, res=h, res_scale=ALPHA, ln=(ln_g, ln_b), name=f"ffn_down_{tag}")
    return (hn, hnb), xh, rs, u


def _ffn_bwd(dy, hb, u, w_in, w_out, cw, tag):
    du, dw_out, dcw = _ffn_gate_bwd(dy, u, w_out, cw, name=f"ffn_gate_bwd_{tag}")
    du = du.reshape((-1,) + du.shape[2:])
    dw_in = _mm_tn(hb, du, n_map=_pair_map, name=f"ffn_dwin_{tag}")
    dh = _mm_nt(du, w_in, n_map=_pair_map, res=dy, res_scale=ALPHA, name=f"ffn_dh_{tag}")
    dcw = dcw.transpose(1, 0, 2, 3).reshape((-1,) + dcw.shape[2:])
    return dh, dw_in, dw_out, dcw


def kernel(x, hgrn_w_in, hgrn_lb_logits, hgrn_gnorm_w, hgrn_w_out, swa_w_q, swa_sinks, swa_w_out, shared_w_kv, rel_bias, ffn_w_in, ffn_conv_w, ffn_conv_b, ffn_w_out, ln_mix_g, ln_mix_b, ln_ffn_g, ln_ffn_b, loss_target, m_hgrn_w_in, m_hgrn_lb_logits, m_hgrn_gnorm_w, m_hgrn_w_out, m_swa_w_q, m_swa_sinks, m_swa_w_out, m_shared_w_kv, m_rel_bias, m_ffn_w_in, m_ffn_conv_w, m_ffn_conv_b, m_ffn_w_out, m_ln_mix_g, m_ln_mix_b, m_ln_ffn_g, m_ln_ffn_b, v_hgrn_w_in, v_hgrn_lb_logits, v_hgrn_gnorm_w, v_hgrn_w_out, v_swa_w_q, v_swa_sinks, v_swa_w_out, v_shared_w_kv, v_rel_bias, v_ffn_w_in, v_ffn_conv_w, v_ffn_conv_b, v_ffn_w_out, v_ln_mix_g, v_ln_mix_b, v_ln_ffn_g, v_ln_ffn_b):
    T = x.shape[1]
    D = D_MODEL
    W = SW_WINDOW
    fb = ffn_w_in.shape[2]
    me = 4 * lax.axis_index("x") + 2 * lax.axis_index("y") + lax.axis_index("c")

    small_fwd, small_fwd_layout = _pack_rows([hgrn_lb_logits, ffn_conv_w])
    gat = lambda *ws: [(w_.astype(MXU), "gather") for w_ in ws]
    (wait_a, wait_b, wait_c), _ = _send_start(
        [gat(hgrn_w_in[0]) + [(small_fwd, "gather")],
         gat(hgrn_w_out[0], ffn_w_in[0], ffn_w_out[0]),
         gat(shared_w_kv, swa_w_q[0], swa_w_out[0], ffn_w_in[1], ffn_w_out[1])], name="gather_start")
    w_hin, small_all = _send_wait(wait_a, (), name="gather_wait_a")
    w_hin = w_hin[None]
    ffn_rows = 2 * ffn_w_out.shape[1]
    (lb_row, lb_rows, _), (cw_row, cw_rows, _) = small_fwd_layout
    lbl = small_all[:, lb_row:lb_row + 2, :].transpose(1, 0, 2).reshape(2, D)
    conv_w_all = small_all[:, cw_row:cw_row + cw_rows, :].reshape(N_DEV, -1)[:, :DEPTH * 3 * fb]
    conv_w_all = conv_w_all.reshape(N_DEV, DEPTH, 3, fb).transpose(1, 0, 2, 3)
    conv_b_all = ffn_conv_b.reshape(DEPTH, N_DEV, 1, fb)
    no_pad = ((0, 0), (0, 0))
    cw = (jnp.pad(conv_w_all, no_pad + ((0, SUBLANES - 3), (0, 0)))
          + jnp.pad(conv_b_all, no_pad + ((3, SUBLANES - 4), (0, 0))))

    row = lambda a, l: a[l:l + 1]

    z = _mm_nn(x, w_hin, name="hgrn_in")
    og, states = _hgrn_fwd(z, lbl, hgrn_gnorm_w, name="hgrn_rec")
    w_hout, w_fin0, w_fout0 = _send_wait(wait_b, (og,), name="gather_wait_b")
    w_hout = w_hout.reshape(1, 1, D, D)
    w_fin = [w_fin0[None], None]
    w_fout = [w_fout0.reshape(4, 1, ffn_rows, D), None]
    h1, h1b, xh1, rs1 = _mm_nn(og, w_hout, res=x, res_scale=ALPHA, ln=(row(ln_mix_g, 0), row(ln_mix_b, 0)),
                               name="hgrn_out")
    (h2, h2b), xh2, rs2, u0 = _ffn_fwd((h1, h1b), w_fin[0], w_fout[0], cw[0], row(ln_ffn_g, 0), row(ln_ffn_b, 0), "l0")
    w_kv, w_q, w_o, w_fin1, w_fout1 = _send_wait(wait_c, (h2,), name="gather_wait_c")
    w_kv = w_kv.reshape(1, 1, D, 2 * SW_KV_HEADS * SW_HEAD_DIM)
    w_q = w_q.reshape(1, 1, D, D)
    w_o = w_o.reshape(1, 1, D, D)
    w_fin[1] = w_fin1[None]
    w_fout[1] = w_fout1.reshape(4, 1, ffn_rows, D)
    kv = _mm_nn(h2b, w_kv, name="swa_kv")
    q = _mm_nn(h2b, w_q, name="swa_q")
    onehot = _bucket_onehot()
    bias = _bias_expand(rel_bias.T, jnp.asarray(onehot.T, jnp.bfloat16), name="swa_bias").reshape(SW_Q_HEADS, W, 2 * W)
    ao, lse = _swa_fwd(q, kv, bias, swa_sinks, name="swa_attn")
    h3, h3b, xh3, rs3 = _mm_nn(ao, w_o, res=h2, res_scale=ALPHA, ln=(row(ln_mix_g, 1), row(ln_mix_b, 1)),
                               name="swa_out")
    (h4, _), xh4, rs4, u1 = _ffn_fwd((h3, h3b), w_fin[1], w_fout[1], cw[1], row(ln_ffn_g, 1), row(ln_ffn_b, 1), "l1")
    dh4, loss_tile = _loss_head(h4, loss_target, name="loss_head")

    dy4, dg_f1, db_f1 = _ln_bwd(dh4, xh4, rs4, row(ln_ffn_g, 1), name="ln_ffn1_bwd")
    sc = lambda *gs: [(g_, "scatter") for g_ in gs]
    dh3, dw_fin1, dw_fout1, dcw1 = _ffn_bwd(dy4, h3b, u1, w_fin[1], w_fout[1], cw[1], "l1")
    (ex1,), tok1 = _send_start([sc(dw_fin1.reshape(N_DEV, D, fb), dw_fout1.reshape(N_DEV, -1, D))],
                               name="grads_start_1")
    dy3, dg_m1, db_m1 = _ln_bwd(dh3, xh3, rs3, row(ln_mix_g, 1), name="ln_mix1_bwd", behind=(tok1,))
    dw_o = _mm_tn(ao, dy3, name="swa_dwo")
    dao = _mm_nt(dy3, w_o, name="swa_dao")
    dq, dkv, dbias, dsinks = _swa_bwd(q, kv, ao, lse, dao, bias, swa_sinks, name="swa_attn_bwd")
    drel_t = _bias_reduce(dbias.reshape(SW_Q_HEADS, W * 2 * W), jnp.asarray(onehot, jnp.bfloat16), name="swa_dbias")
    dw_q = _mm_tn(h2b, dq, name="swa_dwq")
    dw_kv = _mm_tn(h2b, dkv, name="swa_dwkv")
    dh2 = _mm_nt(dq, w_q, res=dy3, res_scale=ALPHA, name="swa_dh_q")
    dh2 = _mm_nt(dkv, w_kv, res=dh2, res_scale=1.0, name="swa_dh_kv")
    (ex2,), tok2 = _send_start([sc(dw_o.reshape(N_DEV, D // N_DEV, D), dw_q.reshape(N_DEV, D // N_DEV, D),
                                   dw_kv.reshape(N_DEV, D // N_DEV, -1))], name="grads_start_2")
    dy2, dg_f0, db_f0 = _ln_bwd(dh2, xh2, rs2, row(ln_ffn_g, 0), name="ln_ffn0_bwd", behind=(tok2,))
    dh1, dw_fin0, dw_fout0, dcw0 = _ffn_bwd(dy2, h1b, u0, w_fin[0], w_fout[0], cw[0], "l0")
    (ex3,), tok3 = _send_start([sc(dw_fin0.reshape(N_DEV, D, fb), dw_fout0.reshape(N_DEV, -1, D))],
                               name="grads_start_3")
    dy1, dg_m0, db_m0 = _ln_bwd(dh1, xh1, rs1, row(ln_mix_g, 0), name="ln_mix0_bwd", behind=(tok3,))
    dw_hout = _mm_tn(og, dy1, name="hgrn_dwout")
    dog = _mm_nt(dy1, w_hout, name="hgrn_dog")
    dz, dlb, dgw = _hgrn_bwd(z, dog, states, lbl, hgrn_gnorm_w, name="hgrn_rec_bwd")
    dw_hin = _mm_tn(x, dz, name="hgrn_dwin")

    p0 = _sigmoid(lbl[0:1] - lbl[1:2])
    dl0 = dlb * p0 * (1.0 - p0)
    d_lbl = dl0 * jnp.array([[1.0], [-1.0]], F32)
    dcw = jnp.stack([dcw0, dcw1], axis=0)
    d_conv_w = dcw[:, :, 0:3, :]
    d_conv_b = dcw[:, :, 3, :].reshape(DEPTH, N_DEV * fb)
    first_row = lax.broadcasted_iota(jnp.int32, (DEPTH, D), 0) == 0
    two_rows = lambda a, b: jnp.where(first_row, a, b)
    d_ln_mix_g = two_rows(dg_m0, dg_m1)
    d_ln_mix_b = two_rows(db_m0, db_m1)
    d_ln_ffn_g = two_rows(dg_f0, dg_f1)
    d_ln_ffn_b = two_rows(db_f0, db_f1)
    small_grads, small_layout = _pack_rows([d_lbl, d_conv_w, dgw, dsinks, drel_t.T, d_conv_b, d_ln_mix_g, d_ln_mix_b,
                                            d_ln_ffn_g, d_ln_ffn_b, loss_tile[0:1, 0:1]])

    (ex4,), tok4 = _send_start([sc(dw_hin.reshape(N_DEV, D, -1), dw_hout.reshape(N_DEV, D // N_DEV, D))
                                + [(small_grads, "gather")]], name="grads_start_4")
    dx = _mm_nt(dz, w_hin, res=dy1, res_scale=ALPHA, name="hgrn_dx", behind=(tok4,))
    r_fin1, r_fout1 = _send_wait(ex1, (dx,), name="grads_wait_1")
    r_o, r_q, r_kv = _send_wait(ex2, (dx,), name="grads_wait_2")
    r_fin0, r_fout0 = _send_wait(ex3, (dx,), name="grads_wait_3")
    r_hin, r_hout, r_small = _send_wait(ex4, (dx,), name="grads_wait_4")
    received = [r_hin, r_hout, r_q, r_o, r_kv, r_fin0, r_fin1, r_fout0, r_fout1, r_small]

    outs = {}

    def put(name_, res):
        outs["grad_" + name_], outs["delta_" + name_], outs["new_m_" + name_], outs["new_v_" + name_] = res

    def big_update(name_, parts, w, m, v):
        shp = w.shape
        if w.ndim == 3 and shp[0] == 1:
            r = _adamw(parts, w[0], m[0], v[0], name="adamw_" + name_)
            put(name_, [a.reshape(shp) for a in r])
        else:
            r = _adamw(parts, w, m, v, name="adamw_" + name_)
            put(name_, r)

    big_update("hgrn_w_in", received[0], hgrn_w_in, m_hgrn_w_in, v_hgrn_w_in)
    big_update("hgrn_w_out", received[1], hgrn_w_out, m_hgrn_w_out, v_hgrn_w_out)
    big_update("swa_w_q", received[2], swa_w_q, m_swa_w_q, v_swa_w_q)
    big_update("swa_w_out", received[3], swa_w_out, m_swa_w_out, v_swa_w_out)
    big_update("shared_w_kv", received[4], shared_w_kv, m_shared_w_kv, v_shared_w_kv)
    for name_, idx, w, m, v in (("ffn_w_in", 5, ffn_w_in, m_ffn_w_in, v_ffn_w_in),
                                ("ffn_w_out", 7, ffn_w_out, m_ffn_w_out, v_ffn_w_out)):
        per_layer = [_adamw(received[idx + l], w, m, v, layer=l, name=f"adamw_{name_}_{l}") for l in range(DEPTH)]
        put(name_, [jnp.stack([per_layer[0][i], per_layer[1][i]], axis=0) for i in range(4)])

    small_sum = _sum_parts(received[9], name="sum_small_grads")
    (g_lbl, g_conv_w, g_gw, g_sinks, g_rel, g_conv_b, g_mix_g, g_mix_b, g_ffn_g, g_ffn_b,
     loss) = _unpack_rows(small_sum, small_layout)
    g_lbl_mine = lax.dynamic_slice_in_dim(g_lbl, me * (D // N_DEV), D // N_DEV, axis=1)
    g_conv_w_mine = lax.dynamic_index_in_dim(g_conv_w, me, axis=1, keepdims=False)
    small_names = ["hgrn_lb_logits", "ffn_conv_w", "hgrn_gnorm_w", "swa_sinks", "rel_bias", "ffn_conv_b",
                   "ln_mix_g", "ln_mix_b", "ln_ffn_g", "ln_ffn_b"]
    small_g = [g_lbl_mine, g_conv_w_mine, g_gw, g_sinks, g_rel, g_conv_b, g_mix_g, g_mix_b, g_ffn_g, g_ffn_b]
    small_w = [hgrn_lb_logits, ffn_conv_w, hgrn_gnorm_w, swa_sinks, rel_bias, ffn_conv_b, ln_mix_g, ln_mix_b,
               ln_ffn_g, ln_ffn_b]
    small_m = [m_hgrn_lb_logits, m_ffn_conv_w, m_hgrn_gnorm_w, m_swa_sinks, m_rel_bias, m_ffn_conv_b, m_ln_mix_g,
               m_ln_mix_b, m_ln_ffn_g, m_ln_ffn_b]
    small_v = [v_hgrn_lb_logits, v_ffn_conv_w, v_hgrn_gnorm_w, v_swa_sinks, v_rel_bias, v_ffn_conv_b, v_ln_mix_g,
               v_ln_mix_b, v_ln_ffn_g, v_ln_ffn_b]
    pg, lay = _pack_rows(small_g)
    pw, _ = _pack_rows(small_w)
    pm, _ = _pack_rows(small_m)
    pv, _ = _pack_rows(small_v)
    res = _adamw(pg[None], pw, pm, pv, name="adamw_small")
    unpacked = [_unpack_rows(r, lay) for r in res]
    for i, name_ in enumerate(small_names):
        put(name_, [unpacked[j][i] for j in range(4)])

    order = ["hgrn_w_in", "hgrn_lb_logits", "hgrn_gnorm_w", "hgrn_w_out", "swa_w_q", "swa_sinks", "swa_w_out",
             "shared_w_kv", "rel_bias", "ffn_w_in", "ffn_conv_w", "ffn_conv_b", "ffn_w_out", "ln_mix_g", "ln_mix_b",
             "ln_ffn_g", "ln_ffn_b"]
    result = [loss.reshape(()), dx]
    for kind in ("grad_", "delta_", "new_m_", "new_v_"):
        result += [outs[kind + n] for n in order]
    return tuple(result)
```

```python
import functools
import math

import numpy as np
import jax
import jax.numpy as jnp
from jax import lax
from jax.experimental import pallas as pl
from jax.experimental.pallas import tpu as pltpu

F32 = jnp.float32
MXU = jnp.bfloat16

N_DEV = 8
D_MODEL = 1024
DEPTH = 2
HG_HEADS = 8
HG_DIM = 128
HG_CHUNK = 64
SW_Q_HEADS = 16
SW_KV_HEADS = 4
SW_GROUP = 4
SW_HEAD_DIM = 64
SW_WINDOW = 128
REL_BUCKETS = 32
REL_MAX_DIST = 128
FFN_DIM = 2816
ALPHA = (2.0 * DEPTH) ** 0.25
LN_EPS = 1e-5
RMS_EPS = 1e-6
ADAM_LR = 0.001
ADAM_B1 = 0.9
ADAM_B2 = 0.999
ADAM_EPS = 1e-08
ADAM_WD = 0.01
ADAM_STEP = 10
EXP_CLAMP = 80.0
NEG_BIG = -1e30

SUBLANES = 8
LANES = 128
VMEM_LIMIT = 48 * 2 ** 20
TOKEN_TILE = 512
WIDE_TOKEN_TILE = 1024
RESIDENT_TOKEN_TILE = 256
REDUCE_TOKEN_TILE = 2048
GRAD_DTYPE = jnp.bfloat16
HALO = 16


def _params(**kw):
    return pltpu.CompilerParams(vmem_limit_bytes=VMEM_LIMIT, **kw)


def _sigmoid(x):
    return 1.0 / (1.0 + jnp.exp(-x))


def _dot(a, b):
    return jnp.dot(a.astype(MXU), b.astype(MXU), preferred_element_type=F32)


def _dot_nt(a, b):
    return lax.dot_general(a.astype(MXU), b.astype(MXU), (((1,), (1,)), ((), ())), preferred_element_type=F32)


def _dot_tn(a, b):
    return lax.dot_general(a.astype(MXU), b.astype(MXU), (((0,), (0,)), ((), ())), preferred_element_type=F32)


def _trunc_bf16(x):
    bits = lax.bitcast_convert_type(x, jnp.int32)
    return lax.bitcast_convert_type(bits & jnp.int32(-65536), F32)


def _split3(x):
    hi = _trunc_bf16(x)
    r = x - hi
    mid = _trunc_bf16(r)
    lo = r - mid
    return hi.astype(jnp.bfloat16), mid.astype(jnp.bfloat16), lo.astype(jnp.bfloat16)


def _dot_hp(a, b, contract):
    def halves(x):
        hi = _trunc_bf16(x)
        return hi.astype(jnp.bfloat16), (x - hi).astype(jnp.bfloat16)

    ah, al = halves(a)
    bh, bl = halves(b)
    d = lambda p, q: lax.dot_general(p, q, (contract, ((), ())), preferred_element_type=F32)
    return d(ah, bh) + d(ah, bl) + d(al, bh)


def _exact_dot(m01, x):
    hi, mid, lo = _split3(x)
    d = lambda p: jnp.dot(m01, p, preferred_element_type=F32)
    return d(hi) + d(mid) + d(lo)


def _exact_dot_r(x, m01):
    hi, mid, lo = _split3(x)
    d = lambda p: jnp.dot(p, m01, preferred_element_type=F32)
    return d(hi) + d(mid) + d(lo)


def _mm_nn(a, w, *, name, res=None, res_scale=1.0, ln=None, out_dtype=F32, tm=None):
    nbk, T, kw = a.shape
    _, nbn, _, nw = w.shape
    tm = min(tm or TOKEN_TILE, T)
    has_res = res is not None
    assert ln is None or nbn == 1

    def body(*refs):
        refs = list(refs)
        a_ref, w_ref = refs[:2]
        pos = 2
        res_ref = None
        if has_res:
            res_ref = refs[pos]
            pos += 1
        if ln is not None:
            g_ref, b_ref = refs[pos:pos + 2]
            pos += 2
        o_ref = refs[pos]
        pos += 1
        if ln is not None:
            ob_ref, xh_ref, rs_ref = refs[pos:pos + 3]
        for n in range(nbn):
            y = _dot(a_ref[0], w_ref[0, n])
            for k in range(1, nbk):
                y = y + _dot(a_ref[k], w_ref[k, n])
            if has_res:
                y = y + res_scale * res_ref[n].astype(F32)
            if ln is None:
                o_ref[n] = y.astype(o_ref.dtype)
            else:
                mu = jnp.mean(y, axis=-1, keepdims=True)
                yc = y - mu
                var = jnp.mean(yc * yc, axis=-1, keepdims=True)
                rstd = lax.rsqrt(var + LN_EPS)
                xh = yc * rstd
                xh_ref[n] = xh
                rs_ref[...] = rstd
                h = xh * g_ref[...] + b_ref[...]
                o_ref[n] = h
                ob_ref[n] = h.astype(ob_ref.dtype)

    in_specs = [pl.BlockSpec((nbk, tm, kw), lambda i: (0, i, 0)),
                pl.BlockSpec((nbk, nbn, kw, nw), lambda i: (0, 0, 0, 0))]
    args = [a, w]
    if has_res:
        in_specs.append(pl.BlockSpec((nbn, tm, nw), lambda i: (0, i, 0)))
        args.append(res)
    if ln is not None:
        in_specs += [pl.BlockSpec((1, nw), lambda i: (0, 0))] * 2
        args += list(ln)
    out_spec = pl.BlockSpec((nbn, tm, nw), lambda i: (0, i, 0))
    out_shape = jax.ShapeDtypeStruct((nbn, T, nw), out_dtype)
    if ln is not None:
        out_specs = [out_spec, out_spec, out_spec, pl.BlockSpec((tm, 1), lambda i: (i, 0))]
        out_shape = [out_shape, jax.ShapeDtypeStruct((nbn, T, nw), MXU), jax.ShapeDtypeStruct((nbn, T, nw), F32),
                     jax.ShapeDtypeStruct((T, 1), F32)]
    else:
        out_specs = out_spec
    return pl.pallas_call(body, name=name, grid=(T // tm,), in_specs=in_specs, out_specs=out_specs,
                          out_shape=out_shape, compiler_params=_params())(*args)


def _same(n):
    return n


def _mm_nt(dy, w, *, name, res=None, res_scale=1.0, out_dtype=F32, tm=None, n_map=_same, behind=()):
    nbn, T, nw = dy.shape
    nbk, _, kw, _ = w.shape
    tm = min(tm or WIDE_TOKEN_TILE, T)
    has_res = res is not None

    def body(*refs):
        refs = list(refs)
        dy_ref, w_ref = refs[:2]
        pos = 2
        res_ref = None
        if has_res:
            res_ref = refs[pos]
            pos += 1
        pos += len(behind)
        o_ref = refs[pos]
        pos += 1
        acc_ref = refs[pos] if nbn > 1 else None
        n = pl.program_id(2)
        part = _dot_nt(dy_ref[...], w_ref[...])

        def finish(acc):
            y = acc
            if has_res:
                y = y + res_scale * res_ref[...].astype(F32)
            o_ref[...] = y.astype(o_ref.dtype)

        if nbn == 1:
            finish(part)
        else:
            @pl.when(n == 0)
            def _():
                acc_ref[...] = part

            @pl.when(n > 0)
            def _():
                acc_ref[...] += part

            @pl.when(n == nbn - 1)
            def _():
                finish(acc_ref[...])

    in_specs = [pl.BlockSpec((None, tm, nw), lambda i, k, n: (n, i, 0)),
                pl.BlockSpec((None, None, kw, nw), lambda i, k, n: (k, n_map(n), 0, 0))]
    args = [dy, w]
    if has_res:
        in_specs.append(pl.BlockSpec((None, tm, kw), lambda i, k, n: (k, i, 0)))
        args.append(res)
    in_specs += [pl.BlockSpec(memory_space=pl.ANY)] * len(behind)
    args += list(behind)
    scratch = [pltpu.VMEM((tm, kw), F32)] if nbn > 1 else []
    return pl.pallas_call(body, name=name, grid=(T // tm, nbk, nbn), in_specs=in_specs,
                          out_specs=pl.BlockSpec((None, tm, kw), lambda i, k, n: (k, i, 0)),
                          out_shape=jax.ShapeDtypeStruct((nbk, T, kw), out_dtype), scratch_shapes=scratch,
                          compiler_params=_params())(*args)


def _mm_nt_resident(dy, w, *, name, res=None, res_scale=1.0, tm=None, n_map=_same, behind=()):
    nbn, T, nw = dy.shape
    nbk, _, kw, _ = w.shape
    assert nbk == 1
    tm = min(tm or RESIDENT_TOKEN_TILE, T)
    has_res = res is not None

    def body(*refs):
        dy_ref, w_ref = refs[:2]
        res_ref = refs[2] if has_res else None
        o_ref = refs[-1]
        y = _dot_nt(dy_ref[0], w_ref[0, n_map(0)])
        for n in range(1, nbn):
            y = y + _dot_nt(dy_ref[n], w_ref[0, n_map(n)])
        if has_res:
            y = y + res_scale * res_ref[0].astype(F32)
        o_ref[0] = y.astype(o_ref.dtype)

    in_specs = [pl.BlockSpec((nbn, tm, nw), lambda i: (0, i, 0)),
                pl.BlockSpec(w.shape, lambda i: (0, 0, 0, 0))]
    args = [dy, w]
    if has_res:
        in_specs.append(pl.BlockSpec((1, tm, kw), lambda i: (0, i, 0)))
        args.append(res)
    in_specs += [pl.BlockSpec(memory_space=pl.ANY)] * len(behind)
    args += list(behind)
    return pl.pallas_call(body, name=name, grid=(T // tm,), in_specs=in_specs,
                          out_specs=pl.BlockSpec((1, tm, kw), lambda i: (0, i, 0)),
                          out_shape=jax.ShapeDtypeStruct((1, T, kw), F32),
                          compiler_params=_params())(*args)


def _mm_tn(a, dy, *, name, tm=None, n_map=_same):
    nbk, T, kw = a.shape
    nbn, _, nw = dy.shape
    tm = min(tm or REDUCE_TOKEN_TILE, T)
    nt = T // tm

    def body(a_ref, dy_ref, o_ref, acc_ref):
        i = pl.program_id(2)
        part = _dot_tn(a_ref[...], dy_ref[...])

        @pl.when(i == 0)
        def _():
            acc_ref[...] = part

        @pl.when(i > 0)
        def _():
            acc_ref[...] += part

        @pl.when(i == nt - 1)
        def _():
            o_ref[...] = acc_ref[...].astype(o_ref.dtype)

    return pl.pallas_call(body, name=name, grid=(nbk, nbn, nt),
                          in_specs=[pl.BlockSpec((None, tm, kw), lambda k, n, i: (k, i, 0)),
                                    pl.BlockSpec((None, tm, nw), lambda k, n, i: (n, i, 0))],
                          out_specs=pl.BlockSpec((None, None, kw, nw), lambda k, n, i: (k, n_map(n), 0, 0)),
                          out_shape=jax.ShapeDtypeStruct((nbk, nbn, kw, nw), GRAD_DTYPE),
                          scratch_shapes=[pltpu.VMEM((kw, nw), F32)],
                          compiler_params=_params())(a, dy)


def _ln_bwd(dh, xhat, rstd, g, *, name, tm=None, behind=()):
    _, T, D = dh.shape
    tm = min(tm or TOKEN_TILE, T)

    def body(dh_ref, xh_ref, rs_ref, g_ref, *rest):
        dy_ref, dg_ref, db_ref = rest[-3:]
        i = pl.program_id(0)
        dhv = dh_ref[...]
        xh = xh_ref[...]
        dxh = dhv * g_ref[...]
        m1 = jnp.mean(dxh, axis=-1, keepdims=True)
        m2 = jnp.mean(dxh * xh, axis=-1, keepdims=True)
        dy_ref[...] = rs_ref[...] * (dxh - m1 - xh * m2)
        dg = jnp.sum(dhv * xh, axis=0, keepdims=True)
        db = jnp.sum(dhv, axis=0, keepdims=True)

        @pl.when(i == 0)
        def _():
            dg_ref[...] = dg
            db_ref[...] = db

        @pl.when(i > 0)
        def _():
            dg_ref[...] += dg
            db_ref[...] += db

    tok = pl.BlockSpec((None, tm, D), lambda i: (0, i, 0))
    vec = pl.BlockSpec((1, D), lambda i: (0, 0))
    return pl.pallas_call(body, name=name, grid=(T // tm,),
                          in_specs=[tok, tok, pl.BlockSpec((tm, 1), lambda i: (i, 0)), vec]
                          + [pl.BlockSpec(memory_space=pl.ANY)] * len(behind),
                          out_specs=[tok, vec, vec],
                          out_shape=[jax.ShapeDtypeStruct((1, T, D), F32), jax.ShapeDtypeStruct((1, D), F32),
                                     jax.ShapeDtypeStruct((1, D), F32)],
                          compiler_params=_params())(dh, xhat, rstd, g, *behind)


def _loss_head(h, tgt, *, name, tm=None):
    _, T, D = h.shape
    tm = min(tm or TOKEN_TILE, T)

    def body(h_ref, t_ref, dh_ref, loss_ref):
        i = pl.program_id(0)
        err = h_ref[...] - t_ref[...]
        dh_ref[...] = err / D
        part = 0.5 * jnp.sum(jnp.mean(err * err, axis=-1, keepdims=True), axis=0, keepdims=True)

        @pl.when(i == 0)
        def _():
            loss_ref[...] = jnp.zeros_like(loss_ref) + part

        @pl.when(i > 0)
        def _():
            loss_ref[...] += part

    tok = pl.BlockSpec((None, tm, D), lambda i: (0, i, 0))
    return pl.pallas_call(body, name=name, grid=(T // tm,), in_specs=[tok, tok],
                          out_specs=[tok, pl.BlockSpec((SUBLANES, LANES), lambda i: (0, 0))],
                          out_shape=[jax.ShapeDtypeStruct((1, T, D), F32),
                                     jax.ShapeDtypeStruct((SUBLANES, LANES), F32)],
                          compiler_params=_params())(h, tgt)


def _shift_rows(ext, k, n, halo):
    if k == 0:
        return ext[halo:halo + n]
    return pltpu.roll(ext, k, axis=0)[halo:halo + n]


def _conv_rows(ext, cw_ref, n, halo):
    return (cw_ref[0:1, :] * _shift_rows(ext, 2, n, halo) + cw_ref[1:2, :] * _shift_rows(ext, 1, n, halo)
            + cw_ref[2:3, :] * ext[halo:halo + n] + cw_ref[3:4, :])


def _pair_map(n):
    return n // 2 + 4 * (n % 2)


def _ffn_up(hb, w_in, cw, *, name, tm=None):
    _, T, D = hb.shape
    _, nb, _, fb = w_in.shape
    half = nb // 2
    tm = min(tm or TOKEN_TILE, T)

    def body(h_ref, wa_ref, wb_ref, cwa_ref, cwb_ref, u_ref, act_ref, carry):
        @pl.when(pl.program_id(1) == 0)
        def _():
            carry[...] = jnp.zeros_like(carry)

        h = h_ref[...]
        conv = []
        for s, (w_ref, cw_ref) in enumerate(((wa_ref, cwa_ref), (wb_ref, cwb_ref))):
            u = _dot(h, w_ref[...]).astype(u_ref.dtype)
            u_ref[s] = u
            uf = u.astype(F32)
            ext = jnp.concatenate([carry[s], uf], axis=0)
            conv.append(_conv_rows(ext, cw_ref, tm, SUBLANES))
            carry[s] = uf[tm - SUBLANES:tm]
        a, b = conv
        act_ref[...] = (a * _sigmoid(a) * b).astype(act_ref.dtype)

    wspec = lambda off: pl.BlockSpec((None, None, D, fb), lambda p, i: (0, p + off, 0, 0))
    cws = lambda off: pl.BlockSpec((None, SUBLANES, fb), lambda p, i: (p + off, 0, 0))
    return pl.pallas_call(body, name=name, grid=(half, T // tm),
                          in_specs=[pl.BlockSpec((None, tm, D), lambda p, i: (0, i, 0)), wspec(0), wspec(half),
                                    cws(0), cws(half)],
                          out_specs=[pl.BlockSpec((None, 2, tm, fb), lambda p, i: (p, 0, i, 0)),
                                     pl.BlockSpec((None, tm, fb), lambda p, i: (p, i, 0))],
                          out_shape=[jax.ShapeDtypeStruct((half, 2, T, fb), MXU),
                                     jax.ShapeDtypeStruct((half, T, fb), MXU)],
                          scratch_shapes=[pltpu.VMEM((2, SUBLANES, fb), F32)],
                          compiler_params=_params())(hb, w_in, w_in, cw, cw)


def _ffn_gate_bwd(dy, u, w_out, cw, *, name, tm=None):
    _, T, D = dy.shape
    half, _, _, fb = u.shape
    tm = min(tm or TOKEN_TILE, T)
    nt = T // tm
    rh = tm // HALO

    def body(dy_ref, u_ref, up_ref, w_ref, cwa_ref, cwb_ref, du_ref, dwo_ref, dcw_ref, carry, acc):
        i = pl.program_id(1)
        tile = nt - 1 - i

        @pl.when(i == 0)
        def _():
            carry[...] = jnp.zeros_like(carry)
            acc[...] = jnp.zeros_like(acc)
            dcw_ref[...] = jnp.zeros_like(dcw_ref)

        dyv = dy_ref[...]
        dact = _dot_nt(dyv, w_ref[...])
        exts, conv = [], []
        for s, cw_ref in enumerate((cwa_ref, cwb_ref)):
            prev = jnp.where(tile == 0, 0.0, up_ref[s].astype(F32))
            ext = jnp.concatenate([prev, u_ref[s].astype(F32)], axis=0)
            exts.append(ext)
            conv.append(_conv_rows(ext, cw_ref, tm, HALO))
        a, b = conv
        sa = _sigmoid(a)
        silu = a * sa
        acc[...] += _dot_tn(silu * b, dyv)
        dcs = (dact * b * (sa * (1.0 + a * (1.0 - sa))), dact * silu)
        m = tm + SUBLANES
        rows = lax.broadcasted_iota(jnp.int32, (SUBLANES, fb), 0)
        for s, cw_ref in enumerate((cwa_ref, cwb_ref)):
            dc = dcs[s]
            nxt = jnp.concatenate([dc, carry[s]], axis=0)
            du = (cw_ref[2:3, :] * dc + cw_ref[1:2, :] * pltpu.roll(nxt, m - 1, axis=0)[:tm]
                  + cw_ref[0:1, :] * pltpu.roll(nxt, m - 2, axis=0)[:tm])
            du_ref[s] = du.astype(du_ref.dtype)
            carry[s] = dc[0:SUBLANES]
            g0 = jnp.sum(dc * _shift_rows(exts[s], 2, tm, HALO), axis=0, keepdims=True)
            g1 = jnp.sum(dc * _shift_rows(exts[s], 1, tm, HALO), axis=0, keepdims=True)
            g2 = jnp.sum(dc * exts[s][HALO:HALO + tm], axis=0, keepdims=True)
            g3 = jnp.sum(dc, axis=0, keepdims=True)
            dcw_ref[s] += jnp.where(rows == 0, g0, jnp.where(rows == 1, g1, jnp.where(rows == 2, g2,
                                                                                jnp.where(rows == 3, g3, 0.0))))

        @pl.when(i == nt - 1)
        def _():
            dwo_ref[...] = acc[...].astype(dwo_ref.dtype)

    rev = lambda i: nt - 1 - i
    cws = lambda off: pl.BlockSpec((None, SUBLANES, fb), lambda p, i: (p + off, 0, 0))
    return pl.pallas_call(body, name=name, grid=(half, nt),
                          in_specs=[pl.BlockSpec((None, tm, D), lambda p, i: (0, rev(i), 0)),
                                    pl.BlockSpec((None, 2, tm, fb), lambda p, i: (p, 0, rev(i), 0)),
                                    pl.BlockSpec((None, 2, HALO, fb),
                                                 lambda p, i: (p, 0, jnp.maximum(rev(i) * rh - 1, 0), 0)),
                                    pl.BlockSpec((None, None, fb, D), lambda p, i: (p, 0, 0, 0)),
                                    cws(0), cws(half)],
                          out_specs=[pl.BlockSpec((None, 2, tm, fb), lambda p, i: (p, 0, rev(i), 0)),
                                     pl.BlockSpec((None, None, fb, D), lambda p, i: (p, 0, 0, 0)),
                                     pl.BlockSpec((None, 2, SUBLANES, fb), lambda p, i: (p, 0, 0, 0))],
                          out_shape=[jax.ShapeDtypeStruct((half, 2, T, fb), MXU),
                                     jax.ShapeDtypeStruct((half, 1, fb, D), GRAD_DTYPE),
                                     jax.ShapeDtypeStruct((half, 2, SUBLANES, fb), F32)],
                          scratch_shapes=[pltpu.VMEM((2, SUBLANES, fb), F32), pltpu.VMEM((fb, D), F32)],
                          compiler_params=_params())(dy, u, u, w_out, cw, cw)


def _tri(n, lower):
    r = lax.broadcasted_iota(jnp.int32, (n, n), 0)
    c = lax.broadcasted_iota(jnp.int32, (n, n), 1)
    return (r >= c) if lower else (r <= c)


def _hgrn_gates(zq, zf, lb):
    sq = _sigmoid(zq)
    sf = _sigmoid(zf)
    fg = lb + (1.0 - lb) * sf
    return zq * sq, sq, sf, fg, jnp.log(fg)


def _lb_of(lbl_ref, cols):
    return _sigmoid(lbl_ref[0:1, cols] - lbl_ref[1:2, cols])


def _ones_where(mask):
    return jnp.where(mask, 1.0, 0.0).astype(jnp.bfloat16)


def _hgrn_fwd(z, lbl, gw, *, name):
    _, T, zw = z.shape
    C = min(HG_CHUNK, T)
    nch = T // C
    hpb = zw // HG_DIM

    def body(z_ref, lbl_ref, gw_ref, og_ref, st_ref, s_scr, bc_scr, q_scr, k_scr):
        c = pl.program_id(0)

        @pl.when(c == 0)
        def _():
            s_scr[...] = jnp.zeros_like(s_scr)

        low = _tri(C, True)
        low01 = _ones_where(low)
        gwv = gw_ref[...]
        H = range(HG_HEADS)
        for blk in range(2):
            cols = slice(blk * zw, (blk + 1) * zw)
            qq, _, _, fg, lf = _hgrn_gates(z_ref[blk], z_ref[2 + blk], _lb_of(lbl_ref, cols))
            q_scr[:, cols] = qq
            k_scr[:, cols] = 1.0 - fg
            bc_scr[:, cols] = _exact_dot(low01, lf)
        col = lambda h: slice(h * HG_DIM, (h + 1) * HG_DIM)
        zcol = lambda part, h: (part + h // hpb, slice(None), slice((h % hpb) * HG_DIM, (h % hpb + 1) * HG_DIM))
        b = [bc_scr[:, col(h)] for h in H]
        bm = [bc_scr[C // 2 - 1:C // 2, col(h)] for h in H]
        bl = [bc_scr[C - 1:C, col(h)] for h in H]
        q_ = [q_scr[:, col(h)] for h in H]
        k_ = [k_scr[:, col(h)] for h in H]
        v_ = [z_ref[zcol(4, h)] for h in H]
        qt = [q_[h] * jnp.exp(jnp.minimum(b[h] - bm[h], EXP_CLAMP)) for h in H]
        kt = [k_[h] * jnp.exp(jnp.minimum(bm[h] - b[h], EXP_CLAMP)) for h in H]
        A = [jnp.where(low, _dot_nt(qt[h], kt[h]), 0.0) for h in H]
        for h in H:
            st_ref[h] = s_scr[h]
        o = [_dot_nt(q_[h] * jnp.exp(b[h]), s_scr[h]) + _dot(A[h], v_[h]) for h in H]
        for h in H:
            s_scr[h] = s_scr[h] * jnp.exp(bl[h]) + _dot_tn(v_[h], k_[h] * jnp.exp(bl[h] - b[h]))
        for h in H:
            g_h = z_ref[zcol(6, h)]
            r = lax.rsqrt(jnp.mean(o[h] * o[h], axis=-1, keepdims=True) + RMS_EPS)
            og_ref[:, col(h)] = (o[h] * r * gwv * (g_h * _sigmoid(g_h))).astype(og_ref.dtype)

    return pl.pallas_call(body, name=name, grid=(nch,),
                          in_specs=[pl.BlockSpec((8, C, zw), lambda c: (0, c, 0)),
                                    pl.BlockSpec((2, D_MODEL), lambda c: (0, 0)),
                                    pl.BlockSpec((1, HG_DIM), lambda c: (0, 0))],
                          out_specs=[pl.BlockSpec((None, C, D_MODEL), lambda c: (0, c, 0)),
                                     pl.BlockSpec((None, HG_HEADS, HG_DIM, HG_DIM), lambda c: (c, 0, 0, 0))],
                          out_shape=[jax.ShapeDtypeStruct((1, T, D_MODEL), MXU),
                                     jax.ShapeDtypeStruct((nch, HG_HEADS, HG_DIM, HG_DIM), F32)],
                          scratch_shapes=[pltpu.VMEM((HG_HEADS, HG_DIM, HG_DIM), F32)]
                          + [pltpu.VMEM((C, D_MODEL), F32)] * 3,
                          compiler_params=_params())(z, lbl, gw)


def _hgrn_bwd(z, dog, states, lbl, gw, *, name):
    _, T, zw = z.shape
    C = min(HG_CHUNK, T)
    nch = T // C
    hpb = zw // HG_DIM

    def body(z_ref, dog_ref, st0_ref, st1_ref, lbl_ref, gw_ref, dz_ref, dlb_ref, dgw_ref,
             d_scr, bc_scr, q_scr, sf_scr, fg_scr, x_scr):
        step = pl.program_id(0)

        @pl.when(step == 0)
        def _():
            d_scr[...] = jnp.zeros_like(d_scr)
            dlb_ref[...] = jnp.zeros_like(dlb_ref)
            dgw_ref[...] = jnp.zeros_like(dgw_ref)

        low = _tri(C, True)
        low01 = _ones_where(low)
        up01 = _ones_where(_tri(C, False))
        gwv = gw_ref[...]
        H = range(HG_HEADS)
        for blk in range(2):
            lbb = _lb_of(lbl_ref, slice(blk * zw, (blk + 1) * zw))
            qq, sq, sf, fg, lf = _hgrn_gates(z_ref[blk], z_ref[2 + blk], lbb)
            q_scr[:, blk * zw:(blk + 1) * zw] = qq
            sf_scr[:, blk * zw:(blk + 1) * zw] = sf
            fg_scr[:, blk * zw:(blk + 1) * zw] = fg
            bc_scr[:, blk * zw:(blk + 1) * zw] = _exact_dot(low01, lf)
        col = lambda h: slice(h * HG_DIM, (h + 1) * HG_DIM)
        zcol = lambda part, h: (part + h // hpb, slice(None), slice((h % hpb) * HG_DIM, (h % hpb + 1) * HG_DIM))
        b = [bc_scr[:, col(h)] for h in H]
        bm = [bc_scr[C // 2 - 1:C // 2, col(h)] for h in H]
        bl = [bc_scr[C - 1:C, col(h)] for h in H]
        q_ = [q_scr[:, col(h)] for h in H]
        k_ = [1.0 - fg_scr[:, col(h)] for h in H]
        v_ = [z_ref[zcol(4, h)] for h in H]
        eq = [jnp.exp(jnp.minimum(b[h] - bm[h], EXP_CLAMP)) for h in H]
        ek = [jnp.exp(jnp.minimum(bm[h] - b[h], EXP_CLAMP)) for h in H]
        eb = [jnp.exp(b[h]) for h in H]
        el = [jnp.exp(bl[h] - b[h]) for h in H]
        qt = [q_[h] * eq[h] for h in H]
        kt = [k_[h] * ek[h] for h in H]
        q0 = [q_[h] * eb[h] for h in H]
        kd = [k_[h] * el[h] for h in H]
        A = [jnp.where(low, _dot_nt(qt[h], kt[h]), 0.0) for h in H]
        o = [_dot_nt(q0[h], st0_ref[h]) + _dot(A[h], v_[h]) for h in H]
        do = []
        dgw_acc = jnp.zeros((1, HG_DIM), F32)
        for h in H:
            g_h = z_ref[zcol(6, h)]
            r = lax.rsqrt(jnp.mean(o[h] * o[h], axis=-1, keepdims=True) + RMS_EPS)
            on = o[h] * r
            sg = _sigmoid(g_h)
            dogh = dog_ref[:, col(h)].astype(F32)
            t1 = dogh * on
            dgw_acc = dgw_acc + jnp.sum(t1 * (g_h * sg), axis=0, keepdims=True)
            dz_ref[zcol(6, h)] = t1 * gwv * (sg * (1.0 + g_h * (1.0 - sg)))
            don = dogh * gwv * (g_h * sg)
            do.append(r * (don - on * jnp.mean(don * on, axis=-1, keepdims=True)))
        dgw_ref[...] += dgw_acc
        P = [jnp.where(low, _dot_nt(do[h], v_[h]), 0.0) for h in H]
        dqq = [eb[h] * _dot(do[h], st0_ref[h]) + eq[h] * _dot_hp(P[h], kt[h], ((1,), (0,))) for h in H]
        dkk = [el[h] * _dot(v_[h], d_scr[h]) + ek[h] * _dot_hp(P[h], qt[h], ((0,), (0,))) for h in H]
        for h in H:
            dz_ref[zcol(4, h)] = _dot_nt(kd[h], d_scr[h]) + _dot_tn(A[h], do[h])
            x_scr[:, col(h)] = q_[h] * dqq[h] - k_[h] * dkk[h]
        edge = [jnp.sum(d_scr[h] * st1_ref[h], axis=0, keepdims=True) for h in H]
        for h in H:
            d_scr[h] = d_scr[h] * jnp.exp(bl[h]) + _dot_tn(do[h], q0[h])
        for blk in range(2):
            x_scr[:, blk * zw:(blk + 1) * zw] = _exact_dot(up01, x_scr[:, blk * zw:(blk + 1) * zw])
        dlb = []
        for h in H:
            dfg = (x_scr[:, col(h)] + edge[h]) / fg_scr[:, col(h)] - dkk[h]
            sf_h = sf_scr[:, col(h)]
            lb_h = _lb_of(lbl_ref, col(h))
            zq_h = z_ref[zcol(0, h)]
            sq_h = _sigmoid(zq_h)
            dlb.append(jnp.sum(dfg * (1.0 - sf_h), axis=0, keepdims=True))
            dz_ref[zcol(0, h)] = dqq[h] * (sq_h * (1.0 + zq_h * (1.0 - sq_h)))
            dz_ref[zcol(2, h)] = dfg * (1.0 - lb_h) * sf_h * (1.0 - sf_h)
        dlb_ref[...] += jnp.concatenate(dlb, axis=1)

    rev = lambda s: nch - 1 - s
    return pl.pallas_call(body, name=name, grid=(nch,),
                          in_specs=[pl.BlockSpec((8, C, zw), lambda s: (0, rev(s), 0)),
                                    pl.BlockSpec((None, C, D_MODEL), lambda s: (0, rev(s), 0)),
                                    pl.BlockSpec((None, HG_HEADS, HG_DIM, HG_DIM), lambda s: (rev(s), 0, 0, 0)),
                                    pl.BlockSpec((None, HG_HEADS, HG_DIM, HG_DIM),
                                                 lambda s: (jnp.minimum(rev(s) + 1, nch - 1), 0, 0, 0)),
                                    pl.BlockSpec((2, D_MODEL), lambda s: (0, 0)),
                                    pl.BlockSpec((1, HG_DIM), lambda s: (0, 0))],
                          out_specs=[pl.BlockSpec((8, C, zw), lambda s: (0, rev(s), 0)),
                                     pl.BlockSpec((1, D_MODEL), lambda s: (0, 0)),
                                     pl.BlockSpec((1, HG_DIM), lambda s: (0, 0))],
                          out_shape=[jax.ShapeDtypeStruct((8, T, zw), F32), jax.ShapeDtypeStruct((1, D_MODEL), F32),
                                     jax.ShapeDtypeStruct((1, HG_DIM), F32)],
                          scratch_shapes=[pltpu.VMEM((HG_HEADS, HG_DIM, HG_DIM), F32)]
                          + [pltpu.VMEM((C, D_MODEL), F32)] * 5,
                          compiler_params=_params())(z, dog, states, states, lbl, gw)


def _bucket_onehot():
    W = SW_WINDOW
    t = np.arange(W)[:, None] + W
    s = np.arange(2 * W)[None, :]
    dist = t - s
    exact = REL_BUCKETS // 2
    d = np.maximum(np.maximum(dist, 0), 1).astype(np.float32)
    log_b = exact + (np.log(d / np.float32(exact)) / np.float32(math.log(REL_MAX_DIST / exact))
                     * np.float32(REL_BUCKETS - exact)).astype(np.int32)
    bucket = np.where(np.maximum(dist, 0) < exact, np.maximum(dist, 0), np.minimum(log_b, REL_BUCKETS - 1))
    valid = (dist >= 0) & (dist < W)
    onehot = (bucket[..., None] == np.arange(REL_BUCKETS)) & valid[..., None]
    return onehot.reshape(W * 2 * W, REL_BUCKETS).astype(np.float32)


def _bias_expand(rel_t, onehot_t, *, name):
    hq, nbk = rel_t.shape
    n = onehot_t.shape[1]

    def body(r_ref, oh_ref, o_ref):
        o_ref[...] = _exact_dot_r(r_ref[...], oh_ref[...])

    return pl.pallas_call(body, name=name, out_shape=jax.ShapeDtypeStruct((hq, n), F32),
                          compiler_params=_params())(rel_t, onehot_t)


def _bias_reduce(dbias, onehot, *, name):
    hq = dbias.shape[0]
    nbk = onehot.shape[1]

    def body(d_ref, oh_ref, o_ref):
        o_ref[...] = _exact_dot_r(d_ref[...], oh_ref[...])

    return pl.pallas_call(body, name=name, out_shape=jax.ShapeDtypeStruct((hq, nbk), F32),
                          compiler_params=_params())(dbias, onehot)


def _swa_mask(j):
    W = SW_WINDOW
    t = lax.broadcasted_iota(jnp.int32, (W, 2 * W), 0) + W
    s = lax.broadcasted_iota(jnp.int32, (W, 2 * W), 1)
    dist = t - s
    band = (dist >= 0) & (dist < W)
    m = band & ((j > 0) | (s >= W))
    return jnp.concatenate([m] * SW_GROUP, axis=0)


def _half_mask(rows, half):
    lane = lax.broadcasted_iota(jnp.int32, (rows, LANES), 1)
    return (lane >= SW_HEAD_DIM) if half else (lane < SW_HEAD_DIM)


def _swa_head(ref, col0, head, to_half):
    slab, half = head // 2, head % 2
    x = ref[:, col0 + slab * LANES:col0 + (slab + 1) * LANES]
    x = jnp.where(_half_mask(x.shape[0], half), x, 0.0)
    return x if half == to_half else pltpu.roll(x, SW_HEAD_DIM, axis=1)


def _swa_stack(ref, g):
    return jnp.concatenate([_swa_head(ref, 0, g * SW_GROUP + r, g % 2) for r in range(SW_GROUP)], axis=0)


def _swa_unstack(ref, x, g):
    W = SW_WINDOW
    for pair in range(SW_GROUP // 2):
        parts = []
        for r in (2 * pair, 2 * pair + 1):
            piece = x[r * W:(r + 1) * W]
            parts.append(piece if r % 2 == g % 2 else pltpu.roll(piece, SW_HEAD_DIM, axis=1))
        slab = (g * SW_GROUP) // 2 + pair
        ref[:, slab * LANES:(slab + 1) * LANES] = parts[0] + parts[1]


def _swa_kv(kp_ref, kc_ref, col0, g):
    return jnp.concatenate([_swa_head(kp_ref, col0, g, g % 2), _swa_head(kc_ref, col0, g, g % 2)], axis=0)


def _lane_pick(tile, h):
    lane = lax.broadcasted_iota(jnp.int32, tile.shape, 1)
    return jnp.sum(jnp.where(lane == h, tile, 0.0), axis=-1, keepdims=True)


def _lane_put(tile, h, col):
    lane = lax.broadcasted_iota(jnp.int32, tile.shape, 1)
    return jnp.where(lane == h, col, tile)


def _swa_rows(vals):
    return jnp.concatenate([jnp.broadcast_to(v, (SW_WINDOW, 1)) for v in vals], axis=0)


def _swa_fwd(q, kv, bias, sinks, *, name):
    _, T, D = q.shape
    W = SW_WINDOW
    nb = T // W
    dh = SW_HEAD_DIM
    kvw = SW_KV_HEADS * dh
    scale = dh ** -0.5

    def body(q_ref, kc_ref, kp_ref, bias_ref, sink_ref, o_ref, lse_ref):
        j = pl.program_id(0)
        mask = _swa_mask(j)
        sk = sink_ref[...]
        lse_tile = jnp.zeros((W, SW_Q_HEADS), F32)
        G = range(SW_KV_HEADS)
        kk = [_swa_kv(kp_ref, kc_ref, 0, g) for g in G]
        vv = [_swa_kv(kp_ref, kc_ref, kvw, g) for g in G]
        qs = [_swa_stack(q_ref, g) for g in G]
        logits = [jnp.where(mask, _dot_nt(qs[g], kk[g]) * scale
                            + bias_ref[g * SW_GROUP:(g + 1) * SW_GROUP].reshape(SW_GROUP * W, 2 * W), NEG_BIG) for g in G]
        sink = [_swa_rows([_lane_pick(sk, g * SW_GROUP + r) for r in range(SW_GROUP)]) for g in G]
        m = [jnp.maximum(jnp.max(logits[g], axis=-1, keepdims=True), sink[g]) for g in G]
        p = [jnp.exp(logits[g] - m[g]) for g in G]
        den = [jnp.sum(p[g], axis=-1, keepdims=True) + jnp.exp(sink[g] - m[g]) for g in G]
        pv = [_dot(p[g], vv[g]) for g in G]
        for g in G:
            _swa_unstack(o_ref, pv[g] / den[g], g)
            lse = m[g] + jnp.log(den[g])
            for r in range(SW_GROUP):
                lse_tile = _lane_put(lse_tile, g * SW_GROUP + r, lse[r * W:(r + 1) * W])
        lse_ref[...] = lse_tile

    return pl.pallas_call(body, name=name, grid=(nb,),
                          in_specs=[pl.BlockSpec((None, W, D), lambda j: (0, j, 0)),
                                    pl.BlockSpec((None, W, 2 * kvw), lambda j: (0, j, 0)),
                                    pl.BlockSpec((None, W, 2 * kvw), lambda j: (0, jnp.maximum(j - 1, 0), 0)),
                                    pl.BlockSpec((SW_Q_HEADS, W, 2 * W), lambda j: (0, 0, 0)),
                                    pl.BlockSpec((1, SW_Q_HEADS), lambda j: (0, 0))],
                          out_specs=[pl.BlockSpec((None, W, D), lambda j: (0, j, 0)),
                                     pl.BlockSpec((W, SW_Q_HEADS), lambda j: (j, 0))],
                          out_shape=[jax.ShapeDtypeStruct((1, T, D), F32), jax.ShapeDtypeStruct((T, SW_Q_HEADS), F32)],
                          compiler_params=_params())(q, kv, kv, bias, sinks)


def _swa_bwd(q, kv, o, lse, do, bias, sinks, *, name):
    _, T, D = q.shape
    W = SW_WINDOW
    nb = T // W
    dh = SW_HEAD_DIM
    kvw = SW_KV_HEADS * dh
    scale = dh ** -0.5
    cl = lambda j: jnp.minimum(j, nb - 1)

    def body(q_ref, kc_ref, kp_ref, o_ref, lse_ref, do_ref, bias_ref, sink_ref,
             dq_ref, dkv_ref, dbias_ref, dsink_ref, carry):
        j = pl.program_id(0)

        @pl.when(j == 0)
        def _():
            carry[...] = jnp.zeros_like(carry)
            dbias_ref[...] = jnp.zeros_like(dbias_ref)
            dsink_ref[...] = jnp.zeros_like(dsink_ref)

        @pl.when(j < nb)
        def _():
            mask = _swa_mask(j)
            sk = sink_ref[...]
            lse_tile = lse_ref[...]
            dsink = jnp.zeros((1, SW_Q_HEADS), F32)
            dks, dvs = [], []
            for g in range(SW_KV_HEADS):
                kk = _swa_kv(kp_ref, kc_ref, 0, g)
                vv = _swa_kv(kp_ref, kc_ref, kvw, g)
                qs = _swa_stack(q_ref, g)
                os_ = _swa_stack(o_ref, g)
                dos = _swa_stack(do_ref, g)
                bias_g = bias_ref[g * SW_GROUP:(g + 1) * SW_GROUP].reshape(SW_GROUP * W, 2 * W)
                heads = [g * SW_GROUP + r for r in range(SW_GROUP)]
                lse = jnp.concatenate([_lane_pick(lse_tile, h) for h in heads], axis=0)
                sink = _swa_rows([_lane_pick(sk, h) for h in heads])
                logits = jnp.where(mask, _dot_nt(qs, kk) * scale + bias_g, NEG_BIG)
                p = jnp.exp(logits - lse)
                psink = jnp.exp(sink - lse)
                delta = jnp.sum(dos * os_, axis=-1, keepdims=True)
                dl = p * (_dot_nt(dos, vv) - delta)
                _swa_unstack(dq_ref, _dot(dl, kk) * scale, g)
                dks.append(_dot_tn(dl, qs) * scale)
                dvs.append(_dot_tn(p, dos))
                dbias_ref[g * SW_GROUP:(g + 1) * SW_GROUP] += dl.reshape(SW_GROUP, W, 2 * W)
                sd = psink * delta
                for r, h in enumerate(heads):
                    dsink = _lane_put(dsink, h, -jnp.sum(sd[r * W:(r + 1) * W], axis=0, keepdims=True))
            dsink_ref[...] += dsink
            for slab in range(SW_KV_HEADS // 2):
                for col0, parts in ((0, dks), (kvw, dvs)):
                    both = parts[2 * slab] + parts[2 * slab + 1]
                    cols = slice(col0 + slab * LANES, col0 + (slab + 1) * LANES)
                    dkv_ref[:, cols] = carry[:, cols] + both[:W]
                    carry[:, cols] = both[W:]

        @pl.when(j == nb)
        def _():
            dkv_ref[...] = carry[...]

    tok = lambda w: pl.BlockSpec((None, W, w), lambda j: (0, cl(j), 0))
    return pl.pallas_call(body, name=name, grid=(nb + 1,),
                          in_specs=[tok(D), tok(2 * kvw),
                                    pl.BlockSpec((None, W, 2 * kvw), lambda j: (0, jnp.maximum(cl(j) - 1, 0), 0)),
                                    tok(D), pl.BlockSpec((W, SW_Q_HEADS), lambda j: (cl(j), 0)), tok(D),
                                    pl.BlockSpec((SW_Q_HEADS, W, 2 * W), lambda j: (0, 0, 0)),
                                    pl.BlockSpec((1, SW_Q_HEADS), lambda j: (0, 0))],
                          out_specs=[tok(D),
                                     pl.BlockSpec((None, W, 2 * kvw), lambda j: (0, jnp.maximum(j - 1, 0), 0)),
                                     pl.BlockSpec((SW_Q_HEADS, W, 2 * W), lambda j: (0, 0, 0)),
                                     pl.BlockSpec((1, SW_Q_HEADS), lambda j: (0, 0))],
                          out_shape=[jax.ShapeDtypeStruct((1, T, D), F32), jax.ShapeDtypeStruct((1, T, 2 * kvw), F32),
                                     jax.ShapeDtypeStruct((SW_Q_HEADS, W, 2 * W), F32),
                                     jax.ShapeDtypeStruct((1, SW_Q_HEADS), F32)],
                          scratch_shapes=[pltpu.VMEM((W, 2 * kvw), F32)],
                          compiler_params=_params())(q, kv, kv, o, lse, do, bias, sinks)


_HBM = pl.BlockSpec(memory_space=pltpu.HBM)
_SEM = pl.BlockSpec(memory_space=pltpu.SEMAPHORE)
_EFFECT = pltpu.SideEffectType.DATAFLOW_SIDE_EFFECTING
N_PEERS = N_DEV - 1


def _peer(k):
    x, y, c = lax.axis_index("x"), lax.axis_index("y"), lax.axis_index("c")
    px = (x + (k >> 2)) % 2
    py = (y + ((k >> 1) & 1)) % 2
    pc = (c + (k & 1)) % 2
    return (px, py, pc), 4 * px + 2 * py + pc


def _my_number():
    return 4 * lax.axis_index("x") + 2 * lax.axis_index("y") + lax.axis_index("c")


def _landing(src, mode):
    me = _my_number()
    own = src if mode == "gather" else lax.dynamic_index_in_dim(src, me, 0, keepdims=False)
    return lax.dynamic_update_index_in_dim(lax.empty((N_DEV,) + own.shape, own.dtype), own, me, 0)


def _copy(src_ref, land_ref, mode, send, recv, j, k, dst_slot):
    peer, pid = _peer(k)
    return pltpu.make_async_remote_copy(
        src_ref=src_ref if mode == "gather" else src_ref.at[pid], dst_ref=land_ref.at[dst_slot(pid)],
        send_sem=send.at[j * N_PEERS + k - 1], recv_sem=recv.at[j * N_PEERS + k - 1],
        device_id=peer, device_id_type=pl.DeviceIdType.MESH)


def _send_start(groups, *, name):
    flat = [t for g in groups for t in g]
    n, ng = len(flat), len(groups)
    srcs = [pltpu.with_memory_space_constraint(s, pltpu.HBM) for s, _ in flat]
    lands = [pltpu.with_memory_space_constraint(_landing(s, m), pltpu.HBM) for s, m in flat]

    def body(*refs):
        src_refs, land_refs = refs[:n], refs[n:2 * n]
        sems = refs[2 * n:2 * n + 2 * ng]
        token = refs[-1]
        me = _my_number()
        a = 0
        for gi, g in enumerate(groups):
            for j, (_, mode) in enumerate(g):
                for k in range(1, N_DEV):
                    _copy(src_refs[a], land_refs[a], mode, sems[2 * gi], sems[2 * gi + 1], j, k, lambda pid: me).start()
                a += 1
        token[...] = jnp.zeros_like(token)

    sem_shapes = []
    for g in groups:
        sem_shapes += [pltpu.SemaphoreType.DMA((len(g) * N_PEERS,))] * 2
    out = pl.pallas_call(
        body, name=name,
        out_shape=tuple(sem_shapes) + tuple(pltpu.HBM(a.shape, a.dtype) for a in srcs + lands)
        + (jax.ShapeDtypeStruct((SUBLANES, LANES), F32),),
        in_specs=[_HBM] * (2 * n), out_specs=[_SEM] * (2 * ng) + [_HBM] * (2 * n) + [pl.BlockSpec(memory_space=pltpu.VMEM)],
        input_output_aliases={i: 2 * ng + i for i in range(2 * n)},
        compiler_params=pltpu.CompilerParams(has_side_effects=_EFFECT))(*srcs, *lands)
    sems, thru, token = out[:2 * ng], out[2 * ng:2 * ng + 2 * n], out[-1]
    handles, a = [], 0
    for gi, g in enumerate(groups):
        m = len(g)
        handles.append((sems[2 * gi], sems[2 * gi + 1], list(thru[a:a + m]), list(thru[n + a:n + a + m]),
                        [mode for _, mode in g]))
        a += m
    return handles, token


def _send_wait(handle, after, *, name):
    send, recv, srcs, lands, modes = handle
    m = len(srcs)

    def body(*refs):
        src_refs, land_refs = refs[:m], refs[m:2 * m]
        send_ref, recv_ref = refs[2 * m], refs[2 * m + 1]
        for j in range(m):
            for k in range(1, N_DEV):
                cp = _copy(src_refs[j], land_refs[j], modes[j], send_ref, recv_ref, j, k, lambda pid: pid)
                cp.wait_send()
                cp.wait_recv()

    out = pl.pallas_call(
        body, name=name, out_shape=tuple(pltpu.HBM(a.shape, a.dtype) for a in srcs + lands),
        in_specs=[_HBM] * (2 * m) + [_SEM, _SEM] + [pl.BlockSpec(memory_space=pl.ANY)] * len(after),
        out_specs=[_HBM] * (2 * m), input_output_aliases={i: i for i in range(2 * m)},
        compiler_params=pltpu.CompilerParams(has_side_effects=_EFFECT))(*srcs, *lands, send, recv, *after)
    return list(out[m:])


def _adam_math(w, g, m, v):
    m = ADAM_B1 * m + (1.0 - ADAM_B1) * g
    v = ADAM_B2 * v + (1.0 - ADAM_B2) * (g * g)
    m_hat = m / (1.0 - ADAM_B1 ** ADAM_STEP)
    v_hat = v / (1.0 - ADAM_B2 ** ADAM_STEP)
    delta = -ADAM_LR * (m_hat / (jnp.sqrt(v_hat) + ADAM_EPS) + ADAM_WD * w)
    return delta, m, v


def _adamw(parts, w, m, v, *, name, layer=None):
    S, R, C = parts.shape
    tr = R
    for cand in (256, 128, 64, 32, 16, 8):
        if R % cand == 0 and S * cand * C * 4 <= 4 * 2 ** 20:
            tr = cand
            break

    def body(p_ref, w_ref, m_ref, v_ref, g_ref, d_ref, nm_ref, nv_ref):
        g = p_ref[0].astype(F32)
        for s in range(1, S):
            g = g + p_ref[s].astype(F32)
        delta, nm, nv = _adam_math(w_ref[...], g, m_ref[...], v_ref[...])
        g_ref[...] = g
        d_ref[...] = delta
        nm_ref[...] = nm
        nv_ref[...] = nv

    if layer is None:
        wspec = pl.BlockSpec((tr, C), lambda i: (i, 0))
    else:
        wspec = pl.BlockSpec((None, tr, C), lambda i: (layer, i, 0))
    ospec = pl.BlockSpec((tr, C), lambda i: (i, 0))
    osh = jax.ShapeDtypeStruct((R, C), F32)
    return pl.pallas_call(body, name=name, grid=(R // tr,),
                          in_specs=[pl.BlockSpec((S, tr, C), lambda i: (0, i, 0)), wspec, wspec, wspec],
                          out_specs=[ospec] * 4, out_shape=[osh] * 4, compiler_params=_params())(parts, w, m, v)


def _sum_parts(parts, *, name):
    S, R, C = parts.shape

    def body(p_ref, o_ref):
        g = p_ref[0]
        for s in range(1, S):
            g = g + p_ref[s]
        o_ref[...] = g

    return pl.pallas_call(body, name=name, out_shape=jax.ShapeDtypeStruct((R, C), F32),
                          compiler_params=_params())(parts)


def _pack_rows(arrays):
    pieces, layout, row = [], [], 0
    for a in arrays:
        flat = a.reshape(-1).astype(F32)
        rows = -(-flat.shape[0] // (SUBLANES * LANES)) * SUBLANES
        flat = jnp.pad(flat, (0, rows * LANES - flat.shape[0]))
        pieces.append(flat.reshape(rows, LANES))
        layout.append((row, rows, a.shape))
        row += rows
    return jnp.concatenate(pieces, axis=0), layout


def _unpack_rows(packed, layout):
    out = []
    for row, rows, shape in layout:
        size = int(np.prod(shape))
        out.append(packed[row:row + rows].reshape(-1)[:size].reshape(shape))
    return out


def _ffn_fwd(h, w_in, w_out, cw, ln_g, ln_b, tag):
    h, hb = h
    u, act = _ffn_up(hb, w_in, cw, name=f"ffn_up_{tag}")
    hn, hnb, xh, rs = _mm_nn(act, w_out, res=h, res_scale=ALPHA, ln=(ln_g, ln_b), name=f"ffn_down_{tag}")
    return (hn, hnb), xh, rs, u


def _ffn_bwd(dy, hb, u, w_in, w_out, cw, tag):
    du, dw_out, dcw = _ffn_gate_bwd(dy, u, w_out, cw, name=f"ffn_gate_bwd_{tag}")
    du = du.reshape((-1,) + du.shape[2:])
    dw_in = _mm_tn(hb, du, n_map=_pair_map, name=f"ffn_dwin_{tag}")
    dh = _mm_nt_resident(du, w_in, n_map=_pair_map, res=dy, res_scale=ALPHA, name=f"ffn_dh_{tag}")
    dcw = dcw.transpose(1, 0, 2, 3).reshape((-1,) + dcw.shape[2:])
    return dh, dw_in, dw_out, dcw


def kernel(x, hgrn_w_in, hgrn_lb_logits, hgrn_gnorm_w, hgrn_w_out, swa_w_q, swa_sinks, swa_w_out, shared_w_kv, rel_bias, ffn_w_in, ffn_conv_w, ffn_conv_b, ffn_w_out, ln_mix_g, ln_mix_b, ln_ffn_g, ln_ffn_b, loss_target, m_hgrn_w_in, m_hgrn_lb_logits, m_hgrn_gnorm_w, m_hgrn_w_out, m_swa_w_q, m_swa_sinks, m_swa_w_out, m_shared_w_kv, m_rel_bias, m_ffn_w_in, m_ffn_conv_w, m_ffn_conv_b, m_ffn_w_out, m_ln_mix_g, m_ln_mix_b, m_ln_ffn_g, m_ln_ffn_b, v_hgrn_w_in, v_hgrn_lb_logits, v_hgrn_gnorm_w, v_hgrn_w_out, v_swa_w_q, v_swa_sinks, v_swa_w_out, v_shared_w_kv, v_rel_bias, v_ffn_w_in, v_ffn_conv_w, v_ffn_conv_b, v_ffn_w_out, v_ln_mix_g, v_ln_mix_b, v_ln_ffn_g, v_ln_ffn_b):
    T = x.shape[1]
    D = D_MODEL
    W = SW_WINDOW
    fb = ffn_w_in.shape[2]
    me = 4 * lax.axis_index("x") + 2 * lax.axis_index("y") + lax.axis_index("c")

    small_fwd, small_fwd_layout = _pack_rows([hgrn_lb_logits, ffn_conv_w])
    gat = lambda *ws: [(w_.astype(MXU), "gather") for w_ in ws]
    (wait_a, wait_b, wait_c), _ = _send_start(
        [gat(hgrn_w_in[0]) + [(small_fwd, "gather")],
         gat(hgrn_w_out[0], ffn_w_in[0], ffn_w_out[0]),
         gat(shared_w_kv, swa_w_q[0], swa_w_out[0], ffn_w_in[1], ffn_w_out[1])], name="gather_start")
    w_hin, small_all = _send_wait(wait_a, (), name="gather_wait_a")
    w_hin = w_hin[None]
    ffn_rows = 2 * ffn_w_out.shape[1]
    (lb_row, lb_rows, _), (cw_row, cw_rows, _) = small_fwd_layout
    lbl = small_all[:, lb_row:lb_row + 2, :].transpose(1, 0, 2).reshape(2, D)
    conv_w_all = small_all[:, cw_row:cw_row + cw_rows, :].reshape(N_DEV, -1)[:, :DEPTH * 3 * fb]
    conv_w_all = conv_w_all.reshape(N_DEV, DEPTH, 3, fb).transpose(1, 0, 2, 3)
    conv_b_all = ffn_conv_b.reshape(DEPTH, N_DEV, 1, fb)
    no_pad = ((0, 0), (0, 0))
    cw = (jnp.pad(conv_w_all, no_pad + ((0, SUBLANES - 3), (0, 0)))
          + jnp.pad(conv_b_all, no_pad + ((3, SUBLANES - 4), (0, 0))))

    row = lambda a, l: a[l:l + 1]

    z = _mm_nn(x, w_hin, name="hgrn_in")
    og, states = _hgrn_fwd(z, lbl, hgrn_gnorm_w, name="hgrn_rec")
    w_hout, w_fin0, w_fout0 = _send_wait(wait_b, (og,), name="gather_wait_b")
    w_hout = w_hout.reshape(1, 1, D, D)
    w_fin = [w_fin0[None], None]
    w_fout = [w_fout0.reshape(4, 1, ffn_rows, D), None]
    h1, h1b, xh1, rs1 = _mm_nn(og, w_hout, res=x, res_scale=ALPHA, ln=(row(ln_mix_g, 0), row(ln_mix_b, 0)),
                               name="hgrn_out")
    (h2, h2b), xh2, rs2, u0 = _ffn_fwd((h1, h1b), w_fin[0], w_fout[0], cw[0], row(ln_ffn_g, 0), row(ln_ffn_b, 0), "l0")
    w_kv, w_q, w_o, w_fin1, w_fout1 = _send_wait(wait_c, (h2,), name="gather_wait_c")
    w_kv = w_kv.reshape(1, 1, D, 2 * SW_KV_HEADS * SW_HEAD_DIM)
    w_q = w_q.reshape(1, 1, D, D)
    w_o = w_o.reshape(1, 1, D, D)
    w_fin[1] = w_fin1[None]
    w_fout[1] = w_fout1.reshape(4, 1, ffn_rows, D)
    kv = _mm_nn(h2b, w_kv, name="swa_kv")
    q = _mm_nn(h2b, w_q, name="swa_q")
    onehot = _bucket_onehot()
    bias = _bias_expand(rel_bias.T, jnp.asarray(onehot.T, jnp.bfloat16), name="swa_bias").reshape(SW_Q_HEADS, W, 2 * W)
    ao, lse = _swa_fwd(q, kv, bias, swa_sinks, name="swa_attn")
    h3, h3b, xh3, rs3 = _mm_nn(ao, w_o, res=h2, res_scale=ALPHA, ln=(row(ln_mix_g, 1), row(ln_mix_b, 1)),
                               name="swa_out")
    (h4, _), xh4, rs4, u1 = _ffn_fwd((h3, h3b), w_fin[1], w_fout[1], cw[1], row(ln_ffn_g, 1), row(ln_ffn_b, 1), "l1")
    dh4, loss_tile = _loss_head(h4, loss_target, name="loss_head")

    dy4, dg_f1, db_f1 = _ln_bwd(dh4, xh4, rs4, row(ln_ffn_g, 1), name="ln_ffn1_bwd")
    sc = lambda *gs: [(g_, "scatter") for g_ in gs]
    dh3, dw_fin1, dw_fout1, dcw1 = _ffn_bwd(dy4, h3b, u1, w_fin[1], w_fout[1], cw[1], "l1")
    (ex1,), tok1 = _send_start([sc(dw_fin1.reshape(N_DEV, D, fb), dw_fout1.reshape(N_DEV, -1, D))],
                               name="grads_start_1")
    dy3, dg_m1, db_m1 = _ln_bwd(dh3, xh3, rs3, row(ln_mix_g, 1), name="ln_mix1_bwd", behind=(tok1,))
    dw_o = _mm_tn(ao, dy3, name="swa_dwo")
    dao = _mm_nt(dy3, w_o, name="swa_dao")
    dq, dkv, dbias, dsinks = _swa_bwd(q, kv, ao, lse, dao, bias, swa_sinks, name="swa_attn_bwd")
    drel_t = _bias_reduce(dbias.reshape(SW_Q_HEADS, W * 2 * W), jnp.asarray(onehot, jnp.bfloat16), name="swa_dbias")
    dw_q = _mm_tn(h2b, dq, name="swa_dwq")
    dw_kv = _mm_tn(h2b, dkv, name="swa_dwkv")
    dh2 = _mm_nt(dq, w_q, res=dy3, res_scale=ALPHA, name="swa_dh_q")
    dh2 = _mm_nt(dkv, w_kv, res=dh2, res_scale=1.0, name="swa_dh_kv")
    (ex2,), tok2 = _send_start([sc(dw_o.reshape(N_DEV, D // N_DEV, D), dw_q.reshape(N_DEV, D // N_DEV, D),
                                   dw_kv.reshape(N_DEV, D // N_DEV, -1))], name="grads_start_2")
    dy2, dg_f0, db_f0 = _ln_bwd(dh2, xh2, rs2, row(ln_ffn_g, 0), name="ln_ffn0_bwd", behind=(tok2,))
    dh1, dw_fin0, dw_fout0, dcw0 = _ffn_bwd(dy2, h1b, u0, w_fin[0], w_fout[0], cw[0], "l0")
    (ex3,), tok3 = _send_start([sc(dw_fin0.reshape(N_DEV, D, fb), dw_fout0.reshape(N_DEV, -1, D))],
                               name="grads_start_3")
    dy1, dg_m0, db_m0 = _ln_bwd(dh1, xh1, rs1, row(ln_mix_g, 0), name="ln_mix0_bwd", behind=(tok3,))
    dw_hout = _mm_tn(og, dy1, name="hgrn_dwout")
    dog = _mm_nt(dy1, w_hout, name="hgrn_dog")
    dz, dlb, dgw = _hgrn_bwd(z, dog, states, lbl, hgrn_gnorm_w, name="hgrn_rec_bwd")
    dw_hin = _mm_tn(x, dz, name="hgrn_dwin")

    p0 = _sigmoid(lbl[0:1] - lbl[1:2])
    dl0 = dlb * p0 * (1.0 - p0)
    d_lbl = dl0 * jnp.array([[1.0], [-1.0]], F32)
    dcw = jnp.stack([dcw0, dcw1], axis=0)
    d_conv_w = dcw[:, :, 0:3, :]
    d_conv_b = dcw[:, :, 3, :].reshape(DEPTH, N_DEV * fb)
    first_row = lax.broadcasted_iota(jnp.int32, (DEPTH, D), 0) == 0
    two_rows = lambda a, b: jnp.where(first_row, a, b)
    d_ln_mix_g = two_rows(dg_m0, dg_m1)
    d_ln_mix_b = two_rows(db_m0, db_m1)
    d_ln_ffn_g = two_rows(dg_f0, dg_f1)
    d_ln_ffn_b = two_rows(db_f0, db_f1)
    small_grads, small_layout = _pack_rows([d_lbl, d_conv_w, dgw, dsinks, drel_t.T, d_conv_b, d_ln_mix_g, d_ln_mix_b,
                                            d_ln_ffn_g, d_ln_ffn_b, loss_tile[0:1, 0:1]])

    (ex4,), tok4 = _send_start([sc(dw_hin.reshape(N_DEV, D, -1), dw_hout.reshape(N_DEV, D // N_DEV, D))
                                + [(small_grads, "gather")]], name="grads_start_4")
    dx = _mm_nt_resident(dz, w_hin, res=dy1, res_scale=ALPHA, name="hgrn_dx", behind=(tok4,))
    r_fin1, r_fout1 = _send_wait(ex1, (dx,), name="grads_wait_1")
    r_o, r_q, r_kv = _send_wait(ex2, (dx,), name="grads_wait_2")
    r_fin0, r_fout0 = _send_wait(ex3, (dx,), name="grads_wait_3")
    r_hin, r_hout, r_small = _send_wait(ex4, (dx,), name="grads_wait_4")
    received = [r_hin, r_hout, r_q, r_o, r_kv, r_fin0, r_fin1, r_fout0, r_fout1, r_small]

    outs = {}

    def put(name_, res):
        outs["grad_" + name_], outs["delta_" + name_], outs["new_m_" + name_], outs["new_v_" + name_] = res

    def big_update(name_, parts, w, m, v):
        shp = w.shape
        if w.ndim == 3 and shp[0] == 1:
            r = _adamw(parts, w[0], m[0], v[0], name="adamw_" + name_)
            put(name_, [a.reshape(shp) for a in r])
        else:
            r = _adamw(parts, w, m, v, name="adamw_" + name_)
            put(name_, r)

    big_update("hgrn_w_in", received[0], hgrn_w_in, m_hgrn_w_in, v_hgrn_w_in)
    big_update("hgrn_w_out", received[1], hgrn_w_out, m_hgrn_w_out, v_hgrn_w_out)
    big_update("swa_w_q", received[2], swa_w_q, m_swa_w_q, v_swa_w_q)
    big_update("swa_w_out", received[3], swa_w_out, m_swa_w_out, v_swa_w_out)
    big_update("shared_w_kv", received[4], shared_w_kv, m_shared_w_kv, v_shared_w_kv)
    for name_, idx, w, m, v in (("ffn_w_in", 5, ffn_w_in, m_ffn_w_in, v_ffn_w_in),
                                ("ffn_w_out", 7, ffn_w_out, m_ffn_w_out, v_ffn_w_out)):
        per_layer = [_adamw(received[idx + l], w, m, v, layer=l, name=f"adamw_{name_}_{l}") for l in range(DEPTH)]
        put(name_, [jnp.stack([per_layer[0][i], per_layer[1][i]], axis=0) for i in range(4)])

    small_sum = _sum_parts(received[9], name="sum_small_grads")
    (g_lbl, g_conv_w, g_gw, g_sinks, g_rel, g_conv_b, g_mix_g, g_mix_b, g_ffn_g, g_ffn_b,
     loss) = _unpack_rows(small_sum, small_layout)
    g_lbl_mine = lax.dynamic_slice_in_dim(g_lbl, me * (D // N_DEV), D // N_DEV, axis=1)
    g_conv_w_mine = lax.dynamic_index_in_dim(g_conv_w, me, axis=1, keepdims=False)
    small_names = ["hgrn_lb_logits", "ffn_conv_w", "hgrn_gnorm_w", "swa_sinks", "rel_bias", "ffn_conv_b",
                   "ln_mix_g", "ln_mix_b", "ln_ffn_g", "ln_ffn_b"]
    small_g = [g_lbl_mine, g_conv_w_mine, g_gw, g_sinks, g_rel, g_conv_b, g_mix_g, g_mix_b, g_ffn_g, g_ffn_b]
    small_w = [hgrn_lb_logits, ffn_conv_w, hgrn_gnorm_w, swa_sinks, rel_bias, ffn_conv_b, ln_mix_g, ln_mix_b,
               ln_ffn_g, ln_ffn_b]
    small_m = [m_hgrn_lb_logits, m_ffn_conv_w, m_hgrn_gnorm_w, m_swa_sinks, m_rel_bias, m_ffn_conv_b, m_ln_mix_g,
               m_ln_mix_b, m_ln_ffn_g, m_ln_ffn_b]
    small_v = [v_hgrn_lb_logits, v_ffn_conv_w, v_hgrn_gnorm_w, v_swa_sinks, v_rel_bias, v_ffn_conv_b, v_ln_mix_g,
               v_ln_mix_b, v_ln_ffn_g, v_ln_ffn_b]
    pg, lay = _pack_rows(small_g)
    pw, _ = _pack_rows(small_w)
    pm, _ = _pack_rows(small_m)
    pv, _ = _pack_rows(small_v)
    res = _adamw(pg[None], pw, pm, pv, name="adamw_small")
    unpacked = [_unpack_rows(r, lay) for r in res]
    for i, name_ in enumerate(small_names):
        put(name_, [unpacked[j][i] for j in range(4)])

    order = ["hgrn_w_in", "hgrn_lb_logits", "hgrn_gnorm_w", "hgrn_w_out", "swa_w_q", "swa_sinks", "swa_w_out",
             "shared_w_kv", "rel_bias", "ffn_w_in", "ffn_conv_w", "ffn_conv_b", "ffn_w_out", "ln_mix_g", "ln_mix_b",
             "ln_ffn_g", "ln_ffn_b"]
    result = [loss.reshape(()), dx]
    for kind in ("grad_", "delta_", "new_m_", "new_v_"):
        result += [outs[kind + n] for n in order]
    return tuple(result)
```

```python
import functools
import math

import numpy as np
import jax
import jax.numpy as jnp
from jax import lax
from jax.experimental import pallas as pl
from jax.experimental.pallas import tpu as pltpu

F32 = jnp.float32
MXU = jnp.bfloat16

N_DEV = 8
D_MODEL = 1024
DEPTH = 2
HG_HEADS = 8
HG_DIM = 128
HG_CHUNK = 64
SW_Q_HEADS = 16
SW_KV_HEADS = 4
SW_GROUP = 4
SW_HEAD_DIM = 64
SW_WINDOW = 128
REL_BUCKETS = 32
REL_MAX_DIST = 128
FFN_DIM = 2816
ALPHA = (2.0 * DEPTH) ** 0.25
LN_EPS = 1e-5
RMS_EPS = 1e-6
ADAM_LR = 0.001
ADAM_B1 = 0.9
ADAM_B2 = 0.999
ADAM_EPS = 1e-08
ADAM_WD = 0.01
ADAM_STEP = 10
EXP_CLAMP = 80.0
NEG_BIG = -1e30

SUBLANES = 8
LANES = 128
VMEM_LIMIT = 48 * 2 ** 20
TOKEN_TILE = 512
WIDE_TOKEN_TILE = 1024
RESIDENT_TOKEN_TILE = 256
REDUCE_TOKEN_TILE = 2048
GRAD_DTYPE = jnp.bfloat16


def _params(**kw):
    return pltpu.CompilerParams(vmem_limit_bytes=VMEM_LIMIT, **kw)


def _sigmoid(x):
    return 1.0 / (1.0 + jnp.exp(-x))


def _dot(a, b):
    return jnp.dot(a.astype(MXU), b.astype(MXU), preferred_element_type=F32)


def _dot_nt(a, b):
    return lax.dot_general(a.astype(MXU), b.astype(MXU), (((1,), (1,)), ((), ())), preferred_element_type=F32)


def _dot_tn(a, b):
    return lax.dot_general(a.astype(MXU), b.astype(MXU), (((0,), (0,)), ((), ())), preferred_element_type=F32)


def _trunc_bf16(x):
    bits = lax.bitcast_convert_type(x, jnp.int32)
    return lax.bitcast_convert_type(bits & jnp.int32(-65536), F32)


def _split3(x):
    hi = _trunc_bf16(x)
    r = x - hi
    mid = _trunc_bf16(r)
    lo = r - mid
    return hi.astype(jnp.bfloat16), mid.astype(jnp.bfloat16), lo.astype(jnp.bfloat16)


def _dot_hp(a, b, contract):
    def halves(x):
        hi = _trunc_bf16(x)
        return hi.astype(jnp.bfloat16), (x - hi).astype(jnp.bfloat16)

    ah, al = halves(a)
    bh, bl = halves(b)
    d = lambda p, q: lax.dot_general(p, q, (contract, ((), ())), preferred_element_type=F32)
    return d(ah, bh) + d(ah, bl) + d(al, bh)


def _exact_dot(m01, x):
    hi, mid, lo = _split3(x)
    d = lambda p: jnp.dot(m01, p, preferred_element_type=F32)
    return d(hi) + d(mid) + d(lo)


def _exact_dot_r(x, m01):
    hi, mid, lo = _split3(x)
    d = lambda p: jnp.dot(p, m01, preferred_element_type=F32)
    return d(hi) + d(mid) + d(lo)


def _mm_nn(a, w, *, name, res=None, res_scale=1.0, ln=None, out_dtype=F32, tm=None):
    nbk, T, kw = a.shape
    _, nbn, _, nw = w.shape
    tm = min(tm or TOKEN_TILE, T)
    has_res = res is not None
    assert ln is None or nbn == 1

    def body(*refs):
        refs = list(refs)
        a_ref, w_ref = refs[:2]
        pos = 2
        res_ref = None
        if has_res:
            res_ref = refs[pos]
            pos += 1
        if ln is not None:
            g_ref, b_ref = refs[pos:pos + 2]
            pos += 2
        o_ref = refs[pos]
        pos += 1
        if ln is not None:
            ob_ref, xh_ref, rs_ref = refs[pos:pos + 3]
        for n in range(nbn):
            y = _dot(a_ref[0], w_ref[0, n])
            for k in range(1, nbk):
                y = y + _dot(a_ref[k], w_ref[k, n])
            if has_res:
                y = y + res_scale * res_ref[n].astype(F32)
            if ln is None:
                o_ref[n] = y.astype(o_ref.dtype)
            else:
                mu = jnp.mean(y, axis=-1, keepdims=True)
                yc = y - mu
                var = jnp.mean(yc * yc, axis=-1, keepdims=True)
                rstd = lax.rsqrt(var + LN_EPS)
                xh = yc * rstd
                xh_ref[n] = xh
                rs_ref[...] = rstd
                h = xh * g_ref[...] + b_ref[...]
                o_ref[n] = h
                ob_ref[n] = h.astype(ob_ref.dtype)

    in_specs = [pl.BlockSpec((nbk, tm, kw), lambda i: (0, i, 0)),
                pl.BlockSpec((nbk, nbn, kw, nw), lambda i: (0, 0, 0, 0))]
    args = [a, w]
    if has_res:
        in_specs.append(pl.BlockSpec((nbn, tm, nw), lambda i: (0, i, 0)))
        args.append(res)
    if ln is not None:
        in_specs += [pl.BlockSpec((1, nw), lambda i: (0, 0))] * 2
        args += list(ln)
    out_spec = pl.BlockSpec((nbn, tm, nw), lambda i: (0, i, 0))
    out_shape = jax.ShapeDtypeStruct((nbn, T, nw), out_dtype)
    if ln is not None:
        out_specs = [out_spec, out_spec, out_spec, pl.BlockSpec((tm, 1), lambda i: (i, 0))]
        out_shape = [out_shape, jax.ShapeDtypeStruct((nbn, T, nw), MXU), jax.ShapeDtypeStruct((nbn, T, nw), F32),
                     jax.ShapeDtypeStruct((T, 1), F32)]
    else:
        out_specs = out_spec
    return pl.pallas_call(body, name=name, grid=(T // tm,), in_specs=in_specs, out_specs=out_specs,
                          out_shape=out_shape, compiler_params=_params())(*args)


def _same(n):
    return n


def _mm_nt(dy, w, *, name, res=None, res_scale=1.0, out_dtype=F32, tm=None, n_map=_same, behind=()):
    nbn, T, nw = dy.shape
    nbk, _, kw, _ = w.shape
    tm = min(tm or WIDE_TOKEN_TILE, T)
    has_res = res is not None

    def body(*refs):
        refs = list(refs)
        dy_ref, w_ref = refs[:2]
        pos = 2
        res_ref = None
        if has_res:
            res_ref = refs[pos]
            pos += 1
        pos += len(behind)
        o_ref = refs[pos]
        pos += 1
        acc_ref = refs[pos] if nbn > 1 else None
        n = pl.program_id(2)
        part = _dot_nt(dy_ref[...], w_ref[...])

        def finish(acc):
            y = acc
            if has_res:
                y = y + res_scale * res_ref[...].astype(F32)
            o_ref[...] = y.astype(o_ref.dtype)

        if nbn == 1:
            finish(part)
        else:
            @pl.when(n == 0)
            def _():
                acc_ref[...] = part

            @pl.when(n > 0)
            def _():
                acc_ref[...] += part

            @pl.when(n == nbn - 1)
            def _():
                finish(acc_ref[...])

    in_specs = [pl.BlockSpec((None, tm, nw), lambda i, k, n: (n, i, 0)),
                pl.BlockSpec((None, None, kw, nw), lambda i, k, n: (k, n_map(n), 0, 0))]
    args = [dy, w]
    if has_res:
        in_specs.append(pl.BlockSpec((None, tm, kw), lambda i, k, n: (k, i, 0)))
        args.append(res)
    in_specs += [pl.BlockSpec(memory_space=pl.ANY)] * len(behind)
    args += list(behind)
    scratch = [pltpu.VMEM((tm, kw), F32)] if nbn > 1 else []
    return pl.pallas_call(body, name=name, grid=(T // tm, nbk, nbn), in_specs=in_specs,
                          out_specs=pl.BlockSpec((None, tm, kw), lambda i, k, n: (k, i, 0)),
                          out_shape=jax.ShapeDtypeStruct((nbk, T, kw), out_dtype), scratch_shapes=scratch,
                          compiler_params=_params())(*args)


def _mm_nt_resident(dy, w, *, name, res=None, res_scale=1.0, tm=None, n_map=_same, behind=()):
    nbn, T, nw = dy.shape
    nbk, _, kw, _ = w.shape
    assert nbk == 1
    tm = min(tm or RESIDENT_TOKEN_TILE, T)
    has_res = res is not None

    def body(*refs):
        dy_ref, w_ref = refs[:2]
        res_ref = refs[2] if has_res else None
        o_ref = refs[-1]
        y = _dot_nt(dy_ref[0], w_ref[0, n_map(0)])
        for n in range(1, nbn):
            y = y + _dot_nt(dy_ref[n], w_ref[0, n_map(n)])
        if has_res:
            y = y + res_scale * res_ref[0].astype(F32)
        o_ref[0] = y.astype(o_ref.dtype)

    in_specs = [pl.BlockSpec((nbn, tm, nw), lambda i: (0, i, 0)),
                pl.BlockSpec(w.shape, lambda i: (0, 0, 0, 0))]
    args = [dy, w]
    if has_res:
        in_specs.append(pl.BlockSpec((1, tm, kw), lambda i: (0, i, 0)))
        args.append(res)
    in_specs += [pl.BlockSpec(memory_space=pl.ANY)] * len(behind)
    args += list(behind)
    return pl.pallas_call(body, name=name, grid=(T // tm,), in_specs=in_specs,
                          out_specs=pl.BlockSpec((1, tm, kw), lambda i: (0, i, 0)),
                          out_shape=jax.ShapeDtypeStruct((1, T, kw), F32),
                          compiler_params=_params())(*args)


def _mm_tn(a, dy, *, name, tm=None, n_map=_same):
    nbk, T, kw = a.shape
    nbn, _, nw = dy.shape
    tm = min(tm or REDUCE_TOKEN_TILE, T)
    nt = T // tm

    def body(a_ref, dy_ref, o_ref, acc_ref):
        i = pl.program_id(2)
        part = _dot_tn(a_ref[...], dy_ref[...])

        @pl.when(i == 0)
        def _():
            acc_ref[...] = part

        @pl.when(i > 0)
        def _():
            acc_ref[...] += part

        @pl.when(i == nt - 1)
        def _():
            o_ref[...] = acc_ref[...].astype(o_ref.dtype)

    return pl.pallas_call(body, name=name, grid=(nbk, nbn, nt),
                          in_specs=[pl.BlockSpec((None, tm, kw), lambda k, n, i: (k, i, 0)),
                                    pl.BlockSpec((None, tm, nw), lambda k, n, i: (n, i, 0))],
                          out_specs=pl.BlockSpec((None, None, kw, nw), lambda k, n, i: (k, n_map(n), 0, 0)),
                          out_shape=jax.ShapeDtypeStruct((nbk, nbn, kw, nw), GRAD_DTYPE),
                          scratch_shapes=[pltpu.VMEM((kw, nw), F32)],
                          compiler_params=_params())(a, dy)


def _ln_bwd(dh, xhat, rstd, g, *, name, tm=None, behind=()):
    _, T, D = dh.shape
    tm = min(tm or TOKEN_TILE, T)

    def body(dh_ref, xh_ref, rs_ref, g_ref, *rest):
        dy_ref, dg_ref, db_ref = rest[-3:]
        i = pl.program_id(0)
        dhv = dh_ref[...]
        xh = xh_ref[...]
        dxh = dhv * g_ref[...]
        m1 = jnp.mean(dxh, axis=-1, keepdims=True)
        m2 = jnp.mean(dxh * xh, axis=-1, keepdims=True)
        dy_ref[...] = rs_ref[...] * (dxh - m1 - xh * m2)
        dg = jnp.sum(dhv * xh, axis=0, keepdims=True)
        db = jnp.sum(dhv, axis=0, keepdims=True)

        @pl.when(i == 0)
        def _():
            dg_ref[...] = dg
            db_ref[...] = db

        @pl.when(i > 0)
        def _():
            dg_ref[...] += dg
            db_ref[...] += db

    tok = pl.BlockSpec((None, tm, D), lambda i: (0, i, 0))
    vec = pl.BlockSpec((1, D), lambda i: (0, 0))
    return pl.pallas_call(body, name=name, grid=(T // tm,),
                          in_specs=[tok, tok, pl.BlockSpec((tm, 1), lambda i: (i, 0)), vec]
                          + [pl.BlockSpec(memory_space=pl.ANY)] * len(behind),
                          out_specs=[tok, vec, vec],
                          out_shape=[jax.ShapeDtypeStruct((1, T, D), F32), jax.ShapeDtypeStruct((1, D), F32),
                                     jax.ShapeDtypeStruct((1, D), F32)],
                          compiler_params=_params())(dh, xhat, rstd, g, *behind)


def _loss_head(h, tgt, *, name, tm=None):
    _, T, D = h.shape
    tm = min(tm or TOKEN_TILE, T)

    def body(h_ref, t_ref, dh_ref, loss_ref):
        i = pl.program_id(0)
        err = h_ref[...] - t_ref[...]
        dh_ref[...] = err / D
        part = 0.5 * jnp.sum(jnp.mean(err * err, axis=-1, keepdims=True), axis=0, keepdims=True)

        @pl.when(i == 0)
        def _():
            loss_ref[...] = jnp.zeros_like(loss_ref) + part

        @pl.when(i > 0)
        def _():
            loss_ref[...] += part

    tok = pl.BlockSpec((None, tm, D), lambda i: (0, i, 0))
    return pl.pallas_call(body, name=name, grid=(T // tm,), in_specs=[tok, tok],
                          out_specs=[tok, pl.BlockSpec((SUBLANES, LANES), lambda i: (0, 0))],
                          out_shape=[jax.ShapeDtypeStruct((1, T, D), F32),
                                     jax.ShapeDtypeStruct((SUBLANES, LANES), F32)],
                          compiler_params=_params())(h, tgt)


def _shift_rows(ext, k, n, halo):
    if k == 0:
        return ext[halo:halo + n]
    return pltpu.roll(ext, k, axis=0)[halo:halo + n]


def _conv_rows(ext, cw_ref, n, halo):
    return (cw_ref[0:1, :] * _shift_rows(ext, 2, n, halo) + cw_ref[1:2, :] * _shift_rows(ext, 1, n, halo)
            + cw_ref[2:3, :] * ext[halo:halo + n] + cw_ref[3:4, :])


def _pair_map(n):
    return n // 2 + 4 * (n % 2)


def _ffn_up(hb, w_in, cw, *, name, tm=None):
    _, T, D = hb.shape
    _, nb, _, fb = w_in.shape
    half = nb // 2
    tm = min(tm or TOKEN_TILE, T)

    def body(h_ref, wa_ref, wb_ref, cwa_ref, cwb_ref, u_ref, ab_ref, act_ref, carry):
        @pl.when(pl.program_id(1) == 0)
        def _():
            carry[...] = jnp.zeros_like(carry)

        h = h_ref[...]
        conv = []
        for s, (w_ref, cw_ref) in enumerate(((wa_ref, cwa_ref), (wb_ref, cwb_ref))):
            u = _dot(h, w_ref[...]).astype(u_ref.dtype)
            u_ref[s] = u
            uf = u.astype(F32)
            ext = jnp.concatenate([carry[s], uf], axis=0)
            c = _conv_rows(ext, cw_ref, tm, SUBLANES)
            ab_ref[s] = c.astype(ab_ref.dtype)
            conv.append(c)
            carry[s] = uf[tm - SUBLANES:tm]
        a, b = conv
        act_ref[...] = (a * _sigmoid(a) * b).astype(act_ref.dtype)

    wspec = lambda off: pl.BlockSpec((None, None, D, fb), lambda p, i: (0, p + off, 0, 0))
    cws = lambda off: pl.BlockSpec((None, SUBLANES, fb), lambda p, i: (p + off, 0, 0))
    return pl.pallas_call(body, name=name, grid=(half, T // tm),
                          in_specs=[pl.BlockSpec((None, tm, D), lambda p, i: (0, i, 0)), wspec(0), wspec(half),
                                    cws(0), cws(half)],
                          out_specs=[pl.BlockSpec((None, 2, tm, fb), lambda p, i: (p, 0, i, 0)),
                                     pl.BlockSpec((None, 2, tm, fb), lambda p, i: (p, 0, i, 0)),
                                     pl.BlockSpec((None, tm, fb), lambda p, i: (p, i, 0))],
                          out_shape=[jax.ShapeDtypeStruct((half, 2, T, fb), MXU),
                                     jax.ShapeDtypeStruct((half, 2, T, fb), MXU),
                                     jax.ShapeDtypeStruct((half, T, fb), MXU)],
                          scratch_shapes=[pltpu.VMEM((2, SUBLANES, fb), F32)],
                          compiler_params=_params())(hb, w_in, w_in, cw, cw)


def _ffn_gate_bwd(dy, u, ab, w_out, cw, *, name, tm=None):
    _, T, D = dy.shape
    half, _, _, fb = u.shape
    tm = min(tm or TOKEN_TILE, T)
    nt = T // tm

    def body(dy_ref, u_ref, ab_ref, w_ref, cwa_ref, cwb_ref, du_ref, dwo_ref, dcw_ref, carry, acc):
        i = pl.program_id(1)

        @pl.when(i == 0)
        def _():
            carry[...] = jnp.zeros_like(carry)
            acc[...] = jnp.zeros_like(acc)
            dcw_ref[...] = jnp.zeros_like(dcw_ref)

        dyv = dy_ref[...]
        dact = _dot_nt(dyv, w_ref[...])
        a = ab_ref[0].astype(F32)
        b = ab_ref[1].astype(F32)
        sa = _sigmoid(a)
        silu = a * sa
        acc[...] += _dot_tn(silu * b, dyv)
        dcs = (dact * b * (sa * (1.0 + a * (1.0 - sa))), dact * silu)
        m = tm + SUBLANES
        rows = lax.broadcasted_iota(jnp.int32, (SUBLANES, fb), 0)
        for s, cw_ref in enumerate((cwa_ref, cwb_ref)):
            dc = dcs[s]
            nxt = jnp.concatenate([dc, carry[s]], axis=0)
            dc1 = pltpu.roll(nxt, m - 1, axis=0)[:tm]
            dc2 = pltpu.roll(nxt, m - 2, axis=0)[:tm]
            du_ref[s] = (cw_ref[2:3, :] * dc + cw_ref[1:2, :] * dc1 + cw_ref[0:1, :] * dc2).astype(du_ref.dtype)
            carry[s] = dc[0:SUBLANES]
            uf = u_ref[s].astype(F32)
            g0 = jnp.sum(dc2 * uf, axis=0, keepdims=True)
            g1 = jnp.sum(dc1 * uf, axis=0, keepdims=True)
            g2 = jnp.sum(dc * uf, axis=0, keepdims=True)
            g3 = jnp.sum(dc, axis=0, keepdims=True)
            dcw_ref[s] += jnp.where(rows == 0, g0, jnp.where(rows == 1, g1, jnp.where(rows == 2, g2,
                                                                                jnp.where(rows == 3, g3, 0.0))))

        @pl.when(i == nt - 1)
        def _():
            dwo_ref[...] = acc[...].astype(dwo_ref.dtype)

    rev = lambda i: nt - 1 - i
    cws = lambda off: pl.BlockSpec((None, SUBLANES, fb), lambda p, i: (p + off, 0, 0))
    pair = lambda: pl.BlockSpec((None, 2, tm, fb), lambda p, i: (p, 0, rev(i), 0))
    return pl.pallas_call(body, name=name, grid=(half, nt),
                          in_specs=[pl.BlockSpec((None, tm, D), lambda p, i: (0, rev(i), 0)), pair(), pair(),
                                    pl.BlockSpec((None, None, fb, D), lambda p, i: (p, 0, 0, 0)), cws(0), cws(half)],
                          out_specs=[pair(),
                                     pl.BlockSpec((None, None, fb, D), lambda p, i: (p, 0, 0, 0)),
                                     pl.BlockSpec((None, 2, SUBLANES, fb), lambda p, i: (p, 0, 0, 0))],
                          out_shape=[jax.ShapeDtypeStruct((half, 2, T, fb), MXU),
                                     jax.ShapeDtypeStruct((half, 1, fb, D), GRAD_DTYPE),
                                     jax.ShapeDtypeStruct((half, 2, SUBLANES, fb), F32)],
                          scratch_shapes=[pltpu.VMEM((2, SUBLANES, fb), F32), pltpu.VMEM((fb, D), F32)],
                          compiler_params=_params())(dy, u, ab, w_out, cw, cw)


def _tri(n, lower):
    r = lax.broadcasted_iota(jnp.int32, (n, n), 0)
    c = lax.broadcasted_iota(jnp.int32, (n, n), 1)
    return (r >= c) if lower else (r <= c)


def _hgrn_gates(zq, zf, lb):
    sq = _sigmoid(zq)
    sf = _sigmoid(zf)
    fg = lb + (1.0 - lb) * sf
    return zq * sq, sq, sf, fg, jnp.log(fg)


def _lb_of(lbl_ref, cols):
    return _sigmoid(lbl_ref[0:1, cols] - lbl_ref[1:2, cols])


def _ones_where(mask):
    return jnp.where(mask, 1.0, 0.0).astype(jnp.bfloat16)


def _hgrn_fwd(z, lbl, gw, *, name):
    _, T, zw = z.shape
    C = min(HG_CHUNK, T)
    nch = T // C
    hpb = zw // HG_DIM

    def body(z_ref, lbl_ref, gw_ref, og_ref, st_ref, s_scr, bc_scr, q_scr, k_scr):
        c = pl.program_id(0)

        @pl.when(c == 0)
        def _():
            s_scr[...] = jnp.zeros_like(s_scr)

        low = _tri(C, True)
        low01 = _ones_where(low)
        gwv = gw_ref[...]
        H = range(HG_HEADS)
        for blk in range(2):
            cols = slice(blk * zw, (blk + 1) * zw)
            qq, _, _, fg, lf = _hgrn_gates(z_ref[blk], z_ref[2 + blk], _lb_of(lbl_ref, cols))
            q_scr[:, cols] = qq
            k_scr[:, cols] = 1.0 - fg
            bc_scr[:, cols] = _exact_dot(low01, lf)
        col = lambda h: slice(h * HG_DIM, (h + 1) * HG_DIM)
        zcol = lambda part, h: (part + h // hpb, slice(None), slice((h % hpb) * HG_DIM, (h % hpb + 1) * HG_DIM))
        b = [bc_scr[:, col(h)] for h in H]
        bm = [bc_scr[C // 2 - 1:C // 2, col(h)] for h in H]
        bl = [bc_scr[C - 1:C, col(h)] for h in H]
        q_ = [q_scr[:, col(h)] for h in H]
        k_ = [k_scr[:, col(h)] for h in H]
        v_ = [z_ref[zcol(4, h)] for h in H]
        qt = [q_[h] * jnp.exp(jnp.minimum(b[h] - bm[h], EXP_CLAMP)) for h in H]
        kt = [k_[h] * jnp.exp(jnp.minimum(bm[h] - b[h], EXP_CLAMP)) for h in H]
        A = [jnp.where(low, _dot_nt(qt[h], kt[h]), 0.0) for h in H]
        for h in H:
            st_ref[h] = s_scr[h]
        o = [_dot_nt(q_[h] * jnp.exp(b[h]), s_scr[h]) + _dot(A[h], v_[h]) for h in H]
        for h in H:
            s_scr[h] = s_scr[h] * jnp.exp(bl[h]) + _dot_tn(v_[h], k_[h] * jnp.exp(bl[h] - b[h]))
        for h in H:
            g_h = z_ref[zcol(6, h)]
            r = lax.rsqrt(jnp.mean(o[h] * o[h], axis=-1, keepdims=True) + RMS_EPS)
            og_ref[:, col(h)] = (o[h] * r * gwv * (g_h * _sigmoid(g_h))).astype(og_ref.dtype)

    return pl.pallas_call(body, name=name, grid=(nch,),
                          in_specs=[pl.BlockSpec((8, C, zw), lambda c: (0, c, 0)),
                                    pl.BlockSpec((2, D_MODEL), lambda c: (0, 0)),
                                    pl.BlockSpec((1, HG_DIM), lambda c: (0, 0))],
                          out_specs=[pl.BlockSpec((None, C, D_MODEL), lambda c: (0, c, 0)),
                                     pl.BlockSpec((None, HG_HEADS, HG_DIM, HG_DIM), lambda c: (c, 0, 0, 0))],
                          out_shape=[jax.ShapeDtypeStruct((1, T, D_MODEL), MXU),
                                     jax.ShapeDtypeStruct((nch, HG_HEADS, HG_DIM, HG_DIM), F32)],
                          scratch_shapes=[pltpu.VMEM((HG_HEADS, HG_DIM, HG_DIM), F32)]
                          + [pltpu.VMEM((C, D_MODEL), F32)] * 3,
                          compiler_params=_params())(z, lbl, gw)


def _hgrn_bwd(z, dog, states, lbl, gw, *, name):
    _, T, zw = z.shape
    C = min(HG_CHUNK, T)
    nch = T // C
    hpb = zw // HG_DIM

    def body(z_ref, dog_ref, st0_ref, st1_ref, lbl_ref, gw_ref, dz_ref, dlb_ref, dgw_ref,
             d_scr, bc_scr, q_scr, sf_scr, fg_scr, x_scr):
        step = pl.program_id(0)

        @pl.when(step == 0)
        def _():
            d_scr[...] = jnp.zeros_like(d_scr)
            dlb_ref[...] = jnp.zeros_like(dlb_ref)
            dgw_ref[...] = jnp.zeros_like(dgw_ref)

        low = _tri(C, True)
        low01 = _ones_where(low)
        up01 = _ones_where(_tri(C, False))
        gwv = gw_ref[...]
        H = range(HG_HEADS)
        for blk in range(2):
            lbb = _lb_of(lbl_ref, slice(blk * zw, (blk + 1) * zw))
            qq, sq, sf, fg, lf = _hgrn_gates(z_ref[blk], z_ref[2 + blk], lbb)
            q_scr[:, blk * zw:(blk + 1) * zw] = qq
            sf_scr[:, blk * zw:(blk + 1) * zw] = sf
            fg_scr[:, blk * zw:(blk + 1) * zw] = fg
            bc_scr[:, blk * zw:(blk + 1) * zw] = _exact_dot(low01, lf)
        col = lambda h: slice(h * HG_DIM, (h + 1) * HG_DIM)
        zcol = lambda part, h: (part + h // hpb, slice(None), slice((h % hpb) * HG_DIM, (h % hpb + 1) * HG_DIM))
        b = [bc_scr[:, col(h)] for h in H]
        bm = [bc_scr[C // 2 - 1:C // 2, col(h)] for h in H]
        bl = [bc_scr[C - 1:C, col(h)] for h in H]
        q_ = [q_scr[:, col(h)] for h in H]
        k_ = [1.0 - fg_scr[:, col(h)] for h in H]
        v_ = [z_ref[zcol(4, h)] for h in H]
        eq = [jnp.exp(jnp.minimum(b[h] - bm[h], EXP_CLAMP)) for h in H]
        ek = [jnp.exp(jnp.minimum(bm[h] - b[h], EXP_CLAMP)) for h in H]
        eb = [jnp.exp(b[h]) for h in H]
        el = [jnp.exp(bl[h] - b[h]) for h in H]
        qt = [q_[h] * eq[h] for h in H]
        kt = [k_[h] * ek[h] for h in H]
        q0 = [q_[h] * eb[h] for h in H]
        kd = [k_[h] * el[h] for h in H]
        A = [jnp.where(low, _dot_nt(qt[h], kt[h]), 0.0) for h in H]
        o = [_dot_nt(q0[h], st0_ref[h]) + _dot(A[h], v_[h]) for h in H]
        do = []
        dgw_acc = jnp.zeros((1, HG_DIM), F32)
        for h in H:
            g_h = z_ref[zcol(6, h)]
            r = lax.rsqrt(jnp.mean(o[h] * o[h], axis=-1, keepdims=True) + RMS_EPS)
            on = o[h] * r
            sg = _sigmoid(g_h)
            dogh = dog_ref[:, col(h)].astype(F32)
            t1 = dogh * on
            dgw_acc = dgw_acc + jnp.sum(t1 * (g_h * sg), axis=0, keepdims=True)
            dz_ref[zcol(6, h)] = t1 * gwv * (sg * (1.0 + g_h * (1.0 - sg)))
            don = dogh * gwv * (g_h * sg)
            do.append(r * (don - on * jnp.mean(don * on, axis=-1, keepdims=True)))
        dgw_ref[...] += dgw_acc
        P = [jnp.where(low, _dot_nt(do[h], v_[h]), 0.0) for h in H]
        dqq = [eb[h] * _dot(do[h], st0_ref[h]) + eq[h] * _dot_hp(P[h], kt[h], ((1,), (0,))) for h in H]
        dkk = [el[h] * _dot(v_[h], d_scr[h]) + ek[h] * _dot_hp(P[h], qt[h], ((0,), (0,))) for h in H]
        for h in H:
            dz_ref[zcol(4, h)] = _dot_nt(kd[h], d_scr[h]) + _dot_tn(A[h], do[h])
            x_scr[:, col(h)] = q_[h] * dqq[h] - k_[h] * dkk[h]
        edge = [jnp.sum(d_scr[h] * st1_ref[h], axis=0, keepdims=True) for h in H]
        for h in H:
            d_scr[h] = d_scr[h] * jnp.exp(bl[h]) + _dot_tn(do[h], q0[h])
        for blk in range(2):
            x_scr[:, blk * zw:(blk + 1) * zw] = _exact_dot(up01, x_scr[:, blk * zw:(blk + 1) * zw])
        dlb = []
        for h in H:
            dfg = (x_scr[:, col(h)] + edge[h]) / fg_scr[:, col(h)] - dkk[h]
            sf_h = sf_scr[:, col(h)]
            lb_h = _lb_of(lbl_ref, col(h))
            zq_h = z_ref[zcol(0, h)]
            sq_h = _sigmoid(zq_h)
            dlb.append(jnp.sum(dfg * (1.0 - sf_h), axis=0, keepdims=True))
            dz_ref[zcol(0, h)] = dqq[h] * (sq_h * (1.0 + zq_h * (1.0 - sq_h)))
            dz_ref[zcol(2, h)] = dfg * (1.0 - lb_h) * sf_h * (1.0 - sf_h)
        dlb_ref[...] += jnp.concatenate(dlb, axis=1)

    rev = lambda s: nch - 1 - s
    return pl.pallas_call(body, name=name, grid=(nch,),
                          in_specs=[pl.BlockSpec((8, C, zw), lambda s: (0, rev(s), 0)),
                                    pl.BlockSpec((None, C, D_MODEL), lambda s: (0, rev(s), 0)),
                                    pl.BlockSpec((None, HG_HEADS, HG_DIM, HG_DIM), lambda s: (rev(s), 0, 0, 0)),
                                    pl.BlockSpec((None, HG_HEADS, HG_DIM, HG_DIM),
                                                 lambda s: (jnp.minimum(rev(s) + 1, nch - 1), 0, 0, 0)),
                                    pl.BlockSpec((2, D_MODEL), lambda s: (0, 0)),
                                    pl.BlockSpec((1, HG_DIM), lambda s: (0, 0))],
                          out_specs=[pl.BlockSpec((8, C, zw), lambda s: (0, rev(s), 0)),
                                     pl.BlockSpec((1, D_MODEL), lambda s: (0, 0)),
                                     pl.BlockSpec((1, HG_DIM), lambda s: (0, 0))],
                          out_shape=[jax.ShapeDtypeStruct((8, T, zw), F32), jax.ShapeDtypeStruct((1, D_MODEL), F32),
                                     jax.ShapeDtypeStruct((1, HG_DIM), F32)],
                          scratch_shapes=[pltpu.VMEM((HG_HEADS, HG_DIM, HG_DIM), F32)]
                          + [pltpu.VMEM((C, D_MODEL), F32)] * 5,
                          compiler_params=_params())(z, dog, states, states, lbl, gw)


def _bucket_onehot():
    W = SW_WINDOW
    t = np.arange(W)[:, None] + W
    s = np.arange(2 * W)[None, :]
    dist = t - s
    exact = REL_BUCKETS // 2
    d = np.maximum(np.maximum(dist, 0), 1).astype(np.float32)
    log_b = exact + (np.log(d / np.float32(exact)) / np.float32(math.log(REL_MAX_DIST / exact))
                     * np.float32(REL_BUCKETS - exact)).astype(np.int32)
    bucket = np.where(np.maximum(dist, 0) < exact, np.maximum(dist, 0), np.minimum(log_b, REL_BUCKETS - 1))
    valid = (dist >= 0) & (dist < W)
    onehot = (bucket[..., None] == np.arange(REL_BUCKETS)) & valid[..., None]
    return onehot.reshape(W * 2 * W, REL_BUCKETS).astype(np.float32)


def _bias_expand(rel_t, onehot_t, *, name):
    hq, nbk = rel_t.shape
    n = onehot_t.shape[1]

    def body(r_ref, oh_ref, o_ref):
        o_ref[...] = _exact_dot_r(r_ref[...], oh_ref[...])

    return pl.pallas_call(body, name=name, out_shape=jax.ShapeDtypeStruct((hq, n), F32),
                          compiler_params=_params())(rel_t, onehot_t)


def _bias_reduce(dbias, onehot, *, name):
    hq = dbias.shape[0]
    nbk = onehot.shape[1]

    def body(d_ref, oh_ref, o_ref):
        o_ref[...] = _exact_dot_r(d_ref[...], oh_ref[...])

    return pl.pallas_call(body, name=name, out_shape=jax.ShapeDtypeStruct((hq, nbk), F32),
                          compiler_params=_params())(dbias, onehot)


def _swa_mask(j):
    W = SW_WINDOW
    t = lax.broadcasted_iota(jnp.int32, (W, 2 * W), 0) + W
    s = lax.broadcasted_iota(jnp.int32, (W, 2 * W), 1)
    dist = t - s
    band = (dist >= 0) & (dist < W)
    m = band & ((j > 0) | (s >= W))
    return jnp.concatenate([m] * SW_GROUP, axis=0)


def _half_mask(rows, half):
    lane = lax.broadcasted_iota(jnp.int32, (rows, LANES), 1)
    return (lane >= SW_HEAD_DIM) if half else (lane < SW_HEAD_DIM)


def _swa_head(ref, col0, head, to_half):
    slab, half = head // 2, head % 2
    x = ref[:, col0 + slab * LANES:col0 + (slab + 1) * LANES]
    x = jnp.where(_half_mask(x.shape[0], half), x, 0.0)
    return x if half == to_half else pltpu.roll(x, SW_HEAD_DIM, axis=1)


def _swa_stack(ref, g):
    return jnp.concatenate([_swa_head(ref, 0, g * SW_GROUP + r, g % 2) for r in range(SW_GROUP)], axis=0)


def _swa_unstack(ref, x, g):
    W = SW_WINDOW
    for pair in range(SW_GROUP // 2):
        parts = []
        for r in (2 * pair, 2 * pair + 1):
            piece = x[r * W:(r + 1) * W]
            parts.append(piece if r % 2 == g % 2 else pltpu.roll(piece, SW_HEAD_DIM, axis=1))
        slab = (g * SW_GROUP) // 2 + pair
        ref[:, slab * LANES:(slab + 1) * LANES] = parts[0] + parts[1]


def _swa_kv(kp_ref, kc_ref, col0, g):
    return jnp.concatenate([_swa_head(kp_ref, col0, g, g % 2), _swa_head(kc_ref, col0, g, g % 2)], axis=0)


def _lane_pick(tile, h):
    lane = lax.broadcasted_iota(jnp.int32, tile.shape, 1)
    return jnp.sum(jnp.where(lane == h, tile, 0.0), axis=-1, keepdims=True)


def _lane_put(tile, h, col):
    lane = lax.broadcasted_iota(jnp.int32, tile.shape, 1)
    return jnp.where(lane == h, col, tile)


def _swa_rows(vals):
    return jnp.concatenate([jnp.broadcast_to(v, (SW_WINDOW, 1)) for v in vals], axis=0)


def _swa_fwd(q, kv, bias, sinks, *, name):
    _, T, D = q.shape
    W = SW_WINDOW
    nb = T // W
    dh = SW_HEAD_DIM
    kvw = SW_KV_HEADS * dh
    scale = dh ** -0.5

    def body(q_ref, kc_ref, kp_ref, bias_ref, sink_ref, o_ref, lse_ref):
        j = pl.program_id(0)
        mask = _swa_mask(j)
        sk = sink_ref[...]
        lse_tile = jnp.zeros((W, SW_Q_HEADS), F32)
        G = range(SW_KV_HEADS)
        kk = [_swa_kv(kp_ref, kc_ref, 0, g) for g in G]
        vv = [_swa_kv(kp_ref, kc_ref, kvw, g) for g in G]
        qs = [_swa_stack(q_ref, g) for g in G]
        logits = [jnp.where(mask, _dot_nt(qs[g], kk[g]) * scale
                            + bias_ref[g * SW_GROUP:(g + 1) * SW_GROUP].reshape(SW_GROUP * W, 2 * W), NEG_BIG) for g in G]
        sink = [_swa_rows([_lane_pick(sk, g * SW_GROUP + r) for r in range(SW_GROUP)]) for g in G]
        m = [jnp.maximum(jnp.max(logits[g], axis=-1, keepdims=True), sink[g]) for g in G]
        p = [jnp.exp(logits[g] - m[g]) for g in G]
        den = [jnp.sum(p[g], axis=-1, keepdims=True) + jnp.exp(sink[g] - m[g]) for g in G]
        pv = [_dot(p[g], vv[g]) for g in G]
        for g in G:
            _swa_unstack(o_ref, pv[g] / den[g], g)
            lse = m[g] + jnp.log(den[g])
            for r in range(SW_GROUP):
                lse_tile = _lane_put(lse_tile, g * SW_GROUP + r, lse[r * W:(r + 1) * W])
        lse_ref[...] = lse_tile

    return pl.pallas_call(body, name=name, grid=(nb,),
                          in_specs=[pl.BlockSpec((None, W, D), lambda j: (0, j, 0)),
                                    pl.BlockSpec((None, W, 2 * kvw), lambda j: (0, j, 0)),
                                    pl.BlockSpec((None, W, 2 * kvw), lambda j: (0, jnp.maximum(j - 1, 0), 0)),
                                    pl.BlockSpec((SW_Q_HEADS, W, 2 * W), lambda j: (0, 0, 0)),
                                    pl.BlockSpec((1, SW_Q_HEADS), lambda j: (0, 0))],
                          out_specs=[pl.BlockSpec((None, W, D), lambda j: (0, j, 0)),
                                     pl.BlockSpec((W, SW_Q_HEADS), lambda j: (j, 0))],
                          out_shape=[jax.ShapeDtypeStruct((1, T, D), F32), jax.ShapeDtypeStruct((T, SW_Q_HEADS), F32)],
                          compiler_params=_params())(q, kv, kv, bias, sinks)


def _swa_bwd(q, kv, o, lse, do, bias, sinks, *, name):
    _, T, D = q.shape
    W = SW_WINDOW
    nb = T // W
    dh = SW_HEAD_DIM
    kvw = SW_KV_HEADS * dh
    scale = dh ** -0.5
    cl = lambda j: jnp.minimum(j, nb - 1)

    def body(q_ref, kc_ref, kp_ref, o_ref, lse_ref, do_ref, bias_ref, sink_ref,
             dq_ref, dkv_ref, dbias_ref, dsink_ref, carry):
        j = pl.program_id(0)

        @pl.when(j == 0)
        def _():
            carry[...] = jnp.zeros_like(carry)
            dbias_ref[...] = jnp.zeros_like(dbias_ref)
            dsink_ref[...] = jnp.zeros_like(dsink_ref)

        @pl.when(j < nb)
        def _():
            mask = _swa_mask(j)
            sk = sink_ref[...]
            lse_tile = lse_ref[...]
            dsink = jnp.zeros((1, SW_Q_HEADS), F32)
            G = range(SW_KV_HEADS)
            heads = [[g * SW_GROUP + r for r in range(SW_GROUP)] for g in G]
            kk = [_swa_kv(kp_ref, kc_ref, 0, g) for g in G]
            vv = [_swa_kv(kp_ref, kc_ref, kvw, g) for g in G]
            qs = [_swa_stack(q_ref, g) for g in G]
            dos = [_swa_stack(do_ref, g) for g in G]
            lse = [jnp.concatenate([_lane_pick(lse_tile, h) for h in heads[g]], axis=0) for g in G]
            sink = [_swa_rows([_lane_pick(sk, h) for h in heads[g]]) for g in G]
            logits = [jnp.where(mask, _dot_nt(qs[g], kk[g]) * scale
                                + bias_ref[g * SW_GROUP:(g + 1) * SW_GROUP].reshape(SW_GROUP * W, 2 * W), NEG_BIG)
                      for g in G]
            dp = [_dot_nt(dos[g], vv[g]) for g in G]
            p = [jnp.exp(logits[g] - lse[g]) for g in G]
            delta = [jnp.sum(dos[g] * _swa_stack(o_ref, g), axis=-1, keepdims=True) for g in G]
            dl = [p[g] * (dp[g] - delta[g]) for g in G]
            dqs = [_dot(dl[g], kk[g]) * scale for g in G]
            dks = [_dot_tn(dl[g], qs[g]) * scale for g in G]
            dvs = [_dot_tn(p[g], dos[g]) for g in G]
            for g in G:
                _swa_unstack(dq_ref, dqs[g], g)
                dbias_ref[g * SW_GROUP:(g + 1) * SW_GROUP] += dl[g].reshape(SW_GROUP, W, 2 * W)
                sd = jnp.exp(sink[g] - lse[g]) * delta[g]
                for r, h in enumerate(heads[g]):
                    dsink = _lane_put(dsink, h, -jnp.sum(sd[r * W:(r + 1) * W], axis=0, keepdims=True))
            dsink_ref[...] += dsink
            for slab in range(SW_KV_HEADS // 2):
                for col0, parts in ((0, dks), (kvw, dvs)):
                    both = parts[2 * slab] + parts[2 * slab + 1]
                    cols = slice(col0 + slab * LANES, col0 + (slab + 1) * LANES)
                    dkv_ref[:, cols] = carry[:, cols] + both[:W]
                    carry[:, cols] = both[W:]

        @pl.when(j == nb)
        def _():
            dkv_ref[...] = carry[...]

    tok = lambda w: pl.BlockSpec((None, W, w), lambda j: (0, cl(j), 0))
    return pl.pallas_call(body, name=name, grid=(nb + 1,),
                          in_specs=[tok(D), tok(2 * kvw),
                                    pl.BlockSpec((None, W, 2 * kvw), lambda j: (0, jnp.maximum(cl(j) - 1, 0), 0)),
                                    tok(D), pl.BlockSpec((W, SW_Q_HEADS), lambda j: (cl(j), 0)), tok(D),
                                    pl.BlockSpec((SW_Q_HEADS, W, 2 * W), lambda j: (0, 0, 0)),
                                    pl.BlockSpec((1, SW_Q_HEADS), lambda j: (0, 0))],
                          out_specs=[tok(D),
                                     pl.BlockSpec((None, W, 2 * kvw), lambda j: (0, jnp.maximum(j - 1, 0), 0)),
                                     pl.BlockSpec((SW_Q_HEADS, W, 2 * W), lambda j: (0, 0, 0)),
                                     pl.BlockSpec((1, SW_Q_HEADS), lambda j: (0, 0))],
                          out_shape=[jax.ShapeDtypeStruct((1, T, D), F32), jax.ShapeDtypeStruct((1, T, 2 * kvw), F32),
                                     jax.ShapeDtypeStruct((SW_Q_HEADS, W, 2 * W), F32),
                                     jax.ShapeDtypeStruct((1, SW_Q_HEADS), F32)],
                          scratch_shapes=[pltpu.VMEM((W, 2 * kvw), F32)],
                          compiler_params=_params())(q, kv, kv, o, lse, do, bias, sinks)


_HBM = pl.BlockSpec(memory_space=pltpu.HBM)
_SEM = pl.BlockSpec(memory_space=pltpu.SEMAPHORE)
_EFFECT = pltpu.SideEffectType.DATAFLOW_SIDE_EFFECTING
N_PEERS = N_DEV - 1


def _peer(k):
    x, y, c = lax.axis_index("x"), lax.axis_index("y"), lax.axis_index("c")
    px = (x + (k >> 2)) % 2
    py = (y + ((k >> 1) & 1)) % 2
    pc = (c + (k & 1)) % 2
    return (px, py, pc), 4 * px + 2 * py + pc


def _my_number():
    return 4 * lax.axis_index("x") + 2 * lax.axis_index("y") + lax.axis_index("c")


def _landing(src, mode):
    me = _my_number()
    own = src if mode == "gather" else lax.dynamic_index_in_dim(src, me, 0, keepdims=False)
    return lax.dynamic_update_index_in_dim(lax.empty((N_DEV,) + own.shape, own.dtype), own, me, 0)


def _copy(src_ref, land_ref, mode, send, recv, j, k, dst_slot):
    peer, pid = _peer(k)
    return pltpu.make_async_remote_copy(
        src_ref=src_ref if mode == "gather" else src_ref.at[pid], dst_ref=land_ref.at[dst_slot(pid)],
        send_sem=send.at[j * N_PEERS + k - 1], recv_sem=recv.at[j * N_PEERS + k - 1],
        device_id=peer, device_id_type=pl.DeviceIdType.MESH)


def _send_start(groups, *, name):
    flat = [t for g in groups for t in g]
    n, ng = len(flat), len(groups)
    srcs = [pltpu.with_memory_space_constraint(s, pltpu.HBM) for s, _ in flat]
    lands = [pltpu.with_memory_space_constraint(_landing(s, m), pltpu.HBM) for s, m in flat]

    def body(*refs):
        src_refs, land_refs = refs[:n], refs[n:2 * n]
        sems = refs[2 * n:2 * n + 2 * ng]
        token = refs[-1]
        me = _my_number()
        a = 0
        for gi, g in enumerate(groups):
            for j, (_, mode) in enumerate(g):
                for k in range(1, N_DEV):
                    _copy(src_refs[a], land_refs[a], mode, sems[2 * gi], sems[2 * gi + 1], j, k, lambda pid: me).start()
                a += 1
        token[...] = jnp.zeros_like(token)

    sem_shapes = []
    for g in groups:
        sem_shapes += [pltpu.SemaphoreType.DMA((len(g) * N_PEERS,))] * 2
    out = pl.pallas_call(
        body, name=name,
        out_shape=tuple(sem_shapes) + tuple(pltpu.HBM(a.shape, a.dtype) for a in srcs + lands)
        + (jax.ShapeDtypeStruct((SUBLANES, LANES), F32),),
        in_specs=[_HBM] * (2 * n), out_specs=[_SEM] * (2 * ng) + [_HBM] * (2 * n) + [pl.BlockSpec(memory_space=pltpu.VMEM)],
        input_output_aliases={i: 2 * ng + i for i in range(2 * n)},
        compiler_params=pltpu.CompilerParams(has_side_effects=_EFFECT))(*srcs, *lands)
    sems, thru, token = out[:2 * ng], out[2 * ng:2 * ng + 2 * n], out[-1]
    handles, a = [], 0
    for gi, g in enumerate(groups):
        m = len(g)
        handles.append((sems[2 * gi], sems[2 * gi + 1], list(thru[a:a + m]), list(thru[n + a:n + a + m]),
                        [mode for _, mode in g]))
        a += m
    return handles, token


def _send_wait(handle, after, *, name):
    send, recv, srcs, lands, modes = handle
    m = len(srcs)

    def body(*refs):
        src_refs, land_refs = refs[:m], refs[m:2 * m]
        send_ref, recv_ref = refs[2 * m], refs[2 * m + 1]
        for j in range(m):
            for k in range(1, N_DEV):
                cp = _copy(src_refs[j], land_refs[j], modes[j], send_ref, recv_ref, j, k, lambda pid: pid)
                cp.wait_send()
                cp.wait_recv()

    out = pl.pallas_call(
        body, name=name, out_shape=tuple(pltpu.HBM(a.shape, a.dtype) for a in srcs + lands),
        in_specs=[_HBM] * (2 * m) + [_SEM, _SEM] + [pl.BlockSpec(memory_space=pl.ANY)] * len(after),
        out_specs=[_HBM] * (2 * m), input_output_aliases={i: i for i in range(2 * m)},
        compiler_params=pltpu.CompilerParams(has_side_effects=_EFFECT))(*srcs, *lands, send, recv, *after)
    return list(out[m:])


def _adam_math(w, g, m, v):
    m = ADAM_B1 * m + (1.0 - ADAM_B1) * g
    v = ADAM_B2 * v + (1.0 - ADAM_B2) * (g * g)
    m_hat = m / (1.0 - ADAM_B1 ** ADAM_STEP)
    v_hat = v / (1.0 - ADAM_B2 ** ADAM_STEP)
    delta = -ADAM_LR * (m_hat / (jnp.sqrt(v_hat) + ADAM_EPS) + ADAM_WD * w)
    return delta, m, v


def _adamw(parts, w, m, v, *, name, layer=None):
    S, R, C = parts.shape
    tr = R
    for cand in (256, 128, 64, 32, 16, 8):
        if R % cand == 0 and S * cand * C * 4 <= 4 * 2 ** 20:
            tr = cand
            break

    def body(p_ref, w_ref, m_ref, v_ref, g_ref, d_ref, nm_ref, nv_ref):
        g = p_ref[0].astype(F32)
        for s in range(1, S):
            g = g + p_ref[s].astype(F32)
        delta, nm, nv = _adam_math(w_ref[...], g, m_ref[...], v_ref[...])
        g_ref[...] = g
        d_ref[...] = delta
        nm_ref[...] = nm
        nv_ref[...] = nv

    if layer is None:
        wspec = pl.BlockSpec((tr, C), lambda i: (i, 0))
    else:
        wspec = pl.BlockSpec((None, tr, C), lambda i: (layer, i, 0))
    ospec = pl.BlockSpec((tr, C), lambda i: (i, 0))
    osh = jax.ShapeDtypeStruct((R, C), F32)
    return pl.pallas_call(body, name=name, grid=(R // tr,),
                          in_specs=[pl.BlockSpec((S, tr, C), lambda i: (0, i, 0)), wspec, wspec, wspec],
                          out_specs=[ospec] * 4, out_shape=[osh] * 4, compiler_params=_params())(parts, w, m, v)


def _sum_parts(parts, *, name):
    S, R, C = parts.shape

    def body(p_ref, o_ref):
        g = p_ref[0]
        for s in range(1, S):
            g = g + p_ref[s]
        o_ref[...] = g

    return pl.pallas_call(body, name=name, out_shape=jax.ShapeDtypeStruct((R, C), F32),
                          compiler_params=_params())(parts)


def _pack_rows(arrays):
    pieces, layout, row = [], [], 0
    for a in arrays:
        flat = a.reshape(-1).astype(F32)
        rows = -(-flat.shape[0] // (SUBLANES * LANES)) * SUBLANES
        flat = jnp.pad(flat, (0, rows * LANES - flat.shape[0]))
        pieces.append(flat.reshape(rows, LANES))
        layout.append((row, rows, a.shape))
        row += rows
    return jnp.concatenate(pieces, axis=0), layout


def _unpack_rows(packed, layout):
    out = []
    for row, rows, shape in layout:
        size = int(np.prod(shape))
        out.append(packed[row:row + rows].reshape(-1)[:size].reshape(shape))
    return out


def _ffn_fwd(h, w_in, w_out, cw, ln_g, ln_b, tag):
    h, hb = h
    u, ab, act = _ffn_up(hb, w_in, cw, name=f"ffn_up_{tag}")
    u = (u, ab)
    hn, hnb, xh, rs = _mm_nn(act, w_out, res=h, res_scale=ALPHA, ln=(ln_g, ln_b), name=f"ffn_down_{tag}")
    return (hn, hnb), xh, rs, u


def _ffn_bwd(dy, hb, u, w_in, w_out, cw, tag):
    du, dw_out, dcw = _ffn_gate_bwd(dy, u[0], u[1], w_out, cw, name=f"ffn_gate_bwd_{tag}")
    du = du.reshape((-1,) + du.shape[2:])
    dw_in = _mm_tn(hb, du, n_map=_pair_map, name=f"ffn_dwin_{tag}")
    dh = _mm_nt_resident(du, w_in, n_map=_pair_map, res=dy, res_scale=ALPHA, name=f"ffn_dh_{tag}")
    dcw = dcw.transpose(1, 0, 2, 3).reshape((-1,) + dcw.shape[2:])
    return dh, dw_in, dw_out, dcw


def kernel(x, hgrn_w_in, hgrn_lb_logits, hgrn_gnorm_w, hgrn_w_out, swa_w_q, swa_sinks, swa_w_out, shared_w_kv, rel_bias, ffn_w_in, ffn_conv_w, ffn_conv_b, ffn_w_out, ln_mix_g, ln_mix_b, ln_ffn_g, ln_ffn_b, loss_target, m_hgrn_w_in, m_hgrn_lb_logits, m_hgrn_gnorm_w, m_hgrn_w_out, m_swa_w_q, m_swa_sinks, m_swa_w_out, m_shared_w_kv, m_rel_bias, m_ffn_w_in, m_ffn_conv_w, m_ffn_conv_b, m_ffn_w_out, m_ln_mix_g, m_ln_mix_b, m_ln_ffn_g, m_ln_ffn_b, v_hgrn_w_in, v_hgrn_lb_logits, v_hgrn_gnorm_w, v_hgrn_w_out, v_swa_w_q, v_swa_sinks, v_swa_w_out, v_shared_w_kv, v_rel_bias, v_ffn_w_in, v_ffn_conv_w, v_ffn_conv_b, v_ffn_w_out, v_ln_mix_g, v_ln_mix_b, v_ln_ffn_g, v_ln_ffn_b):
    T = x.shape[1]
    D = D_MODEL
    W = SW_WINDOW
    fb = ffn_w_in.shape[2]
    me = 4 * lax.axis_index("x") + 2 * lax.axis_index("y") + lax.axis_index("c")

    small_fwd, small_fwd_layout = _pack_rows([hgrn_lb_logits, ffn_conv_w])
    gat = lambda *ws: [(w_.astype(MXU), "gather") for w_ in ws]
    (wait_a, wait_b, wait_c), _ = _send_start(
        [gat(hgrn_w_in[0]) + [(small_fwd, "gather")],
         gat(hgrn_w_out[0], ffn_w_in[0], ffn_w_out[0]),
         gat(shared_w_kv, swa_w_q[0], swa_w_out[0], ffn_w_in[1], ffn_w_out[1])], name="gather_start")
    w_hin, small_all = _send_wait(wait_a, (), name="gather_wait_a")
    w_hin = w_hin[None]
    ffn_rows = 2 * ffn_w_out.shape[1]
    (lb_row, lb_rows, _), (cw_row, cw_rows, _) = small_fwd_layout
    lbl = small_all[:, lb_row:lb_row + 2, :].transpose(1, 0, 2).reshape(2, D)
    conv_w_all = small_all[:, cw_row:cw_row + cw_rows, :].reshape(N_DEV, -1)[:, :DEPTH * 3 * fb]
    conv_w_all = conv_w_all.reshape(N_DEV, DEPTH, 3, fb).transpose(1, 0, 2, 3)
    conv_b_all = ffn_conv_b.reshape(DEPTH, N_DEV, 1, fb)
    no_pad = ((0, 0), (0, 0))
    cw = (jnp.pad(conv_w_all, no_pad + ((0, SUBLANES - 3), (0, 0)))
          + jnp.pad(conv_b_all, no_pad + ((3, SUBLANES - 4), (0, 0))))

    row = lambda a, l: a[l:l + 1]

    z = _mm_nn(x, w_hin, name="hgrn_in")
    og, states = _hgrn_fwd(z, lbl, hgrn_gnorm_w, name="hgrn_rec")
    w_hout, w_fin0, w_fout0 = _send_wait(wait_b, (og,), name="gather_wait_b")
    w_hout = w_hout.reshape(1, 1, D, D)
    w_fin = [w_fin0[None], None]
    w_fout = [w_fout0.reshape(4, 1, ffn_rows, D), None]
    h1, h1b, xh1, rs1 = _mm_nn(og, w_hout, res=x, res_scale=ALPHA, ln=(row(ln_mix_g, 0), row(ln_mix_b, 0)),
                               name="hgrn_out")
    (h2, h2b), xh2, rs2, u0 = _ffn_fwd((h1, h1b), w_fin[0], w_fout[0], cw[0], row(ln_ffn_g, 0), row(ln_ffn_b, 0), "l0")
    w_kv, w_q, w_o, w_fin1, w_fout1 = _send_wait(wait_c, (h2,), name="gather_wait_c")
    w_kv = w_kv.reshape(1, 1, D, 2 * SW_KV_HEADS * SW_HEAD_DIM)
    w_q = w_q.reshape(1, 1, D, D)
    w_o = w_o.reshape(1, 1, D, D)
    w_fin[1] = w_fin1[None]
    w_fout[1] = w_fout1.reshape(4, 1, ffn_rows, D)
    kv = _mm_nn(h2b, w_kv, name="swa_kv")
    q = _mm_nn(h2b, w_q, name="swa_q")
    onehot = _bucket_onehot()
    bias = _bias_expand(rel_bias.T, jnp.asarray(onehot.T, jnp.bfloat16), name="swa_bias").reshape(SW_Q_HEADS, W, 2 * W)
    ao, lse = _swa_fwd(q, kv, bias, swa_sinks, name="swa_attn")
    h3, h3b, xh3, rs3 = _mm_nn(ao, w_o, res=h2, res_scale=ALPHA, ln=(row(ln_mix_g, 1), row(ln_mix_b, 1)),
                               name="swa_out")
    (h4, _), xh4, rs4, u1 = _ffn_fwd((h3, h3b), w_fin[1], w_fout[1], cw[1], row(ln_ffn_g, 1), row(ln_ffn_b, 1), "l1")
    dh4, loss_tile = _loss_head(h4, loss_target, name="loss_head")

    dy4, dg_f1, db_f1 = _ln_bwd(dh4, xh4, rs4, row(ln_ffn_g, 1), name="ln_ffn1_bwd")
    sc = lambda *gs: [(g_, "scatter") for g_ in gs]
    dh3, dw_fin1, dw_fout1, dcw1 = _ffn_bwd(dy4, h3b, u1, w_fin[1], w_fout[1], cw[1], "l1")
    (ex1,), tok1 = _send_start([sc(dw_fin1.reshape(N_DEV, D, fb), dw_fout1.reshape(N_DEV, -1, D))],
                               name="grads_start_1")
    dy3, dg_m1, db_m1 = _ln_bwd(dh3, xh3, rs3, row(ln_mix_g, 1), name="ln_mix1_bwd", behind=(tok1,))
    dw_o = _mm_tn(ao, dy3, name="swa_dwo")
    dao = _mm_nt(dy3, w_o, name="swa_dao")
    dq, dkv, dbias, dsinks = _swa_bwd(q, kv, ao, lse, dao, bias, swa_sinks, name="swa_attn_bwd")
    drel_t = _bias_reduce(dbias.reshape(SW_Q_HEADS, W * 2 * W), jnp.asarray(onehot, jnp.bfloat16), name="swa_dbias")
    dw_q = _mm_tn(h2b, dq, name="swa_dwq")
    dw_kv = _mm_tn(h2b, dkv, name="swa_dwkv")
    dh2 = _mm_nt(dq, w_q, res=dy3, res_scale=ALPHA, name="swa_dh_q")
    dh2 = _mm_nt(dkv, w_kv, res=dh2, res_scale=1.0, name="swa_dh_kv")
    (ex2,), tok2 = _send_start([sc(dw_o.reshape(N_DEV, D // N_DEV, D), dw_q.reshape(N_DEV, D // N_DEV, D),
                                   dw_kv.reshape(N_DEV, D // N_DEV, -1))], name="grads_start_2")
    dy2, dg_f0, db_f0 = _ln_bwd(dh2, xh2, rs2, row(ln_ffn_g, 0), name="ln_ffn0_bwd", behind=(tok2,))
    dh1, dw_fin0, dw_fout0, dcw0 = _ffn_bwd(dy2, h1b, u0, w_fin[0], w_fout[0], cw[0], "l0")
    (ex3,), tok3 = _send_start([sc(dw_fin0.reshape(N_DEV, D, fb), dw_fout0.reshape(N_DEV, -1, D))],
                               name="grads_start_3")
    dy1, dg_m0, db_m0 = _ln_bwd(dh1, xh1, rs1, row(ln_mix_g, 0), name="ln_mix0_bwd", behind=(tok3,))
    dw_hout = _mm_tn(og, dy1, name="hgrn_dwout")
    dog = _mm_nt(dy1, w_hout, name="hgrn_dog")
    dz, dlb, dgw = _hgrn_bwd(z, dog, states, lbl, hgrn_gnorm_w, name="hgrn_rec_bwd")
    dw_hin = _mm_tn(x, dz, name="hgrn_dwin")

    p0 = _sigmoid(lbl[0:1] - lbl[1:2])
    dl0 = dlb * p0 * (1.0 - p0)
    d_lbl = dl0 * jnp.array([[1.0], [-1.0]], F32)
    dcw = jnp.stack([dcw0, dcw1], axis=0)
    d_conv_w = dcw[:, :, 0:3, :]
    d_conv_b = dcw[:, :, 3, :].reshape(DEPTH, N_DEV * fb)
    first_row = lax.broadcasted_iota(jnp.int32, (DEPTH, D), 0) == 0
    two_rows = lambda a, b: jnp.where(first_row, a, b)
    d_ln_mix_g = two_rows(dg_m0, dg_m1)
    d_ln_mix_b = two_rows(db_m0, db_m1)
    d_ln_ffn_g = two_rows(dg_f0, dg_f1)
    d_ln_ffn_b = two_rows(db_f0, db_f1)
    small_grads, small_layout = _pack_rows([d_lbl, d_conv_w, dgw, dsinks, drel_t.T, d_conv_b, d_ln_mix_g, d_ln_mix_b,
                                            d_ln_ffn_g, d_ln_ffn_b, loss_tile[0:1, 0:1]])

    (ex4,), tok4 = _send_start([sc(dw_hin.reshape(N_DEV, D, -1), dw_hout.reshape(N_DEV, D // N_DEV, D))
                                + [(small_grads, "gather")]], name="grads_start_4")
    dx = _mm_nt_resident(dz, w_hin, res=dy1, res_scale=ALPHA, name="hgrn_dx", behind=(tok4,))
    r_fin1, r_fout1 = _send_wait(ex1, (dx,), name="grads_wait_1")
    r_o, r_q, r_kv = _send_wait(ex2, (dx,), name="grads_wait_2")
    r_fin0, r_fout0 = _send_wait(ex3, (dx,), name="grads_wait_3")
    r_hin, r_hout, r_small = _send_wait(ex4, (dx,), name="grads_wait_4")
    received = [r_hin, r_hout, r_q, r_o, r_kv, r_fin0, r_fin1, r_fout0, r_fout1, r_small]

    outs = {}

    def put(name_, res):
        outs["grad_" + name_], outs["delta_" + name_], outs["new_m_" + name_], outs["new_v_" + name_] = res

    def big_update(name_, parts, w, m, v):
        shp = w.shape
        if w.ndim == 3 and shp[0] == 1:
            r = _adamw(parts, w[0], m[0], v[0], name="adamw_" + name_)
            put(name_, [a.reshape(shp) for a in r])
        else:
            r = _adamw(parts, w, m, v, name="adamw_" + name_)
            put(name_, r)

    big_update("hgrn_w_in", received[0], hgrn_w_in, m_hgrn_w_in, v_hgrn_w_in)
    big_update("hgrn_w_out", received[1], hgrn_w_out, m_hgrn_w_out, v_hgrn_w_out)
    big_update("swa_w_q", received[2], swa_w_q, m_swa_w_q, v_swa_w_q)
    big_update("swa_w_out", received[3], swa_w_out, m_swa_w_out, v_swa_w_out)
    big_update("shared_w_kv", received[4], shared_w_kv, m_shared_w_kv, v_shared_w_kv)
    for name_, idx, w, m, v in (("ffn_w_in", 5, ffn_w_in, m_ffn_w_in, v_ffn_w_in),
                                ("ffn_w_out", 7, ffn_w_out, m_ffn_w_out, v_ffn_w_out)):
        per_layer = [_adamw(received[idx + l], w, m, v, layer=l, name=f"adamw_{name_}_{l}") for l in range(DEPTH)]
        put(name_, [jnp.stack([per_layer[0][i], per_layer[1][i]], axis=0) for i in range(4)])

    small_sum = _sum_parts(received[9], name="sum_small_grads")
    (g_lbl, g_conv_w, g_gw, g_sinks, g_rel, g_conv_b, g_mix_g, g_mix_b, g_ffn_g, g_ffn_b,
     loss) = _unpack_rows(small_sum, small_layout)
    g_lbl_mine = lax.dynamic_slice_in_dim(g_lbl, me * (D // N_DEV), D // N_DEV, axis=1)
    g_conv_w_mine = lax.dynamic_index_in_dim(g_conv_w, me, axis=1, keepdims=False)
    small_names = ["hgrn_lb_logits", "ffn_conv_w", "hgrn_gnorm_w", "swa_sinks", "rel_bias", "ffn_conv_b",
                   "ln_mix_g", "ln_mix_b", "ln_ffn_g", "ln_ffn_b"]
    small_g = [g_lbl_mine, g_conv_w_mine, g_gw, g_sinks, g_rel, g_conv_b, g_mix_g, g_mix_b, g_ffn_g, g_ffn_b]
    small_w = [hgrn_lb_logits, ffn_conv_w, hgrn_gnorm_w, swa_sinks, rel_bias, ffn_conv_b, ln_mix_g, ln_mix_b,
               ln_ffn_g, ln_ffn_b]
    small_m = [m_hgrn_lb_logits, m_ffn_conv_w, m_hgrn_gnorm_w, m_swa_sinks, m_rel_bias, m_ffn_conv_b, m_ln_mix_g,
               m_ln_mix_b, m_ln_ffn_g, m_ln_ffn_b]
    small_v = [v_hgrn_lb_logits, v_ffn_conv_w, v_hgrn_gnorm_w, v_swa_sinks, v_rel_bias, v_ffn_conv_b, v_ln_mix_g,
               v_ln_mix_b, v_ln_ffn_g, v_ln_ffn_b]
    pg, lay = _pack_rows(small_g)
    pw, _ = _pack_rows(small_w)
    pm, _ = _pack_rows(small_m)
    pv, _ = _pack_rows(small_v)
    res = _adamw(pg[None], pw, pm, pv, name="adamw_small")
    unpacked = [_unpack_rows(r, lay) for r in res]
    for i, name_ in enumerate(small_names):
        put(name_, [unpacked[j][i] for j in range(4)])

    order = ["hgrn_w_in", "hgrn_lb_logits", "hgrn_gnorm_w", "hgrn_w_out", "swa_w_q", "swa_sinks", "swa_w_out",
             "shared_w_kv", "rel_bias", "ffn_w_in", "ffn_conv_w", "ffn_conv_b", "ffn_w_out", "ln_mix_g", "ln_mix_b",
             "ln_ffn_g", "ln_ffn_b"]
    result = [loss.reshape(()), dx]
    for kind in ("grad_", "delta_", "new_m_", "new_v_"):
        result += [outs[kind + n] for n in order]
    return tuple(result)
```

```python
import functools
import math

import numpy as np
import jax
import jax.numpy as jnp
from jax import lax
from jax.experimental import pallas as pl
from jax.experimental.pallas import tpu as pltpu

F32 = jnp.float32
MXU = jnp.bfloat16

N_DEV = 8
D_MODEL = 1024
DEPTH = 2
HG_HEADS = 8
HG_DIM = 128
HG_CHUNK = 64
SW_Q_HEADS = 16
SW_KV_HEADS = 4
SW_GROUP = 4
SW_HEAD_DIM = 64
SW_WINDOW = 128
REL_BUCKETS = 32
REL_MAX_DIST = 128
FFN_DIM = 2816
ALPHA = (2.0 * DEPTH) ** 0.25
LN_EPS = 1e-5
RMS_EPS = 1e-6
ADAM_LR = 0.001
ADAM_B1 = 0.9
ADAM_B2 = 0.999
ADAM_EPS = 1e-08
ADAM_WD = 0.01
ADAM_STEP = 10
EXP_CLAMP = 80.0
NEG_BIG = -1e30

SUBLANES = 8
LANES = 128
VMEM_LIMIT = 48 * 2 ** 20
TOKEN_TILE = 512
WIDE_TOKEN_TILE = 1024
RESIDENT_TOKEN_TILE = 256
REDUCE_TOKEN_TILE = 2048
GRAD_DTYPE = jnp.bfloat16


def _params(**kw):
    return pltpu.CompilerParams(vmem_limit_bytes=VMEM_LIMIT, **kw)


def _sigmoid(x):
    return 1.0 / (1.0 + jnp.exp(-x))


def _dot(a, b):
    return jnp.dot(a.astype(MXU), b.astype(MXU), preferred_element_type=F32)


def _dot_nt(a, b):
    return lax.dot_general(a.astype(MXU), b.astype(MXU), (((1,), (1,)), ((), ())), preferred_element_type=F32)


def _dot_tn(a, b):
    return lax.dot_general(a.astype(MXU), b.astype(MXU), (((0,), (0,)), ((), ())), preferred_element_type=F32)


def _trunc_bf16(x):
    bits = lax.bitcast_convert_type(x, jnp.int32)
    return lax.bitcast_convert_type(bits & jnp.int32(-65536), F32)


def _split3(x):
    hi = _trunc_bf16(x)
    r = x - hi
    mid = _trunc_bf16(r)
    lo = r - mid
    return hi.astype(jnp.bfloat16), mid.astype(jnp.bfloat16), lo.astype(jnp.bfloat16)


def _dot_hp(a, b, contract):
    def halves(x):
        hi = _trunc_bf16(x)
        return hi.astype(jnp.bfloat16), (x - hi).astype(jnp.bfloat16)

    ah, al = halves(a)
    bh, bl = halves(b)
    d = lambda p, q: lax.dot_general(p, q, (contract, ((), ())), preferred_element_type=F32)
    return d(ah, bh) + d(ah, bl) + d(al, bh)


def _exact_dot(m01, x):
    hi, mid, lo = _split3(x)
    d = lambda p: jnp.dot(m01, p, preferred_element_type=F32)
    return d(hi) + d(mid) + d(lo)


def _exact_dot_r(x, m01):
    hi, mid, lo = _split3(x)
    d = lambda p: jnp.dot(p, m01, preferred_element_type=F32)
    return d(hi) + d(mid) + d(lo)


def _mm_nn(a, w, *, name, res=None, res_scale=1.0, ln=None, out_dtype=F32, tm=None):
    nbk, T, kw = a.shape
    _, nbn, _, nw = w.shape
    tm = min(tm or TOKEN_TILE, T)
    has_res = res is not None
    assert ln is None or nbn == 1

    def body(*refs):
        refs = list(refs)
        a_ref, w_ref = refs[:2]
        pos = 2
        res_ref = None
        if has_res:
            res_ref = refs[pos]
            pos += 1
        if ln is not None:
            g_ref, b_ref = refs[pos:pos + 2]
            pos += 2
        o_ref = refs[pos]
        pos += 1
        if ln is not None:
            ob_ref, xh_ref, rs_ref = refs[pos:pos + 3]
        for n in range(nbn):
            y = _dot(a_ref[0], w_ref[0, n])
            for k in range(1, nbk):
                y = y + _dot(a_ref[k], w_ref[k, n])
            if has_res:
                y = y + res_scale * res_ref[n].astype(F32)
            if ln is None:
                o_ref[n] = y.astype(o_ref.dtype)
            else:
                mu = jnp.mean(y, axis=-1, keepdims=True)
                yc = y - mu
                var = jnp.mean(yc * yc, axis=-1, keepdims=True)
                rstd = lax.rsqrt(var + LN_EPS)
                xh = yc * rstd
                xh_ref[n] = xh
                rs_ref[...] = rstd
                h = xh * g_ref[...] + b_ref[...]
                o_ref[n] = h
                ob_ref[n] = h.astype(ob_ref.dtype)

    in_specs = [pl.BlockSpec((nbk, tm, kw), lambda i: (0, i, 0)),
                pl.BlockSpec((nbk, nbn, kw, nw), lambda i: (0, 0, 0, 0))]
    args = [a, w]
    if has_res:
        in_specs.append(pl.BlockSpec((nbn, tm, nw), lambda i: (0, i, 0)))
        args.append(res)
    if ln is not None:
        in_specs += [pl.BlockSpec((1, nw), lambda i: (0, 0))] * 2
        args += list(ln)
    out_spec = pl.BlockSpec((nbn, tm, nw), lambda i: (0, i, 0))
    out_shape = jax.ShapeDtypeStruct((nbn, T, nw), out_dtype)
    if ln is not None:
        out_specs = [out_spec, out_spec, out_spec, pl.BlockSpec((tm, 1), lambda i: (i, 0))]
        out_shape = [out_shape, jax.ShapeDtypeStruct((nbn, T, nw), MXU), jax.ShapeDtypeStruct((nbn, T, nw), F32),
                     jax.ShapeDtypeStruct((T, 1), F32)]
    else:
        out_specs = out_spec
    return pl.pallas_call(body, name=name, grid=(T // tm,), in_specs=in_specs, out_specs=out_specs,
                          out_shape=out_shape, compiler_params=_params())(*args)


def _same(n):
    return n


def _mm_nt(dy, w, *, name, res=None, res_scale=1.0, out_dtype=F32, tm=None, n_map=_same, behind=()):
    nbn, T, nw = dy.shape
    nbk, _, kw, _ = w.shape
    tm = min(tm or WIDE_TOKEN_TILE, T)
    has_res = res is not None

    def body(*refs):
        refs = list(refs)
        dy_ref, w_ref = refs[:2]
        pos = 2
        res_ref = None
        if has_res:
            res_ref = refs[pos]
            pos += 1
        pos += len(behind)
        o_ref = refs[pos]
        pos += 1
        acc_ref = refs[pos] if nbn > 1 else None
        n = pl.program_id(2)
        part = _dot_nt(dy_ref[...], w_ref[...])

        def finish(acc):
            y = acc
            if has_res:
                y = y + res_scale * res_ref[...].astype(F32)
            o_ref[...] = y.astype(o_ref.dtype)

        if nbn == 1:
            finish(part)
        else:
            @pl.when(n == 0)
            def _():
                acc_ref[...] = part

            @pl.when(n > 0)
            def _():
                acc_ref[...] += part

            @pl.when(n == nbn - 1)
            def _():
                finish(acc_ref[...])

    in_specs = [pl.BlockSpec((None, tm, nw), lambda i, k, n: (n, i, 0)),
                pl.BlockSpec((None, None, kw, nw), lambda i, k, n: (k, n_map(n), 0, 0))]
    args = [dy, w]
    if has_res:
        in_specs.append(pl.BlockSpec((None, tm, kw), lambda i, k, n: (k, i, 0)))
        args.append(res)
    in_specs += [pl.BlockSpec(memory_space=pl.ANY)] * len(behind)
    args += list(behind)
    scratch = [pltpu.VMEM((tm, kw), F32)] if nbn > 1 else []
    return pl.pallas_call(body, name=name, grid=(T // tm, nbk, nbn), in_specs=in_specs,
                          out_specs=pl.BlockSpec((None, tm, kw), lambda i, k, n: (k, i, 0)),
                          out_shape=jax.ShapeDtypeStruct((nbk, T, kw), out_dtype), scratch_shapes=scratch,
                          compiler_params=_params())(*args)


def _mm_nt_resident(dy, w, *, name, res=None, res_scale=1.0, ln_bwd=None, tm=None, n_map=_same, behind=()):
    nbn, T, nw = dy.shape
    nbk, _, kw, _ = w.shape
    assert nbk == 1
    tm = min(tm or RESIDENT_TOKEN_TILE, T)
    has_res = res is not None
    n_in = 2 + has_res + (3 if ln_bwd else 0) + len(behind)

    def body(*refs):
        dy_ref, w_ref = refs[:2]
        res_ref = refs[2] if has_res else None
        y = _dot_nt(dy_ref[0], w_ref[0, n_map(0)])
        for n in range(1, nbn):
            y = y + _dot_nt(dy_ref[n], w_ref[0, n_map(n)])
        if has_res:
            y = y + res_scale * res_ref[0].astype(F32)
        if ln_bwd is None:
            refs[n_in][0] = y
        else:
            xh_ref, rs_ref, g_ref = refs[2 + has_res:5 + has_res]
            o_ref, dg_ref, db_ref = refs[n_in:n_in + 3]
            out, dg, db = _ln_bwd_rows(y, xh_ref[0], rs_ref[...], g_ref[...])
            o_ref[0] = out
            _accumulate(pl.program_id(0), (dg_ref, db_ref), (dg, db))

    tok = pl.BlockSpec((1, tm, kw), lambda i: (0, i, 0))
    vec = pl.BlockSpec((1, kw), lambda i: (0, 0))
    in_specs = [pl.BlockSpec((nbn, tm, nw), lambda i: (0, i, 0)),
                pl.BlockSpec(w.shape, lambda i: (0, 0, 0, 0))]
    args = [dy, w]
    if has_res:
        in_specs.append(tok)
        args.append(res)
    out_specs, out_shape = tok, jax.ShapeDtypeStruct((1, T, kw), F32)
    if ln_bwd is not None:
        in_specs += [tok, pl.BlockSpec((tm, 1), lambda i: (i, 0)), vec]
        args += list(ln_bwd)
        out_specs = [tok, vec, vec]
        out_shape = [out_shape, jax.ShapeDtypeStruct((1, kw), F32), jax.ShapeDtypeStruct((1, kw), F32)]
    in_specs += [pl.BlockSpec(memory_space=pl.ANY)] * len(behind)
    args += list(behind)
    return pl.pallas_call(body, name=name, grid=(T // tm,), in_specs=in_specs, out_specs=out_specs,
                          out_shape=out_shape, compiler_params=_params())(*args)


def _mm_tn(a, dy, *, name, tm=None, n_map=_same):
    nbk, T, kw = a.shape
    nbn, _, nw = dy.shape
    tm = min(tm or REDUCE_TOKEN_TILE, T)
    nt = T // tm

    def body(a_ref, dy_ref, o_ref, acc_ref):
        i = pl.program_id(2)
        part = _dot_tn(a_ref[...], dy_ref[...])

        @pl.when(i == 0)
        def _():
            acc_ref[...] = part

        @pl.when(i > 0)
        def _():
            acc_ref[...] += part

        @pl.when(i == nt - 1)
        def _():
            o_ref[...] = acc_ref[...].astype(o_ref.dtype)

    return pl.pallas_call(body, name=name, grid=(nbk, nbn, nt),
                          in_specs=[pl.BlockSpec((None, tm, kw), lambda k, n, i: (k, i, 0)),
                                    pl.BlockSpec((None, tm, nw), lambda k, n, i: (n, i, 0))],
                          out_specs=pl.BlockSpec((None, None, kw, nw), lambda k, n, i: (k, n_map(n), 0, 0)),
                          out_shape=jax.ShapeDtypeStruct((nbk, nbn, kw, nw), GRAD_DTYPE),
                          scratch_shapes=[pltpu.VMEM((kw, nw), F32)],
                          compiler_params=_params())(a, dy)


def _ln_bwd_rows(dh, xh, rstd, g):
    dxh = dh * g
    m1 = jnp.mean(dxh, axis=-1, keepdims=True)
    m2 = jnp.mean(dxh * xh, axis=-1, keepdims=True)
    dy = rstd * (dxh - m1 - xh * m2)
    return dy, jnp.sum(dh * xh, axis=0, keepdims=True), jnp.sum(dh, axis=0, keepdims=True)


def _accumulate(i, refs, parts):
    @pl.when(i == 0)
    def _():
        for r, p in zip(refs, parts):
            r[...] = jnp.zeros_like(r) + p

    @pl.when(i > 0)
    def _():
        for r, p in zip(refs, parts):
            r[...] += p


def _loss_ln_bwd(h, tgt, xhat, rstd, g, *, name, tm=None):
    _, T, D = h.shape
    tm = min(tm or TOKEN_TILE, T)

    def body(h_ref, t_ref, xh_ref, rs_ref, g_ref, dy_ref, dg_ref, db_ref, loss_ref):
        i = pl.program_id(0)
        err = h_ref[...] - t_ref[...]
        part = 0.5 * jnp.sum(jnp.mean(err * err, axis=-1, keepdims=True), axis=0, keepdims=True)
        dy, dg, db = _ln_bwd_rows(err / D, xh_ref[...], rs_ref[...], g_ref[...])
        dy_ref[...] = dy
        _accumulate(i, (dg_ref, db_ref, loss_ref), (dg, db, part))

    tok = pl.BlockSpec((None, tm, D), lambda i: (0, i, 0))
    vec = pl.BlockSpec((1, D), lambda i: (0, 0))
    return pl.pallas_call(body, name=name, grid=(T // tm,),
                          in_specs=[tok, tok, tok, pl.BlockSpec((tm, 1), lambda i: (i, 0)), vec],
                          out_specs=[tok, vec, vec, pl.BlockSpec((SUBLANES, LANES), lambda i: (0, 0))],
                          out_shape=[jax.ShapeDtypeStruct((1, T, D), F32), jax.ShapeDtypeStruct((1, D), F32),
                                     jax.ShapeDtypeStruct((1, D), F32), jax.ShapeDtypeStruct((SUBLANES, LANES), F32)],
                          compiler_params=_params())(h, tgt, xhat, rstd, g)


def _shift_rows(ext, k, n, halo):
    if k == 0:
        return ext[halo:halo + n]
    return pltpu.roll(ext, k, axis=0)[halo:halo + n]


def _conv_rows(ext, cw_ref, n, halo):
    return (cw_ref[0:1, :] * _shift_rows(ext, 2, n, halo) + cw_ref[1:2, :] * _shift_rows(ext, 1, n, halo)
            + cw_ref[2:3, :] * ext[halo:halo + n] + cw_ref[3:4, :])


def _pair_map(n):
    return n // 2 + 4 * (n % 2)


def _ffn_up(hb, w_in, cw, *, name, tm=None):
    _, T, D = hb.shape
    _, nb, _, fb = w_in.shape
    half = nb // 2
    tm = min(tm or TOKEN_TILE, T)

    def body(h_ref, wa_ref, wb_ref, cwa_ref, cwb_ref, u_ref, ab_ref, act_ref, carry):
        @pl.when(pl.program_id(1) == 0)
        def _():
            carry[...] = jnp.zeros_like(carry)

        h = h_ref[...]
        conv = []
        for s, (w_ref, cw_ref) in enumerate(((wa_ref, cwa_ref), (wb_ref, cwb_ref))):
            u = _dot(h, w_ref[...]).astype(u_ref.dtype)
            u_ref[s] = u
            uf = u.astype(F32)
            ext = jnp.concatenate([carry[s], uf], axis=0)
            c = _conv_rows(ext, cw_ref, tm, SUBLANES)
            ab_ref[s] = c.astype(ab_ref.dtype)
            conv.append(c)
            carry[s] = uf[tm - SUBLANES:tm]
        a, b = conv
        act_ref[...] = (a * _sigmoid(a) * b).astype(act_ref.dtype)

    wspec = lambda off: pl.BlockSpec((None, None, D, fb), lambda p, i: (0, p + off, 0, 0))
    cws = lambda off: pl.BlockSpec((None, SUBLANES, fb), lambda p, i: (p + off, 0, 0))
    return pl.pallas_call(body, name=name, grid=(half, T // tm),
                          in_specs=[pl.BlockSpec((None, tm, D), lambda p, i: (0, i, 0)), wspec(0), wspec(half),
                                    cws(0), cws(half)],
                          out_specs=[pl.BlockSpec((None, 2, tm, fb), lambda p, i: (p, 0, i, 0)),
                                     pl.BlockSpec((None, 2, tm, fb), lambda p, i: (p, 0, i, 0)),
                                     pl.BlockSpec((None, tm, fb), lambda p, i: (p, i, 0))],
                          out_shape=[jax.ShapeDtypeStruct((half, 2, T, fb), MXU),
                                     jax.ShapeDtypeStruct((half, 2, T, fb), MXU),
                                     jax.ShapeDtypeStruct((half, T, fb), MXU)],
                          scratch_shapes=[pltpu.VMEM((2, SUBLANES, fb), F32)],
                          compiler_params=_params())(hb, w_in, w_in, cw, cw)


def _ffn_gate_bwd(dy, u, ab, w_out, cw, *, name, tm=None):
    _, T, D = dy.shape
    half, _, _, fb = u.shape
    tm = min(tm or TOKEN_TILE, T)
    nt = T // tm

    def body(dy_ref, u_ref, ab_ref, w_ref, cwa_ref, cwb_ref, du_ref, dwo_ref, dcw_ref, carry, acc):
        i = pl.program_id(1)

        @pl.when(i == 0)
        def _():
            carry[...] = jnp.zeros_like(carry)
            acc[...] = jnp.zeros_like(acc)
            dcw_ref[...] = jnp.zeros_like(dcw_ref)

        dyv = dy_ref[...]
        dact = _dot_nt(dyv, w_ref[...])
        a = ab_ref[0].astype(F32)
        b = ab_ref[1].astype(F32)
        sa = _sigmoid(a)
        silu = a * sa
        acc[...] += _dot_tn(silu * b, dyv)
        dcs = (dact * b * (sa * (1.0 + a * (1.0 - sa))), dact * silu)
        m = tm + SUBLANES
        rows = lax.broadcasted_iota(jnp.int32, (SUBLANES, fb), 0)
        for s, cw_ref in enumerate((cwa_ref, cwb_ref)):
            dc = dcs[s]
            nxt = jnp.concatenate([dc, carry[s]], axis=0)
            dc1 = pltpu.roll(nxt, m - 1, axis=0)[:tm]
            dc2 = pltpu.roll(nxt, m - 2, axis=0)[:tm]
            du_ref[s] = (cw_ref[2:3, :] * dc + cw_ref[1:2, :] * dc1 + cw_ref[0:1, :] * dc2).astype(du_ref.dtype)
            carry[s] = dc[0:SUBLANES]
            uf = u_ref[s].astype(F32)
            g0 = jnp.sum(dc2 * uf, axis=0, keepdims=True)
            g1 = jnp.sum(dc1 * uf, axis=0, keepdims=True)
            g2 = jnp.sum(dc * uf, axis=0, keepdims=True)
            g3 = jnp.sum(dc, axis=0, keepdims=True)
            dcw_ref[s] += jnp.where(rows == 0, g0, jnp.where(rows == 1, g1, jnp.where(rows == 2, g2,
                                                                                jnp.where(rows == 3, g3, 0.0))))

        @pl.when(i == nt - 1)
        def _():
            dwo_ref[...] = acc[...].astype(dwo_ref.dtype)

    rev = lambda i: nt - 1 - i
    cws = lambda off: pl.BlockSpec((None, SUBLANES, fb), lambda p, i: (p + off, 0, 0))
    pair = lambda: pl.BlockSpec((None, 2, tm, fb), lambda p, i: (p, 0, rev(i), 0))
    return pl.pallas_call(body, name=name, grid=(half, nt),
                          in_specs=[pl.BlockSpec((None, tm, D), lambda p, i: (0, rev(i), 0)), pair(), pair(),
                                    pl.BlockSpec((None, None, fb, D), lambda p, i: (p, 0, 0, 0)), cws(0), cws(half)],
                          out_specs=[pair(),
                                     pl.BlockSpec((None, None, fb, D), lambda p, i: (p, 0, 0, 0)),
                                     pl.BlockSpec((None, 2, SUBLANES, fb), lambda p, i: (p, 0, 0, 0))],
                          out_shape=[jax.ShapeDtypeStruct((half, 2, T, fb), MXU),
                                     jax.ShapeDtypeStruct((half, 1, fb, D), GRAD_DTYPE),
                                     jax.ShapeDtypeStruct((half, 2, SUBLANES, fb), F32)],
                          scratch_shapes=[pltpu.VMEM((2, SUBLANES, fb), F32), pltpu.VMEM((fb, D), F32)],
                          compiler_params=_params())(dy, u, ab, w_out, cw, cw)


def _tri(n, lower):
    r = lax.broadcasted_iota(jnp.int32, (n, n), 0)
    c = lax.broadcasted_iota(jnp.int32, (n, n), 1)
    return (r >= c) if lower else (r <= c)


def _hgrn_gates(zq, zf, lb):
    sq = _sigmoid(zq)
    sf = _sigmoid(zf)
    fg = lb + (1.0 - lb) * sf
    return zq * sq, sq, sf, fg, jnp.log(fg)


def _lb_of(lbl_ref, cols):
    return _sigmoid(lbl_ref[0:1, cols] - lbl_ref[1:2, cols])


def _ones_where(mask):
    return jnp.where(mask, 1.0, 0.0).astype(jnp.bfloat16)


def _hgrn_fwd(z, lbl, gw, *, name):
    _, T, zw = z.shape
    C = min(HG_CHUNK, T)
    nch = T // C
    hpb = zw // HG_DIM

    def body(z_ref, lbl_ref, gw_ref, og_ref, st_ref, s_scr, bc_scr, q_scr, k_scr):
        c = pl.program_id(0)

        @pl.when(c == 0)
        def _():
            s_scr[...] = jnp.zeros_like(s_scr)

        low = _tri(C, True)
        low01 = _ones_where(low)
        gwv = gw_ref[...]
        H = range(HG_HEADS)
        for blk in range(2):
            cols = slice(blk * zw, (blk + 1) * zw)
            qq, _, _, fg, lf = _hgrn_gates(z_ref[blk], z_ref[2 + blk], _lb_of(lbl_ref, cols))
            q_scr[:, cols] = qq
            k_scr[:, cols] = 1.0 - fg
            bc_scr[:, cols] = _exact_dot(low01, lf)
        col = lambda h: slice(h * HG_DIM, (h + 1) * HG_DIM)
        zcol = lambda part, h: (part + h // hpb, slice(None), slice((h % hpb) * HG_DIM, (h % hpb + 1) * HG_DIM))
        b = [bc_scr[:, col(h)] for h in H]
        bm = [bc_scr[C // 2 - 1:C // 2, col(h)] for h in H]
        bl = [bc_scr[C - 1:C, col(h)] for h in H]
        q_ = [q_scr[:, col(h)] for h in H]
        k_ = [k_scr[:, col(h)] for h in H]
        v_ = [z_ref[zcol(4, h)] for h in H]
        qt = [q_[h] * jnp.exp(jnp.minimum(b[h] - bm[h], EXP_CLAMP)) for h in H]
        kt = [k_[h] * jnp.exp(jnp.minimum(bm[h] - b[h], EXP_CLAMP)) for h in H]
        A = [jnp.where(low, _dot_nt(qt[h], kt[h]), 0.0) for h in H]
        for h in H:
            st_ref[h] = s_scr[h]
        o = [_dot_nt(q_[h] * jnp.exp(b[h]), s_scr[h]) + _dot(A[h], v_[h]) for h in H]
        for h in H:
            s_scr[h] = s_scr[h] * jnp.exp(bl[h]) + _dot_tn(v_[h], k_[h] * jnp.exp(bl[h] - b[h]))
        for h in H:
            g_h = z_ref[zcol(6, h)]
            r = lax.rsqrt(jnp.mean(o[h] * o[h], axis=-1, keepdims=True) + RMS_EPS)
            og_ref[:, col(h)] = (o[h] * r * gwv * (g_h * _sigmoid(g_h))).astype(og_ref.dtype)

    return pl.pallas_call(body, name=name, grid=(nch,),
                          in_specs=[pl.BlockSpec((8, C, zw), lambda c: (0, c, 0)),
                                    pl.BlockSpec((2, D_MODEL), lambda c: (0, 0)),
                                    pl.BlockSpec((1, HG_DIM), lambda c: (0, 0))],
                          out_specs=[pl.BlockSpec((None, C, D_MODEL), lambda c: (0, c, 0)),
                                     pl.BlockSpec((None, HG_HEADS, HG_DIM, HG_DIM), lambda c: (c, 0, 0, 0))],
                          out_shape=[jax.ShapeDtypeStruct((1, T, D_MODEL), MXU),
                                     jax.ShapeDtypeStruct((nch, HG_HEADS, HG_DIM, HG_DIM), F32)],
                          scratch_shapes=[pltpu.VMEM((HG_HEADS, HG_DIM, HG_DIM), F32)]
                          + [pltpu.VMEM((C, D_MODEL), F32)] * 3,
                          compiler_params=_params())(z, lbl, gw)


def _hgrn_bwd(z, dog, states, lbl, gw, *, name):
    _, T, zw = z.shape
    C = min(HG_CHUNK, T)
    nch = T // C
    hpb = zw // HG_DIM

    def body(z_ref, dog_ref, st0_ref, st1_ref, lbl_ref, gw_ref, dz_ref, dlb_ref, dgw_ref,
             d_scr, bc_scr, q_scr, sf_scr, fg_scr, x_scr):
        step = pl.program_id(0)

        @pl.when(step == 0)
        def _():
            d_scr[...] = jnp.zeros_like(d_scr)
            dlb_ref[...] = jnp.zeros_like(dlb_ref)
            dgw_ref[...] = jnp.zeros_like(dgw_ref)

        low = _tri(C, True)
        low01 = _ones_where(low)
        up01 = _ones_where(_tri(C, False))
        gwv = gw_ref[...]
        H = range(HG_HEADS)
        for blk in range(2):
            lbb = _lb_of(lbl_ref, slice(blk * zw, (blk + 1) * zw))
            qq, sq, sf, fg, lf = _hgrn_gates(z_ref[blk], z_ref[2 + blk], lbb)
            q_scr[:, blk * zw:(blk + 1) * zw] = qq
            sf_scr[:, blk * zw:(blk + 1) * zw] = sf
            fg_scr[:, blk * zw:(blk + 1) * zw] = fg
            bc_scr[:, blk * zw:(blk + 1) * zw] = _exact_dot(low01, lf)
        col = lambda h: slice(h * HG_DIM, (h + 1) * HG_DIM)
        zcol = lambda part, h: (part + h // hpb, slice(None), slice((h % hpb) * HG_DIM, (h % hpb + 1) * HG_DIM))
        b = [bc_scr[:, col(h)] for h in H]
        bm = [bc_scr[C // 2 - 1:C // 2, col(h)] for h in H]
        bl = [bc_scr[C - 1:C, col(h)] for h in H]
        q_ = [q_scr[:, col(h)] for h in H]
        k_ = [1.0 - fg_scr[:, col(h)] for h in H]
        v_ = [z_ref[zcol(4, h)] for h in H]
        eq = [jnp.exp(jnp.minimum(b[h] - bm[h], EXP_CLAMP)) for h in H]
        ek = [jnp.exp(jnp.minimum(bm[h] - b[h], EXP_CLAMP)) for h in H]
        eb = [jnp.exp(b[h]) for h in H]
        el = [jnp.exp(bl[h] - b[h]) for h in H]
        qt = [q_[h] * eq[h] for h in H]
        kt = [k_[h] * ek[h] for h in H]
        q0 = [q_[h] * eb[h] for h in H]
        kd = [k_[h] * el[h] for h in H]
        A = [jnp.where(low, _dot_nt(qt[h], kt[h]), 0.0) for h in H]
        o = [_dot_nt(q0[h], st0_ref[h]) + _dot(A[h], v_[h]) for h in H]
        do = []
        dgw_acc = jnp.zeros((1, HG_DIM), F32)
        for h in H:
            g_h = z_ref[zcol(6, h)]
            r = lax.rsqrt(jnp.mean(o[h] * o[h], axis=-1, keepdims=True) + RMS_EPS)
            on = o[h] * r
            sg = _sigmoid(g_h)
            dogh = dog_ref[:, col(h)].astype(F32)
            t1 = dogh * on
            dgw_acc = dgw_acc + jnp.sum(t1 * (g_h * sg), axis=0, keepdims=True)
            dz_ref[zcol(6, h)] = t1 * gwv * (sg * (1.0 + g_h * (1.0 - sg)))
            don = dogh * gwv * (g_h * sg)
            do.append(r * (don - on * jnp.mean(don * on, axis=-1, keepdims=True)))
        dgw_ref[...] += dgw_acc
        P = [jnp.where(low, _dot_nt(do[h], v_[h]), 0.0) for h in H]
        dqq = [eb[h] * _dot(do[h], st0_ref[h]) + eq[h] * _dot_hp(P[h], kt[h], ((1,), (0,))) for h in H]
        dkk = [el[h] * _dot(v_[h], d_scr[h]) + ek[h] * _dot_hp(P[h], qt[h], ((0,), (0,))) for h in H]
        for h in H:
            dz_ref[zcol(4, h)] = _dot_nt(kd[h], d_scr[h]) + _dot_tn(A[h], do[h])
            x_scr[:, col(h)] = q_[h] * dqq[h] - k_[h] * dkk[h]
        edge = [jnp.sum(d_scr[h] * st1_ref[h], axis=0, keepdims=True) for h in H]
        for h in H:
            d_scr[h] = d_scr[h] * jnp.exp(bl[h]) + _dot_tn(do[h], q0[h])
        for blk in range(2):
            x_scr[:, blk * zw:(blk + 1) * zw] = _exact_dot(up01, x_scr[:, blk * zw:(blk + 1) * zw])
        dlb = []
        for h in H:
            dfg = (x_scr[:, col(h)] + edge[h]) / fg_scr[:, col(h)] - dkk[h]
            sf_h = sf_scr[:, col(h)]
            lb_h = _lb_of(lbl_ref, col(h))
            zq_h = z_ref[zcol(0, h)]
            sq_h = _sigmoid(zq_h)
            dlb.append(jnp.sum(dfg * (1.0 - sf_h), axis=0, keepdims=True))
            dz_ref[zcol(0, h)] = dqq[h] * (sq_h * (1.0 + zq_h * (1.0 - sq_h)))
            dz_ref[zcol(2, h)] = dfg * (1.0 - lb_h) * sf_h * (1.0 - sf_h)
        dlb_ref[...] += jnp.concatenate(dlb, axis=1)

    rev = lambda s: nch - 1 - s
    return pl.pallas_call(body, name=name, grid=(nch,),
                          in_specs=[pl.BlockSpec((8, C, zw), lambda s: (0, rev(s), 0)),
                                    pl.BlockSpec((None, C, D_MODEL), lambda s: (0, rev(s), 0)),
                                    pl.BlockSpec((None, HG_HEADS, HG_DIM, HG_DIM), lambda s: (rev(s), 0, 0, 0)),
                                    pl.BlockSpec((None, HG_HEADS, HG_DIM, HG_DIM),
                                                 lambda s: (jnp.minimum(rev(s) + 1, nch - 1), 0, 0, 0)),
                                    pl.BlockSpec((2, D_MODEL), lambda s: (0, 0)),
                                    pl.BlockSpec((1, HG_DIM), lambda s: (0, 0))],
                          out_specs=[pl.BlockSpec((8, C, zw), lambda s: (0, rev(s), 0)),
                                     pl.BlockSpec((1, D_MODEL), lambda s: (0, 0)),
                                     pl.BlockSpec((1, HG_DIM), lambda s: (0, 0))],
                          out_shape=[jax.ShapeDtypeStruct((8, T, zw), F32), jax.ShapeDtypeStruct((1, D_MODEL), F32),
                                     jax.ShapeDtypeStruct((1, HG_DIM), F32)],
                          scratch_shapes=[pltpu.VMEM((HG_HEADS, HG_DIM, HG_DIM), F32)]
                          + [pltpu.VMEM((C, D_MODEL), F32)] * 5,
                          compiler_params=_params())(z, dog, states, states, lbl, gw)


def _bucket_onehot():
    W = SW_WINDOW
    t = np.arange(W)[:, None] + W
    s = np.arange(2 * W)[None, :]
    dist = t - s
    exact = REL_BUCKETS // 2
    d = np.maximum(np.maximum(dist, 0), 1).astype(np.float32)
    log_b = exact + (np.log(d / np.float32(exact)) / np.float32(math.log(REL_MAX_DIST / exact))
                     * np.float32(REL_BUCKETS - exact)).astype(np.int32)
    bucket = np.where(np.maximum(dist, 0) < exact, np.maximum(dist, 0), np.minimum(log_b, REL_BUCKETS - 1))
    valid = (dist >= 0) & (dist < W)
    onehot = (bucket[..., None] == np.arange(REL_BUCKETS)) & valid[..., None]
    return onehot.reshape(W * 2 * W, REL_BUCKETS).astype(np.float32)


def _bias_expand(rel_t, onehot_t, *, name):
    hq, nbk = rel_t.shape
    n = onehot_t.shape[1]

    def body(r_ref, oh_ref, o_ref):
        o_ref[...] = _exact_dot_r(r_ref[...], oh_ref[...])

    return pl.pallas_call(body, name=name, out_shape=jax.ShapeDtypeStruct((hq, n), F32),
                          compiler_params=_params())(rel_t, onehot_t)


def _bias_reduce(dbias, onehot, *, name):
    hq = dbias.shape[0]
    nbk = onehot.shape[1]

    def body(d_ref, oh_ref, o_ref):
        o_ref[...] = _exact_dot_r(d_ref[...], oh_ref[...])

    return pl.pallas_call(body, name=name, out_shape=jax.ShapeDtypeStruct((hq, nbk), F32),
                          compiler_params=_params())(dbias, onehot)


def _swa_mask(j):
    W = SW_WINDOW
    t = lax.broadcasted_iota(jnp.int32, (W, 2 * W), 0) + W
    s = lax.broadcasted_iota(jnp.int32, (W, 2 * W), 1)
    dist = t - s
    band = (dist >= 0) & (dist < W)
    m = band & ((j > 0) | (s >= W))
    return jnp.concatenate([m] * SW_GROUP, axis=0)


def _half_mask(rows, half):
    lane = lax.broadcasted_iota(jnp.int32, (rows, LANES), 1)
    return (lane >= SW_HEAD_DIM) if half else (lane < SW_HEAD_DIM)


def _swa_head(ref, col0, head, to_half):
    slab, half = head // 2, head % 2
    x = ref[:, col0 + slab * LANES:col0 + (slab + 1) * LANES]
    x = jnp.where(_half_mask(x.shape[0], half), x, 0.0)
    return x if half == to_half else pltpu.roll(x, SW_HEAD_DIM, axis=1)


def _swa_stack(ref, g):
    return jnp.concatenate([_swa_head(ref, 0, g * SW_GROUP + r, g % 2) for r in range(SW_GROUP)], axis=0)


def _swa_unstack(ref, x, g):
    W = SW_WINDOW
    for pair in range(SW_GROUP // 2):
        parts = []
        for r in (2 * pair, 2 * pair + 1):
            piece = x[r * W:(r + 1) * W]
            parts.append(piece if r % 2 == g % 2 else pltpu.roll(piece, SW_HEAD_DIM, axis=1))
        slab = (g * SW_GROUP) // 2 + pair
        ref[:, slab * LANES:(slab + 1) * LANES] = parts[0] + parts[1]


def _swa_kv(kp_ref, kc_ref, col0, g):
    return jnp.concatenate([_swa_head(kp_ref, col0, g, g % 2), _swa_head(kc_ref, col0, g, g % 2)], axis=0)


def _lane_pick(tile, h):
    lane = lax.broadcasted_iota(jnp.int32, tile.shape, 1)
    return jnp.sum(jnp.where(lane == h, tile, 0.0), axis=-1, keepdims=True)


def _lane_put(tile, h, col):
    lane = lax.broadcasted_iota(jnp.int32, tile.shape, 1)
    return jnp.where(lane == h, col, tile)


def _swa_rows(vals):
    return jnp.concatenate([jnp.broadcast_to(v, (SW_WINDOW, 1)) for v in vals], axis=0)


def _swa_fwd(q, kv, bias, sinks, *, name):
    _, T, D = q.shape
    W = SW_WINDOW
    nb = T // W
    dh = SW_HEAD_DIM
    kvw = SW_KV_HEADS * dh
    scale = dh ** -0.5

    def body(q_ref, kc_ref, kp_ref, bias_ref, sink_ref, o_ref, lse_ref):
        j = pl.program_id(0)
        mask = _swa_mask(j)
        sk = sink_ref[...]
        lse_tile = jnp.zeros((W, SW_Q_HEADS), F32)
        G = range(SW_KV_HEADS)
        kk = [_swa_kv(kp_ref, kc_ref, 0, g) for g in G]
        vv = [_swa_kv(kp_ref, kc_ref, kvw, g) for g in G]
        qs = [_swa_stack(q_ref, g) for g in G]
        logits = [jnp.where(mask, _dot_nt(qs[g], kk[g]) * scale
                            + bias_ref[g * SW_GROUP:(g + 1) * SW_GROUP].reshape(SW_GROUP * W, 2 * W), NEG_BIG) for g in G]
        sink = [_swa_rows([_lane_pick(sk, g * SW_GROUP + r) for r in range(SW_GROUP)]) for g in G]
        m = [jnp.maximum(jnp.max(logits[g], axis=-1, keepdims=True), sink[g]) for g in G]
        p = [jnp.exp(logits[g] - m[g]) for g in G]
        den = [jnp.sum(p[g], axis=-1, keepdims=True) + jnp.exp(sink[g] - m[g]) for g in G]
        pv = [_dot(p[g], vv[g]) for g in G]
        for g in G:
            _swa_unstack(o_ref, pv[g] / den[g], g)
            lse = m[g] + jnp.log(den[g])
            for r in range(SW_GROUP):
                lse_tile = _lane_put(lse_tile, g * SW_GROUP + r, lse[r * W:(r + 1) * W])
        lse_ref[...] = lse_tile

    return pl.pallas_call(body, name=name, grid=(nb,),
                          in_specs=[pl.BlockSpec((None, W, D), lambda j: (0, j, 0)),
                                    pl.BlockSpec((None, W, 2 * kvw), lambda j: (0, j, 0)),
                                    pl.BlockSpec((None, W, 2 * kvw), lambda j: (0, jnp.maximum(j - 1, 0), 0)),
                                    pl.BlockSpec((SW_Q_HEADS, W, 2 * W), lambda j: (0, 0, 0)),
                                    pl.BlockSpec((1, SW_Q_HEADS), lambda j: (0, 0))],
                          out_specs=[pl.BlockSpec((None, W, D), lambda j: (0, j, 0)),
                                     pl.BlockSpec((W, SW_Q_HEADS), lambda j: (j, 0))],
                          out_shape=[jax.ShapeDtypeStruct((1, T, D), F32), jax.ShapeDtypeStruct((T, SW_Q_HEADS), F32)],
                          compiler_params=_params())(q, kv, kv, bias, sinks)


def _swa_bwd(q, kv, o, lse, do, bias, sinks, *, name):
    _, T, D = q.shape
    W = SW_WINDOW
    nb = T // W
    dh = SW_HEAD_DIM
    kvw = SW_KV_HEADS * dh
    scale = dh ** -0.5
    cl = lambda j: jnp.minimum(j, nb - 1)

    def body(q_ref, kc_ref, kp_ref, o_ref, lse_ref, do_ref, bias_ref, sink_ref,
             dq_ref, dkv_ref, dbias_ref, dsink_ref, carry):
        j = pl.program_id(0)

        @pl.when(j == 0)
        def _():
            carry[...] = jnp.zeros_like(carry)
            dbias_ref[...] = jnp.zeros_like(dbias_ref)
            dsink_ref[...] = jnp.zeros_like(dsink_ref)

        @pl.when(j < nb)
        def _():
            mask = _swa_mask(j)
            sk = sink_ref[...]
            lse_tile = lse_ref[...]
            dsink = jnp.zeros((1, SW_Q_HEADS), F32)
            G = range(SW_KV_HEADS)
            heads = [[g * SW_GROUP + r for r in range(SW_GROUP)] for g in G]
            kk = [_swa_kv(kp_ref, kc_ref, 0, g) for g in G]
            vv = [_swa_kv(kp_ref, kc_ref, kvw, g) for g in G]
            qs = [_swa_stack(q_ref, g) for g in G]
            dos = [_swa_stack(do_ref, g) for g in G]
            lse = [jnp.concatenate([_lane_pick(lse_tile, h) for h in heads[g]], axis=0) for g in G]
            sink = [_swa_rows([_lane_pick(sk, h) for h in heads[g]]) for g in G]
            logits = [jnp.where(mask, _dot_nt(qs[g], kk[g]) * scale
                                + bias_ref[g * SW_GROUP:(g + 1) * SW_GROUP].reshape(SW_GROUP * W, 2 * W), NEG_BIG)
                      for g in G]
            dp = [_dot_nt(dos[g], vv[g]) for g in G]
            p = [jnp.exp(logits[g] - lse[g]) for g in G]
            delta = [jnp.sum(dos[g] * _swa_stack(o_ref, g), axis=-1, keepdims=True) for g in G]
            dl = [p[g] * (dp[g] - delta[g]) for g in G]
            dqs = [_dot(dl[g], kk[g]) * scale for g in G]
            dks = [_dot_tn(dl[g], qs[g]) * scale for g in G]
            dvs = [_dot_tn(p[g], dos[g]) for g in G]
            for g in G:
                _swa_unstack(dq_ref, dqs[g], g)
                dbias_ref[g * SW_GROUP:(g + 1) * SW_GROUP] += dl[g].reshape(SW_GROUP, W, 2 * W)
                sd = jnp.exp(sink[g] - lse[g]) * delta[g]
                for r, h in enumerate(heads[g]):
                    dsink = _lane_put(dsink, h, -jnp.sum(sd[r * W:(r + 1) * W], axis=0, keepdims=True))
            dsink_ref[...] += dsink
            for slab in range(SW_KV_HEADS // 2):
                for col0, parts in ((0, dks), (kvw, dvs)):
                    both = parts[2 * slab] + parts[2 * slab + 1]
                    cols = slice(col0 + slab * LANES, col0 + (slab + 1) * LANES)
                    dkv_ref[:, cols] = carry[:, cols] + both[:W]
                    carry[:, cols] = both[W:]

        @pl.when(j == nb)
        def _():
            dkv_ref[...] = carry[...]

    tok = lambda w: pl.BlockSpec((None, W, w), lambda j: (0, cl(j), 0))
    return pl.pallas_call(body, name=name, grid=(nb + 1,),
                          in_specs=[tok(D), tok(2 * kvw),
                                    pl.BlockSpec((None, W, 2 * kvw), lambda j: (0, jnp.maximum(cl(j) - 1, 0), 0)),
                                    tok(D), pl.BlockSpec((W, SW_Q_HEADS), lambda j: (cl(j), 0)), tok(D),
                                    pl.BlockSpec((SW_Q_HEADS, W, 2 * W), lambda j: (0, 0, 0)),
                                    pl.BlockSpec((1, SW_Q_HEADS), lambda j: (0, 0))],
                          out_specs=[tok(D),
                                     pl.BlockSpec((None, W, 2 * kvw), lambda j: (0, jnp.maximum(j - 1, 0), 0)),
                                     pl.BlockSpec((SW_Q_HEADS, W, 2 * W), lambda j: (0, 0, 0)),
                                     pl.BlockSpec((1, SW_Q_HEADS), lambda j: (0, 0))],
                          out_shape=[jax.ShapeDtypeStruct((1, T, D), F32), jax.ShapeDtypeStruct((1, T, 2 * kvw), F32),
                                     jax.ShapeDtypeStruct((SW_Q_HEADS, W, 2 * W), F32),
                                     jax.ShapeDtypeStruct((1, SW_Q_HEADS), F32)],
                          scratch_shapes=[pltpu.VMEM((W, 2 * kvw), F32)],
                          compiler_params=_params())(q, kv, kv, o, lse, do, bias, sinks)


_HBM = pl.BlockSpec(memory_space=pltpu.HBM)
_SEM = pl.BlockSpec(memory_space=pltpu.SEMAPHORE)
_EFFECT = pltpu.SideEffectType.DATAFLOW_SIDE_EFFECTING
N_PEERS = N_DEV - 1


def _peer(k):
    x, y, c = lax.axis_index("x"), lax.axis_index("y"), lax.axis_index("c")
    px = (x + (k >> 2)) % 2
    py = (y + ((k >> 1) & 1)) % 2
    pc = (c + (k & 1)) % 2
    return (px, py, pc), 4 * px + 2 * py + pc


def _my_number():
    return 4 * lax.axis_index("x") + 2 * lax.axis_index("y") + lax.axis_index("c")


def _landing(src, mode):
    me = _my_number()
    own = src if mode == "gather" else lax.dynamic_index_in_dim(src, me, 0, keepdims=False)
    return lax.dynamic_update_index_in_dim(lax.empty((N_DEV,) + own.shape, own.dtype), own, me, 0)


def _copy(src_ref, land_ref, mode, send, recv, j, k, dst_slot):
    peer, pid = _peer(k)
    return pltpu.make_async_remote_copy(
        src_ref=src_ref if mode == "gather" else src_ref.at[pid], dst_ref=land_ref.at[dst_slot(pid)],
        send_sem=send.at[j * N_PEERS + k - 1], recv_sem=recv.at[j * N_PEERS + k - 1],
        device_id=peer, device_id_type=pl.DeviceIdType.MESH)


def _send_start(groups, *, name):
    flat = [t for g in groups for t in g]
    n, ng = len(flat), len(groups)
    srcs = [pltpu.with_memory_space_constraint(s, pltpu.HBM) for s, _ in flat]
    lands = [pltpu.with_memory_space_constraint(_landing(s, m), pltpu.HBM) for s, m in flat]

    def body(*refs):
        src_refs, land_refs = refs[:n], refs[n:2 * n]
        sems = refs[2 * n:2 * n + 2 * ng]
        token = refs[-1]
        me = _my_number()
        a = 0
        for gi, g in enumerate(groups):
            for j, (_, mode) in enumerate(g):
                for k in range(1, N_DEV):
                    _copy(src_refs[a], land_refs[a], mode, sems[2 * gi], sems[2 * gi + 1], j, k, lambda pid: me).start()
                a += 1
        token[...] = jnp.zeros_like(token)

    sem_shapes = []
    for g in groups:
        sem_shapes += [pltpu.SemaphoreType.DMA((len(g) * N_PEERS,))] * 2
    out = pl.pallas_call(
        body, name=name,
        out_shape=tuple(sem_shapes) + tuple(pltpu.HBM(a.shape, a.dtype) for a in srcs + lands)
        + (jax.ShapeDtypeStruct((SUBLANES, LANES), F32),),
        in_specs=[_HBM] * (2 * n), out_specs=[_SEM] * (2 * ng) + [_HBM] * (2 * n) + [pl.BlockSpec(memory_space=pltpu.VMEM)],
        input_output_aliases={i: 2 * ng + i for i in range(2 * n)},
        compiler_params=pltpu.CompilerParams(has_side_effects=_EFFECT))(*srcs, *lands)
    sems, thru, token = out[:2 * ng], out[2 * ng:2 * ng + 2 * n], out[-1]
    handles, a = [], 0
    for gi, g in enumerate(groups):
        m = len(g)
        handles.append((sems[2 * gi], sems[2 * gi + 1], list(thru[a:a + m]), list(thru[n + a:n + a + m]),
                        [mode for _, mode in g]))
        a += m
    return handles, token


def _send_wait(handle, after, *, name):
    send, recv, srcs, lands, modes = handle
    m = len(srcs)

    def body(*refs):
        src_refs, land_refs = refs[:m], refs[m:2 * m]
        send_ref, recv_ref = refs[2 * m], refs[2 * m + 1]
        for j in range(m):
            for k in range(1, N_DEV):
                cp = _copy(src_refs[j], land_refs[j], modes[j], send_ref, recv_ref, j, k, lambda pid: pid)
                cp.wait_send()
                cp.wait_recv()

    out = pl.pallas_call(
        body, name=name, out_shape=tuple(pltpu.HBM(a.shape, a.dtype) for a in srcs + lands),
        in_specs=[_HBM] * (2 * m) + [_SEM, _SEM] + [pl.BlockSpec(memory_space=pl.ANY)] * len(after),
        out_specs=[_HBM] * (2 * m), input_output_aliases={i: i for i in range(2 * m)},
        compiler_params=pltpu.CompilerParams(has_side_effects=_EFFECT))(*srcs, *lands, send, recv, *after)
    return list(out[m:])


def _adam_math(w, g, m, v):
    m = ADAM_B1 * m + (1.0 - ADAM_B1) * g
    v = ADAM_B2 * v + (1.0 - ADAM_B2) * (g * g)
    m_hat = m / (1.0 - ADAM_B1 ** ADAM_STEP)
    v_hat = v / (1.0 - ADAM_B2 ** ADAM_STEP)
    delta = -ADAM_LR * (m_hat / (jnp.sqrt(v_hat) + ADAM_EPS) + ADAM_WD * w)
    return delta, m, v


def _adamw(parts, w, m, v, *, name, layer=None):
    S, R, C = parts.shape
    tr = R
    for cand in (256, 128, 64, 32, 16, 8):
        if R % cand == 0 and S * cand * C * 4 <= 4 * 2 ** 20:
            tr = cand
            break

    def body(p_ref, w_ref, m_ref, v_ref, g_ref, d_ref, nm_ref, nv_ref):
        g = p_ref[0].astype(F32)
        for s in range(1, S):
            g = g + p_ref[s].astype(F32)
        delta, nm, nv = _adam_math(w_ref[...], g, m_ref[...], v_ref[...])
        g_ref[...] = g
        d_ref[...] = delta
        nm_ref[...] = nm
        nv_ref[...] = nv

    if layer is None:
        wspec = pl.BlockSpec((tr, C), lambda i: (i, 0))
    else:
        wspec = pl.BlockSpec((None, tr, C), lambda i: (layer, i, 0))
    ospec = pl.BlockSpec((tr, C), lambda i: (i, 0))
    osh = jax.ShapeDtypeStruct((R, C), F32)
    return pl.pallas_call(body, name=name, grid=(R // tr,),
                          in_specs=[pl.BlockSpec((S, tr, C), lambda i: (0, i, 0)), wspec, wspec, wspec],
                          out_specs=[ospec] * 4, out_shape=[osh] * 4, compiler_params=_params())(parts, w, m, v)


def _sum_parts(parts, *, name):
    S, R, C = parts.shape

    def body(p_ref, o_ref):
        g = p_ref[0]
        for s in range(1, S):
            g = g + p_ref[s]
        o_ref[...] = g

    return pl.pallas_call(body, name=name, out_shape=jax.ShapeDtypeStruct((R, C), F32),
                          compiler_params=_params())(parts)


def _pack_rows(arrays):
    pieces, layout, row = [], [], 0
    for a in arrays:
        flat = a.reshape(-1).astype(F32)
        rows = -(-flat.shape[0] // (SUBLANES * LANES)) * SUBLANES
        flat = jnp.pad(flat, (0, rows * LANES - flat.shape[0]))
        pieces.append(flat.reshape(rows, LANES))
        layout.append((row, rows, a.shape))
        row += rows
    return jnp.concatenate(pieces, axis=0), layout


def _unpack_rows(packed, layout):
    out = []
    for row, rows, shape in layout:
        size = int(np.prod(shape))
        out.append(packed[row:row + rows].reshape(-1)[:size].reshape(shape))
    return out


def _ffn_fwd(h, w_in, w_out, cw, ln_g, ln_b, tag):
    h, hb = h
    u, ab, act = _ffn_up(hb, w_in, cw, name=f"ffn_up_{tag}")
    u = (u, ab)
    hn, hnb, xh, rs = _mm_nn(act, w_out, res=h, res_scale=ALPHA, ln=(ln_g, ln_b), name=f"ffn_down_{tag}")
    return (hn, hnb), xh, rs, u


def _ffn_bwd(dy, hb, u, w_in, w_out, cw, ln_bwd, send, tag):
    du, dw_out, dcw = _ffn_gate_bwd(dy, u[0], u[1], w_out, cw, name=f"ffn_gate_bwd_{tag}")
    du = du.reshape((-1,) + du.shape[2:])
    dw_in = _mm_tn(hb, du, n_map=_pair_map, name=f"ffn_dwin_{tag}")
    handle, token = send(dw_in, dw_out)
    dyp, dg, db = _mm_nt_resident(du, w_in, n_map=_pair_map, res=dy, res_scale=ALPHA, ln_bwd=ln_bwd,
                                  behind=(token,), name=f"ffn_dh_{tag}")
    dcw = dcw.transpose(1, 0, 2, 3).reshape((-1,) + dcw.shape[2:])
    return dyp, dg, db, handle, dcw


def kernel(x, hgrn_w_in, hgrn_lb_logits, hgrn_gnorm_w, hgrn_w_out, swa_w_q, swa_sinks, swa_w_out, shared_w_kv, rel_bias, ffn_w_in, ffn_conv_w, ffn_conv_b, ffn_w_out, ln_mix_g, ln_mix_b, ln_ffn_g, ln_ffn_b, loss_target, m_hgrn_w_in, m_hgrn_lb_logits, m_hgrn_gnorm_w, m_hgrn_w_out, m_swa_w_q, m_swa_sinks, m_swa_w_out, m_shared_w_kv, m_rel_bias, m_ffn_w_in, m_ffn_conv_w, m_ffn_conv_b, m_ffn_w_out, m_ln_mix_g, m_ln_mix_b, m_ln_ffn_g, m_ln_ffn_b, v_hgrn_w_in, v_hgrn_lb_logits, v_hgrn_gnorm_w, v_hgrn_w_out, v_swa_w_q, v_swa_sinks, v_swa_w_out, v_shared_w_kv, v_rel_bias, v_ffn_w_in, v_ffn_conv_w, v_ffn_conv_b, v_ffn_w_out, v_ln_mix_g, v_ln_mix_b, v_ln_ffn_g, v_ln_ffn_b):
    T = x.shape[1]
    D = D_MODEL
    W = SW_WINDOW
    fb = ffn_w_in.shape[2]
    me = 4 * lax.axis_index("x") + 2 * lax.axis_index("y") + lax.axis_index("c")

    small_fwd, small_fwd_layout = _pack_rows([hgrn_lb_logits, ffn_conv_w])
    gat = lambda *ws: [(w_.astype(MXU), "gather") for w_ in ws]
    (wait_a, wait_b, wait_c), _ = _send_start(
        [gat(hgrn_w_in[0]) + [(small_fwd, "gather")],
         gat(hgrn_w_out[0], ffn_w_in[0], ffn_w_out[0]),
         gat(shared_w_kv, swa_w_q[0], swa_w_out[0], ffn_w_in[1], ffn_w_out[1])], name="gather_start")
    w_hin, small_all = _send_wait(wait_a, (), name="gather_wait_a")
    w_hin = w_hin[None]
    ffn_rows = 2 * ffn_w_out.shape[1]
    (lb_row, lb_rows, _), (cw_row, cw_rows, _) = small_fwd_layout
    lbl = small_all[:, lb_row:lb_row + 2, :].transpose(1, 0, 2).reshape(2, D)
    conv_w_all = small_all[:, cw_row:cw_row + cw_rows, :].reshape(N_DEV, -1)[:, :DEPTH * 3 * fb]
    conv_w_all = conv_w_all.reshape(N_DEV, DEPTH, 3, fb).transpose(1, 0, 2, 3)
    conv_b_all = ffn_conv_b.reshape(DEPTH, N_DEV, 1, fb)
    no_pad = ((0, 0), (0, 0))
    cw = (jnp.pad(conv_w_all, no_pad + ((0, SUBLANES - 3), (0, 0)))
          + jnp.pad(conv_b_all, no_pad + ((3, SUBLANES - 4), (0, 0))))

    row = lambda a, l: a[l:l + 1]

    z = _mm_nn(x, w_hin, name="hgrn_in")
    og, states = _hgrn_fwd(z, lbl, hgrn_gnorm_w, name="hgrn_rec")
    w_hout, w_fin0, w_fout0 = _send_wait(wait_b, (og,), name="gather_wait_b")
    w_hout = w_hout.reshape(1, 1, D, D)
    w_fin = [w_fin0[None], None]
    w_fout = [w_fout0.reshape(4, 1, ffn_rows, D), None]
    h1, h1b, xh1, rs1 = _mm_nn(og, w_hout, res=x, res_scale=ALPHA, ln=(row(ln_mix_g, 0), row(ln_mix_b, 0)),
                               name="hgrn_out")
    (h2, h2b), xh2, rs2, u0 = _ffn_fwd((h1, h1b), w_fin[0], w_fout[0], cw[0], row(ln_ffn_g, 0), row(ln_ffn_b, 0), "l0")
    w_kv, w_q, w_o, w_fin1, w_fout1 = _send_wait(wait_c, (h2,), name="gather_wait_c")
    w_kv = w_kv.reshape(1, 1, D, 2 * SW_KV_HEADS * SW_HEAD_DIM)
    w_q = w_q.reshape(1, 1, D, D)
    w_o = w_o.reshape(1, 1, D, D)
    w_fin[1] = w_fin1[None]
    w_fout[1] = w_fout1.reshape(4, 1, ffn_rows, D)
    kv = _mm_nn(h2b, w_kv, name="swa_kv")
    q = _mm_nn(h2b, w_q, name="swa_q")
    onehot = _bucket_onehot()
    bias = _bias_expand(rel_bias.T, jnp.asarray(onehot.T, jnp.bfloat16), name="swa_bias").reshape(SW_Q_HEADS, W, 2 * W)
    ao, lse = _swa_fwd(q, kv, bias, swa_sinks, name="swa_attn")
    h3, h3b, xh3, rs3 = _mm_nn(ao, w_o, res=h2, res_scale=ALPHA, ln=(row(ln_mix_g, 1), row(ln_mix_b, 1)),
                               name="swa_out")
    (h4, _), xh4, rs4, u1 = _ffn_fwd((h3, h3b), w_fin[1], w_fout[1], cw[1], row(ln_ffn_g, 1), row(ln_ffn_b, 1), "l1")
    dy4, dg_f1, db_f1, loss_tile = _loss_ln_bwd(h4, loss_target, xh4, rs4, row(ln_ffn_g, 1), name="loss_ln_ffn1_bwd")
    sc = lambda *gs: [(g_, "scatter") for g_ in gs]

    def send_ffn(name_):
        def send(dw_in, dw_out):
            (handle,), token = _send_start([sc(dw_in.reshape(N_DEV, D, fb), dw_out.reshape(N_DEV, -1, D))], name=name_)
            return handle, token
        return send

    dy3, dg_m1, db_m1, ex1, dcw1 = _ffn_bwd(dy4, h3b, u1, w_fin[1], w_fout[1], cw[1],
                                            (xh3, rs3, row(ln_mix_g, 1)), send_ffn("grads_start_1"), "l1")
    dw_o = _mm_tn(ao, dy3, name="swa_dwo")
    dao = _mm_nt(dy3, w_o, name="swa_dao")
    dq, dkv, dbias, dsinks = _swa_bwd(q, kv, ao, lse, dao, bias, swa_sinks, name="swa_attn_bwd")
    drel_t = _bias_reduce(dbias.reshape(SW_Q_HEADS, W * 2 * W), jnp.asarray(onehot, jnp.bfloat16), name="swa_dbias")
    dw_q = _mm_tn(h2b, dq, name="swa_dwq")
    dw_kv = _mm_tn(h2b, dkv, name="swa_dwkv")
    dh2 = _mm_nt(dq, w_q, res=dy3, res_scale=ALPHA, name="swa_dh_q")
    (ex2,), tok2 = _send_start([sc(dw_o.reshape(N_DEV, D // N_DEV, D), dw_q.reshape(N_DEV, D // N_DEV, D),
                                   dw_kv.reshape(N_DEV, D // N_DEV, -1))], name="grads_start_2")
    dy2, dg_f0, db_f0 = _mm_nt_resident(dkv, w_kv, res=dh2, ln_bwd=(xh2, rs2, row(ln_ffn_g, 0)), behind=(tok2,),
                                        name="swa_dh_kv")
    dy1, dg_m0, db_m0, ex3, dcw0 = _ffn_bwd(dy2, h1b, u0, w_fin[0], w_fout[0], cw[0],
                                            (xh1, rs1, row(ln_mix_g, 0)), send_ffn("grads_start_3"), "l0")
    dw_hout = _mm_tn(og, dy1, name="hgrn_dwout")
    dog = _mm_nt(dy1, w_hout, name="hgrn_dog")
    dz, dlb, dgw = _hgrn_bwd(z, dog, states, lbl, hgrn_gnorm_w, name="hgrn_rec_bwd")
    dw_hin = _mm_tn(x, dz, name="hgrn_dwin")

    p0 = _sigmoid(lbl[0:1] - lbl[1:2])
    dl0 = dlb * p0 * (1.0 - p0)
    d_lbl = dl0 * jnp.array([[1.0], [-1.0]], F32)
    dcw = jnp.stack([dcw0, dcw1], axis=0)
    d_conv_w = dcw[:, :, 0:3, :]
    d_conv_b = dcw[:, :, 3, :].reshape(DEPTH, N_DEV * fb)
    first_row = lax.broadcasted_iota(jnp.int32, (DEPTH, D), 0) == 0
    two_rows = lambda a, b: jnp.where(first_row, a, b)
    d_ln_mix_g = two_rows(dg_m0, dg_m1)
    d_ln_mix_b = two_rows(db_m0, db_m1)
    d_ln_ffn_g = two_rows(dg_f0, dg_f1)
    d_ln_ffn_b = two_rows(db_f0, db_f1)
    small_grads, small_layout = _pack_rows([d_lbl, d_conv_w, dgw, dsinks, drel_t.T, d_conv_b, d_ln_mix_g, d_ln_mix_b,
                                            d_ln_ffn_g, d_ln_ffn_b, loss_tile[0:1, 0:1]])

    (ex4,), tok4 = _send_start([sc(dw_hin.reshape(N_DEV, D, -1), dw_hout.reshape(N_DEV, D // N_DEV, D))
                                + [(small_grads, "gather")]], name="grads_start_4")
    dx = _mm_nt_resident(dz, w_hin, res=dy1, res_scale=ALPHA, name="hgrn_dx", behind=(tok4,))
    r_fin1, r_fout1 = _send_wait(ex1, (dx,), name="grads_wait_1")
    r_o, r_q, r_kv = _send_wait(ex2, (dx,), name="grads_wait_2")
    r_fin0, r_fout0 = _send_wait(ex3, (dx,), name="grads_wait_3")
    r_hin, r_hout, r_small = _send_wait(ex4, (dx,), name="grads_wait_4")
    received = [r_hin, r_hout, r_q, r_o, r_kv, r_fin0, r_fin1, r_fout0, r_fout1, r_small]

    outs = {}

    def put(name_, res):
        outs["grad_" + name_], outs["delta_" + name_], outs["new_m_" + name_], outs["new_v_" + name_] = res

    def big_update(name_, parts, w, m, v):
        shp = w.shape
        if w.ndim == 3 and shp[0] == 1:
            r = _adamw(parts, w[0], m[0], v[0], name="adamw_" + name_)
            put(name_, [a.reshape(shp) for a in r])
        else:
            r = _adamw(parts, w, m, v, name="adamw_" + name_)
            put(name_, r)

    big_update("hgrn_w_in", received[0], hgrn_w_in, m_hgrn_w_in, v_hgrn_w_in)
    big_update("hgrn_w_out", received[1], hgrn_w_out, m_hgrn_w_out, v_hgrn_w_out)
    big_update("swa_w_q", received[2], swa_w_q, m_swa_w_q, v_swa_w_q)
    big_update("swa_w_out", received[3], swa_w_out, m_swa_w_out, v_swa_w_out)
    big_update("shared_w_kv", received[4], shared_w_kv, m_shared_w_kv, v_shared_w_kv)
    for name_, idx, w, m, v in (("ffn_w_in", 5, ffn_w_in, m_ffn_w_in, v_ffn_w_in),
                                ("ffn_w_out", 7, ffn_w_out, m_ffn_w_out, v_ffn_w_out)):
        per_layer = [_adamw(received[idx + l], w, m, v, layer=l, name=f"adamw_{name_}_{l}") for l in range(DEPTH)]
        put(name_, [jnp.stack([per_layer[0][i], per_layer[1][i]], axis=0) for i in range(4)])

    small_sum = _sum_parts(received[9], name="sum_small_grads")
    (g_lbl, g_conv_w, g_gw, g_sinks, g_rel, g_conv_b, g_mix_g, g_mix_b, g_ffn_g, g_ffn_b,
     loss) = _unpack_rows(small_sum, small_layout)
    g_lbl_mine = lax.dynamic_slice_in_dim(g_lbl, me * (D // N_DEV), D // N_DEV, axis=1)
    g_conv_w_mine = lax.dynamic_index_in_dim(g_conv_w, me, axis=1, keepdims=False)
    small_names = ["hgrn_lb_logits", "ffn_conv_w", "hgrn_gnorm_w", "swa_sinks", "rel_bias", "ffn_conv_b",
                   "ln_mix_g", "ln_mix_b", "ln_ffn_g", "ln_ffn_b"]
    small_g = [g_lbl_mine, g_conv_w_mine, g_gw, g_sinks, g_rel, g_conv_b, g_mix_g, g_mix_b, g_ffn_g, g_ffn_b]
    small_w = [hgrn_lb_logits, ffn_conv_w, hgrn_gnorm_w, swa_sinks, rel_bias, ffn_conv_b, ln_mix_g, ln_mix_b,
               ln_ffn_g, ln_ffn_b]
    small_m = [m_hgrn_lb_logits, m_ffn_conv_w, m_hgrn_gnorm_w, m_swa_sinks, m_rel_bias, m_ffn_conv_b, m_ln_mix_g,
               m_ln_mix_b, m_ln_ffn_g, m_ln_ffn_b]
    small_v = [v_hgrn_lb_logits, v_ffn_conv_w, v_hgrn_gnorm_w, v_swa_sinks, v_rel_bias, v_ffn_conv_b, v_ln_mix_g,
               v_ln_mix_b, v_ln_ffn_g, v_ln_ffn_b]
    pg, lay = _pack_rows(small_g)
    pw, _ = _pack_rows(small_w)
    pm, _ = _pack_rows(small_m)
    pv, _ = _pack_rows(small_v)
    res = _adamw(pg[None], pw, pm, pv, name="adamw_small")
    unpacked = [_unpack_rows(r, lay) for r in res]
    for i, name_ in enumerate(small_names):
        put(name_, [unpacked[j][i] for j in range(4)])

    order = ["hgrn_w_in", "hgrn_lb_logits", "hgrn_gnorm_w", "hgrn_w_out", "swa_w_q", "swa_sinks", "swa_w_out",
             "shared_w_kv", "rel_bias", "ffn_w_in", "ffn_conv_w", "ffn_conv_b", "ffn_w_out", "ln_mix_g", "ln_mix_b",
             "ln_ffn_g", "ln_ffn_b"]
    result = [loss.reshape(()), dx]
    for kind in ("grad_", "delta_", "new_m_", "new_v_"):
        result += [outs[kind + n] for n in order]
    return tuple(result)
```

```python
import functools
import math

import numpy as np
import jax
import jax.numpy as jnp
from jax import lax
from jax.experimental import pallas as pl
from jax.experimental.pallas import tpu as pltpu

F32 = jnp.float32
MXU = jnp.bfloat16

N_DEV = 8
D_MODEL = 1024
DEPTH = 2
HG_HEADS = 8
HG_DIM = 128
HG_CHUNK = 64
SW_Q_HEADS = 16
SW_KV_HEADS = 4
SW_GROUP = 4
SW_HEAD_DIM = 64
SW_WINDOW = 128
REL_BUCKETS = 32
REL_MAX_DIST = 128
FFN_DIM = 2816
ALPHA = (2.0 * DEPTH) ** 0.25
LN_EPS = 1e-5
RMS_EPS = 1e-6
ADAM_LR = 0.001
ADAM_B1 = 0.9
ADAM_B2 = 0.999
ADAM_EPS = 1e-08
ADAM_WD = 0.01
ADAM_STEP = 10
EXP_CLAMP = 80.0
NEG_BIG = -1e30

SUBLANES = 8
LANES = 128
VMEM_LIMIT = 48 * 2 ** 20
TOKEN_TILE = 512
WIDE_TOKEN_TILE = 1024
RESIDENT_TOKEN_TILE = 256
REDUCE_TOKEN_TILE = 2048
GRAD_DTYPE = jnp.bfloat16


def _params(**kw):
    return pltpu.CompilerParams(vmem_limit_bytes=VMEM_LIMIT, **kw)


def _sigmoid(x):
    return 1.0 / (1.0 + jnp.exp(-x))


def _dot(a, b):
    return jnp.dot(a.astype(MXU), b.astype(MXU), preferred_element_type=F32)


def _dot_nt(a, b):
    return lax.dot_general(a.astype(MXU), b.astype(MXU), (((1,), (1,)), ((), ())), preferred_element_type=F32)


def _dot_tn(a, b):
    return lax.dot_general(a.astype(MXU), b.astype(MXU), (((0,), (0,)), ((), ())), preferred_element_type=F32)


def _trunc_bf16(x):
    bits = lax.bitcast_convert_type(x, jnp.int32)
    return lax.bitcast_convert_type(bits & jnp.int32(-65536), F32)


def _split3(x):
    hi = _trunc_bf16(x)
    r = x - hi
    mid = _trunc_bf16(r)
    lo = r - mid
    return hi.astype(jnp.bfloat16), mid.astype(jnp.bfloat16), lo.astype(jnp.bfloat16)


def _dot_hp(a, b, contract):
    def halves(x):
        hi = _trunc_bf16(x)
        return hi.astype(jnp.bfloat16), (x - hi).astype(jnp.bfloat16)

    ah, al = halves(a)
    bh, bl = halves(b)
    d = lambda p, q: lax.dot_general(p, q, (contract, ((), ())), preferred_element_type=F32)
    return d(ah, bh) + d(ah, bl) + d(al, bh)


def _exact_dot(m01, x):
    hi, mid, lo = _split3(x)
    d = lambda p: jnp.dot(m01, p, preferred_element_type=F32)
    return d(hi) + d(mid) + d(lo)


def _exact_dot_r(x, m01):
    hi, mid, lo = _split3(x)
    d = lambda p: jnp.dot(p, m01, preferred_element_type=F32)
    return d(hi) + d(mid) + d(lo)


def _mm_nn(a, w, *, name, res=None, res_scale=1.0, ln=None, out_dtype=F32, tm=None):
    nbk, T, kw = a.shape
    _, nbn, _, nw = w.shape
    tm = min(tm or TOKEN_TILE, T)
    has_res = res is not None
    assert ln is None or nbn == 1

    def body(*refs):
        refs = list(refs)
        a_ref, w_ref = refs[:2]
        pos = 2
        res_ref = None
        if has_res:
            res_ref = refs[pos]
            pos += 1
        if ln is not None:
            g_ref, b_ref = refs[pos:pos + 2]
            pos += 2
        o_ref = refs[pos]
        pos += 1
        if ln is not None:
            ob_ref, xh_ref, rs_ref = refs[pos:pos + 3]
        for n in range(nbn):
            y = _dot(a_ref[0], w_ref[0, n])
            for k in range(1, nbk):
                y = y + _dot(a_ref[k], w_ref[k, n])
            if has_res:
                y = y + res_scale * res_ref[n].astype(F32)
            if ln is None:
                o_ref[n] = y.astype(o_ref.dtype)
            else:
                mu = jnp.mean(y, axis=-1, keepdims=True)
                yc = y - mu
                var = jnp.mean(yc * yc, axis=-1, keepdims=True)
                rstd = lax.rsqrt(var + LN_EPS)
                xh = yc * rstd
                xh_ref[n] = xh
                rs_ref[...] = rstd
                h = xh * g_ref[...] + b_ref[...]
                o_ref[n] = h
                ob_ref[n] = h.astype(ob_ref.dtype)

    in_specs = [pl.BlockSpec((nbk, tm, kw), lambda i: (0, i, 0)),
                pl.BlockSpec((nbk, nbn, kw, nw), lambda i: (0, 0, 0, 0))]
    args = [a, w]
    if has_res:
        in_specs.append(pl.BlockSpec((nbn, tm, nw), lambda i: (0, i, 0)))
        args.append(res)
    if ln is not None:
        in_specs += [pl.BlockSpec((1, nw), lambda i: (0, 0))] * 2
        args += list(ln)
    out_spec = pl.BlockSpec((nbn, tm, nw), lambda i: (0, i, 0))
    out_shape = jax.ShapeDtypeStruct((nbn, T, nw), out_dtype)
    if ln is not None:
        out_specs = [out_spec, out_spec, out_spec, pl.BlockSpec((tm, 1), lambda i: (i, 0))]
        out_shape = [out_shape, jax.ShapeDtypeStruct((nbn, T, nw), MXU), jax.ShapeDtypeStruct((nbn, T, nw), F32),
                     jax.ShapeDtypeStruct((T, 1), F32)]
    else:
        out_specs = out_spec
    return pl.pallas_call(body, name=name, grid=(T // tm,), in_specs=in_specs, out_specs=out_specs,
                          out_shape=out_shape, compiler_params=_params())(*args)


def _same(n):
    return n


def _mm_nt(dy, w, *, name, res=None, res_scale=1.0, out_dtype=F32, tm=None, n_map=_same, behind=()):
    nbn, T, nw = dy.shape
    nbk, _, kw, _ = w.shape
    tm = min(tm or WIDE_TOKEN_TILE, T)
    has_res = res is not None

    def body(*refs):
        refs = list(refs)
        dy_ref, w_ref = refs[:2]
        pos = 2
        res_ref = None
        if has_res:
            res_ref = refs[pos]
            pos += 1
        pos += len(behind)
        o_ref = refs[pos]
        pos += 1
        acc_ref = refs[pos] if nbn > 1 else None
        n = pl.program_id(2)
        part = _dot_nt(dy_ref[...], w_ref[...])

        def finish(acc):
            y = acc
            if has_res:
                y = y + res_scale * res_ref[...].astype(F32)
            o_ref[...] = y.astype(o_ref.dtype)

        if nbn == 1:
            finish(part)
        else:
            @pl.when(n == 0)
            def _():
                acc_ref[...] = part

            @pl.when(n > 0)
            def _():
                acc_ref[...] += part

            @pl.when(n == nbn - 1)
            def _():
                finish(acc_ref[...])

    in_specs = [pl.BlockSpec((None, tm, nw), lambda i, k, n: (n, i, 0)),
                pl.BlockSpec((None, None, kw, nw), lambda i, k, n: (k, n_map(n), 0, 0))]
    args = [dy, w]
    if has_res:
        in_specs.append(pl.BlockSpec((None, tm, kw), lambda i, k, n: (k, i, 0)))
        args.append(res)
    in_specs += [pl.BlockSpec(memory_space=pl.ANY)] * len(behind)
    args += list(behind)
    scratch = [pltpu.VMEM((tm, kw), F32)] if nbn > 1 else []
    return pl.pallas_call(body, name=name, grid=(T // tm, nbk, nbn), in_specs=in_specs,
                          out_specs=pl.BlockSpec((None, tm, kw), lambda i, k, n: (k, i, 0)),
                          out_shape=jax.ShapeDtypeStruct((nbk, T, kw), out_dtype), scratch_shapes=scratch,
                          compiler_params=_params())(*args)


def _mm_nt_resident(dy, w, *, name, res=None, res_scale=1.0, ln_bwd=None, tm=None, n_map=_same, behind=()):
    nbn, T, nw = dy.shape
    nbk, _, kw, _ = w.shape
    assert nbk == 1
    tm = min(tm or RESIDENT_TOKEN_TILE, T)
    has_res = res is not None
    n_in = 2 + has_res + (3 if ln_bwd else 0) + len(behind)

    def body(*refs):
        dy_ref, w_ref = refs[:2]
        res_ref = refs[2] if has_res else None
        y = _dot_nt(dy_ref[0], w_ref[0, n_map(0)])
        for n in range(1, nbn):
            y = y + _dot_nt(dy_ref[n], w_ref[0, n_map(n)])
        if has_res:
            y = y + res_scale * res_ref[0].astype(F32)
        if ln_bwd is None:
            refs[n_in][0] = y
        else:
            xh_ref, rs_ref, g_ref = refs[2 + has_res:5 + has_res]
            o_ref, dg_ref, db_ref = refs[n_in:n_in + 3]
            out, dg, db = _ln_bwd_rows(y, xh_ref[0], rs_ref[...], g_ref[...])
            o_ref[0] = out
            _accumulate(pl.program_id(0), (dg_ref, db_ref), (dg, db))

    tok = pl.BlockSpec((1, tm, kw), lambda i: (0, i, 0))
    vec = pl.BlockSpec((1, kw), lambda i: (0, 0))
    in_specs = [pl.BlockSpec((nbn, tm, nw), lambda i: (0, i, 0)),
                pl.BlockSpec(w.shape, lambda i: (0, 0, 0, 0))]
    args = [dy, w]
    if has_res:
        in_specs.append(tok)
        args.append(res)
    out_specs, out_shape = tok, jax.ShapeDtypeStruct((1, T, kw), F32)
    if ln_bwd is not None:
        in_specs += [tok, pl.BlockSpec((tm, 1), lambda i: (i, 0)), vec]
        args += list(ln_bwd)
        out_specs = [tok, vec, vec]
        out_shape = [out_shape, jax.ShapeDtypeStruct((1, kw), F32), jax.ShapeDtypeStruct((1, kw), F32)]
    in_specs += [pl.BlockSpec(memory_space=pl.ANY)] * len(behind)
    args += list(behind)
    return pl.pallas_call(body, name=name, grid=(T // tm,), in_specs=in_specs, out_specs=out_specs,
                          out_shape=out_shape, compiler_params=_params())(*args)


def _mm_tn(a, dy, *, name, tm=None, n_map=_same):
    nbk, T, kw = a.shape
    nbn, _, nw = dy.shape
    tm = min(tm or REDUCE_TOKEN_TILE, T)
    nt = T // tm

    def body(a_ref, dy_ref, o_ref, acc_ref):
        i = pl.program_id(2)
        part = _dot_tn(a_ref[...], dy_ref[...])

        @pl.when(i == 0)
        def _():
            acc_ref[...] = part

        @pl.when(i > 0)
        def _():
            acc_ref[...] += part

        @pl.when(i == nt - 1)
        def _():
            o_ref[...] = acc_ref[...].astype(o_ref.dtype)

    return pl.pallas_call(body, name=name, grid=(nbk, nbn, nt),
                          in_specs=[pl.BlockSpec((None, tm, kw), lambda k, n, i: (k, i, 0)),
                                    pl.BlockSpec((None, tm, nw), lambda k, n, i: (n, i, 0))],
                          out_specs=pl.BlockSpec((None, None, kw, nw), lambda k, n, i: (k, n_map(n), 0, 0)),
                          out_shape=jax.ShapeDtypeStruct((nbk, nbn, kw, nw), GRAD_DTYPE),
                          scratch_shapes=[pltpu.VMEM((kw, nw), F32)],
                          compiler_params=_params())(a, dy)


def _ln_bwd_rows(dh, xh, rstd, g):
    dxh = dh * g
    m1 = jnp.mean(dxh, axis=-1, keepdims=True)
    m2 = jnp.mean(dxh * xh, axis=-1, keepdims=True)
    dy = rstd * (dxh - m1 - xh * m2)
    return dy, jnp.sum(dh * xh, axis=0, keepdims=True), jnp.sum(dh, axis=0, keepdims=True)


def _accumulate(i, refs, parts):
    @pl.when(i == 0)
    def _():
        for r, p in zip(refs, parts):
            r[...] = jnp.zeros_like(r) + p

    @pl.when(i > 0)
    def _():
        for r, p in zip(refs, parts):
            r[...] += p


def _loss_ln_bwd(h, tgt, xhat, rstd, g, *, name, tm=None):
    _, T, D = h.shape
    tm = min(tm or TOKEN_TILE, T)

    def body(h_ref, t_ref, xh_ref, rs_ref, g_ref, dy_ref, dg_ref, db_ref, loss_ref):
        i = pl.program_id(0)
        err = h_ref[...] - t_ref[...]
        part = 0.5 * jnp.sum(jnp.mean(err * err, axis=-1, keepdims=True), axis=0, keepdims=True)
        dy, dg, db = _ln_bwd_rows(err / D, xh_ref[...], rs_ref[...], g_ref[...])
        dy_ref[...] = dy
        _accumulate(i, (dg_ref, db_ref, loss_ref), (dg, db, part))

    tok = pl.BlockSpec((None, tm, D), lambda i: (0, i, 0))
    vec = pl.BlockSpec((1, D), lambda i: (0, 0))
    return pl.pallas_call(body, name=name, grid=(T // tm,),
                          in_specs=[tok, tok, tok, pl.BlockSpec((tm, 1), lambda i: (i, 0)), vec],
                          out_specs=[tok, vec, vec, pl.BlockSpec((SUBLANES, LANES), lambda i: (0, 0))],
                          out_shape=[jax.ShapeDtypeStruct((1, T, D), F32), jax.ShapeDtypeStruct((1, D), F32),
                                     jax.ShapeDtypeStruct((1, D), F32), jax.ShapeDtypeStruct((SUBLANES, LANES), F32)],
                          compiler_params=_params())(h, tgt, xhat, rstd, g)


def _shift_rows(ext, k, n, halo):
    if k == 0:
        return ext[halo:halo + n]
    return pltpu.roll(ext, k, axis=0)[halo:halo + n]


def _conv_rows(ext, cw_ref, n, halo):
    return (cw_ref[0:1, :] * _shift_rows(ext, 2, n, halo) + cw_ref[1:2, :] * _shift_rows(ext, 1, n, halo)
            + cw_ref[2:3, :] * ext[halo:halo + n] + cw_ref[3:4, :])


def _pair_map(n):
    return n // 2 + 4 * (n % 2)


def _ffn_up(hb, w_in, cw, *, name, tm=None):
    _, T, D = hb.shape
    _, nb, _, fb = w_in.shape
    half = nb // 2
    tm = min(tm or TOKEN_TILE, T)

    def body(h_ref, wa_ref, wb_ref, cwa_ref, cwb_ref, u_ref, ab_ref, act_ref, carry):
        @pl.when(pl.program_id(1) == 0)
        def _():
            carry[...] = jnp.zeros_like(carry)

        h = h_ref[...]
        conv = []
        for s, (w_ref, cw_ref) in enumerate(((wa_ref, cwa_ref), (wb_ref, cwb_ref))):
            u = _dot(h, w_ref[...]).astype(u_ref.dtype)
            u_ref[s] = u
            uf = u.astype(F32)
            ext = jnp.concatenate([carry[s], uf], axis=0)
            c = _conv_rows(ext, cw_ref, tm, SUBLANES)
            ab_ref[s] = c.astype(ab_ref.dtype)
            conv.append(c)
            carry[s] = uf[tm - SUBLANES:tm]
        a, b = conv
        act_ref[...] = (a * _sigmoid(a) * b).astype(act_ref.dtype)

    wspec = lambda off: pl.BlockSpec((None, None, D, fb), lambda p, i: (0, p + off, 0, 0))
    cws = lambda off: pl.BlockSpec((None, SUBLANES, fb), lambda p, i: (p + off, 0, 0))
    return pl.pallas_call(body, name=name, grid=(half, T // tm),
                          in_specs=[pl.BlockSpec((None, tm, D), lambda p, i: (0, i, 0)), wspec(0), wspec(half),
                                    cws(0), cws(half)],
                          out_specs=[pl.BlockSpec((None, 2, tm, fb), lambda p, i: (p, 0, i, 0)),
                                     pl.BlockSpec((None, 2, tm, fb), lambda p, i: (p, 0, i, 0)),
                                     pl.BlockSpec((None, tm, fb), lambda p, i: (p, i, 0))],
                          out_shape=[jax.ShapeDtypeStruct((half, 2, T, fb), MXU),
                                     jax.ShapeDtypeStruct((half, 2, T, fb), MXU),
                                     jax.ShapeDtypeStruct((half, T, fb), MXU)],
                          scratch_shapes=[pltpu.VMEM((2, SUBLANES, fb), F32)],
                          compiler_params=_params())(hb, w_in, w_in, cw, cw)


def _ffn_gate_bwd(dy, u, ab, w_out, cw, *, name, tm=None):
    _, T, D = dy.shape
    half, _, _, fb = u.shape
    tm = min(tm or TOKEN_TILE, T)
    nt = T // tm

    def body(dy_ref, u_ref, ab_ref, w_ref, cwa_ref, cwb_ref, du_ref, dwo_ref, dcw_ref, carry, acc):
        i = pl.program_id(1)

        @pl.when(i == 0)
        def _():
            carry[...] = jnp.zeros_like(carry)
            acc[...] = jnp.zeros_like(acc)
            dcw_ref[...] = jnp.zeros_like(dcw_ref)

        dyv = dy_ref[...]
        dact = _dot_nt(dyv, w_ref[...])
        a = ab_ref[0].astype(F32)
        b = ab_ref[1].astype(F32)
        sa = _sigmoid(a)
        silu = a * sa
        acc[...] += _dot_tn(silu * b, dyv)
        dcs = (dact * b * (sa * (1.0 + a * (1.0 - sa))), dact * silu)
        m = tm + SUBLANES
        rows = lax.broadcasted_iota(jnp.int32, (SUBLANES, fb), 0)
        for s, cw_ref in enumerate((cwa_ref, cwb_ref)):
            dc = dcs[s]
            nxt = jnp.concatenate([dc, carry[s]], axis=0)
            dc1 = pltpu.roll(nxt, m - 1, axis=0)[:tm]
            dc2 = pltpu.roll(nxt, m - 2, axis=0)[:tm]
            du_ref[s] = (cw_ref[2:3, :] * dc + cw_ref[1:2, :] * dc1 + cw_ref[0:1, :] * dc2).astype(du_ref.dtype)
            carry[s] = dc[0:SUBLANES]
            uf = u_ref[s].astype(F32)
            g0 = jnp.sum(dc2 * uf, axis=0, keepdims=True)
            g1 = jnp.sum(dc1 * uf, axis=0, keepdims=True)
            g2 = jnp.sum(dc * uf, axis=0, keepdims=True)
            g3 = jnp.sum(dc, axis=0, keepdims=True)
            dcw_ref[s] += jnp.where(rows == 0, g0, jnp.where(rows == 1, g1, jnp.where(rows == 2, g2,
                                                                                jnp.where(rows == 3, g3, 0.0))))

        @pl.when(i == nt - 1)
        def _():
            dwo_ref[...] = acc[...].astype(dwo_ref.dtype)

    rev = lambda i: nt - 1 - i
    cws = lambda off: pl.BlockSpec((None, SUBLANES, fb), lambda p, i: (p + off, 0, 0))
    pair = lambda: pl.BlockSpec((None, 2, tm, fb), lambda p, i: (p, 0, rev(i), 0))
    return pl.pallas_call(body, name=name, grid=(half, nt),
                          in_specs=[pl.BlockSpec((None, tm, D), lambda p, i: (0, rev(i), 0)), pair(), pair(),
                                    pl.BlockSpec((None, None, fb, D), lambda p, i: (p, 0, 0, 0)), cws(0), cws(half)],
                          out_specs=[pair(),
                                     pl.BlockSpec((None, None, fb, D), lambda p, i: (p, 0, 0, 0)),
                                     pl.BlockSpec((None, 2, SUBLANES, fb), lambda p, i: (p, 0, 0, 0))],
                          out_shape=[jax.ShapeDtypeStruct((half, 2, T, fb), MXU),
                                     jax.ShapeDtypeStruct((half, 1, fb, D), GRAD_DTYPE),
                                     jax.ShapeDtypeStruct((half, 2, SUBLANES, fb), F32)],
                          scratch_shapes=[pltpu.VMEM((2, SUBLANES, fb), F32), pltpu.VMEM((fb, D), F32)],
                          compiler_params=_params())(dy, u, ab, w_out, cw, cw)


def _tri(n, lower):
    r = lax.broadcasted_iota(jnp.int32, (n, n), 0)
    c = lax.broadcasted_iota(jnp.int32, (n, n), 1)
    return (r >= c) if lower else (r <= c)


def _hgrn_gates(zq, zf, lb):
    sq = _sigmoid(zq)
    sf = _sigmoid(zf)
    fg = lb + (1.0 - lb) * sf
    return zq * sq, sq, sf, fg, jnp.log(fg)


def _lb_of(lbl_ref, cols):
    return _sigmoid(lbl_ref[0:1, cols] - lbl_ref[1:2, cols])


def _ones_where(mask):
    return jnp.where(mask, 1.0, 0.0).astype(jnp.bfloat16)


def _hgrn_fwd(z, lbl, gw, *, name):
    _, T, zw = z.shape
    C = min(HG_CHUNK, T)
    nch = T // C
    hpb = zw // HG_DIM

    def body(z_ref, lbl_ref, gw_ref, og_ref, st_ref, s_scr, bc_scr, q_scr, k_scr):
        c = pl.program_id(0)

        @pl.when(c == 0)
        def _():
            s_scr[...] = jnp.zeros_like(s_scr)

        low = _tri(C, True)
        low01 = _ones_where(low)
        gwv = gw_ref[...]
        H = range(HG_HEADS)
        for blk in range(2):
            cols = slice(blk * zw, (blk + 1) * zw)
            qq, _, _, fg, lf = _hgrn_gates(z_ref[blk], z_ref[2 + blk], _lb_of(lbl_ref, cols))
            q_scr[:, cols] = qq
            k_scr[:, cols] = 1.0 - fg
            bc_scr[:, cols] = _exact_dot(low01, lf)
        col = lambda h: slice(h * HG_DIM, (h + 1) * HG_DIM)
        zcol = lambda part, h: (part + h // hpb, slice(None), slice((h % hpb) * HG_DIM, (h % hpb + 1) * HG_DIM))
        b = [bc_scr[:, col(h)] for h in H]
        bm = [bc_scr[C // 2 - 1:C // 2, col(h)] for h in H]
        bl = [bc_scr[C - 1:C, col(h)] for h in H]
        q_ = [q_scr[:, col(h)] for h in H]
        k_ = [k_scr[:, col(h)] for h in H]
        v_ = [z_ref[zcol(4, h)] for h in H]
        qt = [q_[h] * jnp.exp(jnp.minimum(b[h] - bm[h], EXP_CLAMP)) for h in H]
        kt = [k_[h] * jnp.exp(jnp.minimum(bm[h] - b[h], EXP_CLAMP)) for h in H]
        A = [jnp.where(low, _dot_nt(qt[h], kt[h]), 0.0) for h in H]
        for h in H:
            st_ref[h] = s_scr[h]
        o = [_dot_nt(q_[h] * jnp.exp(b[h]), s_scr[h]) + _dot(A[h], v_[h]) for h in H]
        for h in H:
            s_scr[h] = s_scr[h] * jnp.exp(bl[h]) + _dot_tn(v_[h], k_[h] * jnp.exp(bl[h] - b[h]))
        for h in H:
            g_h = z_ref[zcol(6, h)]
            r = lax.rsqrt(jnp.mean(o[h] * o[h], axis=-1, keepdims=True) + RMS_EPS)
            og_ref[:, col(h)] = (o[h] * r * gwv * (g_h * _sigmoid(g_h))).astype(og_ref.dtype)

    return pl.pallas_call(body, name=name, grid=(nch,),
                          in_specs=[pl.BlockSpec((8, C, zw), lambda c: (0, c, 0)),
                                    pl.BlockSpec((2, D_MODEL), lambda c: (0, 0)),
                                    pl.BlockSpec((1, HG_DIM), lambda c: (0, 0))],
                          out_specs=[pl.BlockSpec((None, C, D_MODEL), lambda c: (0, c, 0)),
                                     pl.BlockSpec((None, HG_HEADS, HG_DIM, HG_DIM), lambda c: (c, 0, 0, 0))],
                          out_shape=[jax.ShapeDtypeStruct((1, T, D_MODEL), MXU),
                                     jax.ShapeDtypeStruct((nch, HG_HEADS, HG_DIM, HG_DIM), F32)],
                          scratch_shapes=[pltpu.VMEM((HG_HEADS, HG_DIM, HG_DIM), F32)]
                          + [pltpu.VMEM((C, D_MODEL), F32)] * 3,
                          compiler_params=_params())(z, lbl, gw)


def _hgrn_bwd(z, dog, states, lbl, gw, *, name):
    _, T, zw = z.shape
    C = min(HG_CHUNK, T)
    nch = T // C
    hpb = zw // HG_DIM

    def body(z_ref, dog_ref, st0_ref, st1_ref, lbl_ref, gw_ref, dz_ref, dlb_ref, dgw_ref,
             d_scr, bc_scr, q_scr, sf_scr, fg_scr, x_scr):
        step = pl.program_id(0)

        @pl.when(step == 0)
        def _():
            d_scr[...] = jnp.zeros_like(d_scr)
            dlb_ref[...] = jnp.zeros_like(dlb_ref)
            dgw_ref[...] = jnp.zeros_like(dgw_ref)

        low = _tri(C, True)
        low01 = _ones_where(low)
        up01 = _ones_where(_tri(C, False))
        gwv = gw_ref[...]
        H = range(HG_HEADS)
        for blk in range(2):
            lbb = _lb_of(lbl_ref, slice(blk * zw, (blk + 1) * zw))
            qq, sq, sf, fg, lf = _hgrn_gates(z_ref[blk], z_ref[2 + blk], lbb)
            q_scr[:, blk * zw:(blk + 1) * zw] = qq
            sf_scr[:, blk * zw:(blk + 1) * zw] = sf
            fg_scr[:, blk * zw:(blk + 1) * zw] = fg
            bc_scr[:, blk * zw:(blk + 1) * zw] = _exact_dot(low01, lf)
        col = lambda h: slice(h * HG_DIM, (h + 1) * HG_DIM)
        zcol = lambda part, h: (part + h // hpb, slice(None), slice((h % hpb) * HG_DIM, (h % hpb + 1) * HG_DIM))
        b = [bc_scr[:, col(h)] for h in H]
        bm = [bc_scr[C // 2 - 1:C // 2, col(h)] for h in H]
        bl = [bc_scr[C - 1:C, col(h)] for h in H]
        q_ = [q_scr[:, col(h)] for h in H]
        k_ = [1.0 - fg_scr[:, col(h)] for h in H]
        v_ = [z_ref[zcol(4, h)] for h in H]
        eq = [jnp.exp(jnp.minimum(b[h] - bm[h], EXP_CLAMP)) for h in H]
        ek = [jnp.exp(jnp.minimum(bm[h] - b[h], EXP_CLAMP)) for h in H]
        eb = [jnp.exp(b[h]) for h in H]
        el = [jnp.exp(bl[h] - b[h]) for h in H]
        qt = [q_[h] * eq[h] for h in H]
        kt = [k_[h] * ek[h] for h in H]
        q0 = [q_[h] * eb[h] for h in H]
        kd = [k_[h] * el[h] for h in H]
        A = [jnp.where(low, _dot_nt(qt[h], kt[h]), 0.0) for h in H]
        o = [_dot_nt(q0[h], st0_ref[h]) + _dot(A[h], v_[h]) for h in H]
        do = []
        dgw_acc = jnp.zeros((1, HG_DIM), F32)
        for h in H:
            g_h = z_ref[zcol(6, h)]
            r = lax.rsqrt(jnp.mean(o[h] * o[h], axis=-1, keepdims=True) + RMS_EPS)
            on = o[h] * r
            sg = _sigmoid(g_h)
            dogh = dog_ref[:, col(h)].astype(F32)
            t1 = dogh * on
            dgw_acc = dgw_acc + jnp.sum(t1 * (g_h * sg), axis=0, keepdims=True)
            dz_ref[zcol(6, h)] = (t1 * gwv * (sg * (1.0 + g_h * (1.0 - sg)))).astype(dz_ref.dtype)
            don = dogh * gwv * (g_h * sg)
            do.append(r * (don - on * jnp.mean(don * on, axis=-1, keepdims=True)))
        dgw_ref[...] += dgw_acc
        P = [jnp.where(low, _dot_nt(do[h], v_[h]), 0.0) for h in H]
        dqq = [eb[h] * _dot(do[h], st0_ref[h]) + eq[h] * _dot_hp(P[h], kt[h], ((1,), (0,))) for h in H]
        dkk = [el[h] * _dot(v_[h], d_scr[h]) + ek[h] * _dot_hp(P[h], qt[h], ((0,), (0,))) for h in H]
        for h in H:
            dz_ref[zcol(4, h)] = (_dot_nt(kd[h], d_scr[h]) + _dot_tn(A[h], do[h])).astype(dz_ref.dtype)
            x_scr[:, col(h)] = q_[h] * dqq[h] - k_[h] * dkk[h]
        edge = [jnp.sum(d_scr[h] * st1_ref[h], axis=0, keepdims=True) for h in H]
        for h in H:
            d_scr[h] = d_scr[h] * jnp.exp(bl[h]) + _dot_tn(do[h], q0[h])
        for blk in range(2):
            x_scr[:, blk * zw:(blk + 1) * zw] = _exact_dot(up01, x_scr[:, blk * zw:(blk + 1) * zw])
        dlb = []
        for h in H:
            dfg = (x_scr[:, col(h)] + edge[h]) / fg_scr[:, col(h)] - dkk[h]
            sf_h = sf_scr[:, col(h)]
            lb_h = _lb_of(lbl_ref, col(h))
            zq_h = z_ref[zcol(0, h)]
            sq_h = _sigmoid(zq_h)
            dlb.append(jnp.sum(dfg * (1.0 - sf_h), axis=0, keepdims=True))
            dz_ref[zcol(0, h)] = (dqq[h] * (sq_h * (1.0 + zq_h * (1.0 - sq_h)))).astype(dz_ref.dtype)
            dz_ref[zcol(2, h)] = (dfg * (1.0 - lb_h) * sf_h * (1.0 - sf_h)).astype(dz_ref.dtype)
        dlb_ref[...] += jnp.concatenate(dlb, axis=1)

    rev = lambda s: nch - 1 - s
    return pl.pallas_call(body, name=name, grid=(nch,),
                          in_specs=[pl.BlockSpec((8, C, zw), lambda s: (0, rev(s), 0)),
                                    pl.BlockSpec((None, C, D_MODEL), lambda s: (0, rev(s), 0)),
                                    pl.BlockSpec((None, HG_HEADS, HG_DIM, HG_DIM), lambda s: (rev(s), 0, 0, 0)),
                                    pl.BlockSpec((None, HG_HEADS, HG_DIM, HG_DIM),
                                                 lambda s: (jnp.minimum(rev(s) + 1, nch - 1), 0, 0, 0)),
                                    pl.BlockSpec((2, D_MODEL), lambda s: (0, 0)),
                                    pl.BlockSpec((1, HG_DIM), lambda s: (0, 0))],
                          out_specs=[pl.BlockSpec((8, C, zw), lambda s: (0, rev(s), 0)),
                                     pl.BlockSpec((1, D_MODEL), lambda s: (0, 0)),
                                     pl.BlockSpec((1, HG_DIM), lambda s: (0, 0))],
                          out_shape=[jax.ShapeDtypeStruct((8, T, zw), MXU), jax.ShapeDtypeStruct((1, D_MODEL), F32),
                                     jax.ShapeDtypeStruct((1, HG_DIM), F32)],
                          scratch_shapes=[pltpu.VMEM((HG_HEADS, HG_DIM, HG_DIM), F32)]
                          + [pltpu.VMEM((C, D_MODEL), F32)] * 5,
                          compiler_params=_params())(z, dog, states, states, lbl, gw)


def _bucket_onehot():
    W = SW_WINDOW
    t = np.arange(W)[:, None] + W
    s = np.arange(2 * W)[None, :]
    dist = t - s
    exact = REL_BUCKETS // 2
    d = np.maximum(np.maximum(dist, 0), 1).astype(np.float32)
    log_b = exact + (np.log(d / np.float32(exact)) / np.float32(math.log(REL_MAX_DIST / exact))
                     * np.float32(REL_BUCKETS - exact)).astype(np.int32)
    bucket = np.where(np.maximum(dist, 0) < exact, np.maximum(dist, 0), np.minimum(log_b, REL_BUCKETS - 1))
    valid = (dist >= 0) & (dist < W)
    onehot = (bucket[..., None] == np.arange(REL_BUCKETS)) & valid[..., None]
    return onehot.reshape(W * 2 * W, REL_BUCKETS).astype(np.float32)


def _bias_expand(rel_t, onehot_t, *, name):
    hq, nbk = rel_t.shape
    n = onehot_t.shape[1]

    def body(r_ref, oh_ref, o_ref):
        o_ref[...] = _exact_dot_r(r_ref[...], oh_ref[...])

    return pl.pallas_call(body, name=name, out_shape=jax.ShapeDtypeStruct((hq, n), F32),
                          compiler_params=_params())(rel_t, onehot_t)


def _bias_reduce(dbias, onehot, *, name):
    hq = dbias.shape[0]
    nbk = onehot.shape[1]

    def body(d_ref, oh_ref, o_ref):
        o_ref[...] = _exact_dot_r(d_ref[...], oh_ref[...])

    return pl.pallas_call(body, name=name, out_shape=jax.ShapeDtypeStruct((hq, nbk), F32),
                          compiler_params=_params())(dbias, onehot)


def _swa_mask(j):
    W = SW_WINDOW
    t = lax.broadcasted_iota(jnp.int32, (W, 2 * W), 0) + W
    s = lax.broadcasted_iota(jnp.int32, (W, 2 * W), 1)
    dist = t - s
    band = (dist >= 0) & (dist < W)
    m = band & ((j > 0) | (s >= W))
    return jnp.concatenate([m] * SW_GROUP, axis=0)


def _half_mask(rows, half):
    lane = lax.broadcasted_iota(jnp.int32, (rows, LANES), 1)
    return (lane >= SW_HEAD_DIM) if half else (lane < SW_HEAD_DIM)


def _swa_head(ref, col0, head, to_half):
    slab, half = head // 2, head % 2
    x = ref[:, col0 + slab * LANES:col0 + (slab + 1) * LANES]
    x = jnp.where(_half_mask(x.shape[0], half), x, 0.0)
    return x if half == to_half else pltpu.roll(x, SW_HEAD_DIM, axis=1)


def _swa_stack(ref, g):
    return jnp.concatenate([_swa_head(ref, 0, g * SW_GROUP + r, g % 2) for r in range(SW_GROUP)], axis=0)


def _swa_unstack(ref, x, g):
    W = SW_WINDOW
    for pair in range(SW_GROUP // 2):
        parts = []
        for r in (2 * pair, 2 * pair + 1):
            piece = x[r * W:(r + 1) * W]
            parts.append(piece if r % 2 == g % 2 else pltpu.roll(piece, SW_HEAD_DIM, axis=1))
        slab = (g * SW_GROUP) // 2 + pair
        ref[:, slab * LANES:(slab + 1) * LANES] = (parts[0] + parts[1]).astype(ref.dtype)


def _swa_kv(kp_ref, kc_ref, col0, g):
    return jnp.concatenate([_swa_head(kp_ref, col0, g, g % 2), _swa_head(kc_ref, col0, g, g % 2)], axis=0)


def _lane_pick(tile, h):
    lane = lax.broadcasted_iota(jnp.int32, tile.shape, 1)
    return jnp.sum(jnp.where(lane == h, tile, 0.0), axis=-1, keepdims=True)


def _lane_put(tile, h, col):
    lane = lax.broadcasted_iota(jnp.int32, tile.shape, 1)
    return jnp.where(lane == h, col, tile)


def _swa_rows(vals):
    return jnp.concatenate([jnp.broadcast_to(v, (SW_WINDOW, 1)) for v in vals], axis=0)


def _swa_fwd(q, kv, bias, sinks, *, name):
    _, T, D = q.shape
    W = SW_WINDOW
    nb = T // W
    dh = SW_HEAD_DIM
    kvw = SW_KV_HEADS * dh
    scale = dh ** -0.5

    def body(q_ref, kc_ref, kp_ref, bias_ref, sink_ref, o_ref, lse_ref):
        j = pl.program_id(0)
        mask = _swa_mask(j)
        sk = sink_ref[...]
        lse_tile = jnp.zeros((W, SW_Q_HEADS), F32)
        G = range(SW_KV_HEADS)
        kk = [_swa_kv(kp_ref, kc_ref, 0, g) for g in G]
        vv = [_swa_kv(kp_ref, kc_ref, kvw, g) for g in G]
        qs = [_swa_stack(q_ref, g) for g in G]
        logits = [jnp.where(mask, _dot_nt(qs[g], kk[g]) * scale
                            + bias_ref[g * SW_GROUP:(g + 1) * SW_GROUP].reshape(SW_GROUP * W, 2 * W), NEG_BIG) for g in G]
        sink = [_swa_rows([_lane_pick(sk, g * SW_GROUP + r) for r in range(SW_GROUP)]) for g in G]
        m = [jnp.maximum(jnp.max(logits[g], axis=-1, keepdims=True), sink[g]) for g in G]
        p = [jnp.exp(logits[g] - m[g]) for g in G]
        den = [jnp.sum(p[g], axis=-1, keepdims=True) + jnp.exp(sink[g] - m[g]) for g in G]
        pv = [_dot(p[g], vv[g]) for g in G]
        for g in G:
            _swa_unstack(o_ref, pv[g] / den[g], g)
            lse = m[g] + jnp.log(den[g])
            for r in range(SW_GROUP):
                lse_tile = _lane_put(lse_tile, g * SW_GROUP + r, lse[r * W:(r + 1) * W])
        lse_ref[...] = lse_tile

    return pl.pallas_call(body, name=name, grid=(nb,),
                          in_specs=[pl.BlockSpec((None, W, D), lambda j: (0, j, 0)),
                                    pl.BlockSpec((None, W, 2 * kvw), lambda j: (0, j, 0)),
                                    pl.BlockSpec((None, W, 2 * kvw), lambda j: (0, jnp.maximum(j - 1, 0), 0)),
                                    pl.BlockSpec((SW_Q_HEADS, W, 2 * W), lambda j: (0, 0, 0)),
                                    pl.BlockSpec((1, SW_Q_HEADS), lambda j: (0, 0))],
                          out_specs=[pl.BlockSpec((None, W, D), lambda j: (0, j, 0)),
                                     pl.BlockSpec((W, SW_Q_HEADS), lambda j: (j, 0))],
                          out_shape=[jax.ShapeDtypeStruct((1, T, D), F32), jax.ShapeDtypeStruct((T, SW_Q_HEADS), F32)],
                          compiler_params=_params())(q, kv, kv, bias, sinks)


def _swa_bwd(q, kv, o, lse, do, bias, sinks, *, name):
    _, T, D = q.shape
    W = SW_WINDOW
    nb = T // W
    dh = SW_HEAD_DIM
    kvw = SW_KV_HEADS * dh
    scale = dh ** -0.5
    cl = lambda j: jnp.minimum(j, nb - 1)

    def body(q_ref, kc_ref, kp_ref, o_ref, lse_ref, do_ref, bias_ref, sink_ref,
             dq_ref, dkv_ref, dbias_ref, dsink_ref, carry):
        j = pl.program_id(0)

        @pl.when(j == 0)
        def _():
            carry[...] = jnp.zeros_like(carry)
            dbias_ref[...] = jnp.zeros_like(dbias_ref)
            dsink_ref[...] = jnp.zeros_like(dsink_ref)

        @pl.when(j < nb)
        def _():
            mask = _swa_mask(j)
            sk = sink_ref[...]
            lse_tile = lse_ref[...]
            dsink = jnp.zeros((1, SW_Q_HEADS), F32)
            G = range(SW_KV_HEADS)
            heads = [[g * SW_GROUP + r for r in range(SW_GROUP)] for g in G]
            kk = [_swa_kv(kp_ref, kc_ref, 0, g) for g in G]
            vv = [_swa_kv(kp_ref, kc_ref, kvw, g) for g in G]
            qs = [_swa_stack(q_ref, g) for g in G]
            dos = [_swa_stack(do_ref, g) for g in G]
            lse = [jnp.concatenate([_lane_pick(lse_tile, h) for h in heads[g]], axis=0) for g in G]
            sink = [_swa_rows([_lane_pick(sk, h) for h in heads[g]]) for g in G]
            logits = [jnp.where(mask, _dot_nt(qs[g], kk[g]) * scale
                                + bias_ref[g * SW_GROUP:(g + 1) * SW_GROUP].reshape(SW_GROUP * W, 2 * W), NEG_BIG)
                      for g in G]
            dp = [_dot_nt(dos[g], vv[g]) for g in G]
            p = [jnp.exp(logits[g] - lse[g]) for g in G]
            delta = [jnp.sum(dos[g] * _swa_stack(o_ref, g), axis=-1, keepdims=True) for g in G]
            dl = [p[g] * (dp[g] - delta[g]) for g in G]
            dqs = [_dot(dl[g], kk[g]) * scale for g in G]
            dks = [_dot_tn(dl[g], qs[g]) * scale for g in G]
            dvs = [_dot_tn(p[g], dos[g]) for g in G]
            for g in G:
                _swa_unstack(dq_ref, dqs[g], g)
                dbias_ref[g * SW_GROUP:(g + 1) * SW_GROUP] += dl[g].reshape(SW_GROUP, W, 2 * W)
                sd = jnp.exp(sink[g] - lse[g]) * delta[g]
                for r, h in enumerate(heads[g]):
                    dsink = _lane_put(dsink, h, -jnp.sum(sd[r * W:(r + 1) * W], axis=0, keepdims=True))
            dsink_ref[...] += dsink
            for slab in range(SW_KV_HEADS // 2):
                for col0, parts in ((0, dks), (kvw, dvs)):
                    both = parts[2 * slab] + parts[2 * slab + 1]
                    cols = slice(col0 + slab * LANES, col0 + (slab + 1) * LANES)
                    dkv_ref[:, cols] = (carry[:, cols] + both[:W]).astype(dkv_ref.dtype)
                    carry[:, cols] = both[W:]

        @pl.when(j == nb)
        def _():
            dkv_ref[...] = carry[...].astype(dkv_ref.dtype)

    tok = lambda w: pl.BlockSpec((None, W, w), lambda j: (0, cl(j), 0))
    return pl.pallas_call(body, name=name, grid=(nb + 1,),
                          in_specs=[tok(D), tok(2 * kvw),
                                    pl.BlockSpec((None, W, 2 * kvw), lambda j: (0, jnp.maximum(cl(j) - 1, 0), 0)),
                                    tok(D), pl.BlockSpec((W, SW_Q_HEADS), lambda j: (cl(j), 0)), tok(D),
                                    pl.BlockSpec((SW_Q_HEADS, W, 2 * W), lambda j: (0, 0, 0)),
                                    pl.BlockSpec((1, SW_Q_HEADS), lambda j: (0, 0))],
                          out_specs=[tok(D),
                                     pl.BlockSpec((None, W, 2 * kvw), lambda j: (0, jnp.maximum(j - 1, 0), 0)),
                                     pl.BlockSpec((SW_Q_HEADS, W, 2 * W), lambda j: (0, 0, 0)),
                                     pl.BlockSpec((1, SW_Q_HEADS), lambda j: (0, 0))],
                          out_shape=[jax.ShapeDtypeStruct((1, T, D), MXU), jax.ShapeDtypeStruct((1, T, 2 * kvw), MXU),
                                     jax.ShapeDtypeStruct((SW_Q_HEADS, W, 2 * W), F32),
                                     jax.ShapeDtypeStruct((1, SW_Q_HEADS), F32)],
                          scratch_shapes=[pltpu.VMEM((W, 2 * kvw), F32)],
                          compiler_params=_params())(q, kv, kv, o, lse, do, bias, sinks)


_HBM = pl.BlockSpec(memory_space=pltpu.HBM)
_SEM = pl.BlockSpec(memory_space=pltpu.SEMAPHORE)
_EFFECT = pltpu.SideEffectType.DATAFLOW_SIDE_EFFECTING
N_PEERS = N_DEV - 1


def _peer(k):
    x, y, c = lax.axis_index("x"), lax.axis_index("y"), lax.axis_index("c")
    px = (x + (k >> 2)) % 2
    py = (y + ((k >> 1) & 1)) % 2
    pc = (c + (k & 1)) % 2
    return (px, py, pc), 4 * px + 2 * py + pc


def _my_number():
    return 4 * lax.axis_index("x") + 2 * lax.axis_index("y") + lax.axis_index("c")


def _landing(src, mode):
    me = _my_number()
    own = src if mode == "gather" else lax.dynamic_index_in_dim(src, me, 0, keepdims=False)
    return lax.dynamic_update_index_in_dim(lax.empty((N_DEV,) + own.shape, own.dtype), own, me, 0)


def _copy(src_ref, land_ref, mode, send, recv, j, k, dst_slot):
    peer, pid = _peer(k)
    return pltpu.make_async_remote_copy(
        src_ref=src_ref if mode == "gather" else src_ref.at[pid], dst_ref=land_ref.at[dst_slot(pid)],
        send_sem=send.at[j * N_PEERS + k - 1], recv_sem=recv.at[j * N_PEERS + k - 1],
        device_id=peer, device_id_type=pl.DeviceIdType.MESH)


def _send_start(groups, *, name):
    flat = [t for g in groups for t in g]
    n, ng = len(flat), len(groups)
    srcs = [pltpu.with_memory_space_constraint(s, pltpu.HBM) for s, _ in flat]
    lands = [pltpu.with_memory_space_constraint(_landing(s, m), pltpu.HBM) for s, m in flat]

    def body(*refs):
        src_refs, land_refs = refs[:n], refs[n:2 * n]
        sems = refs[2 * n:2 * n + 2 * ng]
        token = refs[-1]
        me = _my_number()
        a = 0
        for gi, g in enumerate(groups):
            for j, (_, mode) in enumerate(g):
                for k in range(1, N_DEV):
                    _copy(src_refs[a], land_refs[a], mode, sems[2 * gi], sems[2 * gi + 1], j, k, lambda pid: me).start()
                a += 1
        token[...] = jnp.zeros_like(token)

    sem_shapes = []
    for g in groups:
        sem_shapes += [pltpu.SemaphoreType.DMA((len(g) * N_PEERS,))] * 2
    out = pl.pallas_call(
        body, name=name,
        out_shape=tuple(sem_shapes) + tuple(pltpu.HBM(a.shape, a.dtype) for a in srcs + lands)
        + (jax.ShapeDtypeStruct((SUBLANES, LANES), F32),),
        in_specs=[_HBM] * (2 * n), out_specs=[_SEM] * (2 * ng) + [_HBM] * (2 * n) + [pl.BlockSpec(memory_space=pltpu.VMEM)],
        input_output_aliases={i: 2 * ng + i for i in range(2 * n)},
        compiler_params=pltpu.CompilerParams(has_side_effects=_EFFECT))(*srcs, *lands)
    sems, thru, token = out[:2 * ng], out[2 * ng:2 * ng + 2 * n], out[-1]
    handles, a = [], 0
    for gi, g in enumerate(groups):
        m = len(g)
        handles.append((sems[2 * gi], sems[2 * gi + 1], list(thru[a:a + m]), list(thru[n + a:n + a + m]),
                        [mode for _, mode in g]))
        a += m
    return handles, token


def _send_wait(handle, after, *, name):
    send, recv, srcs, lands, modes = handle
    m = len(srcs)

    def body(*refs):
        src_refs, land_refs = refs[:m], refs[m:2 * m]
        send_ref, recv_ref = refs[2 * m], refs[2 * m + 1]
        for j in range(m):
            for k in range(1, N_DEV):
                cp = _copy(src_refs[j], land_refs[j], modes[j], send_ref, recv_ref, j, k, lambda pid: pid)
                cp.wait_send()
                cp.wait_recv()

    out = pl.pallas_call(
        body, name=name, out_shape=tuple(pltpu.HBM(a.shape, a.dtype) for a in srcs + lands),
        in_specs=[_HBM] * (2 * m) + [_SEM, _SEM] + [pl.BlockSpec(memory_space=pl.ANY)] * len(after),
        out_specs=[_HBM] * (2 * m), input_output_aliases={i: i for i in range(2 * m)},
        compiler_params=pltpu.CompilerParams(has_side_effects=_EFFECT))(*srcs, *lands, send, recv, *after)
    return list(out[m:])


def _adam_math(w, g, m, v):
    m = ADAM_B1 * m + (1.0 - ADAM_B1) * g
    v = ADAM_B2 * v + (1.0 - ADAM_B2) * (g * g)
    m_hat = m / (1.0 - ADAM_B1 ** ADAM_STEP)
    v_hat = v / (1.0 - ADAM_B2 ** ADAM_STEP)
    delta = -ADAM_LR * (m_hat / (jnp.sqrt(v_hat) + ADAM_EPS) + ADAM_WD * w)
    return delta, m, v


def _adamw(parts, w, m, v, *, name, layer=None):
    S, R, C = parts.shape
    tr = R
    for cand in (256, 128, 64, 32, 16, 8):
        if R % cand == 0 and S * cand * C * 4 <= 4 * 2 ** 20:
            tr = cand
            break

    def body(p_ref, w_ref, m_ref, v_ref, g_ref, d_ref, nm_ref, nv_ref):
        g = p_ref[0].astype(F32)
        for s in range(1, S):
            g = g + p_ref[s].astype(F32)
        delta, nm, nv = _adam_math(w_ref[...], g, m_ref[...], v_ref[...])
        g_ref[...] = g
        d_ref[...] = delta
        nm_ref[...] = nm
        nv_ref[...] = nv

    if layer is None:
        wspec = pl.BlockSpec((tr, C), lambda i: (i, 0))
    else:
        wspec = pl.BlockSpec((None, tr, C), lambda i: (layer, i, 0))
    ospec = pl.BlockSpec((tr, C), lambda i: (i, 0))
    osh = jax.ShapeDtypeStruct((R, C), F32)
    return pl.pallas_call(body, name=name, grid=(R // tr,),
                          in_specs=[pl.BlockSpec((S, tr, C), lambda i: (0, i, 0)), wspec, wspec, wspec],
                          out_specs=[ospec] * 4, out_shape=[osh] * 4, compiler_params=_params())(parts, w, m, v)


def _sum_parts(parts, *, name):
    S, R, C = parts.shape

    def body(p_ref, o_ref):
        g = p_ref[0]
        for s in range(1, S):
            g = g + p_ref[s]
        o_ref[...] = g

    return pl.pallas_call(body, name=name, out_shape=jax.ShapeDtypeStruct((R, C), F32),
                          compiler_params=_params())(parts)


def _pack_rows(arrays):
    pieces, layout, row = [], [], 0
    for a in arrays:
        flat = a.reshape(-1).astype(F32)
        rows = -(-flat.shape[0] // (SUBLANES * LANES)) * SUBLANES
        flat = jnp.pad(flat, (0, rows * LANES - flat.shape[0]))
        pieces.append(flat.reshape(rows, LANES))
        layout.append((row, rows, a.shape))
        row += rows
    return jnp.concatenate(pieces, axis=0), layout


def _unpack_rows(packed, layout):
    out = []
    for row, rows, shape in layout:
        size = int(np.prod(shape))
        out.append(packed[row:row + rows].reshape(-1)[:size].reshape(shape))
    return out


def _ffn_fwd(h, w_in, w_out, cw, ln_g, ln_b, tag):
    h, hb = h
    u, ab, act = _ffn_up(hb, w_in, cw, name=f"ffn_up_{tag}")
    u = (u, ab)
    hn, hnb, xh, rs = _mm_nn(act, w_out, res=h, res_scale=ALPHA, ln=(ln_g, ln_b), name=f"ffn_down_{tag}")
    return (hn, hnb), xh, rs, u


def _ffn_bwd(dy, hb, u, w_in, w_out, cw, ln_bwd, send, tag):
    du, dw_out, dcw = _ffn_gate_bwd(dy, u[0], u[1], w_out, cw, name=f"ffn_gate_bwd_{tag}")
    du = du.reshape((-1,) + du.shape[2:])
    dw_in = _mm_tn(hb, du, n_map=_pair_map, name=f"ffn_dwin_{tag}")
    handle, token = send(dw_in, dw_out)
    dyp, dg, db = _mm_nt_resident(du, w_in, n_map=_pair_map, res=dy, res_scale=ALPHA, ln_bwd=ln_bwd,
                                  behind=(token,), name=f"ffn_dh_{tag}")
    dcw = dcw.transpose(1, 0, 2, 3).reshape((-1,) + dcw.shape[2:])
    return dyp, dg, db, handle, dcw


def kernel(x, hgrn_w_in, hgrn_lb_logits, hgrn_gnorm_w, hgrn_w_out, swa_w_q, swa_sinks, swa_w_out, shared_w_kv, rel_bias, ffn_w_in, ffn_conv_w, ffn_conv_b, ffn_w_out, ln_mix_g, ln_mix_b, ln_ffn_g, ln_ffn_b, loss_target, m_hgrn_w_in, m_hgrn_lb_logits, m_hgrn_gnorm_w, m_hgrn_w_out, m_swa_w_q, m_swa_sinks, m_swa_w_out, m_shared_w_kv, m_rel_bias, m_ffn_w_in, m_ffn_conv_w, m_ffn_conv_b, m_ffn_w_out, m_ln_mix_g, m_ln_mix_b, m_ln_ffn_g, m_ln_ffn_b, v_hgrn_w_in, v_hgrn_lb_logits, v_hgrn_gnorm_w, v_hgrn_w_out, v_swa_w_q, v_swa_sinks, v_swa_w_out, v_shared_w_kv, v_rel_bias, v_ffn_w_in, v_ffn_conv_w, v_ffn_conv_b, v_ffn_w_out, v_ln_mix_g, v_ln_mix_b, v_ln_ffn_g, v_ln_ffn_b):
    T = x.shape[1]
    D = D_MODEL
    W = SW_WINDOW
    fb = ffn_w_in.shape[2]
    me = 4 * lax.axis_index("x") + 2 * lax.axis_index("y") + lax.axis_index("c")

    small_fwd, small_fwd_layout = _pack_rows([hgrn_lb_logits, ffn_conv_w])
    gat = lambda *ws: [(w_.astype(MXU), "gather") for w_ in ws]
    (wait_a, wait_b, wait_c), _ = _send_start(
        [gat(hgrn_w_in[0]) + [(small_fwd, "gather")],
         gat(hgrn_w_out[0], ffn_w_in[0], ffn_w_out[0]),
         gat(shared_w_kv, swa_w_q[0], swa_w_out[0], ffn_w_in[1], ffn_w_out[1])], name="gather_start")
    xb = x.astype(MXU)
    w_hin, small_all = _send_wait(wait_a, (xb,), name="gather_wait_a")
    w_hin = w_hin[None]
    ffn_rows = 2 * ffn_w_out.shape[1]
    (lb_row, lb_rows, _), (cw_row, cw_rows, _) = small_fwd_layout
    lbl = small_all[:, lb_row:lb_row + 2, :].transpose(1, 0, 2).reshape(2, D)
    conv_w_all = small_all[:, cw_row:cw_row + cw_rows, :].reshape(N_DEV, -1)[:, :DEPTH * 3 * fb]
    conv_w_all = conv_w_all.reshape(N_DEV, DEPTH, 3, fb).transpose(1, 0, 2, 3)
    conv_b_all = ffn_conv_b.reshape(DEPTH, N_DEV, 1, fb)
    no_pad = ((0, 0), (0, 0))
    cw = (jnp.pad(conv_w_all, no_pad + ((0, SUBLANES - 3), (0, 0)))
          + jnp.pad(conv_b_all, no_pad + ((3, SUBLANES - 4), (0, 0))))

    row = lambda a, l: a[l:l + 1]

    z = _mm_nn(xb, w_hin, name="hgrn_in")
    og, states = _hgrn_fwd(z, lbl, hgrn_gnorm_w, name="hgrn_rec")
    w_hout, w_fin0, w_fout0 = _send_wait(wait_b, (og,), name="gather_wait_b")
    w_hout = w_hout.reshape(1, 1, D, D)
    w_fin = [w_fin0[None], None]
    w_fout = [w_fout0.reshape(4, 1, ffn_rows, D), None]
    h1, h1b, xh1, rs1 = _mm_nn(og, w_hout, res=x, res_scale=ALPHA, ln=(row(ln_mix_g, 0), row(ln_mix_b, 0)),
                               name="hgrn_out")
    (h2, h2b), xh2, rs2, u0 = _ffn_fwd((h1, h1b), w_fin[0], w_fout[0], cw[0], row(ln_ffn_g, 0), row(ln_ffn_b, 0), "l0")
    w_kv, w_q, w_o, w_fin1, w_fout1 = _send_wait(wait_c, (h2,), name="gather_wait_c")
    w_kv = w_kv.reshape(1, 1, D, 2 * SW_KV_HEADS * SW_HEAD_DIM)
    w_q = w_q.reshape(1, 1, D, D)
    w_o = w_o.reshape(1, 1, D, D)
    w_fin[1] = w_fin1[None]
    w_fout[1] = w_fout1.reshape(4, 1, ffn_rows, D)
    kv = _mm_nn(h2b, w_kv, name="swa_kv")
    q = _mm_nn(h2b, w_q, name="swa_q")
    onehot = _bucket_onehot()
    bias = _bias_expand(rel_bias.T, jnp.asarray(onehot.T, jnp.bfloat16), name="swa_bias").reshape(SW_Q_HEADS, W, 2 * W)
    ao, lse = _swa_fwd(q, kv, bias, swa_sinks, name="swa_attn")
    h3, h3b, xh3, rs3 = _mm_nn(ao, w_o, res=h2, res_scale=ALPHA, ln=(row(ln_mix_g, 1), row(ln_mix_b, 1)),
                               name="swa_out")
    (h4, _), xh4, rs4, u1 = _ffn_fwd((h3, h3b), w_fin[1], w_fout[1], cw[1], row(ln_ffn_g, 1), row(ln_ffn_b, 1), "l1")
    dy4, dg_f1, db_f1, loss_tile = _loss_ln_bwd(h4, loss_target, xh4, rs4, row(ln_ffn_g, 1), name="loss_ln_ffn1_bwd")
    sc = lambda *gs: [(g_, "scatter") for g_ in gs]

    def send_ffn(name_):
        def send(dw_in, dw_out):
            (handle,), token = _send_start([sc(dw_in.reshape(N_DEV, D, fb), dw_out.reshape(N_DEV, -1, D))], name=name_)
            return handle, token
        return send

    dy3, dg_m1, db_m1, ex1, dcw1 = _ffn_bwd(dy4, h3b, u1, w_fin[1], w_fout[1], cw[1],
                                            (xh3, rs3, row(ln_mix_g, 1)), send_ffn("grads_start_1"), "l1")
    dw_o = _mm_tn(ao, dy3, name="swa_dwo")
    dao = _mm_nt(dy3, w_o, name="swa_dao")
    dq, dkv, dbias, dsinks = _swa_bwd(q, kv, ao, lse, dao, bias, swa_sinks, name="swa_attn_bwd")
    drel_t = _bias_reduce(dbias.reshape(SW_Q_HEADS, W * 2 * W), jnp.asarray(onehot, jnp.bfloat16), name="swa_dbias")
    dw_q = _mm_tn(h2b, dq, name="swa_dwq")
    dw_kv = _mm_tn(h2b, dkv, name="swa_dwkv")
    dh2 = _mm_nt(dq, w_q, res=dy3, res_scale=ALPHA, name="swa_dh_q")
    (ex2,), tok2 = _send_start([sc(dw_o.reshape(N_DEV, D // N_DEV, D), dw_q.reshape(N_DEV, D // N_DEV, D),
                                   dw_kv.reshape(N_DEV, D // N_DEV, -1))], name="grads_start_2")
    dy2, dg_f0, db_f0 = _mm_nt_resident(dkv, w_kv, res=dh2, ln_bwd=(xh2, rs2, row(ln_ffn_g, 0)), behind=(tok2,),
                                        name="swa_dh_kv")
    dy1, dg_m0, db_m0, ex3, dcw0 = _ffn_bwd(dy2, h1b, u0, w_fin[0], w_fout[0], cw[0],
                                            (xh1, rs1, row(ln_mix_g, 0)), send_ffn("grads_start_3"), "l0")
    dw_hout = _mm_tn(og, dy1, name="hgrn_dwout")
    dog = _mm_nt(dy1, w_hout, name="hgrn_dog")
    dz, dlb, dgw = _hgrn_bwd(z, dog, states, lbl, hgrn_gnorm_w, name="hgrn_rec_bwd")
    dw_hin = _mm_tn(xb, dz, name="hgrn_dwin")

    p0 = _sigmoid(lbl[0:1] - lbl[1:2])
    dl0 = dlb * p0 * (1.0 - p0)
    d_lbl = dl0 * jnp.array([[1.0], [-1.0]], F32)
    dcw = jnp.stack([dcw0, dcw1], axis=0)
    d_conv_w = dcw[:, :, 0:3, :]
    d_conv_b = dcw[:, :, 3, :].reshape(DEPTH, N_DEV * fb)
    first_row = lax.broadcasted_iota(jnp.int32, (DEPTH, D), 0) == 0
    two_rows = lambda a, b: jnp.where(first_row, a, b)
    d_ln_mix_g = two_rows(dg_m0, dg_m1)
    d_ln_mix_b = two_rows(db_m0, db_m1)
    d_ln_ffn_g = two_rows(dg_f0, dg_f1)
    d_ln_ffn_b = two_rows(db_f0, db_f1)
    small_grads, small_layout = _pack_rows([d_lbl, d_conv_w, dgw, dsinks, drel_t.T, d_conv_b, d_ln_mix_g, d_ln_mix_b,
                                            d_ln_ffn_g, d_ln_ffn_b, loss_tile[0:1, 0:1]])

    (ex4,), tok4 = _send_start([sc(dw_hin.reshape(N_DEV, D, -1), dw_hout.reshape(N_DEV, D // N_DEV, D))
                                + [(small_grads, "gather")]], name="grads_start_4")
    dx = _mm_nt_resident(dz, w_hin, res=dy1, res_scale=ALPHA, name="hgrn_dx", behind=(tok4,))
    r_fin1, r_fout1 = _send_wait(ex1, (dx,), name="grads_wait_1")
    r_o, r_q, r_kv = _send_wait(ex2, (dx,), name="grads_wait_2")
    r_fin0, r_fout0 = _send_wait(ex3, (dx,), name="grads_wait_3")
    r_hin, r_hout, r_small = _send_wait(ex4, (dx,), name="grads_wait_4")
    received = [r_hin, r_hout, r_q, r_o, r_kv, r_fin0, r_fin1, r_fout0, r_fout1, r_small]

    outs = {}

    def put(name_, res):
        outs["grad_" + name_], outs["delta_" + name_], outs["new_m_" + name_], outs["new_v_" + name_] = res

    def big_update(name_, parts, w, m, v):
        shp = w.shape
        if w.ndim == 3 and shp[0] == 1:
            r = _adamw(parts, w[0], m[0], v[0], name="adamw_" + name_)
            put(name_, [a.reshape(shp) for a in r])
        else:
            r = _adamw(parts, w, m, v, name="adamw_" + name_)
            put(name_, r)

    big_update("hgrn_w_in", received[0], hgrn_w_in, m_hgrn_w_in, v_hgrn_w_in)
    big_update("hgrn_w_out", received[1], hgrn_w_out, m_hgrn_w_out, v_hgrn_w_out)
    big_update("swa_w_q", received[2], swa_w_q, m_swa_w_q, v_swa_w_q)
    big_update("swa_w_out", received[3], swa_w_out, m_swa_w_out, v_swa_w_out)
    big_update("shared_w_kv", received[4], shared_w_kv, m_shared_w_kv, v_shared_w_kv)
    for name_, idx, w, m, v in (("ffn_w_in", 5, ffn_w_in, m_ffn_w_in, v_ffn_w_in),
                                ("ffn_w_out", 7, ffn_w_out, m_ffn_w_out, v_ffn_w_out)):
        per_layer = [_adamw(received[idx + l], w, m, v, layer=l, name=f"adamw_{name_}_{l}") for l in range(DEPTH)]
        put(name_, [jnp.stack([per_layer[0][i], per_layer[1][i]], axis=0) for i in range(4)])

    small_sum = _sum_parts(received[9], name="sum_small_grads")
    (g_lbl, g_conv_w, g_gw, g_sinks, g_rel, g_conv_b, g_mix_g, g_mix_b, g_ffn_g, g_ffn_b,
     loss) = _unpack_rows(small_sum, small_layout)
    g_lbl_mine = lax.dynamic_slice_in_dim(g_lbl, me * (D // N_DEV), D // N_DEV, axis=1)
    g_conv_w_mine = lax.dynamic_index_in_dim(g_conv_w, me, axis=1, keepdims=False)
    small_names = ["hgrn_lb_logits", "ffn_conv_w", "hgrn_gnorm_w", "swa_sinks", "rel_bias", "ffn_conv_b",
                   "ln_mix_g", "ln_mix_b", "ln_ffn_g", "ln_ffn_b"]
    small_g = [g_lbl_mine, g_conv_w_mine, g_gw, g_sinks, g_rel, g_conv_b, g_mix_g, g_mix_b, g_ffn_g, g_ffn_b]
    small_w = [hgrn_lb_logits, ffn_conv_w, hgrn_gnorm_w, swa_sinks, rel_bias, ffn_conv_b, ln_mix_g, ln_mix_b,
               ln_ffn_g, ln_ffn_b]
    small_m = [m_hgrn_lb_logits, m_ffn_conv_w, m_hgrn_gnorm_w, m_swa_sinks, m_rel_bias, m_ffn_conv_b, m_ln_mix_g,
               m_ln_mix_b, m_ln_ffn_g, m_ln_ffn_b]
    small_v = [v_hgrn_lb_logits, v_ffn_conv_w, v_hgrn_gnorm_w, v_swa_sinks, v_rel_bias, v_ffn_conv_b, v_ln_mix_g,
               v_ln_mix_b, v_ln_ffn_g, v_ln_ffn_b]
    pg, lay = _pack_rows(small_g)
    pw, _ = _pack_rows(small_w)
    pm, _ = _pack_rows(small_m)
    pv, _ = _pack_rows(small_v)
    res = _adamw(pg[None], pw, pm, pv, name="adamw_small")
    unpacked = [_unpack_rows(r, lay) for r in res]
    for i, name_ in enumerate(small_names):
        put(name_, [unpacked[j][i] for j in range(4)])

    order = ["hgrn_w_in", "hgrn_lb_logits", "hgrn_gnorm_w", "hgrn_w_out", "swa_w_q", "swa_sinks", "swa_w_out",
             "shared_w_kv", "rel_bias", "ffn_w_in", "ffn_conv_w", "ffn_conv_b", "ffn_w_out", "ln_mix_g", "ln_mix_b",
             "ln_ffn_g", "ln_ffn_b"]
    result = [loss.reshape(()), dx]
    for kind in ("grad_", "delta_", "new_m_", "new_v_"):
        result += [outs[kind + n] for n in order]
    return tuple(result)
```

```python
import functools
import math

import numpy as np
import jax
import jax.numpy as jnp
from jax import lax
from jax.experimental import pallas as pl
from jax.experimental.pallas import tpu as pltpu

F32 = jnp.float32
MXU = jnp.bfloat16

N_DEV = 8
D_MODEL = 1024
DEPTH = 2
HG_HEADS = 8
HG_DIM = 128
HG_CHUNK = 64
SW_Q_HEADS = 16
SW_KV_HEADS = 4
SW_GROUP = 4
SW_HEAD_DIM = 64
SW_WINDOW = 128
REL_BUCKETS = 32
REL_MAX_DIST = 128
FFN_DIM = 2816
ALPHA = (2.0 * DEPTH) ** 0.25
LN_EPS = 1e-5
RMS_EPS = 1e-6
ADAM_LR = 0.001
ADAM_B1 = 0.9
ADAM_B2 = 0.999
ADAM_EPS = 1e-08
ADAM_WD = 0.01
ADAM_STEP = 10
EXP_CLAMP = 80.0
NEG_BIG = -1e30

SUBLANES = 8
LANES = 128
VMEM_LIMIT = 48 * 2 ** 20
TOKEN_TILE = 512
WIDE_TOKEN_TILE = 1024
RESIDENT_TOKEN_TILE = 256
REDUCE_TOKEN_TILE = 2048
GRAD_DTYPE = jnp.bfloat16


def _params(**kw):
    return pltpu.CompilerParams(vmem_limit_bytes=VMEM_LIMIT, **kw)


def _sigmoid(x):
    return 1.0 / (1.0 + jnp.exp(-x))


def _dot(a, b):
    return jnp.dot(a.astype(MXU), b.astype(MXU), preferred_element_type=F32)


def _dot_nt(a, b):
    return lax.dot_general(a.astype(MXU), b.astype(MXU), (((1,), (1,)), ((), ())), preferred_element_type=F32)


def _dot_tn(a, b):
    return lax.dot_general(a.astype(MXU), b.astype(MXU), (((0,), (0,)), ((), ())), preferred_element_type=F32)


def _trunc_bf16(x):
    bits = lax.bitcast_convert_type(x, jnp.int32)
    return lax.bitcast_convert_type(bits & jnp.int32(-65536), F32)


def _split3(x):
    hi = _trunc_bf16(x)
    r = x - hi
    mid = _trunc_bf16(r)
    lo = r - mid
    return hi.astype(jnp.bfloat16), mid.astype(jnp.bfloat16), lo.astype(jnp.bfloat16)


def _dot_hp(a, b, contract):
    def halves(x):
        hi = _trunc_bf16(x)
        return hi.astype(jnp.bfloat16), (x - hi).astype(jnp.bfloat16)

    ah, al = halves(a)
    bh, bl = halves(b)
    d = lambda p, q: lax.dot_general(p, q, (contract, ((), ())), preferred_element_type=F32)
    return d(ah, bh) + d(ah, bl) + d(al, bh)


def _exact_dot(m01, x):
    hi, mid, lo = _split3(x)
    d = lambda p: jnp.dot(m01, p, preferred_element_type=F32)
    return d(hi) + d(mid) + d(lo)


def _exact_dot_r(x, m01):
    hi, mid, lo = _split3(x)
    d = lambda p: jnp.dot(p, m01, preferred_element_type=F32)
    return d(hi) + d(mid) + d(lo)


def _mm_nn(a, w, *, name, res=None, res_scale=1.0, ln=None, out_dtype=F32, tm=None):
    nbk, T, kw = a.shape
    _, nbn, _, nw = w.shape
    tm = min(tm or TOKEN_TILE, T)
    has_res = res is not None
    assert ln is None or nbn == 1

    def body(*refs):
        refs = list(refs)
        a_ref, w_ref = refs[:2]
        pos = 2
        res_ref = None
        if has_res:
            res_ref = refs[pos]
            pos += 1
        if ln is not None:
            g_ref, b_ref = refs[pos:pos + 2]
            pos += 2
        o_ref = refs[pos]
        pos += 1
        if ln is not None:
            ob_ref, xh_ref, rs_ref = refs[pos:pos + 3]
        for n in range(nbn):
            y = _dot(a_ref[0], w_ref[0, n])
            for k in range(1, nbk):
                y = y + _dot(a_ref[k], w_ref[k, n])
            if has_res:
                y = y + res_scale * res_ref[n].astype(F32)
            if ln is None:
                o_ref[n] = y.astype(o_ref.dtype)
            else:
                mu = jnp.mean(y, axis=-1, keepdims=True)
                yc = y - mu
                var = jnp.mean(yc * yc, axis=-1, keepdims=True)
                rstd = lax.rsqrt(var + LN_EPS)
                xh = yc * rstd
                xh_ref[n] = xh
                rs_ref[...] = rstd
                h = xh * g_ref[...] + b_ref[...]
                o_ref[n] = h
                ob_ref[n] = h.astype(ob_ref.dtype)

    in_specs = [pl.BlockSpec((nbk, tm, kw), lambda i: (0, i, 0)),
                pl.BlockSpec((nbk, nbn, kw, nw), lambda i: (0, 0, 0, 0))]
    args = [a, w]
    if has_res:
        in_specs.append(pl.BlockSpec((nbn, tm, nw), lambda i: (0, i, 0)))
        args.append(res)
    if ln is not None:
        in_specs += [pl.BlockSpec((1, nw), lambda i: (0, 0))] * 2
        args += list(ln)
    out_spec = pl.BlockSpec((nbn, tm, nw), lambda i: (0, i, 0))
    out_shape = jax.ShapeDtypeStruct((nbn, T, nw), out_dtype)
    if ln is not None:
        out_specs = [out_spec, out_spec, out_spec, pl.BlockSpec((tm, 1), lambda i: (i, 0))]
        out_shape = [out_shape, jax.ShapeDtypeStruct((nbn, T, nw), MXU), jax.ShapeDtypeStruct((nbn, T, nw), F32),
                     jax.ShapeDtypeStruct((T, 1), F32)]
    else:
        out_specs = out_spec
    return pl.pallas_call(body, name=name, grid=(T // tm,), in_specs=in_specs, out_specs=out_specs,
                          out_shape=out_shape, compiler_params=_params())(*args)


def _same(n):
    return n


def _mm_nt(dy, w, *, name, res=None, res_scale=1.0, out_dtype=F32, tm=None, n_map=_same, behind=()):
    nbn, T, nw = dy.shape
    nbk, _, kw, _ = w.shape
    tm = min(tm or WIDE_TOKEN_TILE, T)
    has_res = res is not None

    def body(*refs):
        refs = list(refs)
        dy_ref, w_ref = refs[:2]
        pos = 2
        res_ref = None
        if has_res:
            res_ref = refs[pos]
            pos += 1
        pos += len(behind)
        o_ref = refs[pos]
        pos += 1
        acc_ref = refs[pos] if nbn > 1 else None
        n = pl.program_id(2)
        part = _dot_nt(dy_ref[...], w_ref[...])

        def finish(acc):
            y = acc
            if has_res:
                y = y + res_scale * res_ref[...].astype(F32)
            o_ref[...] = y.astype(o_ref.dtype)

        if nbn == 1:
            finish(part)
        else:
            @pl.when(n == 0)
            def _():
                acc_ref[...] = part

            @pl.when(n > 0)
            def _():
                acc_ref[...] += part

            @pl.when(n == nbn - 1)
            def _():
                finish(acc_ref[...])

    in_specs = [pl.BlockSpec((None, tm, nw), lambda i, k, n: (n, i, 0)),
                pl.BlockSpec((None, None, kw, nw), lambda i, k, n: (k, n_map(n), 0, 0))]
    args = [dy, w]
    if has_res:
        in_specs.append(pl.BlockSpec((None, tm, kw), lambda i, k, n: (k, i, 0)))
        args.append(res)
    in_specs += [pl.BlockSpec(memory_space=pl.ANY)] * len(behind)
    args += list(behind)
    scratch = [pltpu.VMEM((tm, kw), F32)] if nbn > 1 else []
    return pl.pallas_call(body, name=name, grid=(T // tm, nbk, nbn), in_specs=in_specs,
                          out_specs=pl.BlockSpec((None, tm, kw), lambda i, k, n: (k, i, 0)),
                          out_shape=jax.ShapeDtypeStruct((nbk, T, kw), out_dtype), scratch_shapes=scratch,
                          compiler_params=_params())(*args)


def _mm_nt_resident(dy, w, *, name, res=None, res_scale=1.0, ln_bwd=None, tm=None, n_map=_same, behind=()):
    nbn, T, nw = dy.shape
    nbk, _, kw, _ = w.shape
    assert nbk == 1
    tm = min(tm or RESIDENT_TOKEN_TILE, T)
    has_res = res is not None
    n_in = 2 + has_res + (3 if ln_bwd else 0) + len(behind)

    def body(*refs):
        dy_ref, w_ref = refs[:2]
        res_ref = refs[2] if has_res else None
        y = _dot_nt(dy_ref[0], w_ref[0, n_map(0)])
        for n in range(1, nbn):
            y = y + _dot_nt(dy_ref[n], w_ref[0, n_map(n)])
        if has_res:
            y = y + res_scale * res_ref[0].astype(F32)
        if ln_bwd is None:
            refs[n_in][0] = y
        else:
            xh_ref, rs_ref, g_ref = refs[2 + has_res:5 + has_res]
            o_ref, dg_ref, db_ref = refs[n_in:n_in + 3]
            out, dg, db = _ln_bwd_rows(y, xh_ref[0], rs_ref[...], g_ref[...])
            o_ref[0] = out
            _accumulate(pl.program_id(0), (dg_ref, db_ref), (dg, db))

    tok = pl.BlockSpec((1, tm, kw), lambda i: (0, i, 0))
    vec = pl.BlockSpec((1, kw), lambda i: (0, 0))
    in_specs = [pl.BlockSpec((nbn, tm, nw), lambda i: (0, i, 0)),
                pl.BlockSpec(w.shape, lambda i: (0, 0, 0, 0))]
    args = [dy, w]
    if has_res:
        in_specs.append(tok)
        args.append(res)
    out_specs, out_shape = tok, jax.ShapeDtypeStruct((1, T, kw), F32)
    if ln_bwd is not None:
        in_specs += [tok, pl.BlockSpec((tm, 1), lambda i: (i, 0)), vec]
        args += list(ln_bwd)
        out_specs = [tok, vec, vec]
        out_shape = [out_shape, jax.ShapeDtypeStruct((1, kw), F32), jax.ShapeDtypeStruct((1, kw), F32)]
    in_specs += [pl.BlockSpec(memory_space=pl.ANY)] * len(behind)
    args += list(behind)
    return pl.pallas_call(body, name=name, grid=(T // tm,), in_specs=in_specs, out_specs=out_specs,
                          out_shape=out_shape, compiler_params=_params())(*args)


def _mm_tn(a, dy, *, name, tm=None, n_map=_same):
    nbk, T, kw = a.shape
    nbn, _, nw = dy.shape
    tm = min(tm or REDUCE_TOKEN_TILE, T)
    nt = T // tm

    def body(a_ref, dy_ref, o_ref, acc_ref):
        i = pl.program_id(2)
        part = _dot_tn(a_ref[...], dy_ref[...])

        @pl.when(i == 0)
        def _():
            acc_ref[...] = part

        @pl.when(i > 0)
        def _():
            acc_ref[...] += part

        @pl.when(i == nt - 1)
        def _():
            o_ref[...] = acc_ref[...].astype(o_ref.dtype)

    return pl.pallas_call(body, name=name, grid=(nbk, nbn, nt),
                          in_specs=[pl.BlockSpec((None, tm, kw), lambda k, n, i: (k, i, 0)),
                                    pl.BlockSpec((None, tm, nw), lambda k, n, i: (n, i, 0))],
                          out_specs=pl.BlockSpec((None, None, kw, nw), lambda k, n, i: (k, n_map(n), 0, 0)),
                          out_shape=jax.ShapeDtypeStruct((nbk, nbn, kw, nw), GRAD_DTYPE),
                          scratch_shapes=[pltpu.VMEM((kw, nw), F32)],
                          compiler_params=_params())(a, dy)


def _ln_bwd_rows(dh, xh, rstd, g):
    dxh = dh * g
    m1 = jnp.mean(dxh, axis=-1, keepdims=True)
    m2 = jnp.mean(dxh * xh, axis=-1, keepdims=True)
    dy = rstd * (dxh - m1 - xh * m2)
    return dy, jnp.sum(dh * xh, axis=0, keepdims=True), jnp.sum(dh, axis=0, keepdims=True)


def _accumulate(i, refs, parts):
    @pl.when(i == 0)
    def _():
        for r, p in zip(refs, parts):
            r[...] = jnp.zeros_like(r) + p

    @pl.when(i > 0)
    def _():
        for r, p in zip(refs, parts):
            r[...] += p


def _loss_ln_bwd(h, tgt, xhat, rstd, g, *, name, tm=None):
    _, T, D = h.shape
    tm = min(tm or TOKEN_TILE, T)

    def body(h_ref, t_ref, xh_ref, rs_ref, g_ref, dy_ref, dg_ref, db_ref, loss_ref):
        i = pl.program_id(0)
        err = h_ref[...] - t_ref[...]
        part = 0.5 * jnp.sum(jnp.mean(err * err, axis=-1, keepdims=True), axis=0, keepdims=True)
        dy, dg, db = _ln_bwd_rows(err / D, xh_ref[...], rs_ref[...], g_ref[...])
        dy_ref[...] = dy
        _accumulate(i, (dg_ref, db_ref, loss_ref), (dg, db, part))

    tok = pl.BlockSpec((None, tm, D), lambda i: (0, i, 0))
    vec = pl.BlockSpec((1, D), lambda i: (0, 0))
    return pl.pallas_call(body, name=name, grid=(T // tm,),
                          in_specs=[tok, tok, tok, pl.BlockSpec((tm, 1), lambda i: (i, 0)), vec],
                          out_specs=[tok, vec, vec, pl.BlockSpec((SUBLANES, LANES), lambda i: (0, 0))],
                          out_shape=[jax.ShapeDtypeStruct((1, T, D), F32), jax.ShapeDtypeStruct((1, D), F32),
                                     jax.ShapeDtypeStruct((1, D), F32), jax.ShapeDtypeStruct((SUBLANES, LANES), F32)],
                          compiler_params=_params())(h, tgt, xhat, rstd, g)


def _shift_rows(ext, k, n, halo):
    if k == 0:
        return ext[halo:halo + n]
    return pltpu.roll(ext, k, axis=0)[halo:halo + n]


def _conv_rows(ext, cw_ref, n, halo):
    return (cw_ref[0:1, :] * _shift_rows(ext, 2, n, halo) + cw_ref[1:2, :] * _shift_rows(ext, 1, n, halo)
            + cw_ref[2:3, :] * ext[halo:halo + n] + cw_ref[3:4, :])


def _pair_map(n):
    return n // 2 + 4 * (n % 2)


def _ffn_up(hb, w_in, cw, *, name, tm=None):
    _, T, D = hb.shape
    _, nb, _, fb = w_in.shape
    half = nb // 2
    tm = min(tm or TOKEN_TILE, T)

    def body(h_ref, wa_ref, wb_ref, cwa_ref, cwb_ref, u_ref, ab_ref, act_ref, carry):
        @pl.when(pl.program_id(1) == 0)
        def _():
            carry[...] = jnp.zeros_like(carry)

        h = h_ref[...]
        conv = []
        for s, (w_ref, cw_ref) in enumerate(((wa_ref, cwa_ref), (wb_ref, cwb_ref))):
            u = _dot(h, w_ref[...]).astype(u_ref.dtype)
            u_ref[s] = u
            uf = u.astype(F32)
            ext = jnp.concatenate([carry[s], uf], axis=0)
            c = _conv_rows(ext, cw_ref, tm, SUBLANES)
            ab_ref[s] = c.astype(ab_ref.dtype)
            conv.append(c)
            carry[s] = uf[tm - SUBLANES:tm]
        a, b = conv
        act_ref[...] = (a * _sigmoid(a) * b).astype(act_ref.dtype)

    wspec = lambda off: pl.BlockSpec((None, None, D, fb), lambda p, i: (0, p + off, 0, 0))
    cws = lambda off: pl.BlockSpec((None, SUBLANES, fb), lambda p, i: (p + off, 0, 0))
    return pl.pallas_call(body, name=name, grid=(half, T // tm),
                          in_specs=[pl.BlockSpec((None, tm, D), lambda p, i: (0, i, 0)), wspec(0), wspec(half),
                                    cws(0), cws(half)],
                          out_specs=[pl.BlockSpec((None, 2, tm, fb), lambda p, i: (p, 0, i, 0)),
                                     pl.BlockSpec((None, 2, tm, fb), lambda p, i: (p, 0, i, 0)),
                                     pl.BlockSpec((None, tm, fb), lambda p, i: (p, i, 0))],
                          out_shape=[jax.ShapeDtypeStruct((half, 2, T, fb), MXU),
                                     jax.ShapeDtypeStruct((half, 2, T, fb), MXU),
                                     jax.ShapeDtypeStruct((half, T, fb), MXU)],
                          scratch_shapes=[pltpu.VMEM((2, SUBLANES, fb), F32)],
                          compiler_params=_params())(hb, w_in, w_in, cw, cw)


def _ffn_gate_bwd(dy, u, ab, w_out, cw, *, name, tm=None):
    _, T, D = dy.shape
    half, _, _, fb = u.shape
    tm = min(tm or TOKEN_TILE, T)
    nt = T // tm

    n_full = fb // LANES
    tail = slice(n_full * LANES, fb)

    def body(dy_ref, u_ref, ab_ref, w_ref, cwa_ref, cwb_ref, du_ref, dwo_ref, dcw_ref, carry, acc, gacc):
        i = pl.program_id(1)

        @pl.when(i == 0)
        def _():
            carry[...] = jnp.zeros_like(carry)
            acc[...] = jnp.zeros_like(acc)
            gacc[...] = jnp.zeros_like(gacc)
            dcw_ref[...] = jnp.zeros_like(dcw_ref)

        dyv = dy_ref[...]
        dact = _dot_nt(dyv, w_ref[...])
        a = ab_ref[0].astype(F32)
        b = ab_ref[1].astype(F32)
        sa = _sigmoid(a)
        silu = a * sa
        acc[...] += _dot_tn(silu * b, dyv)
        dcs = (dact * b * (sa * (1.0 + a * (1.0 - sa))), dact * silu)
        m = tm + SUBLANES
        rows = lax.broadcasted_iota(jnp.int32, (SUBLANES, fb), 0)
        for s, cw_ref in enumerate((cwa_ref, cwb_ref)):
            dc = dcs[s]
            nxt = jnp.concatenate([dc, carry[s]], axis=0)
            dc1 = pltpu.roll(nxt, m - 1, axis=0)[:tm]
            dc2 = pltpu.roll(nxt, m - 2, axis=0)[:tm]
            du_ref[s] = (cw_ref[2:3, :] * dc + cw_ref[1:2, :] * dc1 + cw_ref[0:1, :] * dc2).astype(du_ref.dtype)
            carry[s] = dc[0:SUBLANES]
            dcb = [x.astype(MXU) for x in (dc, dc1, dc2)]
            for j in range(n_full):
                blk = slice(j * LANES, (j + 1) * LANES)
                gacc[s, j] += _dot_tn(u_ref[s, :, blk], jnp.concatenate([x[:, blk] for x in dcb], axis=1))
            dcw_ref[s] += jnp.where(rows == 3, jnp.sum(dc, axis=0, keepdims=True), 0.0)
            if fb > n_full * LANES:
                ut = u_ref[s, :, tail].astype(F32)
                gt = [jnp.sum(x[:, tail] * ut, axis=0, keepdims=True) for x in (dc2, dc1, dc)]
                rt = rows[:, tail]
                dcw_ref[s, :, tail] += jnp.where(rt == 0, gt[0], jnp.where(rt == 1, gt[1], jnp.where(rt == 2, gt[2], 0.0)))

        @pl.when(i == nt - 1)
        def _():
            dwo_ref[...] = acc[...].astype(dwo_ref.dtype)
            eye = _tri(LANES, True) & _tri(LANES, False)
            for s in range(2):
                for j in range(n_full):
                    g = gacc[s, j]
                    for tap in range(3):
                        d = jnp.where(eye, g[:, (2 - tap) * LANES:(3 - tap) * LANES], 0.0)
                        dcw_ref[s, tap:tap + 1, j * LANES:(j + 1) * LANES] = jnp.sum(d, axis=0, keepdims=True)

    rev = lambda i: nt - 1 - i
    cws = lambda off: pl.BlockSpec((None, SUBLANES, fb), lambda p, i: (p + off, 0, 0))
    pair = lambda: pl.BlockSpec((None, 2, tm, fb), lambda p, i: (p, 0, rev(i), 0))
    return pl.pallas_call(body, name=name, grid=(half, nt),
                          in_specs=[pl.BlockSpec((None, tm, D), lambda p, i: (0, rev(i), 0)), pair(), pair(),
                                    pl.BlockSpec((None, None, fb, D), lambda p, i: (p, 0, 0, 0)), cws(0), cws(half)],
                          out_specs=[pair(),
                                     pl.BlockSpec((None, None, fb, D), lambda p, i: (p, 0, 0, 0)),
                                     pl.BlockSpec((None, 2, SUBLANES, fb), lambda p, i: (p, 0, 0, 0))],
                          out_shape=[jax.ShapeDtypeStruct((half, 2, T, fb), MXU),
                                     jax.ShapeDtypeStruct((half, 1, fb, D), GRAD_DTYPE),
                                     jax.ShapeDtypeStruct((half, 2, SUBLANES, fb), F32)],
                          scratch_shapes=[pltpu.VMEM((2, SUBLANES, fb), F32), pltpu.VMEM((fb, D), F32),
                                          pltpu.VMEM((2, n_full, LANES, 3 * LANES), F32)],
                          compiler_params=_params())(dy, u, ab, w_out, cw, cw)


def _tri(n, lower):
    r = lax.broadcasted_iota(jnp.int32, (n, n), 0)
    c = lax.broadcasted_iota(jnp.int32, (n, n), 1)
    return (r >= c) if lower else (r <= c)


def _hgrn_gates(zq, zf, lb):
    sq = _sigmoid(zq)
    sf = _sigmoid(zf)
    fg = lb + (1.0 - lb) * sf
    return zq * sq, sq, sf, fg, jnp.log(fg)


def _lb_of(lbl_ref, cols):
    return _sigmoid(lbl_ref[0:1, cols] - lbl_ref[1:2, cols])


def _ones_where(mask):
    return jnp.where(mask, 1.0, 0.0).astype(jnp.bfloat16)


def _hgrn_fwd(z, lbl, gw, *, name):
    _, T, zw = z.shape
    C = min(HG_CHUNK, T)
    nch = T // C
    hpb = zw // HG_DIM

    def body(z_ref, lbl_ref, gw_ref, og_ref, st_ref, s_scr, bc_scr, q_scr, k_scr):
        c = pl.program_id(0)

        @pl.when(c == 0)
        def _():
            s_scr[...] = jnp.zeros_like(s_scr)

        low = _tri(C, True)
        low01 = _ones_where(low)
        gwv = gw_ref[...]
        H = range(HG_HEADS)
        for blk in range(2):
            cols = slice(blk * zw, (blk + 1) * zw)
            qq, _, _, fg, lf = _hgrn_gates(z_ref[blk], z_ref[2 + blk], _lb_of(lbl_ref, cols))
            q_scr[:, cols] = qq
            k_scr[:, cols] = 1.0 - fg
            bc_scr[:, cols] = _exact_dot(low01, lf)
        col = lambda h: slice(h * HG_DIM, (h + 1) * HG_DIM)
        zcol = lambda part, h: (part + h // hpb, slice(None), slice((h % hpb) * HG_DIM, (h % hpb + 1) * HG_DIM))
        b = [bc_scr[:, col(h)] for h in H]
        bm = [bc_scr[C // 2 - 1:C // 2, col(h)] for h in H]
        bl = [bc_scr[C - 1:C, col(h)] for h in H]
        q_ = [q_scr[:, col(h)] for h in H]
        k_ = [k_scr[:, col(h)] for h in H]
        v_ = [z_ref[zcol(4, h)] for h in H]
        qt = [q_[h] * jnp.exp(jnp.minimum(b[h] - bm[h], EXP_CLAMP)) for h in H]
        kt = [k_[h] * jnp.exp(jnp.minimum(bm[h] - b[h], EXP_CLAMP)) for h in H]
        A = [jnp.where(low, _dot_nt(qt[h], kt[h]), 0.0) for h in H]
        for h in H:
            st_ref[h] = s_scr[h]
        o = [_dot_nt(q_[h] * jnp.exp(b[h]), s_scr[h]) + _dot(A[h], v_[h]) for h in H]
        for h in H:
            s_scr[h] = s_scr[h] * jnp.exp(bl[h]) + _dot_tn(v_[h], k_[h] * jnp.exp(bl[h] - b[h]))
        for h in H:
            g_h = z_ref[zcol(6, h)]
            r = lax.rsqrt(jnp.mean(o[h] * o[h], axis=-1, keepdims=True) + RMS_EPS)
            og_ref[:, col(h)] = (o[h] * r * gwv * (g_h * _sigmoid(g_h))).astype(og_ref.dtype)

    return pl.pallas_call(body, name=name, grid=(nch,),
                          in_specs=[pl.BlockSpec((8, C, zw), lambda c: (0, c, 0)),
                                    pl.BlockSpec((2, D_MODEL), lambda c: (0, 0)),
                                    pl.BlockSpec((1, HG_DIM), lambda c: (0, 0))],
                          out_specs=[pl.BlockSpec((None, C, D_MODEL), lambda c: (0, c, 0)),
                                     pl.BlockSpec((None, HG_HEADS, HG_DIM, HG_DIM), lambda c: (c, 0, 0, 0))],
                          out_shape=[jax.ShapeDtypeStruct((1, T, D_MODEL), MXU),
                                     jax.ShapeDtypeStruct((nch, HG_HEADS, HG_DIM, HG_DIM), F32)],
                          scratch_shapes=[pltpu.VMEM((HG_HEADS, HG_DIM, HG_DIM), F32)]
                          + [pltpu.VMEM((C, D_MODEL), F32)] * 3,
                          compiler_params=_params())(z, lbl, gw)


def _hgrn_bwd(z, dog, states, lbl, gw, *, name):
    _, T, zw = z.shape
    C = min(HG_CHUNK, T)
    nch = T // C
    hpb = zw // HG_DIM

    def body(z_ref, dog_ref, st0_ref, st1_ref, lbl_ref, gw_ref, dz_ref, dlb_ref, dgw_ref,
             d_scr, bc_scr, q_scr, sf_scr, fg_scr, x_scr):
        step = pl.program_id(0)

        @pl.when(step == 0)
        def _():
            d_scr[...] = jnp.zeros_like(d_scr)
            dlb_ref[...] = jnp.zeros_like(dlb_ref)
            dgw_ref[...] = jnp.zeros_like(dgw_ref)

        low = _tri(C, True)
        low01 = _ones_where(low)
        up01 = _ones_where(_tri(C, False))
        gwv = gw_ref[...]
        H = range(HG_HEADS)
        for blk in range(2):
            lbb = _lb_of(lbl_ref, slice(blk * zw, (blk + 1) * zw))
            qq, sq, sf, fg, lf = _hgrn_gates(z_ref[blk], z_ref[2 + blk], lbb)
            q_scr[:, blk * zw:(blk + 1) * zw] = qq
            sf_scr[:, blk * zw:(blk + 1) * zw] = sf
            fg_scr[:, blk * zw:(blk + 1) * zw] = fg
            bc_scr[:, blk * zw:(blk + 1) * zw] = _exact_dot(low01, lf)
        col = lambda h: slice(h * HG_DIM, (h + 1) * HG_DIM)
        zcol = lambda part, h: (part + h // hpb, slice(None), slice((h % hpb) * HG_DIM, (h % hpb + 1) * HG_DIM))
        b = [bc_scr[:, col(h)] for h in H]
        bm = [bc_scr[C // 2 - 1:C // 2, col(h)] for h in H]
        bl = [bc_scr[C - 1:C, col(h)] for h in H]
        q_ = [q_scr[:, col(h)] for h in H]
        k_ = [1.0 - fg_scr[:, col(h)] for h in H]
        v_ = [z_ref[zcol(4, h)] for h in H]
        eq = [jnp.exp(jnp.minimum(b[h] - bm[h], EXP_CLAMP)) for h in H]
        ek = [jnp.exp(jnp.minimum(bm[h] - b[h], EXP_CLAMP)) for h in H]
        eb = [jnp.exp(b[h]) for h in H]
        el = [jnp.exp(bl[h] - b[h]) for h in H]
        qt = [q_[h] * eq[h] for h in H]
        kt = [k_[h] * ek[h] for h in H]
        q0 = [q_[h] * eb[h] for h in H]
        kd = [k_[h] * el[h] for h in H]
        A = [jnp.where(low, _dot_nt(qt[h], kt[h]), 0.0) for h in H]
        o = [_dot_nt(q0[h], st0_ref[h]) + _dot(A[h], v_[h]) for h in H]
        do = []
        dgw_acc = jnp.zeros((1, HG_DIM), F32)
        for h in H:
            g_h = z_ref[zcol(6, h)]
            r = lax.rsqrt(jnp.mean(o[h] * o[h], axis=-1, keepdims=True) + RMS_EPS)
            on = o[h] * r
            sg = _sigmoid(g_h)
            dogh = dog_ref[:, col(h)].astype(F32)
            t1 = dogh * on
            dgw_acc = dgw_acc + jnp.sum(t1 * (g_h * sg), axis=0, keepdims=True)
            dz_ref[zcol(6, h)] = (t1 * gwv * (sg * (1.0 + g_h * (1.0 - sg)))).astype(dz_ref.dtype)
            don = dogh * gwv * (g_h * sg)
            do.append(r * (don - on * jnp.mean(don * on, axis=-1, keepdims=True)))
        dgw_ref[...] += dgw_acc
        P = [jnp.where(low, _dot_nt(do[h], v_[h]), 0.0) for h in H]
        dqq = [eb[h] * _dot(do[h], st0_ref[h]) + eq[h] * _dot_hp(P[h], kt[h], ((1,), (0,))) for h in H]
        dkk = [el[h] * _dot(v_[h], d_scr[h]) + ek[h] * _dot_hp(P[h], qt[h], ((0,), (0,))) for h in H]
        for h in H:
            dz_ref[zcol(4, h)] = (_dot_nt(kd[h], d_scr[h]) + _dot_tn(A[h], do[h])).astype(dz_ref.dtype)
            x_scr[:, col(h)] = q_[h] * dqq[h] - k_[h] * dkk[h]
        edge = [jnp.sum(d_scr[h] * st1_ref[h], axis=0, keepdims=True) for h in H]
        for h in H:
            d_scr[h] = d_scr[h] * jnp.exp(bl[h]) + _dot_tn(do[h], q0[h])
        for blk in range(2):
            x_scr[:, blk * zw:(blk + 1) * zw] = _exact_dot(up01, x_scr[:, blk * zw:(blk + 1) * zw])
        dlb = []
        for h in H:
            dfg = (x_scr[:, col(h)] + edge[h]) / fg_scr[:, col(h)] - dkk[h]
            sf_h = sf_scr[:, col(h)]
            lb_h = _lb_of(lbl_ref, col(h))
            zq_h = z_ref[zcol(0, h)]
            sq_h = _sigmoid(zq_h)
            dlb.append(jnp.sum(dfg * (1.0 - sf_h), axis=0, keepdims=True))
            dz_ref[zcol(0, h)] = (dqq[h] * (sq_h * (1.0 + zq_h * (1.0 - sq_h)))).astype(dz_ref.dtype)
            dz_ref[zcol(2, h)] = (dfg * (1.0 - lb_h) * sf_h * (1.0 - sf_h)).astype(dz_ref.dtype)
        dlb_ref[...] += jnp.concatenate(dlb, axis=1)

    rev = lambda s: nch - 1 - s
    return pl.pallas_call(body, name=name, grid=(nch,),
                          in_specs=[pl.BlockSpec((8, C, zw), lambda s: (0, rev(s), 0)),
                                    pl.BlockSpec((None, C, D_MODEL), lambda s: (0, rev(s), 0)),
                                    pl.BlockSpec((None, HG_HEADS, HG_DIM, HG_DIM), lambda s: (rev(s), 0, 0, 0)),
                                    pl.BlockSpec((None, HG_HEADS, HG_DIM, HG_DIM),
                                                 lambda s: (jnp.minimum(rev(s) + 1, nch - 1), 0, 0, 0)),
                                    pl.BlockSpec((2, D_MODEL), lambda s: (0, 0)),
                                    pl.BlockSpec((1, HG_DIM), lambda s: (0, 0))],
                          out_specs=[pl.BlockSpec((8, C, zw), lambda s: (0, rev(s), 0)),
                                     pl.BlockSpec((1, D_MODEL), lambda s: (0, 0)),
                                     pl.BlockSpec((1, HG_DIM), lambda s: (0, 0))],
                          out_shape=[jax.ShapeDtypeStruct((8, T, zw), MXU), jax.ShapeDtypeStruct((1, D_MODEL), F32),
                                     jax.ShapeDtypeStruct((1, HG_DIM), F32)],
                          scratch_shapes=[pltpu.VMEM((HG_HEADS, HG_DIM, HG_DIM), F32)]
                          + [pltpu.VMEM((C, D_MODEL), F32)] * 5,
                          compiler_params=_params())(z, dog, states, states, lbl, gw)


def _bucket_onehot():
    W = SW_WINDOW
    t = np.arange(W)[:, None] + W
    s = np.arange(2 * W)[None, :]
    dist = t - s
    exact = REL_BUCKETS // 2
    d = np.maximum(np.maximum(dist, 0), 1).astype(np.float32)
    log_b = exact + (np.log(d / np.float32(exact)) / np.float32(math.log(REL_MAX_DIST / exact))
                     * np.float32(REL_BUCKETS - exact)).astype(np.int32)
    bucket = np.where(np.maximum(dist, 0) < exact, np.maximum(dist, 0), np.minimum(log_b, REL_BUCKETS - 1))
    valid = (dist >= 0) & (dist < W)
    onehot = (bucket[..., None] == np.arange(REL_BUCKETS)) & valid[..., None]
    return onehot.reshape(W * 2 * W, REL_BUCKETS).astype(np.float32)


def _bias_expand(rel_t, onehot_t, *, name):
    hq, nbk = rel_t.shape
    n = onehot_t.shape[1]

    def body(r_ref, oh_ref, o_ref):
        o_ref[...] = _exact_dot_r(r_ref[...], oh_ref[...])

    return pl.pallas_call(body, name=name, out_shape=jax.ShapeDtypeStruct((hq, n), F32),
                          compiler_params=_params())(rel_t, onehot_t)


def _bias_reduce(dbias, onehot, *, name):
    hq = dbias.shape[0]
    nbk = onehot.shape[1]

    def body(d_ref, oh_ref, o_ref):
        o_ref[...] = _exact_dot_r(d_ref[...], oh_ref[...])

    return pl.pallas_call(body, name=name, out_shape=jax.ShapeDtypeStruct((hq, nbk), F32),
                          compiler_params=_params())(dbias, onehot)


def _swa_mask(j):
    W = SW_WINDOW
    t = lax.broadcasted_iota(jnp.int32, (W, 2 * W), 0) + W
    s = lax.broadcasted_iota(jnp.int32, (W, 2 * W), 1)
    dist = t - s
    band = (dist >= 0) & (dist < W)
    m = band & ((j > 0) | (s >= W))
    return jnp.concatenate([m] * SW_GROUP, axis=0)


def _half_mask(rows, half):
    lane = lax.broadcasted_iota(jnp.int32, (rows, LANES), 1)
    return (lane >= SW_HEAD_DIM) if half else (lane < SW_HEAD_DIM)


def _swa_head(ref, col0, head, to_half):
    slab, half = head // 2, head % 2
    x = ref[:, col0 + slab * LANES:col0 + (slab + 1) * LANES]
    x = jnp.where(_half_mask(x.shape[0], half), x, 0.0)
    return x if half == to_half else pltpu.roll(x, SW_HEAD_DIM, axis=1)


def _swa_stack(ref, g):
    return jnp.concatenate([_swa_head(ref, 0, g * SW_GROUP + r, g % 2) for r in range(SW_GROUP)], axis=0)


def _swa_unstack(ref, x, g):
    W = SW_WINDOW
    for pair in range(SW_GROUP // 2):
        parts = []
        for r in (2 * pair, 2 * pair + 1):
            piece = x[r * W:(r + 1) * W]
            parts.append(piece if r % 2 == g % 2 else pltpu.roll(piece, SW_HEAD_DIM, axis=1))
        slab = (g * SW_GROUP) // 2 + pair
        ref[:, slab * LANES:(slab + 1) * LANES] = (parts[0] + parts[1]).astype(ref.dtype)


def _swa_kv(kp_ref, kc_ref, col0, g):
    return jnp.concatenate([_swa_head(kp_ref, col0, g, g % 2), _swa_head(kc_ref, col0, g, g % 2)], axis=0)


def _lane_pick(tile, h):
    lane = lax.broadcasted_iota(jnp.int32, tile.shape, 1)
    return jnp.sum(jnp.where(lane == h, tile, 0.0), axis=-1, keepdims=True)


def _lane_put(tile, h, col):
    lane = lax.broadcasted_iota(jnp.int32, tile.shape, 1)
    return jnp.where(lane == h, col, tile)


def _swa_rows(vals):
    return jnp.concatenate([jnp.broadcast_to(v, (SW_WINDOW, 1)) for v in vals], axis=0)


def _swa_fwd(q, kv, bias, sinks, *, name):
    _, T, D = q.shape
    W = SW_WINDOW
    nb = T // W
    dh = SW_HEAD_DIM
    kvw = SW_KV_HEADS * dh
    scale = dh ** -0.5

    def body(q_ref, kc_ref, kp_ref, bias_ref, sink_ref, o_ref, lse_ref):
        j = pl.program_id(0)
        mask = _swa_mask(j)
        sk = sink_ref[...]
        lse_tile = jnp.zeros((W, SW_Q_HEADS), F32)
        G = range(SW_KV_HEADS)
        kk = [_swa_kv(kp_ref, kc_ref, 0, g) for g in G]
        vv = [_swa_kv(kp_ref, kc_ref, kvw, g) for g in G]
        qs = [_swa_stack(q_ref, g) for g in G]
        logits = [jnp.where(mask, _dot_nt(qs[g], kk[g]) * scale
                            + bias_ref[g * SW_GROUP:(g + 1) * SW_GROUP].reshape(SW_GROUP * W, 2 * W), NEG_BIG) for g in G]
        sink = [_swa_rows([_lane_pick(sk, g * SW_GROUP + r) for r in range(SW_GROUP)]) for g in G]
        m = [jnp.maximum(jnp.max(logits[g], axis=-1, keepdims=True), sink[g]) for g in G]
        p = [jnp.exp(logits[g] - m[g]) for g in G]
        den = [jnp.sum(p[g], axis=-1, keepdims=True) + jnp.exp(sink[g] - m[g]) for g in G]
        pv = [_dot(p[g], vv[g]) for g in G]
        for g in G:
            _swa_unstack(o_ref, pv[g] / den[g], g)
            lse = m[g] + jnp.log(den[g])
            for r in range(SW_GROUP):
                lse_tile = _lane_put(lse_tile, g * SW_GROUP + r, lse[r * W:(r + 1) * W])
        lse_ref[...] = lse_tile

    return pl.pallas_call(body, name=name, grid=(nb,),
                          in_specs=[pl.BlockSpec((None, W, D), lambda j: (0, j, 0)),
                                    pl.BlockSpec((None, W, 2 * kvw), lambda j: (0, j, 0)),
                                    pl.BlockSpec((None, W, 2 * kvw), lambda j: (0, jnp.maximum(j - 1, 0), 0)),
                                    pl.BlockSpec((SW_Q_HEADS, W, 2 * W), lambda j: (0, 0, 0)),
                                    pl.BlockSpec((1, SW_Q_HEADS), lambda j: (0, 0))],
                          out_specs=[pl.BlockSpec((None, W, D), lambda j: (0, j, 0)),
                                     pl.BlockSpec((W, SW_Q_HEADS), lambda j: (j, 0))],
                          out_shape=[jax.ShapeDtypeStruct((1, T, D), F32), jax.ShapeDtypeStruct((T, SW_Q_HEADS), F32)],
                          compiler_params=_params())(q, kv, kv, bias, sinks)


def _swa_bwd(q, kv, o, lse, do, bias, sinks, *, name):
    _, T, D = q.shape
    W = SW_WINDOW
    nb = T // W
    dh = SW_HEAD_DIM
    kvw = SW_KV_HEADS * dh
    scale = dh ** -0.5
    cl = lambda j: jnp.minimum(j, nb - 1)

    def body(q_ref, kc_ref, kp_ref, o_ref, lse_ref, do_ref, bias_ref, sink_ref,
             dq_ref, dkv_ref, dbias_ref, dsink_ref, carry):
        j = pl.program_id(0)

        @pl.when(j == 0)
        def _():
            carry[...] = jnp.zeros_like(carry)
            dbias_ref[...] = jnp.zeros_like(dbias_ref)
            dsink_ref[...] = jnp.zeros_like(dsink_ref)

        @pl.when(j < nb)
        def _():
            mask = _swa_mask(j)
            sk = sink_ref[...]
            lse_tile = lse_ref[...]
            dsink = jnp.zeros((1, SW_Q_HEADS), F32)
            G = range(SW_KV_HEADS)
            heads = [[g * SW_GROUP + r for r in range(SW_GROUP)] for g in G]
            kk = [_swa_kv(kp_ref, kc_ref, 0, g) for g in G]
            vv = [_swa_kv(kp_ref, kc_ref, kvw, g) for g in G]
            qs = [_swa_stack(q_ref, g) for g in G]
            dos = [_swa_stack(do_ref, g) for g in G]
            lse = [jnp.concatenate([_lane_pick(lse_tile, h) for h in heads[g]], axis=0) for g in G]
            sink = [_swa_rows([_lane_pick(sk, h) for h in heads[g]]) for g in G]
            logits = [jnp.where(mask, _dot_nt(qs[g], kk[g]) * scale
                                + bias_ref[g * SW_GROUP:(g + 1) * SW_GROUP].reshape(SW_GROUP * W, 2 * W), NEG_BIG)
                      for g in G]
            dp = [_dot_nt(dos[g], vv[g]) for g in G]
            p = [jnp.exp(logits[g] - lse[g]) for g in G]
            delta = [jnp.sum(dos[g] * _swa_stack(o_ref, g), axis=-1, keepdims=True) for g in G]
            dl = [p[g] * (dp[g] - delta[g]) for g in G]
            dqs = [_dot(dl[g], kk[g]) * scale for g in G]
            dks = [_dot_tn(dl[g], qs[g]) * scale for g in G]
            dvs = [_dot_tn(p[g], dos[g]) for g in G]
            for g in G:
                _swa_unstack(dq_ref, dqs[g], g)
                dbias_ref[g * SW_GROUP:(g + 1) * SW_GROUP] += dl[g].reshape(SW_GROUP, W, 2 * W)
                sd = jnp.exp(sink[g] - lse[g]) * delta[g]
                for r, h in enumerate(heads[g]):
                    dsink = _lane_put(dsink, h, -jnp.sum(sd[r * W:(r + 1) * W], axis=0, keepdims=True))
            dsink_ref[...] += dsink
            for slab in range(SW_KV_HEADS // 2):
                for col0, parts in ((0, dks), (kvw, dvs)):
                    both = parts[2 * slab] + parts[2 * slab + 1]
                    cols = slice(col0 + slab * LANES, col0 + (slab + 1) * LANES)
                    dkv_ref[:, cols] = (carry[:, cols] + both[:W]).astype(dkv_ref.dtype)
                    carry[:, cols] = both[W:]

        @pl.when(j == nb)
        def _():
            dkv_ref[...] = carry[...].astype(dkv_ref.dtype)

    tok = lambda w: pl.BlockSpec((None, W, w), lambda j: (0, cl(j), 0))
    return pl.pallas_call(body, name=name, grid=(nb + 1,),
                          in_specs=[tok(D), tok(2 * kvw),
                                    pl.BlockSpec((None, W, 2 * kvw), lambda j: (0, jnp.maximum(cl(j) - 1, 0), 0)),
                                    tok(D), pl.BlockSpec((W, SW_Q_HEADS), lambda j: (cl(j), 0)), tok(D),
                                    pl.BlockSpec((SW_Q_HEADS, W, 2 * W), lambda j: (0, 0, 0)),
                                    pl.BlockSpec((1, SW_Q_HEADS), lambda j: (0, 0))],
                          out_specs=[tok(D),
                                     pl.BlockSpec((None, W, 2 * kvw), lambda j: (0, jnp.maximum(j - 1, 0), 0)),
                                     pl.BlockSpec((SW_Q_HEADS, W, 2 * W), lambda j: (0, 0, 0)),
                                     pl.BlockSpec((1, SW_Q_HEADS), lambda j: (0, 0))],
                          out_shape=[jax.ShapeDtypeStruct((1, T, D), MXU), jax.ShapeDtypeStruct((1, T, 2 * kvw), MXU),
                                     jax.ShapeDtypeStruct((SW_Q_HEADS, W, 2 * W), F32),
                                     jax.ShapeDtypeStruct((1, SW_Q_HEADS), F32)],
                          scratch_shapes=[pltpu.VMEM((W, 2 * kvw), F32)],
                          compiler_params=_params())(q, kv, kv, o, lse, do, bias, sinks)


_HBM = pl.BlockSpec(memory_space=pltpu.HBM)
_SEM = pl.BlockSpec(memory_space=pltpu.SEMAPHORE)
_EFFECT = pltpu.SideEffectType.DATAFLOW_SIDE_EFFECTING
N_PEERS = N_DEV - 1


def _peer(k):
    x, y, c = lax.axis_index("x"), lax.axis_index("y"), lax.axis_index("c")
    px = (x + (k >> 2)) % 2
    py = (y + ((k >> 1) & 1)) % 2
    pc = (c + (k & 1)) % 2
    return (px, py, pc), 4 * px + 2 * py + pc


def _my_number():
    return 4 * lax.axis_index("x") + 2 * lax.axis_index("y") + lax.axis_index("c")


def _landing(src, mode):
    me = _my_number()
    own = src if mode == "gather" else lax.dynamic_index_in_dim(src, me, 0, keepdims=False)
    return lax.dynamic_update_index_in_dim(lax.empty((N_DEV,) + own.shape, own.dtype), own, me, 0)


def _copy(src_ref, land_ref, mode, send, recv, j, k, dst_slot):
    peer, pid = _peer(k)
    return pltpu.make_async_remote_copy(
        src_ref=src_ref if mode == "gather" else src_ref.at[pid], dst_ref=land_ref.at[dst_slot(pid)],
        send_sem=send.at[j * N_PEERS + k - 1], recv_sem=recv.at[j * N_PEERS + k - 1],
        device_id=peer, device_id_type=pl.DeviceIdType.MESH)


def _send_start(groups, *, name):
    flat = [t for g in groups for t in g]
    n, ng = len(flat), len(groups)
    srcs = [pltpu.with_memory_space_constraint(s, pltpu.HBM) for s, _ in flat]
    lands = [pltpu.with_memory_space_constraint(_landing(s, m), pltpu.HBM) for s, m in flat]

    def body(*refs):
        src_refs, land_refs = refs[:n], refs[n:2 * n]
        sems = refs[2 * n:2 * n + 2 * ng]
        token = refs[-1]
        me = _my_number()
        a = 0
        for gi, g in enumerate(groups):
            for j, (_, mode) in enumerate(g):
                for k in range(1, N_DEV):
                    _copy(src_refs[a], land_refs[a], mode, sems[2 * gi], sems[2 * gi + 1], j, k, lambda pid: me).start()
                a += 1
        token[...] = jnp.zeros_like(token)

    sem_shapes = []
    for g in groups:
        sem_shapes += [pltpu.SemaphoreType.DMA((len(g) * N_PEERS,))] * 2
    out = pl.pallas_call(
        body, name=name,
        out_shape=tuple(sem_shapes) + tuple(pltpu.HBM(a.shape, a.dtype) for a in srcs + lands)
        + (jax.ShapeDtypeStruct((SUBLANES, LANES), F32),),
        in_specs=[_HBM] * (2 * n), out_specs=[_SEM] * (2 * ng) + [_HBM] * (2 * n) + [pl.BlockSpec(memory_space=pltpu.VMEM)],
        input_output_aliases={i: 2 * ng + i for i in range(2 * n)},
        compiler_params=pltpu.CompilerParams(has_side_effects=_EFFECT))(*srcs, *lands)
    sems, thru, token = out[:2 * ng], out[2 * ng:2 * ng + 2 * n], out[-1]
    handles, a = [], 0
    for gi, g in enumerate(groups):
        m = len(g)
        handles.append((sems[2 * gi], sems[2 * gi + 1], list(thru[a:a + m]), list(thru[n + a:n + a + m]),
                        [mode for _, mode in g]))
        a += m
    return handles, token


def _send_wait(handle, after, *, name):
    send, recv, srcs, lands, modes = handle
    m = len(srcs)

    def body(*refs):
        src_refs, land_refs = refs[:m], refs[m:2 * m]
        send_ref, recv_ref = refs[2 * m], refs[2 * m + 1]
        for j in range(m):
            for k in range(1, N_DEV):
                cp = _copy(src_refs[j], land_refs[j], modes[j], send_ref, recv_ref, j, k, lambda pid: pid)
                cp.wait_send()
                cp.wait_recv()

    out = pl.pallas_call(
        body, name=name, out_shape=tuple(pltpu.HBM(a.shape, a.dtype) for a in srcs + lands),
        in_specs=[_HBM] * (2 * m) + [_SEM, _SEM] + [pl.BlockSpec(memory_space=pl.ANY)] * len(after),
        out_specs=[_HBM] * (2 * m), input_output_aliases={i: i for i in range(2 * m)},
        compiler_params=pltpu.CompilerParams(has_side_effects=_EFFECT))(*srcs, *lands, send, recv, *after)
    return list(out[m:])


def _adam_math(w, g, m, v):
    m = ADAM_B1 * m + (1.0 - ADAM_B1) * g
    v = ADAM_B2 * v + (1.0 - ADAM_B2) * (g * g)
    m_hat = m / (1.0 - ADAM_B1 ** ADAM_STEP)
    v_hat = v / (1.0 - ADAM_B2 ** ADAM_STEP)
    delta = -ADAM_LR * (m_hat / (jnp.sqrt(v_hat) + ADAM_EPS) + ADAM_WD * w)
    return delta, m, v


def _adamw(parts, w, m, v, *, name, layer=None):
    S, R, C = parts.shape
    tr = R
    for cand in (256, 128, 64, 32, 16, 8):
        if R % cand == 0 and S * cand * C * 4 <= 4 * 2 ** 20:
            tr = cand
            break

    def body(p_ref, w_ref, m_ref, v_ref, g_ref, d_ref, nm_ref, nv_ref):
        g = p_ref[0].astype(F32)
        for s in range(1, S):
            g = g + p_ref[s].astype(F32)
        delta, nm, nv = _adam_math(w_ref[...], g, m_ref[...], v_ref[...])
        g_ref[...] = g
        d_ref[...] = delta
        nm_ref[...] = nm
        nv_ref[...] = nv

    if layer is None:
        wspec = pl.BlockSpec((tr, C), lambda i: (i, 0))
    else:
        wspec = pl.BlockSpec((None, tr, C), lambda i: (layer, i, 0))
    ospec = pl.BlockSpec((tr, C), lambda i: (i, 0))
    osh = jax.ShapeDtypeStruct((R, C), F32)
    return pl.pallas_call(body, name=name, grid=(R // tr,),
                          in_specs=[pl.BlockSpec((S, tr, C), lambda i: (0, i, 0)), wspec, wspec, wspec],
                          out_specs=[ospec] * 4, out_shape=[osh] * 4, compiler_params=_params())(parts, w, m, v)


def _sum_parts(parts, *, name):
    S, R, C = parts.shape

    def body(p_ref, o_ref):
        g = p_ref[0]
        for s in range(1, S):
            g = g + p_ref[s]
        o_ref[...] = g

    return pl.pallas_call(body, name=name, out_shape=jax.ShapeDtypeStruct((R, C), F32),
                          compiler_params=_params())(parts)


def _pack_rows(arrays):
    pieces, layout, row = [], [], 0
    for a in arrays:
        flat = a.reshape(-1).astype(F32)
        rows = -(-flat.shape[0] // (SUBLANES * LANES)) * SUBLANES
        flat = jnp.pad(flat, (0, rows * LANES - flat.shape[0]))
        pieces.append(flat.reshape(rows, LANES))
        layout.append((row, rows, a.shape))
        row += rows
    return jnp.concatenate(pieces, axis=0), layout


def _unpack_rows(packed, layout):
    out = []
    for row, rows, shape in layout:
        size = int(np.prod(shape))
        out.append(packed[row:row + rows].reshape(-1)[:size].reshape(shape))
    return out


def _ffn_fwd(h, w_in, w_out, cw, ln_g, ln_b, tag):
    h, hb = h
    u, ab, act = _ffn_up(hb, w_in, cw, name=f"ffn_up_{tag}")
    u = (u, ab)
    hn, hnb, xh, rs = _mm_nn(act, w_out, res=h, res_scale=ALPHA, ln=(ln_g, ln_b), name=f"ffn_down_{tag}")
    return (hn, hnb), xh, rs, u


def _ffn_bwd(dy, hb, u, w_in, w_out, cw, ln_bwd, send, tag):
    du, dw_out, dcw = _ffn_gate_bwd(dy, u[0], u[1], w_out, cw, name=f"ffn_gate_bwd_{tag}")
    du = du.reshape((-1,) + du.shape[2:])
    dw_in = _mm_tn(hb, du, n_map=_pair_map, name=f"ffn_dwin_{tag}")
    handle, token = send(dw_in, dw_out)
    dyp, dg, db = _mm_nt_resident(du, w_in, n_map=_pair_map, res=dy, res_scale=ALPHA, ln_bwd=ln_bwd,
                                  behind=(token,), name=f"ffn_dh_{tag}")
    dcw = dcw.transpose(1, 0, 2, 3).reshape((-1,) + dcw.shape[2:])
    return dyp, dg, db, handle, dcw


def kernel(x, hgrn_w_in, hgrn_lb_logits, hgrn_gnorm_w, hgrn_w_out, swa_w_q, swa_sinks, swa_w_out, shared_w_kv, rel_bias, ffn_w_in, ffn_conv_w, ffn_conv_b, ffn_w_out, ln_mix_g, ln_mix_b, ln_ffn_g, ln_ffn_b, loss_target, m_hgrn_w_in, m_hgrn_lb_logits, m_hgrn_gnorm_w, m_hgrn_w_out, m_swa_w_q, m_swa_sinks, m_swa_w_out, m_shared_w_kv, m_rel_bias, m_ffn_w_in, m_ffn_conv_w, m_ffn_conv_b, m_ffn_w_out, m_ln_mix_g, m_ln_mix_b, m_ln_ffn_g, m_ln_ffn_b, v_hgrn_w_in, v_hgrn_lb_logits, v_hgrn_gnorm_w, v_hgrn_w_out, v_swa_w_q, v_swa_sinks, v_swa_w_out, v_shared_w_kv, v_rel_bias, v_ffn_w_in, v_ffn_conv_w, v_ffn_conv_b, v_ffn_w_out, v_ln_mix_g, v_ln_mix_b, v_ln_ffn_g, v_ln_ffn_b):
    T = x.shape[1]
    D = D_MODEL
    W = SW_WINDOW
    fb = ffn_w_in.shape[2]
    me = 4 * lax.axis_index("x") + 2 * lax.axis_index("y") + lax.axis_index("c")

    small_fwd, small_fwd_layout = _pack_rows([hgrn_lb_logits, ffn_conv_w])
    gat = lambda *ws: [(w_.astype(MXU), "gather") for w_ in ws]
    (wait_a, wait_b, wait_c), _ = _send_start(
        [gat(hgrn_w_in[0]) + [(small_fwd, "gather")],
         gat(hgrn_w_out[0], ffn_w_in[0], ffn_w_out[0]),
         gat(shared_w_kv, swa_w_q[0], swa_w_out[0], ffn_w_in[1], ffn_w_out[1])], name="gather_start")
    xb = x.astype(MXU)
    w_hin, small_all = _send_wait(wait_a, (xb,), name="gather_wait_a")
    w_hin = w_hin[None]
    ffn_rows = 2 * ffn_w_out.shape[1]
    (lb_row, lb_rows, _), (cw_row, cw_rows, _) = small_fwd_layout
    lbl = small_all[:, lb_row:lb_row + 2, :].transpose(1, 0, 2).reshape(2, D)
    conv_w_all = small_all[:, cw_row:cw_row + cw_rows, :].reshape(N_DEV, -1)[:, :DEPTH * 3 * fb]
    conv_w_all = conv_w_all.reshape(N_DEV, DEPTH, 3, fb).transpose(1, 0, 2, 3)
    conv_b_all = ffn_conv_b.reshape(DEPTH, N_DEV, 1, fb)
    no_pad = ((0, 0), (0, 0))
    cw = (jnp.pad(conv_w_all, no_pad + ((0, SUBLANES - 3), (0, 0)))
          + jnp.pad(conv_b_all, no_pad + ((3, SUBLANES - 4), (0, 0))))

    row = lambda a, l: a[l:l + 1]

    z = _mm_nn(xb, w_hin, name="hgrn_in")
    og, states = _hgrn_fwd(z, lbl, hgrn_gnorm_w, name="hgrn_rec")
    w_hout, w_fin0, w_fout0 = _send_wait(wait_b, (og,), name="gather_wait_b")
    w_hout = w_hout.reshape(1, 1, D, D)
    w_fin = [w_fin0[None], None]
    w_fout = [w_fout0.reshape(4, 1, ffn_rows, D), None]
    h1, h1b, xh1, rs1 = _mm_nn(og, w_hout, res=x, res_scale=ALPHA, ln=(row(ln_mix_g, 0), row(ln_mix_b, 0)),
                               name="hgrn_out")
    (h2, h2b), xh2, rs2, u0 = _ffn_fwd((h1, h1b), w_fin[0], w_fout[0], cw[0], row(ln_ffn_g, 0), row(ln_ffn_b, 0), "l0")
    w_kv, w_q, w_o, w_fin1, w_fout1 = _send_wait(wait_c, (h2,), name="gather_wait_c")
    w_kv = w_kv.reshape(1, 1, D, 2 * SW_KV_HEADS * SW_HEAD_DIM)
    w_q = w_q.reshape(1, 1, D, D)
    w_o = w_o.reshape(1, 1, D, D)
    w_fin[1] = w_fin1[None]
    w_fout[1] = w_fout1.reshape(4, 1, ffn_rows, D)
    kv = _mm_nn(h2b, w_kv, name="swa_kv")
    q = _mm_nn(h2b, w_q, name="swa_q")
    onehot = _bucket_onehot()
    bias = _bias_expand(rel_bias.T, jnp.asarray(onehot.T, jnp.bfloat16), name="swa_bias").reshape(SW_Q_HEADS, W, 2 * W)
    ao, lse = _swa_fwd(q, kv, bias, swa_sinks, name="swa_attn")
    h3, h3b, xh3, rs3 = _mm_nn(ao, w_o, res=h2, res_scale=ALPHA, ln=(row(ln_mix_g, 1), row(ln_mix_b, 1)),
                               name="swa_out")
    (h4, _), xh4, rs4, u1 = _ffn_fwd((h3, h3b), w_fin[1], w_fout[1], cw[1], row(ln_ffn_g, 1), row(ln_ffn_b, 1), "l1")
    dy4, dg_f1, db_f1, loss_tile = _loss_ln_bwd(h4, loss_target, xh4, rs4, row(ln_ffn_g, 1), name="loss_ln_ffn1_bwd")
    sc = lambda *gs: [(g_, "scatter") for g_ in gs]

    def send_ffn(name_):
        def send(dw_in, dw_out):
            (handle,), token = _send_start([sc(dw_in.reshape(N_DEV, D, fb), dw_out.reshape(N_DEV, -1, D))], name=name_)
            return handle, token
        return send

    dy3, dg_m1, db_m1, ex1, dcw1 = _ffn_bwd(dy4, h3b, u1, w_fin[1], w_fout[1], cw[1],
                                            (xh3, rs3, row(ln_mix_g, 1)), send_ffn("grads_start_1"), "l1")
    dw_o = _mm_tn(ao, dy3, name="swa_dwo")
    dao = _mm_nt(dy3, w_o, name="swa_dao")
    dq, dkv, dbias, dsinks = _swa_bwd(q, kv, ao, lse, dao, bias, swa_sinks, name="swa_attn_bwd")
    drel_t = _bias_reduce(dbias.reshape(SW_Q_HEADS, W * 2 * W), jnp.asarray(onehot, jnp.bfloat16), name="swa_dbias")
    dw_q = _mm_tn(h2b, dq, name="swa_dwq")
    dw_kv = _mm_tn(h2b, dkv, name="swa_dwkv")
    dh2 = _mm_nt(dq, w_q, res=dy3, res_scale=ALPHA, name="swa_dh_q")
    (ex2,), tok2 = _send_start([sc(dw_o.reshape(N_DEV, D // N_DEV, D), dw_q.reshape(N_DEV, D // N_DEV, D),
                                   dw_kv.reshape(N_DEV, D // N_DEV, -1))], name="grads_start_2")
    dy2, dg_f0, db_f0 = _mm_nt_resident(dkv, w_kv, res=dh2, ln_bwd=(xh2, rs2, row(ln_ffn_g, 0)), behind=(tok2,),
                                        name="swa_dh_kv")
    dy1, dg_m0, db_m0, ex3, dcw0 = _ffn_bwd(dy2, h1b, u0, w_fin[0], w_fout[0], cw[0],
                                            (xh1, rs1, row(ln_mix_g, 0)), send_ffn("grads_start_3"), "l0")
    dw_hout = _mm_tn(og, dy1, name="hgrn_dwout")
    dog = _mm_nt(dy1, w_hout, name="hgrn_dog")
    dz, dlb, dgw = _hgrn_bwd(z, dog, states, lbl, hgrn_gnorm_w, name="hgrn_rec_bwd")
    dw_hin = _mm_tn(xb, dz, name="hgrn_dwin")

    p0 = _sigmoid(lbl[0:1] - lbl[1:2])
    dl0 = dlb * p0 * (1.0 - p0)
    d_lbl = dl0 * jnp.array([[1.0], [-1.0]], F32)
    dcw = jnp.stack([dcw0, dcw1], axis=0)
    d_conv_w = dcw[:, :, 0:3, :]
    d_conv_b = dcw[:, :, 3, :].reshape(DEPTH, N_DEV * fb)
    first_row = lax.broadcasted_iota(jnp.int32, (DEPTH, D), 0) == 0
    two_rows = lambda a, b: jnp.where(first_row, a, b)
    d_ln_mix_g = two_rows(dg_m0, dg_m1)
    d_ln_mix_b = two_rows(db_m0, db_m1)
    d_ln_ffn_g = two_rows(dg_f0, dg_f1)
    d_ln_ffn_b = two_rows(db_f0, db_f1)
    small_grads, small_layout = _pack_rows([d_lbl, d_conv_w, dgw, dsinks, drel_t.T, d_conv_b, d_ln_mix_g, d_ln_mix_b,
                                            d_ln_ffn_g, d_ln_ffn_b, loss_tile[0:1, 0:1]])

    (ex4,), tok4 = _send_start([sc(dw_hin.reshape(N_DEV, D, -1), dw_hout.reshape(N_DEV, D // N_DEV, D))
                                + [(small_grads, "gather")]], name="grads_start_4")
    dx = _mm_nt_resident(dz, w_hin, res=dy1, res_scale=ALPHA, name="hgrn_dx", behind=(tok4,))
    r_fin1, r_fout1 = _send_wait(ex1, (dx,), name="grads_wait_1")
    r_o, r_q, r_kv = _send_wait(ex2, (dx,), name="grads_wait_2")
    r_fin0, r_fout0 = _send_wait(ex3, (dx,), name="grads_wait_3")
    r_hin, r_hout, r_small = _send_wait(ex4, (dx,), name="grads_wait_4")
    received = [r_hin, r_hout, r_q, r_o, r_kv, r_fin0, r_fin1, r_fout0, r_fout1, r_small]

    outs = {}

    def put(name_, res):
        outs["grad_" + name_], outs["delta_" + name_], outs["new_m_" + name_], outs["new_v_" + name_] = res

    def big_update(name_, parts, w, m, v):
        shp = w.shape
        if w.ndim == 3 and shp[0] == 1:
            r = _adamw(parts, w[0], m[0], v[0], name="adamw_" + name_)
            put(name_, [a.reshape(shp) for a in r])
        else:
            r = _adamw(parts, w, m, v, name="adamw_" + name_)
            put(name_, r)

    big_update("hgrn_w_in", received[0], hgrn_w_in, m_hgrn_w_in, v_hgrn_w_in)
    big_update("hgrn_w_out", received[1], hgrn_w_out, m_hgrn_w_out, v_hgrn_w_out)
    big_update("swa_w_q", received[2], swa_w_q, m_swa_w_q, v_swa_w_q)
    big_update("swa_w_out", received[3], swa_w_out, m_swa_w_out, v_swa_w_out)
    big_update("shared_w_kv", received[4], shared_w_kv, m_shared_w_kv, v_shared_w_kv)
    for name_, idx, w, m, v in (("ffn_w_in", 5, ffn_w_in, m_ffn_w_in, v_ffn_w_in),
                                ("ffn_w_out", 7, ffn_w_out, m_ffn_w_out, v_ffn_w_out)):
        per_layer = [_adamw(received[idx + l], w, m, v, layer=l, name=f"adamw_{name_}_{l}") for l in range(DEPTH)]
        put(name_, [jnp.stack([per_layer[0][i], per_layer[1][i]], axis=0) for i in range(4)])

    small_sum = _sum_parts(received[9], name="sum_small_grads")
    (g_lbl, g_conv_w, g_gw, g_sinks, g_rel, g_conv_b, g_mix_g, g_mix_b, g_ffn_g, g_ffn_b,
     loss) = _unpack_rows(small_sum, small_layout)
    g_lbl_mine = lax.dynamic_slice_in_dim(g_lbl, me * (D // N_DEV), D // N_DEV, axis=1)
    g_conv_w_mine = lax.dynamic_index_in_dim(g_conv_w, me, axis=1, keepdims=False)
    small_names = ["hgrn_lb_logits", "ffn_conv_w", "hgrn_gnorm_w", "swa_sinks", "rel_bias", "ffn_conv_b",
                   "ln_mix_g", "ln_mix_b", "ln_ffn_g", "ln_ffn_b"]
    small_g = [g_lbl_mine, g_conv_w_mine, g_gw, g_sinks, g_rel, g_conv_b, g_mix_g, g_mix_b, g_ffn_g, g_ffn_b]
    small_w = [hgrn_lb_logits, ffn_conv_w, hgrn_gnorm_w, swa_sinks, rel_bias, ffn_conv_b, ln_mix_g, ln_mix_b,
               ln_ffn_g, ln_ffn_b]
    small_m = [m_hgrn_lb_logits, m_ffn_conv_w, m_hgrn_gnorm_w, m_swa_sinks, m_rel_bias, m_ffn_conv_b, m_ln_mix_g,
               m_ln_mix_b, m_ln_ffn_g, m_ln_ffn_b]
    small_v = [v_hgrn_lb_logits, v_ffn_conv_w, v_hgrn_gnorm_w, v_swa_sinks, v_rel_bias, v_ffn_conv_b, v_ln_mix_g,
               v_ln_mix_b, v_ln_ffn_g, v_ln_ffn_b]
    pg, lay = _pack_rows(small_g)
    pw, _ = _pack_rows(small_w)
    pm, _ = _pack_rows(small_m)
    pv, _ = _pack_rows(small_v)
    res = _adamw(pg[None], pw, pm, pv, name="adamw_small")
    unpacked = [_unpack_rows(r, lay) for r in res]
    for i, name_ in enumerate(small_names):
        put(name_, [unpacked[j][i] for j in range(4)])

    order = ["hgrn_w_in", "hgrn_lb_logits", "hgrn_gnorm_w", "hgrn_w_out", "swa_w_q", "swa_sinks", "swa_w_out",
             "shared_w_kv", "rel_bias", "ffn_w_in", "ffn_conv_w", "ffn_conv_b", "ffn_w_out", "ln_mix_g", "ln_mix_b",
             "ln_ffn_g", "ln_ffn_b"]
    result = [loss.reshape(()), dx]
    for kind in ("grad_", "delta_", "new_m_", "new_v_"):
        result += [outs[kind + n] for n in order]
    return tuple(result)
```

```python
import functools
import math

import numpy as np
import jax
import jax.numpy as jnp
from jax import lax
from jax.experimental import pallas as pl
from jax.experimental.pallas import tpu as pltpu

F32 = jnp.float32
MXU = jnp.bfloat16

N_DEV = 8
D_MODEL = 1024
DEPTH = 2
HG_HEADS = 8
HG_DIM = 128
HG_CHUNK = 64
SW_Q_HEADS = 16
SW_KV_HEADS = 4
SW_GROUP = 4
SW_HEAD_DIM = 64
SW_WINDOW = 128
REL_BUCKETS = 32
REL_MAX_DIST = 128
FFN_DIM = 2816
ALPHA = (2.0 * DEPTH) ** 0.25
LN_EPS = 1e-5
RMS_EPS = 1e-6
ADAM_LR = 0.001
ADAM_B1 = 0.9
ADAM_B2 = 0.999
ADAM_EPS = 1e-08
ADAM_WD = 0.01
ADAM_STEP = 10
EXP_CLAMP = 80.0
NEG_BIG = -1e30

SUBLANES = 8
LANES = 128
VMEM_LIMIT = 48 * 2 ** 20
TOKEN_TILE = 512
WIDE_TOKEN_TILE = 1024
RESIDENT_TOKEN_TILE = 256
REDUCE_TOKEN_TILE = 2048
GRAD_DTYPE = jnp.bfloat16


def _params(**kw):
    return pltpu.CompilerParams(vmem_limit_bytes=VMEM_LIMIT, **kw)


def _sigmoid(x):
    return 1.0 / (1.0 + jnp.exp(-x))


def _dot(a, b):
    return jnp.dot(a.astype(MXU), b.astype(MXU), preferred_element_type=F32)


def _dot_nt(a, b):
    return lax.dot_general(a.astype(MXU), b.astype(MXU), (((1,), (1,)), ((), ())), preferred_element_type=F32)


def _dot_tn(a, b):
    return lax.dot_general(a.astype(MXU), b.astype(MXU), (((0,), (0,)), ((), ())), preferred_element_type=F32)


def _trunc_bf16(x):
    bits = lax.bitcast_convert_type(x, jnp.int32)
    return lax.bitcast_convert_type(bits & jnp.int32(-65536), F32)


def _split3(x):
    hi = _trunc_bf16(x)
    r = x - hi
    mid = _trunc_bf16(r)
    lo = r - mid
    return hi.astype(jnp.bfloat16), mid.astype(jnp.bfloat16), lo.astype(jnp.bfloat16)


def _dot_hp(a, b, contract):
    def halves(x):
        hi = _trunc_bf16(x)
        return hi.astype(jnp.bfloat16), (x - hi).astype(jnp.bfloat16)

    ah, al = halves(a)
    bh, bl = halves(b)
    d = lambda p, q: lax.dot_general(p, q, (contract, ((), ())), preferred_element_type=F32)
    return d(ah, bh) + d(ah, bl) + d(al, bh)


def _exact_dot(m01, x):
    hi, mid, lo = _split3(x)
    d = lambda p: jnp.dot(m01, p, preferred_element_type=F32)
    return d(hi) + d(mid) + d(lo)


def _exact_dot_r(x, m01):
    hi, mid, lo = _split3(x)
    d = lambda p: jnp.dot(p, m01, preferred_element_type=F32)
    return d(hi) + d(mid) + d(lo)


def _mm_nn(a, w, *, name, res=None, res_scale=1.0, ln=None, out_dtype=F32, tm=None):
    nbk, T, kw = a.shape
    _, nbn, _, nw = w.shape
    tm = min(tm or TOKEN_TILE, T)
    has_res = res is not None
    res_ln = isinstance(res, tuple)
    n_res = (3 if res_ln else 1) if has_res else 0
    assert ln is None or nbn == 1

    def body(*refs):
        refs = list(refs)
        a_ref, w_ref = refs[:2]
        res_refs = refs[2:2 + n_res]
        pos = 2 + n_res
        if ln is not None:
            g_ref, b_ref = refs[pos:pos + 2]
            pos += 2
        o_ref = refs[pos]
        if ln is not None:
            xh_ref, rs_ref = refs[pos + 1:pos + 3]
        for n in range(nbn):
            y = _dot(a_ref[0], w_ref[0, n])
            for k in range(1, nbk):
                y = y + _dot(a_ref[k], w_ref[k, n])
            if res_ln:
                y = y + res_scale * (res_refs[0][n] * res_refs[1][...] + res_refs[2][...])
            elif has_res:
                y = y + res_scale * res_refs[0][n].astype(F32)
            if ln is None:
                o_ref[n] = y.astype(o_ref.dtype)
            else:
                mu = jnp.mean(y, axis=-1, keepdims=True)
                yc = y - mu
                var = jnp.mean(yc * yc, axis=-1, keepdims=True)
                rstd = lax.rsqrt(var + LN_EPS)
                xh = yc * rstd
                xh_ref[n] = xh
                rs_ref[...] = rstd
                o_ref[n] = (xh * g_ref[...] + b_ref[...]).astype(o_ref.dtype)

    vec = pl.BlockSpec((1, nw), lambda i: (0, 0))
    in_specs = [pl.BlockSpec((nbk, tm, kw), lambda i: (0, i, 0)),
                pl.BlockSpec((nbk, nbn, kw, nw), lambda i: (0, 0, 0, 0))]
    args = [a, w]
    if has_res:
        in_specs.append(pl.BlockSpec((nbn, tm, nw), lambda i: (0, i, 0)))
        if res_ln:
            in_specs += [vec, vec]
            args += list(res)
        else:
            args.append(res)
    if ln is not None:
        in_specs += [vec, vec]
        args += list(ln)
    out_spec = pl.BlockSpec((nbn, tm, nw), lambda i: (0, i, 0))
    out_shape = jax.ShapeDtypeStruct((nbn, T, nw), out_dtype)
    if ln is not None:
        out_specs = [out_spec, out_spec, pl.BlockSpec((tm, 1), lambda i: (i, 0))]
        out_shape = [jax.ShapeDtypeStruct((nbn, T, nw), MXU), jax.ShapeDtypeStruct((nbn, T, nw), F32),
                     jax.ShapeDtypeStruct((T, 1), F32)]
    else:
        out_specs = out_spec
    return pl.pallas_call(body, name=name, grid=(T // tm,), in_specs=in_specs, out_specs=out_specs,
                          out_shape=out_shape, compiler_params=_params())(*args)


def _same(n):
    return n


def _mm_nt(dy, w, *, name, res=None, res_scale=1.0, out_dtype=F32, tm=None, n_map=_same, behind=()):
    nbn, T, nw = dy.shape
    nbk, _, kw, _ = w.shape
    tm = min(tm or WIDE_TOKEN_TILE, T)
    has_res = res is not None

    def body(*refs):
        refs = list(refs)
        dy_ref, w_ref = refs[:2]
        pos = 2
        res_ref = None
        if has_res:
            res_ref = refs[pos]
            pos += 1
        pos += len(behind)
        o_ref = refs[pos]
        pos += 1
        acc_ref = refs[pos] if nbn > 1 else None
        n = pl.program_id(2)
        part = _dot_nt(dy_ref[...], w_ref[...])

        def finish(acc):
            y = acc
            if has_res:
                y = y + res_scale * res_ref[...].astype(F32)
            o_ref[...] = y.astype(o_ref.dtype)

        if nbn == 1:
            finish(part)
        else:
            @pl.when(n == 0)
            def _():
                acc_ref[...] = part

            @pl.when(n > 0)
            def _():
                acc_ref[...] += part

            @pl.when(n == nbn - 1)
            def _():
                finish(acc_ref[...])

    in_specs = [pl.BlockSpec((None, tm, nw), lambda i, k, n: (n, i, 0)),
                pl.BlockSpec((None, None, kw, nw), lambda i, k, n: (k, n_map(n), 0, 0))]
    args = [dy, w]
    if has_res:
        in_specs.append(pl.BlockSpec((None, tm, kw), lambda i, k, n: (k, i, 0)))
        args.append(res)
    in_specs += [pl.BlockSpec(memory_space=pl.ANY)] * len(behind)
    args += list(behind)
    scratch = [pltpu.VMEM((tm, kw), F32)] if nbn > 1 else []
    return pl.pallas_call(body, name=name, grid=(T // tm, nbk, nbn), in_specs=in_specs,
                          out_specs=pl.BlockSpec((None, tm, kw), lambda i, k, n: (k, i, 0)),
                          out_shape=jax.ShapeDtypeStruct((nbk, T, kw), out_dtype), scratch_shapes=scratch,
                          compiler_params=_params())(*args)


def _mm_nt_resident(dy, w, *, name, res=None, res_scale=1.0, ln_bwd=None, tm=None, n_map=_same, behind=()):
    nbn, T, nw = dy.shape
    nbk, _, kw, _ = w.shape
    assert nbk == 1
    tm = min(tm or RESIDENT_TOKEN_TILE, T)
    has_res = res is not None
    n_in = 2 + has_res + (3 if ln_bwd else 0) + len(behind)

    def body(*refs):
        dy_ref, w_ref = refs[:2]
        res_ref = refs[2] if has_res else None
        y = _dot_nt(dy_ref[0], w_ref[0, n_map(0)])
        for n in range(1, nbn):
            y = y + _dot_nt(dy_ref[n], w_ref[0, n_map(n)])
        if has_res:
            y = y + res_scale * res_ref[0].astype(F32)
        if ln_bwd is None:
            refs[n_in][0] = y
        else:
            xh_ref, rs_ref, g_ref = refs[2 + has_res:5 + has_res]
            o_ref, dg_ref, db_ref = refs[n_in:n_in + 3]
            out, dg, db = _ln_bwd_rows(y, xh_ref[0], rs_ref[...], g_ref[...])
            o_ref[0] = out
            _accumulate(pl.program_id(0), (dg_ref, db_ref), (dg, db))

    tok = pl.BlockSpec((1, tm, kw), lambda i: (0, i, 0))
    vec = pl.BlockSpec((1, kw), lambda i: (0, 0))
    in_specs = [pl.BlockSpec((nbn, tm, nw), lambda i: (0, i, 0)),
                pl.BlockSpec(w.shape, lambda i: (0, 0, 0, 0))]
    args = [dy, w]
    if has_res:
        in_specs.append(tok)
        args.append(res)
    out_specs, out_shape = tok, jax.ShapeDtypeStruct((1, T, kw), F32)
    if ln_bwd is not None:
        in_specs += [tok, pl.BlockSpec((tm, 1), lambda i: (i, 0)), vec]
        args += list(ln_bwd)
        out_specs = [tok, vec, vec]
        out_shape = [out_shape, jax.ShapeDtypeStruct((1, kw), F32), jax.ShapeDtypeStruct((1, kw), F32)]
    in_specs += [pl.BlockSpec(memory_space=pl.ANY)] * len(behind)
    args += list(behind)
    return pl.pallas_call(body, name=name, grid=(T // tm,), in_specs=in_specs, out_specs=out_specs,
                          out_shape=out_shape, compiler_params=_params())(*args)


def _mm_tn(a, dy, *, name, tm=None, n_map=_same):
    nbk, T, kw = a.shape
    nbn, _, nw = dy.shape
    tm = min(tm or REDUCE_TOKEN_TILE, T)
    nt = T // tm

    def body(a_ref, dy_ref, o_ref, acc_ref):
        i = pl.program_id(2)
        part = _dot_tn(a_ref[...], dy_ref[...])

        @pl.when(i == 0)
        def _():
            acc_ref[...] = part

        @pl.when(i > 0)
        def _():
            acc_ref[...] += part

        @pl.when(i == nt - 1)
        def _():
            o_ref[...] = acc_ref[...].astype(o_ref.dtype)

    return pl.pallas_call(body, name=name, grid=(nbk, nbn, nt),
                          in_specs=[pl.BlockSpec((None, tm, kw), lambda k, n, i: (k, i, 0)),
                                    pl.BlockSpec((None, tm, nw), lambda k, n, i: (n, i, 0))],
                          out_specs=pl.BlockSpec((None, None, kw, nw), lambda k, n, i: (k, n_map(n), 0, 0)),
                          out_shape=jax.ShapeDtypeStruct((nbk, nbn, kw, nw), GRAD_DTYPE),
                          scratch_shapes=[pltpu.VMEM((kw, nw), F32)],
                          compiler_params=_params())(a, dy)


def _ln_bwd_rows(dh, xh, rstd, g):
    dxh = dh * g
    m1 = jnp.mean(dxh, axis=-1, keepdims=True)
    m2 = jnp.mean(dxh * xh, axis=-1, keepdims=True)
    dy = rstd * (dxh - m1 - xh * m2)
    return dy, jnp.sum(dh * xh, axis=0, keepdims=True), jnp.sum(dh, axis=0, keepdims=True)


def _accumulate(i, refs, parts):
    @pl.when(i == 0)
    def _():
        for r, p in zip(refs, parts):
            r[...] = jnp.zeros_like(r) + p

    @pl.when(i > 0)
    def _():
        for r, p in zip(refs, parts):
            r[...] += p


def _loss_ln_bwd(tgt, xhat, rstd, g, b, *, name, tm=None):
    _, T, D = xhat.shape
    tm = min(tm or TOKEN_TILE, T)

    def body(t_ref, xh_ref, rs_ref, g_ref, b_ref, dy_ref, dg_ref, db_ref, loss_ref):
        i = pl.program_id(0)
        xh = xh_ref[...]
        err = xh * g_ref[...] + b_ref[...] - t_ref[...]
        part = 0.5 * jnp.sum(jnp.mean(err * err, axis=-1, keepdims=True), axis=0, keepdims=True)
        dy, dg, db = _ln_bwd_rows(err / D, xh, rs_ref[...], g_ref[...])
        dy_ref[...] = dy
        _accumulate(i, (dg_ref, db_ref, loss_ref), (dg, db, part))

    tok = pl.BlockSpec((None, tm, D), lambda i: (0, i, 0))
    vec = pl.BlockSpec((1, D), lambda i: (0, 0))
    return pl.pallas_call(body, name=name, grid=(T // tm,),
                          in_specs=[tok, tok, pl.BlockSpec((tm, 1), lambda i: (i, 0)), vec, vec],
                          out_specs=[tok, vec, vec, pl.BlockSpec((SUBLANES, LANES), lambda i: (0, 0))],
                          out_shape=[jax.ShapeDtypeStruct((1, T, D), F32), jax.ShapeDtypeStruct((1, D), F32),
                                     jax.ShapeDtypeStruct((1, D), F32), jax.ShapeDtypeStruct((SUBLANES, LANES), F32)],
                          compiler_params=_params())(tgt, xhat, rstd, g, b)


def _shift_rows(ext, k, n, halo):
    if k == 0:
        return ext[halo:halo + n]
    return pltpu.roll(ext, k, axis=0)[halo:halo + n]


def _conv_rows(ext, cw_ref, n, halo):
    return (cw_ref[0:1, :] * _shift_rows(ext, 2, n, halo) + cw_ref[1:2, :] * _shift_rows(ext, 1, n, halo)
            + cw_ref[2:3, :] * ext[halo:halo + n] + cw_ref[3:4, :])


def _pair_map(n):
    return n // 2 + 4 * (n % 2)


def _ffn_up(hb, w_in, cw, *, name, tm=None):
    _, T, D = hb.shape
    _, nb, _, fb = w_in.shape
    half = nb // 2
    tm = min(tm or TOKEN_TILE, T)

    def body(h_ref, wa_ref, wb_ref, cwa_ref, cwb_ref, u_ref, ab_ref, act_ref, carry):
        @pl.when(pl.program_id(1) == 0)
        def _():
            carry[...] = jnp.zeros_like(carry)

        h = h_ref[...]
        conv = []
        for s, (w_ref, cw_ref) in enumerate(((wa_ref, cwa_ref), (wb_ref, cwb_ref))):
            uf = _dot(h, w_ref[...])
            u_ref[s] = uf.astype(u_ref.dtype)
            ext = jnp.concatenate([carry[s], uf], axis=0)
            c = _conv_rows(ext, cw_ref, tm, SUBLANES)
            ab_ref[s] = c.astype(ab_ref.dtype)
            conv.append(c)
            carry[s] = uf[tm - SUBLANES:tm]
        a, b = conv
        act_ref[...] = (a * _sigmoid(a) * b).astype(act_ref.dtype)

    wspec = lambda off: pl.BlockSpec((None, None, D, fb), lambda p, i: (0, p + off, 0, 0))
    cws = lambda off: pl.BlockSpec((None, SUBLANES, fb), lambda p, i: (p + off, 0, 0))
    return pl.pallas_call(body, name=name, grid=(half, T // tm),
                          in_specs=[pl.BlockSpec((None, tm, D), lambda p, i: (0, i, 0)), wspec(0), wspec(half),
                                    cws(0), cws(half)],
                          out_specs=[pl.BlockSpec((None, 2, tm, fb), lambda p, i: (p, 0, i, 0)),
                                     pl.BlockSpec((None, 2, tm, fb), lambda p, i: (p, 0, i, 0)),
                                     pl.BlockSpec((None, tm, fb), lambda p, i: (p, i, 0))],
                          out_shape=[jax.ShapeDtypeStruct((half, 2, T, fb), MXU),
                                     jax.ShapeDtypeStruct((half, 2, T, fb), MXU),
                                     jax.ShapeDtypeStruct((half, T, fb), MXU)],
                          scratch_shapes=[pltpu.VMEM((2, SUBLANES, fb), F32)],
                          compiler_params=_params())(hb, w_in, w_in, cw, cw)


def _ffn_gate_bwd(dy, u, ab, w_out, cw, *, name, tm=None):
    _, T, D = dy.shape
    half, _, _, fb = u.shape
    tm = min(tm or TOKEN_TILE, T)
    nt = T // tm

    n_full = fb // LANES
    tail = slice(n_full * LANES, fb)

    def body(dy_ref, u_ref, ab_ref, w_ref, cwa_ref, cwb_ref, du_ref, dwo_ref, dcw_ref, carry, acc, gacc):
        i = pl.program_id(1)

        @pl.when(i == 0)
        def _():
            carry[...] = jnp.zeros_like(carry)
            acc[...] = jnp.zeros_like(acc)
            gacc[...] = jnp.zeros_like(gacc)
            dcw_ref[...] = jnp.zeros_like(dcw_ref)

        dyv = dy_ref[...]
        dact = _dot_nt(dyv, w_ref[...])
        a = ab_ref[0].astype(F32)
        b = ab_ref[1].astype(F32)
        sa = _sigmoid(a)
        silu = a * sa
        acc[...] += _dot_tn(silu * b, dyv)
        dcs = (dact * b * (sa * (1.0 + a * (1.0 - sa))), dact * silu)
        m = tm + SUBLANES
        rows = lax.broadcasted_iota(jnp.int32, (SUBLANES, fb), 0)
        for s, cw_ref in enumerate((cwa_ref, cwb_ref)):
            dc = dcs[s]
            nxt = jnp.concatenate([dc, carry[s]], axis=0)
            dc1 = pltpu.roll(nxt, m - 1, axis=0)[:tm]
            dc2 = pltpu.roll(nxt, m - 2, axis=0)[:tm]
            du_ref[s] = (cw_ref[2:3, :] * dc + cw_ref[1:2, :] * dc1 + cw_ref[0:1, :] * dc2).astype(du_ref.dtype)
            carry[s] = dc[0:SUBLANES]
            dcb = [x.astype(MXU) for x in (dc, dc1, dc2)]
            for j in range(n_full):
                blk = slice(j * LANES, (j + 1) * LANES)
                gacc[s, j] += _dot_tn(u_ref[s, :, blk], jnp.concatenate([x[:, blk] for x in dcb], axis=1))
            dcw_ref[s] += jnp.where(rows == 3, jnp.sum(dc, axis=0, keepdims=True), 0.0)
            if fb > n_full * LANES:
                ut = u_ref[s, :, tail].astype(F32)
                gt = [jnp.sum(x[:, tail] * ut, axis=0, keepdims=True) for x in (dc2, dc1, dc)]
                rt = rows[:, tail]
                dcw_ref[s, :, tail] += jnp.where(rt == 0, gt[0], jnp.where(rt == 1, gt[1], jnp.where(rt == 2, gt[2], 0.0)))

        @pl.when(i == nt - 1)
        def _():
            dwo_ref[...] = acc[...].astype(dwo_ref.dtype)
            eye = _tri(LANES, True) & _tri(LANES, False)
            for s in range(2):
                for j in range(n_full):
                    g = gacc[s, j]
                    for tap in range(3):
                        d = jnp.where(eye, g[:, (2 - tap) * LANES:(3 - tap) * LANES], 0.0)
                        dcw_ref[s, tap:tap + 1, j * LANES:(j + 1) * LANES] = jnp.sum(d, axis=0, keepdims=True)

    rev = lambda i: nt - 1 - i
    cws = lambda off: pl.BlockSpec((None, SUBLANES, fb), lambda p, i: (p + off, 0, 0))
    pair = lambda: pl.BlockSpec((None, 2, tm, fb), lambda p, i: (p, 0, rev(i), 0))
    return pl.pallas_call(body, name=name, grid=(half, nt),
                          in_specs=[pl.BlockSpec((None, tm, D), lambda p, i: (0, rev(i), 0)), pair(), pair(),
                                    pl.BlockSpec((None, None, fb, D), lambda p, i: (p, 0, 0, 0)), cws(0), cws(half)],
                          out_specs=[pair(),
                                     pl.BlockSpec((None, None, fb, D), lambda p, i: (p, 0, 0, 0)),
                                     pl.BlockSpec((None, 2, SUBLANES, fb), lambda p, i: (p, 0, 0, 0))],
                          out_shape=[jax.ShapeDtypeStruct((half, 2, T, fb), MXU),
                                     jax.ShapeDtypeStruct((half, 1, fb, D), GRAD_DTYPE),
                                     jax.ShapeDtypeStruct((half, 2, SUBLANES, fb), F32)],
                          scratch_shapes=[pltpu.VMEM((2, SUBLANES, fb), F32), pltpu.VMEM((fb, D), F32),
                                          pltpu.VMEM((2, n_full, LANES, 3 * LANES), F32)],
                          compiler_params=_params())(dy, u, ab, w_out, cw, cw)


def _tri(n, lower):
    r = lax.broadcasted_iota(jnp.int32, (n, n), 0)
    c = lax.broadcasted_iota(jnp.int32, (n, n), 1)
    return (r >= c) if lower else (r <= c)


def _hgrn_gates(zq, zf, lb):
    sq = _sigmoid(zq)
    sf = _sigmoid(zf)
    fg = lb + (1.0 - lb) * sf
    return zq * sq, sq, sf, fg, jnp.log(fg)


def _lb_of(lbl_ref, cols):
    return _sigmoid(lbl_ref[0:1, cols] - lbl_ref[1:2, cols])


def _ones_where(mask):
    return jnp.where(mask, 1.0, 0.0).astype(jnp.bfloat16)


def _hgrn_fwd(z, lbl, gw, *, name):
    _, T, zw = z.shape
    C = min(HG_CHUNK, T)
    nch = T // C
    hpb = zw // HG_DIM

    def body(z_ref, lbl_ref, gw_ref, og_ref, st_ref, s_scr, bc_scr, q_scr, k_scr):
        c = pl.program_id(0)

        @pl.when(c == 0)
        def _():
            s_scr[...] = jnp.zeros_like(s_scr)

        low = _tri(C, True)
        low01 = _ones_where(low)
        gwv = gw_ref[...]
        H = range(HG_HEADS)
        for blk in range(2):
            cols = slice(blk * zw, (blk + 1) * zw)
            qq, _, _, fg, lf = _hgrn_gates(z_ref[blk], z_ref[2 + blk], _lb_of(lbl_ref, cols))
            q_scr[:, cols] = qq
            k_scr[:, cols] = 1.0 - fg
            bc_scr[:, cols] = _exact_dot(low01, lf)
        col = lambda h: slice(h * HG_DIM, (h + 1) * HG_DIM)
        zcol = lambda part, h: (part + h // hpb, slice(None), slice((h % hpb) * HG_DIM, (h % hpb + 1) * HG_DIM))
        b = [bc_scr[:, col(h)] for h in H]
        bm = [bc_scr[C // 2 - 1:C // 2, col(h)] for h in H]
        bl = [bc_scr[C - 1:C, col(h)] for h in H]
        q_ = [q_scr[:, col(h)] for h in H]
        k_ = [k_scr[:, col(h)] for h in H]
        v_ = [z_ref[zcol(4, h)] for h in H]
        qt = [q_[h] * jnp.exp(jnp.minimum(b[h] - bm[h], EXP_CLAMP)) for h in H]
        kt = [k_[h] * jnp.exp(jnp.minimum(bm[h] - b[h], EXP_CLAMP)) for h in H]
        A = [jnp.where(low, _dot_nt(qt[h], kt[h]), 0.0) for h in H]
        for h in H:
            st_ref[h] = s_scr[h]
        o = [_dot_nt(q_[h] * jnp.exp(b[h]), s_scr[h]) + _dot(A[h], v_[h]) for h in H]
        for h in H:
            s_scr[h] = s_scr[h] * jnp.exp(bl[h]) + _dot_tn(v_[h], k_[h] * jnp.exp(bl[h] - b[h]))
        for h in H:
            g_h = z_ref[zcol(6, h)]
            r = lax.rsqrt(jnp.mean(o[h] * o[h], axis=-1, keepdims=True) + RMS_EPS)
            og_ref[:, col(h)] = (o[h] * r * gwv * (g_h * _sigmoid(g_h))).astype(og_ref.dtype)

    return pl.pallas_call(body, name=name, grid=(nch,),
                          in_specs=[pl.BlockSpec((8, C, zw), lambda c: (0, c, 0)),
                                    pl.BlockSpec((2, D_MODEL), lambda c: (0, 0)),
                                    pl.BlockSpec((1, HG_DIM), lambda c: (0, 0))],
                          out_specs=[pl.BlockSpec((None, C, D_MODEL), lambda c: (0, c, 0)),
                                     pl.BlockSpec((None, HG_HEADS, HG_DIM, HG_DIM), lambda c: (c, 0, 0, 0))],
                          out_shape=[jax.ShapeDtypeStruct((1, T, D_MODEL), MXU),
                                     jax.ShapeDtypeStruct((nch, HG_HEADS, HG_DIM, HG_DIM), F32)],
                          scratch_shapes=[pltpu.VMEM((HG_HEADS, HG_DIM, HG_DIM), F32)]
                          + [pltpu.VMEM((C, D_MODEL), F32)] * 3,
                          compiler_params=_params())(z, lbl, gw)


def _hgrn_bwd(z, dog, states, lbl, gw, *, name):
    _, T, zw = z.shape
    C = min(HG_CHUNK, T)
    nch = T // C
    hpb = zw // HG_DIM

    def body(z_ref, dog_ref, st0_ref, st1_ref, lbl_ref, gw_ref, dz_ref, dlb_ref, dgw_ref,
             d_scr, bc_scr, q_scr, sf_scr, fg_scr, x_scr):
        step = pl.program_id(0)

        @pl.when(step == 0)
        def _():
            d_scr[...] = jnp.zeros_like(d_scr)
            dlb_ref[...] = jnp.zeros_like(dlb_ref)
            dgw_ref[...] = jnp.zeros_like(dgw_ref)

        low = _tri(C, True)
        low01 = _ones_where(low)
        up01 = _ones_where(_tri(C, False))
        gwv = gw_ref[...]
        H = range(HG_HEADS)
        for blk in range(2):
            lbb = _lb_of(lbl_ref, slice(blk * zw, (blk + 1) * zw))
            qq, sq, sf, fg, lf = _hgrn_gates(z_ref[blk], z_ref[2 + blk], lbb)
            q_scr[:, blk * zw:(blk + 1) * zw] = qq
            sf_scr[:, blk * zw:(blk + 1) * zw] = sf
            fg_scr[:, blk * zw:(blk + 1) * zw] = fg
            bc_scr[:, blk * zw:(blk + 1) * zw] = _exact_dot(low01, lf)
        col = lambda h: slice(h * HG_DIM, (h + 1) * HG_DIM)
        zcol = lambda part, h: (part + h // hpb, slice(None), slice((h % hpb) * HG_DIM, (h % hpb + 1) * HG_DIM))
        b = [bc_scr[:, col(h)] for h in H]
        bm = [bc_scr[C // 2 - 1:C // 2, col(h)] for h in H]
        bl = [bc_scr[C - 1:C, col(h)] for h in H]
        q_ = [q_scr[:, col(h)] for h in H]
        k_ = [1.0 - fg_scr[:, col(h)] for h in H]
        v_ = [z_ref[zcol(4, h)] for h in H]
        eq = [jnp.exp(jnp.minimum(b[h] - bm[h], EXP_CLAMP)) for h in H]
        ek = [jnp.exp(jnp.minimum(bm[h] - b[h], EXP_CLAMP)) for h in H]
        eb = [jnp.exp(b[h]) for h in H]
        el = [jnp.exp(bl[h] - b[h]) for h in H]
        qt = [q_[h] * eq[h] for h in H]
        kt = [k_[h] * ek[h] for h in H]
        q0 = [q_[h] * eb[h] for h in H]
        kd = [k_[h] * el[h] for h in H]
        A = [jnp.where(low, _dot_nt(qt[h], kt[h]), 0.0) for h in H]
        o = [_dot_nt(q0[h], st0_ref[h]) + _dot(A[h], v_[h]) for h in H]
        do = []
        dgw_acc = jnp.zeros((1, HG_DIM), F32)
        for h in H:
            g_h = z_ref[zcol(6, h)]
            r = lax.rsqrt(jnp.mean(o[h] * o[h], axis=-1, keepdims=True) + RMS_EPS)
            on = o[h] * r
            sg = _sigmoid(g_h)
            dogh = dog_ref[:, col(h)].astype(F32)
            t1 = dogh * on
            dgw_acc = dgw_acc + jnp.sum(t1 * (g_h * sg), axis=0, keepdims=True)
            dz_ref[zcol(6, h)] = (t1 * gwv * (sg * (1.0 + g_h * (1.0 - sg)))).astype(dz_ref.dtype)
            don = dogh * gwv * (g_h * sg)
            do.append(r * (don - on * jnp.mean(don * on, axis=-1, keepdims=True)))
        dgw_ref[...] += dgw_acc
        P = [jnp.where(low, _dot_nt(do[h], v_[h]), 0.0) for h in H]
        dqq = [eb[h] * _dot(do[h], st0_ref[h]) + eq[h] * _dot_hp(P[h], kt[h], ((1,), (0,))) for h in H]
        dkk = [el[h] * _dot(v_[h], d_scr[h]) + ek[h] * _dot_hp(P[h], qt[h], ((0,), (0,))) for h in H]
        for h in H:
            dz_ref[zcol(4, h)] = (_dot_nt(kd[h], d_scr[h]) + _dot_tn(A[h], do[h])).astype(dz_ref.dtype)
            x_scr[:, col(h)] = q_[h] * dqq[h] - k_[h] * dkk[h]
        edge = [jnp.sum(d_scr[h] * st1_ref[h], axis=0, keepdims=True) for h in H]
        for h in H:
            d_scr[h] = d_scr[h] * jnp.exp(bl[h]) + _dot_tn(do[h], q0[h])
        for blk in range(2):
            x_scr[:, blk * zw:(blk + 1) * zw] = _exact_dot(up01, x_scr[:, blk * zw:(blk + 1) * zw])
        dlb = []
        for h in H:
            dfg = (x_scr[:, col(h)] + edge[h]) / fg_scr[:, col(h)] - dkk[h]
            sf_h = sf_scr[:, col(h)]
            lb_h = _lb_of(lbl_ref, col(h))
            zq_h = z_ref[zcol(0, h)]
            sq_h = _sigmoid(zq_h)
            dlb.append(jnp.sum(dfg * (1.0 - sf_h), axis=0, keepdims=True))
            dz_ref[zcol(0, h)] = (dqq[h] * (sq_h * (1.0 + zq_h * (1.0 - sq_h)))).astype(dz_ref.dtype)
            dz_ref[zcol(2, h)] = (dfg * (1.0 - lb_h) * sf_h * (1.0 - sf_h)).astype(dz_ref.dtype)
        dlb_ref[...] += jnp.concatenate(dlb, axis=1)

    rev = lambda s: nch - 1 - s
    return pl.pallas_call(body, name=name, grid=(nch,),
                          in_specs=[pl.BlockSpec((8, C, zw), lambda s: (0, rev(s), 0)),
                                    pl.BlockSpec((None, C, D_MODEL), lambda s: (0, rev(s), 0)),
                                    pl.BlockSpec((None, HG_HEADS, HG_DIM, HG_DIM), lambda s: (rev(s), 0, 0, 0)),
                                    pl.BlockSpec((None, HG_HEADS, HG_DIM, HG_DIM),
                                                 lambda s: (jnp.minimum(rev(s) + 1, nch - 1), 0, 0, 0)),
                                    pl.BlockSpec((2, D_MODEL), lambda s: (0, 0)),
                                    pl.BlockSpec((1, HG_DIM), lambda s: (0, 0))],
                          out_specs=[pl.BlockSpec((8, C, zw), lambda s: (0, rev(s), 0)),
                                     pl.BlockSpec((1, D_MODEL), lambda s: (0, 0)),
                                     pl.BlockSpec((1, HG_DIM), lambda s: (0, 0))],
                          out_shape=[jax.ShapeDtypeStruct((8, T, zw), MXU), jax.ShapeDtypeStruct((1, D_MODEL), F32),
                                     jax.ShapeDtypeStruct((1, HG_DIM), F32)],
                          scratch_shapes=[pltpu.VMEM((HG_HEADS, HG_DIM, HG_DIM), F32)]
                          + [pltpu.VMEM((C, D_MODEL), F32)] * 5,
                          compiler_params=_params())(z, dog, states, states, lbl, gw)


def _bucket_onehot():
    W = SW_WINDOW
    t = np.arange(W)[:, None] + W
    s = np.arange(2 * W)[None, :]
    dist = t - s
    exact = REL_BUCKETS // 2
    d = np.maximum(np.maximum(dist, 0), 1).astype(np.float32)
    log_b = exact + (np.log(d / np.float32(exact)) / np.float32(math.log(REL_MAX_DIST / exact))
                     * np.float32(REL_BUCKETS - exact)).astype(np.int32)
    bucket = np.where(np.maximum(dist, 0) < exact, np.maximum(dist, 0), np.minimum(log_b, REL_BUCKETS - 1))
    valid = (dist >= 0) & (dist < W)
    onehot = (bucket[..., None] == np.arange(REL_BUCKETS)) & valid[..., None]
    return onehot.reshape(W * 2 * W, REL_BUCKETS).astype(np.float32)


def _bias_expand(rel_t, onehot_t, *, name):
    hq, nbk = rel_t.shape
    n = onehot_t.shape[1]

    def body(r_ref, oh_ref, o_ref):
        o_ref[...] = _exact_dot_r(r_ref[...], oh_ref[...])

    return pl.pallas_call(body, name=name, out_shape=jax.ShapeDtypeStruct((hq, n), F32),
                          compiler_params=_params())(rel_t, onehot_t)


def _bias_reduce(dbias, onehot, *, name):
    hq = dbias.shape[0]
    nbk = onehot.shape[1]

    def body(d_ref, oh_ref, o_ref):
        o_ref[...] = _exact_dot_r(d_ref[...], oh_ref[...])

    return pl.pallas_call(body, name=name, out_shape=jax.ShapeDtypeStruct((hq, nbk), F32),
                          compiler_params=_params())(dbias, onehot)


def _swa_mask(j):
    W = SW_WINDOW
    t = lax.broadcasted_iota(jnp.int32, (W, 2 * W), 0) + W
    s = lax.broadcasted_iota(jnp.int32, (W, 2 * W), 1)
    dist = t - s
    band = (dist >= 0) & (dist < W)
    m = band & ((j > 0) | (s >= W))
    return jnp.concatenate([m] * SW_GROUP, axis=0)


def _half_mask(rows, half):
    lane = lax.broadcasted_iota(jnp.int32, (rows, LANES), 1)
    return (lane >= SW_HEAD_DIM) if half else (lane < SW_HEAD_DIM)


def _swa_head(ref, col0, head, to_half):
    slab, half = head // 2, head % 2
    x = ref[:, col0 + slab * LANES:col0 + (slab + 1) * LANES]
    x = jnp.where(_half_mask(x.shape[0], half), x, 0.0)
    return x if half == to_half else pltpu.roll(x, SW_HEAD_DIM, axis=1)


def _swa_stack(ref, g):
    return jnp.concatenate([_swa_head(ref, 0, g * SW_GROUP + r, g % 2) for r in range(SW_GROUP)], axis=0)


def _swa_unstack(ref, x, g):
    W = SW_WINDOW
    for pair in range(SW_GROUP // 2):
        parts = []
        for r in (2 * pair, 2 * pair + 1):
            piece = x[r * W:(r + 1) * W]
            parts.append(piece if r % 2 == g % 2 else pltpu.roll(piece, SW_HEAD_DIM, axis=1))
        slab = (g * SW_GROUP) // 2 + pair
        ref[:, slab * LANES:(slab + 1) * LANES] = (parts[0] + parts[1]).astype(ref.dtype)


def _swa_kv(kp_ref, kc_ref, col0, g):
    return jnp.concatenate([_swa_head(kp_ref, col0, g, g % 2), _swa_head(kc_ref, col0, g, g % 2)], axis=0)


def _lane_pick(tile, h):
    lane = lax.broadcasted_iota(jnp.int32, tile.shape, 1)
    return jnp.sum(jnp.where(lane == h, tile, 0.0), axis=-1, keepdims=True)


def _lane_put(tile, h, col):
    lane = lax.broadcasted_iota(jnp.int32, tile.shape, 1)
    return jnp.where(lane == h, col, tile)


def _swa_rows(vals):
    return jnp.concatenate([jnp.broadcast_to(v, (SW_WINDOW, 1)) for v in vals], axis=0)


def _swa_fwd(q, kv, bias, sinks, *, name):
    _, T, D = q.shape
    W = SW_WINDOW
    nb = T // W
    dh = SW_HEAD_DIM
    kvw = SW_KV_HEADS * dh
    scale = dh ** -0.5

    def body(q_ref, kc_ref, kp_ref, bias_ref, sink_ref, o_ref, lse_ref):
        j = pl.program_id(0)
        mask = _swa_mask(j)
        sk = sink_ref[...]
        lse_tile = jnp.zeros((W, SW_Q_HEADS), F32)
        G = range(SW_KV_HEADS)
        kk = [_swa_kv(kp_ref, kc_ref, 0, g) for g in G]
        vv = [_swa_kv(kp_ref, kc_ref, kvw, g) for g in G]
        qs = [_swa_stack(q_ref, g) for g in G]
        logits = [jnp.where(mask, _dot_nt(qs[g], kk[g]) * scale
                            + bias_ref[g * SW_GROUP:(g + 1) * SW_GROUP].reshape(SW_GROUP * W, 2 * W), NEG_BIG) for g in G]
        sink = [_swa_rows([_lane_pick(sk, g * SW_GROUP + r) for r in range(SW_GROUP)]) for g in G]
        m = [jnp.maximum(jnp.max(logits[g], axis=-1, keepdims=True), sink[g]) for g in G]
        p = [jnp.exp(logits[g] - m[g]) for g in G]
        den = [jnp.sum(p[g], axis=-1, keepdims=True) + jnp.exp(sink[g] - m[g]) for g in G]
        pv = [_dot(p[g], vv[g]) for g in G]
        for g in G:
            _swa_unstack(o_ref, pv[g] / den[g], g)
            lse = m[g] + jnp.log(den[g])
            for r in range(SW_GROUP):
                lse_tile = _lane_put(lse_tile, g * SW_GROUP + r, lse[r * W:(r + 1) * W])
        lse_ref[...] = lse_tile

    return pl.pallas_call(body, name=name, grid=(nb,),
                          in_specs=[pl.BlockSpec((None, W, D), lambda j: (0, j, 0)),
                                    pl.BlockSpec((None, W, 2 * kvw), lambda j: (0, j, 0)),
                                    pl.BlockSpec((None, W, 2 * kvw), lambda j: (0, jnp.maximum(j - 1, 0), 0)),
                                    pl.BlockSpec((SW_Q_HEADS, W, 2 * W), lambda j: (0, 0, 0)),
                                    pl.BlockSpec((1, SW_Q_HEADS), lambda j: (0, 0))],
                          out_specs=[pl.BlockSpec((None, W, D), lambda j: (0, j, 0)),
                                     pl.BlockSpec((W, SW_Q_HEADS), lambda j: (j, 0))],
                          out_shape=[jax.ShapeDtypeStruct((1, T, D), F32), jax.ShapeDtypeStruct((T, SW_Q_HEADS), F32)],
                          compiler_params=_params())(q, kv, kv, bias, sinks)


def _swa_bwd(q, kv, o, lse, do, bias, sinks, *, name):
    _, T, D = q.shape
    W = SW_WINDOW
    nb = T // W
    dh = SW_HEAD_DIM
    kvw = SW_KV_HEADS * dh
    scale = dh ** -0.5
    cl = lambda j: jnp.minimum(j, nb - 1)

    def body(q_ref, kc_ref, kp_ref, o_ref, lse_ref, do_ref, bias_ref, sink_ref,
             dq_ref, dkv_ref, dbias_ref, dsink_ref, carry):
        j = pl.program_id(0)

        @pl.when(j == 0)
        def _():
            carry[...] = jnp.zeros_like(carry)
            dbias_ref[...] = jnp.zeros_like(dbias_ref)
            dsink_ref[...] = jnp.zeros_like(dsink_ref)

        @pl.when(j < nb)
        def _():
            mask = _swa_mask(j)
            sk = sink_ref[...]
            lse_tile = lse_ref[...]
            dsink = jnp.zeros((1, SW_Q_HEADS), F32)
            G = range(SW_KV_HEADS)
            heads = [[g * SW_GROUP + r for r in range(SW_GROUP)] for g in G]
            kk = [_swa_kv(kp_ref, kc_ref, 0, g) for g in G]
            vv = [_swa_kv(kp_ref, kc_ref, kvw, g) for g in G]
            qs = [_swa_stack(q_ref, g) for g in G]
            dos = [_swa_stack(do_ref, g) for g in G]
            lse = [jnp.concatenate([_lane_pick(lse_tile, h) for h in heads[g]], axis=0) for g in G]
            sink = [_swa_rows([_lane_pick(sk, h) for h in heads[g]]) for g in G]
            logits = [jnp.where(mask, _dot_nt(qs[g], kk[g]) * scale
                                + bias_ref[g * SW_GROUP:(g + 1) * SW_GROUP].reshape(SW_GROUP * W, 2 * W), NEG_BIG)
                      for g in G]
            dp = [_dot_nt(dos[g], vv[g]) for g in G]
            p = [jnp.exp(logits[g] - lse[g]) for g in G]
            delta = [jnp.sum(dos[g] * _swa_stack(o_ref, g), axis=-1, keepdims=True) for g in G]
            dl = [p[g] * (dp[g] - delta[g]) for g in G]
            dqs = [_dot(dl[g], kk[g]) * scale for g in G]
            dks = [_dot_tn(dl[g], qs[g]) * scale for g in G]
            dvs = [_dot_tn(p[g], dos[g]) for g in G]
            for g in G:
                _swa_unstack(dq_ref, dqs[g], g)
                dbias_ref[g * SW_GROUP:(g + 1) * SW_GROUP] += dl[g].reshape(SW_GROUP, W, 2 * W)
                sd = jnp.exp(sink[g] - lse[g]) * delta[g]
                for r, h in enumerate(heads[g]):
                    dsink = _lane_put(dsink, h, -jnp.sum(sd[r * W:(r + 1) * W], axis=0, keepdims=True))
            dsink_ref[...] += dsink
            for slab in range(SW_KV_HEADS // 2):
                for col0, parts in ((0, dks), (kvw, dvs)):
                    both = parts[2 * slab] + parts[2 * slab + 1]
                    cols = slice(col0 + slab * LANES, col0 + (slab + 1) * LANES)
                    dkv_ref[:, cols] = (carry[:, cols] + both[:W]).astype(dkv_ref.dtype)
                    carry[:, cols] = both[W:]

        @pl.when(j == nb)
        def _():
            dkv_ref[...] = carry[...].astype(dkv_ref.dtype)

    tok = lambda w: pl.BlockSpec((None, W, w), lambda j: (0, cl(j), 0))
    return pl.pallas_call(body, name=name, grid=(nb + 1,),
                          in_specs=[tok(D), tok(2 * kvw),
                                    pl.BlockSpec((None, W, 2 * kvw), lambda j: (0, jnp.maximum(cl(j) - 1, 0), 0)),
                                    tok(D), pl.BlockSpec((W, SW_Q_HEADS), lambda j: (cl(j), 0)), tok(D),
                                    pl.BlockSpec((SW_Q_HEADS, W, 2 * W), lambda j: (0, 0, 0)),
                                    pl.BlockSpec((1, SW_Q_HEADS), lambda j: (0, 0))],
                          out_specs=[tok(D),
                                     pl.BlockSpec((None, W, 2 * kvw), lambda j: (0, jnp.maximum(j - 1, 0), 0)),
                                     pl.BlockSpec((SW_Q_HEADS, W, 2 * W), lambda j: (0, 0, 0)),
                                     pl.BlockSpec((1, SW_Q_HEADS), lambda j: (0, 0))],
                          out_shape=[jax.ShapeDtypeStruct((1, T, D), MXU), jax.ShapeDtypeStruct((1, T, 2 * kvw), MXU),
                                     jax.ShapeDtypeStruct((SW_Q_HEADS, W, 2 * W), F32),
                                     jax.ShapeDtypeStruct((1, SW_Q_HEADS), F32)],
                          scratch_shapes=[pltpu.VMEM((W, 2 * kvw), F32)],
                          compiler_params=_params())(q, kv, kv, o, lse, do, bias, sinks)


_HBM = pl.BlockSpec(memory_space=pltpu.HBM)
_SEM = pl.BlockSpec(memory_space=pltpu.SEMAPHORE)
_EFFECT = pltpu.SideEffectType.DATAFLOW_SIDE_EFFECTING
N_PEERS = N_DEV - 1


def _peer(k):
    x, y, c = lax.axis_index("x"), lax.axis_index("y"), lax.axis_index("c")
    px = (x + (k >> 2)) % 2
    py = (y + ((k >> 1) & 1)) % 2
    pc = (c + (k & 1)) % 2
    return (px, py, pc), 4 * px + 2 * py + pc


def _my_number():
    return 4 * lax.axis_index("x") + 2 * lax.axis_index("y") + lax.axis_index("c")


def _landing(src, mode):
    me = _my_number()
    own = src if mode == "gather" else lax.dynamic_index_in_dim(src, me, 0, keepdims=False)
    return lax.dynamic_update_index_in_dim(lax.empty((N_DEV,) + own.shape, own.dtype), own, me, 0)


def _copy(src_ref, land_ref, mode, send, recv, j, k, dst_slot):
    peer, pid = _peer(k)
    return pltpu.make_async_remote_copy(
        src_ref=src_ref if mode == "gather" else src_ref.at[pid], dst_ref=land_ref.at[dst_slot(pid)],
        send_sem=send.at[j * N_PEERS + k - 1], recv_sem=recv.at[j * N_PEERS + k - 1],
        device_id=peer, device_id_type=pl.DeviceIdType.MESH)


def _send_start(groups, *, name):
    flat = [t for g in groups for t in g]
    n, ng = len(flat), len(groups)
    srcs = [pltpu.with_memory_space_constraint(s, pltpu.HBM) for s, _ in flat]
    lands = [pltpu.with_memory_space_constraint(_landing(s, m), pltpu.HBM) for s, m in flat]

    def body(*refs):
        src_refs, land_refs = refs[:n], refs[n:2 * n]
        sems = refs[2 * n:2 * n + 2 * ng]
        token = refs[-1]
        me = _my_number()
        a = 0
        for gi, g in enumerate(groups):
            for j, (_, mode) in enumerate(g):
                for k in range(1, N_DEV):
                    _copy(src_refs[a], land_refs[a], mode, sems[2 * gi], sems[2 * gi + 1], j, k, lambda pid: me).start()
                a += 1
        token[...] = jnp.zeros_like(token)

    sem_shapes = []
    for g in groups:
        sem_shapes += [pltpu.SemaphoreType.DMA((len(g) * N_PEERS,))] * 2
    out = pl.pallas_call(
        body, name=name,
        out_shape=tuple(sem_shapes) + tuple(pltpu.HBM(a.shape, a.dtype) for a in srcs + lands)
        + (jax.ShapeDtypeStruct((SUBLANES, LANES), F32),),
        in_specs=[_HBM] * (2 * n), out_specs=[_SEM] * (2 * ng) + [_HBM] * (2 * n) + [pl.BlockSpec(memory_space=pltpu.VMEM)],
        input_output_aliases={i: 2 * ng + i for i in range(2 * n)},
        compiler_params=pltpu.CompilerParams(has_side_effects=_EFFECT))(*srcs, *lands)
    sems, thru, token = out[:2 * ng], out[2 * ng:2 * ng + 2 * n], out[-1]
    handles, a = [], 0
    for gi, g in enumerate(groups):
        m = len(g)
        handles.append((sems[2 * gi], sems[2 * gi + 1], list(thru[a:a + m]), list(thru[n + a:n + a + m]),
                        [mode for _, mode in g]))
        a += m
    return handles, token


def _send_wait(handle, after, *, name):
    send, recv, srcs, lands, modes = handle
    m = len(srcs)

    def body(*refs):
        src_refs, land_refs = refs[:m], refs[m:2 * m]
        send_ref, recv_ref = refs[2 * m], refs[2 * m + 1]
        for j in range(m):
            for k in range(1, N_DEV):
                cp = _copy(src_refs[j], land_refs[j], modes[j], send_ref, recv_ref, j, k, lambda pid: pid)
                cp.wait_send()
                cp.wait_recv()

    out = pl.pallas_call(
        body, name=name, out_shape=tuple(pltpu.HBM(a.shape, a.dtype) for a in srcs + lands),
        in_specs=[_HBM] * (2 * m) + [_SEM, _SEM] + [pl.BlockSpec(memory_space=pl.ANY)] * len(after),
        out_specs=[_HBM] * (2 * m), input_output_aliases={i: i for i in range(2 * m)},
        compiler_params=pltpu.CompilerParams(has_side_effects=_EFFECT))(*srcs, *lands, send, recv, *after)
    return list(out[m:])


def _adam_math(w, g, m, v):
    m = ADAM_B1 * m + (1.0 - ADAM_B1) * g
    v = ADAM_B2 * v + (1.0 - ADAM_B2) * (g * g)
    m_hat = m / (1.0 - ADAM_B1 ** ADAM_STEP)
    v_hat = v / (1.0 - ADAM_B2 ** ADAM_STEP)
    delta = -ADAM_LR * (m_hat / (jnp.sqrt(v_hat) + ADAM_EPS) + ADAM_WD * w)
    return delta, m, v


def _adamw(parts, w, m, v, *, name, layer=None):
    S, R, C = parts.shape
    tr = R
    for cand in (256, 128, 64, 32, 16, 8):
        if R % cand == 0 and S * cand * C * 4 <= 4 * 2 ** 20:
            tr = cand
            break

    def body(p_ref, w_ref, m_ref, v_ref, g_ref, d_ref, nm_ref, nv_ref):
        g = p_ref[0].astype(F32)
        for s in range(1, S):
            g = g + p_ref[s].astype(F32)
        delta, nm, nv = _adam_math(w_ref[...], g, m_ref[...], v_ref[...])
        g_ref[...] = g
        d_ref[...] = delta
        nm_ref[...] = nm
        nv_ref[...] = nv

    if layer is None:
        wspec = pl.BlockSpec((tr, C), lambda i: (i, 0))
    else:
        wspec = pl.BlockSpec((None, tr, C), lambda i: (layer, i, 0))
    ospec = pl.BlockSpec((tr, C), lambda i: (i, 0))
    osh = jax.ShapeDtypeStruct((R, C), F32)
    return pl.pallas_call(body, name=name, grid=(R // tr,),
                          in_specs=[pl.BlockSpec((S, tr, C), lambda i: (0, i, 0)), wspec, wspec, wspec],
                          out_specs=[ospec] * 4, out_shape=[osh] * 4, compiler_params=_params())(parts, w, m, v)


def _sum_parts(parts, *, name):
    S, R, C = parts.shape

    def body(p_ref, o_ref):
        g = p_ref[0]
        for s in range(1, S):
            g = g + p_ref[s]
        o_ref[...] = g

    return pl.pallas_call(body, name=name, out_shape=jax.ShapeDtypeStruct((R, C), F32),
                          compiler_params=_params())(parts)


def _pack_rows(arrays):
    pieces, layout, row = [], [], 0
    for a in arrays:
        flat = a.reshape(-1).astype(F32)
        rows = -(-flat.shape[0] // (SUBLANES * LANES)) * SUBLANES
        flat = jnp.pad(flat, (0, rows * LANES - flat.shape[0]))
        pieces.append(flat.reshape(rows, LANES))
        layout.append((row, rows, a.shape))
        row += rows
    return jnp.concatenate(pieces, axis=0), layout


def _unpack_rows(packed, layout):
    out = []
    for row, rows, shape in layout:
        size = int(np.prod(shape))
        out.append(packed[row:row + rows].reshape(-1)[:size].reshape(shape))
    return out


def _ffn_fwd(h, w_in, w_out, cw, ln_g, ln_b, tag):
    h, hb = h
    u, ab, act = _ffn_up(hb, w_in, cw, name=f"ffn_up_{tag}")
    hnb, xh, rs = _mm_nn(act, w_out, res=h, res_scale=ALPHA, ln=(ln_g, ln_b), name=f"ffn_down_{tag}")
    return hnb, xh, rs, (u, ab)


def _ffn_bwd(dy, hb, u, w_in, w_out, cw, ln_bwd, send, tag):
    du, dw_out, dcw = _ffn_gate_bwd(dy, u[0], u[1], w_out, cw, name=f"ffn_gate_bwd_{tag}")
    du = du.reshape((-1,) + du.shape[2:])
    dw_in = _mm_tn(hb, du, n_map=_pair_map, name=f"ffn_dwin_{tag}")
    handle, token = send(dw_in, dw_out)
    dyp, dg, db = _mm_nt_resident(du, w_in, n_map=_pair_map, res=dy, res_scale=ALPHA, ln_bwd=ln_bwd,
                                  behind=(token,), name=f"ffn_dh_{tag}")
    dcw = dcw.transpose(1, 0, 2, 3).reshape((-1,) + dcw.shape[2:])
    return dyp, dg, db, handle, dcw


def kernel(x, hgrn_w_in, hgrn_lb_logits, hgrn_gnorm_w, hgrn_w_out, swa_w_q, swa_sinks, swa_w_out, shared_w_kv, rel_bias, ffn_w_in, ffn_conv_w, ffn_conv_b, ffn_w_out, ln_mix_g, ln_mix_b, ln_ffn_g, ln_ffn_b, loss_target, m_hgrn_w_in, m_hgrn_lb_logits, m_hgrn_gnorm_w, m_hgrn_w_out, m_swa_w_q, m_swa_sinks, m_swa_w_out, m_shared_w_kv, m_rel_bias, m_ffn_w_in, m_ffn_conv_w, m_ffn_conv_b, m_ffn_w_out, m_ln_mix_g, m_ln_mix_b, m_ln_ffn_g, m_ln_ffn_b, v_hgrn_w_in, v_hgrn_lb_logits, v_hgrn_gnorm_w, v_hgrn_w_out, v_swa_w_q, v_swa_sinks, v_swa_w_out, v_shared_w_kv, v_rel_bias, v_ffn_w_in, v_ffn_conv_w, v_ffn_conv_b, v_ffn_w_out, v_ln_mix_g, v_ln_mix_b, v_ln_ffn_g, v_ln_ffn_b):
    T = x.shape[1]
    D = D_MODEL
    W = SW_WINDOW
    fb = ffn_w_in.shape[2]
    me = 4 * lax.axis_index("x") + 2 * lax.axis_index("y") + lax.axis_index("c")

    small_fwd, small_fwd_layout = _pack_rows([hgrn_lb_logits, ffn_conv_w])
    gat = lambda *ws: [(w_.astype(MXU), "gather") for w_ in ws]
    (wait_a, wait_b, wait_c), _ = _send_start(
        [gat(hgrn_w_in[0]) + [(small_fwd, "gather")],
         gat(hgrn_w_out[0], ffn_w_in[0], ffn_w_out[0]),
         gat(shared_w_kv, swa_w_q[0], swa_w_out[0], ffn_w_in[1], ffn_w_out[1])], name="gather_start")
    xb = x.astype(MXU)
    w_hin, small_all = _send_wait(wait_a, (xb,), name="gather_wait_a")
    w_hin = w_hin[None]
    ffn_rows = 2 * ffn_w_out.shape[1]
    (lb_row, lb_rows, _), (cw_row, cw_rows, _) = small_fwd_layout
    lbl = small_all[:, lb_row:lb_row + 2, :].transpose(1, 0, 2).reshape(2, D)
    conv_w_all = small_all[:, cw_row:cw_row + cw_rows, :].reshape(N_DEV, -1)[:, :DEPTH * 3 * fb]
    conv_w_all = conv_w_all.reshape(N_DEV, DEPTH, 3, fb).transpose(1, 0, 2, 3)
    conv_b_all = ffn_conv_b.reshape(DEPTH, N_DEV, 1, fb)
    no_pad = ((0, 0), (0, 0))
    cw = (jnp.pad(conv_w_all, no_pad + ((0, SUBLANES - 3), (0, 0)))
          + jnp.pad(conv_b_all, no_pad + ((3, SUBLANES - 4), (0, 0))))

    row = lambda a, l: a[l:l + 1]

    z = _mm_nn(xb, w_hin, name="hgrn_in")
    og, states = _hgrn_fwd(z, lbl, hgrn_gnorm_w, name="hgrn_rec")
    w_hout, w_fin0, w_fout0 = _send_wait(wait_b, (og,), name="gather_wait_b")
    w_hout = w_hout.reshape(1, 1, D, D)
    w_fin = [w_fin0[None], None]
    w_fout = [w_fout0.reshape(4, 1, ffn_rows, D), None]
    h1b, xh1, rs1 = _mm_nn(og, w_hout, res=x, res_scale=ALPHA, ln=(row(ln_mix_g, 0), row(ln_mix_b, 0)), name="hgrn_out")
    h1 = (xh1, row(ln_mix_g, 0), row(ln_mix_b, 0))
    h2b, xh2, rs2, u0 = _ffn_fwd((h1, h1b), w_fin[0], w_fout[0], cw[0], row(ln_ffn_g, 0), row(ln_ffn_b, 0), "l0")
    h2 = (xh2, row(ln_ffn_g, 0), row(ln_ffn_b, 0))
    w_kv, w_q, w_o, w_fin1, w_fout1 = _send_wait(wait_c, (h2b,), name="gather_wait_c")
    w_kv = w_kv.reshape(1, 1, D, 2 * SW_KV_HEADS * SW_HEAD_DIM)
    w_q = w_q.reshape(1, 1, D, D)
    w_o = w_o.reshape(1, 1, D, D)
    w_fin[1] = w_fin1[None]
    w_fout[1] = w_fout1.reshape(4, 1, ffn_rows, D)
    kv = _mm_nn(h2b, w_kv, name="swa_kv")
    q = _mm_nn(h2b, w_q, name="swa_q")
    onehot = _bucket_onehot()
    bias = _bias_expand(rel_bias.T, jnp.asarray(onehot.T, jnp.bfloat16), name="swa_bias").reshape(SW_Q_HEADS, W, 2 * W)
    ao, lse = _swa_fwd(q, kv, bias, swa_sinks, name="swa_attn")
    h3b, xh3, rs3 = _mm_nn(ao, w_o, res=h2, res_scale=ALPHA, ln=(row(ln_mix_g, 1), row(ln_mix_b, 1)), name="swa_out")
    h3 = (xh3, row(ln_mix_g, 1), row(ln_mix_b, 1))
    _, xh4, rs4, u1 = _ffn_fwd((h3, h3b), w_fin[1], w_fout[1], cw[1], row(ln_ffn_g, 1), row(ln_ffn_b, 1), "l1")
    dy4, dg_f1, db_f1, loss_tile = _loss_ln_bwd(loss_target, xh4, rs4, row(ln_ffn_g, 1), row(ln_ffn_b, 1),
                                                name="loss_ln_ffn1_bwd")
    sc = lambda *gs: [(g_, "scatter") for g_ in gs]

    def send_ffn(name_):
        def send(dw_in, dw_out):
            (handle,), token = _send_start([sc(dw_in.reshape(N_DEV, D, fb), dw_out.reshape(N_DEV, -1, D))], name=name_)
            return handle, token
        return send

    dy3, dg_m1, db_m1, ex1, dcw1 = _ffn_bwd(dy4, h3b, u1, w_fin[1], w_fout[1], cw[1],
                                            (xh3, rs3, row(ln_mix_g, 1)), send_ffn("grads_start_1"), "l1")
    dw_o = _mm_tn(ao, dy3, name="swa_dwo")
    dao = _mm_nt(dy3, w_o, name="swa_dao")
    dq, dkv, dbias, dsinks = _swa_bwd(q, kv, ao, lse, dao, bias, swa_sinks, name="swa_attn_bwd")
    drel_t = _bias_reduce(dbias.reshape(SW_Q_HEADS, W * 2 * W), jnp.asarray(onehot, jnp.bfloat16), name="swa_dbias")
    dw_q = _mm_tn(h2b, dq, name="swa_dwq")
    dw_kv = _mm_tn(h2b, dkv, name="swa_dwkv")
    dh2 = _mm_nt(dq, w_q, res=dy3, res_scale=ALPHA, name="swa_dh_q")
    (ex2,), tok2 = _send_start([sc(dw_o.reshape(N_DEV, D // N_DEV, D), dw_q.reshape(N_DEV, D // N_DEV, D),
                                   dw_kv.reshape(N_DEV, D // N_DEV, -1))], name="grads_start_2")
    dy2, dg_f0, db_f0 = _mm_nt_resident(dkv, w_kv, res=dh2, ln_bwd=(xh2, rs2, row(ln_ffn_g, 0)), behind=(tok2,),
                                        name="swa_dh_kv")
    dy1, dg_m0, db_m0, ex3, dcw0 = _ffn_bwd(dy2, h1b, u0, w_fin[0], w_fout[0], cw[0],
                                            (xh1, rs1, row(ln_mix_g, 0)), send_ffn("grads_start_3"), "l0")
    dw_hout = _mm_tn(og, dy1, name="hgrn_dwout")
    dog = _mm_nt(dy1, w_hout, name="hgrn_dog")
    dz, dlb, dgw = _hgrn_bwd(z, dog, states, lbl, hgrn_gnorm_w, name="hgrn_rec_bwd")
    dw_hin = _mm_tn(xb, dz, name="hgrn_dwin")

    p0 = _sigmoid(lbl[0:1] - lbl[1:2])
    dl0 = dlb * p0 * (1.0 - p0)
    d_lbl = dl0 * jnp.array([[1.0], [-1.0]], F32)
    dcw = jnp.stack([dcw0, dcw1], axis=0)
    d_conv_w = dcw[:, :, 0:3, :]
    d_conv_b = dcw[:, :, 3, :].reshape(DEPTH, N_DEV * fb)
    first_row = lax.broadcasted_iota(jnp.int32, (DEPTH, D), 0) == 0
    two_rows = lambda a, b: jnp.where(first_row, a, b)
    d_ln_mix_g = two_rows(dg_m0, dg_m1)
    d_ln_mix_b = two_rows(db_m0, db_m1)
    d_ln_ffn_g = two_rows(dg_f0, dg_f1)
    d_ln_ffn_b = two_rows(db_f0, db_f1)
    small_grads, small_layout = _pack_rows([d_lbl, d_conv_w, dgw, dsinks, drel_t.T, d_conv_b, d_ln_mix_g, d_ln_mix_b,
                                            d_ln_ffn_g, d_ln_ffn_b, loss_tile[0:1, 0:1]])

    (ex4,), tok4 = _send_start([sc(dw_hin.reshape(N_DEV, D, -1), dw_hout.reshape(N_DEV, D // N_DEV, D))
                                + [(small_grads, "gather")]], name="grads_start_4")
    dx = _mm_nt_resident(dz, w_hin, res=dy1, res_scale=ALPHA, name="hgrn_dx", behind=(tok4,))
    r_fin1, r_fout1 = _send_wait(ex1, (dx,), name="grads_wait_1")
    r_o, r_q, r_kv = _send_wait(ex2, (dx,), name="grads_wait_2")
    r_fin0, r_fout0 = _send_wait(ex3, (dx,), name="grads_wait_3")
    r_hin, r_hout, r_small = _send_wait(ex4, (dx,), name="grads_wait_4")
    received = [r_hin, r_hout, r_q, r_o, r_kv, r_fin0, r_fin1, r_fout0, r_fout1, r_small]

    outs = {}

    def put(name_, res):
        outs["grad_" + name_], outs["delta_" + name_], outs["new_m_" + name_], outs["new_v_" + name_] = res

    def big_update(name_, parts, w, m, v):
        shp = w.shape
        if w.ndim == 3 and shp[0] == 1:
            r = _adamw(parts, w[0], m[0], v[0], name="adamw_" + name_)
            put(name_, [a.reshape(shp) for a in r])
        else:
            r = _adamw(parts, w, m, v, name="adamw_" + name_)
            put(name_, r)

    big_update("hgrn_w_in", received[0], hgrn_w_in, m_hgrn_w_in, v_hgrn_w_in)
    big_update("hgrn_w_out", received[1], hgrn_w_out, m_hgrn_w_out, v_hgrn_w_out)
    big_update("swa_w_q", received[2], swa_w_q, m_swa_w_q, v_swa_w_q)
    big_update("swa_w_out", received[3], swa_w_out, m_swa_w_out, v_swa_w_out)
    big_update("shared_w_kv", received[4], shared_w_kv, m_shared_w_kv, v_shared_w_kv)
    for name_, idx, w, m, v in (("ffn_w_in", 5, ffn_w_in, m_ffn_w_in, v_ffn_w_in),
                                ("ffn_w_out", 7, ffn_w_out, m_ffn_w_out, v_ffn_w_out)):
        per_layer = [_adamw(received[idx + l], w, m, v, layer=l, name=f"adamw_{name_}_{l}") for l in range(DEPTH)]
        put(name_, [jnp.stack([per_layer[0][i], per_layer[1][i]], axis=0) for i in range(4)])

    small_sum = _sum_parts(received[9], name="sum_small_grads")
    (g_lbl, g_conv_w, g_gw, g_sinks, g_rel, g_conv_b, g_mix_g, g_mix_b, g_ffn_g, g_ffn_b,
     loss) = _unpack_rows(small_sum, small_layout)
    g_lbl_mine = lax.dynamic_slice_in_dim(g_lbl, me * (D // N_DEV), D // N_DEV, axis=1)
    g_conv_w_mine = lax.dynamic_index_in_dim(g_conv_w, me, axis=1, keepdims=False)
    small_names = ["hgrn_lb_logits", "ffn_conv_w", "hgrn_gnorm_w", "swa_sinks", "rel_bias", "ffn_conv_b",
                   "ln_mix_g", "ln_mix_b", "ln_ffn_g", "ln_ffn_b"]
    small_g = [g_lbl_mine, g_conv_w_mine, g_gw, g_sinks, g_rel, g_conv_b, g_mix_g, g_mix_b, g_ffn_g, g_ffn_b]
    small_w = [hgrn_lb_logits, ffn_conv_w, hgrn_gnorm_w, swa_sinks, rel_bias, ffn_conv_b, ln_mix_g, ln_mix_b,
               ln_ffn_g, ln_ffn_b]
    small_m = [m_hgrn_lb_logits, m_ffn_conv_w, m_hgrn_gnorm_w, m_swa_sinks, m_rel_bias, m_ffn_conv_b, m_ln_mix_g,
               m_ln_mix_b, m_ln_ffn_g, m_ln_ffn_b]
    small_v = [v_hgrn_lb_logits, v_ffn_conv_w, v_hgrn_gnorm_w, v_swa_sinks, v_rel_bias, v_ffn_conv_b, v_ln_mix_g,
               v_ln_mix_b, v_ln_ffn_g, v_ln_ffn_b]
    pg, lay = _pack_rows(small_g)
    pw, _ = _pack_rows(small_w)
    pm, _ = _pack_rows(small_m)
    pv, _ = _pack_rows(small_v)
    res = _adamw(pg[None], pw, pm, pv, name="adamw_small")
    unpacked = [_unpack_rows(r, lay) for r in res]
    for i, name_ in enumerate(small_names):
        put(name_, [unpacked[j][i] for j in range(4)])

    order = ["hgrn_w_in", "hgrn_lb_logits", "hgrn_gnorm_w", "hgrn_w_out", "swa_w_q", "swa_sinks", "swa_w_out",
             "shared_w_kv", "rel_bias", "ffn_w_in", "ffn_conv_w", "ffn_conv_b", "ffn_w_out", "ln_mix_g", "ln_mix_b",
             "ln_ffn_g", "ln_ffn_b"]
    result = [loss.reshape(()), dx]
    for kind in ("grad_", "delta_", "new_m_", "new_v_"):
        result += [outs[kind + n] for n in order]
    return tuple(result)
```

```python
import functools
import math

import numpy as np
import jax
import jax.numpy as jnp
from jax import lax
from jax.experimental import pallas as pl
from jax.experimental.pallas import tpu as pltpu

F32 = jnp.float32
MXU = jnp.bfloat16

N_DEV = 8
D_MODEL = 1024
DEPTH = 2
HG_HEADS = 8
HG_DIM = 128
HG_CHUNK = 64
HG_STEP_CHUNKS = 2
SW_Q_HEADS = 16
SW_KV_HEADS = 4
SW_GROUP = 4
SW_HEAD_DIM = 64
SW_WINDOW = 128
REL_BUCKETS = 32
REL_MAX_DIST = 128
FFN_DIM = 2816
ALPHA = (2.0 * DEPTH) ** 0.25
LN_EPS = 1e-5
RMS_EPS = 1e-6
ADAM_LR = 0.001
ADAM_B1 = 0.9
ADAM_B2 = 0.999
ADAM_EPS = 1e-08
ADAM_WD = 0.01
ADAM_STEP = 10
EXP_CLAMP = 80.0
NEG_BIG = -1e30

SUBLANES = 8
LANES = 128
VMEM_LIMIT = 48 * 2 ** 20
TOKEN_TILE = 512
WIDE_TOKEN_TILE = 1024
RESIDENT_TOKEN_TILE = 256
REDUCE_TOKEN_TILE = 2048
GRAD_DTYPE = jnp.bfloat16


def _params(**kw):
    return pltpu.CompilerParams(vmem_limit_bytes=VMEM_LIMIT, **kw)


def _sigmoid(x):
    return 1.0 / (1.0 + jnp.exp(-x))


def _dot(a, b):
    return jnp.dot(a.astype(MXU), b.astype(MXU), preferred_element_type=F32)


def _dot_nt(a, b):
    return lax.dot_general(a.astype(MXU), b.astype(MXU), (((1,), (1,)), ((), ())), preferred_element_type=F32)


def _dot_tn(a, b):
    return lax.dot_general(a.astype(MXU), b.astype(MXU), (((0,), (0,)), ((), ())), preferred_element_type=F32)


def _trunc_bf16(x):
    bits = lax.bitcast_convert_type(x, jnp.int32)
    return lax.bitcast_convert_type(bits & jnp.int32(-65536), F32)


def _split3(x):
    hi = _trunc_bf16(x)
    r = x - hi
    mid = _trunc_bf16(r)
    lo = r - mid
    return hi.astype(jnp.bfloat16), mid.astype(jnp.bfloat16), lo.astype(jnp.bfloat16)


def _dot_hp(a, b, contract):
    def halves(x):
        hi = _trunc_bf16(x)
        return hi.astype(jnp.bfloat16), (x - hi).astype(jnp.bfloat16)

    ah, al = halves(a)
    bh, bl = halves(b)
    d = lambda p, q: lax.dot_general(p, q, (contract, ((), ())), preferred_element_type=F32)
    return d(ah, bh) + d(ah, bl) + d(al, bh)


def _exact_dot(m01, x):
    hi, mid, lo = _split3(x)
    d = lambda p: jnp.dot(m01, p, preferred_element_type=F32)
    return d(hi) + d(mid) + d(lo)


def _exact_dot_r(x, m01):
    hi, mid, lo = _split3(x)
    d = lambda p: jnp.dot(p, m01, preferred_element_type=F32)
    return d(hi) + d(mid) + d(lo)


def _mm_nn(a, w, *, name, res=None, res_scale=1.0, ln=None, out_dtype=F32, tm=None):
    nbk, T, kw = a.shape
    _, nbn, _, nw = w.shape
    tm = min(tm or TOKEN_TILE, T)
    has_res = res is not None
    res_ln = isinstance(res, tuple)
    n_res = (3 if res_ln else 1) if has_res else 0
    assert ln is None or nbn == 1

    def body(*refs):
        refs = list(refs)
        a_ref, w_ref = refs[:2]
        res_refs = refs[2:2 + n_res]
        pos = 2 + n_res
        if ln is not None:
            g_ref, b_ref = refs[pos:pos + 2]
            pos += 2
        o_ref = refs[pos]
        if ln is not None:
            xh_ref, rs_ref = refs[pos + 1:pos + 3]
        for n in range(nbn):
            y = _dot(a_ref[0], w_ref[0, n])
            for k in range(1, nbk):
                y = y + _dot(a_ref[k], w_ref[k, n])
            if res_ln:
                y = y + res_scale * (res_refs[0][n] * res_refs[1][...] + res_refs[2][...])
            elif has_res:
                y = y + res_scale * res_refs[0][n].astype(F32)
            if ln is None:
                o_ref[n] = y.astype(o_ref.dtype)
            else:
                mu = jnp.mean(y, axis=-1, keepdims=True)
                yc = y - mu
                var = jnp.mean(yc * yc, axis=-1, keepdims=True)
                rstd = lax.rsqrt(var + LN_EPS)
                xh = yc * rstd
                xh_ref[n] = xh
                rs_ref[...] = rstd
                o_ref[n] = (xh * g_ref[...] + b_ref[...]).astype(o_ref.dtype)

    vec = pl.BlockSpec((1, nw), lambda i: (0, 0))
    in_specs = [pl.BlockSpec((nbk, tm, kw), lambda i: (0, i, 0)),
                pl.BlockSpec((nbk, nbn, kw, nw), lambda i: (0, 0, 0, 0))]
    args = [a, w]
    if has_res:
        in_specs.append(pl.BlockSpec((nbn, tm, nw), lambda i: (0, i, 0)))
        if res_ln:
            in_specs += [vec, vec]
            args += list(res)
        else:
            args.append(res)
    if ln is not None:
        in_specs += [vec, vec]
        args += list(ln)
    out_spec = pl.BlockSpec((nbn, tm, nw), lambda i: (0, i, 0))
    out_shape = jax.ShapeDtypeStruct((nbn, T, nw), out_dtype)
    if ln is not None:
        out_specs = [out_spec, out_spec, pl.BlockSpec((tm, 1), lambda i: (i, 0))]
        out_shape = [jax.ShapeDtypeStruct((nbn, T, nw), MXU), jax.ShapeDtypeStruct((nbn, T, nw), F32),
                     jax.ShapeDtypeStruct((T, 1), F32)]
    else:
        out_specs = out_spec
    return pl.pallas_call(body, name=name, grid=(T // tm,), in_specs=in_specs, out_specs=out_specs,
                          out_shape=out_shape, compiler_params=_params())(*args)


def _same(n):
    return n


def _mm_nt(dy, w, *, name, res=None, res_scale=1.0, out_dtype=F32, tm=None, n_map=_same, behind=()):
    nbn, T, nw = dy.shape
    nbk, _, kw, _ = w.shape
    tm = min(tm or WIDE_TOKEN_TILE, T)
    has_res = res is not None

    def body(*refs):
        refs = list(refs)
        dy_ref, w_ref = refs[:2]
        pos = 2
        res_ref = None
        if has_res:
            res_ref = refs[pos]
            pos += 1
        pos += len(behind)
        o_ref = refs[pos]
        pos += 1
        acc_ref = refs[pos] if nbn > 1 else None
        n = pl.program_id(2)
        part = _dot_nt(dy_ref[...], w_ref[...])

        def finish(acc):
            y = acc
            if has_res:
                y = y + res_scale * res_ref[...].astype(F32)
            o_ref[...] = y.astype(o_ref.dtype)

        if nbn == 1:
            finish(part)
        else:
            @pl.when(n == 0)
            def _():
                acc_ref[...] = part

            @pl.when(n > 0)
            def _():
                acc_ref[...] += part

            @pl.when(n == nbn - 1)
            def _():
                finish(acc_ref[...])

    in_specs = [pl.BlockSpec((None, tm, nw), lambda i, k, n: (n, i, 0)),
                pl.BlockSpec((None, None, kw, nw), lambda i, k, n: (k, n_map(n), 0, 0))]
    args = [dy, w]
    if has_res:
        in_specs.append(pl.BlockSpec((None, tm, kw), lambda i, k, n: (k, i, 0)))
        args.append(res)
    in_specs += [pl.BlockSpec(memory_space=pl.ANY)] * len(behind)
    args += list(behind)
    scratch = [pltpu.VMEM((tm, kw), F32)] if nbn > 1 else []
    return pl.pallas_call(body, name=name, grid=(T // tm, nbk, nbn), in_specs=in_specs,
                          out_specs=pl.BlockSpec((None, tm, kw), lambda i, k, n: (k, i, 0)),
                          out_shape=jax.ShapeDtypeStruct((nbk, T, kw), out_dtype), scratch_shapes=scratch,
                          compiler_params=_params())(*args)


def _mm_nt_resident(dy, w, *, name, res=None, res_scale=1.0, ln_bwd=None, tm=None, n_map=_same, behind=()):
    nbn, T, nw = dy.shape
    nbk, _, kw, _ = w.shape
    assert nbk == 1
    tm = min(tm or RESIDENT_TOKEN_TILE, T)
    has_res = res is not None
    n_in = 2 + has_res + (3 if ln_bwd else 0) + len(behind)

    def body(*refs):
        dy_ref, w_ref = refs[:2]
        res_ref = refs[2] if has_res else None
        y = _dot_nt(dy_ref[0], w_ref[0, n_map(0)])
        for n in range(1, nbn):
            y = y + _dot_nt(dy_ref[n], w_ref[0, n_map(n)])
        if has_res:
            y = y + res_scale * res_ref[0].astype(F32)
        if ln_bwd is None:
            refs[n_in][0] = y
        else:
            xh_ref, rs_ref, g_ref = refs[2 + has_res:5 + has_res]
            o_ref, dg_ref, db_ref = refs[n_in:n_in + 3]
            out, dg, db = _ln_bwd_rows(y, xh_ref[0], rs_ref[...], g_ref[...])
            o_ref[0] = out
            _accumulate(pl.program_id(0), (dg_ref, db_ref), (dg, db))

    tok = pl.BlockSpec((1, tm, kw), lambda i: (0, i, 0))
    vec = pl.BlockSpec((1, kw), lambda i: (0, 0))
    in_specs = [pl.BlockSpec((nbn, tm, nw), lambda i: (0, i, 0)),
                pl.BlockSpec(w.shape, lambda i: (0, 0, 0, 0))]
    args = [dy, w]
    if has_res:
        in_specs.append(tok)
        args.append(res)
    out_specs, out_shape = tok, jax.ShapeDtypeStruct((1, T, kw), F32)
    if ln_bwd is not None:
        in_specs += [tok, pl.BlockSpec((tm, 1), lambda i: (i, 0)), vec]
        args += list(ln_bwd)
        out_specs = [tok, vec, vec]
        out_shape = [out_shape, jax.ShapeDtypeStruct((1, kw), F32), jax.ShapeDtypeStruct((1, kw), F32)]
    in_specs += [pl.BlockSpec(memory_space=pl.ANY)] * len(behind)
    args += list(behind)
    return pl.pallas_call(body, name=name, grid=(T // tm,), in_specs=in_specs, out_specs=out_specs,
                          out_shape=out_shape, compiler_params=_params())(*args)


def _mm_tn(a, dy, *, name, tm=None, n_map=_same):
    nbk, T, kw = a.shape
    nbn, _, nw = dy.shape
    tm = min(tm or REDUCE_TOKEN_TILE, T)
    nt = T // tm

    def body(a_ref, dy_ref, o_ref, acc_ref):
        i = pl.program_id(2)
        part = _dot_tn(a_ref[...], dy_ref[...])

        @pl.when(i == 0)
        def _():
            acc_ref[...] = part

        @pl.when(i > 0)
        def _():
            acc_ref[...] += part

        @pl.when(i == nt - 1)
        def _():
            o_ref[...] = acc_ref[...].astype(o_ref.dtype)

    return pl.pallas_call(body, name=name, grid=(nbk, nbn, nt),
                          in_specs=[pl.BlockSpec((None, tm, kw), lambda k, n, i: (k, i, 0)),
                                    pl.BlockSpec((None, tm, nw), lambda k, n, i: (n, i, 0))],
                          out_specs=pl.BlockSpec((None, None, kw, nw), lambda k, n, i: (k, n_map(n), 0, 0)),
                          out_shape=jax.ShapeDtypeStruct((nbk, nbn, kw, nw), GRAD_DTYPE),
                          scratch_shapes=[pltpu.VMEM((kw, nw), F32)],
                          compiler_params=_params())(a, dy)


def _ln_bwd_rows(dh, xh, rstd, g):
    dxh = dh * g
    m1 = jnp.mean(dxh, axis=-1, keepdims=True)
    m2 = jnp.mean(dxh * xh, axis=-1, keepdims=True)
    dy = rstd * (dxh - m1 - xh * m2)
    return dy, jnp.sum(dh * xh, axis=0, keepdims=True), jnp.sum(dh, axis=0, keepdims=True)


def _accumulate(i, refs, parts):
    @pl.when(i == 0)
    def _():
        for r, p in zip(refs, parts):
            r[...] = jnp.zeros_like(r) + p

    @pl.when(i > 0)
    def _():
        for r, p in zip(refs, parts):
            r[...] += p


def _loss_ln_bwd(tgt, xhat, rstd, g, b, *, name, tm=None):
    _, T, D = xhat.shape
    tm = min(tm or TOKEN_TILE, T)

    def body(t_ref, xh_ref, rs_ref, g_ref, b_ref, dy_ref, dg_ref, db_ref, loss_ref):
        i = pl.program_id(0)
        xh = xh_ref[...]
        err = xh * g_ref[...] + b_ref[...] - t_ref[...]
        part = 0.5 * jnp.sum(jnp.mean(err * err, axis=-1, keepdims=True), axis=0, keepdims=True)
        dy, dg, db = _ln_bwd_rows(err / D, xh, rs_ref[...], g_ref[...])
        dy_ref[...] = dy
        _accumulate(i, (dg_ref, db_ref, loss_ref), (dg, db, part))

    tok = pl.BlockSpec((None, tm, D), lambda i: (0, i, 0))
    vec = pl.BlockSpec((1, D), lambda i: (0, 0))
    return pl.pallas_call(body, name=name, grid=(T // tm,),
                          in_specs=[tok, tok, pl.BlockSpec((tm, 1), lambda i: (i, 0)), vec, vec],
                          out_specs=[tok, vec, vec, pl.BlockSpec((SUBLANES, LANES), lambda i: (0, 0))],
                          out_shape=[jax.ShapeDtypeStruct((1, T, D), F32), jax.ShapeDtypeStruct((1, D), F32),
                                     jax.ShapeDtypeStruct((1, D), F32), jax.ShapeDtypeStruct((SUBLANES, LANES), F32)],
                          compiler_params=_params())(tgt, xhat, rstd, g, b)


def _shift_rows(ext, k, n, halo):
    if k == 0:
        return ext[halo:halo + n]
    return pltpu.roll(ext, k, axis=0)[halo:halo + n]


def _conv_rows(ext, cw_ref, n, halo):
    return (cw_ref[0:1, :] * _shift_rows(ext, 2, n, halo) + cw_ref[1:2, :] * _shift_rows(ext, 1, n, halo)
            + cw_ref[2:3, :] * ext[halo:halo + n] + cw_ref[3:4, :])


def _pair_map(n):
    return n // 2 + 4 * (n % 2)


def _ffn_up(hb, w_in, cw, *, name, tm=None):
    _, T, D = hb.shape
    _, nb, _, fb = w_in.shape
    half = nb // 2
    tm = min(tm or TOKEN_TILE, T)

    def body(h_ref, wa_ref, wb_ref, cwa_ref, cwb_ref, u_ref, ab_ref, act_ref, carry):
        @pl.when(pl.program_id(1) == 0)
        def _():
            carry[...] = jnp.zeros_like(carry)

        h = h_ref[...]
        conv = []
        for s, (w_ref, cw_ref) in enumerate(((wa_ref, cwa_ref), (wb_ref, cwb_ref))):
            uf = _dot(h, w_ref[...])
            u_ref[s] = uf.astype(u_ref.dtype)
            ext = jnp.concatenate([carry[s], uf], axis=0)
            c = _conv_rows(ext, cw_ref, tm, SUBLANES)
            ab_ref[s] = c.astype(ab_ref.dtype)
            conv.append(c)
            carry[s] = uf[tm - SUBLANES:tm]
        a, b = conv
        act_ref[...] = (a * _sigmoid(a) * b).astype(act_ref.dtype)

    wspec = lambda off: pl.BlockSpec((None, None, D, fb), lambda p, i: (0, p + off, 0, 0))
    cws = lambda off: pl.BlockSpec((None, SUBLANES, fb), lambda p, i: (p + off, 0, 0))
    return pl.pallas_call(body, name=name, grid=(half, T // tm),
                          in_specs=[pl.BlockSpec((None, tm, D), lambda p, i: (0, i, 0)), wspec(0), wspec(half),
                                    cws(0), cws(half)],
                          out_specs=[pl.BlockSpec((None, 2, tm, fb), lambda p, i: (p, 0, i, 0)),
                                     pl.BlockSpec((None, 2, tm, fb), lambda p, i: (p, 0, i, 0)),
                                     pl.BlockSpec((None, tm, fb), lambda p, i: (p, i, 0))],
                          out_shape=[jax.ShapeDtypeStruct((half, 2, T, fb), MXU),
                                     jax.ShapeDtypeStruct((half, 2, T, fb), MXU),
                                     jax.ShapeDtypeStruct((half, T, fb), MXU)],
                          scratch_shapes=[pltpu.VMEM((2, SUBLANES, fb), F32)],
                          compiler_params=_params())(hb, w_in, w_in, cw, cw)


def _ffn_gate_bwd(dy, u, ab, w_out, cw, *, name, tm=None):
    _, T, D = dy.shape
    half, _, _, fb = u.shape
    tm = min(tm or TOKEN_TILE, T)
    nt = T // tm

    n_full = fb // LANES
    tail = slice(n_full * LANES, fb)

    def body(dy_ref, u_ref, ab_ref, w_ref, cwa_ref, cwb_ref, du_ref, dwo_ref, dcw_ref, carry, acc, gacc):
        i = pl.program_id(1)

        @pl.when(i == 0)
        def _():
            carry[...] = jnp.zeros_like(carry)
            acc[...] = jnp.zeros_like(acc)
            gacc[...] = jnp.zeros_like(gacc)
            dcw_ref[...] = jnp.zeros_like(dcw_ref)

        dyv = dy_ref[...]
        dact = _dot_nt(dyv, w_ref[...])
        a = ab_ref[0].astype(F32)
        b = ab_ref[1].astype(F32)
        sa = _sigmoid(a)
        silu = a * sa
        acc[...] += _dot_tn(silu * b, dyv)
        dcs = (dact * b * (sa * (1.0 + a * (1.0 - sa))), dact * silu)
        m = tm + SUBLANES
        rows = lax.broadcasted_iota(jnp.int32, (SUBLANES, fb), 0)
        for s, cw_ref in enumerate((cwa_ref, cwb_ref)):
            dc = dcs[s]
            nxt = jnp.concatenate([dc, carry[s]], axis=0)
            dc1 = pltpu.roll(nxt, m - 1, axis=0)[:tm]
            dc2 = pltpu.roll(nxt, m - 2, axis=0)[:tm]
            du_ref[s] = (cw_ref[2:3, :] * dc + cw_ref[1:2, :] * dc1 + cw_ref[0:1, :] * dc2).astype(du_ref.dtype)
            carry[s] = dc[0:SUBLANES]
            dcb = [x.astype(MXU) for x in (dc, dc1, dc2)]
            for j in range(n_full):
                blk = slice(j * LANES, (j + 1) * LANES)
                gacc[s, j] += _dot_tn(u_ref[s, :, blk], jnp.concatenate([x[:, blk] for x in dcb], axis=1))
            dcw_ref[s] += jnp.where(rows == 3, jnp.sum(dc, axis=0, keepdims=True), 0.0)
            if fb > n_full * LANES:
                ut = u_ref[s, :, tail].astype(F32)
                gt = [jnp.sum(x[:, tail] * ut, axis=0, keepdims=True) for x in (dc2, dc1, dc)]
                rt = rows[:, tail]
                dcw_ref[s, :, tail] += jnp.where(rt == 0, gt[0], jnp.where(rt == 1, gt[1], jnp.where(rt == 2, gt[2], 0.0)))

        @pl.when(i == nt - 1)
        def _():
            dwo_ref[...] = acc[...].astype(dwo_ref.dtype)
            eye = _tri(LANES, True) & _tri(LANES, False)
            for s in range(2):
                for j in range(n_full):
                    g = gacc[s, j]
                    for tap in range(3):
                        d = jnp.where(eye, g[:, (2 - tap) * LANES:(3 - tap) * LANES], 0.0)
                        dcw_ref[s, tap:tap + 1, j * LANES:(j + 1) * LANES] = jnp.sum(d, axis=0, keepdims=True)

    rev = lambda i: nt - 1 - i
    cws = lambda off: pl.BlockSpec((None, SUBLANES, fb), lambda p, i: (p + off, 0, 0))
    pair = lambda: pl.BlockSpec((None, 2, tm, fb), lambda p, i: (p, 0, rev(i), 0))
    return pl.pallas_call(body, name=name, grid=(half, nt),
                          in_specs=[pl.BlockSpec((None, tm, D), lambda p, i: (0, rev(i), 0)), pair(), pair(),
                                    pl.BlockSpec((None, None, fb, D), lambda p, i: (p, 0, 0, 0)), cws(0), cws(half)],
                          out_specs=[pair(),
                                     pl.BlockSpec((None, None, fb, D), lambda p, i: (p, 0, 0, 0)),
                                     pl.BlockSpec((None, 2, SUBLANES, fb), lambda p, i: (p, 0, 0, 0))],
                          out_shape=[jax.ShapeDtypeStruct((half, 2, T, fb), MXU),
                                     jax.ShapeDtypeStruct((half, 1, fb, D), GRAD_DTYPE),
                                     jax.ShapeDtypeStruct((half, 2, SUBLANES, fb), F32)],
                          scratch_shapes=[pltpu.VMEM((2, SUBLANES, fb), F32), pltpu.VMEM((fb, D), F32),
                                          pltpu.VMEM((2, n_full, LANES, 3 * LANES), F32)],
                          compiler_params=_params())(dy, u, ab, w_out, cw, cw)


def _tri(n, lower):
    r = lax.broadcasted_iota(jnp.int32, (n, n), 0)
    c = lax.broadcasted_iota(jnp.int32, (n, n), 1)
    return (r >= c) if lower else (r <= c)


def _hgrn_gates(zq, zf, lb):
    sq = _sigmoid(zq)
    sf = _sigmoid(zf)
    fg = lb + (1.0 - lb) * sf
    return zq * sq, sq, sf, fg, jnp.log(fg)


def _lb_of(lbl_ref, cols):
    return _sigmoid(lbl_ref[0:1, cols] - lbl_ref[1:2, cols])


def _ones_where(mask):
    return jnp.where(mask, 1.0, 0.0).astype(jnp.bfloat16)


def _hgrn_fwd(z, lbl, gw, *, name):
    _, T, zw = z.shape
    C = min(HG_CHUNK, T)
    nch = T // C
    S = HG_STEP_CHUNKS if nch % HG_STEP_CHUNKS == 0 else 1
    hpb = zw // HG_DIM

    def body(z_ref, lbl_ref, gw_ref, og_ref, st_ref, s_scr, bc_scr, q_scr, k_scr):
        c = pl.program_id(0)

        @pl.when(c == 0)
        def _():
            s_scr[...] = jnp.zeros_like(s_scr)

        low = _tri(C, True)
        low01 = _ones_where(low)
        gwv = gw_ref[...]
        H = range(HG_HEADS)
        col = lambda h: slice(h * HG_DIM, (h + 1) * HG_DIM)
        for sub in range(S):
            chunk_body(z_ref, lbl_ref, og_ref, st_ref, s_scr, bc_scr, q_scr, k_scr,
                       sub, slice(sub * C, (sub + 1) * C), low, low01, gwv, H, col)

    def chunk_body(z_ref, lbl_ref, og_ref, st_ref, s_scr, bc_scr, q_scr, k_scr, sub, rows, low, low01, gwv, H, col):
        for blk in range(2):
            cols = slice(blk * zw, (blk + 1) * zw)
            qq, _, _, fg, lf = _hgrn_gates(z_ref[blk, rows], z_ref[2 + blk, rows], _lb_of(lbl_ref, cols))
            q_scr[:, cols] = qq
            k_scr[:, cols] = 1.0 - fg
            bc_scr[:, cols] = _exact_dot(low01, lf)
        zcol = lambda part, h: (part + h // hpb, rows, slice((h % hpb) * HG_DIM, (h % hpb + 1) * HG_DIM))
        b = [bc_scr[:, col(h)] for h in H]
        bm = [bc_scr[C // 2 - 1:C // 2, col(h)] for h in H]
        bl = [bc_scr[C - 1:C, col(h)] for h in H]
        q_ = [q_scr[:, col(h)] for h in H]
        k_ = [k_scr[:, col(h)] for h in H]
        v_ = [z_ref[zcol(4, h)] for h in H]
        qt = [q_[h] * jnp.exp(jnp.minimum(b[h] - bm[h], EXP_CLAMP)) for h in H]
        kt = [k_[h] * jnp.exp(jnp.minimum(bm[h] - b[h], EXP_CLAMP)) for h in H]
        A = [jnp.where(low, _dot_nt(qt[h], kt[h]), 0.0) for h in H]
        for h in H:
            st_ref[sub, h] = s_scr[h]
        o = [_dot_nt(q_[h] * jnp.exp(b[h]), s_scr[h]) + _dot(A[h], v_[h]) for h in H]
        for h in H:
            s_scr[h] = s_scr[h] * jnp.exp(bl[h]) + _dot_tn(v_[h], k_[h] * jnp.exp(bl[h] - b[h]))
        for h in H:
            g_h = z_ref[zcol(6, h)]
            r = lax.rsqrt(jnp.mean(o[h] * o[h], axis=-1, keepdims=True) + RMS_EPS)
            og_ref[rows, col(h)] = (o[h] * r * gwv * (g_h * _sigmoid(g_h))).astype(og_ref.dtype)

    return pl.pallas_call(body, name=name, grid=(nch // S,),
                          in_specs=[pl.BlockSpec((8, S * C, zw), lambda c: (0, c, 0)),
                                    pl.BlockSpec((2, D_MODEL), lambda c: (0, 0)),
                                    pl.BlockSpec((1, HG_DIM), lambda c: (0, 0))],
                          out_specs=[pl.BlockSpec((None, S * C, D_MODEL), lambda c: (0, c, 0)),
                                     pl.BlockSpec((S, HG_HEADS, HG_DIM, HG_DIM), lambda c: (c, 0, 0, 0))],
                          out_shape=[jax.ShapeDtypeStruct((1, T, D_MODEL), MXU),
                                     jax.ShapeDtypeStruct((nch, HG_HEADS, HG_DIM, HG_DIM), F32)],
                          scratch_shapes=[pltpu.VMEM((HG_HEADS, HG_DIM, HG_DIM), F32)]
                          + [pltpu.VMEM((C, D_MODEL), F32)] * 3,
                          compiler_params=_params())(z, lbl, gw)


def _hgrn_bwd(z, dog, states, lbl, gw, *, name):
    _, T, zw = z.shape
    C = min(HG_CHUNK, T)
    nch = T // C
    S = HG_STEP_CHUNKS if nch % HG_STEP_CHUNKS == 0 else 1
    hpb = zw // HG_DIM

    def body(z_ref, dog_ref, st0_ref, st1_ref, lbl_ref, gw_ref, dz_ref, dlb_ref, dgw_ref,
             d_scr, bc_scr, q_scr, sf_scr, fg_scr, x_scr):
        step = pl.program_id(0)

        @pl.when(step == 0)
        def _():
            d_scr[...] = jnp.zeros_like(d_scr)
            dlb_ref[...] = jnp.zeros_like(dlb_ref)
            dgw_ref[...] = jnp.zeros_like(dgw_ref)

        low = _tri(C, True)
        low01 = _ones_where(low)
        up01 = _ones_where(_tri(C, False))
        gwv = gw_ref[...]
        H = range(HG_HEADS)
        col = lambda h: slice(h * HG_DIM, (h + 1) * HG_DIM)
        for sub in reversed(range(S)):
            chunk_body(z_ref, dog_ref, st0_ref, st1_ref, lbl_ref, dz_ref, dlb_ref, dgw_ref,
                       d_scr, bc_scr, q_scr, sf_scr, fg_scr, x_scr,
                       sub, slice(sub * C, (sub + 1) * C), low, low01, up01, gwv, H, col)

    def chunk_body(z_ref, dog_ref, st0_ref, st1_ref, lbl_ref, dz_ref, dlb_ref, dgw_ref,
                   d_scr, bc_scr, q_scr, sf_scr, fg_scr, x_scr, sub, rows, low, low01, up01, gwv, H, col):
        st0 = lambda h: st0_ref[sub, h]
        st1 = (lambda h: st0_ref[sub + 1, h]) if sub + 1 < S else (lambda h: st1_ref[h])
        for blk in range(2):
            lbb = _lb_of(lbl_ref, slice(blk * zw, (blk + 1) * zw))
            qq, sq, sf, fg, lf = _hgrn_gates(z_ref[blk, rows], z_ref[2 + blk, rows], lbb)
            q_scr[:, blk * zw:(blk + 1) * zw] = qq
            sf_scr[:, blk * zw:(blk + 1) * zw] = sf
            fg_scr[:, blk * zw:(blk + 1) * zw] = fg
            bc_scr[:, blk * zw:(blk + 1) * zw] = _exact_dot(low01, lf)
        zcol = lambda part, h: (part + h // hpb, rows, slice((h % hpb) * HG_DIM, (h % hpb + 1) * HG_DIM))
        b = [bc_scr[:, col(h)] for h in H]
        bm = [bc_scr[C // 2 - 1:C // 2, col(h)] for h in H]
        bl = [bc_scr[C - 1:C, col(h)] for h in H]
        q_ = [q_scr[:, col(h)] for h in H]
        k_ = [1.0 - fg_scr[:, col(h)] for h in H]
        v_ = [z_ref[zcol(4, h)] for h in H]
        eq = [jnp.exp(jnp.minimum(b[h] - bm[h], EXP_CLAMP)) for h in H]
        ek = [jnp.exp(jnp.minimum(bm[h] - b[h], EXP_CLAMP)) for h in H]
        eb = [jnp.exp(b[h]) for h in H]
        el = [jnp.exp(bl[h] - b[h]) for h in H]
        qt = [q_[h] * eq[h] for h in H]
        kt = [k_[h] * ek[h] for h in H]
        q0 = [q_[h] * eb[h] for h in H]
        kd = [k_[h] * el[h] for h in H]
        A = [jnp.where(low, _dot_nt(qt[h], kt[h]), 0.0) for h in H]
        o = [_dot_nt(q0[h], st0(h)) + _dot(A[h], v_[h]) for h in H]
        do = []
        dgw_acc = jnp.zeros((1, HG_DIM), F32)
        for h in H:
            g_h = z_ref[zcol(6, h)]
            r = lax.rsqrt(jnp.mean(o[h] * o[h], axis=-1, keepdims=True) + RMS_EPS)
            on = o[h] * r
            sg = _sigmoid(g_h)
            dogh = dog_ref[rows, col(h)].astype(F32)
            t1 = dogh * on
            dgw_acc = dgw_acc + jnp.sum(t1 * (g_h * sg), axis=0, keepdims=True)
            dz_ref[zcol(6, h)] = (t1 * gwv * (sg * (1.0 + g_h * (1.0 - sg)))).astype(dz_ref.dtype)
            don = dogh * gwv * (g_h * sg)
            do.append(r * (don - on * jnp.mean(don * on, axis=-1, keepdims=True)))
        dgw_ref[...] += dgw_acc
        P = [jnp.where(low, _dot_nt(do[h], v_[h]), 0.0) for h in H]
        dqq = [eb[h] * _dot(do[h], st0(h)) + eq[h] * _dot_hp(P[h], kt[h], ((1,), (0,))) for h in H]
        dkk = [el[h] * _dot(v_[h], d_scr[h]) + ek[h] * _dot_hp(P[h], qt[h], ((0,), (0,))) for h in H]
        for h in H:
            dz_ref[zcol(4, h)] = (_dot_nt(kd[h], d_scr[h]) + _dot_tn(A[h], do[h])).astype(dz_ref.dtype)
            x_scr[:, col(h)] = q_[h] * dqq[h] - k_[h] * dkk[h]
        edge = [jnp.sum(d_scr[h] * st1(h), axis=0, keepdims=True) for h in H]
        for h in H:
            d_scr[h] = d_scr[h] * jnp.exp(bl[h]) + _dot_tn(do[h], q0[h])
        for blk in range(2):
            x_scr[:, blk * zw:(blk + 1) * zw] = _exact_dot(up01, x_scr[:, blk * zw:(blk + 1) * zw])
        dlb = []
        for h in H:
            dfg = (x_scr[:, col(h)] + edge[h]) / fg_scr[:, col(h)] - dkk[h]
            sf_h = sf_scr[:, col(h)]
            lb_h = _lb_of(lbl_ref, col(h))
            zq_h = z_ref[zcol(0, h)]
            sq_h = _sigmoid(zq_h)
            dlb.append(jnp.sum(dfg * (1.0 - sf_h), axis=0, keepdims=True))
            dz_ref[zcol(0, h)] = (dqq[h] * (sq_h * (1.0 + zq_h * (1.0 - sq_h)))).astype(dz_ref.dtype)
            dz_ref[zcol(2, h)] = (dfg * (1.0 - lb_h) * sf_h * (1.0 - sf_h)).astype(dz_ref.dtype)
        dlb_ref[...] += jnp.concatenate(dlb, axis=1)

    nst = nch // S
    rev = lambda s: nst - 1 - s
    return pl.pallas_call(body, name=name, grid=(nst,),
                          in_specs=[pl.BlockSpec((8, S * C, zw), lambda s: (0, rev(s), 0)),
                                    pl.BlockSpec((None, S * C, D_MODEL), lambda s: (0, rev(s), 0)),
                                    pl.BlockSpec((S, HG_HEADS, HG_DIM, HG_DIM), lambda s: (rev(s), 0, 0, 0)),
                                    pl.BlockSpec((None, HG_HEADS, HG_DIM, HG_DIM),
                                                 lambda s: (jnp.minimum((rev(s) + 1) * S, nch - 1), 0, 0, 0)),
                                    pl.BlockSpec((2, D_MODEL), lambda s: (0, 0)),
                                    pl.BlockSpec((1, HG_DIM), lambda s: (0, 0))],
                          out_specs=[pl.BlockSpec((8, S * C, zw), lambda s: (0, rev(s), 0)),
                                     pl.BlockSpec((1, D_MODEL), lambda s: (0, 0)),
                                     pl.BlockSpec((1, HG_DIM), lambda s: (0, 0))],
                          out_shape=[jax.ShapeDtypeStruct((8, T, zw), MXU), jax.ShapeDtypeStruct((1, D_MODEL), F32),
                                     jax.ShapeDtypeStruct((1, HG_DIM), F32)],
                          scratch_shapes=[pltpu.VMEM((HG_HEADS, HG_DIM, HG_DIM), F32)]
                          + [pltpu.VMEM((C, D_MODEL), F32)] * 5,
                          compiler_params=_params())(z, dog, states, states, lbl, gw)


def _bucket_onehot():
    W = SW_WINDOW
    t = np.arange(W)[:, None] + W
    s = np.arange(2 * W)[None, :]
    dist = t - s
    exact = REL_BUCKETS // 2
    d = np.maximum(np.maximum(dist, 0), 1).astype(np.float32)
    log_b = exact + (np.log(d / np.float32(exact)) / np.float32(math.log(REL_MAX_DIST / exact))
                     * np.float32(REL_BUCKETS - exact)).astype(np.int32)
    bucket = np.where(np.maximum(dist, 0) < exact, np.maximum(dist, 0), np.minimum(log_b, REL_BUCKETS - 1))
    valid = (dist >= 0) & (dist < W)
    onehot = (bucket[..., None] == np.arange(REL_BUCKETS)) & valid[..., None]
    return onehot.reshape(W * 2 * W, REL_BUCKETS).astype(np.float32)


def _bias_expand(rel_t, onehot_t, *, name):
    hq, nbk = rel_t.shape
    n = onehot_t.shape[1]

    def body(r_ref, oh_ref, o_ref):
        o_ref[...] = _exact_dot_r(r_ref[...], oh_ref[...])

    return pl.pallas_call(body, name=name, out_shape=jax.ShapeDtypeStruct((hq, n), F32),
                          compiler_params=_params())(rel_t, onehot_t)


def _bias_reduce(dbias, onehot, *, name):
    hq = dbias.shape[0]
    nbk = onehot.shape[1]

    def body(d_ref, oh_ref, o_ref):
        o_ref[...] = _exact_dot_r(d_ref[...], oh_ref[...])

    return pl.pallas_call(body, name=name, out_shape=jax.ShapeDtypeStruct((hq, nbk), F32),
                          compiler_params=_params())(dbias, onehot)


def _swa_mask(j):
    W = SW_WINDOW
    t = lax.broadcasted_iota(jnp.int32, (W, 2 * W), 0) + W
    s = lax.broadcasted_iota(jnp.int32, (W, 2 * W), 1)
    dist = t - s
    band = (dist >= 0) & (dist < W)
    m = band & ((j > 0) | (s >= W))
    return jnp.concatenate([m] * SW_GROUP, axis=0)


def _half_mask(rows, half):
    lane = lax.broadcasted_iota(jnp.int32, (rows, LANES), 1)
    return (lane >= SW_HEAD_DIM) if half else (lane < SW_HEAD_DIM)


def _swa_head(ref, col0, head, to_half):
    slab, half = head // 2, head % 2
    x = ref[:, col0 + slab * LANES:col0 + (slab + 1) * LANES]
    x = jnp.where(_half_mask(x.shape[0], half), x, 0.0)
    return x if half == to_half else pltpu.roll(x, SW_HEAD_DIM, axis=1)


def _swa_stack(ref, g):
    return jnp.concatenate([_swa_head(ref, 0, g * SW_GROUP + r, g % 2) for r in range(SW_GROUP)], axis=0)


def _swa_unstack(ref, x, g):
    W = SW_WINDOW
    for pair in range(SW_GROUP // 2):
        parts = []
        for r in (2 * pair, 2 * pair + 1):
            piece = x[r * W:(r + 1) * W]
            parts.append(piece if r % 2 == g % 2 else pltpu.roll(piece, SW_HEAD_DIM, axis=1))
        slab = (g * SW_GROUP) // 2 + pair
        ref[:, slab * LANES:(slab + 1) * LANES] = (parts[0] + parts[1]).astype(ref.dtype)


def _swa_kv(kp_ref, kc_ref, col0, g):
    return jnp.concatenate([_swa_head(kp_ref, col0, g, g % 2), _swa_head(kc_ref, col0, g, g % 2)], axis=0)


def _lane_pick(tile, h):
    lane = lax.broadcasted_iota(jnp.int32, tile.shape, 1)
    return jnp.sum(jnp.where(lane == h, tile, 0.0), axis=-1, keepdims=True)


def _lane_put(tile, h, col):
    lane = lax.broadcasted_iota(jnp.int32, tile.shape, 1)
    return jnp.where(lane == h, col, tile)


def _swa_rows(vals):
    return jnp.concatenate([jnp.broadcast_to(v, (SW_WINDOW, 1)) for v in vals], axis=0)


def _swa_fwd(q, kv, bias, sinks, *, name):
    _, T, D = q.shape
    W = SW_WINDOW
    nb = T // W
    dh = SW_HEAD_DIM
    kvw = SW_KV_HEADS * dh
    scale = dh ** -0.5

    def body(q_ref, kc_ref, kp_ref, bias_ref, sink_ref, o_ref, lse_ref):
        j = pl.program_id(0)
        mask = _swa_mask(j)
        sk = sink_ref[...]
        lse_tile = jnp.zeros((W, SW_Q_HEADS), F32)
        G = range(SW_KV_HEADS)
        kk = [_swa_kv(kp_ref, kc_ref, 0, g) for g in G]
        vv = [_swa_kv(kp_ref, kc_ref, kvw, g) for g in G]
        qs = [_swa_stack(q_ref, g) for g in G]
        logits = [jnp.where(mask, _dot_nt(qs[g], kk[g]) * scale
                            + bias_ref[g * SW_GROUP:(g + 1) * SW_GROUP].reshape(SW_GROUP * W, 2 * W), NEG_BIG) for g in G]
        sink = [_swa_rows([_lane_pick(sk, g * SW_GROUP + r) for r in range(SW_GROUP)]) for g in G]
        m = [jnp.maximum(jnp.max(logits[g], axis=-1, keepdims=True), sink[g]) for g in G]
        p = [jnp.exp(logits[g] - m[g]) for g in G]
        den = [jnp.sum(p[g], axis=-1, keepdims=True) + jnp.exp(sink[g] - m[g]) for g in G]
        pv = [_dot(p[g], vv[g]) for g in G]
        for g in G:
            _swa_unstack(o_ref, pv[g] / den[g], g)
            lse = m[g] + jnp.log(den[g])
            for r in range(SW_GROUP):
                lse_tile = _lane_put(lse_tile, g * SW_GROUP + r, lse[r * W:(r + 1) * W])
        lse_ref[...] = lse_tile

    return pl.pallas_call(body, name=name, grid=(nb,),
                          in_specs=[pl.BlockSpec((None, W, D), lambda j: (0, j, 0)),
                                    pl.BlockSpec((None, W, 2 * kvw), lambda j: (0, j, 0)),
                                    pl.BlockSpec((None, W, 2 * kvw), lambda j: (0, jnp.maximum(j - 1, 0), 0)),
                                    pl.BlockSpec((SW_Q_HEADS, W, 2 * W), lambda j: (0, 0, 0)),
                                    pl.BlockSpec((1, SW_Q_HEADS), lambda j: (0, 0))],
                          out_specs=[pl.BlockSpec((None, W, D), lambda j: (0, j, 0)),
                                     pl.BlockSpec((W, SW_Q_HEADS), lambda j: (j, 0))],
                          out_shape=[jax.ShapeDtypeStruct((1, T, D), F32), jax.ShapeDtypeStruct((T, SW_Q_HEADS), F32)],
                          compiler_params=_params())(q, kv, kv, bias, sinks)


def _swa_bwd(q, kv, o, lse, do, bias, sinks, *, name):
    _, T, D = q.shape
    W = SW_WINDOW
    nb = T // W
    dh = SW_HEAD_DIM
    kvw = SW_KV_HEADS * dh
    scale = dh ** -0.5
    cl = lambda j: jnp.minimum(j, nb - 1)

    def body(q_ref, kc_ref, kp_ref, o_ref, lse_ref, do_ref, bias_ref, sink_ref,
             dq_ref, dkv_ref, dbias_ref, dsink_ref, carry):
        j = pl.program_id(0)

        @pl.when(j == 0)
        def _():
            carry[...] = jnp.zeros_like(carry)
            dbias_ref[...] = jnp.zeros_like(dbias_ref)
            dsink_ref[...] = jnp.zeros_like(dsink_ref)

        @pl.when(j < nb)
        def _():
            mask = _swa_mask(j)
            sk = sink_ref[...]
            lse_tile = lse_ref[...]
            dsink = jnp.zeros((1, SW_Q_HEADS), F32)
            G = range(SW_KV_HEADS)
            heads = [[g * SW_GROUP + r for r in range(SW_GROUP)] for g in G]
            kk = [_swa_kv(kp_ref, kc_ref, 0, g) for g in G]
            vv = [_swa_kv(kp_ref, kc_ref, kvw, g) for g in G]
            qs = [_swa_stack(q_ref, g) for g in G]
            dos = [_swa_stack(do_ref, g) for g in G]
            lse = [jnp.concatenate([_lane_pick(lse_tile, h) for h in heads[g]], axis=0) for g in G]
            sink = [_swa_rows([_lane_pick(sk, h) for h in heads[g]]) for g in G]
            logits = [jnp.where(mask, _dot_nt(qs[g], kk[g]) * scale
                                + bias_ref[g * SW_GROUP:(g + 1) * SW_GROUP].reshape(SW_GROUP * W, 2 * W), NEG_BIG)
                      for g in G]
            dp = [_dot_nt(dos[g], vv[g]) for g in G]
            p = [jnp.exp(logits[g] - lse[g]) for g in G]
            delta = [jnp.sum(dos[g] * _swa_stack(o_ref, g), axis=-1, keepdims=True) for g in G]
            dl = [p[g] * (dp[g] - delta[g]) for g in G]
            dqs = [_dot(dl[g], kk[g]) * scale for g in G]
            dks = [_dot_tn(dl[g], qs[g]) * scale for g in G]
            dvs = [_dot_tn(p[g], dos[g]) for g in G]
            for g in G:
                _swa_unstack(dq_ref, dqs[g], g)
                dbias_ref[g * SW_GROUP:(g + 1) * SW_GROUP] += dl[g].reshape(SW_GROUP, W, 2 * W)
                sd = jnp.exp(sink[g] - lse[g]) * delta[g]
                for r, h in enumerate(heads[g]):
                    dsink = _lane_put(dsink, h, -jnp.sum(sd[r * W:(r + 1) * W], axis=0, keepdims=True))
            dsink_ref[...] += dsink
            for slab in range(SW_KV_HEADS // 2):
                for col0, parts in ((0, dks), (kvw, dvs)):
                    both = parts[2 * slab] + parts[2 * slab + 1]
                    cols = slice(col0 + slab * LANES, col0 + (slab + 1) * LANES)
                    dkv_ref[:, cols] = (carry[:, cols] + both[:W]).astype(dkv_ref.dtype)
                    carry[:, cols] = both[W:]

        @pl.when(j == nb)
        def _():
            dkv_ref[...] = carry[...].astype(dkv_ref.dtype)

    tok = lambda w: pl.BlockSpec((None, W, w), lambda j: (0, cl(j), 0))
    return pl.pallas_call(body, name=name, grid=(nb + 1,),
                          in_specs=[tok(D), tok(2 * kvw),
                                    pl.BlockSpec((None, W, 2 * kvw), lambda j: (0, jnp.maximum(cl(j) - 1, 0), 0)),
                                    tok(D), pl.BlockSpec((W, SW_Q_HEADS), lambda j: (cl(j), 0)), tok(D),
                                    pl.BlockSpec((SW_Q_HEADS, W, 2 * W), lambda j: (0, 0, 0)),
                                    pl.BlockSpec((1, SW_Q_HEADS), lambda j: (0, 0))],
                          out_specs=[tok(D),
                                     pl.BlockSpec((None, W, 2 * kvw), lambda j: (0, jnp.maximum(j - 1, 0), 0)),
                                     pl.BlockSpec((SW_Q_HEADS, W, 2 * W), lambda j: (0, 0, 0)),
                                     pl.BlockSpec((1, SW_Q_HEADS), lambda j: (0, 0))],
                          out_shape=[jax.ShapeDtypeStruct((1, T, D), MXU), jax.ShapeDtypeStruct((1, T, 2 * kvw), MXU),
                                     jax.ShapeDtypeStruct((SW_Q_HEADS, W, 2 * W), F32),
                                     jax.ShapeDtypeStruct((1, SW_Q_HEADS), F32)],
                          scratch_shapes=[pltpu.VMEM((W, 2 * kvw), F32)],
                          compiler_params=_params())(q, kv, kv, o, lse, do, bias, sinks)


_HBM = pl.BlockSpec(memory_space=pltpu.HBM)
_SEM = pl.BlockSpec(memory_space=pltpu.SEMAPHORE)
_EFFECT = pltpu.SideEffectType.DATAFLOW_SIDE_EFFECTING
N_PEERS = N_DEV - 1


def _peer(k):
    x, y, c = lax.axis_index("x"), lax.axis_index("y"), lax.axis_index("c")
    px = (x + (k >> 2)) % 2
    py = (y + ((k >> 1) & 1)) % 2
    pc = (c + (k & 1)) % 2
    return (px, py, pc), 4 * px + 2 * py + pc


def _my_number():
    return 4 * lax.axis_index("x") + 2 * lax.axis_index("y") + lax.axis_index("c")


def _landing(src, mode):
    me = _my_number()
    own = src if mode == "gather" else lax.dynamic_index_in_dim(src, me, 0, keepdims=False)
    return lax.dynamic_update_index_in_dim(lax.empty((N_DEV,) + own.shape, own.dtype), own, me, 0)


def _copy(src_ref, land_ref, mode, send, recv, j, k, dst_slot):
    peer, pid = _peer(k)
    return pltpu.make_async_remote_copy(
        src_ref=src_ref if mode == "gather" else src_ref.at[pid], dst_ref=land_ref.at[dst_slot(pid)],
        send_sem=send.at[j * N_PEERS + k - 1], recv_sem=recv.at[j * N_PEERS + k - 1],
        device_id=peer, device_id_type=pl.DeviceIdType.MESH)


def _send_start(groups, *, name):
    flat = [t for g in groups for t in g]
    n, ng = len(flat), len(groups)
    srcs = [pltpu.with_memory_space_constraint(s, pltpu.HBM) for s, _ in flat]
    lands = [pltpu.with_memory_space_constraint(_landing(s, m), pltpu.HBM) for s, m in flat]

    def body(*refs):
        src_refs, land_refs = refs[:n], refs[n:2 * n]
        sems = refs[2 * n:2 * n + 2 * ng]
        token = refs[-1]
        me = _my_number()
        a = 0
        for gi, g in enumerate(groups):
            for j, (_, mode) in enumerate(g):
                for k in range(1, N_DEV):
                    _copy(src_refs[a], land_refs[a], mode, sems[2 * gi], sems[2 * gi + 1], j, k, lambda pid: me).start()
                a += 1
        token[...] = jnp.zeros_like(token)

    sem_shapes = []
    for g in groups:
        sem_shapes += [pltpu.SemaphoreType.DMA((len(g) * N_PEERS,))] * 2
    out = pl.pallas_call(
        body, name=name,
        out_shape=tuple(sem_shapes) + tuple(pltpu.HBM(a.shape, a.dtype) for a in srcs + lands)
        + (jax.ShapeDtypeStruct((SUBLANES, LANES), F32),),
        in_specs=[_HBM] * (2 * n), out_specs=[_SEM] * (2 * ng) + [_HBM] * (2 * n) + [pl.BlockSpec(memory_space=pltpu.VMEM)],
        input_output_aliases={i: 2 * ng + i for i in range(2 * n)},
        compiler_params=pltpu.CompilerParams(has_side_effects=_EFFECT))(*srcs, *lands)
    sems, thru, token = out[:2 * ng], out[2 * ng:2 * ng + 2 * n], out[-1]
    handles, a = [], 0
    for gi, g in enumerate(groups):
        m = len(g)
        handles.append((sems[2 * gi], sems[2 * gi + 1], list(thru[a:a + m]), list(thru[n + a:n + a + m]),
                        [mode for _, mode in g]))
        a += m
    return handles, token


def _send_wait(handle, after, *, name):
    send, recv, srcs, lands, modes = handle
    m = len(srcs)

    def body(*refs):
        src_refs, land_refs = refs[:m], refs[m:2 * m]
        send_ref, recv_ref = refs[2 * m], refs[2 * m + 1]
        for j in range(m):
            for k in range(1, N_DEV):
                cp = _copy(src_refs[j], land_refs[j], modes[j], send_ref, recv_ref, j, k, lambda pid: pid)
                cp.wait_send()
                cp.wait_recv()

    out = pl.pallas_call(
        body, name=name, out_shape=tuple(pltpu.HBM(a.shape, a.dtype) for a in srcs + lands),
        in_specs=[_HBM] * (2 * m) + [_SEM, _SEM] + [pl.BlockSpec(memory_space=pl.ANY)] * len(after),
        out_specs=[_HBM] * (2 * m), input_output_aliases={i: i for i in range(2 * m)},
        compiler_params=pltpu.CompilerParams(has_side_effects=_EFFECT))(*srcs, *lands, send, recv, *after)
    return list(out[m:])


def _adam_math(w, g, m, v):
    m = ADAM_B1 * m + (1.0 - ADAM_B1) * g
    v = ADAM_B2 * v + (1.0 - ADAM_B2) * (g * g)
    m_hat = m / (1.0 - ADAM_B1 ** ADAM_STEP)
    v_hat = v / (1.0 - ADAM_B2 ** ADAM_STEP)
    delta = -ADAM_LR * (m_hat / (jnp.sqrt(v_hat) + ADAM_EPS) + ADAM_WD * w)
    return delta, m, v


def _adamw(parts, w, m, v, *, name, layer=None):
    S, R, C = parts.shape
    tr = R
    for cand in (256, 128, 64, 32, 16, 8):
        if R % cand == 0 and S * cand * C * 4 <= 4 * 2 ** 20:
            tr = cand
            break

    def body(p_ref, w_ref, m_ref, v_ref, g_ref, d_ref, nm_ref, nv_ref):
        g = p_ref[0].astype(F32)
        for s in range(1, S):
            g = g + p_ref[s].astype(F32)
        delta, nm, nv = _adam_math(w_ref[...], g, m_ref[...], v_ref[...])
        g_ref[...] = g
        d_ref[...] = delta
        nm_ref[...] = nm
        nv_ref[...] = nv

    if layer is None:
        wspec = pl.BlockSpec((tr, C), lambda i: (i, 0))
    else:
        wspec = pl.BlockSpec((None, tr, C), lambda i: (layer, i, 0))
    ospec = pl.BlockSpec((tr, C), lambda i: (i, 0))
    osh = jax.ShapeDtypeStruct((R, C), F32)
    return pl.pallas_call(body, name=name, grid=(R // tr,),
                          in_specs=[pl.BlockSpec((S, tr, C), lambda i: (0, i, 0)), wspec, wspec, wspec],
                          out_specs=[ospec] * 4, out_shape=[osh] * 4, compiler_params=_params())(parts, w, m, v)


def _sum_parts(parts, *, name):
    S, R, C = parts.shape

    def body(p_ref, o_ref):
        g = p_ref[0]
        for s in range(1, S):
            g = g + p_ref[s]
        o_ref[...] = g

    return pl.pallas_call(body, name=name, out_shape=jax.ShapeDtypeStruct((R, C), F32),
                          compiler_params=_params())(parts)


def _pack_rows(arrays):
    pieces, layout, row = [], [], 0
    for a in arrays:
        flat = a.reshape(-1).astype(F32)
        rows = -(-flat.shape[0] // (SUBLANES * LANES)) * SUBLANES
        flat = jnp.pad(flat, (0, rows * LANES - flat.shape[0]))
        pieces.append(flat.reshape(rows, LANES))
        layout.append((row, rows, a.shape))
        row += rows
    return jnp.concatenate(pieces, axis=0), layout


def _unpack_rows(packed, layout):
    out = []
    for row, rows, shape in layout:
        size = int(np.prod(shape))
        out.append(packed[row:row + rows].reshape(-1)[:size].reshape(shape))
    return out


def _ffn_fwd(h, w_in, w_out, cw, ln_g, ln_b, tag):
    h, hb = h
    u, ab, act = _ffn_up(hb, w_in, cw, name=f"ffn_up_{tag}")
    hnb, xh, rs = _mm_nn(act, w_out, res=h, res_scale=ALPHA, ln=(ln_g, ln_b), name=f"ffn_down_{tag}")
    return hnb, xh, rs, (u, ab)


def _ffn_bwd(dy, hb, u, w_in, w_out, cw, ln_bwd, send, tag):
    du, dw_out, dcw = _ffn_gate_bwd(dy, u[0], u[1], w_out, cw, name=f"ffn_gate_bwd_{tag}")
    du = du.reshape((-1,) + du.shape[2:])
    dw_in = _mm_tn(hb, du, n_map=_pair_map, name=f"ffn_dwin_{tag}")
    handle, token = send(dw_in, dw_out)
    dyp, dg, db = _mm_nt_resident(du, w_in, n_map=_pair_map, res=dy, res_scale=ALPHA, ln_bwd=ln_bwd,
                                  behind=(token,), name=f"ffn_dh_{tag}")
    dcw = dcw.transpose(1, 0, 2, 3).reshape((-1,) + dcw.shape[2:])
    return dyp, dg, db, handle, dcw


def kernel(x, hgrn_w_in, hgrn_lb_logits, hgrn_gnorm_w, hgrn_w_out, swa_w_q, swa_sinks, swa_w_out, shared_w_kv, rel_bias, ffn_w_in, ffn_conv_w, ffn_conv_b, ffn_w_out, ln_mix_g, ln_mix_b, ln_ffn_g, ln_ffn_b, loss_target, m_hgrn_w_in, m_hgrn_lb_logits, m_hgrn_gnorm_w, m_hgrn_w_out, m_swa_w_q, m_swa_sinks, m_swa_w_out, m_shared_w_kv, m_rel_bias, m_ffn_w_in, m_ffn_conv_w, m_ffn_conv_b, m_ffn_w_out, m_ln_mix_g, m_ln_mix_b, m_ln_ffn_g, m_ln_ffn_b, v_hgrn_w_in, v_hgrn_lb_logits, v_hgrn_gnorm_w, v_hgrn_w_out, v_swa_w_q, v_swa_sinks, v_swa_w_out, v_shared_w_kv, v_rel_bias, v_ffn_w_in, v_ffn_conv_w, v_ffn_conv_b, v_ffn_w_out, v_ln_mix_g, v_ln_mix_b, v_ln_ffn_g, v_ln_ffn_b):
    T = x.shape[1]
    D = D_MODEL
    W = SW_WINDOW
    fb = ffn_w_in.shape[2]
    me = 4 * lax.axis_index("x") + 2 * lax.axis_index("y") + lax.axis_index("c")

    small_fwd, small_fwd_layout = _pack_rows([hgrn_lb_logits, ffn_conv_w])
    gat = lambda *ws: [(w_.astype(MXU), "gather") for w_ in ws]
    (wait_a, wait_b, wait_c), _ = _send_start(
        [gat(hgrn_w_in[0]) + [(small_fwd, "gather")],
         gat(hgrn_w_out[0], ffn_w_in[0], ffn_w_out[0]),
         gat(shared_w_kv, swa_w_q[0], swa_w_out[0], ffn_w_in[1], ffn_w_out[1])], name="gather_start")
    xb = x.astype(MXU)
    w_hin, small_all = _send_wait(wait_a, (xb,), name="gather_wait_a")
    w_hin = w_hin[None]
    ffn_rows = 2 * ffn_w_out.shape[1]
    (lb_row, lb_rows, _), (cw_row, cw_rows, _) = small_fwd_layout
    lbl = small_all[:, lb_row:lb_row + 2, :].transpose(1, 0, 2).reshape(2, D)
    conv_w_all = small_all[:, cw_row:cw_row + cw_rows, :].reshape(N_DEV, -1)[:, :DEPTH * 3 * fb]
    conv_w_all = conv_w_all.reshape(N_DEV, DEPTH, 3, fb).transpose(1, 0, 2, 3)
    conv_b_all = ffn_conv_b.reshape(DEPTH, N_DEV, 1, fb)
    no_pad = ((0, 0), (0, 0))
    cw = (jnp.pad(conv_w_all, no_pad + ((0, SUBLANES - 3), (0, 0)))
          + jnp.pad(conv_b_all, no_pad + ((3, SUBLANES - 4), (0, 0))))

    row = lambda a, l: a[l:l + 1]

    z = _mm_nn(xb, w_hin, name="hgrn_in")
    og, states = _hgrn_fwd(z, lbl, hgrn_gnorm_w, name="hgrn_rec")
    w_hout, w_fin0, w_fout0 = _send_wait(wait_b, (og,), name="gather_wait_b")
    w_hout = w_hout.reshape(1, 1, D, D)
    w_fin = [w_fin0[None], None]
    w_fout = [w_fout0.reshape(4, 1, ffn_rows, D), None]
    h1b, xh1, rs1 = _mm_nn(og, w_hout, res=x, res_scale=ALPHA, ln=(row(ln_mix_g, 0), row(ln_mix_b, 0)), name="hgrn_out")
    h1 = (xh1, row(ln_mix_g, 0), row(ln_mix_b, 0))
    h2b, xh2, rs2, u0 = _ffn_fwd((h1, h1b), w_fin[0], w_fout[0], cw[0], row(ln_ffn_g, 0), row(ln_ffn_b, 0), "l0")
    h2 = (xh2, row(ln_ffn_g, 0), row(ln_ffn_b, 0))
    w_kv, w_q, w_o, w_fin1, w_fout1 = _send_wait(wait_c, (h2b,), name="gather_wait_c")
    w_kv = w_kv.reshape(1, 1, D, 2 * SW_KV_HEADS * SW_HEAD_DIM)
    w_q = w_q.reshape(1, 1, D, D)
    w_o = w_o.reshape(1, 1, D, D)
    w_fin[1] = w_fin1[None]
    w_fout[1] = w_fout1.reshape(4, 1, ffn_rows, D)
    kv = _mm_nn(h2b, w_kv, name="swa_kv")
    q = _mm_nn(h2b, w_q, name="swa_q")
    onehot = _bucket_onehot()
    bias = _bias_expand(rel_bias.T, jnp.asarray(onehot.T, jnp.bfloat16), name="swa_bias").reshape(SW_Q_HEADS, W, 2 * W)
    ao, lse = _swa_fwd(q, kv, bias, swa_sinks, name="swa_attn")
    h3b, xh3, rs3 = _mm_nn(ao, w_o, res=h2, res_scale=ALPHA, ln=(row(ln_mix_g, 1), row(ln_mix_b, 1)), name="swa_out")
    h3 = (xh3, row(ln_mix_g, 1), row(ln_mix_b, 1))
    _, xh4, rs4, u1 = _ffn_fwd((h3, h3b), w_fin[1], w_fout[1], cw[1], row(ln_ffn_g, 1), row(ln_ffn_b, 1), "l1")
    dy4, dg_f1, db_f1, loss_tile = _loss_ln_bwd(loss_target, xh4, rs4, row(ln_ffn_g, 1), row(ln_ffn_b, 1),
                                                name="loss_ln_ffn1_bwd")
    sc = lambda *gs: [(g_, "scatter") for g_ in gs]

    def send_ffn(name_):
        def send(dw_in, dw_out):
            (handle,), token = _send_start([sc(dw_in.reshape(N_DEV, D, fb), dw_out.reshape(N_DEV, -1, D))], name=name_)
            return handle, token
        return send

    dy3, dg_m1, db_m1, ex1, dcw1 = _ffn_bwd(dy4, h3b, u1, w_fin[1], w_fout[1], cw[1],
                                            (xh3, rs3, row(ln_mix_g, 1)), send_ffn("grads_start_1"), "l1")
    dw_o = _mm_tn(ao, dy3, name="swa_dwo")
    dao = _mm_nt(dy3, w_o, name="swa_dao")
    dq, dkv, dbias, dsinks = _swa_bwd(q, kv, ao, lse, dao, bias, swa_sinks, name="swa_attn_bwd")
    drel_t = _bias_reduce(dbias.reshape(SW_Q_HEADS, W * 2 * W), jnp.asarray(onehot, jnp.bfloat16), name="swa_dbias")
    dw_q = _mm_tn(h2b, dq, name="swa_dwq")
    dw_kv = _mm_tn(h2b, dkv, name="swa_dwkv")
    dh2 = _mm_nt(dq, w_q, res=dy3, res_scale=ALPHA, name="swa_dh_q")
    (ex2,), tok2 = _send_start([sc(dw_o.reshape(N_DEV, D // N_DEV, D), dw_q.reshape(N_DEV, D // N_DEV, D),
                                   dw_kv.reshape(N_DEV, D // N_DEV, -1))], name="grads_start_2")
    dy2, dg_f0, db_f0 = _mm_nt_resident(dkv, w_kv, res=dh2, ln_bwd=(xh2, rs2, row(ln_ffn_g, 0)), behind=(tok2,),
                                        name="swa_dh_kv")
    dy1, dg_m0, db_m0, ex3, dcw0 = _ffn_bwd(dy2, h1b, u0, w_fin[0], w_fout[0], cw[0],
                                            (xh1, rs1, row(ln_mix_g, 0)), send_ffn("grads_start_3"), "l0")
    dw_hout = _mm_tn(og, dy1, name="hgrn_dwout")
    dog = _mm_nt(dy1, w_hout, name="hgrn_dog")
    dz, dlb, dgw = _hgrn_bwd(z, dog, states, lbl, hgrn_gnorm_w, name="hgrn_rec_bwd")
    dw_hin = _mm_tn(xb, dz, name="hgrn_dwin")

    p0 = _sigmoid(lbl[0:1] - lbl[1:2])
    dl0 = dlb * p0 * (1.0 - p0)
    d_lbl = dl0 * jnp.array([[1.0], [-1.0]], F32)
    dcw = jnp.stack([dcw0, dcw1], axis=0)
    d_conv_w = dcw[:, :, 0:3, :]
    d_conv_b = dcw[:, :, 3, :].reshape(DEPTH, N_DEV * fb)
    first_row = lax.broadcasted_iota(jnp.int32, (DEPTH, D), 0) == 0
    two_rows = lambda a, b: jnp.where(first_row, a, b)
    d_ln_mix_g = two_rows(dg_m0, dg_m1)
    d_ln_mix_b = two_rows(db_m0, db_m1)
    d_ln_ffn_g = two_rows(dg_f0, dg_f1)
    d_ln_ffn_b = two_rows(db_f0, db_f1)
    small_grads, small_layout = _pack_rows([d_lbl, d_conv_w, dgw, dsinks, drel_t.T, d_conv_b, d_ln_mix_g, d_ln_mix_b,
                                            d_ln_ffn_g, d_ln_ffn_b, loss_tile[0:1, 0:1]])

    (ex4,), tok4 = _send_start([sc(dw_hin.reshape(N_DEV, D, -1), dw_hout.reshape(N_DEV, D // N_DEV, D))
                                + [(small_grads, "gather")]], name="grads_start_4")
    dx = _mm_nt_resident(dz, w_hin, res=dy1, res_scale=ALPHA, name="hgrn_dx", behind=(tok4,))
    r_fin1, r_fout1 = _send_wait(ex1, (dx,), name="grads_wait_1")
    r_o, r_q, r_kv = _send_wait(ex2, (dx,), name="grads_wait_2")
    r_fin0, r_fout0 = _send_wait(ex3, (dx,), name="grads_wait_3")
    r_hin, r_hout, r_small = _send_wait(ex4, (dx,), name="grads_wait_4")
    received = [r_hin, r_hout, r_q, r_o, r_kv, r_fin0, r_fin1, r_fout0, r_fout1, r_small]

    outs = {}

    def put(name_, res):
        outs["grad_" + name_], outs["delta_" + name_], outs["new_m_" + name_], outs["new_v_" + name_] = res

    def big_update(name_, parts, w, m, v):
        shp = w.shape
        if w.ndim == 3 and shp[0] == 1:
            r = _adamw(parts, w[0], m[0], v[0], name="adamw_" + name_)
            put(name_, [a.reshape(shp) for a in r])
        else:
            r = _adamw(parts, w, m, v, name="adamw_" + name_)
            put(name_, r)

    big_update("hgrn_w_in", received[0], hgrn_w_in, m_hgrn_w_in, v_hgrn_w_in)
    big_update("hgrn_w_out", received[1], hgrn_w_out, m_hgrn_w_out, v_hgrn_w_out)
    big_update("swa_w_q", received[2], swa_w_q, m_swa_w_q, v_swa_w_q)
    big_update("swa_w_out", received[3], swa_w_out, m_swa_w_out, v_swa_w_out)
    big_update("shared_w_kv", received[4], shared_w_kv, m_shared_w_kv, v_shared_w_kv)
    for name_, idx, w, m, v in (("ffn_w_in", 5, ffn_w_in, m_ffn_w_in, v_ffn_w_in),
                                ("ffn_w_out", 7, ffn_w_out, m_ffn_w_out, v_ffn_w_out)):
        per_layer = [_adamw(received[idx + l], w, m, v, layer=l, name=f"adamw_{name_}_{l}") for l in range(DEPTH)]
        put(name_, [jnp.stack([per_layer[0][i], per_layer[1][i]], axis=0) for i in range(4)])

    small_sum = _sum_parts(received[9], name="sum_small_grads")
    (g_lbl, g_conv_w, g_gw, g_sinks, g_rel, g_conv_b, g_mix_g, g_mix_b, g_ffn_g, g_ffn_b,
     loss) = _unpack_rows(small_sum, small_layout)
    g_lbl_mine = lax.dynamic_slice_in_dim(g_lbl, me * (D // N_DEV), D // N_DEV, axis=1)
    g_conv_w_mine = lax.dynamic_index_in_dim(g_conv_w, me, axis=1, keepdims=False)
    small_names = ["hgrn_lb_logits", "ffn_conv_w", "hgrn_gnorm_w", "swa_sinks", "rel_bias", "ffn_conv_b",
                   "ln_mix_g", "ln_mix_b", "ln_ffn_g", "ln_ffn_b"]
    small_g = [g_lbl_mine, g_conv_w_mine, g_gw, g_sinks, g_rel, g_conv_b, g_mix_g, g_mix_b, g_ffn_g, g_ffn_b]
    small_w = [hgrn_lb_logits, ffn_conv_w, hgrn_gnorm_w, swa_sinks, rel_bias, ffn_conv_b, ln_mix_g, ln_mix_b,
               ln_ffn_g, ln_ffn_b]
    small_m = [m_hgrn_lb_logits, m_ffn_conv_w, m_hgrn_gnorm_w, m_swa_sinks, m_rel_bias, m_ffn_conv_b, m_ln_mix_g,
               m_ln_mix_b, m_ln_ffn_g, m_ln_ffn_b]
    small_v = [v_hgrn_lb_logits, v_ffn_conv_w, v_hgrn_gnorm_w, v_swa_sinks, v_rel_bias, v_ffn_conv_b, v_ln_mix_g,
               v_ln_mix_b, v_ln_ffn_g, v_ln_ffn_b]
    pg, lay = _pack_rows(small_g)
    pw, _ = _pack_rows(small_w)
    pm, _ = _pack_rows(small_m)
    pv, _ = _pack_rows(small_v)
    res = _adamw(pg[None], pw, pm, pv, name="adamw_small")
    unpacked = [_unpack_rows(r, lay) for r in res]
    for i, name_ in enumerate(small_names):
        put(name_, [unpacked[j][i] for j in range(4)])

    order = ["hgrn_w_in", "hgrn_lb_logits", "hgrn_gnorm_w", "hgrn_w_out", "swa_w_q", "swa_sinks", "swa_w_out",
             "shared_w_kv", "rel_bias", "ffn_w_in", "ffn_conv_w", "ffn_conv_b", "ffn_w_out", "ln_mix_g", "ln_mix_b",
             "ln_ffn_g", "ln_ffn_b"]
    result = [loss.reshape(()), dx]
    for kind in ("grad_", "delta_", "new_m_", "new_v_"):
        result += [outs[kind + n] for n in order]
    return tuple(result)
```

```python
import functools
import math

import numpy as np
import jax
import jax.numpy as jnp
from jax import lax
from jax.experimental import pallas as pl
from jax.experimental.pallas import tpu as pltpu

F32 = jnp.float32
MXU = jnp.bfloat16

N_DEV = 8
D_MODEL = 1024
DEPTH = 2
HG_HEADS = 8
HG_DIM = 128
HG_CHUNK = 64
HG_STEP_CHUNKS = 4
SW_Q_HEADS = 16
SW_KV_HEADS = 4
SW_GROUP = 4
SW_HEAD_DIM = 64
SW_WINDOW = 128
SW_STEP_BLOCKS = 2
REL_BUCKETS = 32
REL_MAX_DIST = 128
FFN_DIM = 2816
ALPHA = (2.0 * DEPTH) ** 0.25
LN_EPS = 1e-5
RMS_EPS = 1e-6
ADAM_LR = 0.001
ADAM_B1 = 0.9
ADAM_B2 = 0.999
ADAM_EPS = 1e-08
ADAM_WD = 0.01
ADAM_STEP = 10
EXP_CLAMP = 80.0
NEG_BIG = -1e30

SUBLANES = 8
LANES = 128
VMEM_LIMIT = 48 * 2 ** 20
TOKEN_TILE = 512
WIDE_TOKEN_TILE = 1024
RESIDENT_TOKEN_TILE = 256
REDUCE_TOKEN_TILE = 2048
GRAD_DTYPE = jnp.bfloat16


def _params(**kw):
    return pltpu.CompilerParams(vmem_limit_bytes=VMEM_LIMIT, **kw)


def _sigmoid(x):
    return 1.0 / (1.0 + jnp.exp(-x))


def _dot(a, b):
    return jnp.dot(a.astype(MXU), b.astype(MXU), preferred_element_type=F32)


def _dot_nt(a, b):
    return lax.dot_general(a.astype(MXU), b.astype(MXU), (((1,), (1,)), ((), ())), preferred_element_type=F32)


def _dot_tn(a, b):
    return lax.dot_general(a.astype(MXU), b.astype(MXU), (((0,), (0,)), ((), ())), preferred_element_type=F32)


def _trunc_bf16(x):
    bits = lax.bitcast_convert_type(x, jnp.int32)
    return lax.bitcast_convert_type(bits & jnp.int32(-65536), F32)


def _split3(x):
    hi = _trunc_bf16(x)
    r = x - hi
    mid = _trunc_bf16(r)
    lo = r - mid
    return hi.astype(jnp.bfloat16), mid.astype(jnp.bfloat16), lo.astype(jnp.bfloat16)


def _dot_hp(a, b, contract):
    def halves(x):
        hi = _trunc_bf16(x)
        return hi.astype(jnp.bfloat16), (x - hi).astype(jnp.bfloat16)

    ah, al = halves(a)
    bh, bl = halves(b)
    d = lambda p, q: lax.dot_general(p, q, (contract, ((), ())), preferred_element_type=F32)
    return d(ah, bh) + d(ah, bl) + d(al, bh)


def _exact_dot(m01, x):
    hi, mid, lo = _split3(x)
    d = lambda p: jnp.dot(m01, p, preferred_element_type=F32)
    return d(hi) + d(mid) + d(lo)


def _exact_dot_r(x, m01):
    hi, mid, lo = _split3(x)
    d = lambda p: jnp.dot(p, m01, preferred_element_type=F32)
    return d(hi) + d(mid) + d(lo)


def _mm_nn(a, w, *, name, res=None, res_scale=1.0, ln=None, out_dtype=F32, tm=None):
    nbk, T, kw = a.shape
    _, nbn, _, nw = w.shape
    tm = min(tm or TOKEN_TILE, T)
    has_res = res is not None
    res_ln = isinstance(res, tuple)
    n_res = (3 if res_ln else 1) if has_res else 0
    assert ln is None or nbn == 1

    def body(*refs):
        refs = list(refs)
        a_ref, w_ref = refs[:2]
        res_refs = refs[2:2 + n_res]
        pos = 2 + n_res
        if ln is not None:
            g_ref, b_ref = refs[pos:pos + 2]
            pos += 2
        o_ref = refs[pos]
        if ln is not None:
            xh_ref, rs_ref = refs[pos + 1:pos + 3]
        for n in range(nbn):
            y = _dot(a_ref[0], w_ref[0, n])
            for k in range(1, nbk):
                y = y + _dot(a_ref[k], w_ref[k, n])
            if res_ln:
                y = y + res_scale * (res_refs[0][n] * res_refs[1][...] + res_refs[2][...])
            elif has_res:
                y = y + res_scale * res_refs[0][n].astype(F32)
            if ln is None:
                o_ref[n] = y.astype(o_ref.dtype)
            else:
                mu = jnp.mean(y, axis=-1, keepdims=True)
                yc = y - mu
                var = jnp.mean(yc * yc, axis=-1, keepdims=True)
                rstd = lax.rsqrt(var + LN_EPS)
                xh = yc * rstd
                xh_ref[n] = xh
                rs_ref[...] = rstd
                o_ref[n] = (xh * g_ref[...] + b_ref[...]).astype(o_ref.dtype)

    vec = pl.BlockSpec((1, nw), lambda i: (0, 0))
    in_specs = [pl.BlockSpec((nbk, tm, kw), lambda i: (0, i, 0)),
                pl.BlockSpec((nbk, nbn, kw, nw), lambda i: (0, 0, 0, 0))]
    args = [a, w]
    if has_res:
        in_specs.append(pl.BlockSpec((nbn, tm, nw), lambda i: (0, i, 0)))
        if res_ln:
            in_specs += [vec, vec]
            args += list(res)
        else:
            args.append(res)
    if ln is not None:
        in_specs += [vec, vec]
        args += list(ln)
    out_spec = pl.BlockSpec((nbn, tm, nw), lambda i: (0, i, 0))
    out_shape = jax.ShapeDtypeStruct((nbn, T, nw), out_dtype)
    if ln is not None:
        out_specs = [out_spec, out_spec, pl.BlockSpec((tm, 1), lambda i: (i, 0))]
        out_shape = [jax.ShapeDtypeStruct((nbn, T, nw), MXU), jax.ShapeDtypeStruct((nbn, T, nw), F32),
                     jax.ShapeDtypeStruct((T, 1), F32)]
    else:
        out_specs = out_spec
    return pl.pallas_call(body, name=name, grid=(T // tm,), in_specs=in_specs, out_specs=out_specs,
                          out_shape=out_shape, compiler_params=_params())(*args)


def _same(n):
    return n


def _mm_nt(dy, w, *, name, res=None, res_scale=1.0, out_dtype=F32, tm=None, n_map=_same, behind=()):
    nbn, T, nw = dy.shape
    nbk, _, kw, _ = w.shape
    tm = min(tm or WIDE_TOKEN_TILE, T)
    has_res = res is not None

    def body(*refs):
        refs = list(refs)
        dy_ref, w_ref = refs[:2]
        pos = 2
        res_ref = None
        if has_res:
            res_ref = refs[pos]
            pos += 1
        pos += len(behind)
        o_ref = refs[pos]
        pos += 1
        acc_ref = refs[pos] if nbn > 1 else None
        n = pl.program_id(2)
        part = _dot_nt(dy_ref[...], w_ref[...])

        def finish(acc):
            y = acc
            if has_res:
                y = y + res_scale * res_ref[...].astype(F32)
            o_ref[...] = y.astype(o_ref.dtype)

        if nbn == 1:
            finish(part)
        else:
            @pl.when(n == 0)
            def _():
                acc_ref[...] = part

            @pl.when(n > 0)
            def _():
                acc_ref[...] += part

            @pl.when(n == nbn - 1)
            def _():
                finish(acc_ref[...])

    in_specs = [pl.BlockSpec((None, tm, nw), lambda i, k, n: (n, i, 0)),
                pl.BlockSpec((None, None, kw, nw), lambda i, k, n: (k, n_map(n), 0, 0))]
    args = [dy, w]
    if has_res:
        in_specs.append(pl.BlockSpec((None, tm, kw), lambda i, k, n: (k, i, 0)))
        args.append(res)
    in_specs += [pl.BlockSpec(memory_space=pl.ANY)] * len(behind)
    args += list(behind)
    scratch = [pltpu.VMEM((tm, kw), F32)] if nbn > 1 else []
    return pl.pallas_call(body, name=name, grid=(T // tm, nbk, nbn), in_specs=in_specs,
                          out_specs=pl.BlockSpec((None, tm, kw), lambda i, k, n: (k, i, 0)),
                          out_shape=jax.ShapeDtypeStruct((nbk, T, kw), out_dtype), scratch_shapes=scratch,
                          compiler_params=_params())(*args)


def _mm_nt_resident(dy, w, *, name, res=None, res_scale=1.0, ln_bwd=None, tm=None, n_map=_same, behind=()):
    nbn, T, nw = dy.shape
    nbk, _, kw, _ = w.shape
    assert nbk == 1
    tm = min(tm or RESIDENT_TOKEN_TILE, T)
    has_res = res is not None
    n_in = 2 + has_res + (3 if ln_bwd else 0) + len(behind)

    def body(*refs):
        dy_ref, w_ref = refs[:2]
        res_ref = refs[2] if has_res else None
        y = _dot_nt(dy_ref[0], w_ref[0, n_map(0)])
        for n in range(1, nbn):
            y = y + _dot_nt(dy_ref[n], w_ref[0, n_map(n)])
        if has_res:
            y = y + res_scale * res_ref[0].astype(F32)
        if ln_bwd is None:
            refs[n_in][0] = y
        else:
            xh_ref, rs_ref, g_ref = refs[2 + has_res:5 + has_res]
            o_ref, dg_ref, db_ref = refs[n_in:n_in + 3]
            out, dg, db = _ln_bwd_rows(y, xh_ref[0], rs_ref[...], g_ref[...])
            o_ref[0] = out
            _accumulate(pl.program_id(0), (dg_ref, db_ref), (dg, db))

    tok = pl.BlockSpec((1, tm, kw), lambda i: (0, i, 0))
    vec = pl.BlockSpec((1, kw), lambda i: (0, 0))
    in_specs = [pl.BlockSpec((nbn, tm, nw), lambda i: (0, i, 0)),
                pl.BlockSpec(w.shape, lambda i: (0, 0, 0, 0))]
    args = [dy, w]
    if has_res:
        in_specs.append(tok)
        args.append(res)
    out_specs, out_shape = tok, jax.ShapeDtypeStruct((1, T, kw), F32)
    if ln_bwd is not None:
        in_specs += [tok, pl.BlockSpec((tm, 1), lambda i: (i, 0)), vec]
        args += list(ln_bwd)
        out_specs = [tok, vec, vec]
        out_shape = [out_shape, jax.ShapeDtypeStruct((1, kw), F32), jax.ShapeDtypeStruct((1, kw), F32)]
    in_specs += [pl.BlockSpec(memory_space=pl.ANY)] * len(behind)
    args += list(behind)
    return pl.pallas_call(body, name=name, grid=(T // tm,), in_specs=in_specs, out_specs=out_specs,
                          out_shape=out_shape, compiler_params=_params())(*args)


def _mm_tn(a, dy, *, name, tm=None, n_map=_same):
    nbk, T, kw = a.shape
    nbn, _, nw = dy.shape
    tm = min(tm or REDUCE_TOKEN_TILE, T)
    nt = T // tm

    def body(a_ref, dy_ref, o_ref, acc_ref):
        i = pl.program_id(2)
        part = _dot_tn(a_ref[...], dy_ref[...])

        @pl.when(i == 0)
        def _():
            acc_ref[...] = part

        @pl.when(i > 0)
        def _():
            acc_ref[...] += part

        @pl.when(i == nt - 1)
        def _():
            o_ref[...] = acc_ref[...].astype(o_ref.dtype)

    return pl.pallas_call(body, name=name, grid=(nbk, nbn, nt),
                          in_specs=[pl.BlockSpec((None, tm, kw), lambda k, n, i: (k, i, 0)),
                                    pl.BlockSpec((None, tm, nw), lambda k, n, i: (n, i, 0))],
                          out_specs=pl.BlockSpec((None, None, kw, nw), lambda k, n, i: (k, n_map(n), 0, 0)),
                          out_shape=jax.ShapeDtypeStruct((nbk, nbn, kw, nw), GRAD_DTYPE),
                          scratch_shapes=[pltpu.VMEM((kw, nw), F32)],
                          compiler_params=_params())(a, dy)


def _ln_bwd_rows(dh, xh, rstd, g):
    dxh = dh * g
    m1 = jnp.mean(dxh, axis=-1, keepdims=True)
    m2 = jnp.mean(dxh * xh, axis=-1, keepdims=True)
    dy = rstd * (dxh - m1 - xh * m2)
    return dy, jnp.sum(dh * xh, axis=0, keepdims=True), jnp.sum(dh, axis=0, keepdims=True)


def _accumulate(i, refs, parts):
    @pl.when(i == 0)
    def _():
        for r, p in zip(refs, parts):
            r[...] = jnp.zeros_like(r) + p

    @pl.when(i > 0)
    def _():
        for r, p in zip(refs, parts):
            r[...] += p


def _loss_ln_bwd(tgt, xhat, rstd, g, b, *, name, tm=None):
    _, T, D = xhat.shape
    tm = min(tm or TOKEN_TILE, T)

    def body(t_ref, xh_ref, rs_ref, g_ref, b_ref, dy_ref, dg_ref, db_ref, loss_ref):
        i = pl.program_id(0)
        xh = xh_ref[...]
        err = xh * g_ref[...] + b_ref[...] - t_ref[...]
        part = 0.5 * jnp.sum(jnp.mean(err * err, axis=-1, keepdims=True), axis=0, keepdims=True)
        dy, dg, db = _ln_bwd_rows(err / D, xh, rs_ref[...], g_ref[...])
        dy_ref[...] = dy
        _accumulate(i, (dg_ref, db_ref, loss_ref), (dg, db, part))

    tok = pl.BlockSpec((None, tm, D), lambda i: (0, i, 0))
    vec = pl.BlockSpec((1, D), lambda i: (0, 0))
    return pl.pallas_call(body, name=name, grid=(T // tm,),
                          in_specs=[tok, tok, pl.BlockSpec((tm, 1), lambda i: (i, 0)), vec, vec],
                          out_specs=[tok, vec, vec, pl.BlockSpec((SUBLANES, LANES), lambda i: (0, 0))],
                          out_shape=[jax.ShapeDtypeStruct((1, T, D), F32), jax.ShapeDtypeStruct((1, D), F32),
                                     jax.ShapeDtypeStruct((1, D), F32), jax.ShapeDtypeStruct((SUBLANES, LANES), F32)],
                          compiler_params=_params())(tgt, xhat, rstd, g, b)


def _shift_rows(ext, k, n, halo):
    if k == 0:
        return ext[halo:halo + n]
    return pltpu.roll(ext, k, axis=0)[halo:halo + n]


def _conv_rows(ext, cw_ref, n, halo):
    return (cw_ref[0:1, :] * _shift_rows(ext, 2, n, halo) + cw_ref[1:2, :] * _shift_rows(ext, 1, n, halo)
            + cw_ref[2:3, :] * ext[halo:halo + n] + cw_ref[3:4, :])


def _pair_map(n):
    return n // 2 + 4 * (n % 2)


def _ffn_up(hb, w_in, cw, *, name, tm=None):
    _, T, D = hb.shape
    _, nb, _, fb = w_in.shape
    half = nb // 2
    tm = min(tm or TOKEN_TILE, T)

    def body(h_ref, wa_ref, wb_ref, cwa_ref, cwb_ref, u_ref, ab_ref, act_ref, carry):
        @pl.when(pl.program_id(1) == 0)
        def _():
            carry[...] = jnp.zeros_like(carry)

        h = h_ref[...]
        conv = []
        for s, (w_ref, cw_ref) in enumerate(((wa_ref, cwa_ref), (wb_ref, cwb_ref))):
            uf = _dot(h, w_ref[...])
            u_ref[s] = uf.astype(u_ref.dtype)
            ext = jnp.concatenate([carry[s], uf], axis=0)
            c = _conv_rows(ext, cw_ref, tm, SUBLANES)
            ab_ref[s] = c.astype(ab_ref.dtype)
            conv.append(c)
            carry[s] = uf[tm - SUBLANES:tm]
        a, b = conv
        act_ref[...] = (a * _sigmoid(a) * b).astype(act_ref.dtype)

    wspec = lambda off: pl.BlockSpec((None, None, D, fb), lambda p, i: (0, p + off, 0, 0))
    cws = lambda off: pl.BlockSpec((None, SUBLANES, fb), lambda p, i: (p + off, 0, 0))
    return pl.pallas_call(body, name=name, grid=(half, T // tm),
                          in_specs=[pl.BlockSpec((None, tm, D), lambda p, i: (0, i, 0)), wspec(0), wspec(half),
                                    cws(0), cws(half)],
                          out_specs=[pl.BlockSpec((None, 2, tm, fb), lambda p, i: (p, 0, i, 0)),
                                     pl.BlockSpec((None, 2, tm, fb), lambda p, i: (p, 0, i, 0)),
                                     pl.BlockSpec((None, tm, fb), lambda p, i: (p, i, 0))],
                          out_shape=[jax.ShapeDtypeStruct((half, 2, T, fb), MXU),
                                     jax.ShapeDtypeStruct((half, 2, T, fb), MXU),
                                     jax.ShapeDtypeStruct((half, T, fb), MXU)],
                          scratch_shapes=[pltpu.VMEM((2, SUBLANES, fb), F32)],
                          compiler_params=_params())(hb, w_in, w_in, cw, cw)


def _ffn_gate_bwd(dy, u, ab, w_out, cw, *, name, tm=None):
    _, T, D = dy.shape
    half, _, _, fb = u.shape
    tm = min(tm or TOKEN_TILE, T)
    nt = T // tm

    n_full = fb // LANES
    tail = slice(n_full * LANES, fb)

    def body(dy_ref, u_ref, ab_ref, w_ref, cwa_ref, cwb_ref, du_ref, dwo_ref, dcw_ref, carry, acc, gacc):
        i = pl.program_id(1)

        @pl.when(i == 0)
        def _():
            carry[...] = jnp.zeros_like(carry)
            acc[...] = jnp.zeros_like(acc)
            gacc[...] = jnp.zeros_like(gacc)
            dcw_ref[...] = jnp.zeros_like(dcw_ref)

        dyv = dy_ref[...]
        dact = _dot_nt(dyv, w_ref[...])
        a = ab_ref[0].astype(F32)
        b = ab_ref[1].astype(F32)
        sa = _sigmoid(a)
        silu = a * sa
        acc[...] += _dot_tn(silu * b, dyv)
        dcs = (dact * b * (sa * (1.0 + a * (1.0 - sa))), dact * silu)
        m = tm + SUBLANES
        rows = lax.broadcasted_iota(jnp.int32, (SUBLANES, fb), 0)
        for s, cw_ref in enumerate((cwa_ref, cwb_ref)):
            dc = dcs[s]
            nxt = jnp.concatenate([dc, carry[s]], axis=0)
            dc1 = pltpu.roll(nxt, m - 1, axis=0)[:tm]
            dc2 = pltpu.roll(nxt, m - 2, axis=0)[:tm]
            du_ref[s] = (cw_ref[2:3, :] * dc + cw_ref[1:2, :] * dc1 + cw_ref[0:1, :] * dc2).astype(du_ref.dtype)
            carry[s] = dc[0:SUBLANES]
            dcb = [x.astype(MXU) for x in (dc, dc1, dc2)]
            for j in range(n_full):
                blk = slice(j * LANES, (j + 1) * LANES)
                gacc[s, j] += _dot_tn(u_ref[s, :, blk], jnp.concatenate([x[:, blk] for x in dcb], axis=1))
            dcw_ref[s] += jnp.where(rows == 3, jnp.sum(dc, axis=0, keepdims=True), 0.0)
            if fb > n_full * LANES:
                ut = u_ref[s, :, tail].astype(F32)
                gt = [jnp.sum(x[:, tail] * ut, axis=0, keepdims=True) for x in (dc2, dc1, dc)]
                rt = rows[:, tail]
                dcw_ref[s, :, tail] += jnp.where(rt == 0, gt[0], jnp.where(rt == 1, gt[1], jnp.where(rt == 2, gt[2], 0.0)))

        @pl.when(i == nt - 1)
        def _():
            dwo_ref[...] = acc[...].astype(dwo_ref.dtype)
            eye = _tri(LANES, True) & _tri(LANES, False)
            for s in range(2):
                for j in range(n_full):
                    g = gacc[s, j]
                    for tap in range(3):
                        d = jnp.where(eye, g[:, (2 - tap) * LANES:(3 - tap) * LANES], 0.0)
                        dcw_ref[s, tap:tap + 1, j * LANES:(j + 1) * LANES] = jnp.sum(d, axis=0, keepdims=True)

    rev = lambda i: nt - 1 - i
    cws = lambda off: pl.BlockSpec((None, SUBLANES, fb), lambda p, i: (p + off, 0, 0))
    pair = lambda: pl.BlockSpec((None, 2, tm, fb), lambda p, i: (p, 0, rev(i), 0))
    return pl.pallas_call(body, name=name, grid=(half, nt),
                          in_specs=[pl.BlockSpec((None, tm, D), lambda p, i: (0, rev(i), 0)), pair(), pair(),
                                    pl.BlockSpec((None, None, fb, D), lambda p, i: (p, 0, 0, 0)), cws(0), cws(half)],
                          out_specs=[pair(),
                                     pl.BlockSpec((None, None, fb, D), lambda p, i: (p, 0, 0, 0)),
                                     pl.BlockSpec((None, 2, SUBLANES, fb), lambda p, i: (p, 0, 0, 0))],
                          out_shape=[jax.ShapeDtypeStruct((half, 2, T, fb), MXU),
                                     jax.ShapeDtypeStruct((half, 1, fb, D), GRAD_DTYPE),
                                     jax.ShapeDtypeStruct((half, 2, SUBLANES, fb), F32)],
                          scratch_shapes=[pltpu.VMEM((2, SUBLANES, fb), F32), pltpu.VMEM((fb, D), F32),
                                          pltpu.VMEM((2, n_full, LANES, 3 * LANES), F32)],
                          compiler_params=_params())(dy, u, ab, w_out, cw, cw)


def _tri(n, lower):
    r = lax.broadcasted_iota(jnp.int32, (n, n), 0)
    c = lax.broadcasted_iota(jnp.int32, (n, n), 1)
    return (r >= c) if lower else (r <= c)


def _hgrn_gates(zq, zf, lb):
    sq = _sigmoid(zq)
    sf = _sigmoid(zf)
    fg = lb + (1.0 - lb) * sf
    return zq * sq, sq, sf, fg, jnp.log(fg)


def _lb_of(lbl_ref, cols):
    return _sigmoid(lbl_ref[0:1, cols] - lbl_ref[1:2, cols])


def _ones_where(mask):
    return jnp.where(mask, 1.0, 0.0).astype(jnp.bfloat16)


def _hgrn_fwd(z, lbl, gw, *, name):
    _, T, zw = z.shape
    C = min(HG_CHUNK, T)
    nch = T // C
    S = HG_STEP_CHUNKS if nch % HG_STEP_CHUNKS == 0 else 1
    hpb = zw // HG_DIM

    def body(z_ref, lbl_ref, gw_ref, og_ref, st_ref, s_scr, bc_scr, q_scr, k_scr):
        c = pl.program_id(0)

        @pl.when(c == 0)
        def _():
            s_scr[...] = jnp.zeros_like(s_scr)

        low = _tri(C, True)
        low01 = _ones_where(low)
        gwv = gw_ref[...]
        H = range(HG_HEADS)
        col = lambda h: slice(h * HG_DIM, (h + 1) * HG_DIM)
        for sub in range(S):
            chunk_body(z_ref, lbl_ref, og_ref, st_ref, s_scr, bc_scr, q_scr, k_scr,
                       sub, slice(sub * C, (sub + 1) * C), low, low01, gwv, H, col)

    def chunk_body(z_ref, lbl_ref, og_ref, st_ref, s_scr, bc_scr, q_scr, k_scr, sub, rows, low, low01, gwv, H, col):
        for blk in range(2):
            cols = slice(blk * zw, (blk + 1) * zw)
            qq, _, _, fg, lf = _hgrn_gates(z_ref[blk, rows], z_ref[2 + blk, rows], _lb_of(lbl_ref, cols))
            q_scr[:, cols] = qq
            k_scr[:, cols] = 1.0 - fg
            bc_scr[:, cols] = _exact_dot(low01, lf)
        zcol = lambda part, h: (part + h // hpb, rows, slice((h % hpb) * HG_DIM, (h % hpb + 1) * HG_DIM))
        b = [bc_scr[:, col(h)] for h in H]
        bm = [bc_scr[C // 2 - 1:C // 2, col(h)] for h in H]
        bl = [bc_scr[C - 1:C, col(h)] for h in H]
        q_ = [q_scr[:, col(h)] for h in H]
        k_ = [k_scr[:, col(h)] for h in H]
        v_ = [z_ref[zcol(4, h)] for h in H]
        qt = [q_[h] * jnp.exp(jnp.minimum(b[h] - bm[h], EXP_CLAMP)) for h in H]
        kt = [k_[h] * jnp.exp(jnp.minimum(bm[h] - b[h], EXP_CLAMP)) for h in H]
        A = [jnp.where(low, _dot_nt(qt[h], kt[h]), 0.0) for h in H]
        for h in H:
            st_ref[sub, h] = s_scr[h]
        o = [_dot_nt(q_[h] * jnp.exp(b[h]), s_scr[h]) + _dot(A[h], v_[h]) for h in H]
        for h in H:
            s_scr[h] = s_scr[h] * jnp.exp(bl[h]) + _dot_tn(v_[h], k_[h] * jnp.exp(bl[h] - b[h]))
        for h in H:
            g_h = z_ref[zcol(6, h)]
            r = lax.rsqrt(jnp.mean(o[h] * o[h], axis=-1, keepdims=True) + RMS_EPS)
            og_ref[rows, col(h)] = (o[h] * r * gwv * (g_h * _sigmoid(g_h))).astype(og_ref.dtype)

    return pl.pallas_call(body, name=name, grid=(nch // S,),
                          in_specs=[pl.BlockSpec((8, S * C, zw), lambda c: (0, c, 0)),
                                    pl.BlockSpec((2, D_MODEL), lambda c: (0, 0)),
                                    pl.BlockSpec((1, HG_DIM), lambda c: (0, 0))],
                          out_specs=[pl.BlockSpec((None, S * C, D_MODEL), lambda c: (0, c, 0)),
                                     pl.BlockSpec((S, HG_HEADS, HG_DIM, HG_DIM), lambda c: (c, 0, 0, 0))],
                          out_shape=[jax.ShapeDtypeStruct((1, T, D_MODEL), MXU),
                                     jax.ShapeDtypeStruct((nch, HG_HEADS, HG_DIM, HG_DIM), F32)],
                          scratch_shapes=[pltpu.VMEM((HG_HEADS, HG_DIM, HG_DIM), F32)]
                          + [pltpu.VMEM((C, D_MODEL), F32)] * 3,
                          compiler_params=_params())(z, lbl, gw)


def _hgrn_bwd(z, dog, states, lbl, gw, *, name):
    _, T, zw = z.shape
    C = min(HG_CHUNK, T)
    nch = T // C
    S = HG_STEP_CHUNKS if nch % HG_STEP_CHUNKS == 0 else 1
    hpb = zw // HG_DIM

    def body(z_ref, dog_ref, st0_ref, st1_ref, lbl_ref, gw_ref, dz_ref, dlb_ref, dgw_ref,
             d_scr, bc_scr, q_scr, sf_scr, fg_scr, x_scr):
        step = pl.program_id(0)

        @pl.when(step == 0)
        def _():
            d_scr[...] = jnp.zeros_like(d_scr)
            dlb_ref[...] = jnp.zeros_like(dlb_ref)
            dgw_ref[...] = jnp.zeros_like(dgw_ref)

        low = _tri(C, True)
        low01 = _ones_where(low)
        up01 = _ones_where(_tri(C, False))
        gwv = gw_ref[...]
        H = range(HG_HEADS)
        col = lambda h: slice(h * HG_DIM, (h + 1) * HG_DIM)
        for sub in reversed(range(S)):
            chunk_body(z_ref, dog_ref, st0_ref, st1_ref, lbl_ref, dz_ref, dlb_ref, dgw_ref,
                       d_scr, bc_scr, q_scr, sf_scr, fg_scr, x_scr,
                       sub, slice(sub * C, (sub + 1) * C), low, low01, up01, gwv, H, col)

    def chunk_body(z_ref, dog_ref, st0_ref, st1_ref, lbl_ref, dz_ref, dlb_ref, dgw_ref,
                   d_scr, bc_scr, q_scr, sf_scr, fg_scr, x_scr, sub, rows, low, low01, up01, gwv, H, col):
        st0 = lambda h: st0_ref[sub, h]
        st1 = (lambda h: st0_ref[sub + 1, h]) if sub + 1 < S else (lambda h: st1_ref[h])
        for blk in range(2):
            lbb = _lb_of(lbl_ref, slice(blk * zw, (blk + 1) * zw))
            qq, sq, sf, fg, lf = _hgrn_gates(z_ref[blk, rows], z_ref[2 + blk, rows], lbb)
            q_scr[:, blk * zw:(blk + 1) * zw] = qq
            sf_scr[:, blk * zw:(blk + 1) * zw] = sf
            fg_scr[:, blk * zw:(blk + 1) * zw] = fg
            bc_scr[:, blk * zw:(blk + 1) * zw] = _exact_dot(low01, lf)
        zcol = lambda part, h: (part + h // hpb, rows, slice((h % hpb) * HG_DIM, (h % hpb + 1) * HG_DIM))
        b = [bc_scr[:, col(h)] for h in H]
        bm = [bc_scr[C // 2 - 1:C // 2, col(h)] for h in H]
        bl = [bc_scr[C - 1:C, col(h)] for h in H]
        q_ = [q_scr[:, col(h)] for h in H]
        k_ = [1.0 - fg_scr[:, col(h)] for h in H]
        v_ = [z_ref[zcol(4, h)] for h in H]
        eq = [jnp.exp(jnp.minimum(b[h] - bm[h], EXP_CLAMP)) for h in H]
        ek = [jnp.exp(jnp.minimum(bm[h] - b[h], EXP_CLAMP)) for h in H]
        eb = [jnp.exp(b[h]) for h in H]
        el = [jnp.exp(bl[h] - b[h]) for h in H]
        qt = [q_[h] * eq[h] for h in H]
        kt = [k_[h] * ek[h] for h in H]
        q0 = [q_[h] * eb[h] for h in H]
        kd = [k_[h] * el[h] for h in H]
        A = [jnp.where(low, _dot_nt(qt[h], kt[h]), 0.0) for h in H]
        o = [_dot_nt(q0[h], st0(h)) + _dot(A[h], v_[h]) for h in H]
        do = []
        dgw_acc = jnp.zeros((1, HG_DIM), F32)
        for h in H:
            g_h = z_ref[zcol(6, h)]
            r = lax.rsqrt(jnp.mean(o[h] * o[h], axis=-1, keepdims=True) + RMS_EPS)
            on = o[h] * r
            sg = _sigmoid(g_h)
            dogh = dog_ref[rows, col(h)].astype(F32)
            t1 = dogh * on
            dgw_acc = dgw_acc + jnp.sum(t1 * (g_h * sg), axis=0, keepdims=True)
            dz_ref[zcol(6, h)] = (t1 * gwv * (sg * (1.0 + g_h * (1.0 - sg)))).astype(dz_ref.dtype)
            don = dogh * gwv * (g_h * sg)
            do.append(r * (don - on * jnp.mean(don * on, axis=-1, keepdims=True)))
        dgw_ref[...] += dgw_acc
        P = [jnp.where(low, _dot_nt(do[h], v_[h]), 0.0) for h in H]
        dqq = [eb[h] * _dot(do[h], st0(h)) + eq[h] * _dot_hp(P[h], kt[h], ((1,), (0,))) for h in H]
        dkk = [el[h] * _dot(v_[h], d_scr[h]) + ek[h] * _dot_hp(P[h], qt[h], ((0,), (0,))) for h in H]
        for h in H:
            dz_ref[zcol(4, h)] = (_dot_nt(kd[h], d_scr[h]) + _dot_tn(A[h], do[h])).astype(dz_ref.dtype)
            x_scr[:, col(h)] = q_[h] * dqq[h] - k_[h] * dkk[h]
        edge = [jnp.sum(d_scr[h] * st1(h), axis=0, keepdims=True) for h in H]
        for h in H:
            d_scr[h] = d_scr[h] * jnp.exp(bl[h]) + _dot_tn(do[h], q0[h])
        for blk in range(2):
            x_scr[:, blk * zw:(blk + 1) * zw] = _exact_dot(up01, x_scr[:, blk * zw:(blk + 1) * zw])
        dlb = []
        for h in H:
            dfg = (x_scr[:, col(h)] + edge[h]) / fg_scr[:, col(h)] - dkk[h]
            sf_h = sf_scr[:, col(h)]
            lb_h = _lb_of(lbl_ref, col(h))
            zq_h = z_ref[zcol(0, h)]
            sq_h = _sigmoid(zq_h)
            dlb.append(jnp.sum(dfg * (1.0 - sf_h), axis=0, keepdims=True))
            dz_ref[zcol(0, h)] = (dqq[h] * (sq_h * (1.0 + zq_h * (1.0 - sq_h)))).astype(dz_ref.dtype)
            dz_ref[zcol(2, h)] = (dfg * (1.0 - lb_h) * sf_h * (1.0 - sf_h)).astype(dz_ref.dtype)
        dlb_ref[...] += jnp.concatenate(dlb, axis=1)

    nst = nch // S
    rev = lambda s: nst - 1 - s
    return pl.pallas_call(body, name=name, grid=(nst,),
                          in_specs=[pl.BlockSpec((8, S * C, zw), lambda s: (0, rev(s), 0)),
                                    pl.BlockSpec((None, S * C, D_MODEL), lambda s: (0, rev(s), 0)),
                                    pl.BlockSpec((S, HG_HEADS, HG_DIM, HG_DIM), lambda s: (rev(s), 0, 0, 0)),
                                    pl.BlockSpec((None, HG_HEADS, HG_DIM, HG_DIM),
                                                 lambda s: (jnp.minimum((rev(s) + 1) * S, nch - 1), 0, 0, 0)),
                                    pl.BlockSpec((2, D_MODEL), lambda s: (0, 0)),
                                    pl.BlockSpec((1, HG_DIM), lambda s: (0, 0))],
                          out_specs=[pl.BlockSpec((8, S * C, zw), lambda s: (0, rev(s), 0)),
                                     pl.BlockSpec((1, D_MODEL), lambda s: (0, 0)),
                                     pl.BlockSpec((1, HG_DIM), lambda s: (0, 0))],
                          out_shape=[jax.ShapeDtypeStruct((8, T, zw), MXU), jax.ShapeDtypeStruct((1, D_MODEL), F32),
                                     jax.ShapeDtypeStruct((1, HG_DIM), F32)],
                          scratch_shapes=[pltpu.VMEM((HG_HEADS, HG_DIM, HG_DIM), F32)]
                          + [pltpu.VMEM((C, D_MODEL), F32)] * 5,
                          compiler_params=_params())(z, dog, states, states, lbl, gw)


def _bucket_onehot():
    W = SW_WINDOW
    t = np.arange(W)[:, None] + W
    s = np.arange(2 * W)[None, :]
    dist = t - s
    exact = REL_BUCKETS // 2
    d = np.maximum(np.maximum(dist, 0), 1).astype(np.float32)
    log_b = exact + (np.log(d / np.float32(exact)) / np.float32(math.log(REL_MAX_DIST / exact))
                     * np.float32(REL_BUCKETS - exact)).astype(np.int32)
    bucket = np.where(np.maximum(dist, 0) < exact, np.maximum(dist, 0), np.minimum(log_b, REL_BUCKETS - 1))
    valid = (dist >= 0) & (dist < W)
    onehot = (bucket[..., None] == np.arange(REL_BUCKETS)) & valid[..., None]
    return onehot.reshape(W * 2 * W, REL_BUCKETS).astype(np.float32)


def _bias_expand(rel_t, onehot_t, *, name):
    hq, nbk = rel_t.shape
    n = onehot_t.shape[1]

    def body(r_ref, oh_ref, o_ref):
        o_ref[...] = _exact_dot_r(r_ref[...], oh_ref[...])

    return pl.pallas_call(body, name=name, out_shape=jax.ShapeDtypeStruct((hq, n), F32),
                          compiler_params=_params())(rel_t, onehot_t)


def _bias_reduce(dbias, onehot, *, name):
    hq = dbias.shape[0]
    nbk = onehot.shape[1]

    def body(d_ref, oh_ref, o_ref):
        o_ref[...] = _exact_dot_r(d_ref[...], oh_ref[...])

    return pl.pallas_call(body, name=name, out_shape=jax.ShapeDtypeStruct((hq, nbk), F32),
                          compiler_params=_params())(dbias, onehot)


def _swa_mask(j):
    W = SW_WINDOW
    t = lax.broadcasted_iota(jnp.int32, (W, 2 * W), 0) + W
    s = lax.broadcasted_iota(jnp.int32, (W, 2 * W), 1)
    dist = t - s
    band = (dist >= 0) & (dist < W)
    m = band & ((j > 0) | (s >= W))
    return jnp.concatenate([m] * SW_GROUP, axis=0)


def _half_mask(rows, half):
    lane = lax.broadcasted_iota(jnp.int32, (rows, LANES), 1)
    return (lane >= SW_HEAD_DIM) if half else (lane < SW_HEAD_DIM)


_ALL = slice(None)


def _swa_head(ref, col0, head, to_half, rows=_ALL):
    slab, half = head // 2, head % 2
    x = ref[rows, col0 + slab * LANES:col0 + (slab + 1) * LANES]
    x = jnp.where(_half_mask(x.shape[0], half), x, 0.0)
    return x if half == to_half else pltpu.roll(x, SW_HEAD_DIM, axis=1)


def _swa_stack(ref, g, rows=_ALL):
    return jnp.concatenate([_swa_head(ref, 0, g * SW_GROUP + r, g % 2, rows) for r in range(SW_GROUP)], axis=0)


def _swa_unstack(ref, x, g, rows=_ALL):
    W = SW_WINDOW
    for pair in range(SW_GROUP // 2):
        parts = []
        for r in (2 * pair, 2 * pair + 1):
            piece = x[r * W:(r + 1) * W]
            parts.append(piece if r % 2 == g % 2 else pltpu.roll(piece, SW_HEAD_DIM, axis=1))
        slab = (g * SW_GROUP) // 2 + pair
        ref[rows, slab * LANES:(slab + 1) * LANES] = (parts[0] + parts[1]).astype(ref.dtype)


def _swa_kv(kp_ref, kc_ref, col0, g, prev_rows=_ALL, rows=_ALL):
    return jnp.concatenate([_swa_head(kp_ref, col0, g, g % 2, prev_rows), _swa_head(kc_ref, col0, g, g % 2, rows)],
                           axis=0)


def _lane_pick(tile, h):
    lane = lax.broadcasted_iota(jnp.int32, tile.shape, 1)
    return jnp.sum(jnp.where(lane == h, tile, 0.0), axis=-1, keepdims=True)


def _lane_put(tile, h, col):
    lane = lax.broadcasted_iota(jnp.int32, tile.shape, 1)
    return jnp.where(lane == h, col, tile)


def _swa_rows(vals):
    return jnp.concatenate([jnp.broadcast_to(v, (SW_WINDOW, 1)) for v in vals], axis=0)


def _swa_fwd(q, kv, bias, sinks, *, name):
    _, T, D = q.shape
    W = SW_WINDOW
    nb = T // W
    dh = SW_HEAD_DIM
    kvw = SW_KV_HEADS * dh
    scale = dh ** -0.5

    S = SW_STEP_BLOCKS if nb % SW_STEP_BLOCKS == 0 else 1

    def body(q_ref, kc_ref, kp_ref, bias_ref, sink_ref, o_ref, lse_ref):
        c = pl.program_id(0)
        sk = sink_ref[...]
        rows = [slice(s * W, (s + 1) * W) for s in range(S)]
        before = [(kp_ref, _ALL)] + [(kc_ref, rows[s - 1]) for s in range(1, S)]
        masks = [_swa_mask(c * S + s) for s in range(S)]
        I = [(s, g) for s in range(S) for g in range(SW_KV_HEADS)]
        kv_of = lambda col0, s, g: _swa_kv(before[s][0], kc_ref, col0, g, before[s][1], rows[s])
        kk = [kv_of(0, s, g) for s, g in I]
        vv = [kv_of(kvw, s, g) for s, g in I]
        qs = [_swa_stack(q_ref, g, rows[s]) for s, g in I]
        bias_of = lambda g: bias_ref[g * SW_GROUP:(g + 1) * SW_GROUP].reshape(SW_GROUP * W, 2 * W)
        logits = [jnp.where(masks[s], _dot_nt(qs[n], kk[n]) * scale + bias_of(g), NEG_BIG) for n, (s, g) in enumerate(I)]
        sink = [_swa_rows([_lane_pick(sk, g * SW_GROUP + r) for r in range(SW_GROUP)]) for s, g in I]
        m = [jnp.maximum(jnp.max(logits[n], axis=-1, keepdims=True), sink[n]) for n in range(len(I))]
        p = [jnp.exp(logits[n] - m[n]) for n in range(len(I))]
        den = [jnp.sum(p[n], axis=-1, keepdims=True) + jnp.exp(sink[n] - m[n]) for n in range(len(I))]
        pv = [_dot(p[n], vv[n]) for n in range(len(I))]
        lse_tiles = [jnp.zeros((W, SW_Q_HEADS), F32) for _ in range(S)]
        for n, (s, g) in enumerate(I):
            _swa_unstack(o_ref, pv[n] / den[n], g, rows[s])
            lse = m[n] + jnp.log(den[n])
            for r in range(SW_GROUP):
                lse_tiles[s] = _lane_put(lse_tiles[s], g * SW_GROUP + r, lse[r * W:(r + 1) * W])
        for s in range(S):
            lse_ref[rows[s], :] = lse_tiles[s]

    return pl.pallas_call(body, name=name, grid=(nb // S,),
                          in_specs=[pl.BlockSpec((None, S * W, D), lambda j: (0, j, 0)),
                                    pl.BlockSpec((None, S * W, 2 * kvw), lambda j: (0, j, 0)),
                                    pl.BlockSpec((None, W, 2 * kvw), lambda j: (0, jnp.maximum(j * S - 1, 0), 0)),
                                    pl.BlockSpec((SW_Q_HEADS, W, 2 * W), lambda j: (0, 0, 0)),
                                    pl.BlockSpec((1, SW_Q_HEADS), lambda j: (0, 0))],
                          out_specs=[pl.BlockSpec((None, S * W, D), lambda j: (0, j, 0)),
                                     pl.BlockSpec((S * W, SW_Q_HEADS), lambda j: (j, 0))],
                          out_shape=[jax.ShapeDtypeStruct((1, T, D), F32), jax.ShapeDtypeStruct((T, SW_Q_HEADS), F32)],
                          compiler_params=_params())(q, kv, kv, bias, sinks)


def _swa_bwd(q, kv, o, lse, do, bias, sinks, *, name):
    _, T, D = q.shape
    W = SW_WINDOW
    nb = T // W
    dh = SW_HEAD_DIM
    kvw = SW_KV_HEADS * dh
    scale = dh ** -0.5
    cl = lambda j: jnp.minimum(j, nb - 1)

    def body(q_ref, kc_ref, kp_ref, o_ref, lse_ref, do_ref, bias_ref, sink_ref,
             dq_ref, dkv_ref, dbias_ref, dsink_ref, carry):
        j = pl.program_id(0)

        @pl.when(j == 0)
        def _():
            carry[...] = jnp.zeros_like(carry)
            dbias_ref[...] = jnp.zeros_like(dbias_ref)
            dsink_ref[...] = jnp.zeros_like(dsink_ref)

        @pl.when(j < nb)
        def _():
            mask = _swa_mask(j)
            sk = sink_ref[...]
            lse_tile = lse_ref[...]
            dsink = jnp.zeros((1, SW_Q_HEADS), F32)
            G = range(SW_KV_HEADS)
            heads = [[g * SW_GROUP + r for r in range(SW_GROUP)] for g in G]
            kk = [_swa_kv(kp_ref, kc_ref, 0, g) for g in G]
            vv = [_swa_kv(kp_ref, kc_ref, kvw, g) for g in G]
            qs = [_swa_stack(q_ref, g) for g in G]
            dos = [_swa_stack(do_ref, g) for g in G]
            lse = [jnp.concatenate([_lane_pick(lse_tile, h) for h in heads[g]], axis=0) for g in G]
            sink = [_swa_rows([_lane_pick(sk, h) for h in heads[g]]) for g in G]
            logits = [jnp.where(mask, _dot_nt(qs[g], kk[g]) * scale
                                + bias_ref[g * SW_GROUP:(g + 1) * SW_GROUP].reshape(SW_GROUP * W, 2 * W), NEG_BIG)
                      for g in G]
            dp = [_dot_nt(dos[g], vv[g]) for g in G]
            p = [jnp.exp(logits[g] - lse[g]) for g in G]
            delta = [jnp.sum(dos[g] * _swa_stack(o_ref, g), axis=-1, keepdims=True) for g in G]
            dl = [p[g] * (dp[g] - delta[g]) for g in G]
            dqs = [_dot(dl[g], kk[g]) * scale for g in G]
            dks = [_dot_tn(dl[g], qs[g]) * scale for g in G]
            dvs = [_dot_tn(p[g], dos[g]) for g in G]
            for g in G:
                _swa_unstack(dq_ref, dqs[g], g)
                dbias_ref[g * SW_GROUP:(g + 1) * SW_GROUP] += dl[g].reshape(SW_GROUP, W, 2 * W)
                sd = jnp.exp(sink[g] - lse[g]) * delta[g]
                for r, h in enumerate(heads[g]):
                    dsink = _lane_put(dsink, h, -jnp.sum(sd[r * W:(r + 1) * W], axis=0, keepdims=True))
            dsink_ref[...] += dsink
            for slab in range(SW_KV_HEADS // 2):
                for col0, parts in ((0, dks), (kvw, dvs)):
                    both = parts[2 * slab] + parts[2 * slab + 1]
                    cols = slice(col0 + slab * LANES, col0 + (slab + 1) * LANES)
                    dkv_ref[:, cols] = (carry[:, cols] + both[:W]).astype(dkv_ref.dtype)
                    carry[:, cols] = both[W:]

        @pl.when(j == nb)
        def _():
            dkv_ref[...] = carry[...].astype(dkv_ref.dtype)

    tok = lambda w: pl.BlockSpec((None, W, w), lambda j: (0, cl(j), 0))
    return pl.pallas_call(body, name=name, grid=(nb + 1,),
                          in_specs=[tok(D), tok(2 * kvw),
                                    pl.BlockSpec((None, W, 2 * kvw), lambda j: (0, jnp.maximum(cl(j) - 1, 0), 0)),
                                    tok(D), pl.BlockSpec((W, SW_Q_HEADS), lambda j: (cl(j), 0)), tok(D),
                                    pl.BlockSpec((SW_Q_HEADS, W, 2 * W), lambda j: (0, 0, 0)),
                                    pl.BlockSpec((1, SW_Q_HEADS), lambda j: (0, 0))],
                          out_specs=[tok(D),
                                     pl.BlockSpec((None, W, 2 * kvw), lambda j: (0, jnp.maximum(j - 1, 0), 0)),
                                     pl.BlockSpec((SW_Q_HEADS, W, 2 * W), lambda j: (0, 0, 0)),
                                     pl.BlockSpec((1, SW_Q_HEADS), lambda j: (0, 0))],
                          out_shape=[jax.ShapeDtypeStruct((1, T, D), MXU), jax.ShapeDtypeStruct((1, T, 2 * kvw), MXU),
                                     jax.ShapeDtypeStruct((SW_Q_HEADS, W, 2 * W), F32),
                                     jax.ShapeDtypeStruct((1, SW_Q_HEADS), F32)],
                          scratch_shapes=[pltpu.VMEM((W, 2 * kvw), F32)],
                          compiler_params=_params())(q, kv, kv, o, lse, do, bias, sinks)


_HBM = pl.BlockSpec(memory_space=pltpu.HBM)
_SEM = pl.BlockSpec(memory_space=pltpu.SEMAPHORE)
_EFFECT = pltpu.SideEffectType.DATAFLOW_SIDE_EFFECTING
N_PEERS = N_DEV - 1


def _peer(k):
    x, y, c = lax.axis_index("x"), lax.axis_index("y"), lax.axis_index("c")
    px = (x + (k >> 2)) % 2
    py = (y + ((k >> 1) & 1)) % 2
    pc = (c + (k & 1)) % 2
    return (px, py, pc), 4 * px + 2 * py + pc


def _my_number():
    return 4 * lax.axis_index("x") + 2 * lax.axis_index("y") + lax.axis_index("c")


def _landing(src, mode):
    me = _my_number()
    own = src if mode == "gather" else lax.dynamic_index_in_dim(src, me, 0, keepdims=False)
    return lax.dynamic_update_index_in_dim(lax.empty((N_DEV,) + own.shape, own.dtype), own, me, 0)


def _copy(src_ref, land_ref, mode, send, recv, j, k, dst_slot):
    peer, pid = _peer(k)
    return pltpu.make_async_remote_copy(
        src_ref=src_ref if mode == "gather" else src_ref.at[pid], dst_ref=land_ref.at[dst_slot(pid)],
        send_sem=send.at[j * N_PEERS + k - 1], recv_sem=recv.at[j * N_PEERS + k - 1],
        device_id=peer, device_id_type=pl.DeviceIdType.MESH)


def _send_start(groups, *, name):
    flat = [t for g in groups for t in g]
    n, ng = len(flat), len(groups)
    srcs = [pltpu.with_memory_space_constraint(s, pltpu.HBM) for s, _ in flat]
    lands = [pltpu.with_memory_space_constraint(_landing(s, m), pltpu.HBM) for s, m in flat]

    def body(*refs):
        src_refs, land_refs = refs[:n], refs[n:2 * n]
        sems = refs[2 * n:2 * n + 2 * ng]
        token = refs[-1]
        me = _my_number()
        a = 0
        for gi, g in enumerate(groups):
            for j, (_, mode) in enumerate(g):
                for k in range(1, N_DEV):
                    _copy(src_refs[a], land_refs[a], mode, sems[2 * gi], sems[2 * gi + 1], j, k, lambda pid: me).start()
                a += 1
        token[...] = jnp.zeros_like(token)

    sem_shapes = []
    for g in groups:
        sem_shapes += [pltpu.SemaphoreType.DMA((len(g) * N_PEERS,))] * 2
    out = pl.pallas_call(
        body, name=name,
        out_shape=tuple(sem_shapes) + tuple(pltpu.HBM(a.shape, a.dtype) for a in srcs + lands)
        + (jax.ShapeDtypeStruct((SUBLANES, LANES), F32),),
        in_specs=[_HBM] * (2 * n), out_specs=[_SEM] * (2 * ng) + [_HBM] * (2 * n) + [pl.BlockSpec(memory_space=pltpu.VMEM)],
        input_output_aliases={i: 2 * ng + i for i in range(2 * n)},
        compiler_params=pltpu.CompilerParams(has_side_effects=_EFFECT))(*srcs, *lands)
    sems, thru, token = out[:2 * ng], out[2 * ng:2 * ng + 2 * n], out[-1]
    handles, a = [], 0
    for gi, g in enumerate(groups):
        m = len(g)
        handles.append((sems[2 * gi], sems[2 * gi + 1], list(thru[a:a + m]), list(thru[n + a:n + a + m]),
                        [mode for _, mode in g]))
        a += m
    return handles, token


def _send_wait(handle, after, *, name):
    send, recv, srcs, lands, modes = handle
    m = len(srcs)

    def body(*refs):
        src_refs, land_refs = refs[:m], refs[m:2 * m]
        send_ref, recv_ref = refs[2 * m], refs[2 * m + 1]
        for j in range(m):
            for k in range(1, N_DEV):
                cp = _copy(src_refs[j], land_refs[j], modes[j], send_ref, recv_ref, j, k, lambda pid: pid)
                cp.wait_send()
                cp.wait_recv()

    out = pl.pallas_call(
        body, name=name, out_shape=tuple(pltpu.HBM(a.shape, a.dtype) for a in srcs + lands),
        in_specs=[_HBM] * (2 * m) + [_SEM, _SEM] + [pl.BlockSpec(memory_space=pl.ANY)] * len(after),
        out_specs=[_HBM] * (2 * m), input_output_aliases={i: i for i in range(2 * m)},
        compiler_params=pltpu.CompilerParams(has_side_effects=_EFFECT))(*srcs, *lands, send, recv, *after)
    return list(out[m:])


def _adam_math(w, g, m, v):
    m = ADAM_B1 * m + (1.0 - ADAM_B1) * g
    v = ADAM_B2 * v + (1.0 - ADAM_B2) * (g * g)
    m_hat = m / (1.0 - ADAM_B1 ** ADAM_STEP)
    v_hat = v / (1.0 - ADAM_B2 ** ADAM_STEP)
    delta = -ADAM_LR * (m_hat / (jnp.sqrt(v_hat) + ADAM_EPS) + ADAM_WD * w)
    return delta, m, v


def _adamw(parts, w, m, v, *, name, layer=None):
    S, R, C = parts.shape
    tr = R
    for cand in (256, 128, 64, 32, 16, 8):
        if R % cand == 0 and S * cand * C * 4 <= 4 * 2 ** 20:
            tr = cand
            break

    def body(p_ref, w_ref, m_ref, v_ref, g_ref, d_ref, nm_ref, nv_ref):
        g = p_ref[0].astype(F32)
        for s in range(1, S):
            g = g + p_ref[s].astype(F32)
        delta, nm, nv = _adam_math(w_ref[...], g, m_ref[...], v_ref[...])
        g_ref[...] = g
        d_ref[...] = delta
        nm_ref[...] = nm
        nv_ref[...] = nv

    if layer is None:
        wspec = pl.BlockSpec((tr, C), lambda i: (i, 0))
    else:
        wspec = pl.BlockSpec((None, tr, C), lambda i: (layer, i, 0))
    ospec = pl.BlockSpec((tr, C), lambda i: (i, 0))
    osh = jax.ShapeDtypeStruct((R, C), F32)
    return pl.pallas_call(body, name=name, grid=(R // tr,),
                          in_specs=[pl.BlockSpec((S, tr, C), lambda i: (0, i, 0)), wspec, wspec, wspec],
                          out_specs=[ospec] * 4, out_shape=[osh] * 4, compiler_params=_params())(parts, w, m, v)


def _sum_parts(parts, *, name):
    S, R, C = parts.shape

    def body(p_ref, o_ref):
        g = p_ref[0]
        for s in range(1, S):
            g = g + p_ref[s]
        o_ref[...] = g

    return pl.pallas_call(body, name=name, out_shape=jax.ShapeDtypeStruct((R, C), F32),
                          compiler_params=_params())(parts)


def _pack_rows(arrays):
    pieces, layout, row = [], [], 0
    for a in arrays:
        flat = a.reshape(-1).astype(F32)
        rows = -(-flat.shape[0] // (SUBLANES * LANES)) * SUBLANES
        flat = jnp.pad(flat, (0, rows * LANES - flat.shape[0]))
        pieces.append(flat.reshape(rows, LANES))
        layout.append((row, rows, a.shape))
        row += rows
    return jnp.concatenate(pieces, axis=0), layout


def _unpack_rows(packed, layout):
    out = []
    for row, rows, shape in layout:
        size = int(np.prod(shape))
        out.append(packed[row:row + rows].reshape(-1)[:size].reshape(shape))
    return out


def _ffn_fwd(h, w_in, w_out, cw, ln_g, ln_b, tag):
    h, hb = h
    u, ab, act = _ffn_up(hb, w_in, cw, name=f"ffn_up_{tag}")
    hnb, xh, rs = _mm_nn(act, w_out, res=h, res_scale=ALPHA, ln=(ln_g, ln_b), name=f"ffn_down_{tag}")
    return hnb, xh, rs, (u, ab)


def _ffn_bwd(dy, hb, u, w_in, w_out, cw, ln_bwd, send, tag):
    du, dw_out, dcw = _ffn_gate_bwd(dy, u[0], u[1], w_out, cw, name=f"ffn_gate_bwd_{tag}")
    du = du.reshape((-1,) + du.shape[2:])
    dw_in = _mm_tn(hb, du, n_map=_pair_map, name=f"ffn_dwin_{tag}")
    handle, token = send(dw_in, dw_out)
    dyp, dg, db = _mm_nt_resident(du, w_in, n_map=_pair_map, res=dy, res_scale=ALPHA, ln_bwd=ln_bwd,
                                  behind=(token,), name=f"ffn_dh_{tag}")
    dcw = dcw.transpose(1, 0, 2, 3).reshape((-1,) + dcw.shape[2:])
    return dyp, dg, db, handle, dcw


def kernel(x, hgrn_w_in, hgrn_lb_logits, hgrn_gnorm_w, hgrn_w_out, swa_w_q, swa_sinks, swa_w_out, shared_w_kv, rel_bias, ffn_w_in, ffn_conv_w, ffn_conv_b, ffn_w_out, ln_mix_g, ln_mix_b, ln_ffn_g, ln_ffn_b, loss_target, m_hgrn_w_in, m_hgrn_lb_logits, m_hgrn_gnorm_w, m_hgrn_w_out, m_swa_w_q, m_swa_sinks, m_swa_w_out, m_shared_w_kv, m_rel_bias, m_ffn_w_in, m_ffn_conv_w, m_ffn_conv_b, m_ffn_w_out, m_ln_mix_g, m_ln_mix_b, m_ln_ffn_g, m_ln_ffn_b, v_hgrn_w_in, v_hgrn_lb_logits, v_hgrn_gnorm_w, v_hgrn_w_out, v_swa_w_q, v_swa_sinks, v_swa_w_out, v_shared_w_kv, v_rel_bias, v_ffn_w_in, v_ffn_conv_w, v_ffn_conv_b, v_ffn_w_out, v_ln_mix_g, v_ln_mix_b, v_ln_ffn_g, v_ln_ffn_b):
    T = x.shape[1]
    D = D_MODEL
    W = SW_WINDOW
    fb = ffn_w_in.shape[2]
    me = 4 * lax.axis_index("x") + 2 * lax.axis_index("y") + lax.axis_index("c")

    small_fwd, small_fwd_layout = _pack_rows([hgrn_lb_logits, ffn_conv_w])
    gat = lambda *ws: [(w_.astype(MXU), "gather") for w_ in ws]
    (wait_a, wait_b, wait_c), _ = _send_start(
        [gat(hgrn_w_in[0]) + [(small_fwd, "gather")],
         gat(hgrn_w_out[0], ffn_w_in[0], ffn_w_out[0]),
         gat(shared_w_kv, swa_w_q[0], swa_w_out[0], ffn_w_in[1], ffn_w_out[1])], name="gather_start")
    xb = x.astype(MXU)
    w_hin, small_all = _send_wait(wait_a, (xb,), name="gather_wait_a")
    w_hin = w_hin[None]
    ffn_rows = 2 * ffn_w_out.shape[1]
    (lb_row, lb_rows, _), (cw_row, cw_rows, _) = small_fwd_layout
    lbl = small_all[:, lb_row:lb_row + 2, :].transpose(1, 0, 2).reshape(2, D)
    conv_w_all = small_all[:, cw_row:cw_row + cw_rows, :].reshape(N_DEV, -1)[:, :DEPTH * 3 * fb]
    conv_w_all = conv_w_all.reshape(N_DEV, DEPTH, 3, fb).transpose(1, 0, 2, 3)
    conv_b_all = ffn_conv_b.reshape(DEPTH, N_DEV, 1, fb)
    no_pad = ((0, 0), (0, 0))
    cw = (jnp.pad(conv_w_all, no_pad + ((0, SUBLANES - 3), (0, 0)))
          + jnp.pad(conv_b_all, no_pad + ((3, SUBLANES - 4), (0, 0))))

    row = lambda a, l: a[l:l + 1]

    z = _mm_nn(xb, w_hin, name="hgrn_in")
    og, states = _hgrn_fwd(z, lbl, hgrn_gnorm_w, name="hgrn_rec")
    w_hout, w_fin0, w_fout0 = _send_wait(wait_b, (og,), name="gather_wait_b")
    w_hout = w_hout.reshape(1, 1, D, D)
    w_fin = [w_fin0[None], None]
    w_fout = [w_fout0.reshape(4, 1, ffn_rows, D), None]
    h1b, xh1, rs1 = _mm_nn(og, w_hout, res=x, res_scale=ALPHA, ln=(row(ln_mix_g, 0), row(ln_mix_b, 0)), name="hgrn_out")
    h1 = (xh1, row(ln_mix_g, 0), row(ln_mix_b, 0))
    h2b, xh2, rs2, u0 = _ffn_fwd((h1, h1b), w_fin[0], w_fout[0], cw[0], row(ln_ffn_g, 0), row(ln_ffn_b, 0), "l0")
    h2 = (xh2, row(ln_ffn_g, 0), row(ln_ffn_b, 0))
    w_kv, w_q, w_o, w_fin1, w_fout1 = _send_wait(wait_c, (h2b,), name="gather_wait_c")
    w_kv = w_kv.reshape(1, 1, D, 2 * SW_KV_HEADS * SW_HEAD_DIM)
    w_q = w_q.reshape(1, 1, D, D)
    w_o = w_o.reshape(1, 1, D, D)
    w_fin[1] = w_fin1[None]
    w_fout[1] = w_fout1.reshape(4, 1, ffn_rows, D)
    kv = _mm_nn(h2b, w_kv, name="swa_kv")
    q = _mm_nn(h2b, w_q, name="swa_q")
    onehot = _bucket_onehot()
    bias = _bias_expand(rel_bias.T, jnp.asarray(onehot.T, jnp.bfloat16), name="swa_bias").reshape(SW_Q_HEADS, W, 2 * W)
    ao, lse = _swa_fwd(q, kv, bias, swa_sinks, name="swa_attn")
    h3b, xh3, rs3 = _mm_nn(ao, w_o, res=h2, res_scale=ALPHA, ln=(row(ln_mix_g, 1), row(ln_mix_b, 1)), name="swa_out")
    h3 = (xh3, row(ln_mix_g, 1), row(ln_mix_b, 1))
    _, xh4, rs4, u1 = _ffn_fwd((h3, h3b), w_fin[1], w_fout[1], cw[1], row(ln_ffn_g, 1), row(ln_ffn_b, 1), "l1")
    dy4, dg_f1, db_f1, loss_tile = _loss_ln_bwd(loss_target, xh4, rs4, row(ln_ffn_g, 1), row(ln_ffn_b, 1),
                                                name="loss_ln_ffn1_bwd")
    sc = lambda *gs: [(g_, "scatter") for g_ in gs]

    def send_ffn(name_):
        def send(dw_in, dw_out):
            (handle,), token = _send_start([sc(dw_in.reshape(N_DEV, D, fb), dw_out.reshape(N_DEV, -1, D))], name=name_)
            return handle, token
        return send

    dy3, dg_m1, db_m1, ex1, dcw1 = _ffn_bwd(dy4, h3b, u1, w_fin[1], w_fout[1], cw[1],
                                            (xh3, rs3, row(ln_mix_g, 1)), send_ffn("grads_start_1"), "l1")
    dw_o = _mm_tn(ao, dy3, name="swa_dwo")
    dao = _mm_nt(dy3, w_o, name="swa_dao")
    dq, dkv, dbias, dsinks = _swa_bwd(q, kv, ao, lse, dao, bias, swa_sinks, name="swa_attn_bwd")
    drel_t = _bias_reduce(dbias.reshape(SW_Q_HEADS, W * 2 * W), jnp.asarray(onehot, jnp.bfloat16), name="swa_dbias")
    dw_q = _mm_tn(h2b, dq, name="swa_dwq")
    dw_kv = _mm_tn(h2b, dkv, name="swa_dwkv")
    dh2 = _mm_nt(dq, w_q, res=dy3, res_scale=ALPHA, name="swa_dh_q")
    (ex2,), tok2 = _send_start([sc(dw_o.reshape(N_DEV, D // N_DEV, D), dw_q.reshape(N_DEV, D // N_DEV, D),
                                   dw_kv.reshape(N_DEV, D // N_DEV, -1))], name="grads_start_2")
    dy2, dg_f0, db_f0 = _mm_nt_resident(dkv, w_kv, res=dh2, ln_bwd=(xh2, rs2, row(ln_ffn_g, 0)), behind=(tok2,),
                                        name="swa_dh_kv")
    dy1, dg_m0, db_m0, ex3, dcw0 = _ffn_bwd(dy2, h1b, u0, w_fin[0], w_fout[0], cw[0],
                                            (xh1, rs1, row(ln_mix_g, 0)), send_ffn("grads_start_3"), "l0")
    dw_hout = _mm_tn(og, dy1, name="hgrn_dwout")
    dog = _mm_nt(dy1, w_hout, name="hgrn_dog")
    dz, dlb, dgw = _hgrn_bwd(z, dog, states, lbl, hgrn_gnorm_w, name="hgrn_rec_bwd")
    dw_hin = _mm_tn(xb, dz, name="hgrn_dwin")

    p0 = _sigmoid(lbl[0:1] - lbl[1:2])
    dl0 = dlb * p0 * (1.0 - p0)
    d_lbl = dl0 * jnp.array([[1.0], [-1.0]], F32)
    dcw = jnp.stack([dcw0, dcw1], axis=0)
    d_conv_w = dcw[:, :, 0:3, :]
    d_conv_b = dcw[:, :, 3, :].reshape(DEPTH, N_DEV * fb)
    first_row = lax.broadcasted_iota(jnp.int32, (DEPTH, D), 0) == 0
    two_rows = lambda a, b: jnp.where(first_row, a, b)
    d_ln_mix_g = two_rows(dg_m0, dg_m1)
    d_ln_mix_b = two_rows(db_m0, db_m1)
    d_ln_ffn_g = two_rows(dg_f0, dg_f1)
    d_ln_ffn_b = two_rows(db_f0, db_f1)
    small_grads, small_layout = _pack_rows([d_lbl, d_conv_w, dgw, dsinks, drel_t.T, d_conv_b, d_ln_mix_g, d_ln_mix_b,
                                            d_ln_ffn_g, d_ln_ffn_b, loss_tile[0:1, 0:1]])

    (ex4,), tok4 = _send_start([sc(dw_hin.reshape(N_DEV, D, -1), dw_hout.reshape(N_DEV, D // N_DEV, D))
                                + [(small_grads, "gather")]], name="grads_start_4")
    dx = _mm_nt_resident(dz, w_hin, res=dy1, res_scale=ALPHA, name="hgrn_dx", behind=(tok4,))
    r_fin1, r_fout1 = _send_wait(ex1, (dx,), name="grads_wait_1")
    r_o, r_q, r_kv = _send_wait(ex2, (dx,), name="grads_wait_2")
    r_fin0, r_fout0 = _send_wait(ex3, (dx,), name="grads_wait_3")
    r_hin, r_hout, r_small = _send_wait(ex4, (dx,), name="grads_wait_4")
    received = [r_hin, r_hout, r_q, r_o, r_kv, r_fin0, r_fin1, r_fout0, r_fout1, r_small]

    outs = {}

    def put(name_, res):
        outs["grad_" + name_], outs["delta_" + name_], outs["new_m_" + name_], outs["new_v_" + name_] = res

    def big_update(name_, parts, w, m, v):
        shp = w.shape
        if w.ndim == 3 and shp[0] == 1:
            r = _adamw(parts, w[0], m[0], v[0], name="adamw_" + name_)
            put(name_, [a.reshape(shp) for a in r])
        else:
            r = _adamw(parts, w, m, v, name="adamw_" + name_)
            put(name_, r)

    big_update("hgrn_w_in", received[0], hgrn_w_in, m_hgrn_w_in, v_hgrn_w_in)
    big_update("hgrn_w_out", received[1], hgrn_w_out, m_hgrn_w_out, v_hgrn_w_out)
    big_update("swa_w_q", received[2], swa_w_q, m_swa_w_q, v_swa_w_q)
    big_update("swa_w_out", received[3], swa_w_out, m_swa_w_out, v_swa_w_out)
    big_update("shared_w_kv", received[4], shared_w_kv, m_shared_w_kv, v_shared_w_kv)
    for name_, idx, w, m, v in (("ffn_w_in", 5, ffn_w_in, m_ffn_w_in, v_ffn_w_in),
                                ("ffn_w_out", 7, ffn_w_out, m_ffn_w_out, v_ffn_w_out)):
        per_layer = [_adamw(received[idx + l], w, m, v, layer=l, name=f"adamw_{name_}_{l}") for l in range(DEPTH)]
        put(name_, [jnp.stack([per_layer[0][i], per_layer[1][i]], axis=0) for i in range(4)])

    small_sum = _sum_parts(received[9], name="sum_small_grads")
    (g_lbl, g_conv_w, g_gw, g_sinks, g_rel, g_conv_b, g_mix_g, g_mix_b, g_ffn_g, g_ffn_b,
     loss) = _unpack_rows(small_sum, small_layout)
    g_lbl_mine = lax.dynamic_slice_in_dim(g_lbl, me * (D // N_DEV), D // N_DEV, axis=1)
    g_conv_w_mine = lax.dynamic_index_in_dim(g_conv_w, me, axis=1, keepdims=False)
    small_names = ["hgrn_lb_logits", "ffn_conv_w", "hgrn_gnorm_w", "swa_sinks", "rel_bias", "ffn_conv_b",
                   "ln_mix_g", "ln_mix_b", "ln_ffn_g", "ln_ffn_b"]
    small_g = [g_lbl_mine, g_conv_w_mine, g_gw, g_sinks, g_rel, g_conv_b, g_mix_g, g_mix_b, g_ffn_g, g_ffn_b]
    small_w = [hgrn_lb_logits, ffn_conv_w, hgrn_gnorm_w, swa_sinks, rel_bias, ffn_conv_b, ln_mix_g, ln_mix_b,
               ln_ffn_g, ln_ffn_b]
    small_m = [m_hgrn_lb_logits, m_ffn_conv_w, m_hgrn_gnorm_w, m_swa_sinks, m_rel_bias, m_ffn_conv_b, m_ln_mix_g,
               m_ln_mix_b, m_ln_ffn_g, m_ln_ffn_b]
    small_v = [v_hgrn_lb_logits, v_ffn_conv_w, v_hgrn_gnorm_w, v_swa_sinks, v_rel_bias, v_ffn_conv_b, v_ln_mix_g,
               v_ln_mix_b, v_ln_ffn_g, v_ln_ffn_b]
    pg, lay = _pack_rows(small_g)
    pw, _ = _pack_rows(small_w)
    pm, _ = _pack_rows(small_m)
    pv, _ = _pack_rows(small_v)
    res = _adamw(pg[None], pw, pm, pv, name="adamw_small")
    unpacked = [_unpack_rows(r, lay) for r in res]
    for i, name_ in enumerate(small_names):
        put(name_, [unpacked[j][i] for j in range(4)])

    order = ["hgrn_w_in", "hgrn_lb_logits", "hgrn_gnorm_w", "hgrn_w_out", "swa_w_q", "swa_sinks", "swa_w_out",
             "shared_w_kv", "rel_bias", "ffn_w_in", "ffn_conv_w", "ffn_conv_b", "ffn_w_out", "ln_mix_g", "ln_mix_b",
             "ln_ffn_g", "ln_ffn_b"]
    result = [loss.reshape(()), dx]
    for kind in ("grad_", "delta_", "new_m_", "new_v_"):
        result += [outs[kind + n] for n in order]
    return tuple(result)
```

```python
import functools
import math

import numpy as np
import jax
import jax.numpy as jnp
from jax import lax
from jax.experimental import pallas as pl
from jax.experimental.pallas import tpu as pltpu

F32 = jnp.float32
MXU = jnp.bfloat16

N_DEV = 8
D_MODEL = 1024
DEPTH = 2
HG_HEADS = 8
HG_DIM = 128
HG_CHUNK = 64
HG_STEP_CHUNKS = 4
SW_Q_HEADS = 16
SW_KV_HEADS = 4
SW_GROUP = 4
SW_HEAD_DIM = 64
SW_WINDOW = 128
SW_STEP_BLOCKS = 2
REL_BUCKETS = 32
REL_MAX_DIST = 128
FFN_DIM = 2816
ALPHA = (2.0 * DEPTH) ** 0.25
LN_EPS = 1e-5
RMS_EPS = 1e-6
ADAM_LR = 0.001
ADAM_B1 = 0.9
ADAM_B2 = 0.999
ADAM_EPS = 1e-08
ADAM_WD = 0.01
ADAM_STEP = 10
EXP_CLAMP = 80.0
NEG_BIG = -1e30

SUBLANES = 8
LANES = 128
VMEM_LIMIT = 48 * 2 ** 20
TOKEN_TILE = 512
FFN_TOKEN_TILE = 1024
WIDE_TOKEN_TILE = 1024
RESIDENT_TOKEN_TILE = 256
REDUCE_TOKEN_TILE = 2048
GRAD_DTYPE = jnp.bfloat16


def _params(**kw):
    return pltpu.CompilerParams(vmem_limit_bytes=VMEM_LIMIT, **kw)


def _sigmoid(x):
    return 1.0 / (1.0 + jnp.exp(-x))


def _dot(a, b):
    return jnp.dot(a.astype(MXU), b.astype(MXU), preferred_element_type=F32)


def _dot_nt(a, b):
    return lax.dot_general(a.astype(MXU), b.astype(MXU), (((1,), (1,)), ((), ())), preferred_element_type=F32)


def _dot_tn(a, b):
    return lax.dot_general(a.astype(MXU), b.astype(MXU), (((0,), (0,)), ((), ())), preferred_element_type=F32)


def _trunc_bf16(x):
    bits = lax.bitcast_convert_type(x, jnp.int32)
    return lax.bitcast_convert_type(bits & jnp.int32(-65536), F32)


def _split3(x):
    hi = _trunc_bf16(x)
    r = x - hi
    mid = _trunc_bf16(r)
    lo = r - mid
    return hi.astype(jnp.bfloat16), mid.astype(jnp.bfloat16), lo.astype(jnp.bfloat16)


def _dot_hp(a, b, contract):
    def halves(x):
        hi = _trunc_bf16(x)
        return hi.astype(jnp.bfloat16), (x - hi).astype(jnp.bfloat16)

    ah, al = halves(a)
    bh, bl = halves(b)
    d = lambda p, q: lax.dot_general(p, q, (contract, ((), ())), preferred_element_type=F32)
    return d(ah, bh) + d(ah, bl) + d(al, bh)


def _exact_dot(m01, x):
    hi, mid, lo = _split3(x)
    d = lambda p: jnp.dot(m01, p, preferred_element_type=F32)
    return d(hi) + d(mid) + d(lo)


def _exact_dot_r(x, m01):
    hi, mid, lo = _split3(x)
    d = lambda p: jnp.dot(p, m01, preferred_element_type=F32)
    return d(hi) + d(mid) + d(lo)


def _mm_nn(a, w, *, name, res=None, res_scale=1.0, ln=None, out_dtype=F32, tm=None):
    nbk, T, kw = a.shape
    _, nbn, _, nw = w.shape
    tm = min(tm or TOKEN_TILE, T)
    has_res = res is not None
    res_ln = isinstance(res, tuple)
    n_res = (3 if res_ln else 1) if has_res else 0
    assert ln is None or nbn == 1

    def body(*refs):
        refs = list(refs)
        a_ref, w_ref = refs[:2]
        res_refs = refs[2:2 + n_res]
        pos = 2 + n_res
        if ln is not None:
            g_ref, b_ref = refs[pos:pos + 2]
            pos += 2
        o_ref = refs[pos]
        if ln is not None:
            xh_ref, rs_ref = refs[pos + 1:pos + 3]
        for n in range(nbn):
            y = _dot(a_ref[0], w_ref[0, n])
            for k in range(1, nbk):
                y = y + _dot(a_ref[k], w_ref[k, n])
            if res_ln:
                y = y + res_scale * (res_refs[0][n] * res_refs[1][...] + res_refs[2][...])
            elif has_res:
                y = y + res_scale * res_refs[0][n].astype(F32)
            if ln is None:
                o_ref[n] = y.astype(o_ref.dtype)
            else:
                mu = jnp.mean(y, axis=-1, keepdims=True)
                yc = y - mu
                var = jnp.mean(yc * yc, axis=-1, keepdims=True)
                rstd = lax.rsqrt(var + LN_EPS)
                xh = yc * rstd
                xh_ref[n] = xh
                rs_ref[...] = rstd
                o_ref[n] = (xh * g_ref[...] + b_ref[...]).astype(o_ref.dtype)

    vec = pl.BlockSpec((1, nw), lambda i: (0, 0))
    in_specs = [pl.BlockSpec((nbk, tm, kw), lambda i: (0, i, 0)),
                pl.BlockSpec((nbk, nbn, kw, nw), lambda i: (0, 0, 0, 0))]
    args = [a, w]
    if has_res:
        in_specs.append(pl.BlockSpec((nbn, tm, nw), lambda i: (0, i, 0)))
        if res_ln:
            in_specs += [vec, vec]
            args += list(res)
        else:
            args.append(res)
    if ln is not None:
        in_specs += [vec, vec]
        args += list(ln)
    out_spec = pl.BlockSpec((nbn, tm, nw), lambda i: (0, i, 0))
    out_shape = jax.ShapeDtypeStruct((nbn, T, nw), out_dtype)
    if ln is not None:
        out_specs = [out_spec, out_spec, pl.BlockSpec((tm, 1), lambda i: (i, 0))]
        out_shape = [jax.ShapeDtypeStruct((nbn, T, nw), MXU), jax.ShapeDtypeStruct((nbn, T, nw), F32),
                     jax.ShapeDtypeStruct((T, 1), F32)]
    else:
        out_specs = out_spec
    return pl.pallas_call(body, name=name, grid=(T // tm,), in_specs=in_specs, out_specs=out_specs,
                          out_shape=out_shape, compiler_params=_params())(*args)


def _same(n):
    return n


def _mm_nt(dy, w, *, name, res=None, res_scale=1.0, out_dtype=F32, tm=None, n_map=_same, behind=()):
    nbn, T, nw = dy.shape
    nbk, _, kw, _ = w.shape
    tm = min(tm or WIDE_TOKEN_TILE, T)
    has_res = res is not None

    def body(*refs):
        refs = list(refs)
        dy_ref, w_ref = refs[:2]
        pos = 2
        res_ref = None
        if has_res:
            res_ref = refs[pos]
            pos += 1
        pos += len(behind)
        o_ref = refs[pos]
        pos += 1
        acc_ref = refs[pos] if nbn > 1 else None
        n = pl.program_id(2)
        part = _dot_nt(dy_ref[...], w_ref[...])

        def finish(acc):
            y = acc
            if has_res:
                y = y + res_scale * res_ref[...].astype(F32)
            o_ref[...] = y.astype(o_ref.dtype)

        if nbn == 1:
            finish(part)
        else:
            @pl.when(n == 0)
            def _():
                acc_ref[...] = part

            @pl.when(n > 0)
            def _():
                acc_ref[...] += part

            @pl.when(n == nbn - 1)
            def _():
                finish(acc_ref[...])

    in_specs = [pl.BlockSpec((None, tm, nw), lambda i, k, n: (n, i, 0)),
                pl.BlockSpec((None, None, kw, nw), lambda i, k, n: (k, n_map(n), 0, 0))]
    args = [dy, w]
    if has_res:
        in_specs.append(pl.BlockSpec((None, tm, kw), lambda i, k, n: (k, i, 0)))
        args.append(res)
    in_specs += [pl.BlockSpec(memory_space=pl.ANY)] * len(behind)
    args += list(behind)
    scratch = [pltpu.VMEM((tm, kw), F32)] if nbn > 1 else []
    return pl.pallas_call(body, name=name, grid=(T // tm, nbk, nbn), in_specs=in_specs,
                          out_specs=pl.BlockSpec((None, tm, kw), lambda i, k, n: (k, i, 0)),
                          out_shape=jax.ShapeDtypeStruct((nbk, T, kw), out_dtype), scratch_shapes=scratch,
                          compiler_params=_params())(*args)


def _mm_nt_resident(dy, w, *, name, res=None, res_scale=1.0, ln_bwd=None, tm=None, n_map=_same, behind=()):
    nbn, T, nw = dy.shape
    nbk, _, kw, _ = w.shape
    assert nbk == 1
    tm = min(tm or RESIDENT_TOKEN_TILE, T)
    has_res = res is not None
    n_in = 2 + has_res + (3 if ln_bwd else 0) + len(behind)

    def body(*refs):
        dy_ref, w_ref = refs[:2]
        res_ref = refs[2] if has_res else None
        y = _dot_nt(dy_ref[0], w_ref[0, n_map(0)])
        for n in range(1, nbn):
            y = y + _dot_nt(dy_ref[n], w_ref[0, n_map(n)])
        if has_res:
            y = y + res_scale * res_ref[0].astype(F32)
        if ln_bwd is None:
            refs[n_in][0] = y
        else:
            xh_ref, rs_ref, g_ref = refs[2 + has_res:5 + has_res]
            o_ref, dg_ref, db_ref = refs[n_in:n_in + 3]
            out, dg, db = _ln_bwd_rows(y, xh_ref[0], rs_ref[...], g_ref[...])
            o_ref[0] = out
            _accumulate(pl.program_id(0), (dg_ref, db_ref), (dg, db))

    tok = pl.BlockSpec((1, tm, kw), lambda i: (0, i, 0))
    vec = pl.BlockSpec((1, kw), lambda i: (0, 0))
    in_specs = [pl.BlockSpec((nbn, tm, nw), lambda i: (0, i, 0)),
                pl.BlockSpec(w.shape, lambda i: (0, 0, 0, 0))]
    args = [dy, w]
    if has_res:
        in_specs.append(tok)
        args.append(res)
    out_specs, out_shape = tok, jax.ShapeDtypeStruct((1, T, kw), F32)
    if ln_bwd is not None:
        in_specs += [tok, pl.BlockSpec((tm, 1), lambda i: (i, 0)), vec]
        args += list(ln_bwd)
        out_specs = [tok, vec, vec]
        out_shape = [out_shape, jax.ShapeDtypeStruct((1, kw), F32), jax.ShapeDtypeStruct((1, kw), F32)]
    in_specs += [pl.BlockSpec(memory_space=pl.ANY)] * len(behind)
    args += list(behind)
    return pl.pallas_call(body, name=name, grid=(T // tm,), in_specs=in_specs, out_specs=out_specs,
                          out_shape=out_shape, compiler_params=_params())(*args)


def _mm_tn(a, dy, *, name, tm=None, n_map=_same):
    nbk, T, kw = a.shape
    nbn, _, nw = dy.shape
    tm = min(tm or REDUCE_TOKEN_TILE, T)
    nt = T // tm

    def body(a_ref, dy_ref, o_ref, acc_ref):
        i = pl.program_id(2)
        part = _dot_tn(a_ref[...], dy_ref[...])

        @pl.when(i == 0)
        def _():
            acc_ref[...] = part

        @pl.when(i > 0)
        def _():
            acc_ref[...] += part

        @pl.when(i == nt - 1)
        def _():
            o_ref[...] = acc_ref[...].astype(o_ref.dtype)

    return pl.pallas_call(body, name=name, grid=(nbk, nbn, nt),
                          in_specs=[pl.BlockSpec((None, tm, kw), lambda k, n, i: (k, i, 0)),
                                    pl.BlockSpec((None, tm, nw), lambda k, n, i: (n, i, 0))],
                          out_specs=pl.BlockSpec((None, None, kw, nw), lambda k, n, i: (k, n_map(n), 0, 0)),
                          out_shape=jax.ShapeDtypeStruct((nbk, nbn, kw, nw), GRAD_DTYPE),
                          scratch_shapes=[pltpu.VMEM((kw, nw), F32)],
                          compiler_params=_params())(a, dy)


def _ln_bwd_rows(dh, xh, rstd, g):
    dxh = dh * g
    m1 = jnp.mean(dxh, axis=-1, keepdims=True)
    m2 = jnp.mean(dxh * xh, axis=-1, keepdims=True)
    dy = rstd * (dxh - m1 - xh * m2)
    return dy, jnp.sum(dh * xh, axis=0, keepdims=True), jnp.sum(dh, axis=0, keepdims=True)


def _accumulate(i, refs, parts):
    @pl.when(i == 0)
    def _():
        for r, p in zip(refs, parts):
            r[...] = jnp.zeros_like(r) + p

    @pl.when(i > 0)
    def _():
        for r, p in zip(refs, parts):
            r[...] += p


def _loss_ln_bwd(tgt, xhat, rstd, g, b, *, name, tm=None):
    _, T, D = xhat.shape
    tm = min(tm or TOKEN_TILE, T)

    def body(t_ref, xh_ref, rs_ref, g_ref, b_ref, dy_ref, dg_ref, db_ref, loss_ref):
        i = pl.program_id(0)
        xh = xh_ref[...]
        err = xh * g_ref[...] + b_ref[...] - t_ref[...]
        part = 0.5 * jnp.sum(jnp.mean(err * err, axis=-1, keepdims=True), axis=0, keepdims=True)
        dy, dg, db = _ln_bwd_rows(err / D, xh, rs_ref[...], g_ref[...])
        dy_ref[...] = dy
        _accumulate(i, (dg_ref, db_ref, loss_ref), (dg, db, part))

    tok = pl.BlockSpec((None, tm, D), lambda i: (0, i, 0))
    vec = pl.BlockSpec((1, D), lambda i: (0, 0))
    return pl.pallas_call(body, name=name, grid=(T // tm,),
                          in_specs=[tok, tok, pl.BlockSpec((tm, 1), lambda i: (i, 0)), vec, vec],
                          out_specs=[tok, vec, vec, pl.BlockSpec((SUBLANES, LANES), lambda i: (0, 0))],
                          out_shape=[jax.ShapeDtypeStruct((1, T, D), F32), jax.ShapeDtypeStruct((1, D), F32),
                                     jax.ShapeDtypeStruct((1, D), F32), jax.ShapeDtypeStruct((SUBLANES, LANES), F32)],
                          compiler_params=_params())(tgt, xhat, rstd, g, b)


def _shift_rows(ext, k, n, halo):
    if k == 0:
        return ext[halo:halo + n]
    return pltpu.roll(ext, k, axis=0)[halo:halo + n]


def _conv_rows(ext, cw_ref, n, halo):
    return (cw_ref[0:1, :] * _shift_rows(ext, 2, n, halo) + cw_ref[1:2, :] * _shift_rows(ext, 1, n, halo)
            + cw_ref[2:3, :] * ext[halo:halo + n] + cw_ref[3:4, :])


def _pair_map(n):
    return n // 2 + 4 * (n % 2)


def _ffn_up(hb, w_in, cw, *, name, tm=None):
    _, T, D = hb.shape
    _, nb, _, fb = w_in.shape
    half = nb // 2
    tm = min(tm or FFN_TOKEN_TILE, T)

    def body(h_ref, wa_ref, wb_ref, cwa_ref, cwb_ref, u_ref, ab_ref, act_ref, carry):
        @pl.when(pl.program_id(1) == 0)
        def _():
            carry[...] = jnp.zeros_like(carry)

        h = h_ref[...]
        conv = []
        for s, (w_ref, cw_ref) in enumerate(((wa_ref, cwa_ref), (wb_ref, cwb_ref))):
            uf = _dot(h, w_ref[...])
            u_ref[s] = uf.astype(u_ref.dtype)
            ext = jnp.concatenate([carry[s], uf], axis=0)
            c = _conv_rows(ext, cw_ref, tm, SUBLANES)
            ab_ref[s] = c.astype(ab_ref.dtype)
            conv.append(c)
            carry[s] = uf[tm - SUBLANES:tm]
        a, b = conv
        act_ref[...] = (a * _sigmoid(a) * b).astype(act_ref.dtype)

    wspec = lambda off: pl.BlockSpec((None, None, D, fb), lambda p, i: (0, p + off, 0, 0))
    cws = lambda off: pl.BlockSpec((None, SUBLANES, fb), lambda p, i: (p + off, 0, 0))
    return pl.pallas_call(body, name=name, grid=(half, T // tm),
                          in_specs=[pl.BlockSpec((None, tm, D), lambda p, i: (0, i, 0)), wspec(0), wspec(half),
                                    cws(0), cws(half)],
                          out_specs=[pl.BlockSpec((None, 2, tm, fb), lambda p, i: (p, 0, i, 0)),
                                     pl.BlockSpec((None, 2, tm, fb), lambda p, i: (p, 0, i, 0)),
                                     pl.BlockSpec((None, tm, fb), lambda p, i: (p, i, 0))],
                          out_shape=[jax.ShapeDtypeStruct((half, 2, T, fb), MXU),
                                     jax.ShapeDtypeStruct((half, 2, T, fb), MXU),
                                     jax.ShapeDtypeStruct((half, T, fb), MXU)],
                          scratch_shapes=[pltpu.VMEM((2, SUBLANES, fb), F32)],
                          compiler_params=_params())(hb, w_in, w_in, cw, cw)


def _ffn_gate_bwd(dy, u, ab, w_out, cw, *, name, tm=None):
    _, T, D = dy.shape
    half, _, _, fb = u.shape
    tm = min(tm or FFN_TOKEN_TILE, T)
    nt = T // tm

    n_full = fb // LANES
    tail = slice(n_full * LANES, fb)

    def body(dy_ref, u_ref, ab_ref, w_ref, cwa_ref, cwb_ref, du_ref, dwo_ref, dcw_ref, carry, acc, gacc):
        i = pl.program_id(1)

        @pl.when(i == 0)
        def _():
            carry[...] = jnp.zeros_like(carry)
            acc[...] = jnp.zeros_like(acc)
            gacc[...] = jnp.zeros_like(gacc)
            dcw_ref[...] = jnp.zeros_like(dcw_ref)

        dyv = dy_ref[...]
        dact = _dot_nt(dyv, w_ref[...])
        a = ab_ref[0].astype(F32)
        b = ab_ref[1].astype(F32)
        sa = _sigmoid(a)
        silu = a * sa
        acc[...] += _dot_tn(silu * b, dyv)
        dcs = (dact * b * (sa * (1.0 + a * (1.0 - sa))), dact * silu)
        m = tm + SUBLANES
        rows = lax.broadcasted_iota(jnp.int32, (SUBLANES, fb), 0)
        for s, cw_ref in enumerate((cwa_ref, cwb_ref)):
            dc = dcs[s]
            nxt = jnp.concatenate([dc, carry[s]], axis=0)
            dc1 = pltpu.roll(nxt, m - 1, axis=0)[:tm]
            dc2 = pltpu.roll(nxt, m - 2, axis=0)[:tm]
            du_ref[s] = (cw_ref[2:3, :] * dc + cw_ref[1:2, :] * dc1 + cw_ref[0:1, :] * dc2).astype(du_ref.dtype)
            carry[s] = dc[0:SUBLANES]
            dcb = [x.astype(MXU) for x in (dc, dc1, dc2)]
            for j in range(n_full):
                blk = slice(j * LANES, (j + 1) * LANES)
                gacc[s, j] += _dot_tn(u_ref[s, :, blk], jnp.concatenate([x[:, blk] for x in dcb], axis=1))
            dcw_ref[s] += jnp.where(rows == 3, jnp.sum(dc, axis=0, keepdims=True), 0.0)
            if fb > n_full * LANES:
                ut = u_ref[s, :, tail].astype(F32)
                gt = [jnp.sum(x[:, tail] * ut, axis=0, keepdims=True) for x in (dc2, dc1, dc)]
                rt = rows[:, tail]
                dcw_ref[s, :, tail] += jnp.where(rt == 0, gt[0], jnp.where(rt == 1, gt[1], jnp.where(rt == 2, gt[2], 0.0)))

        @pl.when(i == nt - 1)
        def _():
            dwo_ref[...] = acc[...].astype(dwo_ref.dtype)
            eye = _tri(LANES, True) & _tri(LANES, False)
            for s in range(2):
                for j in range(n_full):
                    g = gacc[s, j]
                    for tap in range(3):
                        d = jnp.where(eye, g[:, (2 - tap) * LANES:(3 - tap) * LANES], 0.0)
                        dcw_ref[s, tap:tap + 1, j * LANES:(j + 1) * LANES] = jnp.sum(d, axis=0, keepdims=True)

    rev = lambda i: nt - 1 - i
    cws = lambda off: pl.BlockSpec((None, SUBLANES, fb), lambda p, i: (p + off, 0, 0))
    pair = lambda: pl.BlockSpec((None, 2, tm, fb), lambda p, i: (p, 0, rev(i), 0))
    return pl.pallas_call(body, name=name, grid=(half, nt),
                          in_specs=[pl.BlockSpec((None, tm, D), lambda p, i: (0, rev(i), 0)), pair(), pair(),
                                    pl.BlockSpec((None, None, fb, D), lambda p, i: (p, 0, 0, 0)), cws(0), cws(half)],
                          out_specs=[pair(),
                                     pl.BlockSpec((None, None, fb, D), lambda p, i: (p, 0, 0, 0)),
                                     pl.BlockSpec((None, 2, SUBLANES, fb), lambda p, i: (p, 0, 0, 0))],
                          out_shape=[jax.ShapeDtypeStruct((half, 2, T, fb), MXU),
                                     jax.ShapeDtypeStruct((half, 1, fb, D), GRAD_DTYPE),
                                     jax.ShapeDtypeStruct((half, 2, SUBLANES, fb), F32)],
                          scratch_shapes=[pltpu.VMEM((2, SUBLANES, fb), F32), pltpu.VMEM((fb, D), F32),
                                          pltpu.VMEM((2, n_full, LANES, 3 * LANES), F32)],
                          compiler_params=_params())(dy, u, ab, w_out, cw, cw)


def _tri(n, lower):
    r = lax.broadcasted_iota(jnp.int32, (n, n), 0)
    c = lax.broadcasted_iota(jnp.int32, (n, n), 1)
    return (r >= c) if lower else (r <= c)


def _hgrn_gates(zq, zf, lb):
    sq = _sigmoid(zq)
    sf = _sigmoid(zf)
    fg = lb + (1.0 - lb) * sf
    return zq * sq, sq, sf, fg, jnp.log(fg)


def _lb_of(lbl_ref, cols):
    return _sigmoid(lbl_ref[0:1, cols] - lbl_ref[1:2, cols])


def _ones_where(mask):
    return jnp.where(mask, 1.0, 0.0).astype(jnp.bfloat16)


def _hgrn_fwd(z, lbl, gw, *, name):
    _, T, zw = z.shape
    C = min(HG_CHUNK, T)
    nch = T // C
    S = HG_STEP_CHUNKS if nch % HG_STEP_CHUNKS == 0 else 1
    hpb = zw // HG_DIM

    def body(z_ref, lbl_ref, gw_ref, og_ref, st_ref, s_scr, bc_scr, q_scr, k_scr):
        c = pl.program_id(0)

        @pl.when(c == 0)
        def _():
            s_scr[...] = jnp.zeros_like(s_scr)

        low = _tri(C, True)
        low01 = _ones_where(low)
        gwv = gw_ref[...]
        H = range(HG_HEADS)
        col = lambda h: slice(h * HG_DIM, (h + 1) * HG_DIM)
        for sub in range(S):
            chunk_body(z_ref, lbl_ref, og_ref, st_ref, s_scr, bc_scr, q_scr, k_scr,
                       sub, slice(sub * C, (sub + 1) * C), low, low01, gwv, H, col)

    def chunk_body(z_ref, lbl_ref, og_ref, st_ref, s_scr, bc_scr, q_scr, k_scr, sub, rows, low, low01, gwv, H, col):
        for blk in range(2):
            cols = slice(blk * zw, (blk + 1) * zw)
            qq, _, _, fg, lf = _hgrn_gates(z_ref[blk, rows], z_ref[2 + blk, rows], _lb_of(lbl_ref, cols))
            q_scr[:, cols] = qq
            k_scr[:, cols] = 1.0 - fg
            bc_scr[:, cols] = _exact_dot(low01, lf)
        zcol = lambda part, h: (part + h // hpb, rows, slice((h % hpb) * HG_DIM, (h % hpb + 1) * HG_DIM))
        b = [bc_scr[:, col(h)] for h in H]
        bm = [bc_scr[C // 2 - 1:C // 2, col(h)] for h in H]
        bl = [bc_scr[C - 1:C, col(h)] for h in H]
        q_ = [q_scr[:, col(h)] for h in H]
        k_ = [k_scr[:, col(h)] for h in H]
        v_ = [z_ref[zcol(4, h)] for h in H]
        qt = [q_[h] * jnp.exp(jnp.minimum(b[h] - bm[h], EXP_CLAMP)) for h in H]
        kt = [k_[h] * jnp.exp(jnp.minimum(bm[h] - b[h], EXP_CLAMP)) for h in H]
        A = [jnp.where(low, _dot_nt(qt[h], kt[h]), 0.0) for h in H]
        for h in H:
            st_ref[sub, h] = s_scr[h]
        o = [_dot_nt(q_[h] * jnp.exp(b[h]), s_scr[h]) + _dot(A[h], v_[h]) for h in H]
        for h in H:
            s_scr[h] = s_scr[h] * jnp.exp(bl[h]) + _dot_tn(v_[h], k_[h] * jnp.exp(bl[h] - b[h]))
        for h in H:
            g_h = z_ref[zcol(6, h)]
            r = lax.rsqrt(jnp.mean(o[h] * o[h], axis=-1, keepdims=True) + RMS_EPS)
            og_ref[rows, col(h)] = (o[h] * r * gwv * (g_h * _sigmoid(g_h))).astype(og_ref.dtype)

    return pl.pallas_call(body, name=name, grid=(nch // S,),
                          in_specs=[pl.BlockSpec((8, S * C, zw), lambda c: (0, c, 0)),
                                    pl.BlockSpec((2, D_MODEL), lambda c: (0, 0)),
                                    pl.BlockSpec((1, HG_DIM), lambda c: (0, 0))],
                          out_specs=[pl.BlockSpec((None, S * C, D_MODEL), lambda c: (0, c, 0)),
                                     pl.BlockSpec((S, HG_HEADS, HG_DIM, HG_DIM), lambda c: (c, 0, 0, 0))],
                          out_shape=[jax.ShapeDtypeStruct((1, T, D_MODEL), MXU),
                                     jax.ShapeDtypeStruct((nch, HG_HEADS, HG_DIM, HG_DIM), F32)],
                          scratch_shapes=[pltpu.VMEM((HG_HEADS, HG_DIM, HG_DIM), F32)]
                          + [pltpu.VMEM((C, D_MODEL), F32)] * 3,
                          compiler_params=_params())(z, lbl, gw)


def _hgrn_bwd(z, dog, states, lbl, gw, *, name):
    _, T, zw = z.shape
    C = min(HG_CHUNK, T)
    nch = T // C
    S = HG_STEP_CHUNKS if nch % HG_STEP_CHUNKS == 0 else 1
    hpb = zw // HG_DIM

    def body(z_ref, dog_ref, st0_ref, st1_ref, lbl_ref, gw_ref, dz_ref, dlb_ref, dgw_ref,
             d_scr, bc_scr, q_scr, sf_scr, fg_scr, x_scr):
        step = pl.program_id(0)

        @pl.when(step == 0)
        def _():
            d_scr[...] = jnp.zeros_like(d_scr)
            dlb_ref[...] = jnp.zeros_like(dlb_ref)
            dgw_ref[...] = jnp.zeros_like(dgw_ref)

        low = _tri(C, True)
        low01 = _ones_where(low)
        up01 = _ones_where(_tri(C, False))
        gwv = gw_ref[...]
        H = range(HG_HEADS)
        col = lambda h: slice(h * HG_DIM, (h + 1) * HG_DIM)
        for sub in reversed(range(S)):
            chunk_body(z_ref, dog_ref, st0_ref, st1_ref, lbl_ref, dz_ref, dlb_ref, dgw_ref,
                       d_scr, bc_scr, q_scr, sf_scr, fg_scr, x_scr,
                       sub, slice(sub * C, (sub + 1) * C), low, low01, up01, gwv, H, col)

    def chunk_body(z_ref, dog_ref, st0_ref, st1_ref, lbl_ref, dz_ref, dlb_ref, dgw_ref,
                   d_scr, bc_scr, q_scr, sf_scr, fg_scr, x_scr, sub, rows, low, low01, up01, gwv, H, col):
        st0 = lambda h: st0_ref[sub, h]
        st1 = (lambda h: st0_ref[sub + 1, h]) if sub + 1 < S else (lambda h: st1_ref[h])
        for blk in range(2):
            lbb = _lb_of(lbl_ref, slice(blk * zw, (blk + 1) * zw))
            qq, sq, sf, fg, lf = _hgrn_gates(z_ref[blk, rows], z_ref[2 + blk, rows], lbb)
            q_scr[:, blk * zw:(blk + 1) * zw] = qq
            sf_scr[:, blk * zw:(blk + 1) * zw] = sf
            fg_scr[:, blk * zw:(blk + 1) * zw] = fg
            bc_scr[:, blk * zw:(blk + 1) * zw] = _exact_dot(low01, lf)
        zcol = lambda part, h: (part + h // hpb, rows, slice((h % hpb) * HG_DIM, (h % hpb + 1) * HG_DIM))
        b = [bc_scr[:, col(h)] for h in H]
        bm = [bc_scr[C // 2 - 1:C // 2, col(h)] for h in H]
        bl = [bc_scr[C - 1:C, col(h)] for h in H]
        q_ = [q_scr[:, col(h)] for h in H]
        k_ = [1.0 - fg_scr[:, col(h)] for h in H]
        v_ = [z_ref[zcol(4, h)] for h in H]
        eq = [jnp.exp(jnp.minimum(b[h] - bm[h], EXP_CLAMP)) for h in H]
        ek = [jnp.exp(jnp.minimum(bm[h] - b[h], EXP_CLAMP)) for h in H]
        eb = [jnp.exp(b[h]) for h in H]
        el = [jnp.exp(bl[h] - b[h]) for h in H]
        qt = [q_[h] * eq[h] for h in H]
        kt = [k_[h] * ek[h] for h in H]
        q0 = [q_[h] * eb[h] for h in H]
        kd = [k_[h] * el[h] for h in H]
        A = [jnp.where(low, _dot_nt(qt[h], kt[h]), 0.0) for h in H]
        o = [_dot_nt(q0[h], st0(h)) + _dot(A[h], v_[h]) for h in H]
        do = []
        dgw_acc = jnp.zeros((1, HG_DIM), F32)
        for h in H:
            g_h = z_ref[zcol(6, h)]
            r = lax.rsqrt(jnp.mean(o[h] * o[h], axis=-1, keepdims=True) + RMS_EPS)
            on = o[h] * r
            sg = _sigmoid(g_h)
            dogh = dog_ref[rows, col(h)].astype(F32)
            t1 = dogh * on
            dgw_acc = dgw_acc + jnp.sum(t1 * (g_h * sg), axis=0, keepdims=True)
            dz_ref[zcol(6, h)] = (t1 * gwv * (sg * (1.0 + g_h * (1.0 - sg)))).astype(dz_ref.dtype)
            don = dogh * gwv * (g_h * sg)
            do.append(r * (don - on * jnp.mean(don * on, axis=-1, keepdims=True)))
        dgw_ref[...] += dgw_acc
        P = [jnp.where(low, _dot_nt(do[h], v_[h]), 0.0) for h in H]
        dqq = [eb[h] * _dot(do[h], st0(h)) + eq[h] * _dot_hp(P[h], kt[h], ((1,), (0,))) for h in H]
        dkk = [el[h] * _dot(v_[h], d_scr[h]) + ek[h] * _dot_hp(P[h], qt[h], ((0,), (0,))) for h in H]
        for h in H:
            dz_ref[zcol(4, h)] = (_dot_nt(kd[h], d_scr[h]) + _dot_tn(A[h], do[h])).astype(dz_ref.dtype)
            x_scr[:, col(h)] = q_[h] * dqq[h] - k_[h] * dkk[h]
        edge = [jnp.sum(d_scr[h] * st1(h), axis=0, keepdims=True) for h in H]
        for h in H:
            d_scr[h] = d_scr[h] * jnp.exp(bl[h]) + _dot_tn(do[h], q0[h])
        for blk in range(2):
            x_scr[:, blk * zw:(blk + 1) * zw] = _exact_dot(up01, x_scr[:, blk * zw:(blk + 1) * zw])
        dlb = []
        for h in H:
            dfg = (x_scr[:, col(h)] + edge[h]) / fg_scr[:, col(h)] - dkk[h]
            sf_h = sf_scr[:, col(h)]
            lb_h = _lb_of(lbl_ref, col(h))
            zq_h = z_ref[zcol(0, h)]
            sq_h = _sigmoid(zq_h)
            dlb.append(jnp.sum(dfg * (1.0 - sf_h), axis=0, keepdims=True))
            dz_ref[zcol(0, h)] = (dqq[h] * (sq_h * (1.0 + zq_h * (1.0 - sq_h)))).astype(dz_ref.dtype)
            dz_ref[zcol(2, h)] = (dfg * (1.0 - lb_h) * sf_h * (1.0 - sf_h)).astype(dz_ref.dtype)
        dlb_ref[...] += jnp.concatenate(dlb, axis=1)

    nst = nch // S
    rev = lambda s: nst - 1 - s
    return pl.pallas_call(body, name=name, grid=(nst,),
                          in_specs=[pl.BlockSpec((8, S * C, zw), lambda s: (0, rev(s), 0)),
                                    pl.BlockSpec((None, S * C, D_MODEL), lambda s: (0, rev(s), 0)),
                                    pl.BlockSpec((S, HG_HEADS, HG_DIM, HG_DIM), lambda s: (rev(s), 0, 0, 0)),
                                    pl.BlockSpec((None, HG_HEADS, HG_DIM, HG_DIM),
                                                 lambda s: (jnp.minimum((rev(s) + 1) * S, nch - 1), 0, 0, 0)),
                                    pl.BlockSpec((2, D_MODEL), lambda s: (0, 0)),
                                    pl.BlockSpec((1, HG_DIM), lambda s: (0, 0))],
                          out_specs=[pl.BlockSpec((8, S * C, zw), lambda s: (0, rev(s), 0)),
                                     pl.BlockSpec((1, D_MODEL), lambda s: (0, 0)),
                                     pl.BlockSpec((1, HG_DIM), lambda s: (0, 0))],
                          out_shape=[jax.ShapeDtypeStruct((8, T, zw), MXU), jax.ShapeDtypeStruct((1, D_MODEL), F32),
                                     jax.ShapeDtypeStruct((1, HG_DIM), F32)],
                          scratch_shapes=[pltpu.VMEM((HG_HEADS, HG_DIM, HG_DIM), F32)]
                          + [pltpu.VMEM((C, D_MODEL), F32)] * 5,
                          compiler_params=_params())(z, dog, states, states, lbl, gw)


def _bucket_onehot():
    W = SW_WINDOW
    t = np.arange(W)[:, None] + W
    s = np.arange(2 * W)[None, :]
    dist = t - s
    exact = REL_BUCKETS // 2
    d = np.maximum(np.maximum(dist, 0), 1).astype(np.float32)
    log_b = exact + (np.log(d / np.float32(exact)) / np.float32(math.log(REL_MAX_DIST / exact))
                     * np.float32(REL_BUCKETS - exact)).astype(np.int32)
    bucket = np.where(np.maximum(dist, 0) < exact, np.maximum(dist, 0), np.minimum(log_b, REL_BUCKETS - 1))
    valid = (dist >= 0) & (dist < W)
    onehot = (bucket[..., None] == np.arange(REL_BUCKETS)) & valid[..., None]
    return onehot.reshape(W * 2 * W, REL_BUCKETS).astype(np.float32)


def _bias_expand(rel_t, onehot_t, *, name):
    hq, nbk = rel_t.shape
    n = onehot_t.shape[1]

    def body(r_ref, oh_ref, o_ref):
        o_ref[...] = _exact_dot_r(r_ref[...], oh_ref[...])

    return pl.pallas_call(body, name=name, out_shape=jax.ShapeDtypeStruct((hq, n), F32),
                          compiler_params=_params())(rel_t, onehot_t)


def _bias_reduce(dbias, onehot, *, name):
    hq = dbias.shape[0]
    nbk = onehot.shape[1]

    def body(d_ref, oh_ref, o_ref):
        o_ref[...] = _exact_dot_r(d_ref[...], oh_ref[...])

    return pl.pallas_call(body, name=name, out_shape=jax.ShapeDtypeStruct((hq, nbk), F32),
                          compiler_params=_params())(dbias, onehot)


def _swa_mask(j):
    W = SW_WINDOW
    t = lax.broadcasted_iota(jnp.int32, (W, 2 * W), 0) + W
    s = lax.broadcasted_iota(jnp.int32, (W, 2 * W), 1)
    dist = t - s
    band = (dist >= 0) & (dist < W)
    m = band & ((j > 0) | (s >= W))
    return jnp.concatenate([m] * SW_GROUP, axis=0)


def _half_mask(rows, half):
    lane = lax.broadcasted_iota(jnp.int32, (rows, LANES), 1)
    return (lane >= SW_HEAD_DIM) if half else (lane < SW_HEAD_DIM)


_ALL = slice(None)


def _swa_head(ref, col0, head, to_half, rows=_ALL):
    slab, half = head // 2, head % 2
    x = ref[rows, col0 + slab * LANES:col0 + (slab + 1) * LANES]
    x = jnp.where(_half_mask(x.shape[0], half), x, 0.0)
    return x if half == to_half else pltpu.roll(x, SW_HEAD_DIM, axis=1)


def _swa_stack(ref, g, rows=_ALL):
    return jnp.concatenate([_swa_head(ref, 0, g * SW_GROUP + r, g % 2, rows) for r in range(SW_GROUP)], axis=0)


def _swa_unstack(ref, x, g, rows=_ALL):
    W = SW_WINDOW
    for pair in range(SW_GROUP // 2):
        parts = []
        for r in (2 * pair, 2 * pair + 1):
            piece = x[r * W:(r + 1) * W]
            parts.append(piece if r % 2 == g % 2 else pltpu.roll(piece, SW_HEAD_DIM, axis=1))
        slab = (g * SW_GROUP) // 2 + pair
        ref[rows, slab * LANES:(slab + 1) * LANES] = (parts[0] + parts[1]).astype(ref.dtype)


def _swa_kv(kp_ref, kc_ref, col0, g, prev_rows=_ALL, rows=_ALL):
    return jnp.concatenate([_swa_head(kp_ref, col0, g, g % 2, prev_rows), _swa_head(kc_ref, col0, g, g % 2, rows)],
                           axis=0)


def _lane_pick(tile, h):
    lane = lax.broadcasted_iota(jnp.int32, tile.shape, 1)
    return jnp.sum(jnp.where(lane == h, tile, 0.0), axis=-1, keepdims=True)


def _lane_put(tile, h, col):
    lane = lax.broadcasted_iota(jnp.int32, tile.shape, 1)
    return jnp.where(lane == h, col, tile)


def _swa_rows(vals):
    return jnp.concatenate([jnp.broadcast_to(v, (SW_WINDOW, 1)) for v in vals], axis=0)


def _swa_fwd(q, kv, bias, sinks, *, name):
    _, T, D = q.shape
    W = SW_WINDOW
    nb = T // W
    dh = SW_HEAD_DIM
    kvw = SW_KV_HEADS * dh
    scale = dh ** -0.5

    S = SW_STEP_BLOCKS if nb % SW_STEP_BLOCKS == 0 else 1

    def body(q_ref, kc_ref, kp_ref, bias_ref, sink_ref, o_ref, lse_ref):
        c = pl.program_id(0)
        sk = sink_ref[...]
        rows = [slice(s * W, (s + 1) * W) for s in range(S)]
        before = [(kp_ref, _ALL)] + [(kc_ref, rows[s - 1]) for s in range(1, S)]
        masks = [_swa_mask(c * S + s) for s in range(S)]
        I = [(s, g) for s in range(S) for g in range(SW_KV_HEADS)]
        kv_of = lambda col0, s, g: _swa_kv(before[s][0], kc_ref, col0, g, before[s][1], rows[s])
        kk = [kv_of(0, s, g) for s, g in I]
        vv = [kv_of(kvw, s, g) for s, g in I]
        qs = [_swa_stack(q_ref, g, rows[s]) for s, g in I]
        bias_of = lambda g: bias_ref[g * SW_GROUP:(g + 1) * SW_GROUP].reshape(SW_GROUP * W, 2 * W)
        logits = [jnp.where(masks[s], _dot_nt(qs[n], kk[n]) * scale + bias_of(g), NEG_BIG) for n, (s, g) in enumerate(I)]
        sink = [_swa_rows([_lane_pick(sk, g * SW_GROUP + r) for r in range(SW_GROUP)]) for s, g in I]
        m = [jnp.maximum(jnp.max(logits[n], axis=-1, keepdims=True), sink[n]) for n in range(len(I))]
        p = [jnp.exp(logits[n] - m[n]) for n in range(len(I))]
        den = [jnp.sum(p[n], axis=-1, keepdims=True) + jnp.exp(sink[n] - m[n]) for n in range(len(I))]
        pv = [_dot(p[n], vv[n]) for n in range(len(I))]
        lse_tiles = [jnp.zeros((W, SW_Q_HEADS), F32) for _ in range(S)]
        for n, (s, g) in enumerate(I):
            _swa_unstack(o_ref, pv[n] / den[n], g, rows[s])
            lse = m[n] + jnp.log(den[n])
            for r in range(SW_GROUP):
                lse_tiles[s] = _lane_put(lse_tiles[s], g * SW_GROUP + r, lse[r * W:(r + 1) * W])
        for s in range(S):
            lse_ref[rows[s], :] = lse_tiles[s]

    return pl.pallas_call(body, name=name, grid=(nb // S,),
                          in_specs=[pl.BlockSpec((None, S * W, D), lambda j: (0, j, 0)),
                                    pl.BlockSpec((None, S * W, 2 * kvw), lambda j: (0, j, 0)),
                                    pl.BlockSpec((None, W, 2 * kvw), lambda j: (0, jnp.maximum(j * S - 1, 0), 0)),
                                    pl.BlockSpec((SW_Q_HEADS, W, 2 * W), lambda j: (0, 0, 0)),
                                    pl.BlockSpec((1, SW_Q_HEADS), lambda j: (0, 0))],
                          out_specs=[pl.BlockSpec((None, S * W, D), lambda j: (0, j, 0)),
                                     pl.BlockSpec((S * W, SW_Q_HEADS), lambda j: (j, 0))],
                          out_shape=[jax.ShapeDtypeStruct((1, T, D), F32), jax.ShapeDtypeStruct((T, SW_Q_HEADS), F32)],
                          compiler_params=_params())(q, kv, kv, bias, sinks)


def _swa_bwd(q, kv, o, lse, do, bias, sinks, *, name):
    _, T, D = q.shape
    W = SW_WINDOW
    nb = T // W
    dh = SW_HEAD_DIM
    kvw = SW_KV_HEADS * dh
    scale = dh ** -0.5
    cl = lambda j: jnp.minimum(j, nb - 1)

    def body(q_ref, kc_ref, kp_ref, o_ref, lse_ref, do_ref, bias_ref, sink_ref,
             dq_ref, dkv_ref, dbias_ref, dsink_ref, carry):
        j = pl.program_id(0)

        @pl.when(j == 0)
        def _():
            carry[...] = jnp.zeros_like(carry)
            dbias_ref[...] = jnp.zeros_like(dbias_ref)
            dsink_ref[...] = jnp.zeros_like(dsink_ref)

        @pl.when(j < nb)
        def _():
            mask = _swa_mask(j)
            sk = sink_ref[...]
            lse_tile = lse_ref[...]
            dsink = jnp.zeros((1, SW_Q_HEADS), F32)
            G = range(SW_KV_HEADS)
            heads = [[g * SW_GROUP + r for r in range(SW_GROUP)] for g in G]
            kk = [_swa_kv(kp_ref, kc_ref, 0, g) for g in G]
            vv = [_swa_kv(kp_ref, kc_ref, kvw, g) for g in G]
            qs = [_swa_stack(q_ref, g) for g in G]
            dos = [_swa_stack(do_ref, g) for g in G]
            lse = [jnp.concatenate([_lane_pick(lse_tile, h) for h in heads[g]], axis=0) for g in G]
            sink = [_swa_rows([_lane_pick(sk, h) for h in heads[g]]) for g in G]
            logits = [jnp.where(mask, _dot_nt(qs[g], kk[g]) * scale
                                + bias_ref[g * SW_GROUP:(g + 1) * SW_GROUP].reshape(SW_GROUP * W, 2 * W), NEG_BIG)
                      for g in G]
            dp = [_dot_nt(dos[g], vv[g]) for g in G]
            p = [jnp.exp(logits[g] - lse[g]) for g in G]
            delta = [jnp.sum(dos[g] * _swa_stack(o_ref, g), axis=-1, keepdims=True) for g in G]
            dl = [p[g] * (dp[g] - delta[g]) for g in G]
            dqs = [_dot(dl[g], kk[g]) * scale for g in G]
            dks = [_dot_tn(dl[g], qs[g]) * scale for g in G]
            dvs = [_dot_tn(p[g], dos[g]) for g in G]
            for g in G:
                _swa_unstack(dq_ref, dqs[g], g)
                dbias_ref[g * SW_GROUP:(g + 1) * SW_GROUP] += dl[g].reshape(SW_GROUP, W, 2 * W)
                sd = jnp.exp(sink[g] - lse[g]) * delta[g]
                for r, h in enumerate(heads[g]):
                    dsink = _lane_put(dsink, h, -jnp.sum(sd[r * W:(r + 1) * W], axis=0, keepdims=True))
            dsink_ref[...] += dsink
            for slab in range(SW_KV_HEADS // 2):
                for col0, parts in ((0, dks), (kvw, dvs)):
                    both = parts[2 * slab] + parts[2 * slab + 1]
                    cols = slice(col0 + slab * LANES, col0 + (slab + 1) * LANES)
                    dkv_ref[:, cols] = (carry[:, cols] + both[:W]).astype(dkv_ref.dtype)
                    carry[:, cols] = both[W:]

        @pl.when(j == nb)
        def _():
            dkv_ref[...] = carry[...].astype(dkv_ref.dtype)

    tok = lambda w: pl.BlockSpec((None, W, w), lambda j: (0, cl(j), 0))
    return pl.pallas_call(body, name=name, grid=(nb + 1,),
                          in_specs=[tok(D), tok(2 * kvw),
                                    pl.BlockSpec((None, W, 2 * kvw), lambda j: (0, jnp.maximum(cl(j) - 1, 0), 0)),
                                    tok(D), pl.BlockSpec((W, SW_Q_HEADS), lambda j: (cl(j), 0)), tok(D),
                                    pl.BlockSpec((SW_Q_HEADS, W, 2 * W), lambda j: (0, 0, 0)),
                                    pl.BlockSpec((1, SW_Q_HEADS), lambda j: (0, 0))],
                          out_specs=[tok(D),
                                     pl.BlockSpec((None, W, 2 * kvw), lambda j: (0, jnp.maximum(j - 1, 0), 0)),
                                     pl.BlockSpec((SW_Q_HEADS, W, 2 * W), lambda j: (0, 0, 0)),
                                     pl.BlockSpec((1, SW_Q_HEADS), lambda j: (0, 0))],
                          out_shape=[jax.ShapeDtypeStruct((1, T, D), MXU), jax.ShapeDtypeStruct((1, T, 2 * kvw), MXU),
                                     jax.ShapeDtypeStruct((SW_Q_HEADS, W, 2 * W), F32),
                                     jax.ShapeDtypeStruct((1, SW_Q_HEADS), F32)],
                          scratch_shapes=[pltpu.VMEM((W, 2 * kvw), F32)],
                          compiler_params=_params())(q, kv, kv, o, lse, do, bias, sinks)


_HBM = pl.BlockSpec(memory_space=pltpu.HBM)
_SEM = pl.BlockSpec(memory_space=pltpu.SEMAPHORE)
_EFFECT = pltpu.SideEffectType.DATAFLOW_SIDE_EFFECTING
N_PEERS = N_DEV - 1


def _peer(k):
    x, y, c = lax.axis_index("x"), lax.axis_index("y"), lax.axis_index("c")
    px = (x + (k >> 2)) % 2
    py = (y + ((k >> 1) & 1)) % 2
    pc = (c + (k & 1)) % 2
    return (px, py, pc), 4 * px + 2 * py + pc


def _my_number():
    return 4 * lax.axis_index("x") + 2 * lax.axis_index("y") + lax.axis_index("c")


def _landing(src, mode):
    me = _my_number()
    own = src if mode == "gather" else lax.dynamic_index_in_dim(src, me, 0, keepdims=False)
    return lax.dynamic_update_index_in_dim(lax.empty((N_DEV,) + own.shape, own.dtype), own, me, 0)


def _copy(src_ref, land_ref, mode, send, recv, j, k, dst_slot):
    peer, pid = _peer(k)
    return pltpu.make_async_remote_copy(
        src_ref=src_ref if mode == "gather" else src_ref.at[pid], dst_ref=land_ref.at[dst_slot(pid)],
        send_sem=send.at[j * N_PEERS + k - 1], recv_sem=recv.at[j * N_PEERS + k - 1],
        device_id=peer, device_id_type=pl.DeviceIdType.MESH)


def _send_start(groups, *, name):
    flat = [t for g in groups for t in g]
    n, ng = len(flat), len(groups)
    srcs = [pltpu.with_memory_space_constraint(s, pltpu.HBM) for s, _ in flat]
    lands = [pltpu.with_memory_space_constraint(_landing(s, m), pltpu.HBM) for s, m in flat]

    def body(*refs):
        src_refs, land_refs = refs[:n], refs[n:2 * n]
        sems = refs[2 * n:2 * n + 2 * ng]
        token = refs[-1]
        me = _my_number()
        a = 0
        for gi, g in enumerate(groups):
            for j, (_, mode) in enumerate(g):
                for k in range(1, N_DEV):
                    _copy(src_refs[a], land_refs[a], mode, sems[2 * gi], sems[2 * gi + 1], j, k, lambda pid: me).start()
                a += 1
        token[...] = jnp.zeros_like(token)

    sem_shapes = []
    for g in groups:
        sem_shapes += [pltpu.SemaphoreType.DMA((len(g) * N_PEERS,))] * 2
    out = pl.pallas_call(
        body, name=name,
        out_shape=tuple(sem_shapes) + tuple(pltpu.HBM(a.shape, a.dtype) for a in srcs + lands)
        + (jax.ShapeDtypeStruct((SUBLANES, LANES), F32),),
        in_specs=[_HBM] * (2 * n), out_specs=[_SEM] * (2 * ng) + [_HBM] * (2 * n) + [pl.BlockSpec(memory_space=pltpu.VMEM)],
        input_output_aliases={i: 2 * ng + i for i in range(2 * n)},
        compiler_params=pltpu.CompilerParams(has_side_effects=_EFFECT))(*srcs, *lands)
    sems, thru, token = out[:2 * ng], out[2 * ng:2 * ng + 2 * n], out[-1]
    handles, a = [], 0
    for gi, g in enumerate(groups):
        m = len(g)
        handles.append((sems[2 * gi], sems[2 * gi + 1], list(thru[a:a + m]), list(thru[n + a:n + a + m]),
                        [mode for _, mode in g]))
        a += m
    return handles, token


def _send_wait(handle, after, *, name):
    send, recv, srcs, lands, modes = handle
    m = len(srcs)

    def body(*refs):
        src_refs, land_refs = refs[:m], refs[m:2 * m]
        send_ref, recv_ref = refs[2 * m], refs[2 * m + 1]
        for j in range(m):
            for k in range(1, N_DEV):
                cp = _copy(src_refs[j], land_refs[j], modes[j], send_ref, recv_ref, j, k, lambda pid: pid)
                cp.wait_send()
                cp.wait_recv()

    out = pl.pallas_call(
        body, name=name, out_shape=tuple(pltpu.HBM(a.shape, a.dtype) for a in srcs + lands),
        in_specs=[_HBM] * (2 * m) + [_SEM, _SEM] + [pl.BlockSpec(memory_space=pl.ANY)] * len(after),
        out_specs=[_HBM] * (2 * m), input_output_aliases={i: i for i in range(2 * m)},
        compiler_params=pltpu.CompilerParams(has_side_effects=_EFFECT))(*srcs, *lands, send, recv, *after)
    return list(out[m:])


def _adam_math(w, g, m, v):
    m = ADAM_B1 * m + (1.0 - ADAM_B1) * g
    v = ADAM_B2 * v + (1.0 - ADAM_B2) * (g * g)
    m_hat = m / (1.0 - ADAM_B1 ** ADAM_STEP)
    v_hat = v / (1.0 - ADAM_B2 ** ADAM_STEP)
    delta = -ADAM_LR * (m_hat / (jnp.sqrt(v_hat) + ADAM_EPS) + ADAM_WD * w)
    return delta, m, v


def _adamw(parts, w, m, v, *, name, layer=None):
    S, R, C = parts.shape
    tr = R
    for cand in (256, 128, 64, 32, 16, 8):
        if R % cand == 0 and S * cand * C * 4 <= 4 * 2 ** 20:
            tr = cand
            break

    def body(p_ref, w_ref, m_ref, v_ref, g_ref, d_ref, nm_ref, nv_ref):
        g = p_ref[0].astype(F32)
        for s in range(1, S):
            g = g + p_ref[s].astype(F32)
        delta, nm, nv = _adam_math(w_ref[...], g, m_ref[...], v_ref[...])
        g_ref[...] = g
        d_ref[...] = delta
        nm_ref[...] = nm
        nv_ref[...] = nv

    if layer is None:
        wspec = pl.BlockSpec((tr, C), lambda i: (i, 0))
    else:
        wspec = pl.BlockSpec((None, tr, C), lambda i: (layer, i, 0))
    ospec = pl.BlockSpec((tr, C), lambda i: (i, 0))
    osh = jax.ShapeDtypeStruct((R, C), F32)
    return pl.pallas_call(body, name=name, grid=(R // tr,),
                          in_specs=[pl.BlockSpec((S, tr, C), lambda i: (0, i, 0)), wspec, wspec, wspec],
                          out_specs=[ospec] * 4, out_shape=[osh] * 4, compiler_params=_params())(parts, w, m, v)


def _sum_parts(parts, *, name):
    S, R, C = parts.shape

    def body(p_ref, o_ref):
        g = p_ref[0]
        for s in range(1, S):
            g = g + p_ref[s]
        o_ref[...] = g

    return pl.pallas_call(body, name=name, out_shape=jax.ShapeDtypeStruct((R, C), F32),
                          compiler_params=_params())(parts)


def _pack_rows(arrays):
    pieces, layout, row = [], [], 0
    for a in arrays:
        flat = a.reshape(-1).astype(F32)
        rows = -(-flat.shape[0] // (SUBLANES * LANES)) * SUBLANES
        flat = jnp.pad(flat, (0, rows * LANES - flat.shape[0]))
        pieces.append(flat.reshape(rows, LANES))
        layout.append((row, rows, a.shape))
        row += rows
    return jnp.concatenate(pieces, axis=0), layout


def _unpack_rows(packed, layout):
    out = []
    for row, rows, shape in layout:
        size = int(np.prod(shape))
        out.append(packed[row:row + rows].reshape(-1)[:size].reshape(shape))
    return out


def _ffn_fwd(h, w_in, w_out, cw, ln_g, ln_b, tag):
    h, hb = h
    u, ab, act = _ffn_up(hb, w_in, cw, name=f"ffn_up_{tag}")
    hnb, xh, rs = _mm_nn(act, w_out, res=h, res_scale=ALPHA, ln=(ln_g, ln_b), name=f"ffn_down_{tag}")
    return hnb, xh, rs, (u, ab)


def _ffn_bwd(dy, hb, u, w_in, w_out, cw, ln_bwd, send, tag):
    du, dw_out, dcw = _ffn_gate_bwd(dy, u[0], u[1], w_out, cw, name=f"ffn_gate_bwd_{tag}")
    du = du.reshape((-1,) + du.shape[2:])
    dw_in = _mm_tn(hb, du, n_map=_pair_map, name=f"ffn_dwin_{tag}")
    handle, token = send(dw_in, dw_out)
    dyp, dg, db = _mm_nt_resident(du, w_in, n_map=_pair_map, res=dy, res_scale=ALPHA, ln_bwd=ln_bwd,
                                  behind=(token,), name=f"ffn_dh_{tag}")
    dcw = dcw.transpose(1, 0, 2, 3).reshape((-1,) + dcw.shape[2:])
    return dyp, dg, db, handle, dcw


def kernel(x, hgrn_w_in, hgrn_lb_logits, hgrn_gnorm_w, hgrn_w_out, swa_w_q, swa_sinks, swa_w_out, shared_w_kv, rel_bias, ffn_w_in, ffn_conv_w, ffn_conv_b, ffn_w_out, ln_mix_g, ln_mix_b, ln_ffn_g, ln_ffn_b, loss_target, m_hgrn_w_in, m_hgrn_lb_logits, m_hgrn_gnorm_w, m_hgrn_w_out, m_swa_w_q, m_swa_sinks, m_swa_w_out, m_shared_w_kv, m_rel_bias, m_ffn_w_in, m_ffn_conv_w, m_ffn_conv_b, m_ffn_w_out, m_ln_mix_g, m_ln_mix_b, m_ln_ffn_g, m_ln_ffn_b, v_hgrn_w_in, v_hgrn_lb_logits, v_hgrn_gnorm_w, v_hgrn_w_out, v_swa_w_q, v_swa_sinks, v_swa_w_out, v_shared_w_kv, v_rel_bias, v_ffn_w_in, v_ffn_conv_w, v_ffn_conv_b, v_ffn_w_out, v_ln_mix_g, v_ln_mix_b, v_ln_ffn_g, v_ln_ffn_b):
    T = x.shape[1]
    D = D_MODEL
    W = SW_WINDOW
    fb = ffn_w_in.shape[2]
    me = 4 * lax.axis_index("x") + 2 * lax.axis_index("y") + lax.axis_index("c")

    small_fwd, small_fwd_layout = _pack_rows([hgrn_lb_logits, ffn_conv_w])
    gat = lambda *ws: [(w_.astype(MXU), "gather") for w_ in ws]
    (wait_a, wait_b, wait_c), _ = _send_start(
        [gat(hgrn_w_in[0]) + [(small_fwd, "gather")],
         gat(hgrn_w_out[0], ffn_w_in[0], ffn_w_out[0]),
         gat(shared_w_kv, swa_w_q[0], swa_w_out[0], ffn_w_in[1], ffn_w_out[1])], name="gather_start")
    xb = x.astype(MXU)
    w_hin, small_all = _send_wait(wait_a, (xb,), name="gather_wait_a")
    w_hin = w_hin[None]
    ffn_rows = 2 * ffn_w_out.shape[1]
    (lb_row, lb_rows, _), (cw_row, cw_rows, _) = small_fwd_layout
    lbl = small_all[:, lb_row:lb_row + 2, :].transpose(1, 0, 2).reshape(2, D)
    conv_w_all = small_all[:, cw_row:cw_row + cw_rows, :].reshape(N_DEV, -1)[:, :DEPTH * 3 * fb]
    conv_w_all = conv_w_all.reshape(N_DEV, DEPTH, 3, fb).transpose(1, 0, 2, 3)
    conv_b_all = ffn_conv_b.reshape(DEPTH, N_DEV, 1, fb)
    no_pad = ((0, 0), (0, 0))
    cw = (jnp.pad(conv_w_all, no_pad + ((0, SUBLANES - 3), (0, 0)))
          + jnp.pad(conv_b_all, no_pad + ((3, SUBLANES - 4), (0, 0))))

    row = lambda a, l: a[l:l + 1]

    z = _mm_nn(xb, w_hin, name="hgrn_in")
    og, states = _hgrn_fwd(z, lbl, hgrn_gnorm_w, name="hgrn_rec")
    w_hout, w_fin0, w_fout0 = _send_wait(wait_b, (og,), name="gather_wait_b")
    w_hout = w_hout.reshape(1, 1, D, D)
    w_fin = [w_fin0[None], None]
    w_fout = [w_fout0.reshape(4, 1, ffn_rows, D), None]
    h1b, xh1, rs1 = _mm_nn(og, w_hout, res=x, res_scale=ALPHA, ln=(row(ln_mix_g, 0), row(ln_mix_b, 0)), name="hgrn_out")
    h1 = (xh1, row(ln_mix_g, 0), row(ln_mix_b, 0))
    h2b, xh2, rs2, u0 = _ffn_fwd((h1, h1b), w_fin[0], w_fout[0], cw[0], row(ln_ffn_g, 0), row(ln_ffn_b, 0), "l0")
    h2 = (xh2, row(ln_ffn_g, 0), row(ln_ffn_b, 0))
    w_kv, w_q, w_o, w_fin1, w_fout1 = _send_wait(wait_c, (h2b,), name="gather_wait_c")
    w_kv = w_kv.reshape(1, 1, D, 2 * SW_KV_HEADS * SW_HEAD_DIM)
    w_q = w_q.reshape(1, 1, D, D)
    w_o = w_o.reshape(1, 1, D, D)
    w_fin[1] = w_fin1[None]
    w_fout[1] = w_fout1.reshape(4, 1, ffn_rows, D)
    kv = _mm_nn(h2b, w_kv, name="swa_kv")
    q = _mm_nn(h2b, w_q, name="swa_q")
    onehot = _bucket_onehot()
    bias = _bias_expand(rel_bias.T, jnp.asarray(onehot.T, jnp.bfloat16), name="swa_bias").reshape(SW_Q_HEADS, W, 2 * W)
    ao, lse = _swa_fwd(q, kv, bias, swa_sinks, name="swa_attn")
    h3b, xh3, rs3 = _mm_nn(ao, w_o, res=h2, res_scale=ALPHA, ln=(row(ln_mix_g, 1), row(ln_mix_b, 1)), name="swa_out")
    h3 = (xh3, row(ln_mix_g, 1), row(ln_mix_b, 1))
    _, xh4, rs4, u1 = _ffn_fwd((h3, h3b), w_fin[1], w_fout[1], cw[1], row(ln_ffn_g, 1), row(ln_ffn_b, 1), "l1")
    dy4, dg_f1, db_f1, loss_tile = _loss_ln_bwd(loss_target, xh4, rs4, row(ln_ffn_g, 1), row(ln_ffn_b, 1),
                                                name="loss_ln_ffn1_bwd")
    sc = lambda *gs: [(g_, "scatter") for g_ in gs]

    def send_ffn(name_):
        def send(dw_in, dw_out):
            (handle,), token = _send_start([sc(dw_in.reshape(N_DEV, D, fb), dw_out.reshape(N_DEV, -1, D))], name=name_)
            return handle, token
        return send

    dy3, dg_m1, db_m1, ex1, dcw1 = _ffn_bwd(dy4, h3b, u1, w_fin[1], w_fout[1], cw[1],
                                            (xh3, rs3, row(ln_mix_g, 1)), send_ffn("grads_start_1"), "l1")
    dw_o = _mm_tn(ao, dy3, name="swa_dwo")
    dao = _mm_nt(dy3, w_o, name="swa_dao")
    dq, dkv, dbias, dsinks = _swa_bwd(q, kv, ao, lse, dao, bias, swa_sinks, name="swa_attn_bwd")
    drel_t = _bias_reduce(dbias.reshape(SW_Q_HEADS, W * 2 * W), jnp.asarray(onehot, jnp.bfloat16), name="swa_dbias")
    dw_q = _mm_tn(h2b, dq, name="swa_dwq")
    dw_kv = _mm_tn(h2b, dkv, name="swa_dwkv")
    dh2 = _mm_nt(dq, w_q, res=dy3, res_scale=ALPHA, name="swa_dh_q")
    (ex2,), tok2 = _send_start([sc(dw_o.reshape(N_DEV, D // N_DEV, D), dw_q.reshape(N_DEV, D // N_DEV, D),
                                   dw_kv.reshape(N_DEV, D // N_DEV, -1))], name="grads_start_2")
    dy2, dg_f0, db_f0 = _mm_nt_resident(dkv, w_kv, res=dh2, ln_bwd=(xh2, rs2, row(ln_ffn_g, 0)), behind=(tok2,),
                                        name="swa_dh_kv")
    dy1, dg_m0, db_m0, ex3, dcw0 = _ffn_bwd(dy2, h1b, u0, w_fin[0], w_fout[0], cw[0],
                                            (xh1, rs1, row(ln_mix_g, 0)), send_ffn("grads_start_3"), "l0")
    dw_hout = _mm_tn(og, dy1, name="hgrn_dwout")
    dog = _mm_nt(dy1, w_hout, name="hgrn_dog")
    dz, dlb, dgw = _hgrn_bwd(z, dog, states, lbl, hgrn_gnorm_w, name="hgrn_rec_bwd")
    dw_hin = _mm_tn(xb, dz, name="hgrn_dwin")

    p0 = _sigmoid(lbl[0:1] - lbl[1:2])
    dl0 = dlb * p0 * (1.0 - p0)
    d_lbl = dl0 * jnp.array([[1.0], [-1.0]], F32)
    dcw = jnp.stack([dcw0, dcw1], axis=0)
    d_conv_w = dcw[:, :, 0:3, :]
    d_conv_b = dcw[:, :, 3, :].reshape(DEPTH, N_DEV * fb)
    first_row = lax.broadcasted_iota(jnp.int32, (DEPTH, D), 0) == 0
    two_rows = lambda a, b: jnp.where(first_row, a, b)
    d_ln_mix_g = two_rows(dg_m0, dg_m1)
    d_ln_mix_b = two_rows(db_m0, db_m1)
    d_ln_ffn_g = two_rows(dg_f0, dg_f1)
    d_ln_ffn_b = two_rows(db_f0, db_f1)
    small_grads, small_layout = _pack_rows([d_lbl, d_conv_w, dgw, dsinks, drel_t.T, d_conv_b, d_ln_mix_g, d_ln_mix_b,
                                            d_ln_ffn_g, d_ln_ffn_b, loss_tile[0:1, 0:1]])

    (ex4,), tok4 = _send_start([sc(dw_hin.reshape(N_DEV, D, -1), dw_hout.reshape(N_DEV, D // N_DEV, D))
                                + [(small_grads, "gather")]], name="grads_start_4")
    dx = _mm_nt_resident(dz, w_hin, res=dy1, res_scale=ALPHA, name="hgrn_dx", behind=(tok4,))
    r_fin1, r_fout1 = _send_wait(ex1, (dx,), name="grads_wait_1")
    r_o, r_q, r_kv = _send_wait(ex2, (dx,), name="grads_wait_2")
    r_fin0, r_fout0 = _send_wait(ex3, (dx,), name="grads_wait_3")
    r_hin, r_hout, r_small = _send_wait(ex4, (dx,), name="grads_wait_4")
    received = [r_hin, r_hout, r_q, r_o, r_kv, r_fin0, r_fin1, r_fout0, r_fout1, r_small]

    outs = {}

    def put(name_, res):
        outs["grad_" + name_], outs["delta_" + name_], outs["new_m_" + name_], outs["new_v_" + name_] = res

    def big_update(name_, parts, w, m, v):
        shp = w.shape
        if w.ndim == 3 and shp[0] == 1:
            r = _adamw(parts, w[0], m[0], v[0], name="adamw_" + name_)
            put(name_, [a.reshape(shp) for a in r])
        else:
            r = _adamw(parts, w, m, v, name="adamw_" + name_)
            put(name_, r)

    big_update("hgrn_w_in", received[0], hgrn_w_in, m_hgrn_w_in, v_hgrn_w_in)
    big_update("hgrn_w_out", received[1], hgrn_w_out, m_hgrn_w_out, v_hgrn_w_out)
    big_update("swa_w_q", received[2], swa_w_q, m_swa_w_q, v_swa_w_q)
    big_update("swa_w_out", received[3], swa_w_out, m_swa_w_out, v_swa_w_out)
    big_update("shared_w_kv", received[4], shared_w_kv, m_shared_w_kv, v_shared_w_kv)
    for name_, idx, w, m, v in (("ffn_w_in", 5, ffn_w_in, m_ffn_w_in, v_ffn_w_in),
                                ("ffn_w_out", 7, ffn_w_out, m_ffn_w_out, v_ffn_w_out)):
        per_layer = [_adamw(received[idx + l], w, m, v, layer=l, name=f"adamw_{name_}_{l}") for l in range(DEPTH)]
        put(name_, [jnp.stack([per_layer[0][i], per_layer[1][i]], axis=0) for i in range(4)])

    small_sum = _sum_parts(received[9], name="sum_small_grads")
    (g_lbl, g_conv_w, g_gw, g_sinks, g_rel, g_conv_b, g_mix_g, g_mix_b, g_ffn_g, g_ffn_b,
     loss) = _unpack_rows(small_sum, small_layout)
    g_lbl_mine = lax.dynamic_slice_in_dim(g_lbl, me * (D // N_DEV), D // N_DEV, axis=1)
    g_conv_w_mine = lax.dynamic_index_in_dim(g_conv_w, me, axis=1, keepdims=False)
    small_names = ["hgrn_lb_logits", "ffn_conv_w", "hgrn_gnorm_w", "swa_sinks", "rel_bias", "ffn_conv_b",
                   "ln_mix_g", "ln_mix_b", "ln_ffn_g", "ln_ffn_b"]
    small_g = [g_lbl_mine, g_conv_w_mine, g_gw, g_sinks, g_rel, g_conv_b, g_mix_g, g_mix_b, g_ffn_g, g_ffn_b]
    small_w = [hgrn_lb_logits, ffn_conv_w, hgrn_gnorm_w, swa_sinks, rel_bias, ffn_conv_b, ln_mix_g, ln_mix_b,
               ln_ffn_g, ln_ffn_b]
    small_m = [m_hgrn_lb_logits, m_ffn_conv_w, m_hgrn_gnorm_w, m_swa_sinks, m_rel_bias, m_ffn_conv_b, m_ln_mix_g,
               m_ln_mix_b, m_ln_ffn_g, m_ln_ffn_b]
    small_v = [v_hgrn_lb_logits, v_ffn_conv_w, v_hgrn_gnorm_w, v_swa_sinks, v_rel_bias, v_ffn_conv_b, v_ln_mix_g,
               v_ln_mix_b, v_ln_ffn_g, v_ln_ffn_b]
    pg, lay = _pack_rows(small_g)
    pw, _ = _pack_rows(small_w)
    pm, _ = _pack_rows(small_m)
    pv, _ = _pack_rows(small_v)
    res = _adamw(pg[None], pw, pm, pv, name="adamw_small")
    unpacked = [_unpack_rows(r, lay) for r in res]
    for i, name_ in enumerate(small_names):
        put(name_, [unpacked[j][i] for j in range(4)])

    order = ["hgrn_w_in", "hgrn_lb_logits", "hgrn_gnorm_w", "hgrn_w_out", "swa_w_q", "swa_sinks", "swa_w_out",
             "shared_w_kv", "rel_bias", "ffn_w_in", "ffn_conv_w", "ffn_conv_b", "ffn_w_out", "ln_mix_g", "ln_mix_b",
             "ln_ffn_g", "ln_ffn_b"]
    result = [loss.reshape(()), dx]
    for kind in ("grad_", "delta_", "new_m_", "new_v_"):
        result += [outs[kind + n] for n in order]
    return tuple(result)
```

```python
import functools
import math

import numpy as np
import jax
import jax.numpy as jnp
from jax import lax
from jax.experimental import pallas as pl
from jax.experimental.pallas import tpu as pltpu

F32 = jnp.float32
MXU = jnp.bfloat16

N_DEV = 8
D_MODEL = 1024
DEPTH = 2
HG_HEADS = 8
HG_DIM = 128
HG_CHUNK = 64
HG_STEP_CHUNKS = 4
SW_Q_HEADS = 16
SW_KV_HEADS = 4
SW_GROUP = 4
SW_HEAD_DIM = 64
SW_WINDOW = 128
SW_STEP_BLOCKS = 2
REL_BUCKETS = 32
REL_MAX_DIST = 128
FFN_DIM = 2816
ALPHA = (2.0 * DEPTH) ** 0.25
LN_EPS = 1e-5
RMS_EPS = 1e-6
ADAM_LR = 0.001
ADAM_B1 = 0.9
ADAM_B2 = 0.999
ADAM_EPS = 1e-08
ADAM_WD = 0.01
ADAM_STEP = 10
EXP_CLAMP = 80.0
NEG_BIG = -1e30

SUBLANES = 8
LANES = 128
VMEM_LIMIT = 48 * 2 ** 20
TOKEN_TILE = 512
FFN_TOKEN_TILE = 1024
WIDE_TOKEN_TILE = 1024
RESIDENT_TOKEN_TILE = 256
REDUCE_TOKEN_TILE = 2048
REDUCE_TOKEN_TILE_16BIT = 4096
GRAD_DTYPE = jnp.bfloat16


def _params(**kw):
    return pltpu.CompilerParams(vmem_limit_bytes=VMEM_LIMIT, **kw)


def _sigmoid(x):
    return 1.0 / (1.0 + jnp.exp(-x))


def _dot(a, b):
    return jnp.dot(a.astype(MXU), b.astype(MXU), preferred_element_type=F32)


def _dot_nt(a, b):
    return lax.dot_general(a.astype(MXU), b.astype(MXU), (((1,), (1,)), ((), ())), preferred_element_type=F32)


def _dot_tn(a, b):
    return lax.dot_general(a.astype(MXU), b.astype(MXU), (((0,), (0,)), ((), ())), preferred_element_type=F32)


def _trunc_bf16(x):
    bits = lax.bitcast_convert_type(x, jnp.int32)
    return lax.bitcast_convert_type(bits & jnp.int32(-65536), F32)


def _split3(x):
    hi = _trunc_bf16(x)
    r = x - hi
    mid = _trunc_bf16(r)
    lo = r - mid
    return hi.astype(jnp.bfloat16), mid.astype(jnp.bfloat16), lo.astype(jnp.bfloat16)


def _dot_hp(a, b, contract):
    def halves(x):
        hi = _trunc_bf16(x)
        return hi.astype(jnp.bfloat16), (x - hi).astype(jnp.bfloat16)

    ah, al = halves(a)
    bh, bl = halves(b)
    d = lambda p, q: lax.dot_general(p, q, (contract, ((), ())), preferred_element_type=F32)
    return d(ah, bh) + d(ah, bl) + d(al, bh)


def _exact_dot(m01, x):
    hi, mid, lo = _split3(x)
    d = lambda p: jnp.dot(m01, p, preferred_element_type=F32)
    return d(hi) + d(mid) + d(lo)


def _exact_dot_r(x, m01):
    hi, mid, lo = _split3(x)
    d = lambda p: jnp.dot(p, m01, preferred_element_type=F32)
    return d(hi) + d(mid) + d(lo)


def _mm_nn(a, w, *, name, res=None, res_scale=1.0, ln=None, out_dtype=F32, tm=None):
    nbk, T, kw = a.shape
    _, nbn, _, nw = w.shape
    tm = min(tm or TOKEN_TILE, T)
    has_res = res is not None
    res_ln = isinstance(res, tuple)
    n_res = (3 if res_ln else 1) if has_res else 0
    assert ln is None or nbn == 1

    def body(*refs):
        refs = list(refs)
        a_ref, w_ref = refs[:2]
        res_refs = refs[2:2 + n_res]
        pos = 2 + n_res
        if ln is not None:
            g_ref, b_ref = refs[pos:pos + 2]
            pos += 2
        o_ref = refs[pos]
        if ln is not None:
            xh_ref, rs_ref = refs[pos + 1:pos + 3]
        for n in range(nbn):
            y = _dot(a_ref[0], w_ref[0, n])
            for k in range(1, nbk):
                y = y + _dot(a_ref[k], w_ref[k, n])
            if res_ln:
                y = y + res_scale * (res_refs[0][n] * res_refs[1][...] + res_refs[2][...])
            elif has_res:
                y = y + res_scale * res_refs[0][n].astype(F32)
            if ln is None:
                o_ref[n] = y.astype(o_ref.dtype)
            else:
                mu = jnp.mean(y, axis=-1, keepdims=True)
                yc = y - mu
                var = jnp.mean(yc * yc, axis=-1, keepdims=True)
                rstd = lax.rsqrt(var + LN_EPS)
                xh = yc * rstd
                xh_ref[n] = xh
                rs_ref[...] = rstd
                o_ref[n] = (xh * g_ref[...] + b_ref[...]).astype(o_ref.dtype)

    vec = pl.BlockSpec((1, nw), lambda i: (0, 0))
    in_specs = [pl.BlockSpec((nbk, tm, kw), lambda i: (0, i, 0)),
                pl.BlockSpec((nbk, nbn, kw, nw), lambda i: (0, 0, 0, 0))]
    args = [a, w]
    if has_res:
        in_specs.append(pl.BlockSpec((nbn, tm, nw), lambda i: (0, i, 0)))
        if res_ln:
            in_specs += [vec, vec]
            args += list(res)
        else:
            args.append(res)
    if ln is not None:
        in_specs += [vec, vec]
        args += list(ln)
    out_spec = pl.BlockSpec((nbn, tm, nw), lambda i: (0, i, 0))
    out_shape = jax.ShapeDtypeStruct((nbn, T, nw), out_dtype)
    if ln is not None:
        out_specs = [out_spec, out_spec, pl.BlockSpec((tm, 1), lambda i: (i, 0))]
        out_shape = [jax.ShapeDtypeStruct((nbn, T, nw), MXU), jax.ShapeDtypeStruct((nbn, T, nw), F32),
                     jax.ShapeDtypeStruct((T, 1), F32)]
    else:
        out_specs = out_spec
    return pl.pallas_call(body, name=name, grid=(T // tm,), in_specs=in_specs, out_specs=out_specs,
                          out_shape=out_shape, compiler_params=_params())(*args)


def _same(n):
    return n


def _mm_nt(dy, w, *, name, res=None, res_scale=1.0, out_dtype=F32, tm=None, n_map=_same, behind=()):
    nbn, T, nw = dy.shape
    nbk, _, kw, _ = w.shape
    tm = min(tm or WIDE_TOKEN_TILE, T)
    has_res = res is not None

    def body(*refs):
        refs = list(refs)
        dy_ref, w_ref = refs[:2]
        pos = 2
        res_ref = None
        if has_res:
            res_ref = refs[pos]
            pos += 1
        pos += len(behind)
        o_ref = refs[pos]
        pos += 1
        acc_ref = refs[pos] if nbn > 1 else None
        n = pl.program_id(2)
        part = _dot_nt(dy_ref[...], w_ref[...])

        def finish(acc):
            y = acc
            if has_res:
                y = y + res_scale * res_ref[...].astype(F32)
            o_ref[...] = y.astype(o_ref.dtype)

        if nbn == 1:
            finish(part)
        else:
            @pl.when(n == 0)
            def _():
                acc_ref[...] = part

            @pl.when(n > 0)
            def _():
                acc_ref[...] += part

            @pl.when(n == nbn - 1)
            def _():
                finish(acc_ref[...])

    in_specs = [pl.BlockSpec((None, tm, nw), lambda i, k, n: (n, i, 0)),
                pl.BlockSpec((None, None, kw, nw), lambda i, k, n: (k, n_map(n), 0, 0))]
    args = [dy, w]
    if has_res:
        in_specs.append(pl.BlockSpec((None, tm, kw), lambda i, k, n: (k, i, 0)))
        args.append(res)
    in_specs += [pl.BlockSpec(memory_space=pl.ANY)] * len(behind)
    args += list(behind)
    scratch = [pltpu.VMEM((tm, kw), F32)] if nbn > 1 else []
    return pl.pallas_call(body, name=name, grid=(T // tm, nbk, nbn), in_specs=in_specs,
                          out_specs=pl.BlockSpec((None, tm, kw), lambda i, k, n: (k, i, 0)),
                          out_shape=jax.ShapeDtypeStruct((nbk, T, kw), out_dtype), scratch_shapes=scratch,
                          compiler_params=_params())(*args)


def _mm_nt_resident(dy, w, *, name, res=None, res_scale=1.0, ln_bwd=None, tm=None, n_map=_same, behind=()):
    nbn, T, nw = dy.shape
    nbk, _, kw, _ = w.shape
    assert nbk == 1
    tm = min(tm or RESIDENT_TOKEN_TILE, T)
    has_res = res is not None
    n_in = 2 + has_res + (3 if ln_bwd else 0) + len(behind)

    def body(*refs):
        dy_ref, w_ref = refs[:2]
        res_ref = refs[2] if has_res else None
        y = _dot_nt(dy_ref[0], w_ref[0, n_map(0)])
        for n in range(1, nbn):
            y = y + _dot_nt(dy_ref[n], w_ref[0, n_map(n)])
        if has_res:
            y = y + res_scale * res_ref[0].astype(F32)
        if ln_bwd is None:
            refs[n_in][0] = y
        else:
            xh_ref, rs_ref, g_ref = refs[2 + has_res:5 + has_res]
            o_ref, dg_ref, db_ref = refs[n_in:n_in + 3]
            out, dg, db = _ln_bwd_rows(y, xh_ref[0], rs_ref[...], g_ref[...])
            o_ref[0] = out
            _accumulate(pl.program_id(0), (dg_ref, db_ref), (dg, db))

    tok = pl.BlockSpec((1, tm, kw), lambda i: (0, i, 0))
    vec = pl.BlockSpec((1, kw), lambda i: (0, 0))
    in_specs = [pl.BlockSpec((nbn, tm, nw), lambda i: (0, i, 0)),
                pl.BlockSpec(w.shape, lambda i: (0, 0, 0, 0))]
    args = [dy, w]
    if has_res:
        in_specs.append(tok)
        args.append(res)
    out_specs, out_shape = tok, jax.ShapeDtypeStruct((1, T, kw), F32)
    if ln_bwd is not None:
        in_specs += [tok, pl.BlockSpec((tm, 1), lambda i: (i, 0)), vec]
        args += list(ln_bwd)
        out_specs = [tok, vec, vec]
        out_shape = [out_shape, jax.ShapeDtypeStruct((1, kw), F32), jax.ShapeDtypeStruct((1, kw), F32)]
    in_specs += [pl.BlockSpec(memory_space=pl.ANY)] * len(behind)
    args += list(behind)
    return pl.pallas_call(body, name=name, grid=(T // tm,), in_specs=in_specs, out_specs=out_specs,
                          out_shape=out_shape, compiler_params=_params())(*args)


def _mm_tn(a, dy, *, name, tm=None, n_map=_same):
    nbk, T, kw = a.shape
    nbn, _, nw = dy.shape
    tm = min(tm or REDUCE_TOKEN_TILE, T)
    nt = T // tm

    def body(a_ref, dy_ref, o_ref, acc_ref):
        i = pl.program_id(2)
        part = _dot_tn(a_ref[...], dy_ref[...])

        @pl.when(i == 0)
        def _():
            acc_ref[...] = part

        @pl.when(i > 0)
        def _():
            acc_ref[...] += part

        @pl.when(i == nt - 1)
        def _():
            o_ref[...] = acc_ref[...].astype(o_ref.dtype)

    return pl.pallas_call(body, name=name, grid=(nbk, nbn, nt),
                          in_specs=[pl.BlockSpec((None, tm, kw), lambda k, n, i: (k, i, 0)),
                                    pl.BlockSpec((None, tm, nw), lambda k, n, i: (n, i, 0))],
                          out_specs=pl.BlockSpec((None, None, kw, nw), lambda k, n, i: (k, n_map(n), 0, 0)),
                          out_shape=jax.ShapeDtypeStruct((nbk, nbn, kw, nw), GRAD_DTYPE),
                          scratch_shapes=[pltpu.VMEM((kw, nw), F32)],
                          compiler_params=_params())(a, dy)


def _ln_bwd_rows(dh, xh, rstd, g):
    dxh = dh * g
    m1 = jnp.mean(dxh, axis=-1, keepdims=True)
    m2 = jnp.mean(dxh * xh, axis=-1, keepdims=True)
    dy = rstd * (dxh - m1 - xh * m2)
    return dy, jnp.sum(dh * xh, axis=0, keepdims=True), jnp.sum(dh, axis=0, keepdims=True)


def _accumulate(i, refs, parts):
    @pl.when(i == 0)
    def _():
        for r, p in zip(refs, parts):
            r[...] = jnp.zeros_like(r) + p

    @pl.when(i > 0)
    def _():
        for r, p in zip(refs, parts):
            r[...] += p


def _loss_ln_bwd(tgt, xhat, rstd, g, b, *, name, tm=None):
    _, T, D = xhat.shape
    tm = min(tm or TOKEN_TILE, T)

    def body(t_ref, xh_ref, rs_ref, g_ref, b_ref, dy_ref, dg_ref, db_ref, loss_ref):
        i = pl.program_id(0)
        xh = xh_ref[...]
        err = xh * g_ref[...] + b_ref[...] - t_ref[...]
        part = 0.5 * jnp.sum(jnp.mean(err * err, axis=-1, keepdims=True), axis=0, keepdims=True)
        dy, dg, db = _ln_bwd_rows(err / D, xh, rs_ref[...], g_ref[...])
        dy_ref[...] = dy
        _accumulate(i, (dg_ref, db_ref, loss_ref), (dg, db, part))

    tok = pl.BlockSpec((None, tm, D), lambda i: (0, i, 0))
    vec = pl.BlockSpec((1, D), lambda i: (0, 0))
    return pl.pallas_call(body, name=name, grid=(T // tm,),
                          in_specs=[tok, tok, pl.BlockSpec((tm, 1), lambda i: (i, 0)), vec, vec],
                          out_specs=[tok, vec, vec, pl.BlockSpec((SUBLANES, LANES), lambda i: (0, 0))],
                          out_shape=[jax.ShapeDtypeStruct((1, T, D), F32), jax.ShapeDtypeStruct((1, D), F32),
                                     jax.ShapeDtypeStruct((1, D), F32), jax.ShapeDtypeStruct((SUBLANES, LANES), F32)],
                          compiler_params=_params())(tgt, xhat, rstd, g, b)


def _shift_rows(ext, k, n, halo):
    if k == 0:
        return ext[halo:halo + n]
    return pltpu.roll(ext, k, axis=0)[halo:halo + n]


def _conv_rows(ext, cw_ref, n, halo):
    return (cw_ref[0:1, :] * _shift_rows(ext, 2, n, halo) + cw_ref[1:2, :] * _shift_rows(ext, 1, n, halo)
            + cw_ref[2:3, :] * ext[halo:halo + n] + cw_ref[3:4, :])


def _pair_map(n):
    return n // 2 + 4 * (n % 2)


def _ffn_up(hb, w_in, cw, *, name, tm=None):
    _, T, D = hb.shape
    _, nb, _, fb = w_in.shape
    half = nb // 2
    tm = min(tm or FFN_TOKEN_TILE, T)

    def body(h_ref, wa_ref, wb_ref, cwa_ref, cwb_ref, u_ref, ab_ref, act_ref, carry):
        @pl.when(pl.program_id(1) == 0)
        def _():
            carry[...] = jnp.zeros_like(carry)

        h = h_ref[...]
        conv = []
        for s, (w_ref, cw_ref) in enumerate(((wa_ref, cwa_ref), (wb_ref, cwb_ref))):
            uf = _dot(h, w_ref[...])
            u_ref[s] = uf.astype(u_ref.dtype)
            ext = jnp.concatenate([carry[s], uf], axis=0)
            c = _conv_rows(ext, cw_ref, tm, SUBLANES)
            ab_ref[s] = c.astype(ab_ref.dtype)
            conv.append(c)
            carry[s] = uf[tm - SUBLANES:tm]
        a, b = conv
        act_ref[...] = (a * _sigmoid(a) * b).astype(act_ref.dtype)

    wspec = lambda off: pl.BlockSpec((None, None, D, fb), lambda p, i: (0, p + off, 0, 0))
    cws = lambda off: pl.BlockSpec((None, SUBLANES, fb), lambda p, i: (p + off, 0, 0))
    return pl.pallas_call(body, name=name, grid=(half, T // tm),
                          in_specs=[pl.BlockSpec((None, tm, D), lambda p, i: (0, i, 0)), wspec(0), wspec(half),
                                    cws(0), cws(half)],
                          out_specs=[pl.BlockSpec((None, 2, tm, fb), lambda p, i: (p, 0, i, 0)),
                                     pl.BlockSpec((None, 2, tm, fb), lambda p, i: (p, 0, i, 0)),
                                     pl.BlockSpec((None, tm, fb), lambda p, i: (p, i, 0))],
                          out_shape=[jax.ShapeDtypeStruct((half, 2, T, fb), MXU),
                                     jax.ShapeDtypeStruct((half, 2, T, fb), MXU),
                                     jax.ShapeDtypeStruct((half, T, fb), MXU)],
                          scratch_shapes=[pltpu.VMEM((2, SUBLANES, fb), F32)],
                          compiler_params=_params())(hb, w_in, w_in, cw, cw)


def _ffn_gate_bwd(dy, u, ab, w_out, cw, *, name, tm=None):
    _, T, D = dy.shape
    half, _, _, fb = u.shape
    tm = min(tm or FFN_TOKEN_TILE, T)
    nt = T // tm

    n_full = fb // LANES
    tail = slice(n_full * LANES, fb)

    def body(dy_ref, u_ref, ab_ref, w_ref, cwa_ref, cwb_ref, du_ref, dwo_ref, dcw_ref, carry, acc, gacc):
        i = pl.program_id(1)

        @pl.when(i == 0)
        def _():
            carry[...] = jnp.zeros_like(carry)
            acc[...] = jnp.zeros_like(acc)
            gacc[...] = jnp.zeros_like(gacc)
            dcw_ref[...] = jnp.zeros_like(dcw_ref)

        dyv = dy_ref[...]
        dact = _dot_nt(dyv, w_ref[...])
        a = ab_ref[0].astype(F32)
        b = ab_ref[1].astype(F32)
        sa = _sigmoid(a)
        silu = a * sa
        acc[...] += _dot_tn(silu * b, dyv)
        dcs = (dact * b * (sa * (1.0 + a * (1.0 - sa))), dact * silu)
        m = tm + SUBLANES
        rows = lax.broadcasted_iota(jnp.int32, (SUBLANES, fb), 0)
        for s, cw_ref in enumerate((cwa_ref, cwb_ref)):
            dc = dcs[s]
            nxt = jnp.concatenate([dc, carry[s]], axis=0)
            dc1 = pltpu.roll(nxt, m - 1, axis=0)[:tm]
            dc2 = pltpu.roll(nxt, m - 2, axis=0)[:tm]
            du_ref[s] = (cw_ref[2:3, :] * dc + cw_ref[1:2, :] * dc1 + cw_ref[0:1, :] * dc2).astype(du_ref.dtype)
            carry[s] = dc[0:SUBLANES]
            dcb = [x.astype(MXU) for x in (dc, dc1, dc2)]
            for j in range(n_full):
                blk = slice(j * LANES, (j + 1) * LANES)
                gacc[s, j] += _dot_tn(u_ref[s, :, blk], jnp.concatenate([x[:, blk] for x in dcb], axis=1))
            dcw_ref[s] += jnp.where(rows == 3, jnp.sum(dc, axis=0, keepdims=True), 0.0)
            if fb > n_full * LANES:
                ut = u_ref[s, :, tail].astype(F32)
                gt = [jnp.sum(x[:, tail] * ut, axis=0, keepdims=True) for x in (dc2, dc1, dc)]
                rt = rows[:, tail]
                dcw_ref[s, :, tail] += jnp.where(rt == 0, gt[0], jnp.where(rt == 1, gt[1], jnp.where(rt == 2, gt[2], 0.0)))

        @pl.when(i == nt - 1)
        def _():
            dwo_ref[...] = acc[...].astype(dwo_ref.dtype)
            eye = _tri(LANES, True) & _tri(LANES, False)
            for s in range(2):
                for j in range(n_full):
                    g = gacc[s, j]
                    for tap in range(3):
                        d = jnp.where(eye, g[:, (2 - tap) * LANES:(3 - tap) * LANES], 0.0)
                        dcw_ref[s, tap:tap + 1, j * LANES:(j + 1) * LANES] = jnp.sum(d, axis=0, keepdims=True)

    rev = lambda i: nt - 1 - i
    cws = lambda off: pl.BlockSpec((None, SUBLANES, fb), lambda p, i: (p + off, 0, 0))
    pair = lambda: pl.BlockSpec((None, 2, tm, fb), lambda p, i: (p, 0, rev(i), 0))
    return pl.pallas_call(body, name=name, grid=(half, nt),
                          in_specs=[pl.BlockSpec((None, tm, D), lambda p, i: (0, rev(i), 0)), pair(), pair(),
                                    pl.BlockSpec((None, None, fb, D), lambda p, i: (p, 0, 0, 0)), cws(0), cws(half)],
                          out_specs=[pair(),
                                     pl.BlockSpec((None, None, fb, D), lambda p, i: (p, 0, 0, 0)),
                                     pl.BlockSpec((None, 2, SUBLANES, fb), lambda p, i: (p, 0, 0, 0))],
                          out_shape=[jax.ShapeDtypeStruct((half, 2, T, fb), MXU),
                                     jax.ShapeDtypeStruct((half, 1, fb, D), GRAD_DTYPE),
                                     jax.ShapeDtypeStruct((half, 2, SUBLANES, fb), F32)],
                          scratch_shapes=[pltpu.VMEM((2, SUBLANES, fb), F32), pltpu.VMEM((fb, D), F32),
                                          pltpu.VMEM((2, n_full, LANES, 3 * LANES), F32)],
                          compiler_params=_params())(dy, u, ab, w_out, cw, cw)


def _tri(n, lower):
    r = lax.broadcasted_iota(jnp.int32, (n, n), 0)
    c = lax.broadcasted_iota(jnp.int32, (n, n), 1)
    return (r >= c) if lower else (r <= c)


def _hgrn_gates(zq, zf, lb):
    sq = _sigmoid(zq)
    sf = _sigmoid(zf)
    fg = lb + (1.0 - lb) * sf
    return zq * sq, sq, sf, fg, jnp.log(fg)


def _lb_of(lbl_ref, cols):
    return _sigmoid(lbl_ref[0:1, cols] - lbl_ref[1:2, cols])


def _ones_where(mask):
    return jnp.where(mask, 1.0, 0.0).astype(jnp.bfloat16)


def _hgrn_fwd(z, lbl, gw, *, name):
    _, T, zw = z.shape
    C = min(HG_CHUNK, T)
    nch = T // C
    S = HG_STEP_CHUNKS if nch % HG_STEP_CHUNKS == 0 else 1
    hpb = zw // HG_DIM

    def body(z_ref, lbl_ref, gw_ref, og_ref, st_ref, s_scr, bc_scr, q_scr, k_scr):
        c = pl.program_id(0)

        @pl.when(c == 0)
        def _():
            s_scr[...] = jnp.zeros_like(s_scr)

        low = _tri(C, True)
        low01 = _ones_where(low)
        gwv = gw_ref[...]
        H = range(HG_HEADS)
        col = lambda h: slice(h * HG_DIM, (h + 1) * HG_DIM)
        for sub in range(S):
            chunk_body(z_ref, lbl_ref, og_ref, st_ref, s_scr, bc_scr, q_scr, k_scr,
                       sub, slice(sub * C, (sub + 1) * C), low, low01, gwv, H, col)

    def chunk_body(z_ref, lbl_ref, og_ref, st_ref, s_scr, bc_scr, q_scr, k_scr, sub, rows, low, low01, gwv, H, col):
        for blk in range(2):
            cols = slice(blk * zw, (blk + 1) * zw)
            qq, _, _, fg, lf = _hgrn_gates(z_ref[blk, rows], z_ref[2 + blk, rows], _lb_of(lbl_ref, cols))
            q_scr[:, cols] = qq
            k_scr[:, cols] = 1.0 - fg
            bc_scr[:, cols] = _exact_dot(low01, lf)
        zcol = lambda part, h: (part + h // hpb, rows, slice((h % hpb) * HG_DIM, (h % hpb + 1) * HG_DIM))
        b = [bc_scr[:, col(h)] for h in H]
        bm = [bc_scr[C // 2 - 1:C // 2, col(h)] for h in H]
        bl = [bc_scr[C - 1:C, col(h)] for h in H]
        q_ = [q_scr[:, col(h)] for h in H]
        k_ = [k_scr[:, col(h)] for h in H]
        v_ = [z_ref[zcol(4, h)] for h in H]
        qt = [q_[h] * jnp.exp(jnp.minimum(b[h] - bm[h], EXP_CLAMP)) for h in H]
        kt = [k_[h] * jnp.exp(jnp.minimum(bm[h] - b[h], EXP_CLAMP)) for h in H]
        A = [jnp.where(low, _dot_nt(qt[h], kt[h]), 0.0) for h in H]
        for h in H:
            st_ref[sub, h] = s_scr[h]
        o = [_dot_nt(q_[h] * jnp.exp(b[h]), s_scr[h]) + _dot(A[h], v_[h]) for h in H]
        for h in H:
            s_scr[h] = s_scr[h] * jnp.exp(bl[h]) + _dot_tn(v_[h], k_[h] * jnp.exp(bl[h] - b[h]))
        for h in H:
            g_h = z_ref[zcol(6, h)]
            r = lax.rsqrt(jnp.mean(o[h] * o[h], axis=-1, keepdims=True) + RMS_EPS)
            og_ref[rows, col(h)] = (o[h] * r * gwv * (g_h * _sigmoid(g_h))).astype(og_ref.dtype)

    return pl.pallas_call(body, name=name, grid=(nch // S,),
                          in_specs=[pl.BlockSpec((8, S * C, zw), lambda c: (0, c, 0)),
                                    pl.BlockSpec((2, D_MODEL), lambda c: (0, 0)),
                                    pl.BlockSpec((1, HG_DIM), lambda c: (0, 0))],
                          out_specs=[pl.BlockSpec((None, S * C, D_MODEL), lambda c: (0, c, 0)),
                                     pl.BlockSpec((S, HG_HEADS, HG_DIM, HG_DIM), lambda c: (c, 0, 0, 0))],
                          out_shape=[jax.ShapeDtypeStruct((1, T, D_MODEL), MXU),
                                     jax.ShapeDtypeStruct((nch, HG_HEADS, HG_DIM, HG_DIM), F32)],
                          scratch_shapes=[pltpu.VMEM((HG_HEADS, HG_DIM, HG_DIM), F32)]
                          + [pltpu.VMEM((C, D_MODEL), F32)] * 3,
                          compiler_params=_params())(z, lbl, gw)


def _hgrn_bwd(z, dog, states, lbl, gw, *, name):
    _, T, zw = z.shape
    C = min(HG_CHUNK, T)
    nch = T // C
    S = HG_STEP_CHUNKS if nch % HG_STEP_CHUNKS == 0 else 1
    hpb = zw // HG_DIM

    def body(z_ref, dog_ref, st0_ref, st1_ref, lbl_ref, gw_ref, dz_ref, dlb_ref, dgw_ref,
             d_scr, bc_scr, q_scr, sf_scr, fg_scr, x_scr):
        step = pl.program_id(0)

        @pl.when(step == 0)
        def _():
            d_scr[...] = jnp.zeros_like(d_scr)
            dlb_ref[...] = jnp.zeros_like(dlb_ref)
            dgw_ref[...] = jnp.zeros_like(dgw_ref)

        low = _tri(C, True)
        low01 = _ones_where(low)
        up01 = _ones_where(_tri(C, False))
        gwv = gw_ref[...]
        H = range(HG_HEADS)
        col = lambda h: slice(h * HG_DIM, (h + 1) * HG_DIM)
        for sub in reversed(range(S)):
            chunk_body(z_ref, dog_ref, st0_ref, st1_ref, lbl_ref, dz_ref, dlb_ref, dgw_ref,
                       d_scr, bc_scr, q_scr, sf_scr, fg_scr, x_scr,
                       sub, slice(sub * C, (sub + 1) * C), low, low01, up01, gwv, H, col)

    def chunk_body(z_ref, dog_ref, st0_ref, st1_ref, lbl_ref, dz_ref, dlb_ref, dgw_ref,
                   d_scr, bc_scr, q_scr, sf_scr, fg_scr, x_scr, sub, rows, low, low01, up01, gwv, H, col):
        st0 = lambda h: st0_ref[sub, h]
        st1 = (lambda h: st0_ref[sub + 1, h]) if sub + 1 < S else (lambda h: st1_ref[h])
        for blk in range(2):
            lbb = _lb_of(lbl_ref, slice(blk * zw, (blk + 1) * zw))
            qq, sq, sf, fg, lf = _hgrn_gates(z_ref[blk, rows], z_ref[2 + blk, rows], lbb)
            q_scr[:, blk * zw:(blk + 1) * zw] = qq
            sf_scr[:, blk * zw:(blk + 1) * zw] = sf
            fg_scr[:, blk * zw:(blk + 1) * zw] = fg
            bc_scr[:, blk * zw:(blk + 1) * zw] = _exact_dot(low01, lf)
        zcol = lambda part, h: (part + h // hpb, rows, slice((h % hpb) * HG_DIM, (h % hpb + 1) * HG_DIM))
        b = [bc_scr[:, col(h)] for h in H]
        bm = [bc_scr[C // 2 - 1:C // 2, col(h)] for h in H]
        bl = [bc_scr[C - 1:C, col(h)] for h in H]
        q_ = [q_scr[:, col(h)] for h in H]
        k_ = [1.0 - fg_scr[:, col(h)] for h in H]
        v_ = [z_ref[zcol(4, h)] for h in H]
        eq = [jnp.exp(jnp.minimum(b[h] - bm[h], EXP_CLAMP)) for h in H]
        ek = [jnp.exp(jnp.minimum(bm[h] - b[h], EXP_CLAMP)) for h in H]
        eb = [jnp.exp(b[h]) for h in H]
        el = [jnp.exp(bl[h] - b[h]) for h in H]
        qt = [q_[h] * eq[h] for h in H]
        kt = [k_[h] * ek[h] for h in H]
        q0 = [q_[h] * eb[h] for h in H]
        kd = [k_[h] * el[h] for h in H]
        A = [jnp.where(low, _dot_nt(qt[h], kt[h]), 0.0) for h in H]
        o = [_dot_nt(q0[h], st0(h)) + _dot(A[h], v_[h]) for h in H]
        do = []
        dgw_acc = jnp.zeros((1, HG_DIM), F32)
        for h in H:
            g_h = z_ref[zcol(6, h)]
            r = lax.rsqrt(jnp.mean(o[h] * o[h], axis=-1, keepdims=True) + RMS_EPS)
            on = o[h] * r
            sg = _sigmoid(g_h)
            dogh = dog_ref[rows, col(h)].astype(F32)
            t1 = dogh * on
            dgw_acc = dgw_acc + jnp.sum(t1 * (g_h * sg), axis=0, keepdims=True)
            dz_ref[zcol(6, h)] = (t1 * gwv * (sg * (1.0 + g_h * (1.0 - sg)))).astype(dz_ref.dtype)
            don = dogh * gwv * (g_h * sg)
            do.append(r * (don - on * jnp.mean(don * on, axis=-1, keepdims=True)))
        dgw_ref[...] += dgw_acc
        P = [jnp.where(low, _dot_nt(do[h], v_[h]), 0.0) for h in H]
        dqq = [eb[h] * _dot(do[h], st0(h)) + eq[h] * _dot_hp(P[h], kt[h], ((1,), (0,))) for h in H]
        dkk = [el[h] * _dot(v_[h], d_scr[h]) + ek[h] * _dot_hp(P[h], qt[h], ((0,), (0,))) for h in H]
        for h in H:
            dz_ref[zcol(4, h)] = (_dot_nt(kd[h], d_scr[h]) + _dot_tn(A[h], do[h])).astype(dz_ref.dtype)
            x_scr[:, col(h)] = q_[h] * dqq[h] - k_[h] * dkk[h]
        edge = [jnp.sum(d_scr[h] * st1(h), axis=0, keepdims=True) for h in H]
        for h in H:
            d_scr[h] = d_scr[h] * jnp.exp(bl[h]) + _dot_tn(do[h], q0[h])
        for blk in range(2):
            x_scr[:, blk * zw:(blk + 1) * zw] = _exact_dot(up01, x_scr[:, blk * zw:(blk + 1) * zw])
        dlb = []
        for h in H:
            dfg = (x_scr[:, col(h)] + edge[h]) / fg_scr[:, col(h)] - dkk[h]
            sf_h = sf_scr[:, col(h)]
            lb_h = _lb_of(lbl_ref, col(h))
            zq_h = z_ref[zcol(0, h)]
            sq_h = _sigmoid(zq_h)
            dlb.append(jnp.sum(dfg * (1.0 - sf_h), axis=0, keepdims=True))
            dz_ref[zcol(0, h)] = (dqq[h] * (sq_h * (1.0 + zq_h * (1.0 - sq_h)))).astype(dz_ref.dtype)
            dz_ref[zcol(2, h)] = (dfg * (1.0 - lb_h) * sf_h * (1.0 - sf_h)).astype(dz_ref.dtype)
        dlb_ref[...] += jnp.concatenate(dlb, axis=1)

    nst = nch // S
    rev = lambda s: nst - 1 - s
    return pl.pallas_call(body, name=name, grid=(nst,),
                          in_specs=[pl.BlockSpec((8, S * C, zw), lambda s: (0, rev(s), 0)),
                                    pl.BlockSpec((None, S * C, D_MODEL), lambda s: (0, rev(s), 0)),
                                    pl.BlockSpec((S, HG_HEADS, HG_DIM, HG_DIM), lambda s: (rev(s), 0, 0, 0)),
                                    pl.BlockSpec((None, HG_HEADS, HG_DIM, HG_DIM),
                                                 lambda s: (jnp.minimum((rev(s) + 1) * S, nch - 1), 0, 0, 0)),
                                    pl.BlockSpec((2, D_MODEL), lambda s: (0, 0)),
                                    pl.BlockSpec((1, HG_DIM), lambda s: (0, 0))],
                          out_specs=[pl.BlockSpec((8, S * C, zw), lambda s: (0, rev(s), 0)),
                                     pl.BlockSpec((1, D_MODEL), lambda s: (0, 0)),
                                     pl.BlockSpec((1, HG_DIM), lambda s: (0, 0))],
                          out_shape=[jax.ShapeDtypeStruct((8, T, zw), MXU), jax.ShapeDtypeStruct((1, D_MODEL), F32),
                                     jax.ShapeDtypeStruct((1, HG_DIM), F32)],
                          scratch_shapes=[pltpu.VMEM((HG_HEADS, HG_DIM, HG_DIM), F32)]
                          + [pltpu.VMEM((C, D_MODEL), F32)] * 5,
                          compiler_params=_params())(z, dog, states, states, lbl, gw)


def _bucket_onehot():
    W = SW_WINDOW
    t = np.arange(W)[:, None] + W
    s = np.arange(2 * W)[None, :]
    dist = t - s
    exact = REL_BUCKETS // 2
    d = np.maximum(np.maximum(dist, 0), 1).astype(np.float32)
    log_b = exact + (np.log(d / np.float32(exact)) / np.float32(math.log(REL_MAX_DIST / exact))
                     * np.float32(REL_BUCKETS - exact)).astype(np.int32)
    bucket = np.where(np.maximum(dist, 0) < exact, np.maximum(dist, 0), np.minimum(log_b, REL_BUCKETS - 1))
    valid = (dist >= 0) & (dist < W)
    onehot = (bucket[..., None] == np.arange(REL_BUCKETS)) & valid[..., None]
    return onehot.reshape(W * 2 * W, REL_BUCKETS).astype(np.float32)


def _bias_expand(rel_t, onehot_t, *, name):
    hq, nbk = rel_t.shape
    n = onehot_t.shape[1]

    def body(r_ref, oh_ref, o_ref):
        o_ref[...] = _exact_dot_r(r_ref[...], oh_ref[...])

    return pl.pallas_call(body, name=name, out_shape=jax.ShapeDtypeStruct((hq, n), F32),
                          compiler_params=_params())(rel_t, onehot_t)


def _bias_reduce(dbias, onehot, *, name):
    hq = dbias.shape[0]
    nbk = onehot.shape[1]

    def body(d_ref, oh_ref, o_ref):
        o_ref[...] = _exact_dot_r(d_ref[...], oh_ref[...])

    return pl.pallas_call(body, name=name, out_shape=jax.ShapeDtypeStruct((hq, nbk), F32),
                          compiler_params=_params())(dbias, onehot)


def _swa_mask(j):
    W = SW_WINDOW
    t = lax.broadcasted_iota(jnp.int32, (W, 2 * W), 0) + W
    s = lax.broadcasted_iota(jnp.int32, (W, 2 * W), 1)
    dist = t - s
    band = (dist >= 0) & (dist < W)
    m = band & ((j > 0) | (s >= W))
    return jnp.concatenate([m] * SW_GROUP, axis=0)


def _half_mask(rows, half):
    lane = lax.broadcasted_iota(jnp.int32, (rows, LANES), 1)
    return (lane >= SW_HEAD_DIM) if half else (lane < SW_HEAD_DIM)


_ALL = slice(None)


def _swa_head(ref, col0, head, to_half, rows=_ALL):
    slab, half = head // 2, head % 2
    x = ref[rows, col0 + slab * LANES:col0 + (slab + 1) * LANES]
    x = jnp.where(_half_mask(x.shape[0], half), x, 0.0)
    return x if half == to_half else pltpu.roll(x, SW_HEAD_DIM, axis=1)


def _swa_stack(ref, g, rows=_ALL):
    return jnp.concatenate([_swa_head(ref, 0, g * SW_GROUP + r, g % 2, rows) for r in range(SW_GROUP)], axis=0)


def _swa_unstack(ref, x, g, rows=_ALL):
    W = SW_WINDOW
    for pair in range(SW_GROUP // 2):
        parts = []
        for r in (2 * pair, 2 * pair + 1):
            piece = x[r * W:(r + 1) * W]
            parts.append(piece if r % 2 == g % 2 else pltpu.roll(piece, SW_HEAD_DIM, axis=1))
        slab = (g * SW_GROUP) // 2 + pair
        ref[rows, slab * LANES:(slab + 1) * LANES] = (parts[0] + parts[1]).astype(ref.dtype)


def _swa_kv(kp_ref, kc_ref, col0, g, prev_rows=_ALL, rows=_ALL):
    return jnp.concatenate([_swa_head(kp_ref, col0, g, g % 2, prev_rows), _swa_head(kc_ref, col0, g, g % 2, rows)],
                           axis=0)


def _lane_pick(tile, h):
    lane = lax.broadcasted_iota(jnp.int32, tile.shape, 1)
    return jnp.sum(jnp.where(lane == h, tile, 0.0), axis=-1, keepdims=True)


def _lane_put(tile, h, col):
    lane = lax.broadcasted_iota(jnp.int32, tile.shape, 1)
    return jnp.where(lane == h, col, tile)


def _swa_rows(vals):
    return jnp.concatenate([jnp.broadcast_to(v, (SW_WINDOW, 1)) for v in vals], axis=0)


def _swa_fwd(q, kv, bias, sinks, *, name):
    _, T, D = q.shape
    W = SW_WINDOW
    nb = T // W
    dh = SW_HEAD_DIM
    kvw = SW_KV_HEADS * dh
    scale = dh ** -0.5

    S = SW_STEP_BLOCKS if nb % SW_STEP_BLOCKS == 0 else 1

    def body(q_ref, kc_ref, kp_ref, bias_ref, sink_ref, o_ref, lse_ref):
        c = pl.program_id(0)
        sk = sink_ref[...]
        rows = [slice(s * W, (s + 1) * W) for s in range(S)]
        before = [(kp_ref, _ALL)] + [(kc_ref, rows[s - 1]) for s in range(1, S)]
        masks = [_swa_mask(c * S + s) for s in range(S)]
        I = [(s, g) for s in range(S) for g in range(SW_KV_HEADS)]
        kv_of = lambda col0, s, g: _swa_kv(before[s][0], kc_ref, col0, g, before[s][1], rows[s])
        kk = [kv_of(0, s, g) for s, g in I]
        vv = [kv_of(kvw, s, g) for s, g in I]
        qs = [_swa_stack(q_ref, g, rows[s]) for s, g in I]
        bias_of = lambda g: bias_ref[g * SW_GROUP:(g + 1) * SW_GROUP].reshape(SW_GROUP * W, 2 * W)
        logits = [jnp.where(masks[s], _dot_nt(qs[n], kk[n]) * scale + bias_of(g), NEG_BIG) for n, (s, g) in enumerate(I)]
        sink = [_swa_rows([_lane_pick(sk, g * SW_GROUP + r) for r in range(SW_GROUP)]) for s, g in I]
        m = [jnp.maximum(jnp.max(logits[n], axis=-1, keepdims=True), sink[n]) for n in range(len(I))]
        p = [jnp.exp(logits[n] - m[n]) for n in range(len(I))]
        den = [jnp.sum(p[n], axis=-1, keepdims=True) + jnp.exp(sink[n] - m[n]) for n in range(len(I))]
        pv = [_dot(p[n], vv[n]) for n in range(len(I))]
        lse_tiles = [jnp.zeros((W, SW_Q_HEADS), F32) for _ in range(S)]
        for n, (s, g) in enumerate(I):
            _swa_unstack(o_ref, pv[n] / den[n], g, rows[s])
            lse = m[n] + jnp.log(den[n])
            for r in range(SW_GROUP):
                lse_tiles[s] = _lane_put(lse_tiles[s], g * SW_GROUP + r, lse[r * W:(r + 1) * W])
        for s in range(S):
            lse_ref[rows[s], :] = lse_tiles[s]

    return pl.pallas_call(body, name=name, grid=(nb // S,),
                          in_specs=[pl.BlockSpec((None, S * W, D), lambda j: (0, j, 0)),
                                    pl.BlockSpec((None, S * W, 2 * kvw), lambda j: (0, j, 0)),
                                    pl.BlockSpec((None, W, 2 * kvw), lambda j: (0, jnp.maximum(j * S - 1, 0), 0)),
                                    pl.BlockSpec((SW_Q_HEADS, W, 2 * W), lambda j: (0, 0, 0)),
                                    pl.BlockSpec((1, SW_Q_HEADS), lambda j: (0, 0))],
                          out_specs=[pl.BlockSpec((None, S * W, D), lambda j: (0, j, 0)),
                                     pl.BlockSpec((S * W, SW_Q_HEADS), lambda j: (j, 0))],
                          out_shape=[jax.ShapeDtypeStruct((1, T, D), F32), jax.ShapeDtypeStruct((T, SW_Q_HEADS), F32)],
                          compiler_params=_params())(q, kv, kv, bias, sinks)


def _swa_bwd(q, kv, o, lse, do, bias, sinks, *, name):
    _, T, D = q.shape
    W = SW_WINDOW
    nb = T // W
    dh = SW_HEAD_DIM
    kvw = SW_KV_HEADS * dh
    scale = dh ** -0.5
    cl = lambda j: jnp.minimum(j, nb - 1)

    def body(q_ref, kc_ref, kp_ref, o_ref, lse_ref, do_ref, bias_ref, sink_ref,
             dq_ref, dkv_ref, dbias_ref, dsink_ref, carry):
        j = pl.program_id(0)

        @pl.when(j == 0)
        def _():
            carry[...] = jnp.zeros_like(carry)
            dbias_ref[...] = jnp.zeros_like(dbias_ref)
            dsink_ref[...] = jnp.zeros_like(dsink_ref)

        @pl.when(j < nb)
        def _():
            mask = _swa_mask(j)
            sk = sink_ref[...]
            lse_tile = lse_ref[...]
            dsink = jnp.zeros((1, SW_Q_HEADS), F32)
            G = range(SW_KV_HEADS)
            heads = [[g * SW_GROUP + r for r in range(SW_GROUP)] for g in G]
            kk = [_swa_kv(kp_ref, kc_ref, 0, g) for g in G]
            vv = [_swa_kv(kp_ref, kc_ref, kvw, g) for g in G]
            qs = [_swa_stack(q_ref, g) for g in G]
            dos = [_swa_stack(do_ref, g) for g in G]
            lse = [jnp.concatenate([_lane_pick(lse_tile, h) for h in heads[g]], axis=0) for g in G]
            sink = [_swa_rows([_lane_pick(sk, h) for h in heads[g]]) for g in G]
            logits = [jnp.where(mask, _dot_nt(qs[g], kk[g]) * scale
                                + bias_ref[g * SW_GROUP:(g + 1) * SW_GROUP].reshape(SW_GROUP * W, 2 * W), NEG_BIG)
                      for g in G]
            dp = [_dot_nt(dos[g], vv[g]) for g in G]
            p = [jnp.exp(logits[g] - lse[g]) for g in G]
            delta = [jnp.sum(dos[g] * _swa_stack(o_ref, g), axis=-1, keepdims=True) for g in G]
            dl = [p[g] * (dp[g] - delta[g]) for g in G]
            dqs = [_dot(dl[g], kk[g]) * scale for g in G]
            dks = [_dot_tn(dl[g], qs[g]) * scale for g in G]
            dvs = [_dot_tn(p[g], dos[g]) for g in G]
            for g in G:
                _swa_unstack(dq_ref, dqs[g], g)
                dbias_ref[g * SW_GROUP:(g + 1) * SW_GROUP] += dl[g].reshape(SW_GROUP, W, 2 * W)
                sd = jnp.exp(sink[g] - lse[g]) * delta[g]
                for r, h in enumerate(heads[g]):
                    dsink = _lane_put(dsink, h, -jnp.sum(sd[r * W:(r + 1) * W], axis=0, keepdims=True))
            dsink_ref[...] += dsink
            for slab in range(SW_KV_HEADS // 2):
                for col0, parts in ((0, dks), (kvw, dvs)):
                    both = parts[2 * slab] + parts[2 * slab + 1]
                    cols = slice(col0 + slab * LANES, col0 + (slab + 1) * LANES)
                    dkv_ref[:, cols] = (carry[:, cols] + both[:W]).astype(dkv_ref.dtype)
                    carry[:, cols] = both[W:]

        @pl.when(j == nb)
        def _():
            dkv_ref[...] = carry[...].astype(dkv_ref.dtype)

    tok = lambda w: pl.BlockSpec((None, W, w), lambda j: (0, cl(j), 0))
    return pl.pallas_call(body, name=name, grid=(nb + 1,),
                          in_specs=[tok(D), tok(2 * kvw),
                                    pl.BlockSpec((None, W, 2 * kvw), lambda j: (0, jnp.maximum(cl(j) - 1, 0), 0)),
                                    tok(D), pl.BlockSpec((W, SW_Q_HEADS), lambda j: (cl(j), 0)), tok(D),
                                    pl.BlockSpec((SW_Q_HEADS, W, 2 * W), lambda j: (0, 0, 0)),
                                    pl.BlockSpec((1, SW_Q_HEADS), lambda j: (0, 0))],
                          out_specs=[tok(D),
                                     pl.BlockSpec((None, W, 2 * kvw), lambda j: (0, jnp.maximum(j - 1, 0), 0)),
                                     pl.BlockSpec((SW_Q_HEADS, W, 2 * W), lambda j: (0, 0, 0)),
                                     pl.BlockSpec((1, SW_Q_HEADS), lambda j: (0, 0))],
                          out_shape=[jax.ShapeDtypeStruct((1, T, D), MXU), jax.ShapeDtypeStruct((1, T, 2 * kvw), MXU),
                                     jax.ShapeDtypeStruct((SW_Q_HEADS, W, 2 * W), F32),
                                     jax.ShapeDtypeStruct((1, SW_Q_HEADS), F32)],
                          scratch_shapes=[pltpu.VMEM((W, 2 * kvw), F32)],
                          compiler_params=_params())(q, kv, kv, o, lse, do, bias, sinks)


_HBM = pl.BlockSpec(memory_space=pltpu.HBM)
_SEM = pl.BlockSpec(memory_space=pltpu.SEMAPHORE)
_EFFECT = pltpu.SideEffectType.DATAFLOW_SIDE_EFFECTING
N_PEERS = N_DEV - 1


def _peer(k):
    x, y, c = lax.axis_index("x"), lax.axis_index("y"), lax.axis_index("c")
    px = (x + (k >> 2)) % 2
    py = (y + ((k >> 1) & 1)) % 2
    pc = (c + (k & 1)) % 2
    return (px, py, pc), 4 * px + 2 * py + pc


def _my_number():
    return 4 * lax.axis_index("x") + 2 * lax.axis_index("y") + lax.axis_index("c")


def _landing(src, mode):
    me = _my_number()
    own = src if mode == "gather" else lax.dynamic_index_in_dim(src, me, 0, keepdims=False)
    return lax.dynamic_update_index_in_dim(lax.empty((N_DEV,) + own.shape, own.dtype), own, me, 0)


def _copy(src_ref, land_ref, mode, send, recv, j, k, dst_slot):
    peer, pid = _peer(k)
    return pltpu.make_async_remote_copy(
        src_ref=src_ref if mode == "gather" else src_ref.at[pid], dst_ref=land_ref.at[dst_slot(pid)],
        send_sem=send.at[j * N_PEERS + k - 1], recv_sem=recv.at[j * N_PEERS + k - 1],
        device_id=peer, device_id_type=pl.DeviceIdType.MESH)


def _send_start(groups, *, name):
    flat = [t for g in groups for t in g]
    n, ng = len(flat), len(groups)
    srcs = [pltpu.with_memory_space_constraint(s, pltpu.HBM) for s, _ in flat]
    lands = [pltpu.with_memory_space_constraint(_landing(s, m), pltpu.HBM) for s, m in flat]

    def body(*refs):
        src_refs, land_refs = refs[:n], refs[n:2 * n]
        sems = refs[2 * n:2 * n + 2 * ng]
        token = refs[-1]
        me = _my_number()
        a = 0
        for gi, g in enumerate(groups):
            for j, (_, mode) in enumerate(g):
                for k in range(1, N_DEV):
                    _copy(src_refs[a], land_refs[a], mode, sems[2 * gi], sems[2 * gi + 1], j, k, lambda pid: me).start()
                a += 1
        token[...] = jnp.zeros_like(token)

    sem_shapes = []
    for g in groups:
        sem_shapes += [pltpu.SemaphoreType.DMA((len(g) * N_PEERS,))] * 2
    out = pl.pallas_call(
        body, name=name,
        out_shape=tuple(sem_shapes) + tuple(pltpu.HBM(a.shape, a.dtype) for a in srcs + lands)
        + (jax.ShapeDtypeStruct((SUBLANES, LANES), F32),),
        in_specs=[_HBM] * (2 * n), out_specs=[_SEM] * (2 * ng) + [_HBM] * (2 * n) + [pl.BlockSpec(memory_space=pltpu.VMEM)],
        input_output_aliases={i: 2 * ng + i for i in range(2 * n)},
        compiler_params=pltpu.CompilerParams(has_side_effects=_EFFECT))(*srcs, *lands)
    sems, thru, token = out[:2 * ng], out[2 * ng:2 * ng + 2 * n], out[-1]
    handles, a = [], 0
    for gi, g in enumerate(groups):
        m = len(g)
        handles.append((sems[2 * gi], sems[2 * gi + 1], list(thru[a:a + m]), list(thru[n + a:n + a + m]),
                        [mode for _, mode in g]))
        a += m
    return handles, token


def _send_wait(handle, after, *, name):
    send, recv, srcs, lands, modes = handle
    m = len(srcs)

    def body(*refs):
        src_refs, land_refs = refs[:m], refs[m:2 * m]
        send_ref, recv_ref = refs[2 * m], refs[2 * m + 1]
        for j in range(m):
            for k in range(1, N_DEV):
                cp = _copy(src_refs[j], land_refs[j], modes[j], send_ref, recv_ref, j, k, lambda pid: pid)
                cp.wait_send()
                cp.wait_recv()

    out = pl.pallas_call(
        body, name=name, out_shape=tuple(pltpu.HBM(a.shape, a.dtype) for a in srcs + lands),
        in_specs=[_HBM] * (2 * m) + [_SEM, _SEM] + [pl.BlockSpec(memory_space=pl.ANY)] * len(after),
        out_specs=[_HBM] * (2 * m), input_output_aliases={i: i for i in range(2 * m)},
        compiler_params=pltpu.CompilerParams(has_side_effects=_EFFECT))(*srcs, *lands, send, recv, *after)
    return list(out[m:])


def _adam_math(w, g, m, v):
    m = ADAM_B1 * m + (1.0 - ADAM_B1) * g
    v = ADAM_B2 * v + (1.0 - ADAM_B2) * (g * g)
    m_hat = m / (1.0 - ADAM_B1 ** ADAM_STEP)
    v_hat = v / (1.0 - ADAM_B2 ** ADAM_STEP)
    delta = -ADAM_LR * (m_hat / (jnp.sqrt(v_hat) + ADAM_EPS) + ADAM_WD * w)
    return delta, m, v


def _adamw(parts, w, m, v, *, name, layer=None):
    S, R, C = parts.shape
    tr = R
    for cand in (256, 128, 64, 32, 16, 8):
        if R % cand == 0 and S * cand * C * 4 <= 4 * 2 ** 20:
            tr = cand
            break

    def body(p_ref, w_ref, m_ref, v_ref, g_ref, d_ref, nm_ref, nv_ref):
        g = p_ref[0].astype(F32)
        for s in range(1, S):
            g = g + p_ref[s].astype(F32)
        delta, nm, nv = _adam_math(w_ref[...], g, m_ref[...], v_ref[...])
        g_ref[...] = g
        d_ref[...] = delta
        nm_ref[...] = nm
        nv_ref[...] = nv

    if layer is None:
        wspec = pl.BlockSpec((tr, C), lambda i: (i, 0))
    else:
        wspec = pl.BlockSpec((None, tr, C), lambda i: (layer, i, 0))
    ospec = pl.BlockSpec((tr, C), lambda i: (i, 0))
    osh = jax.ShapeDtypeStruct((R, C), F32)
    return pl.pallas_call(body, name=name, grid=(R // tr,),
                          in_specs=[pl.BlockSpec((S, tr, C), lambda i: (0, i, 0)), wspec, wspec, wspec],
                          out_specs=[ospec] * 4, out_shape=[osh] * 4, compiler_params=_params())(parts, w, m, v)


def _sum_parts(parts, *, name):
    S, R, C = parts.shape

    def body(p_ref, o_ref):
        g = p_ref[0]
        for s in range(1, S):
            g = g + p_ref[s]
        o_ref[...] = g

    return pl.pallas_call(body, name=name, out_shape=jax.ShapeDtypeStruct((R, C), F32),
                          compiler_params=_params())(parts)


def _pack_rows(arrays):
    pieces, layout, row = [], [], 0
    for a in arrays:
        flat = a.reshape(-1).astype(F32)
        rows = -(-flat.shape[0] // (SUBLANES * LANES)) * SUBLANES
        flat = jnp.pad(flat, (0, rows * LANES - flat.shape[0]))
        pieces.append(flat.reshape(rows, LANES))
        layout.append((row, rows, a.shape))
        row += rows
    return jnp.concatenate(pieces, axis=0), layout


def _unpack_rows(packed, layout):
    out = []
    for row, rows, shape in layout:
        size = int(np.prod(shape))
        out.append(packed[row:row + rows].reshape(-1)[:size].reshape(shape))
    return out


def _ffn_fwd(h, w_in, w_out, cw, ln_g, ln_b, tag):
    h, hb = h
    u, ab, act = _ffn_up(hb, w_in, cw, name=f"ffn_up_{tag}")
    hnb, xh, rs = _mm_nn(act, w_out, res=h, res_scale=ALPHA, ln=(ln_g, ln_b), name=f"ffn_down_{tag}")
    return hnb, xh, rs, (u, ab)


def _ffn_bwd(dy, hb, u, w_in, w_out, cw, ln_bwd, send, tag):
    du, dw_out, dcw = _ffn_gate_bwd(dy, u[0], u[1], w_out, cw, name=f"ffn_gate_bwd_{tag}")
    du = du.reshape((-1,) + du.shape[2:])
    dw_in = _mm_tn(hb, du, n_map=_pair_map, tm=REDUCE_TOKEN_TILE_16BIT, name=f"ffn_dwin_{tag}")
    handle, token = send(dw_in, dw_out)
    dyp, dg, db = _mm_nt_resident(du, w_in, n_map=_pair_map, res=dy, res_scale=ALPHA, ln_bwd=ln_bwd,
                                  behind=(token,), name=f"ffn_dh_{tag}")
    dcw = dcw.transpose(1, 0, 2, 3).reshape((-1,) + dcw.shape[2:])
    return dyp, dg, db, handle, dcw


def kernel(x, hgrn_w_in, hgrn_lb_logits, hgrn_gnorm_w, hgrn_w_out, swa_w_q, swa_sinks, swa_w_out, shared_w_kv, rel_bias, ffn_w_in, ffn_conv_w, ffn_conv_b, ffn_w_out, ln_mix_g, ln_mix_b, ln_ffn_g, ln_ffn_b, loss_target, m_hgrn_w_in, m_hgrn_lb_logits, m_hgrn_gnorm_w, m_hgrn_w_out, m_swa_w_q, m_swa_sinks, m_swa_w_out, m_shared_w_kv, m_rel_bias, m_ffn_w_in, m_ffn_conv_w, m_ffn_conv_b, m_ffn_w_out, m_ln_mix_g, m_ln_mix_b, m_ln_ffn_g, m_ln_ffn_b, v_hgrn_w_in, v_hgrn_lb_logits, v_hgrn_gnorm_w, v_hgrn_w_out, v_swa_w_q, v_swa_sinks, v_swa_w_out, v_shared_w_kv, v_rel_bias, v_ffn_w_in, v_ffn_conv_w, v_ffn_conv_b, v_ffn_w_out, v_ln_mix_g, v_ln_mix_b, v_ln_ffn_g, v_ln_ffn_b):
    T = x.shape[1]
    D = D_MODEL
    W = SW_WINDOW
    fb = ffn_w_in.shape[2]
    me = 4 * lax.axis_index("x") + 2 * lax.axis_index("y") + lax.axis_index("c")

    small_fwd, small_fwd_layout = _pack_rows([hgrn_lb_logits, ffn_conv_w])
    gat = lambda *ws: [(w_.astype(MXU), "gather") for w_ in ws]
    (wait_a, wait_b, wait_c), _ = _send_start(
        [gat(hgrn_w_in[0]) + [(small_fwd, "gather")],
         gat(hgrn_w_out[0], ffn_w_in[0], ffn_w_out[0]),
         gat(shared_w_kv, swa_w_q[0], swa_w_out[0], ffn_w_in[1], ffn_w_out[1])], name="gather_start")
    xb = x.astype(MXU)
    w_hin, small_all = _send_wait(wait_a, (xb,), name="gather_wait_a")
    w_hin = w_hin[None]
    ffn_rows = 2 * ffn_w_out.shape[1]
    (lb_row, lb_rows, _), (cw_row, cw_rows, _) = small_fwd_layout
    lbl = small_all[:, lb_row:lb_row + 2, :].transpose(1, 0, 2).reshape(2, D)
    conv_w_all = small_all[:, cw_row:cw_row + cw_rows, :].reshape(N_DEV, -1)[:, :DEPTH * 3 * fb]
    conv_w_all = conv_w_all.reshape(N_DEV, DEPTH, 3, fb).transpose(1, 0, 2, 3)
    conv_b_all = ffn_conv_b.reshape(DEPTH, N_DEV, 1, fb)
    no_pad = ((0, 0), (0, 0))
    cw = (jnp.pad(conv_w_all, no_pad + ((0, SUBLANES - 3), (0, 0)))
          + jnp.pad(conv_b_all, no_pad + ((3, SUBLANES - 4), (0, 0))))

    row = lambda a, l: a[l:l + 1]

    z = _mm_nn(xb, w_hin, name="hgrn_in")
    og, states = _hgrn_fwd(z, lbl, hgrn_gnorm_w, name="hgrn_rec")
    w_hout, w_fin0, w_fout0 = _send_wait(wait_b, (og,), name="gather_wait_b")
    w_hout = w_hout.reshape(1, 1, D, D)
    w_fin = [w_fin0[None], None]
    w_fout = [w_fout0.reshape(4, 1, ffn_rows, D), None]
    h1b, xh1, rs1 = _mm_nn(og, w_hout, res=x, res_scale=ALPHA, ln=(row(ln_mix_g, 0), row(ln_mix_b, 0)), name="hgrn_out")
    h1 = (xh1, row(ln_mix_g, 0), row(ln_mix_b, 0))
    h2b, xh2, rs2, u0 = _ffn_fwd((h1, h1b), w_fin[0], w_fout[0], cw[0], row(ln_ffn_g, 0), row(ln_ffn_b, 0), "l0")
    h2 = (xh2, row(ln_ffn_g, 0), row(ln_ffn_b, 0))
    w_kv, w_q, w_o, w_fin1, w_fout1 = _send_wait(wait_c, (h2b,), name="gather_wait_c")
    w_kv = w_kv.reshape(1, 1, D, 2 * SW_KV_HEADS * SW_HEAD_DIM)
    w_q = w_q.reshape(1, 1, D, D)
    w_o = w_o.reshape(1, 1, D, D)
    w_fin[1] = w_fin1[None]
    w_fout[1] = w_fout1.reshape(4, 1, ffn_rows, D)
    kv = _mm_nn(h2b, w_kv, name="swa_kv")
    q = _mm_nn(h2b, w_q, name="swa_q")
    onehot = _bucket_onehot()
    bias = _bias_expand(rel_bias.T, jnp.asarray(onehot.T, jnp.bfloat16), name="swa_bias").reshape(SW_Q_HEADS, W, 2 * W)
    ao, lse = _swa_fwd(q, kv, bias, swa_sinks, name="swa_attn")
    h3b, xh3, rs3 = _mm_nn(ao, w_o, res=h2, res_scale=ALPHA, ln=(row(ln_mix_g, 1), row(ln_mix_b, 1)), name="swa_out")
    h3 = (xh3, row(ln_mix_g, 1), row(ln_mix_b, 1))
    _, xh4, rs4, u1 = _ffn_fwd((h3, h3b), w_fin[1], w_fout[1], cw[1], row(ln_ffn_g, 1), row(ln_ffn_b, 1), "l1")
    dy4, dg_f1, db_f1, loss_tile = _loss_ln_bwd(loss_target, xh4, rs4, row(ln_ffn_g, 1), row(ln_ffn_b, 1),
                                                name="loss_ln_ffn1_bwd")
    sc = lambda *gs: [(g_, "scatter") for g_ in gs]

    def send_ffn(name_):
        def send(dw_in, dw_out):
            (handle,), token = _send_start([sc(dw_in.reshape(N_DEV, D, fb), dw_out.reshape(N_DEV, -1, D))], name=name_)
            return handle, token
        return send

    dy3, dg_m1, db_m1, ex1, dcw1 = _ffn_bwd(dy4, h3b, u1, w_fin[1], w_fout[1], cw[1],
                                            (xh3, rs3, row(ln_mix_g, 1)), send_ffn("grads_start_1"), "l1")
    dw_o = _mm_tn(ao, dy3, name="swa_dwo")
    dao = _mm_nt(dy3, w_o, name="swa_dao")
    dq, dkv, dbias, dsinks = _swa_bwd(q, kv, ao, lse, dao, bias, swa_sinks, name="swa_attn_bwd")
    drel_t = _bias_reduce(dbias.reshape(SW_Q_HEADS, W * 2 * W), jnp.asarray(onehot, jnp.bfloat16), name="swa_dbias")
    dw_q = _mm_tn(h2b, dq, name="swa_dwq")
    dw_kv = _mm_tn(h2b, dkv, name="swa_dwkv")
    dh2 = _mm_nt(dq, w_q, res=dy3, res_scale=ALPHA, name="swa_dh_q")
    (ex2,), tok2 = _send_start([sc(dw_o.reshape(N_DEV, D // N_DEV, D), dw_q.reshape(N_DEV, D // N_DEV, D),
                                   dw_kv.reshape(N_DEV, D // N_DEV, -1))], name="grads_start_2")
    dy2, dg_f0, db_f0 = _mm_nt_resident(dkv, w_kv, res=dh2, ln_bwd=(xh2, rs2, row(ln_ffn_g, 0)), behind=(tok2,),
                                        name="swa_dh_kv")
    dy1, dg_m0, db_m0, ex3, dcw0 = _ffn_bwd(dy2, h1b, u0, w_fin[0], w_fout[0], cw[0],
                                            (xh1, rs1, row(ln_mix_g, 0)), send_ffn("grads_start_3"), "l0")
    dw_hout = _mm_tn(og, dy1, name="hgrn_dwout")
    dog = _mm_nt(dy1, w_hout, name="hgrn_dog")
    dz, dlb, dgw = _hgrn_bwd(z, dog, states, lbl, hgrn_gnorm_w, name="hgrn_rec_bwd")
    dw_hin = _mm_tn(xb, dz, tm=REDUCE_TOKEN_TILE_16BIT, name="hgrn_dwin")

    p0 = _sigmoid(lbl[0:1] - lbl[1:2])
    dl0 = dlb * p0 * (1.0 - p0)
    d_lbl = dl0 * jnp.array([[1.0], [-1.0]], F32)
    dcw = jnp.stack([dcw0, dcw1], axis=0)
    d_conv_w = dcw[:, :, 0:3, :]
    d_conv_b = dcw[:, :, 3, :].reshape(DEPTH, N_DEV * fb)
    first_row = lax.broadcasted_iota(jnp.int32, (DEPTH, D), 0) == 0
    two_rows = lambda a, b: jnp.where(first_row, a, b)
    d_ln_mix_g = two_rows(dg_m0, dg_m1)
    d_ln_mix_b = two_rows(db_m0, db_m1)
    d_ln_ffn_g = two_rows(dg_f0, dg_f1)
    d_ln_ffn_b = two_rows(db_f0, db_f1)
    small_grads, small_layout = _pack_rows([d_lbl, d_conv_w, dgw, dsinks, drel_t.T, d_conv_b, d_ln_mix_g, d_ln_mix_b,
                                            d_ln_ffn_g, d_ln_ffn_b, loss_tile[0:1, 0:1]])

    (ex4,), tok4 = _send_start([sc(dw_hin.reshape(N_DEV, D, -1), dw_hout.reshape(N_DEV, D // N_DEV, D))
                                + [(small_grads, "gather")]], name="grads_start_4")
    dx = _mm_nt_resident(dz, w_hin, res=dy1, res_scale=ALPHA, name="hgrn_dx", behind=(tok4,))
    r_fin1, r_fout1 = _send_wait(ex1, (dx,), name="grads_wait_1")
    r_o, r_q, r_kv = _send_wait(ex2, (dx,), name="grads_wait_2")
    r_fin0, r_fout0 = _send_wait(ex3, (dx,), name="grads_wait_3")
    r_hin, r_hout, r_small = _send_wait(ex4, (dx,), name="grads_wait_4")
    received = [r_hin, r_hout, r_q, r_o, r_kv, r_fin0, r_fin1, r_fout0, r_fout1, r_small]

    outs = {}

    def put(name_, res):
        outs["grad_" + name_], outs["delta_" + name_], outs["new_m_" + name_], outs["new_v_" + name_] = res

    def big_update(name_, parts, w, m, v):
        shp = w.shape
        if w.ndim == 3 and shp[0] == 1:
            r = _adamw(parts, w[0], m[0], v[0], name="adamw_" + name_)
            put(name_, [a.reshape(shp) for a in r])
        else:
            r = _adamw(parts, w, m, v, name="adamw_" + name_)
            put(name_, r)

    big_update("hgrn_w_in", received[0], hgrn_w_in, m_hgrn_w_in, v_hgrn_w_in)
    big_update("hgrn_w_out", received[1], hgrn_w_out, m_hgrn_w_out, v_hgrn_w_out)
    big_update("swa_w_q", received[2], swa_w_q, m_swa_w_q, v_swa_w_q)
    big_update("swa_w_out", received[3], swa_w_out, m_swa_w_out, v_swa_w_out)
    big_update("shared_w_kv", received[4], shared_w_kv, m_shared_w_kv, v_shared_w_kv)
    for name_, idx, w, m, v in (("ffn_w_in", 5, ffn_w_in, m_ffn_w_in, v_ffn_w_in),
                                ("ffn_w_out", 7, ffn_w_out, m_ffn_w_out, v_ffn_w_out)):
        per_layer = [_adamw(received[idx + l], w, m, v, layer=l, name=f"adamw_{name_}_{l}") for l in range(DEPTH)]
        put(name_, [jnp.stack([per_layer[0][i], per_layer[1][i]], axis=0) for i in range(4)])

    small_sum = _sum_parts(received[9], name="sum_small_grads")
    (g_lbl, g_conv_w, g_gw, g_sinks, g_rel, g_conv_b, g_mix_g, g_mix_b, g_ffn_g, g_ffn_b,
     loss) = _unpack_rows(small_sum, small_layout)
    g_lbl_mine = lax.dynamic_slice_in_dim(g_lbl, me * (D // N_DEV), D // N_DEV, axis=1)
    g_conv_w_mine = lax.dynamic_index_in_dim(g_conv_w, me, axis=1, keepdims=False)
    small_names = ["hgrn_lb_logits", "ffn_conv_w", "hgrn_gnorm_w", "swa_sinks", "rel_bias", "ffn_conv_b",
                   "ln_mix_g", "ln_mix_b", "ln_ffn_g", "ln_ffn_b"]
    small_g = [g_lbl_mine, g_conv_w_mine, g_gw, g_sinks, g_rel, g_conv_b, g_mix_g, g_mix_b, g_ffn_g, g_ffn_b]
    small_w = [hgrn_lb_logits, ffn_conv_w, hgrn_gnorm_w, swa_sinks, rel_bias, ffn_conv_b, ln_mix_g, ln_mix_b,
               ln_ffn_g, ln_ffn_b]
    small_m = [m_hgrn_lb_logits, m_ffn_conv_w, m_hgrn_gnorm_w, m_swa_sinks, m_rel_bias, m_ffn_conv_b, m_ln_mix_g,
               m_ln_mix_b, m_ln_ffn_g, m_ln_ffn_b]
    small_v = [v_hgrn_lb_logits, v_ffn_conv_w, v_hgrn_gnorm_w, v_swa_sinks, v_rel_bias, v_ffn_conv_b, v_ln_mix_g,
               v_ln_mix_b, v_ln_ffn_g, v_ln_ffn_b]
    pg, lay = _pack_rows(small_g)
    pw, _ = _pack_rows(small_w)
    pm, _ = _pack_rows(small_m)
    pv, _ = _pack_rows(small_v)
    res = _adamw(pg[None], pw, pm, pv, name="adamw_small")
    unpacked = [_unpack_rows(r, lay) for r in res]
    for i, name_ in enumerate(small_names):
        put(name_, [unpacked[j][i] for j in range(4)])

    order = ["hgrn_w_in", "hgrn_lb_logits", "hgrn_gnorm_w", "hgrn_w_out", "swa_w_q", "swa_sinks", "swa_w_out",
             "shared_w_kv", "rel_bias", "ffn_w_in", "ffn_conv_w", "ffn_conv_b", "ffn_w_out", "ln_mix_g", "ln_mix_b",
             "ln_ffn_g", "ln_ffn_b"]
    result = [loss.reshape(()), dx]
    for kind in ("grad_", "delta_", "new_m_", "new_v_"):
        result += [outs[kind + n] for n in order]
    return tuple(result)
```

```python
import functools
import math

import numpy as np
import jax
import jax.numpy as jnp
from jax import lax
from jax.experimental import pallas as pl
from jax.experimental.pallas import tpu as pltpu

F32 = jnp.float32
MXU = jnp.bfloat16

N_DEV = 8
D_MODEL = 1024
DEPTH = 2
HG_HEADS = 8
HG_DIM = 128
HG_CHUNK = 64
HG_STEP_CHUNKS = 8
SW_Q_HEADS = 16
SW_KV_HEADS = 4
SW_GROUP = 4
SW_HEAD_DIM = 64
SW_WINDOW = 128
SW_STEP_BLOCKS = 2
REL_BUCKETS = 32
REL_MAX_DIST = 128
FFN_DIM = 2816
ALPHA = (2.0 * DEPTH) ** 0.25
LN_EPS = 1e-5
RMS_EPS = 1e-6
ADAM_LR = 0.001
ADAM_B1 = 0.9
ADAM_B2 = 0.999
ADAM_EPS = 1e-08
ADAM_WD = 0.01
ADAM_STEP = 10
EXP_CLAMP = 80.0
NEG_BIG = -1e30

SUBLANES = 8
LANES = 128
VMEM_LIMIT = 48 * 2 ** 20
TOKEN_TILE = 512
FFN_TOKEN_TILE = 1024
WIDE_TOKEN_TILE = 1024
RESIDENT_TOKEN_TILE = 256
REDUCE_TOKEN_TILE = 2048
REDUCE_TOKEN_TILE_16BIT = 4096
GRAD_DTYPE = jnp.bfloat16


def _params(**kw):
    return pltpu.CompilerParams(vmem_limit_bytes=VMEM_LIMIT, **kw)


def _sigmoid(x):
    return 1.0 / (1.0 + jnp.exp(-x))


def _dot(a, b):
    return jnp.dot(a.astype(MXU), b.astype(MXU), preferred_element_type=F32)


def _dot_nt(a, b):
    return lax.dot_general(a.astype(MXU), b.astype(MXU), (((1,), (1,)), ((), ())), preferred_element_type=F32)


def _dot_tn(a, b):
    return lax.dot_general(a.astype(MXU), b.astype(MXU), (((0,), (0,)), ((), ())), preferred_element_type=F32)


def _trunc_bf16(x):
    bits = lax.bitcast_convert_type(x, jnp.int32)
    return lax.bitcast_convert_type(bits & jnp.int32(-65536), F32)


def _split3(x):
    hi = _trunc_bf16(x)
    r = x - hi
    mid = _trunc_bf16(r)
    lo = r - mid
    return hi.astype(jnp.bfloat16), mid.astype(jnp.bfloat16), lo.astype(jnp.bfloat16)


def _dot_hp(a, b, contract):
    def halves(x):
        hi = _trunc_bf16(x)
        return hi.astype(jnp.bfloat16), (x - hi).astype(jnp.bfloat16)

    ah, al = halves(a)
    bh, bl = halves(b)
    d = lambda p, q: lax.dot_general(p, q, (contract, ((), ())), preferred_element_type=F32)
    return d(ah, bh) + d(ah, bl) + d(al, bh)


def _exact_dot(m01, x):
    hi, mid, lo = _split3(x)
    d = lambda p: jnp.dot(m01, p, preferred_element_type=F32)
    return d(hi) + d(mid) + d(lo)


def _exact_dot_r(x, m01):
    hi, mid, lo = _split3(x)
    d = lambda p: jnp.dot(p, m01, preferred_element_type=F32)
    return d(hi) + d(mid) + d(lo)


def _mm_nn(a, w, *, name, res=None, res_scale=1.0, ln=None, out_dtype=F32, tm=None):
    nbk, T, kw = a.shape
    _, nbn, _, nw = w.shape
    tm = min(tm or TOKEN_TILE, T)
    has_res = res is not None
    res_ln = isinstance(res, tuple)
    n_res = (3 if res_ln else 1) if has_res else 0
    assert ln is None or nbn == 1

    def body(*refs):
        refs = list(refs)
        a_ref, w_ref = refs[:2]
        res_refs = refs[2:2 + n_res]
        pos = 2 + n_res
        if ln is not None:
            g_ref, b_ref = refs[pos:pos + 2]
            pos += 2
        o_ref = refs[pos]
        if ln is not None:
            xh_ref, rs_ref = refs[pos + 1:pos + 3]
        for n in range(nbn):
            y = _dot(a_ref[0], w_ref[0, n])
            for k in range(1, nbk):
                y = y + _dot(a_ref[k], w_ref[k, n])
            if res_ln:
                y = y + res_scale * (res_refs[0][n] * res_refs[1][...] + res_refs[2][...])
            elif has_res:
                y = y + res_scale * res_refs[0][n].astype(F32)
            if ln is None:
                o_ref[n] = y.astype(o_ref.dtype)
            else:
                mu = jnp.mean(y, axis=-1, keepdims=True)
                yc = y - mu
                var = jnp.mean(yc * yc, axis=-1, keepdims=True)
                rstd = lax.rsqrt(var + LN_EPS)
                xh = yc * rstd
                xh_ref[n] = xh
                rs_ref[...] = rstd
                o_ref[n] = (xh * g_ref[...] + b_ref[...]).astype(o_ref.dtype)

    vec = pl.BlockSpec((1, nw), lambda i: (0, 0))
    in_specs = [pl.BlockSpec((nbk, tm, kw), lambda i: (0, i, 0)),
                pl.BlockSpec((nbk, nbn, kw, nw), lambda i: (0, 0, 0, 0))]
    args = [a, w]
    if has_res:
        in_specs.append(pl.BlockSpec((nbn, tm, nw), lambda i: (0, i, 0)))
        if res_ln:
            in_specs += [vec, vec]
            args += list(res)
        else:
            args.append(res)
    if ln is not None:
        in_specs += [vec, vec]
        args += list(ln)
    out_spec = pl.BlockSpec((nbn, tm, nw), lambda i: (0, i, 0))
    out_shape = jax.ShapeDtypeStruct((nbn, T, nw), out_dtype)
    if ln is not None:
        out_specs = [out_spec, out_spec, pl.BlockSpec((tm, 1), lambda i: (i, 0))]
        out_shape = [jax.ShapeDtypeStruct((nbn, T, nw), MXU), jax.ShapeDtypeStruct((nbn, T, nw), F32),
                     jax.ShapeDtypeStruct((T, 1), F32)]
    else:
        out_specs = out_spec
    return pl.pallas_call(body, name=name, grid=(T // tm,), in_specs=in_specs, out_specs=out_specs,
                          out_shape=out_shape, compiler_params=_params())(*args)


def _same(n):
    return n


def _mm_nt(dy, w, *, name, res=None, res_scale=1.0, out_dtype=F32, tm=None, n_map=_same, behind=()):
    nbn, T, nw = dy.shape
    nbk, _, kw, _ = w.shape
    tm = min(tm or WIDE_TOKEN_TILE, T)
    has_res = res is not None

    def body(*refs):
        refs = list(refs)
        dy_ref, w_ref = refs[:2]
        pos = 2
        res_ref = None
        if has_res:
            res_ref = refs[pos]
            pos += 1
        pos += len(behind)
        o_ref = refs[pos]
        pos += 1
        acc_ref = refs[pos] if nbn > 1 else None
        n = pl.program_id(2)
        part = _dot_nt(dy_ref[...], w_ref[...])

        def finish(acc):
            y = acc
            if has_res:
                y = y + res_scale * res_ref[...].astype(F32)
            o_ref[...] = y.astype(o_ref.dtype)

        if nbn == 1:
            finish(part)
        else:
            @pl.when(n == 0)
            def _():
                acc_ref[...] = part

            @pl.when(n > 0)
            def _():
                acc_ref[...] += part

            @pl.when(n == nbn - 1)
            def _():
                finish(acc_ref[...])

    in_specs = [pl.BlockSpec((None, tm, nw), lambda i, k, n: (n, i, 0)),
                pl.BlockSpec((None, None, kw, nw), lambda i, k, n: (k, n_map(n), 0, 0))]
    args = [dy, w]
    if has_res:
        in_specs.append(pl.BlockSpec((None, tm, kw), lambda i, k, n: (k, i, 0)))
        args.append(res)
    in_specs += [pl.BlockSpec(memory_space=pl.ANY)] * len(behind)
    args += list(behind)
    scratch = [pltpu.VMEM((tm, kw), F32)] if nbn > 1 else []
    return pl.pallas_call(body, name=name, grid=(T // tm, nbk, nbn), in_specs=in_specs,
                          out_specs=pl.BlockSpec((None, tm, kw), lambda i, k, n: (k, i, 0)),
                          out_shape=jax.ShapeDtypeStruct((nbk, T, kw), out_dtype), scratch_shapes=scratch,
                          compiler_params=_params())(*args)


def _mm_nt_resident(dy, w, *, name, res=None, res_scale=1.0, ln_bwd=None, tm=None, n_map=_same, behind=()):
    nbn, T, nw = dy.shape
    nbk, _, kw, _ = w.shape
    assert nbk == 1
    tm = min(tm or RESIDENT_TOKEN_TILE, T)
    has_res = res is not None
    n_in = 2 + has_res + (3 if ln_bwd else 0) + len(behind)

    def body(*refs):
        dy_ref, w_ref = refs[:2]
        res_ref = refs[2] if has_res else None
        y = _dot_nt(dy_ref[0], w_ref[0, n_map(0)])
        for n in range(1, nbn):
            y = y + _dot_nt(dy_ref[n], w_ref[0, n_map(n)])
        if has_res:
            y = y + res_scale * res_ref[0].astype(F32)
        if ln_bwd is None:
            refs[n_in][0] = y
        else:
            xh_ref, rs_ref, g_ref = refs[2 + has_res:5 + has_res]
            o_ref, dg_ref, db_ref = refs[n_in:n_in + 3]
            out, dg, db = _ln_bwd_rows(y, xh_ref[0], rs_ref[...], g_ref[...])
            o_ref[0] = out
            _accumulate(pl.program_id(0), (dg_ref, db_ref), (dg, db))

    tok = pl.BlockSpec((1, tm, kw), lambda i: (0, i, 0))
    vec = pl.BlockSpec((1, kw), lambda i: (0, 0))
    in_specs = [pl.BlockSpec((nbn, tm, nw), lambda i: (0, i, 0)),
                pl.BlockSpec(w.shape, lambda i: (0, 0, 0, 0))]
    args = [dy, w]
    if has_res:
        in_specs.append(tok)
        args.append(res)
    out_specs, out_shape = tok, jax.ShapeDtypeStruct((1, T, kw), F32)
    if ln_bwd is not None:
        in_specs += [tok, pl.BlockSpec((tm, 1), lambda i: (i, 0)), vec]
        args += list(ln_bwd)
        out_specs = [tok, vec, vec]
        out_shape = [out_shape, jax.ShapeDtypeStruct((1, kw), F32), jax.ShapeDtypeStruct((1, kw), F32)]
    in_specs += [pl.BlockSpec(memory_space=pl.ANY)] * len(behind)
    args += list(behind)
    return pl.pallas_call(body, name=name, grid=(T // tm,), in_specs=in_specs, out_specs=out_specs,
                          out_shape=out_shape, compiler_params=_params())(*args)


def _mm_tn(a, dy, *, name, tm=None, n_map=_same):
    nbk, T, kw = a.shape
    nbn, _, nw = dy.shape
    tm = min(tm or REDUCE_TOKEN_TILE, T)
    nt = T // tm

    def body(a_ref, dy_ref, o_ref, acc_ref):
        i = pl.program_id(2)
        part = _dot_tn(a_ref[...], dy_ref[...])

        @pl.when(i == 0)
        def _():
            acc_ref[...] = part

        @pl.when(i > 0)
        def _():
            acc_ref[...] += part

        @pl.when(i == nt - 1)
        def _():
            o_ref[...] = acc_ref[...].astype(o_ref.dtype)

    return pl.pallas_call(body, name=name, grid=(nbk, nbn, nt),
                          in_specs=[pl.BlockSpec((None, tm, kw), lambda k, n, i: (k, i, 0)),
                                    pl.BlockSpec((None, tm, nw), lambda k, n, i: (n, i, 0))],
                          out_specs=pl.BlockSpec((None, None, kw, nw), lambda k, n, i: (k, n_map(n), 0, 0)),
                          out_shape=jax.ShapeDtypeStruct((nbk, nbn, kw, nw), GRAD_DTYPE),
                          scratch_shapes=[pltpu.VMEM((kw, nw), F32)],
                          compiler_params=_params())(a, dy)


def _ln_bwd_rows(dh, xh, rstd, g):
    dxh = dh * g
    m1 = jnp.mean(dxh, axis=-1, keepdims=True)
    m2 = jnp.mean(dxh * xh, axis=-1, keepdims=True)
    dy = rstd * (dxh - m1 - xh * m2)
    return dy, jnp.sum(dh * xh, axis=0, keepdims=True), jnp.sum(dh, axis=0, keepdims=True)


def _accumulate(i, refs, parts):
    @pl.when(i == 0)
    def _():
        for r, p in zip(refs, parts):
            r[...] = jnp.zeros_like(r) + p

    @pl.when(i > 0)
    def _():
        for r, p in zip(refs, parts):
            r[...] += p


def _loss_ln_bwd(tgt, xhat, rstd, g, b, *, name, tm=None):
    _, T, D = xhat.shape
    tm = min(tm or TOKEN_TILE, T)

    def body(t_ref, xh_ref, rs_ref, g_ref, b_ref, dy_ref, dg_ref, db_ref, loss_ref):
        i = pl.program_id(0)
        xh = xh_ref[...]
        err = xh * g_ref[...] + b_ref[...] - t_ref[...]
        part = 0.5 * jnp.sum(jnp.mean(err * err, axis=-1, keepdims=True), axis=0, keepdims=True)
        dy, dg, db = _ln_bwd_rows(err / D, xh, rs_ref[...], g_ref[...])
        dy_ref[...] = dy
        _accumulate(i, (dg_ref, db_ref, loss_ref), (dg, db, part))

    tok = pl.BlockSpec((None, tm, D), lambda i: (0, i, 0))
    vec = pl.BlockSpec((1, D), lambda i: (0, 0))
    return pl.pallas_call(body, name=name, grid=(T // tm,),
                          in_specs=[tok, tok, pl.BlockSpec((tm, 1), lambda i: (i, 0)), vec, vec],
                          out_specs=[tok, vec, vec, pl.BlockSpec((SUBLANES, LANES), lambda i: (0, 0))],
                          out_shape=[jax.ShapeDtypeStruct((1, T, D), F32), jax.ShapeDtypeStruct((1, D), F32),
                                     jax.ShapeDtypeStruct((1, D), F32), jax.ShapeDtypeStruct((SUBLANES, LANES), F32)],
                          compiler_params=_params())(tgt, xhat, rstd, g, b)


def _shift_rows(ext, k, n, halo):
    if k == 0:
        return ext[halo:halo + n]
    return pltpu.roll(ext, k, axis=0)[halo:halo + n]


def _conv_rows(ext, cw_ref, n, halo):
    return (cw_ref[0:1, :] * _shift_rows(ext, 2, n, halo) + cw_ref[1:2, :] * _shift_rows(ext, 1, n, halo)
            + cw_ref[2:3, :] * ext[halo:halo + n] + cw_ref[3:4, :])


def _pair_map(n):
    return n // 2 + 4 * (n % 2)


def _ffn_up(hb, w_in, cw, *, name, tm=None):
    _, T, D = hb.shape
    _, nb, _, fb = w_in.shape
    half = nb // 2
    tm = min(tm or FFN_TOKEN_TILE, T)

    def body(h_ref, wa_ref, wb_ref, cwa_ref, cwb_ref, u_ref, ab_ref, act_ref, carry):
        @pl.when(pl.program_id(1) == 0)
        def _():
            carry[...] = jnp.zeros_like(carry)

        h = h_ref[...]
        conv = []
        for s, (w_ref, cw_ref) in enumerate(((wa_ref, cwa_ref), (wb_ref, cwb_ref))):
            uf = _dot(h, w_ref[...])
            u_ref[s] = uf.astype(u_ref.dtype)
            ext = jnp.concatenate([carry[s], uf], axis=0)
            c = _conv_rows(ext, cw_ref, tm, SUBLANES)
            ab_ref[s] = c.astype(ab_ref.dtype)
            conv.append(c)
            carry[s] = uf[tm - SUBLANES:tm]
        a, b = conv
        act_ref[...] = (a * _sigmoid(a) * b).astype(act_ref.dtype)

    wspec = lambda off: pl.BlockSpec((None, None, D, fb), lambda p, i: (0, p + off, 0, 0))
    cws = lambda off: pl.BlockSpec((None, SUBLANES, fb), lambda p, i: (p + off, 0, 0))
    return pl.pallas_call(body, name=name, grid=(half, T // tm),
                          in_specs=[pl.BlockSpec((None, tm, D), lambda p, i: (0, i, 0)), wspec(0), wspec(half),
                                    cws(0), cws(half)],
                          out_specs=[pl.BlockSpec((None, 2, tm, fb), lambda p, i: (p, 0, i, 0)),
                                     pl.BlockSpec((None, 2, tm, fb), lambda p, i: (p, 0, i, 0)),
                                     pl.BlockSpec((None, tm, fb), lambda p, i: (p, i, 0))],
                          out_shape=[jax.ShapeDtypeStruct((half, 2, T, fb), MXU),
                                     jax.ShapeDtypeStruct((half, 2, T, fb), MXU),
                                     jax.ShapeDtypeStruct((half, T, fb), MXU)],
                          scratch_shapes=[pltpu.VMEM((2, SUBLANES, fb), F32)],
                          compiler_params=_params())(hb, w_in, w_in, cw, cw)


def _ffn_gate_bwd(dy, u, ab, w_out, cw, *, name, tm=None):
    _, T, D = dy.shape
    half, _, _, fb = u.shape
    tm = min(tm or FFN_TOKEN_TILE, T)
    nt = T // tm

    n_full = fb // LANES
    tail = slice(n_full * LANES, fb)

    def body(dy_ref, u_ref, ab_ref, w_ref, cwa_ref, cwb_ref, du_ref, dwo_ref, dcw_ref, carry, acc, gacc):
        i = pl.program_id(1)

        @pl.when(i == 0)
        def _():
            carry[...] = jnp.zeros_like(carry)
            acc[...] = jnp.zeros_like(acc)
            gacc[...] = jnp.zeros_like(gacc)
            dcw_ref[...] = jnp.zeros_like(dcw_ref)

        dyv = dy_ref[...]
        dact = _dot_nt(dyv, w_ref[...])
        a = ab_ref[0].astype(F32)
        b = ab_ref[1].astype(F32)
        sa = _sigmoid(a)
        silu = a * sa
        acc[...] += _dot_tn(silu * b, dyv)
        dcs = (dact * b * (sa * (1.0 + a * (1.0 - sa))), dact * silu)
        m = tm + SUBLANES
        rows = lax.broadcasted_iota(jnp.int32, (SUBLANES, fb), 0)
        for s, cw_ref in enumerate((cwa_ref, cwb_ref)):
            dc = dcs[s]
            nxt = jnp.concatenate([dc, carry[s]], axis=0)
            dc1 = pltpu.roll(nxt, m - 1, axis=0)[:tm]
            dc2 = pltpu.roll(nxt, m - 2, axis=0)[:tm]
            du_ref[s] = (cw_ref[2:3, :] * dc + cw_ref[1:2, :] * dc1 + cw_ref[0:1, :] * dc2).astype(du_ref.dtype)
            carry[s] = dc[0:SUBLANES]
            dcb = [x.astype(MXU) for x in (dc, dc1, dc2)]
            for j in range(n_full):
                blk = slice(j * LANES, (j + 1) * LANES)
                gacc[s, j] += _dot_tn(u_ref[s, :, blk], jnp.concatenate([x[:, blk] for x in dcb], axis=1))
            dcw_ref[s] += jnp.where(rows == 3, jnp.sum(dc, axis=0, keepdims=True), 0.0)
            if fb > n_full * LANES:
                ut = u_ref[s, :, tail].astype(F32)
                gt = [jnp.sum(x[:, tail] * ut, axis=0, keepdims=True) for x in (dc2, dc1, dc)]
                rt = rows[:, tail]
                dcw_ref[s, :, tail] += jnp.where(rt == 0, gt[0], jnp.where(rt == 1, gt[1], jnp.where(rt == 2, gt[2], 0.0)))

        @pl.when(i == nt - 1)
        def _():
            dwo_ref[...] = acc[...].astype(dwo_ref.dtype)
            eye = _tri(LANES, True) & _tri(LANES, False)
            for s in range(2):
                for j in range(n_full):
                    g = gacc[s, j]
                    for tap in range(3):
                        d = jnp.where(eye, g[:, (2 - tap) * LANES:(3 - tap) * LANES], 0.0)
                        dcw_ref[s, tap:tap + 1, j * LANES:(j + 1) * LANES] = jnp.sum(d, axis=0, keepdims=True)

    rev = lambda i: nt - 1 - i
    cws = lambda off: pl.BlockSpec((None, SUBLANES, fb), lambda p, i: (p + off, 0, 0))
    pair = lambda: pl.BlockSpec((None, 2, tm, fb), lambda p, i: (p, 0, rev(i), 0))
    return pl.pallas_call(body, name=name, grid=(half, nt),
                          in_specs=[pl.BlockSpec((None, tm, D), lambda p, i: (0, rev(i), 0)), pair(), pair(),
                                    pl.BlockSpec((None, None, fb, D), lambda p, i: (p, 0, 0, 0)), cws(0), cws(half)],
                          out_specs=[pair(),
                                     pl.BlockSpec((None, None, fb, D), lambda p, i: (p, 0, 0, 0)),
                                     pl.BlockSpec((None, 2, SUBLANES, fb), lambda p, i: (p, 0, 0, 0))],
                          out_shape=[jax.ShapeDtypeStruct((half, 2, T, fb), MXU),
                                     jax.ShapeDtypeStruct((half, 1, fb, D), GRAD_DTYPE),
                                     jax.ShapeDtypeStruct((half, 2, SUBLANES, fb), F32)],
                          scratch_shapes=[pltpu.VMEM((2, SUBLANES, fb), F32), pltpu.VMEM((fb, D), F32),
                                          pltpu.VMEM((2, n_full, LANES, 3 * LANES), F32)],
                          compiler_params=_params())(dy, u, ab, w_out, cw, cw)


def _tri(n, lower):
    r = lax.broadcasted_iota(jnp.int32, (n, n), 0)
    c = lax.broadcasted_iota(jnp.int32, (n, n), 1)
    return (r >= c) if lower else (r <= c)


def _hgrn_gates(zq, zf, lb):
    sq = _sigmoid(zq)
    sf = _sigmoid(zf)
    fg = lb + (1.0 - lb) * sf
    return zq * sq, sq, sf, fg, jnp.log(fg)


def _lb_of(lbl_ref, cols):
    return _sigmoid(lbl_ref[0:1, cols] - lbl_ref[1:2, cols])


def _ones_where(mask):
    return jnp.where(mask, 1.0, 0.0).astype(jnp.bfloat16)


def _hgrn_fwd(z, lbl, gw, *, name):
    _, T, zw = z.shape
    C = min(HG_CHUNK, T)
    nch = T // C
    S = HG_STEP_CHUNKS if nch % HG_STEP_CHUNKS == 0 else 1
    hpb = zw // HG_DIM

    def body(z_ref, lbl_ref, gw_ref, og_ref, st_ref, s_scr, bc_scr, q_scr, k_scr):
        c = pl.program_id(0)

        @pl.when(c == 0)
        def _():
            s_scr[...] = jnp.zeros_like(s_scr)

        low = _tri(C, True)
        low01 = _ones_where(low)
        gwv = gw_ref[...]
        H = range(HG_HEADS)
        col = lambda h: slice(h * HG_DIM, (h + 1) * HG_DIM)
        for sub in range(S):
            chunk_body(z_ref, lbl_ref, og_ref, st_ref, s_scr, bc_scr, q_scr, k_scr,
                       sub, slice(sub * C, (sub + 1) * C), low, low01, gwv, H, col)

    def chunk_body(z_ref, lbl_ref, og_ref, st_ref, s_scr, bc_scr, q_scr, k_scr, sub, rows, low, low01, gwv, H, col):
        for blk in range(2):
            cols = slice(blk * zw, (blk + 1) * zw)
            qq, _, _, fg, lf = _hgrn_gates(z_ref[blk, rows], z_ref[2 + blk, rows], _lb_of(lbl_ref, cols))
            q_scr[:, cols] = qq
            k_scr[:, cols] = 1.0 - fg
            bc_scr[:, cols] = _exact_dot(low01, lf)
        zcol = lambda part, h: (part + h // hpb, rows, slice((h % hpb) * HG_DIM, (h % hpb + 1) * HG_DIM))
        b = [bc_scr[:, col(h)] for h in H]
        bm = [bc_scr[C // 2 - 1:C // 2, col(h)] for h in H]
        bl = [bc_scr[C - 1:C, col(h)] for h in H]
        q_ = [q_scr[:, col(h)] for h in H]
        k_ = [k_scr[:, col(h)] for h in H]
        v_ = [z_ref[zcol(4, h)] for h in H]
        qt = [q_[h] * jnp.exp(jnp.minimum(b[h] - bm[h], EXP_CLAMP)) for h in H]
        kt = [k_[h] * jnp.exp(jnp.minimum(bm[h] - b[h], EXP_CLAMP)) for h in H]
        A = [jnp.where(low, _dot_nt(qt[h], kt[h]), 0.0) for h in H]
        for h in H:
            st_ref[sub, h] = s_scr[h]
        o = [_dot_nt(q_[h] * jnp.exp(b[h]), s_scr[h]) + _dot(A[h], v_[h]) for h in H]
        for h in H:
            s_scr[h] = s_scr[h] * jnp.exp(bl[h]) + _dot_tn(v_[h], k_[h] * jnp.exp(bl[h] - b[h]))
        for h in H:
            g_h = z_ref[zcol(6, h)]
            r = lax.rsqrt(jnp.mean(o[h] * o[h], axis=-1, keepdims=True) + RMS_EPS)
            og_ref[rows, col(h)] = (o[h] * r * gwv * (g_h * _sigmoid(g_h))).astype(og_ref.dtype)

    return pl.pallas_call(body, name=name, grid=(nch // S,),
                          in_specs=[pl.BlockSpec((8, S * C, zw), lambda c: (0, c, 0)),
                                    pl.BlockSpec((2, D_MODEL), lambda c: (0, 0)),
                                    pl.BlockSpec((1, HG_DIM), lambda c: (0, 0))],
                          out_specs=[pl.BlockSpec((None, S * C, D_MODEL), lambda c: (0, c, 0)),
                                     pl.BlockSpec((S, HG_HEADS, HG_DIM, HG_DIM), lambda c: (c, 0, 0, 0))],
                          out_shape=[jax.ShapeDtypeStruct((1, T, D_MODEL), MXU),
                                     jax.ShapeDtypeStruct((nch, HG_HEADS, HG_DIM, HG_DIM), F32)],
                          scratch_shapes=[pltpu.VMEM((HG_HEADS, HG_DIM, HG_DIM), F32)]
                          + [pltpu.VMEM((C, D_MODEL), F32)] * 3,
                          compiler_params=_params())(z, lbl, gw)


def _hgrn_bwd(z, dog, states, lbl, gw, *, name):
    _, T, zw = z.shape
    C = min(HG_CHUNK, T)
    nch = T // C
    S = HG_STEP_CHUNKS if nch % HG_STEP_CHUNKS == 0 else 1
    hpb = zw // HG_DIM

    def body(z_ref, dog_ref, st0_ref, st1_ref, lbl_ref, gw_ref, dz_ref, dlb_ref, dgw_ref,
             d_scr, bc_scr, q_scr, sf_scr, fg_scr, x_scr):
        step = pl.program_id(0)

        @pl.when(step == 0)
        def _():
            d_scr[...] = jnp.zeros_like(d_scr)
            dlb_ref[...] = jnp.zeros_like(dlb_ref)
            dgw_ref[...] = jnp.zeros_like(dgw_ref)

        low = _tri(C, True)
        low01 = _ones_where(low)
        up01 = _ones_where(_tri(C, False))
        gwv = gw_ref[...]
        H = range(HG_HEADS)
        col = lambda h: slice(h * HG_DIM, (h + 1) * HG_DIM)
        for sub in reversed(range(S)):
            chunk_body(z_ref, dog_ref, st0_ref, st1_ref, lbl_ref, dz_ref, dlb_ref, dgw_ref,
                       d_scr, bc_scr, q_scr, sf_scr, fg_scr, x_scr,
                       sub, slice(sub * C, (sub + 1) * C), low, low01, up01, gwv, H, col)

    def chunk_body(z_ref, dog_ref, st0_ref, st1_ref, lbl_ref, dz_ref, dlb_ref, dgw_ref,
                   d_scr, bc_scr, q_scr, sf_scr, fg_scr, x_scr, sub, rows, low, low01, up01, gwv, H, col):
        st0 = lambda h: st0_ref[sub, h]
        st1 = (lambda h: st0_ref[sub + 1, h]) if sub + 1 < S else (lambda h: st1_ref[h])
        for blk in range(2):
            lbb = _lb_of(lbl_ref, slice(blk * zw, (blk + 1) * zw))
            qq, sq, sf, fg, lf = _hgrn_gates(z_ref[blk, rows], z_ref[2 + blk, rows], lbb)
            q_scr[:, blk * zw:(blk + 1) * zw] = qq
            sf_scr[:, blk * zw:(blk + 1) * zw] = sf
            fg_scr[:, blk * zw:(blk + 1) * zw] = fg
            bc_scr[:, blk * zw:(blk + 1) * zw] = _exact_dot(low01, lf)
        zcol = lambda part, h: (part + h // hpb, rows, slice((h % hpb) * HG_DIM, (h % hpb + 1) * HG_DIM))
        b = [bc_scr[:, col(h)] for h in H]
        bm = [bc_scr[C // 2 - 1:C // 2, col(h)] for h in H]
        bl = [bc_scr[C - 1:C, col(h)] for h in H]
        q_ = [q_scr[:, col(h)] for h in H]
        k_ = [1.0 - fg_scr[:, col(h)] for h in H]
        v_ = [z_ref[zcol(4, h)] for h in H]
        eq = [jnp.exp(jnp.minimum(b[h] - bm[h], EXP_CLAMP)) for h in H]
        ek = [jnp.exp(jnp.minimum(bm[h] - b[h], EXP_CLAMP)) for h in H]
        eb = [jnp.exp(b[h]) for h in H]
        el = [jnp.exp(bl[h] - b[h]) for h in H]
        qt = [q_[h] * eq[h] for h in H]
        kt = [k_[h] * ek[h] for h in H]
        q0 = [q_[h] * eb[h] for h in H]
        kd = [k_[h] * el[h] for h in H]
        A = [jnp.where(low, _dot_nt(qt[h], kt[h]), 0.0) for h in H]
        o = [_dot_nt(q0[h], st0(h)) + _dot(A[h], v_[h]) for h in H]
        do = []
        dgw_acc = jnp.zeros((1, HG_DIM), F32)
        for h in H:
            g_h = z_ref[zcol(6, h)]
            r = lax.rsqrt(jnp.mean(o[h] * o[h], axis=-1, keepdims=True) + RMS_EPS)
            on = o[h] * r
            sg = _sigmoid(g_h)
            dogh = dog_ref[rows, col(h)].astype(F32)
            t1 = dogh * on
            dgw_acc = dgw_acc + jnp.sum(t1 * (g_h * sg), axis=0, keepdims=True)
            dz_ref[zcol(6, h)] = (t1 * gwv * (sg * (1.0 + g_h * (1.0 - sg)))).astype(dz_ref.dtype)
            don = dogh * gwv * (g_h * sg)
            do.append(r * (don - on * jnp.mean(don * on, axis=-1, keepdims=True)))
        dgw_ref[...] += dgw_acc
        P = [jnp.where(low, _dot_nt(do[h], v_[h]), 0.0) for h in H]
        dqq = [eb[h] * _dot(do[h], st0(h)) + eq[h] * _dot_hp(P[h], kt[h], ((1,), (0,))) for h in H]
        dkk = [el[h] * _dot(v_[h], d_scr[h]) + ek[h] * _dot_hp(P[h], qt[h], ((0,), (0,))) for h in H]
        for h in H:
            dz_ref[zcol(4, h)] = (_dot_nt(kd[h], d_scr[h]) + _dot_tn(A[h], do[h])).astype(dz_ref.dtype)
            x_scr[:, col(h)] = q_[h] * dqq[h] - k_[h] * dkk[h]
        edge = [jnp.sum(d_scr[h] * st1(h), axis=0, keepdims=True) for h in H]
        for h in H:
            d_scr[h] = d_scr[h] * jnp.exp(bl[h]) + _dot_tn(do[h], q0[h])
        for blk in range(2):
            x_scr[:, blk * zw:(blk + 1) * zw] = _exact_dot(up01, x_scr[:, blk * zw:(blk + 1) * zw])
        dlb = []
        for h in H:
            dfg = (x_scr[:, col(h)] + edge[h]) / fg_scr[:, col(h)] - dkk[h]
            sf_h = sf_scr[:, col(h)]
            lb_h = _lb_of(lbl_ref, col(h))
            zq_h = z_ref[zcol(0, h)]
            sq_h = _sigmoid(zq_h)
            dlb.append(jnp.sum(dfg * (1.0 - sf_h), axis=0, keepdims=True))
            dz_ref[zcol(0, h)] = (dqq[h] * (sq_h * (1.0 + zq_h * (1.0 - sq_h)))).astype(dz_ref.dtype)
            dz_ref[zcol(2, h)] = (dfg * (1.0 - lb_h) * sf_h * (1.0 - sf_h)).astype(dz_ref.dtype)
        dlb_ref[...] += jnp.concatenate(dlb, axis=1)

    nst = nch // S
    rev = lambda s: nst - 1 - s
    return pl.pallas_call(body, name=name, grid=(nst,),
                          in_specs=[pl.BlockSpec((8, S * C, zw), lambda s: (0, rev(s), 0)),
                                    pl.BlockSpec((None, S * C, D_MODEL), lambda s: (0, rev(s), 0)),
                                    pl.BlockSpec((S, HG_HEADS, HG_DIM, HG_DIM), lambda s: (rev(s), 0, 0, 0)),
                                    pl.BlockSpec((None, HG_HEADS, HG_DIM, HG_DIM),
                                                 lambda s: (jnp.minimum((rev(s) + 1) * S, nch - 1), 0, 0, 0)),
                                    pl.BlockSpec((2, D_MODEL), lambda s: (0, 0)),
                                    pl.BlockSpec((1, HG_DIM), lambda s: (0, 0))],
                          out_specs=[pl.BlockSpec((8, S * C, zw), lambda s: (0, rev(s), 0)),
                                     pl.BlockSpec((1, D_MODEL), lambda s: (0, 0)),
                                     pl.BlockSpec((1, HG_DIM), lambda s: (0, 0))],
                          out_shape=[jax.ShapeDtypeStruct((8, T, zw), MXU), jax.ShapeDtypeStruct((1, D_MODEL), F32),
                                     jax.ShapeDtypeStruct((1, HG_DIM), F32)],
                          scratch_shapes=[pltpu.VMEM((HG_HEADS, HG_DIM, HG_DIM), F32)]
                          + [pltpu.VMEM((C, D_MODEL), F32)] * 5,
                          compiler_params=_params())(z, dog, states, states, lbl, gw)


def _bucket_onehot():
    W = SW_WINDOW
    t = np.arange(W)[:, None] + W
    s = np.arange(2 * W)[None, :]
    dist = t - s
    exact = REL_BUCKETS // 2
    d = np.maximum(np.maximum(dist, 0), 1).astype(np.float32)
    log_b = exact + (np.log(d / np.float32(exact)) / np.float32(math.log(REL_MAX_DIST / exact))
                     * np.float32(REL_BUCKETS - exact)).astype(np.int32)
    bucket = np.where(np.maximum(dist, 0) < exact, np.maximum(dist, 0), np.minimum(log_b, REL_BUCKETS - 1))
    valid = (dist >= 0) & (dist < W)
    onehot = (bucket[..., None] == np.arange(REL_BUCKETS)) & valid[..., None]
    return onehot.reshape(W * 2 * W, REL_BUCKETS).astype(np.float32)


def _bias_expand(rel_t, onehot_t, *, name):
    hq, nbk = rel_t.shape
    n = onehot_t.shape[1]

    def body(r_ref, oh_ref, o_ref):
        o_ref[...] = _exact_dot_r(r_ref[...], oh_ref[...])

    return pl.pallas_call(body, name=name, out_shape=jax.ShapeDtypeStruct((hq, n), F32),
                          compiler_params=_params())(rel_t, onehot_t)


def _bias_reduce(dbias, onehot, *, name):
    hq = dbias.shape[0]
    nbk = onehot.shape[1]

    def body(d_ref, oh_ref, o_ref):
        o_ref[...] = _exact_dot_r(d_ref[...], oh_ref[...])

    return pl.pallas_call(body, name=name, out_shape=jax.ShapeDtypeStruct((hq, nbk), F32),
                          compiler_params=_params())(dbias, onehot)


def _swa_mask(j):
    W = SW_WINDOW
    t = lax.broadcasted_iota(jnp.int32, (W, 2 * W), 0) + W
    s = lax.broadcasted_iota(jnp.int32, (W, 2 * W), 1)
    dist = t - s
    band = (dist >= 0) & (dist < W)
    m = band & ((j > 0) | (s >= W))
    return jnp.concatenate([m] * SW_GROUP, axis=0)


def _half_mask(rows, half):
    lane = lax.broadcasted_iota(jnp.int32, (rows, LANES), 1)
    return (lane >= SW_HEAD_DIM) if half else (lane < SW_HEAD_DIM)


_ALL = slice(None)


def _swa_head(ref, col0, head, to_half, rows=_ALL):
    slab, half = head // 2, head % 2
    x = ref[rows, col0 + slab * LANES:col0 + (slab + 1) * LANES]
    x = jnp.where(_half_mask(x.shape[0], half), x, 0.0)
    return x if half == to_half else pltpu.roll(x, SW_HEAD_DIM, axis=1)


def _swa_stack(ref, g, rows=_ALL):
    return jnp.concatenate([_swa_head(ref, 0, g * SW_GROUP + r, g % 2, rows) for r in range(SW_GROUP)], axis=0)


def _swa_unstack(ref, x, g, rows=_ALL):
    W = SW_WINDOW
    for pair in range(SW_GROUP // 2):
        parts = []
        for r in (2 * pair, 2 * pair + 1):
            piece = x[r * W:(r + 1) * W]
            parts.append(piece if r % 2 == g % 2 else pltpu.roll(piece, SW_HEAD_DIM, axis=1))
        slab = (g * SW_GROUP) // 2 + pair
        ref[rows, slab * LANES:(slab + 1) * LANES] = (parts[0] + parts[1]).astype(ref.dtype)


def _swa_kv(kp_ref, kc_ref, col0, g, prev_rows=_ALL, rows=_ALL):
    return jnp.concatenate([_swa_head(kp_ref, col0, g, g % 2, prev_rows), _swa_head(kc_ref, col0, g, g % 2, rows)],
                           axis=0)


def _lane_pick(tile, h):
    lane = lax.broadcasted_iota(jnp.int32, tile.shape, 1)
    return jnp.sum(jnp.where(lane == h, tile, 0.0), axis=-1, keepdims=True)


def _lane_put(tile, h, col):
    lane = lax.broadcasted_iota(jnp.int32, tile.shape, 1)
    return jnp.where(lane == h, col, tile)


def _swa_rows(vals):
    return jnp.concatenate([jnp.broadcast_to(v, (SW_WINDOW, 1)) for v in vals], axis=0)


def _swa_fwd(q, kv, bias, sinks, *, name):
    _, T, D = q.shape
    W = SW_WINDOW
    nb = T // W
    dh = SW_HEAD_DIM
    kvw = SW_KV_HEADS * dh
    scale = dh ** -0.5

    S = SW_STEP_BLOCKS if nb % SW_STEP_BLOCKS == 0 else 1

    def body(q_ref, kc_ref, kp_ref, bias_ref, sink_ref, o_ref, lse_ref):
        c = pl.program_id(0)
        sk = sink_ref[...]
        rows = [slice(s * W, (s + 1) * W) for s in range(S)]
        before = [(kp_ref, _ALL)] + [(kc_ref, rows[s - 1]) for s in range(1, S)]
        masks = [_swa_mask(c * S + s) for s in range(S)]
        I = [(s, g) for s in range(S) for g in range(SW_KV_HEADS)]
        kv_of = lambda col0, s, g: _swa_kv(before[s][0], kc_ref, col0, g, before[s][1], rows[s])
        kk = [kv_of(0, s, g) for s, g in I]
        vv = [kv_of(kvw, s, g) for s, g in I]
        qs = [_swa_stack(q_ref, g, rows[s]) for s, g in I]
        bias_of = lambda g: bias_ref[g * SW_GROUP:(g + 1) * SW_GROUP].reshape(SW_GROUP * W, 2 * W)
        logits = [jnp.where(masks[s], _dot_nt(qs[n], kk[n]) * scale + bias_of(g), NEG_BIG) for n, (s, g) in enumerate(I)]
        sink = [_swa_rows([_lane_pick(sk, g * SW_GROUP + r) for r in range(SW_GROUP)]) for s, g in I]
        m = [jnp.maximum(jnp.max(logits[n], axis=-1, keepdims=True), sink[n]) for n in range(len(I))]
        p = [jnp.exp(logits[n] - m[n]) for n in range(len(I))]
        den = [jnp.sum(p[n], axis=-1, keepdims=True) + jnp.exp(sink[n] - m[n]) for n in range(len(I))]
        pv = [_dot(p[n], vv[n]) for n in range(len(I))]
        lse_tiles = [jnp.zeros((W, SW_Q_HEADS), F32) for _ in range(S)]
        for n, (s, g) in enumerate(I):
            _swa_unstack(o_ref, pv[n] / den[n], g, rows[s])
            lse = m[n] + jnp.log(den[n])
            for r in range(SW_GROUP):
                lse_tiles[s] = _lane_put(lse_tiles[s], g * SW_GROUP + r, lse[r * W:(r + 1) * W])
        for s in range(S):
            lse_ref[rows[s], :] = lse_tiles[s]

    return pl.pallas_call(body, name=name, grid=(nb // S,),
                          in_specs=[pl.BlockSpec((None, S * W, D), lambda j: (0, j, 0)),
                                    pl.BlockSpec((None, S * W, 2 * kvw), lambda j: (0, j, 0)),
                                    pl.BlockSpec((None, W, 2 * kvw), lambda j: (0, jnp.maximum(j * S - 1, 0), 0)),
                                    pl.BlockSpec((SW_Q_HEADS, W, 2 * W), lambda j: (0, 0, 0)),
                                    pl.BlockSpec((1, SW_Q_HEADS), lambda j: (0, 0))],
                          out_specs=[pl.BlockSpec((None, S * W, D), lambda j: (0, j, 0)),
                                     pl.BlockSpec((S * W, SW_Q_HEADS), lambda j: (j, 0))],
                          out_shape=[jax.ShapeDtypeStruct((1, T, D), F32), jax.ShapeDtypeStruct((T, SW_Q_HEADS), F32)],
                          compiler_params=_params())(q, kv, kv, bias, sinks)


def _swa_bwd(q, kv, o, lse, do, bias, sinks, *, name):
    _, T, D = q.shape
    W = SW_WINDOW
    nb = T // W
    dh = SW_HEAD_DIM
    kvw = SW_KV_HEADS * dh
    scale = dh ** -0.5
    cl = lambda j: jnp.minimum(j, nb - 1)

    def body(q_ref, kc_ref, kp_ref, o_ref, lse_ref, do_ref, bias_ref, sink_ref,
             dq_ref, dkv_ref, dbias_ref, dsink_ref, carry):
        j = pl.program_id(0)

        @pl.when(j == 0)
        def _():
            carry[...] = jnp.zeros_like(carry)
            dbias_ref[...] = jnp.zeros_like(dbias_ref)
            dsink_ref[...] = jnp.zeros_like(dsink_ref)

        @pl.when(j < nb)
        def _():
            mask = _swa_mask(j)
            sk = sink_ref[...]
            lse_tile = lse_ref[...]
            dsink = jnp.zeros((1, SW_Q_HEADS), F32)
            G = range(SW_KV_HEADS)
            heads = [[g * SW_GROUP + r for r in range(SW_GROUP)] for g in G]
            kk = [_swa_kv(kp_ref, kc_ref, 0, g) for g in G]
            vv = [_swa_kv(kp_ref, kc_ref, kvw, g) for g in G]
            qs = [_swa_stack(q_ref, g) for g in G]
            dos = [_swa_stack(do_ref, g) for g in G]
            lse = [jnp.concatenate([_lane_pick(lse_tile, h) for h in heads[g]], axis=0) for g in G]
            sink = [_swa_rows([_lane_pick(sk, h) for h in heads[g]]) for g in G]
            logits = [jnp.where(mask, _dot_nt(qs[g], kk[g]) * scale
                                + bias_ref[g * SW_GROUP:(g + 1) * SW_GROUP].reshape(SW_GROUP * W, 2 * W), NEG_BIG)
                      for g in G]
            dp = [_dot_nt(dos[g], vv[g]) for g in G]
            p = [jnp.exp(logits[g] - lse[g]) for g in G]
            delta = [jnp.sum(dos[g] * _swa_stack(o_ref, g), axis=-1, keepdims=True) for g in G]
            dl = [p[g] * (dp[g] - delta[g]) for g in G]
            dqs = [_dot(dl[g], kk[g]) * scale for g in G]
            dks = [_dot_tn(dl[g], qs[g]) * scale for g in G]
            dvs = [_dot_tn(p[g], dos[g]) for g in G]
            for g in G:
                _swa_unstack(dq_ref, dqs[g], g)
                dbias_ref[g * SW_GROUP:(g + 1) * SW_GROUP] += dl[g].reshape(SW_GROUP, W, 2 * W)
                sd = jnp.exp(sink[g] - lse[g]) * delta[g]
                for r, h in enumerate(heads[g]):
                    dsink = _lane_put(dsink, h, -jnp.sum(sd[r * W:(r + 1) * W], axis=0, keepdims=True))
            dsink_ref[...] += dsink
            for slab in range(SW_KV_HEADS // 2):
                for col0, parts in ((0, dks), (kvw, dvs)):
                    both = parts[2 * slab] + parts[2 * slab + 1]
                    cols = slice(col0 + slab * LANES, col0 + (slab + 1) * LANES)
                    dkv_ref[:, cols] = (carry[:, cols] + both[:W]).astype(dkv_ref.dtype)
                    carry[:, cols] = both[W:]

        @pl.when(j == nb)
        def _():
            dkv_ref[...] = carry[...].astype(dkv_ref.dtype)

    tok = lambda w: pl.BlockSpec((None, W, w), lambda j: (0, cl(j), 0))
    return pl.pallas_call(body, name=name, grid=(nb + 1,),
                          in_specs=[tok(D), tok(2 * kvw),
                                    pl.BlockSpec((None, W, 2 * kvw), lambda j: (0, jnp.maximum(cl(j) - 1, 0), 0)),
                                    tok(D), pl.BlockSpec((W, SW_Q_HEADS), lambda j: (cl(j), 0)), tok(D),
                                    pl.BlockSpec((SW_Q_HEADS, W, 2 * W), lambda j: (0, 0, 0)),
                                    pl.BlockSpec((1, SW_Q_HEADS), lambda j: (0, 0))],
                          out_specs=[tok(D),
                                     pl.BlockSpec((None, W, 2 * kvw), lambda j: (0, jnp.maximum(j - 1, 0), 0)),
                                     pl.BlockSpec((SW_Q_HEADS, W, 2 * W), lambda j: (0, 0, 0)),
                                     pl.BlockSpec((1, SW_Q_HEADS), lambda j: (0, 0))],
                          out_shape=[jax.ShapeDtypeStruct((1, T, D), MXU), jax.ShapeDtypeStruct((1, T, 2 * kvw), MXU),
                                     jax.ShapeDtypeStruct((SW_Q_HEADS, W, 2 * W), F32),
                                     jax.ShapeDtypeStruct((1, SW_Q_HEADS), F32)],
                          scratch_shapes=[pltpu.VMEM((W, 2 * kvw), F32)],
                          compiler_params=_params())(q, kv, kv, o, lse, do, bias, sinks)


_HBM = pl.BlockSpec(memory_space=pltpu.HBM)
_SEM = pl.BlockSpec(memory_space=pltpu.SEMAPHORE)
_EFFECT = pltpu.SideEffectType.DATAFLOW_SIDE_EFFECTING
N_PEERS = N_DEV - 1


def _peer(k):
    x, y, c = lax.axis_index("x"), lax.axis_index("y"), lax.axis_index("c")
    px = (x + (k >> 2)) % 2
    py = (y + ((k >> 1) & 1)) % 2
    pc = (c + (k & 1)) % 2
    return (px, py, pc), 4 * px + 2 * py + pc


def _my_number():
    return 4 * lax.axis_index("x") + 2 * lax.axis_index("y") + lax.axis_index("c")


def _landing(src, mode):
    me = _my_number()
    own = src if mode == "gather" else lax.dynamic_index_in_dim(src, me, 0, keepdims=False)
    return lax.dynamic_update_index_in_dim(lax.empty((N_DEV,) + own.shape, own.dtype), own, me, 0)


def _copy(src_ref, land_ref, mode, send, recv, j, k, dst_slot):
    peer, pid = _peer(k)
    return pltpu.make_async_remote_copy(
        src_ref=src_ref if mode == "gather" else src_ref.at[pid], dst_ref=land_ref.at[dst_slot(pid)],
        send_sem=send.at[j * N_PEERS + k - 1], recv_sem=recv.at[j * N_PEERS + k - 1],
        device_id=peer, device_id_type=pl.DeviceIdType.MESH)


def _send_start(groups, *, name):
    flat = [t for g in groups for t in g]
    n, ng = len(flat), len(groups)
    srcs = [pltpu.with_memory_space_constraint(s, pltpu.HBM) for s, _ in flat]
    lands = [pltpu.with_memory_space_constraint(_landing(s, m), pltpu.HBM) for s, m in flat]

    def body(*refs):
        src_refs, land_refs = refs[:n], refs[n:2 * n]
        sems = refs[2 * n:2 * n + 2 * ng]
        token = refs[-1]
        me = _my_number()
        a = 0
        for gi, g in enumerate(groups):
            for j, (_, mode) in enumerate(g):
                for k in range(1, N_DEV):
                    _copy(src_refs[a], land_refs[a], mode, sems[2 * gi], sems[2 * gi + 1], j, k, lambda pid: me).start()
                a += 1
        token[...] = jnp.zeros_like(token)

    sem_shapes = []
    for g in groups:
        sem_shapes += [pltpu.SemaphoreType.DMA((len(g) * N_PEERS,))] * 2
    out = pl.pallas_call(
        body, name=name,
        out_shape=tuple(sem_shapes) + tuple(pltpu.HBM(a.shape, a.dtype) for a in srcs + lands)
        + (jax.ShapeDtypeStruct((SUBLANES, LANES), F32),),
        in_specs=[_HBM] * (2 * n), out_specs=[_SEM] * (2 * ng) + [_HBM] * (2 * n) + [pl.BlockSpec(memory_space=pltpu.VMEM)],
        input_output_aliases={i: 2 * ng + i for i in range(2 * n)},
        compiler_params=pltpu.CompilerParams(has_side_effects=_EFFECT))(*srcs, *lands)
    sems, thru, token = out[:2 * ng], out[2 * ng:2 * ng + 2 * n], out[-1]
    handles, a = [], 0
    for gi, g in enumerate(groups):
        m = len(g)
        handles.append((sems[2 * gi], sems[2 * gi + 1], list(thru[a:a + m]), list(thru[n + a:n + a + m]),
                        [mode for _, mode in g]))
        a += m
    return handles, token


def _send_wait(handle, after, *, name):
    send, recv, srcs, lands, modes = handle
    m = len(srcs)

    def body(*refs):
        src_refs, land_refs = refs[:m], refs[m:2 * m]
        send_ref, recv_ref = refs[2 * m], refs[2 * m + 1]
        for j in range(m):
            for k in range(1, N_DEV):
                cp = _copy(src_refs[j], land_refs[j], modes[j], send_ref, recv_ref, j, k, lambda pid: pid)
                cp.wait_send()
                cp.wait_recv()

    out = pl.pallas_call(
        body, name=name, out_shape=tuple(pltpu.HBM(a.shape, a.dtype) for a in srcs + lands),
        in_specs=[_HBM] * (2 * m) + [_SEM, _SEM] + [pl.BlockSpec(memory_space=pl.ANY)] * len(after),
        out_specs=[_HBM] * (2 * m), input_output_aliases={i: i for i in range(2 * m)},
        compiler_params=pltpu.CompilerParams(has_side_effects=_EFFECT))(*srcs, *lands, send, recv, *after)
    return list(out[m:])


def _adam_math(w, g, m, v):
    m = ADAM_B1 * m + (1.0 - ADAM_B1) * g
    v = ADAM_B2 * v + (1.0 - ADAM_B2) * (g * g)
    m_hat = m / (1.0 - ADAM_B1 ** ADAM_STEP)
    v_hat = v / (1.0 - ADAM_B2 ** ADAM_STEP)
    delta = -ADAM_LR * (m_hat / (jnp.sqrt(v_hat) + ADAM_EPS) + ADAM_WD * w)
    return delta, m, v


def _adamw(parts, w, m, v, *, name, layer=None):
    S, R, C = parts.shape
    tr = R
    for cand in (256, 128, 64, 32, 16, 8):
        if R % cand == 0 and S * cand * C * 4 <= 4 * 2 ** 20:
            tr = cand
            break

    def body(p_ref, w_ref, m_ref, v_ref, g_ref, d_ref, nm_ref, nv_ref):
        g = p_ref[0].astype(F32)
        for s in range(1, S):
            g = g + p_ref[s].astype(F32)
        delta, nm, nv = _adam_math(w_ref[...], g, m_ref[...], v_ref[...])
        g_ref[...] = g
        d_ref[...] = delta
        nm_ref[...] = nm
        nv_ref[...] = nv

    if layer is None:
        wspec = pl.BlockSpec((tr, C), lambda i: (i, 0))
    else:
        wspec = pl.BlockSpec((None, tr, C), lambda i: (layer, i, 0))
    ospec = pl.BlockSpec((tr, C), lambda i: (i, 0))
    osh = jax.ShapeDtypeStruct((R, C), F32)
    return pl.pallas_call(body, name=name, grid=(R // tr,),
                          in_specs=[pl.BlockSpec((S, tr, C), lambda i: (0, i, 0)), wspec, wspec, wspec],
                          out_specs=[ospec] * 4, out_shape=[osh] * 4, compiler_params=_params())(parts, w, m, v)


def _sum_parts(parts, *, name):
    S, R, C = parts.shape

    def body(p_ref, o_ref):
        g = p_ref[0]
        for s in range(1, S):
            g = g + p_ref[s]
        o_ref[...] = g

    return pl.pallas_call(body, name=name, out_shape=jax.ShapeDtypeStruct((R, C), F32),
                          compiler_params=_params())(parts)


def _pack_rows(arrays):
    pieces, layout, row = [], [], 0
    for a in arrays:
        flat = a.reshape(-1).astype(F32)
        rows = -(-flat.shape[0] // (SUBLANES * LANES)) * SUBLANES
        flat = jnp.pad(flat, (0, rows * LANES - flat.shape[0]))
        pieces.append(flat.reshape(rows, LANES))
        layout.append((row, rows, a.shape))
        row += rows
    return jnp.concatenate(pieces, axis=0), layout


def _unpack_rows(packed, layout):
    out = []
    for row, rows, shape in layout:
        size = int(np.prod(shape))
        out.append(packed[row:row + rows].reshape(-1)[:size].reshape(shape))
    return out


def _ffn_fwd(h, w_in, w_out, cw, ln_g, ln_b, tag):
    h, hb = h
    u, ab, act = _ffn_up(hb, w_in, cw, name=f"ffn_up_{tag}")
    hnb, xh, rs = _mm_nn(act, w_out, res=h, res_scale=ALPHA, ln=(ln_g, ln_b), name=f"ffn_down_{tag}")
    return hnb, xh, rs, (u, ab)


def _ffn_bwd(dy, hb, u, w_in, w_out, cw, ln_bwd, send, tag):
    du, dw_out, dcw = _ffn_gate_bwd(dy, u[0], u[1], w_out, cw, name=f"ffn_gate_bwd_{tag}")
    du = du.reshape((-1,) + du.shape[2:])
    dw_in = _mm_tn(hb, du, n_map=_pair_map, tm=REDUCE_TOKEN_TILE_16BIT, name=f"ffn_dwin_{tag}")
    handle, token = send(dw_in, dw_out)
    dyp, dg, db = _mm_nt_resident(du, w_in, n_map=_pair_map, res=dy, res_scale=ALPHA, ln_bwd=ln_bwd,
                                  behind=(token,), name=f"ffn_dh_{tag}")
    dcw = dcw.transpose(1, 0, 2, 3).reshape((-1,) + dcw.shape[2:])
    return dyp, dg, db, handle, dcw


def kernel(x, hgrn_w_in, hgrn_lb_logits, hgrn_gnorm_w, hgrn_w_out, swa_w_q, swa_sinks, swa_w_out, shared_w_kv, rel_bias, ffn_w_in, ffn_conv_w, ffn_conv_b, ffn_w_out, ln_mix_g, ln_mix_b, ln_ffn_g, ln_ffn_b, loss_target, m_hgrn_w_in, m_hgrn_lb_logits, m_hgrn_gnorm_w, m_hgrn_w_out, m_swa_w_q, m_swa_sinks, m_swa_w_out, m_shared_w_kv, m_rel_bias, m_ffn_w_in, m_ffn_conv_w, m_ffn_conv_b, m_ffn_w_out, m_ln_mix_g, m_ln_mix_b, m_ln_ffn_g, m_ln_ffn_b, v_hgrn_w_in, v_hgrn_lb_logits, v_hgrn_gnorm_w, v_hgrn_w_out, v_swa_w_q, v_swa_sinks, v_swa_w_out, v_shared_w_kv, v_rel_bias, v_ffn_w_in, v_ffn_conv_w, v_ffn_conv_b, v_ffn_w_out, v_ln_mix_g, v_ln_mix_b, v_ln_ffn_g, v_ln_ffn_b):
    T = x.shape[1]
    D = D_MODEL
    W = SW_WINDOW
    fb = ffn_w_in.shape[2]
    me = 4 * lax.axis_index("x") + 2 * lax.axis_index("y") + lax.axis_index("c")

    small_fwd, small_fwd_layout = _pack_rows([hgrn_lb_logits, ffn_conv_w])
    gat = lambda *ws: [(w_.astype(MXU), "gather") for w_ in ws]
    (wait_a, wait_b, wait_c), _ = _send_start(
        [gat(hgrn_w_in[0]) + [(small_fwd, "gather")],
         gat(hgrn_w_out[0], ffn_w_in[0], ffn_w_out[0]),
         gat(shared_w_kv, swa_w_q[0], swa_w_out[0], ffn_w_in[1], ffn_w_out[1])], name="gather_start")
    xb = x.astype(MXU)
    w_hin, small_all = _send_wait(wait_a, (xb,), name="gather_wait_a")
    w_hin = w_hin[None]
    ffn_rows = 2 * ffn_w_out.shape[1]
    (lb_row, lb_rows, _), (cw_row, cw_rows, _) = small_fwd_layout
    lbl = small_all[:, lb_row:lb_row + 2, :].transpose(1, 0, 2).reshape(2, D)
    conv_w_all = small_all[:, cw_row:cw_row + cw_rows, :].reshape(N_DEV, -1)[:, :DEPTH * 3 * fb]
    conv_w_all = conv_w_all.reshape(N_DEV, DEPTH, 3, fb).transpose(1, 0, 2, 3)
    conv_b_all = ffn_conv_b.reshape(DEPTH, N_DEV, 1, fb)
    no_pad = ((0, 0), (0, 0))
    cw = (jnp.pad(conv_w_all, no_pad + ((0, SUBLANES - 3), (0, 0)))
          + jnp.pad(conv_b_all, no_pad + ((3, SUBLANES - 4), (0, 0))))

    row = lambda a, l: a[l:l + 1]

    z = _mm_nn(xb, w_hin, name="hgrn_in")
    og, states = _hgrn_fwd(z, lbl, hgrn_gnorm_w, name="hgrn_rec")
    w_hout, w_fin0, w_fout0 = _send_wait(wait_b, (og,), name="gather_wait_b")
    w_hout = w_hout.reshape(1, 1, D, D)
    w_fin = [w_fin0[None], None]
    w_fout = [w_fout0.reshape(4, 1, ffn_rows, D), None]
    h1b, xh1, rs1 = _mm_nn(og, w_hout, res=x, res_scale=ALPHA, ln=(row(ln_mix_g, 0), row(ln_mix_b, 0)), name="hgrn_out")
    h1 = (xh1, row(ln_mix_g, 0), row(ln_mix_b, 0))
    h2b, xh2, rs2, u0 = _ffn_fwd((h1, h1b), w_fin[0], w_fout[0], cw[0], row(ln_ffn_g, 0), row(ln_ffn_b, 0), "l0")
    h2 = (xh2, row(ln_ffn_g, 0), row(ln_ffn_b, 0))
    w_kv, w_q, w_o, w_fin1, w_fout1 = _send_wait(wait_c, (h2b,), name="gather_wait_c")
    w_kv = w_kv.reshape(1, 1, D, 2 * SW_KV_HEADS * SW_HEAD_DIM)
    w_q = w_q.reshape(1, 1, D, D)
    w_o = w_o.reshape(1, 1, D, D)
    w_fin[1] = w_fin1[None]
    w_fout[1] = w_fout1.reshape(4, 1, ffn_rows, D)
    kv = _mm_nn(h2b, w_kv, name="swa_kv")
    q = _mm_nn(h2b, w_q, name="swa_q")
    onehot = _bucket_onehot()
    bias = _bias_expand(rel_bias.T, jnp.asarray(onehot.T, jnp.bfloat16), name="swa_bias").reshape(SW_Q_HEADS, W, 2 * W)
    ao, lse = _swa_fwd(q, kv, bias, swa_sinks, name="swa_attn")
    h3b, xh3, rs3 = _mm_nn(ao, w_o, res=h2, res_scale=ALPHA, ln=(row(ln_mix_g, 1), row(ln_mix_b, 1)), name="swa_out")
    h3 = (xh3, row(ln_mix_g, 1), row(ln_mix_b, 1))
    _, xh4, rs4, u1 = _ffn_fwd((h3, h3b), w_fin[1], w_fout[1], cw[1], row(ln_ffn_g, 1), row(ln_ffn_b, 1), "l1")
    dy4, dg_f1, db_f1, loss_tile = _loss_ln_bwd(loss_target, xh4, rs4, row(ln_ffn_g, 1), row(ln_ffn_b, 1),
                                                name="loss_ln_ffn1_bwd")
    sc = lambda *gs: [(g_, "scatter") for g_ in gs]

    def send_ffn(name_):
        def send(dw_in, dw_out):
            (handle,), token = _send_start([sc(dw_in.reshape(N_DEV, D, fb), dw_out.reshape(N_DEV, -1, D))], name=name_)
            return handle, token
        return send

    dy3, dg_m1, db_m1, ex1, dcw1 = _ffn_bwd(dy4, h3b, u1, w_fin[1], w_fout[1], cw[1],
                                            (xh3, rs3, row(ln_mix_g, 1)), send_ffn("grads_start_1"), "l1")
    dw_o = _mm_tn(ao, dy3, name="swa_dwo")
    dao = _mm_nt(dy3, w_o, name="swa_dao")
    dq, dkv, dbias, dsinks = _swa_bwd(q, kv, ao, lse, dao, bias, swa_sinks, name="swa_attn_bwd")
    drel_t = _bias_reduce(dbias.reshape(SW_Q_HEADS, W * 2 * W), jnp.asarray(onehot, jnp.bfloat16), name="swa_dbias")
    dw_q = _mm_tn(h2b, dq, name="swa_dwq")
    dw_kv = _mm_tn(h2b, dkv, name="swa_dwkv")
    dh2 = _mm_nt(dq, w_q, res=dy3, res_scale=ALPHA, name="swa_dh_q")
    (ex2,), tok2 = _send_start([sc(dw_o.reshape(N_DEV, D // N_DEV, D), dw_q.reshape(N_DEV, D // N_DEV, D),
                                   dw_kv.reshape(N_DEV, D // N_DEV, -1))], name="grads_start_2")
    dy2, dg_f0, db_f0 = _mm_nt_resident(dkv, w_kv, res=dh2, ln_bwd=(xh2, rs2, row(ln_ffn_g, 0)), behind=(tok2,),
                                        name="swa_dh_kv")
    dy1, dg_m0, db_m0, ex3, dcw0 = _ffn_bwd(dy2, h1b, u0, w_fin[0], w_fout[0], cw[0],
                                            (xh1, rs1, row(ln_mix_g, 0)), send_ffn("grads_start_3"), "l0")
    dw_hout = _mm_tn(og, dy1, name="hgrn_dwout")
    dog = _mm_nt(dy1, w_hout, name="hgrn_dog")
    dz, dlb, dgw = _hgrn_bwd(z, dog, states, lbl, hgrn_gnorm_w, name="hgrn_rec_bwd")
    dw_hin = _mm_tn(xb, dz, tm=REDUCE_TOKEN_TILE_16BIT, name="hgrn_dwin")

    p0 = _sigmoid(lbl[0:1] - lbl[1:2])
    dl0 = dlb * p0 * (1.0 - p0)
    d_lbl = dl0 * jnp.array([[1.0], [-1.0]], F32)
    dcw = jnp.stack([dcw0, dcw1], axis=0)
    d_conv_w = dcw[:, :, 0:3, :]
    d_conv_b = dcw[:, :, 3, :].reshape(DEPTH, N_DEV * fb)
    first_row = lax.broadcasted_iota(jnp.int32, (DEPTH, D), 0) == 0
    two_rows = lambda a, b: jnp.where(first_row, a, b)
    d_ln_mix_g = two_rows(dg_m0, dg_m1)
    d_ln_mix_b = two_rows(db_m0, db_m1)
    d_ln_ffn_g = two_rows(dg_f0, dg_f1)
    d_ln_ffn_b = two_rows(db_f0, db_f1)
    small_grads, small_layout = _pack_rows([d_lbl, d_conv_w, dgw, dsinks, drel_t.T, d_conv_b, d_ln_mix_g, d_ln_mix_b,
                                            d_ln_ffn_g, d_ln_ffn_b, loss_tile[0:1, 0:1]])

    (ex4,), tok4 = _send_start([sc(dw_hin.reshape(N_DEV, D, -1), dw_hout.reshape(N_DEV, D // N_DEV, D))
                                + [(small_grads, "gather")]], name="grads_start_4")
    dx = _mm_nt_resident(dz, w_hin, res=dy1, res_scale=ALPHA, name="hgrn_dx", behind=(tok4,))
    r_fin1, r_fout1 = _send_wait(ex1, (dx,), name="grads_wait_1")
    r_o, r_q, r_kv = _send_wait(ex2, (dx,), name="grads_wait_2")
    r_fin0, r_fout0 = _send_wait(ex3, (dx,), name="grads_wait_3")
    r_hin, r_hout, r_small = _send_wait(ex4, (dx,), name="grads_wait_4")
    received = [r_hin, r_hout, r_q, r_o, r_kv, r_fin0, r_fin1, r_fout0, r_fout1, r_small]

    outs = {}

    def put(name_, res):
        outs["grad_" + name_], outs["delta_" + name_], outs["new_m_" + name_], outs["new_v_" + name_] = res

    def big_update(name_, parts, w, m, v):
        shp = w.shape
        if w.ndim == 3 and shp[0] == 1:
            r = _adamw(parts, w[0], m[0], v[0], name="adamw_" + name_)
            put(name_, [a.reshape(shp) for a in r])
        else:
            r = _adamw(parts, w, m, v, name="adamw_" + name_)
            put(name_, r)

    big_update("hgrn_w_in", received[0], hgrn_w_in, m_hgrn_w_in, v_hgrn_w_in)
    big_update("hgrn_w_out", received[1], hgrn_w_out, m_hgrn_w_out, v_hgrn_w_out)
    big_update("swa_w_q", received[2], swa_w_q, m_swa_w_q, v_swa_w_q)
    big_update("swa_w_out", received[3], swa_w_out, m_swa_w_out, v_swa_w_out)
    big_update("shared_w_kv", received[4], shared_w_kv, m_shared_w_kv, v_shared_w_kv)
    for name_, idx, w, m, v in (("ffn_w_in", 5, ffn_w_in, m_ffn_w_in, v_ffn_w_in),
                                ("ffn_w_out", 7, ffn_w_out, m_ffn_w_out, v_ffn_w_out)):
        per_layer = [_adamw(received[idx + l], w, m, v, layer=l, name=f"adamw_{name_}_{l}") for l in range(DEPTH)]
        put(name_, [jnp.stack([per_layer[0][i], per_layer[1][i]], axis=0) for i in range(4)])

    small_sum = _sum_parts(received[9], name="sum_small_grads")
    (g_lbl, g_conv_w, g_gw, g_sinks, g_rel, g_conv_b, g_mix_g, g_mix_b, g_ffn_g, g_ffn_b,
     loss) = _unpack_rows(small_sum, small_layout)
    g_lbl_mine = lax.dynamic_slice_in_dim(g_lbl, me * (D // N_DEV), D // N_DEV, axis=1)
    g_conv_w_mine = lax.dynamic_index_in_dim(g_conv_w, me, axis=1, keepdims=False)
    small_names = ["hgrn_lb_logits", "ffn_conv_w", "hgrn_gnorm_w", "swa_sinks", "rel_bias", "ffn_conv_b",
                   "ln_mix_g", "ln_mix_b", "ln_ffn_g", "ln_ffn_b"]
    small_g = [g_lbl_mine, g_conv_w_mine, g_gw, g_sinks, g_rel, g_conv_b, g_mix_g, g_mix_b, g_ffn_g, g_ffn_b]
    small_w = [hgrn_lb_logits, ffn_conv_w, hgrn_gnorm_w, swa_sinks, rel_bias, ffn_conv_b, ln_mix_g, ln_mix_b,
               ln_ffn_g, ln_ffn_b]
    small_m = [m_hgrn_lb_logits, m_ffn_conv_w, m_hgrn_gnorm_w, m_swa_sinks, m_rel_bias, m_ffn_conv_b, m_ln_mix_g,
               m_ln_mix_b, m_ln_ffn_g, m_ln_ffn_b]
    small_v = [v_hgrn_lb_logits, v_ffn_conv_w, v_hgrn_gnorm_w, v_swa_sinks, v_rel_bias, v_ffn_conv_b, v_ln_mix_g,
               v_ln_mix_b, v_ln_ffn_g, v_ln_ffn_b]
    pg, lay = _pack_rows(small_g)
    pw, _ = _pack_rows(small_w)
    pm, _ = _pack_rows(small_m)
    pv, _ = _pack_rows(small_v)
    res = _adamw(pg[None], pw, pm, pv, name="adamw_small")
    unpacked = [_unpack_rows(r, lay) for r in res]
    for i, name_ in enumerate(small_names):
        put(name_, [unpacked[j][i] for j in range(4)])

    order = ["hgrn_w_in", "hgrn_lb_logits", "hgrn_gnorm_w", "hgrn_w_out", "swa_w_q", "swa_sinks", "swa_w_out",
             "shared_w_kv", "rel_bias", "ffn_w_in", "ffn_conv_w", "ffn_conv_b", "ffn_w_out", "ln_mix_g", "ln_mix_b",
             "ln_ffn_g", "ln_ffn_b"]
    result = [loss.reshape(()), dx]
    for kind in ("grad_", "delta_", "new_m_", "new_v_"):
        result += [outs[kind + n] for n in order]
    return tuple(result)
```
